```python
import jax, jax.numpy as jnp
from jax import lax
import numpy as np

D_MODEL = 2048
BATCH = 16
SEQ = 2048
DEPTH = 2

CHUNK = 128
GM_WIDTH = D_MODEL // 4
GM_HEAD_DIM = 128
GM_HEADS = GM_WIDTH // GM_HEAD_DIM
ATT_WIDTH = D_MODEL // 4
ATT_HEAD_DIM = 64
ATT_HEADS = ATT_WIDTH // ATT_HEAD_DIM
ATT_KV_HEADS = ATT_HEADS // 4
KV_WIDTH = ATT_KV_HEADS * ATT_HEAD_DIM
WINDOW = 128
SSM_WIDTH = D_MODEL // 2
SSM_HEAD_DIM = 64
SSM_HEADS = SSM_WIDTH // SSM_HEAD_DIM
SSM_GROUPS = 2
SSM_STATE = 128
CONV_WIDTH = 4
BC_WIDTH = SSM_GROUPS * SSM_STATE
CONV_CH = SSM_WIDTH + 2 * BC_WIDTH
MIX_WIDTH = GM_WIDTH + ATT_WIDTH + SSM_WIDTH
IN_SIZES = (GM_WIDTH, GM_WIDTH, ATT_WIDTH, KV_WIDTH, KV_WIDTH, SSM_WIDTH, CONV_CH, SSM_HEADS)
IN_WIDTH = 2 * GM_WIDTH + ATT_WIDTH + 2 * KV_WIDTH + SSM_WIDTH + CONV_CH + SSM_HEADS
D_FF = 4 * D_MODEL
NEG_INF = -1e30
EPS = 1e-6

kernel_name = 'hybrid_gmlp_swa_ssd_parallel_heads'


def rms_norm(x, g):
    xf = x.astype(jnp.float32)
    y = xf * lax.rsqrt(jnp.mean(xf * xf, axis=-1, keepdims=True) + EPS)
    return (y * g.astype(jnp.float32)).astype(x.dtype)


def layer_norm(x, g, b):
    xf = x.astype(jnp.float32)
    mu = jnp.mean(xf, axis=-1, keepdims=True)
    xc = xf - mu
    y = xc * lax.rsqrt(jnp.mean(xc * xc, axis=-1, keepdims=True) + 1e-5)
    return (y * g.astype(jnp.float32) + b.astype(jnp.float32)).astype(x.dtype)


def split_projection(p):
    offs = np.cumsum(np.array(IN_SIZES))[:-1].tolist()
    return jnp.split(p, offs, axis=-1)


def spatial_gating_mixer(u, v, ln_g, ln_b, w_s, b_s, out_g):
    bsz, s, _ = u.shape
    nc = s // CHUNK
    u = jax.nn.gelu(u).reshape(bsz, nc, CHUNK, GM_HEADS, GM_HEAD_DIM)
    v = jax.nn.gelu(v).reshape(bsz, s, GM_HEADS, GM_HEAD_DIM)
    v = layer_norm(v, ln_g, ln_b).reshape(bsz, nc, CHUNK, GM_HEADS, GM_HEAD_DIM)
    w = jnp.tril(w_s).astype(v.dtype)
    gate = jnp.einsum('hts,bcshe->bcthe', w, v) + b_s.T.astype(v.dtype)[None, None, :, :, None]
    y = (u * gate).reshape(bsz, s, GM_WIDTH)
    return rms_norm(y, out_g)


def sliding_window_sink_attention(q, k, v, sinks, out_g):
    bsz, s, _ = q.shape
    nb = s // WINDOW
    grp = ATT_HEADS // ATT_KV_HEADS
    qb = q.reshape(bsz, nb, WINDOW, ATT_KV_HEADS, grp, ATT_HEAD_DIM)
    pad = ((0, 0), (WINDOW, 0), (0, 0))
    kp = jnp.pad(k, pad).reshape(bsz, nb + 1, WINDOW, ATT_KV_HEADS, ATT_HEAD_DIM)
    vp = jnp.pad(v, pad).reshape(bsz, nb + 1, WINDOW, ATT_KV_HEADS, ATT_HEAD_DIM)
    kb = jnp.concatenate([kp[:, :-1], kp[:, 1:]], axis=2)
    vb = jnp.concatenate([vp[:, :-1], vp[:, 1:]], axis=2)
    scores = jnp.einsum('bnqkgd,bnjkd->bnkgqj', qb, kb).astype(jnp.float32) * (ATT_HEAD_DIM ** -0.5)
    qi = jnp.arange(WINDOW)[:, None]
    kj = jnp.arange(2 * WINDOW)[None, :]
    diff = qi + WINDOW - kj
    band = (diff >= 0) & (diff < WINDOW)
    blk = jnp.arange(nb)[:, None, None]
    valid = band[None] & ((blk * WINDOW + kj[None] - WINDOW) >= 0)
    scores = jnp.where(valid[None, :, None, None], scores, NEG_INF)
    sink = sinks.astype(jnp.float32).reshape(ATT_KV_HEADS, grp)[None, None, :, :, None, None]
    m = jnp.maximum(jnp.max(scores, axis=-1, keepdims=True), sink)
    e = jnp.exp(scores - m)
    p = e / (jnp.sum(e, axis=-1, keepdims=True) + jnp.exp(sink - m))
    o = jnp.einsum('bnkgqj,bnjkd->bnqkgd', p.astype(v.dtype), vb)
    return rms_norm(o.reshape(bsz, s, ATT_WIDTH), out_g)


def causal_depthwise_conv(x, w, b):
    kern = w[:, None, :].astype(x.dtype)
    out = lax.conv_general_dilated(x, kern, window_strides=(1,), padding=((CONV_WIDTH - 1, 0),),
                                   dimension_numbers=('NWC', 'WIO', 'NWC'),
                                   feature_group_count=x.shape[-1])
    return out + b.astype(x.dtype)


def ssd_chunked_scan(xs, dt, a_log, bm, cm, d_skip):
    bsz, s, nh, hp = xs.shape
    nc = s // CHUNK
    hg = nh // SSM_GROUPS
    a = -jnp.exp(a_log.astype(jnp.float32))
    xf = xs.astype(jnp.float32)
    xdt = (xf * dt[..., None]).reshape(bsz, nc, CHUNK, SSM_GROUPS, hg, hp)
    da = (dt * a).reshape(bsz, nc, CHUNK, SSM_GROUPS, hg).transpose(0, 1, 3, 4, 2)
    bc = bm.astype(jnp.float32).reshape(bsz, nc, CHUNK, SSM_GROUPS, SSM_STATE)
    cc = cm.astype(jnp.float32).reshape(bsz, nc, CHUNK, SSM_GROUPS, SSM_STATE)
    a_cs = jnp.cumsum(da, axis=-1)
    idx = jnp.arange(CHUNK)
    causal = idx[:, None] >= idx[None, :]
    decay = jnp.exp(jnp.where(causal, a_cs[..., :, None] - a_cs[..., None, :], -jnp.inf))
    cb = jnp.einsum('bclgn,bcsgn->bcgls', cc, bc)
    y_diag = jnp.einsum('bcghls,bcsghp->bclghp', cb[:, :, :, None] * decay, xdt)
    decay_states = jnp.exp(a_cs[..., -1:] - a_cs)
    states = jnp.einsum('bclgn,bcghl,bclghp->bcghpn', bc, decay_states, xdt)
    chunk_decay = jnp.exp(a_cs[..., -1])

    def step(carry, inp):
        st, dec = inp
        return carry * dec[..., None, None] + st, carry

    init = jnp.zeros((bsz, SSM_GROUPS, hg, hp, SSM_STATE), jnp.float32)
    _, prev = lax.scan(step, init, (jnp.moveaxis(states, 1, 0), jnp.moveaxis(chunk_decay, 1, 0)))
    prev = jnp.moveaxis(prev, 0, 1)
    y_off = jnp.einsum('bclgn,bcghpn,bcghl->bclghp', cc, prev, jnp.exp(a_cs))
    y = (y_diag + y_off).reshape(bsz, s, nh, hp)
    return y + xf * d_skip.astype(jnp.float32)[:, None]


def ssd_mixer(z, xbc, dt_raw, conv_w, conv_b, dt_bias, a_log, d_skip, norm_g):
    bsz, s, _ = z.shape
    xbc = jax.nn.silu(causal_depthwise_conv(xbc, conv_w, conv_b))
    xs, bm, cm = jnp.split(xbc, [SSM_WIDTH, SSM_WIDTH + BC_WIDTH], axis=-1)
    xs = xs.reshape(bsz, s, SSM_HEADS, SSM_HEAD_DIM)
    bm = bm.reshape(bsz, s, SSM_GROUPS, SSM_STATE)
    cm = cm.reshape(bsz, s, SSM_GROUPS, SSM_STATE)
    dt = jax.nn.softplus(dt_raw.astype(jnp.float32) + dt_bias.astype(jnp.float32))
    y = ssd_chunked_scan(xs, dt, a_log, bm, cm, d_skip)
    y = y.reshape(bsz, s, SSM_WIDTH) * jax.nn.silu(z.astype(jnp.float32))
    yg = y.reshape(bsz, s, SSM_GROUPS, SSM_WIDTH // SSM_GROUPS)
    yg = yg * lax.rsqrt(jnp.mean(yg * yg, axis=-1, keepdims=True) + EPS)
    return (yg.reshape(bsz, s, SSM_WIDTH) * norm_g.astype(jnp.float32)).astype(z.dtype)


def _fwd_setup_inputs(seed: int = 0) -> dict:
    key = jax.random.key(seed)
    ks = jax.random.split(key, 26)
    L, D = DEPTH, D_MODEL

    def nrm(k, shape, scale):
        return jax.random.normal(k, shape, jnp.float32) * scale

    dt0 = jnp.exp(jax.random.uniform(ks[15], (L, SSM_HEADS), jnp.float32, np.log(1e-3), np.log(1e-1)))
    return {
        'x': nrm(ks[0], (BATCH, SEQ, D), 1.0),
        'c': nrm(ks[1], (BATCH, D), 1.0),
        'ada_w': nrm(ks[2], (L, D, 6 * D), 0.5 * D ** -0.5),
        'ada_b': nrm(ks[3], (L, 6 * D), 0.02),
        'norm1_g': 1.0 + nrm(ks[4], (L, D), 0.05),
        'w_in': nrm(ks[5], (L, D, IN_WIDTH), D ** -0.5),
        'gm_ln_g': 1.0 + nrm(ks[6], (L, GM_HEADS, GM_HEAD_DIM), 0.05),
        'gm_ln_b': nrm(ks[7], (L, GM_HEADS, GM_HEAD_DIM), 0.02),
        'gm_ws': nrm(ks[8], (L, GM_HEADS, CHUNK, CHUNK), CHUNK ** -0.5),
        'gm_bs': 1.0 + nrm(ks[9], (L, GM_HEADS, CHUNK), 0.05),
        'gm_norm_g': 1.0 + nrm(ks[10], (L, GM_WIDTH), 0.05),
        'attn_sinks': nrm(ks[11], (L, ATT_HEADS), 0.5),
        'attn_norm_g': 1.0 + nrm(ks[12], (L, ATT_WIDTH), 0.05),
        'conv_w': nrm(ks[13], (L, CONV_WIDTH, CONV_CH), CONV_WIDTH ** -0.5),
        'conv_b': nrm(ks[14], (L, CONV_CH), 0.02),
        'dt_bias': dt0 + jnp.log(-jnp.expm1(-dt0)),
        'a_log': jnp.log(jax.random.uniform(ks[16], (L, SSM_HEADS), jnp.float32, 1.0, 16.0)),
        'd_skip': 1.0 + nrm(ks[17], (L, SSM_HEADS), 0.1),
        'ssm_norm_g': 1.0 + nrm(ks[18], (L, SSM_WIDTH), 0.05),
        'w_out': nrm(ks[19], (L, MIX_WIDTH, D), MIX_WIDTH ** -0.5),
        'norm2_g': 1.0 + nrm(ks[20], (L, D), 0.05),
        'w_mlp1': nrm(ks[21], (L, D, D_FF), D ** -0.5),
        'w_mlp2': nrm(ks[22], (L, D_FF, D), D_FF ** -0.5),
        'final_norm_g': 1.0 + nrm(ks[23], (D,), 0.05),
    }


def _fwd_reference(x, c, ada_w, ada_b, norm1_g, w_in, gm_ln_g, gm_ln_b, gm_ws, gm_bs, gm_norm_g,
              attn_sinks, attn_norm_g, conv_w, conv_b, dt_bias, a_log, d_skip, ssm_norm_g,
              w_out, norm2_g, w_mlp1, w_mlp2, final_norm_g):
    c_act = jax.nn.silu(c)
    for l in range(DEPTH):
        mod = c_act @ ada_w[l] + ada_b[l]
        sh1, sc1, g1, sh2, sc2, g2 = [m[:, None, :] for m in jnp.split(mod, 6, axis=-1)]
        h = rms_norm(x, norm1_g[l]) * (1.0 + sc1) + sh1
        u_a, v_a, q_b, k_b, v_b, z_c, xbc_c, dt_c = split_projection(h @ w_in[l])
        out_a = spatial_gating_mixer(u_a, v_a, gm_ln_g[l], gm_ln_b[l], gm_ws[l], gm_bs[l], gm_norm_g[l])
        out_b = sliding_window_sink_attention(q_b, k_b, v_b, attn_sinks[l], attn_norm_g[l])
        out_c = ssd_mixer(z_c, xbc_c, dt_c, conv_w[l], conv_b[l], dt_bias[l], a_log[l], d_skip[l], ssm_norm_g[l])
        mix = jnp.concatenate([out_a, out_b, out_c], axis=-1) @ w_out[l]
        x = x + g1 * mix
        h = rms_norm(x, norm2_g[l]) * (1.0 + sc2) + sh2
        x = x + g2 * (jnp.square(jax.nn.relu(h @ w_mlp1[l])) @ w_mlp2[l])
    return rms_norm(x, final_norm_g)


import jax as _jax
import jax.numpy as _jnp

TWIN_FORMAT = 'train_step'
FWD_PARAMS = ['x', 'c', 'ada_w', 'ada_b', 'norm1_g', 'w_in', 'gm_ln_g', 'gm_ln_b', 'gm_ws', 'gm_bs', 'gm_norm_g', 'attn_sinks', 'attn_norm_g', 'conv_w', 'conv_b', 'dt_bias', 'a_log', 'd_skip', 'ssm_norm_g', 'w_out', 'norm2_g', 'w_mlp1', 'w_mlp2', 'final_norm_g']
TWIN_WEIGHTS = ['ada_w', 'ada_b', 'norm1_g', 'w_in', 'gm_ln_g', 'gm_ln_b', 'gm_ws', 'gm_bs', 'gm_norm_g', 'attn_sinks', 'attn_norm_g', 'conv_w', 'conv_b', 'dt_bias', 'a_log', 'd_skip', 'ssm_norm_g', 'w_out', 'norm2_g', 'w_mlp1', 'w_mlp2', 'final_norm_g']
TWIN_DIFF_INPUT = 'x'
TWIN_INPUTS = ['x', 'c', 'ada_w', 'ada_b', 'norm1_g', 'w_in', 'gm_ln_g', 'gm_ln_b', 'gm_ws', 'gm_bs', 'gm_norm_g', 'attn_sinks', 'attn_norm_g', 'conv_w', 'conv_b', 'dt_bias', 'a_log', 'd_skip', 'ssm_norm_g', 'w_out', 'norm2_g', 'w_mlp1', 'w_mlp2', 'final_norm_g', 'loss_target', 'm_ada_w', 'm_ada_b', 'm_norm1_g', 'm_w_in', 'm_gm_ln_g', 'm_gm_ln_b', 'm_gm_ws', 'm_gm_bs', 'm_gm_norm_g', 'm_attn_sinks', 'm_attn_norm_g', 'm_conv_w', 'm_conv_b', 'm_dt_bias', 'm_a_log', 'm_d_skip', 'm_ssm_norm_g', 'm_w_out', 'm_norm2_g', 'm_w_mlp1', 'm_w_mlp2', 'm_final_norm_g', 'v_ada_w', 'v_ada_b', 'v_norm1_g', 'v_w_in', 'v_gm_ln_g', 'v_gm_ln_b', 'v_gm_ws', 'v_gm_bs', 'v_gm_norm_g', 'v_attn_sinks', 'v_attn_norm_g', 'v_conv_w', 'v_conv_b', 'v_dt_bias', 'v_a_log', 'v_d_skip', 'v_ssm_norm_g', 'v_w_out', 'v_norm2_g', 'v_w_mlp1', 'v_w_mlp2', 'v_final_norm_g']
TWIN_OUTPUTS = ['loss', 'grad_x', 'grad_ada_w', 'grad_ada_b', 'grad_norm1_g', 'grad_w_in', 'grad_gm_ln_g', 'grad_gm_ln_b', 'grad_gm_ws', 'grad_gm_bs', 'grad_gm_norm_g', 'grad_attn_sinks', 'grad_attn_norm_g', 'grad_conv_w', 'grad_conv_b', 'grad_dt_bias', 'grad_a_log', 'grad_d_skip', 'grad_ssm_norm_g', 'grad_w_out', 'grad_norm2_g', 'grad_w_mlp1', 'grad_w_mlp2', 'grad_final_norm_g', 'delta_ada_w', 'delta_ada_b', 'delta_norm1_g', 'delta_w_in', 'delta_gm_ln_g', 'delta_gm_ln_b', 'delta_gm_ws', 'delta_gm_bs', 'delta_gm_norm_g', 'delta_attn_sinks', 'delta_attn_norm_g', 'delta_conv_w', 'delta_conv_b', 'delta_dt_bias', 'delta_a_log', 'delta_d_skip', 'delta_ssm_norm_g', 'delta_w_out', 'delta_norm2_g', 'delta_w_mlp1', 'delta_w_mlp2', 'delta_final_norm_g', 'new_m_ada_w', 'new_m_ada_b', 'new_m_norm1_g', 'new_m_w_in', 'new_m_gm_ln_g', 'new_m_gm_ln_b', 'new_m_gm_ws', 'new_m_gm_bs', 'new_m_gm_norm_g', 'new_m_attn_sinks', 'new_m_attn_norm_g', 'new_m_conv_w', 'new_m_conv_b', 'new_m_dt_bias', 'new_m_a_log', 'new_m_d_skip', 'new_m_ssm_norm_g', 'new_m_w_out', 'new_m_norm2_g', 'new_m_w_mlp1', 'new_m_w_mlp2', 'new_m_final_norm_g', 'new_v_ada_w', 'new_v_ada_b', 'new_v_norm1_g', 'new_v_w_in', 'new_v_gm_ln_g', 'new_v_gm_ln_b', 'new_v_gm_ws', 'new_v_gm_bs', 'new_v_gm_norm_g', 'new_v_attn_sinks', 'new_v_attn_norm_g', 'new_v_conv_w', 'new_v_conv_b', 'new_v_dt_bias', 'new_v_a_log', 'new_v_d_skip', 'new_v_ssm_norm_g', 'new_v_w_out', 'new_v_norm2_g', 'new_v_w_mlp1', 'new_v_w_mlp2', 'new_v_final_norm_g']
TWIN_LEAF_KINDS = {'loss': 'loss', 'grad_x': 'grad_x', 'grad_ada_w': 'grad_w', 'grad_ada_b': 'grad_w', 'grad_norm1_g': 'grad_w', 'grad_w_in': 'grad_w', 'grad_gm_ln_g': 'grad_w', 'grad_gm_ln_b': 'grad_w', 'grad_gm_ws': 'grad_w', 'grad_gm_bs': 'grad_w', 'grad_gm_norm_g': 'grad_w', 'grad_attn_sinks': 'grad_w', 'grad_attn_norm_g': 'grad_w', 'grad_conv_w': 'grad_w', 'grad_conv_b': 'grad_w', 'grad_dt_bias': 'grad_w', 'grad_a_log': 'grad_w', 'grad_d_skip': 'grad_w', 'grad_ssm_norm_g': 'grad_w', 'grad_w_out': 'grad_w', 'grad_norm2_g': 'grad_w', 'grad_w_mlp1': 'grad_w', 'grad_w_mlp2': 'grad_w', 'grad_final_norm_g': 'grad_w', 'delta_ada_w': 'delta_w', 'delta_ada_b': 'delta_w', 'delta_norm1_g': 'delta_w', 'delta_w_in': 'delta_w', 'delta_gm_ln_g': 'delta_w', 'delta_gm_ln_b': 'delta_w', 'delta_gm_ws': 'delta_w', 'delta_gm_bs': 'delta_w', 'delta_gm_norm_g': 'delta_w', 'delta_attn_sinks': 'delta_w', 'delta_attn_norm_g': 'delta_w', 'delta_conv_w': 'delta_w', 'delta_conv_b': 'delta_w', 'delta_dt_bias': 'delta_w', 'delta_a_log': 'delta_w', 'delta_d_skip': 'delta_w', 'delta_ssm_norm_g': 'delta_w', 'delta_w_out': 'delta_w', 'delta_norm2_g': 'delta_w', 'delta_w_mlp1': 'delta_w', 'delta_w_mlp2': 'delta_w', 'delta_final_norm_g': 'delta_w', 'new_m_ada_w': 'new_m', 'new_m_ada_b': 'new_m', 'new_m_norm1_g': 'new_m', 'new_m_w_in': 'new_m', 'new_m_gm_ln_g': 'new_m', 'new_m_gm_ln_b': 'new_m', 'new_m_gm_ws': 'new_m', 'new_m_gm_bs': 'new_m', 'new_m_gm_norm_g': 'new_m', 'new_m_attn_sinks': 'new_m', 'new_m_attn_norm_g': 'new_m', 'new_m_conv_w': 'new_m', 'new_m_conv_b': 'new_m', 'new_m_dt_bias': 'new_m', 'new_m_a_log': 'new_m', 'new_m_d_skip': 'new_m', 'new_m_ssm_norm_g': 'new_m', 'new_m_w_out': 'new_m', 'new_m_norm2_g': 'new_m', 'new_m_w_mlp1': 'new_m', 'new_m_w_mlp2': 'new_m', 'new_m_final_norm_g': 'new_m', 'new_v_ada_w': 'new_v', 'new_v_ada_b': 'new_v', 'new_v_norm1_g': 'new_v', 'new_v_w_in': 'new_v', 'new_v_gm_ln_g': 'new_v', 'new_v_gm_ln_b': 'new_v', 'new_v_gm_ws': 'new_v', 'new_v_gm_bs': 'new_v', 'new_v_gm_norm_g': 'new_v', 'new_v_attn_sinks': 'new_v', 'new_v_attn_norm_g': 'new_v', 'new_v_conv_w': 'new_v', 'new_v_conv_b': 'new_v', 'new_v_dt_bias': 'new_v', 'new_v_a_log': 'new_v', 'new_v_d_skip': 'new_v', 'new_v_ssm_norm_g': 'new_v', 'new_v_w_out': 'new_v', 'new_v_norm2_g': 'new_v', 'new_v_w_mlp1': 'new_v', 'new_v_w_mlp2': 'new_v', 'new_v_final_norm_g': 'new_v'}


def _forward(args):
    return _fwd_reference(*[args[k] for k in FWD_PARAMS])


def _output_shape():
    out = _jax.eval_shape(lambda: _forward(_fwd_setup_inputs(0)))
    return out.shape, out.dtype

N_MICROBATCH = 1
ADAM_LR = 0.001
ADAM_B1 = 0.9
ADAM_B2 = 0.999
ADAM_EPS = 1e-08
ADAM_WD = 0.01
ADAM_STEP = 10
PER_EXAMPLE_BATCH_AXIS = {'x': 0, 'c': 0, 'loss_target': 0}
SHARED_INPUTS = []
_WEIGHT_DTYPES = {'ada_w': _jnp.float32, 'ada_b': _jnp.float32, 'norm1_g': _jnp.float32, 'w_in': _jnp.float32, 'gm_ln_g': _jnp.float32, 'gm_ln_b': _jnp.float32, 'gm_ws': _jnp.float32, 'gm_bs': _jnp.float32, 'gm_norm_g': _jnp.float32, 'attn_sinks': _jnp.float32, 'attn_norm_g': _jnp.float32, 'conv_w': _jnp.float32, 'conv_b': _jnp.float32, 'dt_bias': _jnp.float32, 'a_log': _jnp.float32, 'd_skip': _jnp.float32, 'ssm_norm_g': _jnp.float32, 'w_out': _jnp.float32, 'norm2_g': _jnp.float32, 'w_mlp1': _jnp.float32, 'w_mlp2': _jnp.float32, 'final_norm_g': _jnp.float32}
MOMENT_SCALE = {'ada_w': 7.777400e-02, 'ada_b': 1.460224e-01, 'norm1_g': 3.144489e-02, 'w_in': 2.613755e-02, 'gm_ln_g': 1.377308e-02, 'gm_ln_b': 1.428349e-02, 'gm_ws': 1.414913e-02, 'gm_bs': 1.965107e-02, 'gm_norm_g': 3.183761e-02, 'attn_sinks': 7.421805e-03, 'attn_norm_g': 5.231257e-02, 'conv_w': 2.280193e-02, 'conv_b': 3.059488e-02, 'dt_bias': 5.492288e-02, 'a_log': 9.573008e-02, 'd_skip': 1.638116e-01, 'ssm_norm_g': 3.170951e-02, 'w_out': 3.413593e-02, 'norm2_g': 4.028746e-02, 'w_mlp1': 2.073815e-02, 'w_mlp2': 4.856479e-02, 'final_norm_g': 1.619247e+01}


def _to_microbatches(a, axis):
    t = _jnp.moveaxis(a, axis, 0)
    t = t.reshape((N_MICROBATCH, t.shape[0] // N_MICROBATCH) + t.shape[1:])
    return _jnp.moveaxis(t, 1, axis + 1)


def setup_inputs(seed: int = 0) -> dict:
    inp = _fwd_setup_inputs(seed)
    key = _jax.random.fold_in(_jax.random.key(seed), 7919)
    shape, _ = _output_shape()
    out = dict(inp)
    out["loss_target"] = _jax.random.normal(_jax.random.fold_in(key, 0), shape, _jnp.float32)
    for i, name in enumerate(TWIN_WEIGHTS):
        w = inp[name].astype(_jnp.float32)
        if MOMENT_SCALE is None:
            s = _jnp.sqrt(_jnp.mean(_jnp.square(w)) + 1e-30)
        else:
            s = MOMENT_SCALE[name]
        km, kv = _jax.random.split(_jax.random.fold_in(key, i + 1))
        out[name] = w
        out["m_" + name] = s * _jax.random.normal(km, w.shape, _jnp.float32)
        out["v_" + name] = (s * s) * _jax.random.uniform(kv, w.shape, _jnp.float32, 0.5, 1.5)
    if N_MICROBATCH > 1:
        for name, axis in PER_EXAMPLE_BATCH_AXIS.items():
            out[name] = _to_microbatches(out[name], axis)
    return {'x': out['x'], 'c': out['c'], 'ada_w': out['ada_w'], 'ada_b': out['ada_b'], 'norm1_g': out['norm1_g'], 'w_in': out['w_in'], 'gm_ln_g': out['gm_ln_g'], 'gm_ln_b': out['gm_ln_b'], 'gm_ws': out['gm_ws'], 'gm_bs': out['gm_bs'], 'gm_norm_g': out['gm_norm_g'], 'attn_sinks': out['attn_sinks'], 'attn_norm_g': out['attn_norm_g'], 'conv_w': out['conv_w'], 'conv_b': out['conv_b'], 'dt_bias': out['dt_bias'], 'a_log': out['a_log'], 'd_skip': out['d_skip'], 'ssm_norm_g': out['ssm_norm_g'], 'w_out': out['w_out'], 'norm2_g': out['norm2_g'], 'w_mlp1': out['w_mlp1'], 'w_mlp2': out['w_mlp2'], 'final_norm_g': out['final_norm_g'], 'loss_target': out['loss_target'], 'm_ada_w': out['m_ada_w'], 'm_ada_b': out['m_ada_b'], 'm_norm1_g': out['m_norm1_g'], 'm_w_in': out['m_w_in'], 'm_gm_ln_g': out['m_gm_ln_g'], 'm_gm_ln_b': out['m_gm_ln_b'], 'm_gm_ws': out['m_gm_ws'], 'm_gm_bs': out['m_gm_bs'], 'm_gm_norm_g': out['m_gm_norm_g'], 'm_attn_sinks': out['m_attn_sinks'], 'm_attn_norm_g': out['m_attn_norm_g'], 'm_conv_w': out['m_conv_w'], 'm_conv_b': out['m_conv_b'], 'm_dt_bias': out['m_dt_bias'], 'm_a_log': out['m_a_log'], 'm_d_skip': out['m_d_skip'], 'm_ssm_norm_g': out['m_ssm_norm_g'], 'm_w_out': out['m_w_out'], 'm_norm2_g': out['m_norm2_g'], 'm_w_mlp1': out['m_w_mlp1'], 'm_w_mlp2': out['m_w_mlp2'], 'm_final_norm_g': out['m_final_norm_g'], 'v_ada_w': out['v_ada_w'], 'v_ada_b': out['v_ada_b'], 'v_norm1_g': out['v_norm1_g'], 'v_w_in': out['v_w_in'], 'v_gm_ln_g': out['v_gm_ln_g'], 'v_gm_ln_b': out['v_gm_ln_b'], 'v_gm_ws': out['v_gm_ws'], 'v_gm_bs': out['v_gm_bs'], 'v_gm_norm_g': out['v_gm_norm_g'], 'v_attn_sinks': out['v_attn_sinks'], 'v_attn_norm_g': out['v_attn_norm_g'], 'v_conv_w': out['v_conv_w'], 'v_conv_b': out['v_conv_b'], 'v_dt_bias': out['v_dt_bias'], 'v_a_log': out['v_a_log'], 'v_d_skip': out['v_d_skip'], 'v_ssm_norm_g': out['v_ssm_norm_g'], 'v_w_out': out['v_w_out'], 'v_norm2_g': out['v_norm2_g'], 'v_w_mlp1': out['v_w_mlp1'], 'v_w_mlp2': out['v_w_mlp2'], 'v_final_norm_g': out['v_final_norm_g']}


def _loss(weights, diff, rest, loss_target):
    with _jax.named_scope("forward"):
        args = {**rest, TWIN_DIFF_INPUT: diff, **{k: w.astype(_WEIGHT_DTYPES[k]) for k, w in weights.items()}}
        y = _forward(args)
    with _jax.named_scope("loss_head"):
        err = _jnp.square(y.astype(_jnp.float32) - loss_target)
        return 0.5 * _jnp.sum(_jnp.mean(err, axis=-1)) if err.ndim else 0.5 * err


def _adamw(w, g, m, v):
    m = ADAM_B1 * m + (1.0 - ADAM_B1) * g
    v = ADAM_B2 * v + (1.0 - ADAM_B2) * _jnp.square(g)
    m_hat = m / (1.0 - ADAM_B1 ** ADAM_STEP)
    v_hat = v / (1.0 - ADAM_B2 ** ADAM_STEP)
    delta = -ADAM_LR * (m_hat / (_jnp.sqrt(v_hat) + ADAM_EPS) + ADAM_WD * w)
    return delta, m, v


def reference(x, c, ada_w, ada_b, norm1_g, w_in, gm_ln_g, gm_ln_b, gm_ws, gm_bs, gm_norm_g, attn_sinks, attn_norm_g, conv_w, conv_b, dt_bias, a_log, d_skip, ssm_norm_g, w_out, norm2_g, w_mlp1, w_mlp2, final_norm_g, loss_target, m_ada_w, m_ada_b, m_norm1_g, m_w_in, m_gm_ln_g, m_gm_ln_b, m_gm_ws, m_gm_bs, m_gm_norm_g, m_attn_sinks, m_attn_norm_g, m_conv_w, m_conv_b, m_dt_bias, m_a_log, m_d_skip, m_ssm_norm_g, m_w_out, m_norm2_g, m_w_mlp1, m_w_mlp2, m_final_norm_g, v_ada_w, v_ada_b, v_norm1_g, v_w_in, v_gm_ln_g, v_gm_ln_b, v_gm_ws, v_gm_bs, v_gm_norm_g, v_attn_sinks, v_attn_norm_g, v_conv_w, v_conv_b, v_dt_bias, v_a_log, v_d_skip, v_ssm_norm_g, v_w_out, v_norm2_g, v_w_mlp1, v_w_mlp2, v_final_norm_g):
    given = dict(x=x, c=c, ada_w=ada_w, ada_b=ada_b, norm1_g=norm1_g, w_in=w_in, gm_ln_g=gm_ln_g, gm_ln_b=gm_ln_b, gm_ws=gm_ws, gm_bs=gm_bs, gm_norm_g=gm_norm_g, attn_sinks=attn_sinks, attn_norm_g=attn_norm_g, conv_w=conv_w, conv_b=conv_b, dt_bias=dt_bias, a_log=a_log, d_skip=d_skip, ssm_norm_g=ssm_norm_g, w_out=w_out, norm2_g=norm2_g, w_mlp1=w_mlp1, w_mlp2=w_mlp2, final_norm_g=final_norm_g, loss_target=loss_target, m_ada_w=m_ada_w, m_ada_b=m_ada_b, m_norm1_g=m_norm1_g, m_w_in=m_w_in, m_gm_ln_g=m_gm_ln_g, m_gm_ln_b=m_gm_ln_b, m_gm_ws=m_gm_ws, m_gm_bs=m_gm_bs, m_gm_norm_g=m_gm_norm_g, m_attn_sinks=m_attn_sinks, m_attn_norm_g=m_attn_norm_g, m_conv_w=m_conv_w, m_conv_b=m_conv_b, m_dt_bias=m_dt_bias, m_a_log=m_a_log, m_d_skip=m_d_skip, m_ssm_norm_g=m_ssm_norm_g, m_w_out=m_w_out, m_norm2_g=m_norm2_g, m_w_mlp1=m_w_mlp1, m_w_mlp2=m_w_mlp2, m_final_norm_g=m_final_norm_g, v_ada_w=v_ada_w, v_ada_b=v_ada_b, v_norm1_g=v_norm1_g, v_w_in=v_w_in, v_gm_ln_g=v_gm_ln_g, v_gm_ln_b=v_gm_ln_b, v_gm_ws=v_gm_ws, v_gm_bs=v_gm_bs, v_gm_norm_g=v_gm_norm_g, v_attn_sinks=v_attn_sinks, v_attn_norm_g=v_attn_norm_g, v_conv_w=v_conv_w, v_conv_b=v_conv_b, v_dt_bias=v_dt_bias, v_a_log=v_a_log, v_d_skip=v_d_skip, v_ssm_norm_g=v_ssm_norm_g, v_w_out=v_w_out, v_norm2_g=v_norm2_g, v_w_mlp1=v_w_mlp1, v_w_mlp2=v_w_mlp2, v_final_norm_g=v_final_norm_g)
    weights = {n: given[n] for n in TWIN_WEIGHTS}
    shared = {n: given[n] for n in SHARED_INPUTS}
    per_example = {n: given[n] for n in ['x', 'c']}
    grad_fn = _jax.value_and_grad(_loss, argnums=(0, 1))

    def one_microbatch(ex, loss_target):
        ex = dict(ex)
        diff = ex.pop(TWIN_DIFF_INPUT)
        return grad_fn(weights, diff, {**shared, **ex}, loss_target)

    if N_MICROBATCH == 1:
        loss, (grad_w, grad_x) = one_microbatch(per_example, given["loss_target"])
    else:
        def body(carry, xs):
            loss_sum, grad_sum = carry
            l_k, (gw_k, gx_k) = one_microbatch(xs[0], xs[1])
            with _jax.named_scope("update"):
                return (loss_sum + l_k, _jax.tree.map(_jnp.add, grad_sum, gw_k)), gx_k

        init = (_jnp.zeros((), _jnp.float32), _jax.tree.map(_jnp.zeros_like, weights))
        (loss, grad_w), grad_x = _jax.lax.scan(body, init, (per_example, given["loss_target"]))
    with _jax.named_scope("update"):
        delta_w, new_m, new_v = {}, {}, {}
        for n in TWIN_WEIGHTS:
            delta_w[n], new_m[n], new_v[n] = _adamw(weights[n], grad_w[n], given["m_" + n], given["v_" + n])
    return (loss, grad_x, *[grad_w[n] for n in TWIN_WEIGHTS], *[delta_w[n] for n in TWIN_WEIGHTS],
            *[new_m[n] for n in TWIN_WEIGHTS], *[new_v[n] for n in TWIN_WEIGHTS])
```

```python
import functools
import math

import jax
import jax.numpy as jnp
import numpy as np
from jax import lax
from jax.experimental import pallas as pl
from jax.experimental.pallas import tpu as pltpu

F32 = jnp.float32
BF16 = jnp.bfloat16
HI = lax.Precision.HIGHEST
MESH = pl.DeviceIdType.MESH

CHUNK = 128
GM_HEADS, GM_HD = 4, 128
ATT_HEADS, ATT_KV, ATT_HD = 8, 2, 64
WINDOW = 128
SSM_HEADS, SSM_HD, SSM_GROUPS, SSM_STATE, CONV_K = 16, 64, 2, 128, 4
EPS = 1e-6
LN_EPS = 1e-5
NEG = -1e30
LANES = 128

GMW = GM_HEADS * GM_HD
ATW = ATT_HEADS * ATT_HD
KVW = ATT_KV * ATT_HD
SSW = SSM_HEADS * SSM_HD
BCW = SSM_GROUPS * SSM_STATE
CCH = SSW + 2 * BCW
GRW = SSW // SSM_GROUPS
IN_SIZES = (GMW, GMW, ATW, KVW, KVW, SSW, CCH, SSM_HEADS)
IN_W = sum(IN_SIZES)
OFF_XBC, OFF_Q, OFF_Z, OFF_U, OFF_V, OFF_K, OFF_VV, OFF_DT = 0, 1536, 2048, 3072, 3584, 4096, 4224, 4352
PW = 4608

ADAM_LR, ADAM_B1, ADAM_B2, ADAM_EPS, ADAM_WD, ADAM_STEP = 0.001, 0.9, 0.999, 1e-08, 0.01, 10

VMEM_LIMIT = 56 * 1024 * 1024


def _cp(sem=None):
    return pltpu.CompilerParams(dimension_semantics=sem, vmem_limit_bytes=VMEM_LIMIT)


_DN = {"nn": (((1,), (0,)), ((), ())), "nt": (((1,), (1,)), ((), ())), "tn": (((0,), (0,)), ((), ()))}


def _dot(form, a, b):
    return lax.dot_general(a.astype(BF16), b.astype(BF16), _DN[form], preferred_element_type=F32)


@jax.custom_vjp
def _nn(a, b):
    return _dot("nn", a, b)


@jax.custom_vjp
def _nt(a, b):
    return _dot("nt", a, b)


@jax.custom_vjp
def _tn(a, b):
    return _dot("tn", a, b)


_nn.defvjp(lambda a, b: (_dot("nn", a, b), (a, b)), lambda r, g: (_dot("nt", g, r[1]), _dot("tn", r[0], g)))
_nt.defvjp(lambda a, b: (_dot("nt", a, b), (a, b)), lambda r, g: (_dot("nn", g, r[1]), _dot("tn", g, r[0])))
_tn.defvjp(lambda a, b: (_dot("tn", a, b), (a, b)), lambda r, g: (_dot("nt", r[1], g), _dot("nn", r[0], g)))


def _hdot(a, b):
    return jnp.dot(a, b, precision=HI, preferred_element_type=F32)


def _silu(x):
    return x * (1.0 / (1.0 + jnp.exp(-x)))


def _softplus(x):
    return jnp.maximum(x, 0.0) + jnp.log1p(jnp.exp(-jnp.abs(x)))


def _gelu(x):
    return 0.5 * x * (1.0 + jnp.tanh(math.sqrt(2.0 / math.pi) * (x + 0.044715 * (x * x * x))))


def _rms(y, g):
    return y * lax.rsqrt(jnp.mean(y * y, axis=-1, keepdims=True) + EPS) * g


def _mm(form, a, b, *, dims, tm, tn, tk, out_dtypes, name, a_spec=None, b_spec=None, out_specs=None,
        out_shapes=None, extras=(), epi=None, pro_a=None, pro_b=None):
    m, n, k = dims
    tm, tn, tk = min(tm, m), min(tn, n), min(tk, k)
    assert m % tm == 0 and n % tn == 0 and k % tk == 0, (name, dims, tm, tn, tk)
    nk = k // tk
    if a_spec is None:
        a_spec = (pl.BlockSpec((tk, tm), lambda i, j, kk: (kk, i)) if form == "tn"
                  else pl.BlockSpec((tm, tk), lambda i, j, kk: (i, kk)))
    if b_spec is None:
        b_spec = (pl.BlockSpec((tn, tk), lambda i, j, kk: (j, kk)) if form == "nt"
                  else pl.BlockSpec((tk, tn), lambda i, j, kk: (kk, j)))
    n_out = len(out_dtypes)
    if out_specs is None:
        out_specs = [pl.BlockSpec((tm, tn), lambda i, j, kk: (i, j))] * n_out
    if out_shapes is None:
        out_shapes = [(m, n)] * n_out
    ne = len(extras)

    def body(*refs):
        a_ref, b_ref = refs[0], refs[1]
        ex = refs[2:2 + ne]
        outs = refs[2 + ne:2 + ne + n_out]

        def write(val):
            res = epi(val, *[e[...] for e in ex]) if epi is not None else (val,)
            for o, r in zip(outs, res):
                o[...] = r.astype(o.dtype)

        av = a_ref[...]
        if pro_a is not None:
            av = pro_a(av)
        bv = b_ref[...]
        if pro_b is not None:
            bv = pro_b(bv)
        part = lax.dot_general(av, bv, _DN[form], preferred_element_type=F32)
        if nk == 1:
            write(part)
        else:
            acc = refs[-1]
            kk = pl.program_id(2)

            @pl.when(kk == 0)
            def _():
                acc[...] = part

            @pl.when(kk > 0)
            def _():
                acc[...] += part

            @pl.when(kk == nk - 1)
            def _():
                write(acc[...])

    res = pl.pallas_call(
        body, name=name, grid=(m // tm, n // tn, nk),
        in_specs=[a_spec, b_spec] + [s for _, s in extras],
        out_specs=out_specs,
        out_shape=[jax.ShapeDtypeStruct(s, d) for s, d in zip(out_shapes, out_dtypes)],
        scratch_shapes=[pltpu.VMEM((tm, tn), F32)] if nk > 1 else [],
        compiler_params=_cp(("parallel", "parallel", "arbitrary")),
    )(a, b, *[e for e, _ in extras])
    return res


def _row_tile(s):
    return min(512, s)


def ln_mod_fwd(x, g, sc, sh, name):
    bsz, s, d = x.shape
    ts = _row_tile(s)

    def body(x_ref, g_ref, sc_ref, sh_ref, o_ref):
        xv = x_ref[...]
        r = lax.rsqrt(jnp.mean(xv * xv, axis=-1, keepdims=True) + EPS)
        o_ref[...] = ((xv * r * g_ref[...]) * (1.0 + sc_ref[...]) + sh_ref[...]).astype(o_ref.dtype)

    row = pl.BlockSpec((None, ts, d), lambda b, i: (b, i, 0))
    vec = pl.BlockSpec((None, 1, d), lambda b, i: (b, 0, 0))
    return pl.pallas_call(
        body, name=name, grid=(bsz, s // ts),
        in_specs=[row, pl.BlockSpec((1, d), lambda b, i: (0, 0)), vec, vec],
        out_specs=row, out_shape=jax.ShapeDtypeStruct(x.shape, BF16),
        compiler_params=_cp(("parallel", "parallel")),
    )(x, g, sc, sh)


def ln_mod_bwd(dh, x, dres, g, sc, name):
    bsz, s, d = x.shape
    ts = _row_tile(s)

    def body(dh_ref, x_ref, dres_ref, g_ref, sc_ref, dx_ref, dsc_ref, dsh_ref, dg_ref):
        b, i = pl.program_id(0), pl.program_id(1)
        xv, dhv, gv = x_ref[...], dh_ref[...], g_ref[...]
        r = lax.rsqrt(jnp.mean(xv * xv, axis=-1, keepdims=True) + EPS)
        xn = xv * r
        a = dhv * (1.0 + sc_ref[...])
        dxn = a * gv
        dx_ref[...] = dres_ref[...] + r * (dxn - xn * jnp.mean(dxn * xn, axis=-1, keepdims=True))
        p_sc = jnp.sum(dhv * (xn * gv), axis=0, keepdims=True)
        p_sh = jnp.sum(dhv, axis=0, keepdims=True)
        p_g = jnp.sum(a * xn, axis=0, keepdims=True)

        @pl.when(i == 0)
        def _():
            dsc_ref[...] = p_sc
            dsh_ref[...] = p_sh

        @pl.when(i > 0)
        def _():
            dsc_ref[...] += p_sc
            dsh_ref[...] += p_sh

        @pl.when((i == 0) & (b == 0))
        def _():
            dg_ref[...] = p_g

        @pl.when((i > 0) | (b > 0))
        def _():
            dg_ref[...] += p_g

    row = pl.BlockSpec((None, ts, d), lambda b, i: (b, i, 0))
    vec = pl.BlockSpec((None, 1, d), lambda b, i: (b, 0, 0))
    one = pl.BlockSpec((1, d), lambda b, i: (0, 0))
    return pl.pallas_call(
        body, name=name, grid=(bsz, s // ts),
        in_specs=[row, row, row, one, vec],
        out_specs=[row, vec, vec, one],
        out_shape=[jax.ShapeDtypeStruct(x.shape, F32), jax.ShapeDtypeStruct((bsz, 1, d), F32),
                   jax.ShapeDtypeStruct((bsz, 1, d), F32), jax.ShapeDtypeStruct((1, d), F32)],
        compiler_params=_cp(("arbitrary", "arbitrary")),
    )(dh, x, dres, g, sc)


def gate_bwd(dx, mm, gate, name):
    bsz, s, d = dx.shape
    ts = _row_tile(s)

    def body(dx_ref, m_ref, g_ref, dm_ref, dg_ref):
        i = pl.program_id(1)
        dxv = dx_ref[...]
        dm_ref[...] = (dxv * g_ref[...]).astype(dm_ref.dtype)
        p = jnp.sum(dxv * m_ref[...], axis=0, keepdims=True)

        @pl.when(i == 0)
        def _():
            dg_ref[...] = p

        @pl.when(i > 0)
        def _():
            dg_ref[...] += p

    row = pl.BlockSpec((None, ts, d), lambda b, i: (b, i, 0))
    vec = pl.BlockSpec((None, 1, d), lambda b, i: (b, 0, 0))
    return pl.pallas_call(
        body, name=name, grid=(bsz, s // ts),
        in_specs=[row, row, vec], out_specs=[row, vec],
        out_shape=[jax.ShapeDtypeStruct(dx.shape, BF16), jax.ShapeDtypeStruct((bsz, 1, d), F32)],
        compiler_params=_cp(("parallel", "arbitrary")),
    )(dx, mm, gate)


def loss_head(x, g, tgt, name):
    bsz, s, d = x.shape
    ts = _row_tile(s)

    def body(x_ref, g_ref, t_ref, dx_ref, dg_ref, l_ref):
        b, i = pl.program_id(0), pl.program_id(1)
        xv, gv = x_ref[...], g_ref[...]
        r = lax.rsqrt(jnp.mean(xv * xv, axis=-1, keepdims=True) + EPS)
        xn = xv * r
        e = xn * gv - t_ref[...]
        dy = e * (1.0 / d)
        dxn = dy * gv
        dx_ref[...] = r * (dxn - xn * jnp.mean(dxn * xn, axis=-1, keepdims=True))
        p_g = jnp.sum(dy * xn, axis=0, keepdims=True)
        p_l = jnp.zeros((1, LANES), F32) + jnp.sum(e * e) * (0.5 / d)
        first = (i == 0) & (b == 0)

        @pl.when(first)
        def _():
            dg_ref[...] = p_g
            l_ref[...] = p_l

        @pl.when(jnp.logical_not(first))
        def _():
            dg_ref[...] += p_g
            l_ref[...] += p_l

    row = pl.BlockSpec((None, ts, d), lambda b, i: (b, i, 0))
    one = pl.BlockSpec((1, d), lambda b, i: (0, 0))
    return pl.pallas_call(
        body, name=name, grid=(bsz, s // ts),
        in_specs=[row, one, row],
        out_specs=[row, one, pl.BlockSpec((1, LANES), lambda b, i: (0, 0))],
        out_shape=[jax.ShapeDtypeStruct(x.shape, F32), jax.ShapeDtypeStruct((1, d), F32),
                   jax.ShapeDtypeStruct((1, LANES), F32)],
        compiler_params=_cp(("arbitrary", "arbitrary")),
    )(x, g, tgt)


def _gmlp_chunk(u_raw, v_raw, ln_g, ln_b, w, bs_t, out_g):
    c = u_raw.shape[0]
    u, v = _gelu(u_raw), _gelu(v_raw)
    tril = lax.broadcasted_iota(jnp.int32, (c, c), 0) >= lax.broadcasted_iota(jnp.int32, (c, c), 1)
    ys = []
    for h in range(GM_HEADS):
        sl = slice(h * GM_HD, (h + 1) * GM_HD)
        vh = v[:, sl]
        xc = vh - jnp.mean(vh, axis=-1, keepdims=True)
        vn = xc * lax.rsqrt(jnp.mean(xc * xc, axis=-1, keepdims=True) + LN_EPS) * ln_g[:, sl] + ln_b[:, sl]
        gate = _nn(jnp.where(tril, w[h], 0.0), vn) + bs_t[:, h:h + 1]
        ys.append(u[:, sl] * gate)
    return _rms(jnp.concatenate(ys, axis=1), out_g)


def _gmlp_specs(bsz, nc):
    seg = lambda off: pl.BlockSpec((None, CHUNK, GMW), lambda b, c: (b, c, off // GMW))
    full = lambda shape: pl.BlockSpec(shape, lambda b, c: (0,) * len(shape))
    par = [full((1, GMW)), full((1, GMW)), full((GM_HEADS, CHUNK, CHUNK)), full((CHUNK, GM_HEADS)), full((1, GMW))]
    return seg, full, par


def gmlp_fwd(p, prm, name):
    bsz, s, _ = p.shape
    nc = s // CHUNK
    seg, _, par = _gmlp_specs(bsz, nc)

    def body(u_ref, v_ref, lg, lb, w, bt, og, o_ref):
        o_ref[...] = _gmlp_chunk(u_ref[...], v_ref[...], lg[...], lb[...], w[...], bt[...], og[...]).astype(o_ref.dtype)

    return pl.pallas_call(
        body, name=name, grid=(bsz, nc),
        in_specs=[seg(OFF_U), seg(OFF_V)] + par,
        out_specs=pl.BlockSpec((None, CHUNK, GMW), lambda b, c: (b, c, 0)),
        out_shape=jax.ShapeDtypeStruct((bsz, s, GMW), BF16),
        compiler_params=_cp(("parallel", "parallel")),
    )(p, p, *prm)


def _accumulate(first, refs, vals):
    @pl.when(first)
    def _():
        for r, v in zip(refs, vals):
            r[...] = v

    @pl.when(jnp.logical_not(first))
    def _():
        for r, v in zip(refs, vals):
            r[...] += v


def gmlp_bwd(p, dmix, prm, name):
    bsz, s, _ = p.shape
    nc = s // CHUNK
    seg, full, par = _gmlp_specs(bsz, nc)

    def body(u_ref, v_ref, do_ref, lg, lb, w, bt, og, du_ref, dv_ref, *dpar):
        first = (pl.program_id(0) == 0) & (pl.program_id(1) == 0)
        _, vjp = jax.vjp(_gmlp_chunk, u_ref[...], v_ref[...], lg[...], lb[...], w[...], bt[...], og[...])
        gr = vjp(do_ref[...])
        du_ref[...] = gr[0].astype(du_ref.dtype)
        dv_ref[...] = gr[1].astype(dv_ref.dtype)
        _accumulate(first, dpar, gr[2:])

    out_seg = pl.BlockSpec((None, CHUNK, GMW), lambda b, c: (b, c, 0))
    return pl.pallas_call(
        body, name=name, grid=(bsz, nc),
        in_specs=[seg(OFF_U), seg(OFF_V), out_seg] + par,
        out_specs=[out_seg, out_seg] + par,
        out_shape=[jax.ShapeDtypeStruct((bsz, s, GMW), BF16)] * 2 + [jax.ShapeDtypeStruct(x.shape, F32) for x in prm],
        compiler_params=_cp(("arbitrary", "arbitrary")),
    )(p, p, dmix, *prm)


def _attn_block(q, kp, kc, vp, vc, sinks, out_g, has_prev):
    w = q.shape[0]
    k2 = jnp.concatenate([kp, kc], axis=0)
    v2 = jnp.concatenate([vp, vc], axis=0)
    qi = lax.broadcasted_iota(jnp.int32, (w, 2 * w), 0)
    kj = lax.broadcasted_iota(jnp.int32, (w, 2 * w), 1)
    diff = qi + w - kj
    valid = (diff >= 0) & (diff < w) & ((kj >= w) | has_prev)
    grp = ATT_HEADS // ATT_KV
    outs = []
    for kv in range(ATT_KV):
        kh = k2[:, kv * ATT_HD:(kv + 1) * ATT_HD]
        vh = v2[:, kv * ATT_HD:(kv + 1) * ATT_HD]
        for gi in range(grp):
            h = kv * grp + gi
            sc = _nt(q[:, h * ATT_HD:(h + 1) * ATT_HD], kh) * (ATT_HD ** -0.5)
            sc = jnp.where(valid, sc, NEG)
            sink = sinks[:, h:h + 1]
            m = jnp.maximum(jnp.max(sc, axis=-1, keepdims=True), sink)
            e = jnp.exp(sc - m)
            pr = e / (jnp.sum(e, axis=-1, keepdims=True) + jnp.exp(sink - m))
            outs.append(_nn(pr, vh))
    return _rms(jnp.concatenate(outs, axis=1), out_g)


def attn_fwd(p, sinks, out_g, name):
    bsz, s, _ = p.shape
    nb = s // WINDOW

    def body(q_ref, kp_ref, kc_ref, vp_ref, vc_ref, s_ref, g_ref, o_ref):
        o_ref[...] = _attn_block(q_ref[...], kp_ref[...], kc_ref[...], vp_ref[...], vc_ref[...], s_ref[...],
                                 g_ref[...], pl.program_id(1) > 0).astype(o_ref.dtype)

    cur = lambda off: pl.BlockSpec((None, WINDOW, KVW), lambda b, n: (b, n, off // KVW))
    prev = lambda off: pl.BlockSpec((None, WINDOW, KVW), lambda b, n: (b, jnp.maximum(n - 1, 0), off // KVW))
    return pl.pallas_call(
        body, name=name, grid=(bsz, nb),
        in_specs=[pl.BlockSpec((None, WINDOW, ATW), lambda b, n: (b, n, OFF_Q // ATW)),
                  prev(OFF_K), cur(OFF_K), prev(OFF_VV), cur(OFF_VV),
                  pl.BlockSpec((1, ATT_HEADS), lambda b, n: (0, 0)), pl.BlockSpec((1, ATW), lambda b, n: (0, 0))],
        out_specs=pl.BlockSpec((None, WINDOW, ATW), lambda b, n: (b, n, 0)),
        out_shape=jax.ShapeDtypeStruct((bsz, s, ATW), BF16),
        compiler_params=_cp(("parallel", "parallel")),
    )(p, p, p, p, p, sinks, out_g)


def attn_bwd(p, dmix, sinks, out_g, name):
    bsz, s, _ = p.shape
    nb = s // WINDOW

    def body(q_ref, kp_ref, kc_ref, vp_ref, vc_ref, do_ref, s_ref, g_ref,
             dq_ref, dk_ref, dv_ref, ds_ref, dg_ref, ck, cv):
        b, n = pl.program_id(0), pl.program_id(1)

        @pl.when(n == 0)
        def _():
            ck[...] = jnp.zeros_like(ck)
            cv[...] = jnp.zeros_like(cv)

        @pl.when(n < nb)
        def _():
            fn = functools.partial(_attn_block, has_prev=n > 0)
            _, vjp = jax.vjp(fn, q_ref[...], kp_ref[...], kc_ref[...], vp_ref[...], vc_ref[...], s_ref[...], g_ref[...])
            dq, dkp, dkc, dvp, dvc, dsk, dgg = vjp(do_ref[...])
            dq_ref[...] = dq.astype(dq_ref.dtype)
            dk_ref[...] = (ck[...] + dkp).astype(dk_ref.dtype)
            dv_ref[...] = (cv[...] + dvp).astype(dv_ref.dtype)
            ck[...] = dkc
            cv[...] = dvc
            _accumulate((b == 0) & (n == 0), (ds_ref, dg_ref), (dsk, dgg))

        @pl.when(n == nb)
        def _():
            dk_ref[...] = ck[...].astype(dk_ref.dtype)
            dv_ref[...] = cv[...].astype(dv_ref.dtype)

    at = lambda n: jnp.minimum(n, nb - 1)
    cur = lambda off: pl.BlockSpec((None, WINDOW, KVW), lambda b, n: (b, at(n), off // KVW))
    prev = lambda off: pl.BlockSpec((None, WINDOW, KVW), lambda b, n: (b, jnp.maximum(at(n) - 1, 0), off // KVW))
    kv_out = pl.BlockSpec((None, WINDOW, KVW), lambda b, n: (b, jnp.maximum(n - 1, 0), 0))
    return pl.pallas_call(
        body, name=name, grid=(bsz, nb + 1),
        in_specs=[pl.BlockSpec((None, WINDOW, ATW), lambda b, n: (b, at(n), OFF_Q // ATW)),
                  prev(OFF_K), cur(OFF_K), prev(OFF_VV), cur(OFF_VV),
                  pl.BlockSpec((None, WINDOW, ATW), lambda b, n: (b, at(n), GMW // ATW)),
                  pl.BlockSpec((1, ATT_HEADS), lambda b, n: (0, 0)), pl.BlockSpec((1, ATW), lambda b, n: (0, 0))],
        out_specs=[pl.BlockSpec((None, WINDOW, ATW), lambda b, n: (b, at(n), 0)), kv_out, kv_out,
                   pl.BlockSpec((1, ATT_HEADS), lambda b, n: (0, 0)), pl.BlockSpec((1, ATW), lambda b, n: (0, 0))],
        out_shape=[jax.ShapeDtypeStruct((bsz, s, ATW), BF16), jax.ShapeDtypeStruct((bsz, s, KVW), BF16),
                   jax.ShapeDtypeStruct((bsz, s, KVW), BF16), jax.ShapeDtypeStruct((1, ATT_HEADS), F32),
                   jax.ShapeDtypeStruct((1, ATW), F32)],
        scratch_shapes=[pltpu.VMEM((WINDOW, KVW), F32), pltpu.VMEM((WINDOW, KVW), F32)],
        compiler_params=_cp(("arbitrary", "arbitrary")),
    )(p, p, p, p, p, dmix, sinks, out_g)


CONV_CT = 256


def _shift_down(x, j):
    if j == 0:
        return x
    rows = lax.broadcasted_iota(jnp.int32, x.shape, 0)
    return jnp.where(rows >= j, pltpu.roll(x, j, 0), 0.0)


def _shift_up(x, j):
    if j == 0:
        return x
    s = x.shape[0]
    rows = lax.broadcasted_iota(jnp.int32, x.shape, 0)
    return jnp.where(rows < s - j, pltpu.roll(x, s - j, 0), 0.0)


def conv_fwd(p, w, bias, name):
    bsz, s, _ = p.shape

    def body(x_ref, w_ref, b_ref, o_ref):
        xv, wv = x_ref[...], w_ref[...]
        pre = b_ref[...] + sum(wv[k:k + 1, :] * _shift_down(xv, CONV_K - 1 - k) for k in range(CONV_K))
        o_ref[...] = _silu(pre)

    blk = pl.BlockSpec((None, s, CONV_CT), lambda b, j: (b, 0, j))
    return pl.pallas_call(
        body, name=name, grid=(bsz, CCH // CONV_CT),
        in_specs=[blk, pl.BlockSpec((CONV_K, CONV_CT), lambda b, j: (0, j)), pl.BlockSpec((1, CONV_CT), lambda b, j: (0, j))],
        out_specs=blk, out_shape=jax.ShapeDtypeStruct((bsz, s, CCH), F32),
        compiler_params=_cp(("parallel", "parallel")),
    )(p, w, bias)


def conv_bwd(p, dxc, w, bias, name):
    bsz, s, _ = p.shape

    def body(x_ref, d_ref, w_ref, b_ref, dx_ref, dw_ref, db_ref):
        b = pl.program_id(1)
        xv, wv = x_ref[...], w_ref[...]
        xs = [_shift_down(xv, CONV_K - 1 - k) for k in range(CONV_K)]
        pre = b_ref[...] + sum(wv[k:k + 1, :] * xs[k] for k in range(CONV_K))
        sg = 1.0 / (1.0 + jnp.exp(-pre))
        dpre = d_ref[...] * (sg * (1.0 + pre * (1.0 - sg)))
        dx_ref[...] = sum(wv[k:k + 1, :] * _shift_up(dpre, CONV_K - 1 - k) for k in range(CONV_K)).astype(dx_ref.dtype)
        p_w = jnp.concatenate([jnp.sum(dpre * xs[k], axis=0, keepdims=True) for k in range(CONV_K)], axis=0)
        p_b = jnp.sum(dpre, axis=0, keepdims=True)
        _accumulate(b == 0, (dw_ref, db_ref), (p_w, p_b))

    blk = pl.BlockSpec((None, s, CONV_CT), lambda j, b: (b, 0, j))
    wsp = pl.BlockSpec((CONV_K, CONV_CT), lambda j, b: (0, j))
    bsp = pl.BlockSpec((1, CONV_CT), lambda j, b: (0, j))
    return pl.pallas_call(
        body, name=name, grid=(CCH // CONV_CT, bsz),
        in_specs=[blk, blk, wsp, bsp], out_specs=[blk, wsp, bsp],
        out_shape=[jax.ShapeDtypeStruct((bsz, s, CCH), BF16), jax.ShapeDtypeStruct((CONV_K, CCH), F32),
                   jax.ShapeDtypeStruct((1, CCH), F32)],
        compiler_params=_cp(("parallel", "arbitrary")),
    )(p, dxc, w, bias)


def _ssd_consts():
    c = CHUNK
    r = lax.broadcasted_iota(jnp.int32, (c, c), 0)
    q = lax.broadcasted_iota(jnp.int32, (c, c), 1)
    hrow = lax.broadcasted_iota(jnp.int32, (LANES, SSW), 0)
    hcol = lax.broadcasted_iota(jnp.int32, (LANES, SSW), 1) // SSM_HD
    expand = (hrow == hcol).astype(F32)
    return expand, (r >= q).astype(F32), (r <= q).astype(F32), r >= q


def _ssd_chunk(xc, dtr, z, prev_t, dt_bias, a_log, d_skip, norm_g):
    c = xc.shape[0]
    expand, tril1, triu1, causal = _ssd_consts()
    xs, bm, cm = xc[:, :SSW], xc[:, SSW:SSW + BCW], xc[:, SSW + BCW:]
    dt = _softplus(dtr + dt_bias)
    da = dt * (-jnp.exp(a_log))
    a_cs = _hdot(tril1, da)
    a_cs_t = _hdot(da.T, triu1)
    dt_e = _hdot(dt, expand)
    acs_e = _hdot(a_cs, expand)
    alast_e = acs_e[c - 1:c, :]
    dsk_e = _hdot(jnp.broadcast_to(d_skip, (8, LANES)), expand)[0:1, :]
    xdt = xs * dt_e
    hg = SSM_HEADS // SSM_GROUPS
    ys, new_t = [], []
    for g in range(SSM_GROUPS):
        bg = bm[:, g * SSM_STATE:(g + 1) * SSM_STATE]
        cg = cm[:, g * SSM_STATE:(g + 1) * SSM_STATE]
        sl = slice(g * GRW, (g + 1) * GRW)
        cb = _nt(cg, bg)
        xdt_g = xdt[:, sl]
        st = _tn(bg, xdt_g * jnp.exp(alast_e[:, sl] - acs_e[:, sl]))
        new_t.append(prev_t[:, sl] * jnp.exp(alast_e[:, sl]) + st)
        y_off = _nn(cg, prev_t[:, sl]) * jnp.exp(acs_e[:, sl])
        yd = []
        for hh in range(hg):
            h = g * hg + hh
            decay = jnp.exp(jnp.where(causal, a_cs[:, h:h + 1] - a_cs_t[h:h + 1, :], NEG))
            yd.append(_nn(cb * decay, xdt_g[:, hh * SSM_HD:(hh + 1) * SSM_HD]))
        ys.append(jnp.concatenate(yd, axis=1) + y_off)
    y = (jnp.concatenate(ys, axis=1) + xs * dsk_e) * _silu(z)
    yn = [y[:, g * GRW:(g + 1) * GRW] * lax.rsqrt(jnp.mean(jnp.square(y[:, g * GRW:(g + 1) * GRW]), axis=-1, keepdims=True) + EPS)
          for g in range(SSM_GROUPS)]
    return jnp.concatenate(yn, axis=1) * norm_g, jnp.concatenate(new_t, axis=1)


def ssd_fwd(xc, p, prm, name):
    bsz, s, _ = p.shape
    nc = s // CHUNK

    def body(xc_ref, dt_ref, z_ref, db, al, dk, ng, o_ref, st_ref, state):
        @pl.when(pl.program_id(1) == 0)
        def _():
            state[...] = jnp.zeros_like(state)

        prev = state[...]
        st_ref[...] = prev
        out, new = _ssd_chunk(xc_ref[...], dt_ref[...], z_ref[...], prev, db[...], al[...], dk[...], ng[...])
        o_ref[...] = out.astype(o_ref.dtype)
        state[...] = new

    vec = pl.BlockSpec((1, LANES), lambda b, c: (0, 0))
    return pl.pallas_call(
        body, name=name, grid=(bsz, nc),
        in_specs=[pl.BlockSpec((None, CHUNK, CCH), lambda b, c: (b, c, 0)),
                  pl.BlockSpec((None, CHUNK, LANES), lambda b, c: (b, c, OFF_DT // LANES)),
                  pl.BlockSpec((None, CHUNK, SSW), lambda b, c: (b, c, OFF_Z // SSW)),
                  vec, vec, vec, pl.BlockSpec((1, SSW), lambda b, c: (0, 0))],
        out_specs=[pl.BlockSpec((None, CHUNK, SSW), lambda b, c: (b, c, 0)),
                   pl.BlockSpec((None, None, SSM_STATE, SSW), lambda b, c: (b, c, 0, 0))],
        out_shape=[jax.ShapeDtypeStruct((bsz, s, SSW), BF16), jax.ShapeDtypeStruct((bsz, nc, SSM_STATE, SSW), F32)],
        scratch_shapes=[pltpu.VMEM((SSM_STATE, SSW), F32)],
        compiler_params=_cp(("parallel", "arbitrary")),
    )(xc, p, p, *prm)


def ssd_bwd(xc, p, states, dmix, prm, name):
    bsz, s, _ = p.shape
    nc = s // CHUNK

    def body(xc_ref, dt_ref, z_ref, st_ref, do_ref, db, al, dk, ng, dxc_ref, ddt_ref, dz_ref, *rest):
        dpar, dstate = rest[:4], rest[4]
        b, c = pl.program_id(0), pl.program_id(1)

        @pl.when(c == 0)
        def _():
            dstate[...] = jnp.zeros_like(dstate)

        _, vjp = jax.vjp(_ssd_chunk, xc_ref[...], dt_ref[...], z_ref[...], st_ref[...], db[...], al[...], dk[...], ng[...])
        gr = vjp((do_ref[...], dstate[...]))
        dxc_ref[...] = gr[0]
        ddt_ref[...] = gr[1].astype(ddt_ref.dtype)
        dz_ref[...] = gr[2].astype(dz_ref.dtype)
        dstate[...] = gr[3]
        _accumulate((b == 0) & (c == 0), dpar, gr[4:])

    rv = lambda c: nc - 1 - c
    vec = pl.BlockSpec((1, LANES), lambda b, c: (0, 0))
    ngs = pl.BlockSpec((1, SSW), lambda b, c: (0, 0))
    return pl.pallas_call(
        body, name=name, grid=(bsz, nc),
        in_specs=[pl.BlockSpec((None, CHUNK, CCH), lambda b, c: (b, rv(c), 0)),
                  pl.BlockSpec((None, CHUNK, LANES), lambda b, c: (b, rv(c), OFF_DT // LANES)),
                  pl.BlockSpec((None, CHUNK, SSW), lambda b, c: (b, rv(c), OFF_Z // SSW)),
                  pl.BlockSpec((None, None, SSM_STATE, SSW), lambda b, c: (b, rv(c), 0, 0)),
                  pl.BlockSpec((None, CHUNK, SSW), lambda b, c: (b, rv(c), (GMW + ATW) // SSW)),
                  vec, vec, vec, ngs],
        out_specs=[pl.BlockSpec((None, CHUNK, CCH), lambda b, c: (b, rv(c), 0)),
                   pl.BlockSpec((None, CHUNK, LANES), lambda b, c: (b, rv(c), 0)),
                   pl.BlockSpec((None, CHUNK, SSW), lambda b, c: (b, rv(c), 0)),
                   vec, vec, vec, ngs],
        out_shape=[jax.ShapeDtypeStruct((bsz, s, CCH), F32), jax.ShapeDtypeStruct((bsz, s, LANES), BF16),
                   jax.ShapeDtypeStruct((bsz, s, SSW), BF16)] + [jax.ShapeDtypeStruct((1, LANES), F32)] * 3
                  + [jax.ShapeDtypeStruct((1, SSW), F32)],
        scratch_shapes=[pltpu.VMEM((SSM_STATE, SSW), F32)],
        compiler_params=_cp(("arbitrary", "arbitrary")),
    )(xc, p, p, states, dmix, *prm)


def _rows2d(a):
    return a.reshape(-1, a.shape[-1])


def _ew_tile(r, c):
    t = r
    while t * c > (1 << 20) and t % 16 == 0:
        t //= 2
    return t


def add_pair(a, b, name):
    k, h, c = a.shape
    tr = _ew_tile(h, c)

    def body(a_ref, b_ref, o_ref, ob_ref):
        s = a_ref[...] + b_ref[...]
        o_ref[...] = s
        ob_ref[...] = s.astype(ob_ref.dtype)

    blk = pl.BlockSpec((None, tr, c), lambda kk, i: (kk, i, 0))
    return pl.pallas_call(
        body, name=name, grid=(k, h // tr), in_specs=[blk, blk], out_specs=[blk, blk],
        out_shape=[jax.ShapeDtypeStruct(a.shape, F32), jax.ShapeDtypeStruct(a.shape, BF16)],
        compiler_params=_cp(("parallel", "parallel")),
    )(a, b)


def sum_own_recv(own, recv, name):
    h, c = own.shape
    tr = _ew_tile(h, c)

    def body(o_ref, r_ref, out_ref):
        s = o_ref[...]
        for j in range(3):
            s = s + r_ref[j].astype(F32)
        out_ref[...] = s

    return pl.pallas_call(
        body, name=name, grid=(h // tr,),
        in_specs=[pl.BlockSpec((tr, c), lambda i: (i, 0)), pl.BlockSpec((3, tr, c), lambda i: (0, i, 0))],
        out_specs=pl.BlockSpec((tr, c), lambda i: (i, 0)),
        out_shape=jax.ShapeDtypeStruct((h, c), F32),
        compiler_params=_cp(("parallel",)),
    )(own, recv)


def sum_devices(parts, name):
    n, r, c = parts.shape
    tr = _ew_tile(r, c * n)

    def body(p_ref, o_ref):
        s = p_ref[0]
        for j in range(1, n):
            s = s + p_ref[j]
        o_ref[...] = s

    return pl.pallas_call(
        body, name=name, grid=(r // tr,),
        in_specs=[pl.BlockSpec((n, tr, c), lambda i: (0, i, 0))],
        out_specs=pl.BlockSpec((tr, c), lambda i: (i, 0)),
        out_shape=jax.ShapeDtypeStruct((r, c), F32),
        compiler_params=_cp(("parallel",)),
    )(parts)


def adamw(w, m, v, g, name):
    r, c = w.shape
    tr = _ew_tile(r, c * 2)

    def body(w_ref, m_ref, v_ref, g_ref, d_ref, mo_ref, vo_ref):
        gv = g_ref[...]
        mn = ADAM_B1 * m_ref[...] + (1.0 - ADAM_B1) * gv
        vn = ADAM_B2 * v_ref[...] + (1.0 - ADAM_B2) * (gv * gv)
        mh = mn / (1.0 - ADAM_B1 ** ADAM_STEP)
        vh = vn / (1.0 - ADAM_B2 ** ADAM_STEP)
        d_ref[...] = -ADAM_LR * (mh / (jnp.sqrt(vh) + ADAM_EPS) + ADAM_WD * w_ref[...])
        mo_ref[...] = mn
        vo_ref[...] = vn

    blk = pl.BlockSpec((tr, c), lambda i: (i, 0))
    return pl.pallas_call(
        body, name=name, grid=(r // tr,), in_specs=[blk] * 4, out_specs=[blk] * 3,
        out_shape=[jax.ShapeDtypeStruct((r, c), F32)] * 3,
        compiler_params=_cp(("parallel",)),
    )(w, m, v, g)


def _place():
    x, y, c = lax.axis_index("x"), lax.axis_index("y"), lax.axis_index("c")
    chips = [(1 - x, y), (x, 1 - y), (1 - x, 1 - y)]
    return x, y, c, chips


def all_gather_small(v, name):
    r, w = v.shape

    def body(x_ref, out_ref, send_sems, recv_sems, local_sem):
        x, y, c, chips = _place()
        me, sibling = (x, y, c), (x, y, 1 - c)

        def rows(px, py, pc):
            return out_ref.at[pl.ds((4 * px + 2 * py + pc) * r, r), :]

        def copy(k, block, to, src=None):
            return pltpu.make_async_remote_copy(
                src_ref=rows(*block) if src is None else src, dst_ref=rows(*block),
                send_sem=send_sems.at[k], recv_sem=recv_sems.at[k], device_id=to, device_id_type=MESH)

        mine = pltpu.make_async_copy(x_ref, rows(*me), local_sem)
        mine.start()
        first = [copy(0, me, sibling, src=x_ref)]
        first += [copy(1 + j, me, (*chip, c), src=x_ref) for j, chip in enumerate(chips)]
        for cp in first:
            cp.start()
        passed = [copy(4 + j, (*chip, c), sibling) for j, chip in enumerate(chips)]
        for j, chip in enumerate(chips):
            copy(1 + j, (*chip, c), me).wait_recv()
            passed[j].start()
        copy(0, sibling, me).wait_recv()
        for j, chip in enumerate(chips):
            copy(4 + j, (*chip, 1 - c), me).wait_recv()
        for cp in first + passed:
            cp.wait_send()
        mine.wait()

    out = pl.pallas_call(
        body, name=name, out_shape=jax.ShapeDtypeStruct((8 * r, w), v.dtype),
        in_specs=[pl.BlockSpec(memory_space=pltpu.VMEM)], out_specs=pl.BlockSpec(memory_space=pltpu.VMEM),
        scratch_shapes=[pltpu.SemaphoreType.DMA((7,)), pltpu.SemaphoreType.DMA((7,)), pltpu.SemaphoreType.DMA],
        compiler_params=pltpu.CompilerParams(vmem_limit_bytes=VMEM_LIMIT),
    )(v)
    return out.reshape(8, r, w)


_HBM = pl.BlockSpec(memory_space=pltpu.HBM)


def gather_weights(shards, name):
    n = len(shards)

    def body(*refs):
        ins, outs = refs[:n], refs[n:2 * n]
        send_sems, recv_sems, local_sems = refs[2 * n:]
        x, y, c, chips = _place()
        sibling = (x, y, 1 - c)
        kme = 2 * x + y

        def half(i, k, hc):
            h = ins[i].shape[0] // 2
            return outs[i].at[k, pl.ds(hc * h, h), :]

        def copy(i, s, k, hc, to, src=None):
            return pltpu.make_async_remote_copy(
                src_ref=half(i, k, hc) if src is None else src, dst_ref=half(i, k, hc),
                send_sem=send_sems.at[6 * i + s], recv_sem=recv_sems.at[6 * i + s], device_id=to, device_id_type=MESH)

        local = [pltpu.make_async_copy(ins[i], outs[i].at[kme], local_sems.at[i]) for i in range(n)]
        for cp in local:
            cp.start()
        sent = []
        for i in range(n):
            h = ins[i].shape[0] // 2
            for j, chip in enumerate(chips):
                sent.append(copy(i, j, kme, c, (*chip, c), src=ins[i].at[pl.ds(c * h, h), :]))
                sent[-1].start()
        for i in range(n):
            for j, (cx, cy) in enumerate(chips):
                copy(i, j, 2 * cx + cy, c, sibling).wait_recv()
                sent.append(copy(i, 3 + j, 2 * cx + cy, c, sibling))
                sent[-1].start()
        for i in range(n):
            for j, (cx, cy) in enumerate(chips):
                copy(i, 3 + j, 2 * cx + cy, 1 - c, sibling).wait_recv()
        for cp in sent:
            cp.wait_send()
        for cp in local:
            cp.wait()

    return pl.pallas_call(
        body, name=name, out_shape=[jax.ShapeDtypeStruct((4,) + s.shape, s.dtype) for s in shards],
        in_specs=[_HBM] * n, out_specs=[_HBM] * n,
        scratch_shapes=[pltpu.SemaphoreType.DMA((6 * n,)), pltpu.SemaphoreType.DMA((6 * n,)), pltpu.SemaphoreType.DMA((n,))],
    )(*shards)


def swap_halves(grads, name):
    n = len(grads)

    def body(*refs):
        ins, mine, theirs = refs[:n], refs[n:2 * n], refs[2 * n:3 * n]
        send_sems, recv_sems, local_sems = refs[3 * n:]
        x, y, c, _ = _place()
        sibling = (x, y, 1 - c)
        local, sent = [], []
        for i in range(n):
            h = ins[i].shape[1] // 2
            local.append(pltpu.make_async_copy(ins[i].at[:, pl.ds(c * h, h), :], mine[i], local_sems.at[i]))
            local[-1].start()
            sent.append(pltpu.make_async_remote_copy(
                src_ref=ins[i].at[:, pl.ds((1 - c) * h, h), :], dst_ref=theirs[i],
                send_sem=send_sems.at[i], recv_sem=recv_sems.at[i], device_id=sibling, device_id_type=MESH))
            sent[-1].start()
        for cp in sent:
            cp.wait_recv()
        for cp in sent:
            cp.wait_send()
        for cp in local:
            cp.wait()

    half = [jax.ShapeDtypeStruct((4, g.shape[1] // 2, g.shape[2]), g.dtype) for g in grads]
    res = pl.pallas_call(
        body, name=name, out_shape=half + half, in_specs=[_HBM] * n, out_specs=[_HBM] * (2 * n),
        scratch_shapes=[pltpu.SemaphoreType.DMA((n,)), pltpu.SemaphoreType.DMA((n,)), pltpu.SemaphoreType.DMA((n,))],
    )(*grads)
    return res[:n], res[n:]


def scatter_to_chips(sums_f32, sums_bf16, name):
    n = len(sums_f32)

    def body(*refs):
        sf, sb, own, got = refs[:n], refs[n:2 * n], refs[2 * n:3 * n], refs[3 * n:4 * n]
        send_sems, recv_sems, local_sems = refs[4 * n:]
        x, y, c, chips = _place()
        local, sent = [], []
        for i in range(n):
            local.append(pltpu.make_async_copy(sf[i].at[2 * x + y], own[i], local_sems.at[i]))
            local[-1].start()
            for j, (cx, cy) in enumerate(chips):
                sent.append(pltpu.make_async_remote_copy(
                    src_ref=sb[i].at[2 * cx + cy], dst_ref=got[i].at[j],
                    send_sem=send_sems.at[3 * i + j], recv_sem=recv_sems.at[3 * i + j],
                    device_id=(cx, cy, c), device_id_type=MESH))
                sent[-1].start()
        for cp in sent:
            cp.wait_recv()
        for cp in sent:
            cp.wait_send()
        for cp in local:
            cp.wait()

    res = pl.pallas_call(
        body, name=name,
        out_shape=[jax.ShapeDtypeStruct(s.shape[1:], F32) for s in sums_f32]
        + [jax.ShapeDtypeStruct((3,) + s.shape[1:], BF16) for s in sums_f32],
        in_specs=[_HBM] * (2 * n), out_specs=[_HBM] * (2 * n),
        scratch_shapes=[pltpu.SemaphoreType.DMA((3 * n,)), pltpu.SemaphoreType.DMA((3 * n,)), pltpu.SemaphoreType.DMA((n,))],
    )(*sums_f32, *sums_bf16)
    return res[:n], res[n:]


def join_halves(halves, name):
    n = len(halves)

    def body(*refs):
        ins, outs = refs[:n], refs[n:2 * n]
        send_sems, recv_sems, local_sems = refs[2 * n:]
        x, y, c, _ = _place()
        sibling = (x, y, 1 - c)
        local, sent = [], []
        for i in range(n):
            h = ins[i].shape[0]
            dst = outs[i].at[pl.ds(c * h, h), :]
            local.append(pltpu.make_async_copy(ins[i], dst, local_sems.at[i]))
            local[-1].start()
            sent.append(pltpu.make_async_remote_copy(
                src_ref=ins[i], dst_ref=dst, send_sem=send_sems.at[i], recv_sem=recv_sems.at[i],
                device_id=sibling, device_id_type=MESH))
            sent[-1].start()
        for i in range(n):
            h = ins[i].shape[0]
            pltpu.make_async_remote_copy(
                src_ref=ins[i], dst_ref=outs[i].at[pl.ds((1 - c) * h, h), :], send_sem=send_sems.at[i],
                recv_sem=recv_sems.at[i], device_id=sibling, device_id_type=MESH).wait_recv()
        for cp in sent:
            cp.wait_send()
        for cp in local:
            cp.wait()

    return pl.pallas_call(
        body, name=name, out_shape=[jax.ShapeDtypeStruct((2 * s.shape[0], s.shape[1]), F32) for s in halves],
        in_specs=[_HBM] * n, out_specs=[_HBM] * n,
        scratch_shapes=[pltpu.SemaphoreType.DMA((n,)), pltpu.SemaphoreType.DMA((n,)), pltpu.SemaphoreType.DMA((n,))],
    )(*halves)


def reduce_scatter(grads, tag):
    n = len(grads)
    mine, theirs = swap_halves(grads, f"rs_swap_{tag}")
    sums = [add_pair(a, b, f"rs_add_{tag}_{i}") for i, (a, b) in enumerate(zip(mine, theirs))]
    own, got = scatter_to_chips([s[0] for s in sums], [s[1] for s in sums], f"rs_scatter_{tag}")
    halves = [sum_own_recv(o, g, f"rs_sum_{tag}_{i}") for i, (o, g) in enumerate(zip(own, got))]
    return join_halves(halves, f"rs_join_{tag}")


_PACK_ROWS = 8 * LANES


def _pack(arrs):
    flat = jnp.concatenate([a.reshape(-1).astype(F32) for a in arrs])
    pad = (-flat.shape[0]) % _PACK_ROWS
    return jnp.pad(flat, (0, pad)).reshape(-1, LANES)


def _unpack(flat, shapes):
    flat = flat.reshape(-1)
    out, off = [], 0
    for s in shapes:
        n = int(np.prod(s))
        out.append(flat[off:off + n].reshape(s))
        off += n
    return out


_SEGS = [(0, OFF_U, GMW), (GMW, OFF_V, GMW), (2 * GMW, OFF_Q, ATW), (2 * GMW + ATW, OFF_K, KVW),
         (2 * GMW + ATW + KVW, OFF_VV, KVW), (2 * GMW + ATW + 2 * KVW, OFF_Z, SSW),
         (2 * GMW + ATW + 2 * KVW + SSW, OFF_XBC, CCH), (IN_W - SSM_HEADS, OFF_DT, SSM_HEADS)]


def _win_to_kernel_layout(w):
    out = jnp.zeros((w.shape[0], PW), w.dtype)
    for src, dst, wd in _SEGS:
        out = lax.dynamic_update_slice(out, w[:, src:src + wd], (0, dst))
    return out


def _win_from_kernel_layout(w):
    return jnp.concatenate([w[:, dst:dst + wd] for _, dst, wd in _SEGS], axis=1)


def _relu2(a):
    r = jnp.maximum(a, 0)
    return r * r


def kernel(x, c, ada_w, ada_b, norm1_g, w_in, gm_ln_g, gm_ln_b, gm_ws, gm_bs, gm_norm_g, attn_sinks, attn_norm_g, conv_w, conv_b, dt_bias, a_log, d_skip, ssm_norm_g, w_out, norm2_g, w_mlp1, w_mlp2, final_norm_g, loss_target, m_ada_w, m_ada_b, m_norm1_g, m_w_in, m_gm_ln_g, m_gm_ln_b, m_gm_ws, m_gm_bs, m_gm_norm_g, m_attn_sinks, m_attn_norm_g, m_conv_w, m_conv_b, m_dt_bias, m_a_log, m_d_skip, m_ssm_norm_g, m_w_out, m_norm2_g, m_w_mlp1, m_w_mlp2, m_final_norm_g, v_ada_w, v_ada_b, v_norm1_g, v_w_in, v_gm_ln_g, v_gm_ln_b, v_gm_ws, v_gm_bs, v_gm_norm_g, v_attn_sinks, v_attn_norm_g, v_conv_w, v_conv_b, v_dt_bias, v_a_log, v_d_skip, v_ssm_norm_g, v_w_out, v_norm2_g, v_w_mlp1, v_w_mlp2, v_final_norm_g):
    nl = ada_w.shape[0]
    bl, s, d = x.shape
    t = bl * s
    dff4 = w_mlp1.shape[2]
    dff = 4 * dff4
    mod_w = ada_w.shape[2]
    cw_w = conv_w.shape[2]
    xi, yi, ci = lax.axis_index("x"), lax.axis_index("y"), lax.axis_index("c")
    chip = 2 * xi + yi
    dev = 2 * chip + ci
    nex = 8 * bl

    g0 = all_gather_small(_pack([c, conv_w]), "ag_c")
    g0 = g0.reshape(8, -1)
    c_all = g0[:, :bl * d].reshape(nex, d)
    cw_parts = g0[0::2, bl * d:bl * d + conv_w.size].reshape(4, nl, CONV_K, cw_w)
    conv_w_full = cw_parts.transpose(1, 2, 0, 3).reshape(nl, CONV_K, CCH)

    def c_act(a):
        return _silu(a).astype(BF16)

    def to_bf16(a):
        return a.astype(BF16)

    mod_parts = []
    for l in range(nl):
        bias = lax.dynamic_slice(ada_b[l].reshape(1, -1), (0, chip * mod_w), (1, mod_w))
        mod_parts.append(_mm("nn", c_all, ada_w[l], dims=(nex, mod_w, d), tm=nex, tn=512, tk=d, out_dtypes=[F32],
                             name=f"mod_{l}", pro_a=c_act, pro_b=to_bf16,
                             extras=[(bias, pl.BlockSpec((1, 512), lambda i, j, kk: (0, j)))],
                             epi=lambda acc, bv: (acc + bv,))[0])
    g1 = all_gather_small(_pack(mod_parts), "ag_mod").reshape(8, -1)
    mod_all = g1[0::2, :nl * nex * mod_w].reshape(4, nl, nex, mod_w).transpose(1, 2, 0, 3).reshape(nl, nex, 4 * mod_w)
    mod = lax.dynamic_slice(mod_all, (0, dev * bl, 0), (nl, bl, 4 * mod_w))
    mods = [[mod[l, :, i * d:(i + 1) * d].reshape(bl, 1, d) for i in range(6)] for l in range(nl)]

    wfull = []
    for l in range(nl):
        gi, go, g1w, g2w = gather_weights(
            [w_in[l].astype(BF16), w_out[l].astype(BF16), w_mlp1[l].astype(BF16), w_mlp2[l].astype(BF16)], f"gather_w_{l}")
        win = _win_to_kernel_layout(gi.transpose(1, 0, 2).reshape(d, IN_W))
        wfull.append((win, go.reshape(-1, d), g1w, g2w.reshape(dff, d)))

    row = lambda a: a.reshape(1, -1)
    pad16 = lambda a: jnp.pad(a.reshape(1, -1), ((0, 0), (0, LANES - SSM_HEADS)))
    tm_res = min(1024, s)

    def residual(acc, xt, gt):
        return acc, xt + gt * acc

    def res_extras(xin, gate):
        return [(xin.reshape(t, d), pl.BlockSpec((tm_res, 512), lambda i, j, kk: (i, j))),
                (gate, pl.BlockSpec((None, 1, 512), lambda i, j, kk: (i * tm_res // s, 0, j)))]

    w1_blk = lambda tk, tn: pl.BlockSpec((None, tk, tn), lambda i, j, kk: (j // (dff4 // tn), kk, j % (dff4 // tn)))

    saved = []
    xcur = x
    for l in range(nl):
        sh1, sc1, gt1, sh2, sc2, gt2 = mods[l]
        win, wout, w1, w2 = wfull[l]
        prm_a = (row(gm_ln_g[l]), row(gm_ln_b[l]), gm_ws[l], gm_bs[l].T, row(gm_norm_g[l]))
        prm_b = (row(attn_sinks[l]), row(attn_norm_g[l]))
        prm_c = (pad16(dt_bias[l]), pad16(a_log[l]), pad16(d_skip[l]), row(ssm_norm_g[l]))
        h1 = ln_mod_fwd(xcur, row(norm1_g[l]), sc1, sh1, f"ln1_fwd_{l}")
        p = _mm("nn", h1.reshape(t, d), win, dims=(t, PW, d), tm=1024, tn=512, tk=d, out_dtypes=[F32],
                name=f"proj_in_{l}")[0].reshape(bl, s, PW)
        out_a = gmlp_fwd(p, prm_a, f"gmlp_fwd_{l}")
        out_b = attn_fwd(p, *prm_b, f"attn_fwd_{l}")
        xc = conv_fwd(p, conv_w_full[l], row(conv_b[l]), f"conv_fwd_{l}")
        out_c, states = ssd_fwd(xc, p, prm_c, f"ssd_fwd_{l}")
        mix = jnp.concatenate([out_a, out_b, out_c], axis=-1)
        mm1, x2 = _mm("nn", mix.reshape(t, d), wout, dims=(t, d, d), tm=tm_res, tn=512, tk=d, out_dtypes=[F32, F32],
                      name=f"proj_out_{l}", extras=res_extras(xcur, gt1), epi=residual)
        x2 = x2.reshape(bl, s, d)
        h2 = ln_mod_fwd(x2, row(norm2_g[l]), sc2, sh2, f"ln2_fwd_{l}")
        a1 = _mm("nn", h2.reshape(t, d), w1, dims=(t, dff, d), tm=1024, tn=512, tk=d, out_dtypes=[BF16],
                 name=f"mlp1_{l}", b_spec=w1_blk(d, 512))[0]
        mm2, x3 = _mm("nn", a1, w2, dims=(t, d, dff), tm=tm_res, tn=512, tk=2048, out_dtypes=[F32, F32],
                      name=f"mlp2_{l}", extras=res_extras(x2, gt2), epi=residual, pro_a=_relu2)
        x3 = x3.reshape(bl, s, d)
        saved.append((xcur, h1, p, xc, states, mix, mm1.reshape(bl, s, d), x2, h2, a1, mm2.reshape(bl, s, d),
                      prm_a, prm_b, prm_c))
        xcur = x3

    dx, d_final_g, loss_part = loss_head(xcur, row(final_norm_g), loss_target, "loss_head")
    loss = lax.psum(loss_part[0, 0], ("x", "y", "c"))

    small_parts = [None] * nl
    dmods = [None] * nl
    big_grads = [None] * nl
    for l in reversed(range(nl)):
        sh1, sc1, gt1, sh2, sc2, gt2 = mods[l]
        win, wout, w1, w2 = wfull[l]
        xin, h1, p, xc, states, mix, mm1, x2, h2, a1, mm2, prm_a, prm_b, prm_c = saved[l]
        dm2, dgt2 = gate_bwd(dx, mm2, gt2, f"gate2_bwd_{l}")
        dm2 = dm2.reshape(t, d)
        da1 = _mm("nt", dm2, w2, dims=(t, dff, d), tm=1024, tn=512, tk=d, out_dtypes=[BF16], name=f"mlp2_dx_{l}",
                  extras=[(a1, pl.BlockSpec((1024 if t >= 1024 else t, 512), lambda i, j, kk: (i, j)))],
                  epi=lambda acc, av: (acc * (2.0 * jnp.maximum(av, 0).astype(F32)),))[0]
        dw2 = _mm("tn", a1, dm2, dims=(dff, d, t), tm=512, tn=512, tk=2048, out_dtypes=[F32], name=f"mlp2_dw_{l}",
                  pro_a=_relu2)[0]
        dw1 = _mm("tn", h2.reshape(t, d), da1, dims=(d, dff, t), tm=512, tn=512, tk=2048, out_dtypes=[F32],
                  name=f"mlp1_dw_{l}", out_shapes=[(4, d, dff4)],
                  out_specs=[pl.BlockSpec((None, 512, 512), lambda i, j, kk: (j // (dff4 // 512), i, j % (dff4 // 512)))])[0]
        dh2 = _mm("nt", da1, w1, dims=(t, d, dff), tm=1024, tn=512, tk=2048, out_dtypes=[F32], name=f"mlp1_dx_{l}",
                  b_spec=pl.BlockSpec((None, 512, 2048 if dff4 >= 2048 else dff4),
                                      lambda i, j, kk: (kk // (dff4 // min(2048, dff4)), j, kk % (dff4 // min(2048, dff4)))))[0]
        dx2, dsc2, dsh2, dn2 = ln_mod_bwd(dh2.reshape(bl, s, d), x2, dx, row(norm2_g[l]), sc2, f"ln2_bwd_{l}")
        dm1, dgt1 = gate_bwd(dx2, mm1, gt1, f"gate1_bwd_{l}")
        dm1 = dm1.reshape(t, d)
        dmix = _mm("nt", dm1, wout, dims=(t, d, d), tm=1024, tn=512, tk=d, out_dtypes=[F32],
                   name=f"proj_out_dx_{l}")[0].reshape(bl, s, d)
        dwout = _mm("tn", mix.reshape(t, d), dm1, dims=(d, d, t), tm=512, tn=512, tk=2048, out_dtypes=[F32],
                    name=f"proj_out_dw_{l}")[0]
        du, dv, dlg, dlb, dws, dbst, dgng = gmlp_bwd(p, dmix, prm_a, f"gmlp_bwd_{l}")
        dq, dk, dvv, dsinks, dang = attn_bwd(p, dmix, *prm_b, f"attn_bwd_{l}")
        dxc, ddt, dz, ddtb, dalog, ddsk, dsng = ssd_bwd(xc, p, states, dmix, prm_c, f"ssd_bwd_{l}")
        dxbc, dcw, dcb = conv_bwd(p, dxc, conv_w_full[l], row(conv_b[l]), f"conv_bwd_{l}")
        dp = jnp.concatenate([dxbc, dq, dz, du, dv, dk, dvv, ddt, jnp.zeros((bl, s, PW - OFF_DT - LANES), BF16)],
                             axis=-1).reshape(t, PW)
        dwin = _mm("tn", h1.reshape(t, d), dp, dims=(d, PW, t), tm=512, tn=512, tk=2048, out_dtypes=[F32],
                   name=f"proj_in_dw_{l}")[0]
        dh1 = _mm("nt", dp, win, dims=(t, d, PW), tm=1024, tn=512, tk=PW // 2, out_dtypes=[F32],
                  name=f"proj_in_dx_{l}")[0]
        dx, dsc1, dsh1, dn1 = ln_mod_bwd(dh1.reshape(bl, s, d), xin, dx2, row(norm1_g[l]), sc1, f"ln1_bwd_{l}")
        dmods[l] = jnp.concatenate([dsh1, dsc1, dgt1, dsh2, dsc2, dgt2], axis=-1).reshape(bl, 6 * d)
        small_parts[l] = [dn1, dlg, dlb, dws, dbst.T, dgng, dsinks, dang, dcw, dcb, ddtb[:, :SSM_HEADS],
                          dalog[:, :SSM_HEADS], ddsk[:, :SSM_HEADS], dsng, dn2]
        dwin_blocks = _win_from_kernel_layout(dwin).reshape(d, 4, IN_W // 4).transpose(1, 0, 2)
        big_grads[l] = [dwin_blocks, dwout.reshape(4, d // 4, d), dw1, dw2.reshape(4, dff4, d)]
    grad_x = dx

    reduced = [reduce_scatter(big_grads[l], str(l)) for l in range(nl)]
    big_out = []
    for i, (wt, mt, vt) in enumerate([(w_in, m_w_in, v_w_in), (w_out, m_w_out, v_w_out),
                                      (w_mlp1, m_w_mlp1, v_w_mlp1), (w_mlp2, m_w_mlp2, v_w_mlp2)]):
        g = jnp.stack([reduced[l][i] for l in range(nl)]).reshape(wt.shape)
        dl, mn, vn = adamw(_rows2d(wt), _rows2d(mt), _rows2d(vt), _rows2d(g), f"adamw_big_{i}")
        big_out.append((g, dl.reshape(wt.shape), mn.reshape(wt.shape), vn.reshape(wt.shape)))

    small_names = [norm1_g, gm_ln_g, gm_ln_b, gm_ws, gm_bs, gm_norm_g, attn_sinks, attn_norm_g, None, conv_b, dt_bias,
                   a_log, d_skip, ssm_norm_g, norm2_g]
    n_small = len(small_names)
    per_param = [jnp.stack([small_parts[l][i].reshape(-1) for l in range(nl)]) for i in range(n_small)]
    small_vec = _pack(per_param + [d_final_g])
    rs_small = small_vec.shape[0]
    dmod_local = jnp.stack(dmods, axis=1)
    g2 = all_gather_small(jnp.concatenate([small_vec, _pack([dmod_local])], axis=0), "ag_small")
    g_small = sum_devices(g2[:, :rs_small, :], "sum_small")
    dmod_all = g2[:, rs_small:, :].reshape(8, -1)[:, :bl * nl * 6 * d].reshape(nex, nl * 6 * d)
    g_ada_b = sum_devices(dmod_all.reshape(nex, -1, LANES), "sum_ada_b").reshape(nl, 6 * d)
    shapes = [(nl, int(np.prod(small_parts[0][i].shape))) for i in range(n_small)] + [(d,)]
    g_list = _unpack(g_small, shapes)
    g_conv_w = lax.dynamic_slice(g_list[8].reshape(nl, CONV_K, CCH), (0, 0, chip * cw_w), (nl, CONV_K, cw_w))

    g_ada_w = []
    for l in range(nl):
        dm_cols = lax.dynamic_slice(dmod_all.reshape(nex, nl, 6 * d)[:, l, :], (0, chip * mod_w), (nex, mod_w))
        g_ada_w.append(_mm("tn", c_all, dm_cols, dims=(d, mod_w, nex), tm=512, tn=512, tk=nex, out_dtypes=[F32],
                           name=f"ada_w_grad_{l}", pro_a=c_act, pro_b=to_bf16)[0])
    g_ada_w = jnp.stack(g_ada_w)
    d_ada_w, m_ada_w_n, v_ada_w_n = [a.reshape(ada_w.shape) for a in
                                     adamw(_rows2d(ada_w), _rows2d(m_ada_w), _rows2d(v_ada_w), _rows2d(g_ada_w), "adamw_ada_w")]

    smalls = {
        "ada_b": (ada_b, m_ada_b, v_ada_b, g_ada_b), "norm1_g": (norm1_g, m_norm1_g, v_norm1_g, g_list[0]),
        "gm_ln_g": (gm_ln_g, m_gm_ln_g, v_gm_ln_g, g_list[1]), "gm_ln_b": (gm_ln_b, m_gm_ln_b, v_gm_ln_b, g_list[2]),
        "gm_ws": (gm_ws, m_gm_ws, v_gm_ws, g_list[3]), "gm_bs": (gm_bs, m_gm_bs, v_gm_bs, g_list[4]),
        "gm_norm_g": (gm_norm_g, m_gm_norm_g, v_gm_norm_g, g_list[5]),
        "attn_sinks": (attn_sinks, m_attn_sinks, v_attn_sinks, g_list[6]),
        "attn_norm_g": (attn_norm_g, m_attn_norm_g, v_attn_norm_g, g_list[7]),
        "conv_w": (conv_w, m_conv_w, v_conv_w, g_conv_w), "conv_b": (conv_b, m_conv_b, v_conv_b, g_list[9]),
        "dt_bias": (dt_bias, m_dt_bias, v_dt_bias, g_list[10]), "a_log": (a_log, m_a_log, v_a_log, g_list[11]),
        "d_skip": (d_skip, m_d_skip, v_d_skip, g_list[12]),
        "ssm_norm_g": (ssm_norm_g, m_ssm_norm_g, v_ssm_norm_g, g_list[13]),
        "norm2_g": (norm2_g, m_norm2_g, v_norm2_g, g_list[14]),
        "final_norm_g": (final_norm_g, m_final_norm_g, v_final_norm_g, g_list[15]),
    }
    keys = list(smalls)
    wv, mv, vv_, gv = [_pack([smalls[k][i].reshape(smalls[k][0].shape) for k in keys]) for i in range(4)]
    sd_, sm_, sv_ = adamw(wv, mv, vv_, gv, "adamw_small")
    shp = [smalls[k][0].shape for k in keys]
    small_out = {k: (smalls[k][3].reshape(smalls[k][0].shape), a, b, cc)
                 for k, a, b, cc in zip(keys, _unpack(sd_, shp), _unpack(sm_, shp), _unpack(sv_, shp))}

    out = {"ada_w": (g_ada_w, d_ada_w, m_ada_w_n, v_ada_w_n), "w_in": big_out[0], "w_out": big_out[1],
           "w_mlp1": big_out[2], "w_mlp2": big_out[3], **small_out}
    order = ["ada_w", "ada_b", "norm1_g", "w_in", "gm_ln_g", "gm_ln_b", "gm_ws", "gm_bs", "gm_norm_g", "attn_sinks",
             "attn_norm_g", "conv_w", "conv_b", "dt_bias", "a_log", "d_skip", "ssm_norm_g", "w_out", "norm2_g",
             "w_mlp1", "w_mlp2", "final_norm_g"]
    return (loss, grad_x, *[out[k][0] for k in order], *[out[k][1] for k in order],
            *[out[k][2] for k in order], *[out[k][3] for k in order])
```

```python
import functools
import math

import jax
import jax.numpy as jnp
import numpy as np
from jax import lax
from jax.experimental import pallas as pl
from jax.experimental.pallas import tpu as pltpu

F32 = jnp.float32
BF16 = jnp.bfloat16
HI = lax.Precision.HIGHEST
MESH = pl.DeviceIdType.MESH

CHUNK = 128
GM_HEADS, GM_HD = 4, 128
ATT_HEADS, ATT_KV, ATT_HD = 8, 2, 64
WINDOW = 128
SSM_HEADS, SSM_HD, SSM_GROUPS, SSM_STATE, CONV_K = 16, 64, 2, 128, 4
EPS = 1e-6
LN_EPS = 1e-5
NEG = -1e30
LANES = 128

GMW = GM_HEADS * GM_HD
ATW = ATT_HEADS * ATT_HD
KVW = ATT_KV * ATT_HD
SSW = SSM_HEADS * SSM_HD
BCW = SSM_GROUPS * SSM_STATE
CCH = SSW + 2 * BCW
GRW = SSW // SSM_GROUPS
IN_SIZES = (GMW, GMW, ATW, KVW, KVW, SSW, CCH, SSM_HEADS)
IN_W = sum(IN_SIZES)
OFF_XBC, OFF_Q, OFF_Z, OFF_U, OFF_V, OFF_K, OFF_VV, OFF_DT = 0, 1536, 2048, 3072, 3584, 4096, 4224, 4352
PW = 4608

ADAM_LR, ADAM_B1, ADAM_B2, ADAM_EPS, ADAM_WD, ADAM_STEP = 0.001, 0.9, 0.999, 1e-08, 0.01, 10

VMEM_LIMIT = 56 * 1024 * 1024


def _cp(sem=None):
    return pltpu.CompilerParams(dimension_semantics=sem, vmem_limit_bytes=VMEM_LIMIT)


_DN = {"nn": (((1,), (0,)), ((), ())), "nt": (((1,), (1,)), ((), ())), "tn": (((0,), (0,)), ((), ()))}


def _dot(form, a, b):
    return lax.dot_general(a.astype(BF16), b.astype(BF16), _DN[form], preferred_element_type=F32)


@jax.custom_vjp
def _nn(a, b):
    return _dot("nn", a, b)


@jax.custom_vjp
def _nt(a, b):
    return _dot("nt", a, b)


@jax.custom_vjp
def _tn(a, b):
    return _dot("tn", a, b)


_nn.defvjp(lambda a, b: (_dot("nn", a, b), (a, b)), lambda r, g: (_dot("nt", g, r[1]), _dot("tn", r[0], g)))
_nt.defvjp(lambda a, b: (_dot("nt", a, b), (a, b)), lambda r, g: (_dot("nn", g, r[1]), _dot("tn", g, r[0])))
_tn.defvjp(lambda a, b: (_dot("tn", a, b), (a, b)), lambda r, g: (_dot("nt", r[1], g), _dot("nn", r[0], g)))


def _hdot(a, b):
    return jnp.dot(a, b, precision=HI, preferred_element_type=F32)


def _silu(x):
    return x * (1.0 / (1.0 + jnp.exp(-x)))


def _softplus(x):
    return jnp.maximum(x, 0.0) + jnp.log1p(jnp.exp(-jnp.abs(x)))


def _gelu(x):
    return 0.5 * x * (1.0 + jnp.tanh(math.sqrt(2.0 / math.pi) * (x + 0.044715 * (x * x * x))))


def _rms(y, g):
    return y * lax.rsqrt(jnp.mean(y * y, axis=-1, keepdims=True) + EPS) * g


def _mm(form, a, b, *, dims, tm, tn, tk, out_dtypes, name, a_spec=None, b_spec=None, out_specs=None,
        out_shapes=None, extras=(), epi=None, pro_a=None, pro_b=None):
    m, n, k = dims
    tm, tn, tk = min(tm, m), min(tn, n), min(tk, k)
    assert m % tm == 0 and n % tn == 0 and k % tk == 0, (name, dims, tm, tn, tk)
    nk = k // tk
    if a_spec is None:
        a_spec = (pl.BlockSpec((tk, tm), lambda i, j, kk: (kk, i)) if form == "tn"
                  else pl.BlockSpec((tm, tk), lambda i, j, kk: (i, kk)))
    if b_spec is None:
        b_spec = (pl.BlockSpec((tn, tk), lambda i, j, kk: (j, kk)) if form == "nt"
                  else pl.BlockSpec((tk, tn), lambda i, j, kk: (kk, j)))
    n_out = len(out_dtypes)
    if out_specs is None:
        out_specs = [pl.BlockSpec((tm, tn), lambda i, j, kk: (i, j))] * n_out
    if out_shapes is None:
        out_shapes = [(m, n)] * n_out
    ne = len(extras)

    def body(*refs):
        a_ref, b_ref = refs[0], refs[1]
        ex = refs[2:2 + ne]
        outs = refs[2 + ne:2 + ne + n_out]

        def write(val):
            res = epi(val, *[e[...] for e in ex]) if epi is not None else (val,)
            for o, r in zip(outs, res):
                o[...] = r.astype(o.dtype)

        av = a_ref[...]
        if pro_a is not None:
            av = pro_a(av)
        bv = b_ref[...]
        if pro_b is not None:
            bv = pro_b(bv)
        part = lax.dot_general(av, bv, _DN[form], preferred_element_type=F32)
        if nk == 1:
            write(part)
        else:
            acc = refs[-1]
            kk = pl.program_id(2)

            @pl.when(kk == 0)
            def _():
                acc[...] = part

            @pl.when(kk > 0)
            def _():
                acc[...] += part

            @pl.when(kk == nk - 1)
            def _():
                write(acc[...])

    res = pl.pallas_call(
        body, name=name, grid=(m // tm, n // tn, nk),
        in_specs=[a_spec, b_spec] + [s for _, s in extras],
        out_specs=out_specs,
        out_shape=[jax.ShapeDtypeStruct(s, d) for s, d in zip(out_shapes, out_dtypes)],
        scratch_shapes=[pltpu.VMEM((tm, tn), F32)] if nk > 1 else [],
        compiler_params=_cp(("parallel", "parallel", "arbitrary")),
    )(a, b, *[e for e, _ in extras])
    return res


def _row_tile(s):
    return min(512, s)


def ln_mod_fwd(x, g, sc, sh, name):
    bsz, s, d = x.shape
    ts = _row_tile(s)

    def body(x_ref, g_ref, sc_ref, sh_ref, o_ref):
        xv = x_ref[...]
        r = lax.rsqrt(jnp.mean(xv * xv, axis=-1, keepdims=True) + EPS)
        o_ref[...] = ((xv * r * g_ref[...]) * (1.0 + sc_ref[...]) + sh_ref[...]).astype(o_ref.dtype)

    row = pl.BlockSpec((None, ts, d), lambda b, i: (b, i, 0))
    vec = pl.BlockSpec((None, 1, d), lambda b, i: (b, 0, 0))
    return pl.pallas_call(
        body, name=name, grid=(bsz, s // ts),
        in_specs=[row, pl.BlockSpec((1, d), lambda b, i: (0, 0)), vec, vec],
        out_specs=row, out_shape=jax.ShapeDtypeStruct(x.shape, BF16),
        compiler_params=_cp(("parallel", "parallel")),
    )(x, g, sc, sh)


def ln_mod_bwd(dh, x, dres, g, sc, name):
    bsz, s, d = x.shape
    ts = _row_tile(s)

    def body(dh_ref, x_ref, dres_ref, g_ref, sc_ref, dx_ref, dsc_ref, dsh_ref, dg_ref):
        b, i = pl.program_id(0), pl.program_id(1)
        xv, dhv, gv = x_ref[...], dh_ref[...], g_ref[...]
        r = lax.rsqrt(jnp.mean(xv * xv, axis=-1, keepdims=True) + EPS)
        xn = xv * r
        a = dhv * (1.0 + sc_ref[...])
        dxn = a * gv
        dx_ref[...] = dres_ref[...] + r * (dxn - xn * jnp.mean(dxn * xn, axis=-1, keepdims=True))
        p_sc = jnp.sum(dhv * (xn * gv), axis=0, keepdims=True)
        p_sh = jnp.sum(dhv, axis=0, keepdims=True)
        p_g = jnp.sum(a * xn, axis=0, keepdims=True)

        @pl.when(i == 0)
        def _():
            dsc_ref[...] = p_sc
            dsh_ref[...] = p_sh

        @pl.when(i > 0)
        def _():
            dsc_ref[...] += p_sc
            dsh_ref[...] += p_sh

        @pl.when((i == 0) & (b == 0))
        def _():
            dg_ref[...] = p_g

        @pl.when((i > 0) | (b > 0))
        def _():
            dg_ref[...] += p_g

    row = pl.BlockSpec((None, ts, d), lambda b, i: (b, i, 0))
    vec = pl.BlockSpec((None, 1, d), lambda b, i: (b, 0, 0))
    one = pl.BlockSpec((1, d), lambda b, i: (0, 0))
    return pl.pallas_call(
        body, name=name, grid=(bsz, s // ts),
        in_specs=[row, row, row, one, vec],
        out_specs=[row, vec, vec, one],
        out_shape=[jax.ShapeDtypeStruct(x.shape, F32), jax.ShapeDtypeStruct((bsz, 1, d), F32),
                   jax.ShapeDtypeStruct((bsz, 1, d), F32), jax.ShapeDtypeStruct((1, d), F32)],
        compiler_params=_cp(("arbitrary", "arbitrary")),
    )(dh, x, dres, g, sc)


def gate_bwd(dx, mm, gate, name):
    bsz, s, d = dx.shape
    ts = _row_tile(s)

    def body(dx_ref, m_ref, g_ref, dm_ref, dg_ref):
        i = pl.program_id(1)
        dxv = dx_ref[...]
        dm_ref[...] = (dxv * g_ref[...]).astype(dm_ref.dtype)
        p = jnp.sum(dxv * m_ref[...], axis=0, keepdims=True)

        @pl.when(i == 0)
        def _():
            dg_ref[...] = p

        @pl.when(i > 0)
        def _():
            dg_ref[...] += p

    row = pl.BlockSpec((None, ts, d), lambda b, i: (b, i, 0))
    vec = pl.BlockSpec((None, 1, d), lambda b, i: (b, 0, 0))
    return pl.pallas_call(
        body, name=name, grid=(bsz, s // ts),
        in_specs=[row, row, vec], out_specs=[row, vec],
        out_shape=[jax.ShapeDtypeStruct(dx.shape, BF16), jax.ShapeDtypeStruct((bsz, 1, d), F32)],
        compiler_params=_cp(("parallel", "arbitrary")),
    )(dx, mm, gate)


def loss_head(x, g, tgt, name):
    bsz, s, d = x.shape
    ts = _row_tile(s)

    def body(x_ref, g_ref, t_ref, dx_ref, dg_ref, l_ref):
        b, i = pl.program_id(0), pl.program_id(1)
        xv, gv = x_ref[...], g_ref[...]
        r = lax.rsqrt(jnp.mean(xv * xv, axis=-1, keepdims=True) + EPS)
        xn = xv * r
        e = xn * gv - t_ref[...]
        dy = e * (1.0 / d)
        dxn = dy * gv
        dx_ref[...] = r * (dxn - xn * jnp.mean(dxn * xn, axis=-1, keepdims=True))
        p_g = jnp.sum(dy * xn, axis=0, keepdims=True)
        p_l = jnp.zeros((1, LANES), F32) + jnp.sum(e * e) * (0.5 / d)
        first = (i == 0) & (b == 0)

        @pl.when(first)
        def _():
            dg_ref[...] = p_g
            l_ref[...] = p_l

        @pl.when(jnp.logical_not(first))
        def _():
            dg_ref[...] += p_g
            l_ref[...] += p_l

    row = pl.BlockSpec((None, ts, d), lambda b, i: (b, i, 0))
    one = pl.BlockSpec((1, d), lambda b, i: (0, 0))
    return pl.pallas_call(
        body, name=name, grid=(bsz, s // ts),
        in_specs=[row, one, row],
        out_specs=[row, one, pl.BlockSpec((1, LANES), lambda b, i: (0, 0))],
        out_shape=[jax.ShapeDtypeStruct(x.shape, F32), jax.ShapeDtypeStruct((1, d), F32),
                   jax.ShapeDtypeStruct((1, LANES), F32)],
        compiler_params=_cp(("arbitrary", "arbitrary")),
    )(x, g, tgt)


def _gmlp_chunk(u_raw, v_raw, ln_g, ln_b, w, bs_t, out_g):
    c = u_raw.shape[0]
    u, v = _gelu(u_raw), _gelu(v_raw)
    tril = lax.broadcasted_iota(jnp.int32, (c, c), 0) >= lax.broadcasted_iota(jnp.int32, (c, c), 1)
    ys = []
    for h in range(GM_HEADS):
        sl = slice(h * GM_HD, (h + 1) * GM_HD)
        vh = v[:, sl]
        xc = vh - jnp.mean(vh, axis=-1, keepdims=True)
        vn = xc * lax.rsqrt(jnp.mean(xc * xc, axis=-1, keepdims=True) + LN_EPS) * ln_g[:, sl] + ln_b[:, sl]
        gate = _nn(jnp.where(tril, w[h], 0.0), vn) + bs_t[:, h:h + 1]
        ys.append(u[:, sl] * gate)
    return _rms(jnp.concatenate(ys, axis=1), out_g)


def _gmlp_specs(bsz, nc):
    seg = lambda off: pl.BlockSpec((None, CHUNK, GMW), lambda b, c: (b, c, off // GMW))
    full = lambda shape: pl.BlockSpec(shape, lambda b, c: (0,) * len(shape))
    par = [full((1, GMW)), full((1, GMW)), full((GM_HEADS, CHUNK, CHUNK)), full((CHUNK, GM_HEADS)), full((1, GMW))]
    return seg, full, par


def gmlp_fwd(p, prm, name):
    bsz, s, _ = p.shape
    nc = s // CHUNK
    seg, _, par = _gmlp_specs(bsz, nc)

    def body(u_ref, v_ref, lg, lb, w, bt, og, o_ref):
        o_ref[...] = _gmlp_chunk(u_ref[...], v_ref[...], lg[...], lb[...], w[...], bt[...], og[...]).astype(o_ref.dtype)

    return pl.pallas_call(
        body, name=name, grid=(bsz, nc),
        in_specs=[seg(OFF_U), seg(OFF_V)] + par,
        out_specs=pl.BlockSpec((None, CHUNK, GMW), lambda b, c: (b, c, 0)),
        out_shape=jax.ShapeDtypeStruct((bsz, s, GMW), BF16),
        compiler_params=_cp(("parallel", "parallel")),
    )(p, p, *prm)


def _accumulate(first, refs, vals):
    @pl.when(first)
    def _():
        for r, v in zip(refs, vals):
            r[...] = v

    @pl.when(jnp.logical_not(first))
    def _():
        for r, v in zip(refs, vals):
            r[...] += v


def gmlp_bwd(p, dmix, prm, name):
    bsz, s, _ = p.shape
    nc = s // CHUNK
    seg, full, par = _gmlp_specs(bsz, nc)

    def body(u_ref, v_ref, do_ref, lg, lb, w, bt, og, du_ref, dv_ref, *dpar):
        first = (pl.program_id(0) == 0) & (pl.program_id(1) == 0)
        _, vjp = jax.vjp(_gmlp_chunk, u_ref[...], v_ref[...], lg[...], lb[...], w[...], bt[...], og[...])
        gr = vjp(do_ref[...])
        du_ref[...] = gr[0].astype(du_ref.dtype)
        dv_ref[...] = gr[1].astype(dv_ref.dtype)
        _accumulate(first, dpar, gr[2:])

    out_seg = pl.BlockSpec((None, CHUNK, GMW), lambda b, c: (b, c, 0))
    return pl.pallas_call(
        body, name=name, grid=(bsz, nc),
        in_specs=[seg(OFF_U), seg(OFF_V), out_seg] + par,
        out_specs=[out_seg, out_seg] + par,
        out_shape=[jax.ShapeDtypeStruct((bsz, s, GMW), BF16)] * 2 + [jax.ShapeDtypeStruct(x.shape, F32) for x in prm],
        compiler_params=_cp(("arbitrary", "arbitrary")),
    )(p, p, dmix, *prm)


def _attn_block(q, kp, kc, vp, vc, sinks, out_g, has_prev):
    w = q.shape[0]
    k2 = jnp.concatenate([kp, kc], axis=0)
    v2 = jnp.concatenate([vp, vc], axis=0)
    qi = lax.broadcasted_iota(jnp.int32, (w, 2 * w), 0)
    kj = lax.broadcasted_iota(jnp.int32, (w, 2 * w), 1)
    diff = qi + w - kj
    grp = ATT_HEADS // ATT_KV
    valid = (diff >= 0) & (diff < w) & ((kj >= w) | has_prev)
    valid = jnp.concatenate([valid] * grp, axis=0)
    outs = []
    for kv in range(ATT_KV):
        kh = k2[:, kv * ATT_HD:(kv + 1) * ATT_HD]
        vh = v2[:, kv * ATT_HD:(kv + 1) * ATT_HD]
        heads = range(kv * grp, (kv + 1) * grp)
        qs = jnp.concatenate([q[:, h * ATT_HD:(h + 1) * ATT_HD] for h in heads], axis=0)
        sink = jnp.concatenate([jnp.broadcast_to(sinks[:, h:h + 1], (w, 1)) for h in heads], axis=0)
        sc = jnp.where(valid, _nt(qs, kh) * (ATT_HD ** -0.5), NEG)
        m = jnp.maximum(jnp.max(sc, axis=-1, keepdims=True), sink)
        e = jnp.exp(sc - m)
        pr = e / (jnp.sum(e, axis=-1, keepdims=True) + jnp.exp(sink - m))
        o = _nn(pr, vh)
        outs += [o[gi * w:(gi + 1) * w] for gi in range(grp)]
    return _rms(jnp.concatenate(outs, axis=1), out_g)


def attn_fwd(p, sinks, out_g, name):
    bsz, s, _ = p.shape
    nb = s // WINDOW

    def body(q_ref, kp_ref, kc_ref, vp_ref, vc_ref, s_ref, g_ref, o_ref):
        o_ref[...] = _attn_block(q_ref[...], kp_ref[...], kc_ref[...], vp_ref[...], vc_ref[...], s_ref[...],
                                 g_ref[...], pl.program_id(1) > 0).astype(o_ref.dtype)

    cur = lambda off: pl.BlockSpec((None, WINDOW, KVW), lambda b, n: (b, n, off // KVW))
    prev = lambda off: pl.BlockSpec((None, WINDOW, KVW), lambda b, n: (b, jnp.maximum(n - 1, 0), off // KVW))
    return pl.pallas_call(
        body, name=name, grid=(bsz, nb),
        in_specs=[pl.BlockSpec((None, WINDOW, ATW), lambda b, n: (b, n, OFF_Q // ATW)),
                  prev(OFF_K), cur(OFF_K), prev(OFF_VV), cur(OFF_VV),
                  pl.BlockSpec((1, ATT_HEADS), lambda b, n: (0, 0)), pl.BlockSpec((1, ATW), lambda b, n: (0, 0))],
        out_specs=pl.BlockSpec((None, WINDOW, ATW), lambda b, n: (b, n, 0)),
        out_shape=jax.ShapeDtypeStruct((bsz, s, ATW), BF16),
        compiler_params=_cp(("parallel", "parallel")),
    )(p, p, p, p, p, sinks, out_g)


def attn_bwd(p, dmix, sinks, out_g, name):
    bsz, s, _ = p.shape
    nb = s // WINDOW

    def body(q_ref, kp_ref, kc_ref, vp_ref, vc_ref, do_ref, s_ref, g_ref,
             dq_ref, dk_ref, dv_ref, ds_ref, dg_ref, ck, cv):
        b, n = pl.program_id(0), pl.program_id(1)

        @pl.when(n == 0)
        def _():
            ck[...] = jnp.zeros_like(ck)
            cv[...] = jnp.zeros_like(cv)

        @pl.when(n < nb)
        def _():
            fn = functools.partial(_attn_block, has_prev=n > 0)
            _, vjp = jax.vjp(fn, q_ref[...], kp_ref[...], kc_ref[...], vp_ref[...], vc_ref[...], s_ref[...], g_ref[...])
            dq, dkp, dkc, dvp, dvc, dsk, dgg = vjp(do_ref[...])
            dq_ref[...] = dq.astype(dq_ref.dtype)
            dk_ref[...] = (ck[...] + dkp).astype(dk_ref.dtype)
            dv_ref[...] = (cv[...] + dvp).astype(dv_ref.dtype)
            ck[...] = dkc
            cv[...] = dvc
            _accumulate((b == 0) & (n == 0), (ds_ref, dg_ref), (dsk, dgg))

        @pl.when(n == nb)
        def _():
            dk_ref[...] = ck[...].astype(dk_ref.dtype)
            dv_ref[...] = cv[...].astype(dv_ref.dtype)

    at = lambda n: jnp.minimum(n, nb - 1)
    cur = lambda off: pl.BlockSpec((None, WINDOW, KVW), lambda b, n: (b, at(n), off // KVW))
    prev = lambda off: pl.BlockSpec((None, WINDOW, KVW), lambda b, n: (b, jnp.maximum(at(n) - 1, 0), off // KVW))
    kv_out = pl.BlockSpec((None, WINDOW, KVW), lambda b, n: (b, jnp.maximum(n - 1, 0), 0))
    return pl.pallas_call(
        body, name=name, grid=(bsz, nb + 1),
        in_specs=[pl.BlockSpec((None, WINDOW, ATW), lambda b, n: (b, at(n), OFF_Q // ATW)),
                  prev(OFF_K), cur(OFF_K), prev(OFF_VV), cur(OFF_VV),
                  pl.BlockSpec((None, WINDOW, ATW), lambda b, n: (b, at(n), GMW // ATW)),
                  pl.BlockSpec((1, ATT_HEADS), lambda b, n: (0, 0)), pl.BlockSpec((1, ATW), lambda b, n: (0, 0))],
        out_specs=[pl.BlockSpec((None, WINDOW, ATW), lambda b, n: (b, at(n), 0)), kv_out, kv_out,
                   pl.BlockSpec((1, ATT_HEADS), lambda b, n: (0, 0)), pl.BlockSpec((1, ATW), lambda b, n: (0, 0))],
        out_shape=[jax.ShapeDtypeStruct((bsz, s, ATW), BF16), jax.ShapeDtypeStruct((bsz, s, KVW), BF16),
                   jax.ShapeDtypeStruct((bsz, s, KVW), BF16), jax.ShapeDtypeStruct((1, ATT_HEADS), F32),
                   jax.ShapeDtypeStruct((1, ATW), F32)],
        scratch_shapes=[pltpu.VMEM((WINDOW, KVW), F32), pltpu.VMEM((WINDOW, KVW), F32)],
        compiler_params=_cp(("arbitrary", "arbitrary")),
    )(p, p, p, p, p, dmix, sinks, out_g)


CONV_CT = 256


def _shift_down(x, j):
    if j == 0:
        return x
    rows = lax.broadcasted_iota(jnp.int32, x.shape, 0)
    return jnp.where(rows >= j, pltpu.roll(x, j, 0), 0.0)


def _shift_up(x, j):
    if j == 0:
        return x
    s = x.shape[0]
    rows = lax.broadcasted_iota(jnp.int32, x.shape, 0)
    return jnp.where(rows < s - j, pltpu.roll(x, s - j, 0), 0.0)


def conv_fwd(p, w, bias, name):
    bsz, s, _ = p.shape

    def body(x_ref, w_ref, b_ref, o_ref):
        xv, wv = x_ref[...], w_ref[...]
        pre = b_ref[...] + sum(wv[k:k + 1, :] * _shift_down(xv, CONV_K - 1 - k) for k in range(CONV_K))
        o_ref[...] = _silu(pre)

    blk = pl.BlockSpec((None, s, CONV_CT), lambda b, j: (b, 0, j))
    return pl.pallas_call(
        body, name=name, grid=(bsz, CCH // CONV_CT),
        in_specs=[blk, pl.BlockSpec((CONV_K, CONV_CT), lambda b, j: (0, j)), pl.BlockSpec((1, CONV_CT), lambda b, j: (0, j))],
        out_specs=blk, out_shape=jax.ShapeDtypeStruct((bsz, s, CCH), F32),
        compiler_params=_cp(("parallel", "parallel")),
    )(p, w, bias)


def conv_bwd(p, dxc, w, bias, name):
    bsz, s, _ = p.shape

    def body(x_ref, d_ref, w_ref, b_ref, dx_ref, dw_ref, db_ref):
        b = pl.program_id(1)
        xv, wv = x_ref[...], w_ref[...]
        xs = [_shift_down(xv, CONV_K - 1 - k) for k in range(CONV_K)]
        pre = b_ref[...] + sum(wv[k:k + 1, :] * xs[k] for k in range(CONV_K))
        sg = 1.0 / (1.0 + jnp.exp(-pre))
        dpre = d_ref[...] * (sg * (1.0 + pre * (1.0 - sg)))
        dx_ref[...] = sum(wv[k:k + 1, :] * _shift_up(dpre, CONV_K - 1 - k) for k in range(CONV_K)).astype(dx_ref.dtype)
        p_w = jnp.concatenate([jnp.sum(dpre * xs[k], axis=0, keepdims=True) for k in range(CONV_K)], axis=0)
        p_b = jnp.sum(dpre, axis=0, keepdims=True)
        _accumulate(b == 0, (dw_ref, db_ref), (p_w, p_b))

    blk = pl.BlockSpec((None, s, CONV_CT), lambda j, b: (b, 0, j))
    wsp = pl.BlockSpec((CONV_K, CONV_CT), lambda j, b: (0, j))
    bsp = pl.BlockSpec((1, CONV_CT), lambda j, b: (0, j))
    return pl.pallas_call(
        body, name=name, grid=(CCH // CONV_CT, bsz),
        in_specs=[blk, blk, wsp, bsp], out_specs=[blk, wsp, bsp],
        out_shape=[jax.ShapeDtypeStruct((bsz, s, CCH), BF16), jax.ShapeDtypeStruct((CONV_K, CCH), F32),
                   jax.ShapeDtypeStruct((1, CCH), F32)],
        compiler_params=_cp(("parallel", "arbitrary")),
    )(p, dxc, w, bias)


def _ssd_consts():
    c = CHUNK
    r = lax.broadcasted_iota(jnp.int32, (c, c), 0)
    q = lax.broadcasted_iota(jnp.int32, (c, c), 1)
    hrow = lax.broadcasted_iota(jnp.int32, (LANES, SSW), 0)
    hcol = lax.broadcasted_iota(jnp.int32, (LANES, SSW), 1) // SSM_HD
    expand = (hrow == hcol).astype(F32)
    return expand, (r >= q).astype(F32), (r <= q).astype(F32), r >= q


def _ssd_chunk(xc, dtr, z, prev_t, dt_bias, a_log, d_skip, norm_g):
    c = xc.shape[0]
    expand, tril1, triu1, causal = _ssd_consts()
    xs, bm, cm = xc[:, :SSW], xc[:, SSW:SSW + BCW], xc[:, SSW + BCW:]
    dt = _softplus(dtr + dt_bias)
    da = dt * (-jnp.exp(a_log))
    a_cs = _hdot(tril1, da)
    a_cs_t = _hdot(da.T, triu1)
    dt_e = _hdot(dt, expand)
    acs_e = _hdot(a_cs, expand)
    alast_e = acs_e[c - 1:c, :]
    dsk_e = _hdot(jnp.broadcast_to(d_skip, (8, LANES)), expand)[0:1, :]
    xdt = xs * dt_e
    hg = SSM_HEADS // SSM_GROUPS
    ys, new_t = [], []
    for g in range(SSM_GROUPS):
        bg = bm[:, g * SSM_STATE:(g + 1) * SSM_STATE]
        cg = cm[:, g * SSM_STATE:(g + 1) * SSM_STATE]
        sl = slice(g * GRW, (g + 1) * GRW)
        cb = _nt(cg, bg)
        xdt_g = xdt[:, sl]
        st = _tn(bg, xdt_g * jnp.exp(alast_e[:, sl] - acs_e[:, sl]))
        new_t.append(prev_t[:, sl] * jnp.exp(alast_e[:, sl]) + st)
        y_off = _nn(cg, prev_t[:, sl]) * jnp.exp(acs_e[:, sl])
        yd = []
        for hh in range(hg):
            h = g * hg + hh
            decay = jnp.exp(jnp.where(causal, a_cs[:, h:h + 1] - a_cs_t[h:h + 1, :], NEG))
            yd.append(_nn(cb * decay, xdt_g[:, hh * SSM_HD:(hh + 1) * SSM_HD]))
        ys.append(jnp.concatenate(yd, axis=1) + y_off)
    y = (jnp.concatenate(ys, axis=1) + xs * dsk_e) * _silu(z)
    yn = [y[:, g * GRW:(g + 1) * GRW] * lax.rsqrt(jnp.mean(jnp.square(y[:, g * GRW:(g + 1) * GRW]), axis=-1, keepdims=True) + EPS)
          for g in range(SSM_GROUPS)]
    return jnp.concatenate(yn, axis=1) * norm_g, jnp.concatenate(new_t, axis=1)


def ssd_fwd(xc, p, prm, name):
    bsz, s, _ = p.shape
    nc = s // CHUNK

    def body(xc_ref, dt_ref, z_ref, db, al, dk, ng, o_ref, st_ref, state):
        @pl.when(pl.program_id(1) == 0)
        def _():
            state[...] = jnp.zeros_like(state)

        prev = state[...]
        st_ref[...] = prev
        out, new = _ssd_chunk(xc_ref[...], dt_ref[...], z_ref[...], prev, db[...], al[...], dk[...], ng[...])
        o_ref[...] = out.astype(o_ref.dtype)
        state[...] = new

    vec = pl.BlockSpec((1, LANES), lambda b, c: (0, 0))
    return pl.pallas_call(
        body, name=name, grid=(bsz, nc),
        in_specs=[pl.BlockSpec((None, CHUNK, CCH), lambda b, c: (b, c, 0)),
                  pl.BlockSpec((None, CHUNK, LANES), lambda b, c: (b, c, OFF_DT // LANES)),
                  pl.BlockSpec((None, CHUNK, SSW), lambda b, c: (b, c, OFF_Z // SSW)),
                  vec, vec, vec, pl.BlockSpec((1, SSW), lambda b, c: (0, 0))],
        out_specs=[pl.BlockSpec((None, CHUNK, SSW), lambda b, c: (b, c, 0)),
                   pl.BlockSpec((None, None, SSM_STATE, SSW), lambda b, c: (b, c, 0, 0))],
        out_shape=[jax.ShapeDtypeStruct((bsz, s, SSW), BF16), jax.ShapeDtypeStruct((bsz, nc, SSM_STATE, SSW), F32)],
        scratch_shapes=[pltpu.VMEM((SSM_STATE, SSW), F32)],
        compiler_params=_cp(("parallel", "arbitrary")),
    )(xc, p, p, *prm)


def ssd_bwd(xc, p, states, dmix, prm, name):
    bsz, s, _ = p.shape
    nc = s // CHUNK

    def body(xc_ref, dt_ref, z_ref, st_ref, do_ref, db, al, dk, ng, dxc_ref, ddt_ref, dz_ref, *rest):
        dpar, dstate = rest[:4], rest[4]
        b, c = pl.program_id(0), pl.program_id(1)

        @pl.when(c == 0)
        def _():
            dstate[...] = jnp.zeros_like(dstate)

        _, vjp = jax.vjp(_ssd_chunk, xc_ref[...], dt_ref[...], z_ref[...], st_ref[...], db[...], al[...], dk[...], ng[...])
        gr = vjp((do_ref[...], dstate[...]))
        dxc_ref[...] = gr[0]
        ddt_ref[...] = gr[1].astype(ddt_ref.dtype)
        dz_ref[...] = gr[2].astype(dz_ref.dtype)
        dstate[...] = gr[3]
        _accumulate((b == 0) & (c == 0), dpar, gr[4:])

    rv = lambda c: nc - 1 - c
    vec = pl.BlockSpec((1, LANES), lambda b, c: (0, 0))
    ngs = pl.BlockSpec((1, SSW), lambda b, c: (0, 0))
    return pl.pallas_call(
        body, name=name, grid=(bsz, nc),
        in_specs=[pl.BlockSpec((None, CHUNK, CCH), lambda b, c: (b, rv(c), 0)),
                  pl.BlockSpec((None, CHUNK, LANES), lambda b, c: (b, rv(c), OFF_DT // LANES)),
                  pl.BlockSpec((None, CHUNK, SSW), lambda b, c: (b, rv(c), OFF_Z // SSW)),
                  pl.BlockSpec((None, None, SSM_STATE, SSW), lambda b, c: (b, rv(c), 0, 0)),
                  pl.BlockSpec((None, CHUNK, SSW), lambda b, c: (b, rv(c), (GMW + ATW) // SSW)),
                  vec, vec, vec, ngs],
        out_specs=[pl.BlockSpec((None, CHUNK, CCH), lambda b, c: (b, rv(c), 0)),
                   pl.BlockSpec((None, CHUNK, LANES), lambda b, c: (b, rv(c), 0)),
                   pl.BlockSpec((None, CHUNK, SSW), lambda b, c: (b, rv(c), 0)),
                   vec, vec, vec, ngs],
        out_shape=[jax.ShapeDtypeStruct((bsz, s, CCH), F32), jax.ShapeDtypeStruct((bsz, s, LANES), BF16),
                   jax.ShapeDtypeStruct((bsz, s, SSW), BF16)] + [jax.ShapeDtypeStruct((1, LANES), F32)] * 3
                  + [jax.ShapeDtypeStruct((1, SSW), F32)],
        scratch_shapes=[pltpu.VMEM((SSM_STATE, SSW), F32)],
        compiler_params=_cp(("arbitrary", "arbitrary")),
    )(xc, p, p, states, dmix, *prm)


def _rows2d(a):
    return a.reshape(-1, a.shape[-1])


def _ew_tile(r, c):
    t = r
    while t * c > (1 << 20) and t % 16 == 0:
        t //= 2
    return t


def add_pair(a, b, name):
    k, h, c = a.shape
    tr = _ew_tile(h, c)

    def body(a_ref, b_ref, o_ref, ob_ref):
        s = a_ref[...] + b_ref[...]
        o_ref[...] = s
        ob_ref[...] = s.astype(ob_ref.dtype)

    blk = pl.BlockSpec((None, tr, c), lambda kk, i: (kk, i, 0))
    return pl.pallas_call(
        body, name=name, grid=(k, h // tr), in_specs=[blk, blk], out_specs=[blk, blk],
        out_shape=[jax.ShapeDtypeStruct(a.shape, F32), jax.ShapeDtypeStruct(a.shape, BF16)],
        compiler_params=_cp(("parallel", "parallel")),
    )(a, b)


def sum_own_recv(own, recv, name):
    h, c = own.shape
    tr = _ew_tile(h, c)

    def body(o_ref, r_ref, out_ref):
        s = o_ref[...]
        for j in range(3):
            s = s + r_ref[j].astype(F32)
        out_ref[...] = s

    return pl.pallas_call(
        body, name=name, grid=(h // tr,),
        in_specs=[pl.BlockSpec((tr, c), lambda i: (i, 0)), pl.BlockSpec((3, tr, c), lambda i: (0, i, 0))],
        out_specs=pl.BlockSpec((tr, c), lambda i: (i, 0)),
        out_shape=jax.ShapeDtypeStruct((h, c), F32),
        compiler_params=_cp(("parallel",)),
    )(own, recv)


def sum_devices(parts, name):
    n, r, c = parts.shape
    tr = _ew_tile(r, c * n)

    def body(p_ref, o_ref):
        s = p_ref[0]
        for j in range(1, n):
            s = s + p_ref[j]
        o_ref[...] = s

    return pl.pallas_call(
        body, name=name, grid=(r // tr,),
        in_specs=[pl.BlockSpec((n, tr, c), lambda i: (0, i, 0))],
        out_specs=pl.BlockSpec((tr, c), lambda i: (i, 0)),
        out_shape=jax.ShapeDtypeStruct((r, c), F32),
        compiler_params=_cp(("parallel",)),
    )(parts)


def adamw(w, m, v, g, name):
    r, c = w.shape
    tr = _ew_tile(r, c * 2)

    def body(w_ref, m_ref, v_ref, g_ref, d_ref, mo_ref, vo_ref):
        gv = g_ref[...]
        mn = ADAM_B1 * m_ref[...] + (1.0 - ADAM_B1) * gv
        vn = ADAM_B2 * v_ref[...] + (1.0 - ADAM_B2) * (gv * gv)
        mh = mn / (1.0 - ADAM_B1 ** ADAM_STEP)
        vh = vn / (1.0 - ADAM_B2 ** ADAM_STEP)
        d_ref[...] = -ADAM_LR * (mh / (jnp.sqrt(vh) + ADAM_EPS) + ADAM_WD * w_ref[...])
        mo_ref[...] = mn
        vo_ref[...] = vn

    blk = pl.BlockSpec((tr, c), lambda i: (i, 0))
    return pl.pallas_call(
        body, name=name, grid=(r // tr,), in_specs=[blk] * 4, out_specs=[blk] * 3,
        out_shape=[jax.ShapeDtypeStruct((r, c), F32)] * 3,
        compiler_params=_cp(("parallel",)),
    )(w, m, v, g)


def _place():
    x, y, c = lax.axis_index("x"), lax.axis_index("y"), lax.axis_index("c")
    chips = [(1 - x, y), (x, 1 - y), (1 - x, 1 - y)]
    return x, y, c, chips


def all_gather_small(v, name):
    r, w = v.shape

    def body(x_ref, out_ref, send_sems, recv_sems, local_sem):
        x, y, c, chips = _place()
        me, sibling = (x, y, c), (x, y, 1 - c)

        def rows(px, py, pc):
            return out_ref.at[pl.ds((4 * px + 2 * py + pc) * r, r), :]

        def copy(k, block, to, src=None):
            return pltpu.make_async_remote_copy(
                src_ref=rows(*block) if src is None else src, dst_ref=rows(*block),
                send_sem=send_sems.at[k], recv_sem=recv_sems.at[k], device_id=to, device_id_type=MESH)

        mine = pltpu.make_async_copy(x_ref, rows(*me), local_sem)
        mine.start()
        first = [copy(0, me, sibling, src=x_ref)]
        first += [copy(1 + j, me, (*chip, c), src=x_ref) for j, chip in enumerate(chips)]
        for cp in first:
            cp.start()
        passed = [copy(4 + j, (*chip, c), sibling) for j, chip in enumerate(chips)]
        for j, chip in enumerate(chips):
            copy(1 + j, (*chip, c), me).wait_recv()
            passed[j].start()
        copy(0, sibling, me).wait_recv()
        for j, chip in enumerate(chips):
            copy(4 + j, (*chip, 1 - c), me).wait_recv()
        for cp in first + passed:
            cp.wait_send()
        mine.wait()

    out = pl.pallas_call(
        body, name=name, out_shape=jax.ShapeDtypeStruct((8 * r, w), v.dtype),
        in_specs=[pl.BlockSpec(memory_space=pltpu.VMEM)], out_specs=pl.BlockSpec(memory_space=pltpu.VMEM),
        scratch_shapes=[pltpu.SemaphoreType.DMA((7,)), pltpu.SemaphoreType.DMA((7,)), pltpu.SemaphoreType.DMA],
        compiler_params=pltpu.CompilerParams(vmem_limit_bytes=VMEM_LIMIT),
    )(v)
    return out.reshape(8, r, w)


_HBM = pl.BlockSpec(memory_space=pltpu.HBM)


def gather_weights(shards, name):
    n = len(shards)

    def body(*refs):
        ins, outs = refs[:n], refs[n:2 * n]
        send_sems, recv_sems = refs[2 * n:]
        x, y, c, chips = _place()
        sibling = (x, y, 1 - c)
        kme = 2 * x + y

        def half(i, k, hc):
            h = ins[i].shape[0] // 2
            return outs[i].at[k, pl.ds(hc * h, h), :]

        def copy(i, s, k, hc, to, src=None):
            return pltpu.make_async_remote_copy(
                src_ref=half(i, k, hc) if src is None else src, dst_ref=half(i, k, hc),
                send_sem=send_sems.at[6 * i + s], recv_sem=recv_sems.at[6 * i + s], device_id=to, device_id_type=MESH)

        sent = []
        for i in range(n):
            h = ins[i].shape[0] // 2
            for j, chip in enumerate(chips):
                sent.append(copy(i, j, kme, c, (*chip, c), src=ins[i].at[pl.ds(c * h, h), :]))
                sent[-1].start()
        for i in range(n):
            for j, (cx, cy) in enumerate(chips):
                copy(i, j, 2 * cx + cy, c, sibling).wait_recv()
                sent.append(copy(i, 3 + j, 2 * cx + cy, c, sibling))
                sent[-1].start()
        for i in range(n):
            for j, (cx, cy) in enumerate(chips):
                copy(i, 3 + j, 2 * cx + cy, 1 - c, sibling).wait_recv()
        for cp in sent:
            cp.wait_send()

    got = pl.pallas_call(
        body, name=name, out_shape=[jax.ShapeDtypeStruct((4,) + s.shape, s.dtype) for s in shards],
        in_specs=[_HBM] * n, out_specs=[_HBM] * n,
        scratch_shapes=[pltpu.SemaphoreType.DMA((6 * n,)), pltpu.SemaphoreType.DMA((6 * n,))],
    )(*shards)
    kme = 2 * lax.axis_index("x") + lax.axis_index("y")
    return [lax.dynamic_update_slice(g, s[None], (kme, 0, 0)) for g, s in zip(got, shards)]


def swap_halves(grads, name):
    n = len(grads)

    def body(*refs):
        ins, theirs = refs[:n], refs[n:2 * n]
        send_sems, recv_sems = refs[2 * n:]
        x, y, c, _ = _place()
        sibling = (x, y, 1 - c)
        sent = []
        for i in range(n):
            h = ins[i].shape[1] // 2
            sent.append(pltpu.make_async_remote_copy(
                src_ref=ins[i].at[:, pl.ds((1 - c) * h, h), :], dst_ref=theirs[i],
                send_sem=send_sems.at[i], recv_sem=recv_sems.at[i], device_id=sibling, device_id_type=MESH))
            sent[-1].start()
        for cp in sent:
            cp.wait_recv()
        for cp in sent:
            cp.wait_send()

    half = [jax.ShapeDtypeStruct((4, g.shape[1] // 2, g.shape[2]), g.dtype) for g in grads]
    theirs = pl.pallas_call(
        body, name=name, out_shape=half, in_specs=[_HBM] * n, out_specs=[_HBM] * n,
        scratch_shapes=[pltpu.SemaphoreType.DMA((n,)), pltpu.SemaphoreType.DMA((n,))],
    )(*grads)
    ci = lax.axis_index("c")
    mine = [lax.dynamic_slice_in_dim(g, ci * (g.shape[1] // 2), g.shape[1] // 2, axis=1) for g in grads]
    return mine, theirs


def scatter_to_chips(sums_f32, sums_bf16, name):
    n = len(sums_f32)

    def body(*refs):
        sb, got = refs[:n], refs[n:2 * n]
        send_sems, recv_sems = refs[2 * n:]
        x, y, c, chips = _place()
        sent = []
        for i in range(n):
            for j, (cx, cy) in enumerate(chips):
                sent.append(pltpu.make_async_remote_copy(
                    src_ref=sb[i].at[2 * cx + cy], dst_ref=got[i].at[j],
                    send_sem=send_sems.at[3 * i + j], recv_sem=recv_sems.at[3 * i + j],
                    device_id=(cx, cy, c), device_id_type=MESH))
                sent[-1].start()
        for cp in sent:
            cp.wait_recv()
        for cp in sent:
            cp.wait_send()

    got = pl.pallas_call(
        body, name=name, out_shape=[jax.ShapeDtypeStruct((3,) + s.shape[1:], BF16) for s in sums_f32],
        in_specs=[_HBM] * n, out_specs=[_HBM] * n,
        scratch_shapes=[pltpu.SemaphoreType.DMA((3 * n,)), pltpu.SemaphoreType.DMA((3 * n,))],
    )(*sums_bf16)
    kme = 2 * lax.axis_index("x") + lax.axis_index("y")
    own = [lax.dynamic_index_in_dim(s, kme, axis=0, keepdims=False) for s in sums_f32]
    return own, got


def join_halves(halves, name):
    n = len(halves)

    def body(*refs):
        ins, outs = refs[:n], refs[n:2 * n]
        send_sems, recv_sems = refs[2 * n:]
        x, y, c, _ = _place()
        sibling = (x, y, 1 - c)
        sent = []
        for i in range(n):
            h = ins[i].shape[0]
            sent.append(pltpu.make_async_remote_copy(
                src_ref=ins[i], dst_ref=outs[i].at[pl.ds(c * h, h), :], send_sem=send_sems.at[i],
                recv_sem=recv_sems.at[i], device_id=sibling, device_id_type=MESH))
            sent[-1].start()
        for i in range(n):
            h = ins[i].shape[0]
            pltpu.make_async_remote_copy(
                src_ref=ins[i], dst_ref=outs[i].at[pl.ds((1 - c) * h, h), :], send_sem=send_sems.at[i],
                recv_sem=recv_sems.at[i], device_id=sibling, device_id_type=MESH).wait_recv()
        for cp in sent:
            cp.wait_send()

    full = pl.pallas_call(
        body, name=name, out_shape=[jax.ShapeDtypeStruct((2 * s.shape[0], s.shape[1]), F32) for s in halves],
        in_specs=[_HBM] * n, out_specs=[_HBM] * n,
        scratch_shapes=[pltpu.SemaphoreType.DMA((n,)), pltpu.SemaphoreType.DMA((n,))],
    )(*halves)
    ci = lax.axis_index("c")
    return [lax.dynamic_update_slice(f, s, (ci * s.shape[0], 0)) for f, s in zip(full, halves)]


def reduce_scatter(grads, tag):
    n = len(grads)
    mine, theirs = swap_halves(grads, f"rs_swap_{tag}")
    sums = [add_pair(a, b, f"rs_add_{tag}_{i}") for i, (a, b) in enumerate(zip(mine, theirs))]
    own, got = scatter_to_chips([s[0] for s in sums], [s[1] for s in sums], f"rs_scatter_{tag}")
    halves = [sum_own_recv(o, g, f"rs_sum_{tag}_{i}") for i, (o, g) in enumerate(zip(own, got))]
    return join_halves(halves, f"rs_join_{tag}")


_PACK_ROWS = 8 * LANES


def _pack(arrs):
    flat = jnp.concatenate([a.reshape(-1).astype(F32) for a in arrs])
    pad = (-flat.shape[0]) % _PACK_ROWS
    return jnp.pad(flat, (0, pad)).reshape(-1, LANES)


def _unpack(flat, shapes):
    flat = flat.reshape(-1)
    out, off = [], 0
    for s in shapes:
        n = int(np.prod(s))
        out.append(flat[off:off + n].reshape(s))
        off += n
    return out


_SEGS = [(0, OFF_U, GMW), (GMW, OFF_V, GMW), (2 * GMW, OFF_Q, ATW), (2 * GMW + ATW, OFF_K, KVW),
         (2 * GMW + ATW + KVW, OFF_VV, KVW), (2 * GMW + ATW + 2 * KVW, OFF_Z, SSW),
         (2 * GMW + ATW + 2 * KVW + SSW, OFF_XBC, CCH), (IN_W - SSM_HEADS, OFF_DT, SSM_HEADS)]


def _win_to_kernel_layout(w):
    out = jnp.zeros((w.shape[0], PW), w.dtype)
    for src, dst, wd in _SEGS:
        out = lax.dynamic_update_slice(out, w[:, src:src + wd], (0, dst))
    return out


def _win_from_kernel_layout(w):
    return jnp.concatenate([w[:, dst:dst + wd] for _, dst, wd in _SEGS], axis=1)


def _relu2(a):
    r = jnp.maximum(a, 0)
    return r * r


def kernel(x, c, ada_w, ada_b, norm1_g, w_in, gm_ln_g, gm_ln_b, gm_ws, gm_bs, gm_norm_g, attn_sinks, attn_norm_g, conv_w, conv_b, dt_bias, a_log, d_skip, ssm_norm_g, w_out, norm2_g, w_mlp1, w_mlp2, final_norm_g, loss_target, m_ada_w, m_ada_b, m_norm1_g, m_w_in, m_gm_ln_g, m_gm_ln_b, m_gm_ws, m_gm_bs, m_gm_norm_g, m_attn_sinks, m_attn_norm_g, m_conv_w, m_conv_b, m_dt_bias, m_a_log, m_d_skip, m_ssm_norm_g, m_w_out, m_norm2_g, m_w_mlp1, m_w_mlp2, m_final_norm_g, v_ada_w, v_ada_b, v_norm1_g, v_w_in, v_gm_ln_g, v_gm_ln_b, v_gm_ws, v_gm_bs, v_gm_norm_g, v_attn_sinks, v_attn_norm_g, v_conv_w, v_conv_b, v_dt_bias, v_a_log, v_d_skip, v_ssm_norm_g, v_w_out, v_norm2_g, v_w_mlp1, v_w_mlp2, v_final_norm_g):
    nl = ada_w.shape[0]
    bl, s, d = x.shape
    t = bl * s
    dff4 = w_mlp1.shape[2]
    dff = 4 * dff4
    mod_w = ada_w.shape[2]
    cw_w = conv_w.shape[2]
    xi, yi, ci = lax.axis_index("x"), lax.axis_index("y"), lax.axis_index("c")
    chip = 2 * xi + yi
    dev = 2 * chip + ci
    nex = 8 * bl

    g0 = all_gather_small(_pack([c, conv_w]), "ag_c")
    g0 = g0.reshape(8, -1)
    c_all = g0[:, :bl * d].reshape(nex, d)
    cw_parts = g0[0::2, bl * d:bl * d + conv_w.size].reshape(4, nl, CONV_K, cw_w)
    conv_w_full = cw_parts.transpose(1, 2, 0, 3).reshape(nl, CONV_K, CCH)

    def c_act(a):
        return _silu(a).astype(BF16)

    def to_bf16(a):
        return a.astype(BF16)

    mod_parts = []
    for l in range(nl):
        bias = lax.dynamic_slice(ada_b[l].reshape(1, -1), (0, chip * mod_w), (1, mod_w))
        mod_parts.append(_mm("nn", c_all, ada_w[l], dims=(nex, mod_w, d), tm=nex, tn=512, tk=d, out_dtypes=[F32],
                             name=f"mod_{l}", pro_a=c_act, pro_b=to_bf16,
                             extras=[(bias, pl.BlockSpec((1, 512), lambda i, j, kk: (0, j)))],
                             epi=lambda acc, bv: (acc + bv,))[0])
    g1 = all_gather_small(_pack(mod_parts), "ag_mod").reshape(8, -1)
    mod_all = g1[0::2, :nl * nex * mod_w].reshape(4, nl, nex, mod_w).transpose(1, 2, 0, 3).reshape(nl, nex, 4 * mod_w)
    mod = lax.dynamic_slice(mod_all, (0, dev * bl, 0), (nl, bl, 4 * mod_w))
    mods = [[mod[l, :, i * d:(i + 1) * d].reshape(bl, 1, d) for i in range(6)] for l in range(nl)]

    wfull = []
    for l in range(nl):
        gi, go, g1w, g2w = gather_weights(
            [w_in[l].astype(BF16), w_out[l].astype(BF16), w_mlp1[l].astype(BF16), w_mlp2[l].astype(BF16)], f"gather_w_{l}")
        win = _win_to_kernel_layout(gi.transpose(1, 0, 2).reshape(d, IN_W))
        wfull.append((win, go.reshape(-1, d), g1w, g2w.reshape(dff, d)))

    row = lambda a: a.reshape(1, -1)
    pad16 = lambda a: jnp.pad(a.reshape(1, -1), ((0, 0), (0, LANES - SSM_HEADS)))
    tm_res = min(1024, s)

    def residual(acc, xt, gt):
        return acc, xt + gt * acc

    def res_extras(xin, gate):
        return [(xin.reshape(t, d), pl.BlockSpec((tm_res, 512), lambda i, j, kk: (i, j))),
                (gate, pl.BlockSpec((None, 1, 512), lambda i, j, kk: (i * tm_res // s, 0, j)))]

    w1_blk = lambda tk, tn: pl.BlockSpec((None, tk, tn), lambda i, j, kk: (j // (dff4 // tn), kk, j % (dff4 // tn)))

    saved = []
    xcur = x
    for l in range(nl):
        sh1, sc1, gt1, sh2, sc2, gt2 = mods[l]
        win, wout, w1, w2 = wfull[l]
        prm_a = (row(gm_ln_g[l]), row(gm_ln_b[l]), gm_ws[l], gm_bs[l].T, row(gm_norm_g[l]))
        prm_b = (row(attn_sinks[l]), row(attn_norm_g[l]))
        prm_c = (pad16(dt_bias[l]), pad16(a_log[l]), pad16(d_skip[l]), row(ssm_norm_g[l]))
        h1 = ln_mod_fwd(xcur, row(norm1_g[l]), sc1, sh1, f"ln1_fwd_{l}")
        p = _mm("nn", h1.reshape(t, d), win, dims=(t, PW, d), tm=1024, tn=512, tk=d, out_dtypes=[F32],
                name=f"proj_in_{l}")[0].reshape(bl, s, PW)
        out_a = gmlp_fwd(p, prm_a, f"gmlp_fwd_{l}")
        out_b = attn_fwd(p, *prm_b, f"attn_fwd_{l}")
        xc = conv_fwd(p, conv_w_full[l], row(conv_b[l]), f"conv_fwd_{l}")
        out_c, states = ssd_fwd(xc, p, prm_c, f"ssd_fwd_{l}")
        mix = jnp.concatenate([out_a, out_b, out_c], axis=-1)
        mm1, x2 = _mm("nn", mix.reshape(t, d), wout, dims=(t, d, d), tm=tm_res, tn=512, tk=d, out_dtypes=[F32, F32],
                      name=f"proj_out_{l}", extras=res_extras(xcur, gt1), epi=residual)
        x2 = x2.reshape(bl, s, d)
        h2 = ln_mod_fwd(x2, row(norm2_g[l]), sc2, sh2, f"ln2_fwd_{l}")
        a1 = _mm("nn", h2.reshape(t, d), w1, dims=(t, dff, d), tm=1024, tn=512, tk=d, out_dtypes=[BF16],
                 name=f"mlp1_{l}", b_spec=w1_blk(d, 512))[0]
        mm2, x3 = _mm("nn", a1, w2, dims=(t, d, dff), tm=tm_res, tn=512, tk=2048, out_dtypes=[F32, F32],
                      name=f"mlp2_{l}", extras=res_extras(x2, gt2), epi=residual, pro_a=_relu2)
        x3 = x3.reshape(bl, s, d)
        saved.append((xcur, h1, p, xc, states, mix, mm1.reshape(bl, s, d), x2, h2, a1, mm2.reshape(bl, s, d),
                      prm_a, prm_b, prm_c))
        xcur = x3

    dx, d_final_g, loss_part = loss_head(xcur, row(final_norm_g), loss_target, "loss_head")
    loss = lax.psum(loss_part[0, 0], ("x", "y", "c"))

    small_parts = [None] * nl
    dmods = [None] * nl
    big_grads = [None] * nl
    for l in reversed(range(nl)):
        sh1, sc1, gt1, sh2, sc2, gt2 = mods[l]
        win, wout, w1, w2 = wfull[l]
        xin, h1, p, xc, states, mix, mm1, x2, h2, a1, mm2, prm_a, prm_b, prm_c = saved[l]
        dm2, dgt2 = gate_bwd(dx, mm2, gt2, f"gate2_bwd_{l}")
        dm2 = dm2.reshape(t, d)
        da1 = _mm("nt", dm2, w2, dims=(t, dff, d), tm=1024, tn=512, tk=d, out_dtypes=[BF16], name=f"mlp2_dx_{l}",
                  extras=[(a1, pl.BlockSpec((1024 if t >= 1024 else t, 512), lambda i, j, kk: (i, j)))],
                  epi=lambda acc, av: (acc * (2.0 * jnp.maximum(av, 0).astype(F32)),))[0]
        dw2 = _mm("tn", a1, dm2, dims=(dff, d, t), tm=512, tn=2048, tk=2048, out_dtypes=[F32], name=f"mlp2_dw_{l}",
                  pro_a=_relu2)[0]
        dw1 = _mm("tn", h2.reshape(t, d), da1, dims=(d, dff, t), tm=512, tn=dff4, tk=2048, out_dtypes=[F32],
                  name=f"mlp1_dw_{l}", out_shapes=[(4, d, dff4)],
                  out_specs=[pl.BlockSpec((None, 512, dff4), lambda i, j, kk: (j, i, 0))])[0]
        dh2 = _mm("nt", da1, w1, dims=(t, d, dff), tm=1024, tn=512, tk=2048, out_dtypes=[F32], name=f"mlp1_dx_{l}",
                  b_spec=pl.BlockSpec((None, 512, 2048 if dff4 >= 2048 else dff4),
                                      lambda i, j, kk: (kk // (dff4 // min(2048, dff4)), j, kk % (dff4 // min(2048, dff4)))))[0]
        dx2, dsc2, dsh2, dn2 = ln_mod_bwd(dh2.reshape(bl, s, d), x2, dx, row(norm2_g[l]), sc2, f"ln2_bwd_{l}")
        dm1, dgt1 = gate_bwd(dx2, mm1, gt1, f"gate1_bwd_{l}")
        dm1 = dm1.reshape(t, d)
        dmix = _mm("nt", dm1, wout, dims=(t, d, d), tm=1024, tn=512, tk=d, out_dtypes=[F32],
                   name=f"proj_out_dx_{l}")[0].reshape(bl, s, d)
        dwout = _mm("tn", mix.reshape(t, d), dm1, dims=(d, d, t), tm=512, tn=2048, tk=2048, out_dtypes=[F32],
                    name=f"proj_out_dw_{l}")[0]
        du, dv, dlg, dlb, dws, dbst, dgng = gmlp_bwd(p, dmix, prm_a, f"gmlp_bwd_{l}")
        dq, dk, dvv, dsinks, dang = attn_bwd(p, dmix, *prm_b, f"attn_bwd_{l}")
        dxc, ddt, dz, ddtb, dalog, ddsk, dsng = ssd_bwd(xc, p, states, dmix, prm_c, f"ssd_bwd_{l}")
        dxbc, dcw, dcb = conv_bwd(p, dxc, conv_w_full[l], row(conv_b[l]), f"conv_bwd_{l}")
        dp = jnp.concatenate([dxbc, dq, dz, du, dv, dk, dvv, ddt, jnp.zeros((bl, s, PW - OFF_DT - LANES), BF16)],
                             axis=-1).reshape(t, PW)
        dwin = _mm("tn", h1.reshape(t, d), dp, dims=(d, PW, t), tm=512, tn=PW // 3, tk=2048, out_dtypes=[F32],
                   name=f"proj_in_dw_{l}")[0]
        dh1 = _mm("nt", dp, win, dims=(t, d, PW), tm=1024, tn=512, tk=PW // 2, out_dtypes=[F32],
                  name=f"proj_in_dx_{l}")[0]
        dx, dsc1, dsh1, dn1 = ln_mod_bwd(dh1.reshape(bl, s, d), xin, dx2, row(norm1_g[l]), sc1, f"ln1_bwd_{l}")
        dmods[l] = jnp.concatenate([dsh1, dsc1, dgt1, dsh2, dsc2, dgt2], axis=-1).reshape(bl, 6 * d)
        small_parts[l] = [dn1, dlg, dlb, dws, dbst.T, dgng, dsinks, dang, dcw, dcb, ddtb[:, :SSM_HEADS],
                          dalog[:, :SSM_HEADS], ddsk[:, :SSM_HEADS], dsng, dn2]
        dwin_blocks = _win_from_kernel_layout(dwin).reshape(d, 4, IN_W // 4).transpose(1, 0, 2)
        big_grads[l] = [dwin_blocks, dwout.reshape(4, d // 4, d), dw1, dw2.reshape(4, dff4, d)]
    grad_x = dx

    reduced = [reduce_scatter(big_grads[l], str(l)) for l in range(nl)]
    big_out = []
    for i, (wt, mt, vt) in enumerate([(w_in, m_w_in, v_w_in), (w_out, m_w_out, v_w_out),
                                      (w_mlp1, m_w_mlp1, v_w_mlp1), (w_mlp2, m_w_mlp2, v_w_mlp2)]):
        g = jnp.stack([reduced[l][i] for l in range(nl)]).reshape(wt.shape)
        dl, mn, vn = adamw(_rows2d(wt), _rows2d(mt), _rows2d(vt), _rows2d(g), f"adamw_big_{i}")
        big_out.append((g, dl.reshape(wt.shape), mn.reshape(wt.shape), vn.reshape(wt.shape)))

    small_names = [norm1_g, gm_ln_g, gm_ln_b, gm_ws, gm_bs, gm_norm_g, attn_sinks, attn_norm_g, None, conv_b, dt_bias,
                   a_log, d_skip, ssm_norm_g, norm2_g]
    n_small = len(small_names)
    per_param = [jnp.stack([small_parts[l][i].reshape(-1) for l in range(nl)]) for i in range(n_small)]
    small_vec = _pack(per_param + [d_final_g])
    rs_small = small_vec.shape[0]
    dmod_local = jnp.stack(dmods, axis=1)
    g2 = all_gather_small(jnp.concatenate([small_vec, _pack([dmod_local])], axis=0), "ag_small")
    g_small = sum_devices(g2[:, :rs_small, :], "sum_small")
    dmod_all = g2[:, rs_small:, :].reshape(8, -1)[:, :bl * nl * 6 * d].reshape(nex, nl * 6 * d)
    g_ada_b = sum_devices(dmod_all.reshape(nex, -1, LANES), "sum_ada_b").reshape(nl, 6 * d)
    shapes = [(nl, int(np.prod(small_parts[0][i].shape))) for i in range(n_small)] + [(d,)]
    g_list = _unpack(g_small, shapes)
    g_conv_w = lax.dynamic_slice(g_list[8].reshape(nl, CONV_K, CCH), (0, 0, chip * cw_w), (nl, CONV_K, cw_w))

    g_ada_w = []
    for l in range(nl):
        dm_cols = lax.dynamic_slice(dmod_all.reshape(nex, nl, 6 * d)[:, l, :], (0, chip * mod_w), (nex, mod_w))
        g_ada_w.append(_mm("tn", c_all, dm_cols, dims=(d, mod_w, nex), tm=512, tn=512, tk=nex, out_dtypes=[F32],
                           name=f"ada_w_grad_{l}", pro_a=c_act, pro_b=to_bf16)[0])
    g_ada_w = jnp.stack(g_ada_w)
    d_ada_w, m_ada_w_n, v_ada_w_n = [a.reshape(ada_w.shape) for a in
                                     adamw(_rows2d(ada_w), _rows2d(m_ada_w), _rows2d(v_ada_w), _rows2d(g_ada_w), "adamw_ada_w")]

    smalls = {
        "ada_b": (ada_b, m_ada_b, v_ada_b, g_ada_b), "norm1_g": (norm1_g, m_norm1_g, v_norm1_g, g_list[0]),
        "gm_ln_g": (gm_ln_g, m_gm_ln_g, v_gm_ln_g, g_list[1]), "gm_ln_b": (gm_ln_b, m_gm_ln_b, v_gm_ln_b, g_list[2]),
        "gm_ws": (gm_ws, m_gm_ws, v_gm_ws, g_list[3]), "gm_bs": (gm_bs, m_gm_bs, v_gm_bs, g_list[4]),
        "gm_norm_g": (gm_norm_g, m_gm_norm_g, v_gm_norm_g, g_list[5]),
        "attn_sinks": (attn_sinks, m_attn_sinks, v_attn_sinks, g_list[6]),
        "attn_norm_g": (attn_norm_g, m_attn_norm_g, v_attn_norm_g, g_list[7]),
        "conv_w": (conv_w, m_conv_w, v_conv_w, g_conv_w), "conv_b": (conv_b, m_conv_b, v_conv_b, g_list[9]),
        "dt_bias": (dt_bias, m_dt_bias, v_dt_bias, g_list[10]), "a_log": (a_log, m_a_log, v_a_log, g_list[11]),
        "d_skip": (d_skip, m_d_skip, v_d_skip, g_list[12]),
        "ssm_norm_g": (ssm_norm_g, m_ssm_norm_g, v_ssm_norm_g, g_list[13]),
        "norm2_g": (norm2_g, m_norm2_g, v_norm2_g, g_list[14]),
        "final_norm_g": (final_norm_g, m_final_norm_g, v_final_norm_g, g_list[15]),
    }
    keys = list(smalls)
    wv, mv, vv_, gv = [_pack([smalls[k][i].reshape(smalls[k][0].shape) for k in keys]) for i in range(4)]
    sd_, sm_, sv_ = adamw(wv, mv, vv_, gv, "adamw_small")
    shp = [smalls[k][0].shape for k in keys]
    small_out = {k: (smalls[k][3].reshape(smalls[k][0].shape), a, b, cc)
                 for k, a, b, cc in zip(keys, _unpack(sd_, shp), _unpack(sm_, shp), _unpack(sv_, shp))}

    out = {"ada_w": (g_ada_w, d_ada_w, m_ada_w_n, v_ada_w_n), "w_in": big_out[0], "w_out": big_out[1],
           "w_mlp1": big_out[2], "w_mlp2": big_out[3], **small_out}
    order = ["ada_w", "ada_b", "norm1_g", "w_in", "gm_ln_g", "gm_ln_b", "gm_ws", "gm_bs", "gm_norm_g", "attn_sinks",
             "attn_norm_g", "conv_w", "conv_b", "dt_bias", "a_log", "d_skip", "ssm_norm_g", "w_out", "norm2_g",
             "w_mlp1", "w_mlp2", "final_norm_g"]
    return (loss, grad_x, *[out[k][0] for k in order], *[out[k][1] for k in order],
            *[out[k][2] for k in order], *[out[k][3] for k in order])
```

```python
import functools
import math

import jax
import jax.numpy as jnp
import numpy as np
from jax import lax
from jax.experimental import pallas as pl
from jax.experimental.pallas import tpu as pltpu

F32 = jnp.float32
BF16 = jnp.bfloat16
HI = lax.Precision.HIGHEST
MESH = pl.DeviceIdType.MESH

CHUNK = 128
GM_HEADS, GM_HD = 4, 128
ATT_HEADS, ATT_KV, ATT_HD = 8, 2, 64
WINDOW = 128
SSM_HEADS, SSM_HD, SSM_GROUPS, SSM_STATE, CONV_K = 16, 64, 2, 128, 4
EPS = 1e-6
LN_EPS = 1e-5
NEG = -1e30
LANES = 128

GMW = GM_HEADS * GM_HD
ATW = ATT_HEADS * ATT_HD
KVW = ATT_KV * ATT_HD
SSW = SSM_HEADS * SSM_HD
BCW = SSM_GROUPS * SSM_STATE
CCH = SSW + 2 * BCW
GRW = SSW // SSM_GROUPS
IN_SIZES = (GMW, GMW, ATW, KVW, KVW, SSW, CCH, SSM_HEADS)
IN_W = sum(IN_SIZES)
OFF_XBC, OFF_Q, OFF_Z, OFF_U, OFF_V, OFF_K, OFF_VV, OFF_DT = 0, 1536, 2048, 3072, 3584, 4096, 4224, 4352
PW = 4608

ADAM_LR, ADAM_B1, ADAM_B2, ADAM_EPS, ADAM_WD, ADAM_STEP = 0.001, 0.9, 0.999, 1e-08, 0.01, 10

VMEM_LIMIT = 56 * 1024 * 1024


def _cp(sem=None):
    return pltpu.CompilerParams(dimension_semantics=sem, vmem_limit_bytes=VMEM_LIMIT)


_DN = {"nn": (((1,), (0,)), ((), ())), "nt": (((1,), (1,)), ((), ())), "tn": (((0,), (0,)), ((), ()))}


def _dot(form, a, b):
    return lax.dot_general(a.astype(BF16), b.astype(BF16), _DN[form], preferred_element_type=F32)


@jax.custom_vjp
def _nn(a, b):
    return _dot("nn", a, b)


@jax.custom_vjp
def _nt(a, b):
    return _dot("nt", a, b)


@jax.custom_vjp
def _tn(a, b):
    return _dot("tn", a, b)


_nn.defvjp(lambda a, b: (_dot("nn", a, b), (a, b)), lambda r, g: (_dot("nt", g, r[1]), _dot("tn", r[0], g)))
_nt.defvjp(lambda a, b: (_dot("nt", a, b), (a, b)), lambda r, g: (_dot("nn", g, r[1]), _dot("tn", g, r[0])))
_tn.defvjp(lambda a, b: (_dot("tn", a, b), (a, b)), lambda r, g: (_dot("nt", r[1], g), _dot("nn", r[0], g)))


def _hdot(a, b):
    return jnp.dot(a, b, precision=HI, preferred_element_type=F32)


def _silu(x):
    return x * (1.0 / (1.0 + jnp.exp(-x)))


def _softplus(x):
    return jnp.maximum(x, 0.0) + jnp.log1p(jnp.exp(-jnp.abs(x)))


def _gelu(x):
    return 0.5 * x * (1.0 + jnp.tanh(math.sqrt(2.0 / math.pi) * (x + 0.044715 * (x * x * x))))


def _rms(y, g):
    return y * lax.rsqrt(jnp.mean(y * y, axis=-1, keepdims=True) + EPS) * g


def _mm(form, a, b, *, dims, tm, tn, tk, out_dtypes, name, a_spec=None, b_spec=None, out_specs=None,
        out_shapes=None, extras=(), epi=None, pro_a=None, pro_b=None):
    m, n, k = dims
    tm, tn, tk = min(tm, m), min(tn, n), min(tk, k)
    assert m % tm == 0 and n % tn == 0 and k % tk == 0, (name, dims, tm, tn, tk)
    nk = k // tk
    if a_spec is None:
        a_spec = (pl.BlockSpec((tk, tm), lambda i, j, kk: (kk, i)) if form == "tn"
                  else pl.BlockSpec((tm, tk), lambda i, j, kk: (i, kk)))
    if b_spec is None:
        b_spec = (pl.BlockSpec((tn, tk), lambda i, j, kk: (j, kk)) if form == "nt"
                  else pl.BlockSpec((tk, tn), lambda i, j, kk: (kk, j)))
    n_out = len(out_dtypes)
    if out_specs is None:
        out_specs = [pl.BlockSpec((tm, tn), lambda i, j, kk: (i, j))] * n_out
    if out_shapes is None:
        out_shapes = [(m, n)] * n_out
    ne = len(extras)

    def body(*refs):
        a_ref, b_ref = refs[0], refs[1]
        ex = refs[2:2 + ne]
        outs = refs[2 + ne:2 + ne + n_out]

        def write(val):
            res = epi(val, *[e[...] for e in ex]) if epi is not None else (val,)
            for o, r in zip(outs, res):
                o[...] = r.astype(o.dtype)

        av = a_ref[...]
        if pro_a is not None:
            av = pro_a(av)
        bv = b_ref[...]
        if pro_b is not None:
            bv = pro_b(bv)
        part = lax.dot_general(av, bv, _DN[form], preferred_element_type=F32)
        if nk == 1:
            write(part)
        else:
            acc = refs[-1]
            kk = pl.program_id(2)

            @pl.when(kk == 0)
            def _():
                acc[...] = part

            @pl.when(kk > 0)
            def _():
                acc[...] += part

            @pl.when(kk == nk - 1)
            def _():
                write(acc[...])

    res = pl.pallas_call(
        body, name=name, grid=(m // tm, n // tn, nk),
        in_specs=[a_spec, b_spec] + [s for _, s in extras],
        out_specs=out_specs,
        out_shape=[jax.ShapeDtypeStruct(s, d) for s, d in zip(out_shapes, out_dtypes)],
        scratch_shapes=[pltpu.VMEM((tm, tn), F32)] if nk > 1 else [],
        compiler_params=_cp(("parallel", "parallel", "arbitrary")),
    )(a, b, *[e for e, _ in extras])
    return res


def _row_tile(s):
    return min(512, s)


def ln_mod_fwd(x, g, sc, sh, name):
    bsz, s, d = x.shape
    ts = _row_tile(s)

    def body(x_ref, g_ref, sc_ref, sh_ref, o_ref):
        xv = x_ref[...]
        r = lax.rsqrt(jnp.mean(xv * xv, axis=-1, keepdims=True) + EPS)
        o_ref[...] = ((xv * r * g_ref[...]) * (1.0 + sc_ref[...]) + sh_ref[...]).astype(o_ref.dtype)

    row = pl.BlockSpec((None, ts, d), lambda b, i: (b, i, 0))
    vec = pl.BlockSpec((None, 1, d), lambda b, i: (b, 0, 0))
    return pl.pallas_call(
        body, name=name, grid=(bsz, s // ts),
        in_specs=[row, pl.BlockSpec((1, d), lambda b, i: (0, 0)), vec, vec],
        out_specs=row, out_shape=jax.ShapeDtypeStruct(x.shape, BF16),
        compiler_params=_cp(("parallel", "parallel")),
    )(x, g, sc, sh)


def ln_mod_bwd(dh, x, dres, g, sc, name):
    bsz, s, d = x.shape
    ts = _row_tile(s)

    def body(dh_ref, x_ref, dres_ref, g_ref, sc_ref, dx_ref, dsc_ref, dsh_ref, dg_ref):
        b, i = pl.program_id(0), pl.program_id(1)
        xv, dhv, gv = x_ref[...], dh_ref[...], g_ref[...]
        r = lax.rsqrt(jnp.mean(xv * xv, axis=-1, keepdims=True) + EPS)
        xn = xv * r
        a = dhv * (1.0 + sc_ref[...])
        dxn = a * gv
        dx_ref[...] = dres_ref[...] + r * (dxn - xn * jnp.mean(dxn * xn, axis=-1, keepdims=True))
        p_sc = jnp.sum(dhv * (xn * gv), axis=0, keepdims=True)
        p_sh = jnp.sum(dhv, axis=0, keepdims=True)
        p_g = jnp.sum(a * xn, axis=0, keepdims=True)

        @pl.when(i == 0)
        def _():
            dsc_ref[...] = p_sc
            dsh_ref[...] = p_sh

        @pl.when(i > 0)
        def _():
            dsc_ref[...] += p_sc
            dsh_ref[...] += p_sh

        @pl.when((i == 0) & (b == 0))
        def _():
            dg_ref[...] = p_g

        @pl.when((i > 0) | (b > 0))
        def _():
            dg_ref[...] += p_g

    row = pl.BlockSpec((None, ts, d), lambda b, i: (b, i, 0))
    vec = pl.BlockSpec((None, 1, d), lambda b, i: (b, 0, 0))
    one = pl.BlockSpec((1, d), lambda b, i: (0, 0))
    return pl.pallas_call(
        body, name=name, grid=(bsz, s // ts),
        in_specs=[row, row, row, one, vec],
        out_specs=[row, vec, vec, one],
        out_shape=[jax.ShapeDtypeStruct(x.shape, F32), jax.ShapeDtypeStruct((bsz, 1, d), F32),
                   jax.ShapeDtypeStruct((bsz, 1, d), F32), jax.ShapeDtypeStruct((1, d), F32)],
        compiler_params=_cp(("arbitrary", "arbitrary")),
    )(dh, x, dres, g, sc)


def gate_bwd(dx, mm, gate, name):
    bsz, s, d = dx.shape
    ts = _row_tile(s)

    def body(dx_ref, m_ref, g_ref, dm_ref, dg_ref):
        i = pl.program_id(1)
        dxv = dx_ref[...]
        dm_ref[...] = (dxv * g_ref[...]).astype(dm_ref.dtype)
        p = jnp.sum(dxv * m_ref[...], axis=0, keepdims=True)

        @pl.when(i == 0)
        def _():
            dg_ref[...] = p

        @pl.when(i > 0)
        def _():
            dg_ref[...] += p

    row = pl.BlockSpec((None, ts, d), lambda b, i: (b, i, 0))
    vec = pl.BlockSpec((None, 1, d), lambda b, i: (b, 0, 0))
    return pl.pallas_call(
        body, name=name, grid=(bsz, s // ts),
        in_specs=[row, row, vec], out_specs=[row, vec],
        out_shape=[jax.ShapeDtypeStruct(dx.shape, BF16), jax.ShapeDtypeStruct((bsz, 1, d), F32)],
        compiler_params=_cp(("parallel", "arbitrary")),
    )(dx, mm, gate)


def loss_head(x, g, tgt, name):
    bsz, s, d = x.shape
    ts = _row_tile(s)

    def body(x_ref, g_ref, t_ref, dx_ref, dg_ref, l_ref):
        b, i = pl.program_id(0), pl.program_id(1)
        xv, gv = x_ref[...], g_ref[...]
        r = lax.rsqrt(jnp.mean(xv * xv, axis=-1, keepdims=True) + EPS)
        xn = xv * r
        e = xn * gv - t_ref[...]
        dy = e * (1.0 / d)
        dxn = dy * gv
        dx_ref[...] = r * (dxn - xn * jnp.mean(dxn * xn, axis=-1, keepdims=True))
        p_g = jnp.sum(dy * xn, axis=0, keepdims=True)
        p_l = jnp.zeros((1, LANES), F32) + jnp.sum(e * e) * (0.5 / d)
        first = (i == 0) & (b == 0)

        @pl.when(first)
        def _():
            dg_ref[...] = p_g
            l_ref[...] = p_l

        @pl.when(jnp.logical_not(first))
        def _():
            dg_ref[...] += p_g
            l_ref[...] += p_l

    row = pl.BlockSpec((None, ts, d), lambda b, i: (b, i, 0))
    one = pl.BlockSpec((1, d), lambda b, i: (0, 0))
    return pl.pallas_call(
        body, name=name, grid=(bsz, s // ts),
        in_specs=[row, one, row],
        out_specs=[row, one, pl.BlockSpec((1, LANES), lambda b, i: (0, 0))],
        out_shape=[jax.ShapeDtypeStruct(x.shape, F32), jax.ShapeDtypeStruct((1, d), F32),
                   jax.ShapeDtypeStruct((1, LANES), F32)],
        compiler_params=_cp(("arbitrary", "arbitrary")),
    )(x, g, tgt)


def _gmlp_chunk(u_raw, v_raw, ln_g, ln_b, w, bs_t, out_g):
    c = u_raw.shape[0]
    u, v = _gelu(u_raw), _gelu(v_raw)
    tril = lax.broadcasted_iota(jnp.int32, (c, c), 0) >= lax.broadcasted_iota(jnp.int32, (c, c), 1)
    ys = []
    for h in range(GM_HEADS):
        sl = slice(h * GM_HD, (h + 1) * GM_HD)
        vh = v[:, sl]
        xc = vh - jnp.mean(vh, axis=-1, keepdims=True)
        vn = xc * lax.rsqrt(jnp.mean(xc * xc, axis=-1, keepdims=True) + LN_EPS) * ln_g[:, sl] + ln_b[:, sl]
        gate = _nn(jnp.where(tril, w[h], 0.0), vn) + bs_t[:, h:h + 1]
        ys.append(u[:, sl] * gate)
    return _rms(jnp.concatenate(ys, axis=1), out_g)


def _gmlp_specs(bsz, nc):
    seg = lambda off: pl.BlockSpec((None, CHUNK, GMW), lambda b, c: (b, c, off // GMW))
    full = lambda shape: pl.BlockSpec(shape, lambda b, c: (0,) * len(shape))
    par = [full((1, GMW)), full((1, GMW)), full((GM_HEADS, CHUNK, CHUNK)), full((CHUNK, GM_HEADS)), full((1, GMW))]
    return seg, full, par


def gmlp_fwd(p, prm, name):
    bsz, s, _ = p.shape
    nc = s // CHUNK
    seg, _, par = _gmlp_specs(bsz, nc)

    def body(u_ref, v_ref, lg, lb, w, bt, og, o_ref):
        o_ref[...] = _gmlp_chunk(u_ref[...], v_ref[...], lg[...], lb[...], w[...], bt[...], og[...]).astype(o_ref.dtype)

    return pl.pallas_call(
        body, name=name, grid=(bsz, nc),
        in_specs=[seg(OFF_U), seg(OFF_V)] + par,
        out_specs=pl.BlockSpec((None, CHUNK, GMW), lambda b, c: (b, c, 0)),
        out_shape=jax.ShapeDtypeStruct((bsz, s, GMW), BF16),
        compiler_params=_cp(("parallel", "parallel")),
    )(p, p, *prm)


def _accumulate(first, refs, vals):
    @pl.when(first)
    def _():
        for r, v in zip(refs, vals):
            r[...] = v

    @pl.when(jnp.logical_not(first))
    def _():
        for r, v in zip(refs, vals):
            r[...] += v


def gmlp_bwd(p, dmix, prm, name):
    bsz, s, _ = p.shape
    nc = s // CHUNK
    seg, full, par = _gmlp_specs(bsz, nc)

    def body(u_ref, v_ref, do_ref, lg, lb, w, bt, og, du_ref, dv_ref, *dpar):
        first = (pl.program_id(0) == 0) & (pl.program_id(1) == 0)
        _, vjp = jax.vjp(_gmlp_chunk, u_ref[...], v_ref[...], lg[...], lb[...], w[...], bt[...], og[...])
        gr = vjp(do_ref[...])
        du_ref[...] = gr[0].astype(du_ref.dtype)
        dv_ref[...] = gr[1].astype(dv_ref.dtype)
        _accumulate(first, dpar, gr[2:])

    out_seg = pl.BlockSpec((None, CHUNK, GMW), lambda b, c: (b, c, 0))
    return pl.pallas_call(
        body, name=name, grid=(bsz, nc),
        in_specs=[seg(OFF_U), seg(OFF_V), out_seg] + par,
        out_specs=[out_seg, out_seg] + par,
        out_shape=[jax.ShapeDtypeStruct((bsz, s, GMW), BF16)] * 2 + [jax.ShapeDtypeStruct(x.shape, F32) for x in prm],
        compiler_params=_cp(("arbitrary", "arbitrary")),
    )(p, p, dmix, *prm)


def _attn_block(q, kp, kc, vp, vc, sinks, out_g, has_prev):
    w = q.shape[0]
    k2 = jnp.concatenate([kp, kc], axis=0)
    v2 = jnp.concatenate([vp, vc], axis=0)
    qi = lax.broadcasted_iota(jnp.int32, (w, 2 * w), 0)
    kj = lax.broadcasted_iota(jnp.int32, (w, 2 * w), 1)
    diff = qi + w - kj
    grp = ATT_HEADS // ATT_KV
    valid = (diff >= 0) & (diff < w) & ((kj >= w) | has_prev)
    valid = jnp.concatenate([valid] * grp, axis=0)
    outs = []
    for kv in range(ATT_KV):
        kh = k2[:, kv * ATT_HD:(kv + 1) * ATT_HD]
        vh = v2[:, kv * ATT_HD:(kv + 1) * ATT_HD]
        heads = range(kv * grp, (kv + 1) * grp)
        qs = jnp.concatenate([q[:, h * ATT_HD:(h + 1) * ATT_HD] for h in heads], axis=0)
        sink = jnp.concatenate([jnp.broadcast_to(sinks[:, h:h + 1], (w, 1)) for h in heads], axis=0)
        sc = jnp.where(valid, _nt(qs, kh) * (ATT_HD ** -0.5), NEG)
        m = jnp.maximum(jnp.max(sc, axis=-1, keepdims=True), sink)
        e = jnp.exp(sc - m)
        pr = e / (jnp.sum(e, axis=-1, keepdims=True) + jnp.exp(sink - m))
        o = _nn(pr, vh)
        outs += [o[gi * w:(gi + 1) * w] for gi in range(grp)]
    return _rms(jnp.concatenate(outs, axis=1), out_g)


def attn_fwd(p, sinks, out_g, name):
    bsz, s, _ = p.shape
    nb = s // WINDOW

    def body(q_ref, kp_ref, kc_ref, vp_ref, vc_ref, s_ref, g_ref, o_ref):
        o_ref[...] = _attn_block(q_ref[...], kp_ref[...], kc_ref[...], vp_ref[...], vc_ref[...], s_ref[...],
                                 g_ref[...], pl.program_id(1) > 0).astype(o_ref.dtype)

    cur = lambda off: pl.BlockSpec((None, WINDOW, KVW), lambda b, n: (b, n, off // KVW))
    prev = lambda off: pl.BlockSpec((None, WINDOW, KVW), lambda b, n: (b, jnp.maximum(n - 1, 0), off // KVW))
    return pl.pallas_call(
        body, name=name, grid=(bsz, nb),
        in_specs=[pl.BlockSpec((None, WINDOW, ATW), lambda b, n: (b, n, OFF_Q // ATW)),
                  prev(OFF_K), cur(OFF_K), prev(OFF_VV), cur(OFF_VV),
                  pl.BlockSpec((1, ATT_HEADS), lambda b, n: (0, 0)), pl.BlockSpec((1, ATW), lambda b, n: (0, 0))],
        out_specs=pl.BlockSpec((None, WINDOW, ATW), lambda b, n: (b, n, 0)),
        out_shape=jax.ShapeDtypeStruct((bsz, s, ATW), BF16),
        compiler_params=_cp(("parallel", "parallel")),
    )(p, p, p, p, p, sinks, out_g)


def attn_bwd(p, dmix, sinks, out_g, name):
    bsz, s, _ = p.shape
    nb = s // WINDOW

    def body(q_ref, kp_ref, kc_ref, vp_ref, vc_ref, do_ref, s_ref, g_ref,
             dq_ref, dk_ref, dv_ref, ds_ref, dg_ref, ck, cv):
        b, n = pl.program_id(0), pl.program_id(1)

        @pl.when(n == 0)
        def _():
            ck[...] = jnp.zeros_like(ck)
            cv[...] = jnp.zeros_like(cv)

        @pl.when(n < nb)
        def _():
            fn = functools.partial(_attn_block, has_prev=n > 0)
            _, vjp = jax.vjp(fn, q_ref[...], kp_ref[...], kc_ref[...], vp_ref[...], vc_ref[...], s_ref[...], g_ref[...])
            dq, dkp, dkc, dvp, dvc, dsk, dgg = vjp(do_ref[...])
            dq_ref[...] = dq.astype(dq_ref.dtype)
            dk_ref[...] = (ck[...] + dkp).astype(dk_ref.dtype)
            dv_ref[...] = (cv[...] + dvp).astype(dv_ref.dtype)
            ck[...] = dkc
            cv[...] = dvc
            _accumulate((b == 0) & (n == 0), (ds_ref, dg_ref), (dsk, dgg))

        @pl.when(n == nb)
        def _():
            dk_ref[...] = ck[...].astype(dk_ref.dtype)
            dv_ref[...] = cv[...].astype(dv_ref.dtype)

    at = lambda n: jnp.minimum(n, nb - 1)
    cur = lambda off: pl.BlockSpec((None, WINDOW, KVW), lambda b, n: (b, at(n), off // KVW))
    prev = lambda off: pl.BlockSpec((None, WINDOW, KVW), lambda b, n: (b, jnp.maximum(at(n) - 1, 0), off // KVW))
    kv_out = pl.BlockSpec((None, WINDOW, KVW), lambda b, n: (b, jnp.maximum(n - 1, 0), 0))
    return pl.pallas_call(
        body, name=name, grid=(bsz, nb + 1),
        in_specs=[pl.BlockSpec((None, WINDOW, ATW), lambda b, n: (b, at(n), OFF_Q // ATW)),
                  prev(OFF_K), cur(OFF_K), prev(OFF_VV), cur(OFF_VV),
                  pl.BlockSpec((None, WINDOW, ATW), lambda b, n: (b, at(n), GMW // ATW)),
                  pl.BlockSpec((1, ATT_HEADS), lambda b, n: (0, 0)), pl.BlockSpec((1, ATW), lambda b, n: (0, 0))],
        out_specs=[pl.BlockSpec((None, WINDOW, ATW), lambda b, n: (b, at(n), 0)), kv_out, kv_out,
                   pl.BlockSpec((1, ATT_HEADS), lambda b, n: (0, 0)), pl.BlockSpec((1, ATW), lambda b, n: (0, 0))],
        out_shape=[jax.ShapeDtypeStruct((bsz, s, ATW), BF16), jax.ShapeDtypeStruct((bsz, s, KVW), BF16),
                   jax.ShapeDtypeStruct((bsz, s, KVW), BF16), jax.ShapeDtypeStruct((1, ATT_HEADS), F32),
                   jax.ShapeDtypeStruct((1, ATW), F32)],
        scratch_shapes=[pltpu.VMEM((WINDOW, KVW), F32), pltpu.VMEM((WINDOW, KVW), F32)],
        compiler_params=_cp(("arbitrary", "arbitrary")),
    )(p, p, p, p, p, dmix, sinks, out_g)


CONV_CT = 256


def _shift_down(x, j):
    if j == 0:
        return x
    rows = lax.broadcasted_iota(jnp.int32, x.shape, 0)
    return jnp.where(rows >= j, pltpu.roll(x, j, 0), 0.0)


def _shift_up(x, j):
    if j == 0:
        return x
    s = x.shape[0]
    rows = lax.broadcasted_iota(jnp.int32, x.shape, 0)
    return jnp.where(rows < s - j, pltpu.roll(x, s - j, 0), 0.0)


def conv_fwd(p, w, bias, name):
    bsz, s, _ = p.shape

    def body(x_ref, w_ref, b_ref, o_ref):
        xv, wv = x_ref[...], w_ref[...]
        pre = b_ref[...] + sum(wv[k:k + 1, :] * _shift_down(xv, CONV_K - 1 - k) for k in range(CONV_K))
        o_ref[...] = _silu(pre)

    blk = pl.BlockSpec((None, s, CONV_CT), lambda b, j: (b, 0, j))
    return pl.pallas_call(
        body, name=name, grid=(bsz, CCH // CONV_CT),
        in_specs=[blk, pl.BlockSpec((CONV_K, CONV_CT), lambda b, j: (0, j)), pl.BlockSpec((1, CONV_CT), lambda b, j: (0, j))],
        out_specs=blk, out_shape=jax.ShapeDtypeStruct((bsz, s, CCH), F32),
        compiler_params=_cp(("parallel", "parallel")),
    )(p, w, bias)


def conv_bwd(p, dxc, w, bias, name):
    bsz, s, _ = p.shape

    def body(x_ref, d_ref, w_ref, b_ref, dx_ref, dw_ref, db_ref):
        b = pl.program_id(1)
        xv, wv = x_ref[...], w_ref[...]
        xs = [_shift_down(xv, CONV_K - 1 - k) for k in range(CONV_K)]
        pre = b_ref[...] + sum(wv[k:k + 1, :] * xs[k] for k in range(CONV_K))
        sg = 1.0 / (1.0 + jnp.exp(-pre))
        dpre = d_ref[...] * (sg * (1.0 + pre * (1.0 - sg)))
        dx_ref[...] = sum(wv[k:k + 1, :] * _shift_up(dpre, CONV_K - 1 - k) for k in range(CONV_K)).astype(dx_ref.dtype)
        p_w = jnp.concatenate([jnp.sum(dpre * xs[k], axis=0, keepdims=True) for k in range(CONV_K)], axis=0)
        p_b = jnp.sum(dpre, axis=0, keepdims=True)
        _accumulate(b == 0, (dw_ref, db_ref), (p_w, p_b))

    blk = pl.BlockSpec((None, s, CONV_CT), lambda j, b: (b, 0, j))
    wsp = pl.BlockSpec((CONV_K, CONV_CT), lambda j, b: (0, j))
    bsp = pl.BlockSpec((1, CONV_CT), lambda j, b: (0, j))
    return pl.pallas_call(
        body, name=name, grid=(CCH // CONV_CT, bsz),
        in_specs=[blk, blk, wsp, bsp], out_specs=[blk, wsp, bsp],
        out_shape=[jax.ShapeDtypeStruct((bsz, s, CCH), BF16), jax.ShapeDtypeStruct((CONV_K, CCH), F32),
                   jax.ShapeDtypeStruct((1, CCH), F32)],
        compiler_params=_cp(("parallel", "arbitrary")),
    )(p, dxc, w, bias)


def _ssd_consts():
    c = CHUNK
    r = lax.broadcasted_iota(jnp.int32, (c, c), 0)
    q = lax.broadcasted_iota(jnp.int32, (c, c), 1)
    hrow = lax.broadcasted_iota(jnp.int32, (LANES, SSW), 0)
    hcol = lax.broadcasted_iota(jnp.int32, (LANES, SSW), 1) // SSM_HD
    expand = (hrow == hcol).astype(F32)
    return expand, (r >= q).astype(F32), (r <= q).astype(F32), r >= q


def _ssd_chunk(xc, dtr, z, prev_t, dt_bias, a_log, d_skip, norm_g):
    c = xc.shape[0]
    expand, tril1, triu1, causal = _ssd_consts()
    xs, bm, cm = xc[:, :SSW], xc[:, SSW:SSW + BCW], xc[:, SSW + BCW:]
    dt = _softplus(dtr + dt_bias)
    da = dt * (-jnp.exp(a_log))
    a_cs = _hdot(tril1, da)
    a_cs_t = _hdot(da.T, triu1)
    dt_e = _hdot(dt, expand)
    acs_e = _hdot(a_cs, expand)
    alast_e = acs_e[c - 1:c, :]
    dsk_e = _hdot(jnp.broadcast_to(d_skip, (8, LANES)), expand)[0:1, :]
    xdt = xs * dt_e
    hg = SSM_HEADS // SSM_GROUPS
    ys, new_t = [], []
    for g in range(SSM_GROUPS):
        bg = bm[:, g * SSM_STATE:(g + 1) * SSM_STATE]
        cg = cm[:, g * SSM_STATE:(g + 1) * SSM_STATE]
        sl = slice(g * GRW, (g + 1) * GRW)
        cb = _nt(cg, bg)
        xdt_g = xdt[:, sl]
        st = _tn(bg, xdt_g * jnp.exp(alast_e[:, sl] - acs_e[:, sl]))
        new_t.append(prev_t[:, sl] * jnp.exp(alast_e[:, sl]) + st)
        y_off = _nn(cg, prev_t[:, sl]) * jnp.exp(acs_e[:, sl])
        yd = []
        for hh in range(hg):
            h = g * hg + hh
            decay = jnp.exp(jnp.where(causal, a_cs[:, h:h + 1] - a_cs_t[h:h + 1, :], NEG))
            yd.append(_nn(cb * decay, xdt_g[:, hh * SSM_HD:(hh + 1) * SSM_HD]))
        ys.append(jnp.concatenate(yd, axis=1) + y_off)
    y = (jnp.concatenate(ys, axis=1) + xs * dsk_e) * _silu(z)
    yn = [y[:, g * GRW:(g + 1) * GRW] * lax.rsqrt(jnp.mean(jnp.square(y[:, g * GRW:(g + 1) * GRW]), axis=-1, keepdims=True) + EPS)
          for g in range(SSM_GROUPS)]
    return jnp.concatenate(yn, axis=1) * norm_g, jnp.concatenate(new_t, axis=1)


def ssd_fwd(xc, p, prm, name):
    bsz, s, _ = p.shape
    nc = s // CHUNK

    def body(xc_ref, dt_ref, z_ref, db, al, dk, ng, o_ref, st_ref, state):
        @pl.when(pl.program_id(1) == 0)
        def _():
            state[...] = jnp.zeros_like(state)

        prev = state[...]
        st_ref[...] = prev
        out, new = _ssd_chunk(xc_ref[...], dt_ref[...], z_ref[...], prev, db[...], al[...], dk[...], ng[...])
        o_ref[...] = out.astype(o_ref.dtype)
        state[...] = new

    vec = pl.BlockSpec((1, LANES), lambda b, c: (0, 0))
    return pl.pallas_call(
        body, name=name, grid=(bsz, nc),
        in_specs=[pl.BlockSpec((None, CHUNK, CCH), lambda b, c: (b, c, 0)),
                  pl.BlockSpec((None, CHUNK, LANES), lambda b, c: (b, c, OFF_DT // LANES)),
                  pl.BlockSpec((None, CHUNK, SSW), lambda b, c: (b, c, OFF_Z // SSW)),
                  vec, vec, vec, pl.BlockSpec((1, SSW), lambda b, c: (0, 0))],
        out_specs=[pl.BlockSpec((None, CHUNK, SSW), lambda b, c: (b, c, 0)),
                   pl.BlockSpec((None, None, SSM_STATE, SSW), lambda b, c: (b, c, 0, 0))],
        out_shape=[jax.ShapeDtypeStruct((bsz, s, SSW), BF16), jax.ShapeDtypeStruct((bsz, nc, SSM_STATE, SSW), F32)],
        scratch_shapes=[pltpu.VMEM((SSM_STATE, SSW), F32)],
        compiler_params=_cp(("parallel", "arbitrary")),
    )(xc, p, p, *prm)


def ssd_bwd(xc, p, states, dmix, prm, name):
    bsz, s, _ = p.shape
    nc = s // CHUNK

    def body(xc_ref, dt_ref, z_ref, st_ref, do_ref, db, al, dk, ng, dxc_ref, ddt_ref, dz_ref, *rest):
        dpar, dstate = rest[:4], rest[4]
        b, c = pl.program_id(0), pl.program_id(1)

        @pl.when(c == 0)
        def _():
            dstate[...] = jnp.zeros_like(dstate)

        _, vjp = jax.vjp(_ssd_chunk, xc_ref[...], dt_ref[...], z_ref[...], st_ref[...], db[...], al[...], dk[...], ng[...])
        gr = vjp((do_ref[...], dstate[...]))
        dxc_ref[...] = gr[0]
        ddt_ref[...] = gr[1].astype(ddt_ref.dtype)
        dz_ref[...] = gr[2].astype(dz_ref.dtype)
        dstate[...] = gr[3]
        _accumulate((b == 0) & (c == 0), dpar, gr[4:])

    rv = lambda c: nc - 1 - c
    vec = pl.BlockSpec((1, LANES), lambda b, c: (0, 0))
    ngs = pl.BlockSpec((1, SSW), lambda b, c: (0, 0))
    return pl.pallas_call(
        body, name=name, grid=(bsz, nc),
        in_specs=[pl.BlockSpec((None, CHUNK, CCH), lambda b, c: (b, rv(c), 0)),
                  pl.BlockSpec((None, CHUNK, LANES), lambda b, c: (b, rv(c), OFF_DT // LANES)),
                  pl.BlockSpec((None, CHUNK, SSW), lambda b, c: (b, rv(c), OFF_Z // SSW)),
                  pl.BlockSpec((None, None, SSM_STATE, SSW), lambda b, c: (b, rv(c), 0, 0)),
                  pl.BlockSpec((None, CHUNK, SSW), lambda b, c: (b, rv(c), (GMW + ATW) // SSW)),
                  vec, vec, vec, ngs],
        out_specs=[pl.BlockSpec((None, CHUNK, CCH), lambda b, c: (b, rv(c), 0)),
                   pl.BlockSpec((None, CHUNK, LANES), lambda b, c: (b, rv(c), 0)),
                   pl.BlockSpec((None, CHUNK, SSW), lambda b, c: (b, rv(c), 0)),
                   vec, vec, vec, ngs],
        out_shape=[jax.ShapeDtypeStruct((bsz, s, CCH), F32), jax.ShapeDtypeStruct((bsz, s, LANES), BF16),
                   jax.ShapeDtypeStruct((bsz, s, SSW), BF16)] + [jax.ShapeDtypeStruct((1, LANES), F32)] * 3
                  + [jax.ShapeDtypeStruct((1, SSW), F32)],
        scratch_shapes=[pltpu.VMEM((SSM_STATE, SSW), F32)],
        compiler_params=_cp(("arbitrary", "arbitrary")),
    )(xc, p, p, states, dmix, *prm)


def _rows2d(a):
    return a.reshape(-1, a.shape[-1])


def _ew_tile(r, c):
    t = r
    while t * c > (1 << 20) and t % 16 == 0:
        t //= 2
    return t


def add_pair(g, theirs, core, name):
    k, r, c = g.shape
    h = r // 2
    tr = _ew_tile(h, c)
    nb = h // tr

    def body(c_ref, a_ref, b_ref, o_ref, ob_ref):
        s = a_ref[...] + b_ref[...]
        o_ref[...] = s
        ob_ref[...] = s.astype(ob_ref.dtype)

    blk = pl.BlockSpec((None, tr, c), lambda kk, i, cr: (kk, i, 0))
    return pl.pallas_call(
        body, name=name,
        grid_spec=pltpu.PrefetchScalarGridSpec(
            num_scalar_prefetch=1, grid=(k, nb),
            in_specs=[pl.BlockSpec((None, tr, c), lambda kk, i, cr: (kk, cr[0] * nb + i, 0)), blk],
            out_specs=[blk, blk]),
        out_shape=[jax.ShapeDtypeStruct(theirs.shape, F32), jax.ShapeDtypeStruct(theirs.shape, BF16)],
        compiler_params=_cp(("parallel", "parallel")),
    )(core.reshape(1).astype(jnp.int32), g, theirs)


def sum_own_recv(sums, recv, chip, name):
    _, h, c = sums.shape
    tr = _ew_tile(h, c)

    def body(k_ref, o_ref, r_ref, out_ref):
        s = o_ref[...]
        for j in range(3):
            s = s + r_ref[j].astype(F32)
        out_ref[...] = s

    return pl.pallas_call(
        body, name=name,
        grid_spec=pltpu.PrefetchScalarGridSpec(
            num_scalar_prefetch=1, grid=(h // tr,),
            in_specs=[pl.BlockSpec((None, tr, c), lambda i, kr: (kr[0], i, 0)),
                      pl.BlockSpec((3, tr, c), lambda i, kr: (0, i, 0))],
            out_specs=pl.BlockSpec((tr, c), lambda i, kr: (i, 0))),
        out_shape=jax.ShapeDtypeStruct((h, c), F32),
        compiler_params=_cp(("parallel",)),
    )(chip.reshape(1).astype(jnp.int32), sums, recv)


def _adam_math(w, m, v, g):
    mn = ADAM_B1 * m + (1.0 - ADAM_B1) * g
    vn = ADAM_B2 * v + (1.0 - ADAM_B2) * (g * g)
    mh = mn / (1.0 - ADAM_B1 ** ADAM_STEP)
    vh = vn / (1.0 - ADAM_B2 ** ADAM_STEP)
    return -ADAM_LR * (mh / (jnp.sqrt(vh) + ADAM_EPS) + ADAM_WD * w), mn, vn


def adamw_layer(w, m, v, g, layer, prev, name):
    nl, r, c = w.shape
    tr = _ew_tile(r, c * 2)

    def body(w_ref, m_ref, v_ref, g_ref, *rest):
        go_ref, d_ref, mo_ref, vo_ref = rest[-4:]
        gv = g_ref[...]
        dl, mn, vn = _adam_math(w_ref[...], m_ref[...], v_ref[...], gv)
        go_ref[...] = gv
        d_ref[...] = dl
        mo_ref[...] = mn
        vo_ref[...] = vn

    lay = pl.BlockSpec((None, tr, c), lambda i: (layer, i, 0))
    n_prev = 0 if prev is None else 4
    return pl.pallas_call(
        body, name=name, grid=(r // tr,),
        in_specs=[lay, lay, lay, pl.BlockSpec((tr, c), lambda i: (i, 0))] + [_ANY] * n_prev,
        out_specs=[lay] * 4, out_shape=[jax.ShapeDtypeStruct(w.shape, F32)] * 4,
        input_output_aliases={4 + i: i for i in range(n_prev)},
        compiler_params=_cp(("parallel",)),
    )(w, m, v, g, *(prev or ()))


def sum_devices(parts, name):
    n, r, c = parts.shape
    tr = _ew_tile(r, c * n)

    def body(p_ref, o_ref):
        s = p_ref[0]
        for j in range(1, n):
            s = s + p_ref[j]
        o_ref[...] = s

    return pl.pallas_call(
        body, name=name, grid=(r // tr,),
        in_specs=[pl.BlockSpec((n, tr, c), lambda i: (0, i, 0))],
        out_specs=pl.BlockSpec((tr, c), lambda i: (i, 0)),
        out_shape=jax.ShapeDtypeStruct((r, c), F32),
        compiler_params=_cp(("parallel",)),
    )(parts)


def adamw(w, m, v, g, name):
    r, c = w.shape
    tr = _ew_tile(r, c * 2)

    def body(w_ref, m_ref, v_ref, g_ref, d_ref, mo_ref, vo_ref):
        gv = g_ref[...]
        mn = ADAM_B1 * m_ref[...] + (1.0 - ADAM_B1) * gv
        vn = ADAM_B2 * v_ref[...] + (1.0 - ADAM_B2) * (gv * gv)
        mh = mn / (1.0 - ADAM_B1 ** ADAM_STEP)
        vh = vn / (1.0 - ADAM_B2 ** ADAM_STEP)
        d_ref[...] = -ADAM_LR * (mh / (jnp.sqrt(vh) + ADAM_EPS) + ADAM_WD * w_ref[...])
        mo_ref[...] = mn
        vo_ref[...] = vn

    blk = pl.BlockSpec((tr, c), lambda i: (i, 0))
    return pl.pallas_call(
        body, name=name, grid=(r // tr,), in_specs=[blk] * 4, out_specs=[blk] * 3,
        out_shape=[jax.ShapeDtypeStruct((r, c), F32)] * 3,
        compiler_params=_cp(("parallel",)),
    )(w, m, v, g)


def _place():
    x, y, c = lax.axis_index("x"), lax.axis_index("y"), lax.axis_index("c")
    chips = [(1 - x, y), (x, 1 - y), (1 - x, 1 - y)]
    return x, y, c, chips


def all_gather_small(v, name):
    r, w = v.shape

    def body(x_ref, out_ref, send_sems, recv_sems, local_sem):
        x, y, c, chips = _place()
        me, sibling = (x, y, c), (x, y, 1 - c)

        def rows(px, py, pc):
            return out_ref.at[pl.ds((4 * px + 2 * py + pc) * r, r), :]

        def copy(k, block, to, src=None):
            return pltpu.make_async_remote_copy(
                src_ref=rows(*block) if src is None else src, dst_ref=rows(*block),
                send_sem=send_sems.at[k], recv_sem=recv_sems.at[k], device_id=to, device_id_type=MESH)

        mine = pltpu.make_async_copy(x_ref, rows(*me), local_sem)
        mine.start()
        first = [copy(0, me, sibling, src=x_ref)]
        first += [copy(1 + j, me, (*chip, c), src=x_ref) for j, chip in enumerate(chips)]
        for cp in first:
            cp.start()
        passed = [copy(4 + j, (*chip, c), sibling) for j, chip in enumerate(chips)]
        for j, chip in enumerate(chips):
            copy(1 + j, (*chip, c), me).wait_recv()
            passed[j].start()
        copy(0, sibling, me).wait_recv()
        for j, chip in enumerate(chips):
            copy(4 + j, (*chip, 1 - c), me).wait_recv()
        for cp in first + passed:
            cp.wait_send()
        mine.wait()

    out = pl.pallas_call(
        body, name=name, out_shape=jax.ShapeDtypeStruct((8 * r, w), v.dtype),
        in_specs=[pl.BlockSpec(memory_space=pltpu.VMEM)], out_specs=pl.BlockSpec(memory_space=pltpu.VMEM),
        scratch_shapes=[pltpu.SemaphoreType.DMA((7,)), pltpu.SemaphoreType.DMA((7,)), pltpu.SemaphoreType.DMA],
        compiler_params=pltpu.CompilerParams(vmem_limit_bytes=VMEM_LIMIT),
    )(v)
    return out.reshape(8, r, w)


_HBM = pl.BlockSpec(memory_space=pltpu.HBM)


_SEM = pl.BlockSpec(memory_space=pltpu.SEMAPHORE)
_ANY = pl.BlockSpec(memory_space=pl.ANY)
_EFFECT = pltpu.SideEffectType.DATAFLOW_SIDE_EFFECTING


def _hbm(a):
    return pltpu.with_memory_space_constraint(a, pltpu.HBM)


def split_copy_start(srcs, land_shapes, copies, after, name):
    n, nl = len(srcs), len(land_shapes)
    ncopy = [0]

    def body(*refs):
        ins, lands = refs[:n], refs[n:n + nl]
        send_sems, recv_sems = refs[n + nl + 1], refs[n + nl + 2]
        token = refs[-1]
        x, y, c, chips = _place()
        for k, (src, dst, to) in enumerate(copies(x, y, c, chips, ins, lands)):
            pltpu.make_async_remote_copy(src_ref=src, dst_ref=dst, send_sem=send_sems.at[k], recv_sem=recv_sems.at[k],
                                         device_id=to, device_id_type=MESH).start()
        token[...] = jnp.zeros_like(token)

    ncopy[0] = len(copies(0, 0, 0, [(1, 0), (0, 1), (1, 1)], [None] * n, [None] * nl, count_only=True))
    k = ncopy[0]
    lands = [_hbm(lax.empty(s.shape, s.dtype)) for s in land_shapes]
    res = pl.pallas_call(
        body, name=name,
        out_shape=(pltpu.SemaphoreType.DMA((k,)), pltpu.SemaphoreType.DMA((k,)))
        + tuple(pltpu.HBM(s.shape, s.dtype) for s in srcs) + tuple(pltpu.HBM(s.shape, s.dtype) for s in land_shapes)
        + (jax.ShapeDtypeStruct((8, LANES), F32),),
        in_specs=[_HBM] * (n + nl) + [_ANY],
        out_specs=(_SEM, _SEM) + (_HBM,) * (n + nl) + (pl.BlockSpec(memory_space=pltpu.VMEM),),
        input_output_aliases={i: 2 + i for i in range(n + nl)},
        compiler_params=pltpu.CompilerParams(has_side_effects=_EFFECT),
    )(*[_hbm(s) for s in srcs], *lands, after)
    return res[0], res[1], list(res[2:2 + n]), list(res[2 + n:2 + n + nl]), res[-1]


def split_copy_wait(send_sems, recv_sems, srcs, lands, copies, after, name):
    n, nl = len(srcs), len(lands)

    def body(*refs):
        ins, lnd = refs[:n], refs[n:n + nl]
        ss, rs = refs[n + nl], refs[n + nl + 1]
        x, y, c, chips = _place()
        for k, (src, dst, to) in enumerate(copies(x, y, c, chips, ins, lnd, receive=True)):
            cp = pltpu.make_async_remote_copy(src_ref=src, dst_ref=dst, send_sem=ss.at[k], recv_sem=rs.at[k],
                                              device_id=to, device_id_type=MESH)
            cp.wait_send()
            cp.wait_recv()

    res = pl.pallas_call(
        body, name=name,
        out_shape=tuple(pltpu.HBM(s.shape, s.dtype) for s in srcs) + tuple(pltpu.HBM(s.shape, s.dtype) for s in lands),
        in_specs=[_HBM] * (n + nl) + [_SEM, _SEM, _ANY], out_specs=(_HBM,) * (n + nl),
        input_output_aliases={i: i for i in range(n + nl)},
        compiler_params=pltpu.CompilerParams(has_side_effects=_EFFECT),
    )(*srcs, *lands, send_sems, recv_sems, after)
    return list(res[:n]), list(res[n:])


def _gather_copies(x, y, c, chips, ins, lands, receive=False, count_only=False):
    out = []
    for i in range(len(ins)):
        for cx, cy in chips:
            if count_only:
                out.append(None)
                continue
            h = ins[i].shape[0] // 2
            rows = pl.ds(c * h, h)
            k_dst = (2 * cx + cy) if receive else (2 * x + y)
            out.append((ins[i].at[rows, :], lands[i].at[k_dst, rows, :], (cx, cy, c)))
    return out


def _scatter_copies(x, y, c, chips, ins, lands, receive=False, count_only=False):
    out = []
    for i in range(len(ins)):
        for j, (cx, cy) in enumerate(chips):
            if count_only:
                out.append(None)
                continue
            out.append((ins[i].at[2 * cx + cy], lands[i].at[j], (cx, cy, c)))
    return out


def forward_halves(lands, name):
    n = len(lands)

    def body(*refs):
        ins, outs = refs[:n], refs[n:2 * n]
        send_sems, recv_sems = refs[2 * n:]
        x, y, c, chips = _place()
        sibling = (x, y, 1 - c)
        sent = []
        for i in range(n):
            h = ins[i].shape[1] // 2
            for j, (cx, cy) in enumerate(chips):
                blk = ins[i].at[2 * cx + cy, pl.ds(c * h, h), :]
                sent.append(pltpu.make_async_remote_copy(
                    src_ref=blk, dst_ref=outs[i].at[2 * cx + cy, pl.ds(c * h, h), :], send_sem=send_sems.at[3 * i + j],
                    recv_sem=recv_sems.at[3 * i + j], device_id=sibling, device_id_type=MESH))
                sent[-1].start()
        for i in range(n):
            h = ins[i].shape[1] // 2
            for j, (cx, cy) in enumerate(chips):
                theirs = outs[i].at[2 * cx + cy, pl.ds((1 - c) * h, h), :]
                pltpu.make_async_remote_copy(
                    src_ref=theirs, dst_ref=theirs, send_sem=send_sems.at[3 * i + j], recv_sem=recv_sems.at[3 * i + j],
                    device_id=sibling, device_id_type=MESH).wait_recv()
        for cp in sent:
            cp.wait_send()

    return pl.pallas_call(
        body, name=name, out_shape=[jax.ShapeDtypeStruct(s.shape, s.dtype) for s in lands],
        in_specs=[_HBM] * n, out_specs=[_HBM] * n, input_output_aliases={i: i for i in range(n)},
        scratch_shapes=[pltpu.SemaphoreType.DMA((3 * n,)), pltpu.SemaphoreType.DMA((3 * n,))],
    )(*lands)


def swap_halves(grads, name):
    n = len(grads)

    def body(*refs):
        ins, theirs = refs[:n], refs[n:2 * n]
        send_sems, recv_sems = refs[2 * n:]
        x, y, c, _ = _place()
        sibling = (x, y, 1 - c)
        sent = []
        for i in range(n):
            h = ins[i].shape[1] // 2
            sent.append(pltpu.make_async_remote_copy(
                src_ref=ins[i].at[:, pl.ds((1 - c) * h, h), :], dst_ref=theirs[i],
                send_sem=send_sems.at[i], recv_sem=recv_sems.at[i], device_id=sibling, device_id_type=MESH))
            sent[-1].start()
        for cp in sent:
            cp.wait_recv()
        for cp in sent:
            cp.wait_send()

    half = [jax.ShapeDtypeStruct((4, g.shape[1] // 2, g.shape[2]), g.dtype) for g in grads]
    theirs = pl.pallas_call(
        body, name=name, out_shape=half, in_specs=[_HBM] * n, out_specs=[_HBM] * n,
        scratch_shapes=[pltpu.SemaphoreType.DMA((n,)), pltpu.SemaphoreType.DMA((n,))],
    )(*grads)
    return theirs


def join_halves(halves, name):
    n = len(halves)

    def body(*refs):
        ins, outs = refs[:n], refs[n:2 * n]
        send_sems, recv_sems = refs[2 * n:]
        x, y, c, _ = _place()
        sibling = (x, y, 1 - c)
        sent = []
        for i in range(n):
            h = ins[i].shape[0]
            sent.append(pltpu.make_async_remote_copy(
                src_ref=ins[i], dst_ref=outs[i].at[pl.ds(c * h, h), :], send_sem=send_sems.at[i],
                recv_sem=recv_sems.at[i], device_id=sibling, device_id_type=MESH))
            sent[-1].start()
        for i in range(n):
            h = ins[i].shape[0]
            pltpu.make_async_remote_copy(
                src_ref=ins[i], dst_ref=outs[i].at[pl.ds((1 - c) * h, h), :], send_sem=send_sems.at[i],
                recv_sem=recv_sems.at[i], device_id=sibling, device_id_type=MESH).wait_recv()
        for cp in sent:
            cp.wait_send()

    full = pl.pallas_call(
        body, name=name, out_shape=[jax.ShapeDtypeStruct((2 * s.shape[0], s.shape[1]), F32) for s in halves],
        in_specs=[_HBM] * n, out_specs=[_HBM] * n,
        scratch_shapes=[pltpu.SemaphoreType.DMA((n,)), pltpu.SemaphoreType.DMA((n,))],
    )(*halves)
    ci = lax.axis_index("c")
    return [lax.dynamic_update_slice(f, s, (ci * s.shape[0], 0)) for f, s in zip(full, halves)]


_PACK_ROWS = 8 * LANES


def _pack(arrs):
    flat = jnp.concatenate([a.reshape(-1).astype(F32) for a in arrs])
    pad = (-flat.shape[0]) % _PACK_ROWS
    return jnp.pad(flat, (0, pad)).reshape(-1, LANES)


def _unpack(flat, shapes):
    flat = flat.reshape(-1)
    out, off = [], 0
    for s in shapes:
        n = int(np.prod(s))
        out.append(flat[off:off + n].reshape(s))
        off += n
    return out


_SEGS = [(0, OFF_U, GMW), (GMW, OFF_V, GMW), (2 * GMW, OFF_Q, ATW), (2 * GMW + ATW, OFF_K, KVW),
         (2 * GMW + ATW + KVW, OFF_VV, KVW), (2 * GMW + ATW + 2 * KVW, OFF_Z, SSW),
         (2 * GMW + ATW + 2 * KVW + SSW, OFF_XBC, CCH), (IN_W - SSM_HEADS, OFF_DT, SSM_HEADS)]


def _win_to_kernel_layout(w):
    out = jnp.zeros((w.shape[0], PW), w.dtype)
    for src, dst, wd in _SEGS:
        out = lax.dynamic_update_slice(out, w[:, src:src + wd], (0, dst))
    return out


def _win_from_kernel_layout(w):
    return jnp.concatenate([w[:, dst:dst + wd] for _, dst, wd in _SEGS], axis=1)


def _relu2(a):
    r = jnp.maximum(a, 0)
    return r * r


def kernel(x, c, ada_w, ada_b, norm1_g, w_in, gm_ln_g, gm_ln_b, gm_ws, gm_bs, gm_norm_g, attn_sinks, attn_norm_g, conv_w, conv_b, dt_bias, a_log, d_skip, ssm_norm_g, w_out, norm2_g, w_mlp1, w_mlp2, final_norm_g, loss_target, m_ada_w, m_ada_b, m_norm1_g, m_w_in, m_gm_ln_g, m_gm_ln_b, m_gm_ws, m_gm_bs, m_gm_norm_g, m_attn_sinks, m_attn_norm_g, m_conv_w, m_conv_b, m_dt_bias, m_a_log, m_d_skip, m_ssm_norm_g, m_w_out, m_norm2_g, m_w_mlp1, m_w_mlp2, m_final_norm_g, v_ada_w, v_ada_b, v_norm1_g, v_w_in, v_gm_ln_g, v_gm_ln_b, v_gm_ws, v_gm_bs, v_gm_norm_g, v_attn_sinks, v_attn_norm_g, v_conv_w, v_conv_b, v_dt_bias, v_a_log, v_d_skip, v_ssm_norm_g, v_w_out, v_norm2_g, v_w_mlp1, v_w_mlp2, v_final_norm_g):
    nl = ada_w.shape[0]
    bl, s, d = x.shape
    t = bl * s
    dff4 = w_mlp1.shape[2]
    dff = 4 * dff4
    mod_w = ada_w.shape[2]
    cw_w = conv_w.shape[2]
    xi, yi, ci = lax.axis_index("x"), lax.axis_index("y"), lax.axis_index("c")
    chip = 2 * xi + yi
    dev = 2 * chip + ci
    nex = 8 * bl

    g0 = all_gather_small(_pack([c, conv_w]), "ag_c")
    g0 = g0.reshape(8, -1)
    c_all = g0[:, :bl * d].reshape(nex, d)
    cw_parts = g0[0::2, bl * d:bl * d + conv_w.size].reshape(4, nl, CONV_K, cw_w)
    conv_w_full = cw_parts.transpose(1, 2, 0, 3).reshape(nl, CONV_K, CCH)

    def c_act(a):
        return _silu(a).astype(BF16)

    def to_bf16(a):
        return a.astype(BF16)

    mod_parts = []
    for l in range(nl):
        bias = lax.dynamic_slice(ada_b[l].reshape(1, -1), (0, chip * mod_w), (1, mod_w))
        mod_parts.append(_mm("nn", c_all, ada_w[l], dims=(nex, mod_w, d), tm=nex, tn=512, tk=d, out_dtypes=[F32],
                             name=f"mod_{l}", pro_a=c_act, pro_b=to_bf16,
                             extras=[(bias, pl.BlockSpec((1, 512), lambda i, j, kk: (0, j)))],
                             epi=lambda acc, bv: (acc + bv,))[0])
    g1 = all_gather_small(_pack(mod_parts), "ag_mod").reshape(8, -1)
    mod_all = g1[0::2, :nl * nex * mod_w].reshape(4, nl, nex, mod_w).transpose(1, 2, 0, 3).reshape(nl, nex, 4 * mod_w)
    mod = lax.dynamic_slice(mod_all, (0, dev * bl, 0), (nl, bl, 4 * mod_w))
    mods = [[mod[l, :, i * d:(i + 1) * d].reshape(bl, 1, d) for i in range(6)] for l in range(nl)]

    shards = [[w_in[l].astype(BF16), w_out[l].astype(BF16), w_mlp1[l].astype(BF16), w_mlp2[l].astype(BF16)]
              for l in range(nl)]
    groups = [[shards[0][i]] for i in range(4)] + [shards[l] for l in range(1, nl)]
    pending, after = [], g1
    for gi, grp in enumerate(groups):
        ss, rs, srcs, lands, after = split_copy_start(
            grp, [jax.ShapeDtypeStruct((4,) + a.shape, a.dtype) for a in grp], _gather_copies, after, f"gather_start_{gi}")
        pending.append((ss, rs, srcs, lands))
    mods[0][0] = mods[0][0] + after[0, 0]

    def fetch(gi, behind):
        ss, rs, srcs, lands = pending[gi]
        srcs, lands = split_copy_wait(ss, rs, srcs, lands, _gather_copies, behind, f"gather_wait_{gi}")
        lands = forward_halves(lands, f"gather_pass_{gi}")
        return [lax.dynamic_update_slice(g, a[None], (chip, 0, 0)) for g, a in zip(lands, srcs)]

    def as_win(g):
        return _win_to_kernel_layout(g.transpose(1, 0, 2).reshape(d, IN_W))

    wfull = [None] * nl
    row = lambda a: a.reshape(1, -1)
    pad16 = lambda a: jnp.pad(a.reshape(1, -1), ((0, 0), (0, LANES - SSM_HEADS)))
    tm_res = min(1024, s)

    def residual(acc, xt, gt):
        return acc, xt + gt * acc

    def res_extras(xin, gate):
        return [(xin.reshape(t, d), pl.BlockSpec((tm_res, 512), lambda i, j, kk: (i, j))),
                (gate, pl.BlockSpec((None, 1, 512), lambda i, j, kk: (i * tm_res // s, 0, j)))]

    w1_blk = lambda tk, tn: pl.BlockSpec((None, tk, tn), lambda i, j, kk: (j // (dff4 // tn), kk, j % (dff4 // tn)))

    saved = []
    xcur = x
    for l in range(nl):
        sh1, sc1, gt1, sh2, sc2, gt2 = mods[l]
        if l == 0:
            win = as_win(fetch(0, mod)[0])
        else:
            g_in, g_out, w1, g_2 = fetch(3 + l, xcur)
            win, wout, w2 = as_win(g_in), g_out.reshape(-1, d), g_2.reshape(dff, d)
        prm_a = (row(gm_ln_g[l]), row(gm_ln_b[l]), gm_ws[l], gm_bs[l].T, row(gm_norm_g[l]))
        prm_b = (row(attn_sinks[l]), row(attn_norm_g[l]))
        prm_c = (pad16(dt_bias[l]), pad16(a_log[l]), pad16(d_skip[l]), row(ssm_norm_g[l]))
        h1 = ln_mod_fwd(xcur, row(norm1_g[l]), sc1, sh1, f"ln1_fwd_{l}")
        p = _mm("nn", h1.reshape(t, d), win, dims=(t, PW, d), tm=1024, tn=512, tk=d, out_dtypes=[F32],
                name=f"proj_in_{l}")[0].reshape(bl, s, PW)
        out_a = gmlp_fwd(p, prm_a, f"gmlp_fwd_{l}")
        out_b = attn_fwd(p, *prm_b, f"attn_fwd_{l}")
        xc = conv_fwd(p, conv_w_full[l], row(conv_b[l]), f"conv_fwd_{l}")
        out_c, states = ssd_fwd(xc, p, prm_c, f"ssd_fwd_{l}")
        mix = jnp.concatenate([out_a, out_b, out_c], axis=-1)
        if l == 0:
            wout = fetch(1, mix)[0].reshape(-1, d)
        mm1, x2 = _mm("nn", mix.reshape(t, d), wout, dims=(t, d, d), tm=tm_res, tn=512, tk=d, out_dtypes=[F32, F32],
                      name=f"proj_out_{l}", extras=res_extras(xcur, gt1), epi=residual)
        x2 = x2.reshape(bl, s, d)
        h2 = ln_mod_fwd(x2, row(norm2_g[l]), sc2, sh2, f"ln2_fwd_{l}")
        if l == 0:
            w1 = fetch(2, h2)[0]
        a1 = _mm("nn", h2.reshape(t, d), w1, dims=(t, dff, d), tm=1024, tn=512, tk=d, out_dtypes=[BF16],
                 name=f"mlp1_{l}", b_spec=w1_blk(d, 512))[0]
        if l == 0:
            w2 = fetch(3, a1)[0].reshape(dff, d)
        mm2, x3 = _mm("nn", a1, w2, dims=(t, d, dff), tm=tm_res, tn=512, tk=2048, out_dtypes=[F32, F32],
                      name=f"mlp2_{l}", extras=res_extras(x2, gt2), epi=residual, pro_a=_relu2)
        x3 = x3.reshape(bl, s, d)
        wfull[l] = (win, wout, w1, w2)
        saved.append((xcur, h1, p, xc, states, mix, mm1.reshape(bl, s, d), x2, h2, a1, mm2.reshape(bl, s, d),
                      prm_a, prm_b, prm_c))
        xcur = x3

    dx, d_final_g, loss_part = loss_head(xcur, row(final_norm_g), loss_target, "loss_head")
    loss = lax.psum(loss_part[0, 0], ("x", "y", "c"))

    def rs_begin(grads, tag):
        theirs = swap_halves(grads, f"rs_swap_{tag}")
        sums = [add_pair(g, th, ci, f"rs_add_{tag}_{i}") for i, (g, th) in enumerate(zip(grads, theirs))]
        ss, rs, srcs, lands, token = split_copy_start(
            [sm[1] for sm in sums], [jax.ShapeDtypeStruct((3,) + sm[1].shape[1:], BF16) for sm in sums],
            _scatter_copies, sums[0][0], f"rs_start_{tag}")
        return (ss, rs, srcs, lands, [sm[0] for sm in sums]), token

    def rs_end(state, behind, tag):
        ss, rs, srcs, lands, sums_f32 = state
        _, got = split_copy_wait(ss, rs, srcs, lands, _scatter_copies, behind, f"rs_wait_{tag}")
        halves = [sum_own_recv(sf, g, chip, f"rs_sum_{tag}_{i}") for i, (sf, g) in enumerate(zip(sums_f32, got))]
        return join_halves(halves, f"rs_join_{tag}")

    small_parts = [None] * nl
    dmods = [None] * nl
    reduced = [None] * nl
    rs_state, rs_token = None, None
    for l in reversed(range(nl)):
        sh1, sc1, gt1, sh2, sc2, gt2 = mods[l]
        win, wout, w1, w2 = wfull[l]
        xin, h1, p, xc, states, mix, mm1, x2, h2, a1, mm2, prm_a, prm_b, prm_c = saved[l]
        if rs_token is not None:
            gt2 = gt2 + rs_token[0, 0]
        dm2, dgt2 = gate_bwd(dx, mm2, gt2, f"gate2_bwd_{l}")
        dm2 = dm2.reshape(t, d)
        da1 = _mm("nt", dm2, w2, dims=(t, dff, d), tm=1024, tn=512, tk=d, out_dtypes=[BF16], name=f"mlp2_dx_{l}",
                  extras=[(a1, pl.BlockSpec((1024 if t >= 1024 else t, 512), lambda i, j, kk: (i, j)))],
                  epi=lambda acc, av: (acc * (2.0 * jnp.maximum(av, 0).astype(F32)),))[0]
        dw2 = _mm("tn", a1, dm2, dims=(dff, d, t), tm=512, tn=2048, tk=2048, out_dtypes=[F32], name=f"mlp2_dw_{l}",
                  pro_a=_relu2)[0]
        dw1 = _mm("tn", h2.reshape(t, d), da1, dims=(d, dff, t), tm=512, tn=dff4, tk=2048, out_dtypes=[F32],
                  name=f"mlp1_dw_{l}", out_shapes=[(4, d, dff4)],
                  out_specs=[pl.BlockSpec((None, 512, dff4), lambda i, j, kk: (j, i, 0))])[0]
        dh2 = _mm("nt", da1, w1, dims=(t, d, dff), tm=1024, tn=512, tk=2048, out_dtypes=[F32], name=f"mlp1_dx_{l}",
                  b_spec=pl.BlockSpec((None, 512, 2048 if dff4 >= 2048 else dff4),
                                      lambda i, j, kk: (kk // (dff4 // min(2048, dff4)), j, kk % (dff4 // min(2048, dff4)))))[0]
        dx2, dsc2, dsh2, dn2 = ln_mod_bwd(dh2.reshape(bl, s, d), x2, dx, row(norm2_g[l]), sc2, f"ln2_bwd_{l}")
        dm1, dgt1 = gate_bwd(dx2, mm1, gt1, f"gate1_bwd_{l}")
        dm1 = dm1.reshape(t, d)
        dmix = _mm("nt", dm1, wout, dims=(t, d, d), tm=1024, tn=512, tk=d, out_dtypes=[F32],
                   name=f"proj_out_dx_{l}")[0].reshape(bl, s, d)
        dwout = _mm("tn", mix.reshape(t, d), dm1, dims=(d, d, t), tm=512, tn=2048, tk=2048, out_dtypes=[F32],
                    name=f"proj_out_dw_{l}")[0]
        du, dv, dlg, dlb, dws, dbst, dgng = gmlp_bwd(p, dmix, prm_a, f"gmlp_bwd_{l}")
        dq, dk, dvv, dsinks, dang = attn_bwd(p, dmix, *prm_b, f"attn_bwd_{l}")
        dxc, ddt, dz, ddtb, dalog, ddsk, dsng = ssd_bwd(xc, p, states, dmix, prm_c, f"ssd_bwd_{l}")
        dxbc, dcw, dcb = conv_bwd(p, dxc, conv_w_full[l], row(conv_b[l]), f"conv_bwd_{l}")
        dp = jnp.concatenate([dxbc, dq, dz, du, dv, dk, dvv, ddt, jnp.zeros((bl, s, PW - OFF_DT - LANES), BF16)],
                             axis=-1).reshape(t, PW)
        dwin = _mm("tn", h1.reshape(t, d), dp, dims=(d, PW, t), tm=512, tn=PW // 3, tk=2048, out_dtypes=[F32],
                   name=f"proj_in_dw_{l}")[0]
        dh1 = _mm("nt", dp, win, dims=(t, d, PW), tm=1024, tn=512, tk=PW // 2, out_dtypes=[F32],
                  name=f"proj_in_dx_{l}")[0]
        dx, dsc1, dsh1, dn1 = ln_mod_bwd(dh1.reshape(bl, s, d), xin, dx2, row(norm1_g[l]), sc1, f"ln1_bwd_{l}")
        dmods[l] = jnp.concatenate([dsh1, dsc1, dgt1, dsh2, dsc2, dgt2], axis=-1).reshape(bl, 6 * d)
        small_parts[l] = [dn1, dlg, dlb, dws, dbst.T, dgng, dsinks, dang, dcw, dcb, ddtb[:, :SSM_HEADS],
                          dalog[:, :SSM_HEADS], ddsk[:, :SSM_HEADS], dsng, dn2]
        dwin_blocks = _win_from_kernel_layout(dwin).reshape(d, 4, IN_W // 4).transpose(1, 0, 2)
        if rs_state is not None:
            reduced[l + 1] = rs_end(rs_state, dx, str(l + 1))
        rs_state, rs_token = rs_begin([dwin_blocks, dwout.reshape(4, d // 4, d), dw1, dw2.reshape(4, dff4, d)], str(l))
    grad_x = dx

    big = [(w_in, m_w_in, v_w_in), (w_out, m_w_out, v_w_out), (w_mlp1, m_w_mlp1, v_w_mlp1), (w_mlp2, m_w_mlp2, v_w_mlp2)]
    big_out = [None] * 4
    for l in reversed(range(1, nl)):
        for i, (wt, mt, vt) in enumerate(big):
            big_out[i] = adamw_layer(wt, mt, vt, reduced[l][i], l, big_out[i], f"adamw_big_{i}_{l}")

    small_names = [norm1_g, gm_ln_g, gm_ln_b, gm_ws, gm_bs, gm_norm_g, attn_sinks, attn_norm_g, None, conv_b, dt_bias,
                   a_log, d_skip, ssm_norm_g, norm2_g]
    n_small = len(small_names)
    per_param = [jnp.stack([small_parts[l][i].reshape(-1) for l in range(nl)]) for i in range(n_small)]
    small_vec = _pack(per_param + [d_final_g]) + rs_token[0, 0]
    rs_small = small_vec.shape[0]
    dmod_local = jnp.stack(dmods, axis=1)
    g2 = all_gather_small(jnp.concatenate([small_vec, _pack([dmod_local])], axis=0), "ag_small")
    g_small = sum_devices(g2[:, :rs_small, :], "sum_small")
    dmod_all = g2[:, rs_small:, :].reshape(8, -1)[:, :bl * nl * 6 * d].reshape(nex, nl * 6 * d)
    g_ada_b = sum_devices(dmod_all.reshape(nex, -1, LANES), "sum_ada_b").reshape(nl, 6 * d)
    shapes = [(nl, int(np.prod(small_parts[0][i].shape))) for i in range(n_small)] + [(d,)]
    g_list = _unpack(g_small, shapes)
    g_conv_w = lax.dynamic_slice(g_list[8].reshape(nl, CONV_K, CCH), (0, 0, chip * cw_w), (nl, CONV_K, cw_w))

    g_ada_w = []
    for l in range(nl):
        dm_cols = lax.dynamic_slice(dmod_all.reshape(nex, nl, 6 * d)[:, l, :], (0, chip * mod_w), (nex, mod_w))
        g_ada_w.append(_mm("tn", c_all, dm_cols, dims=(d, mod_w, nex), tm=512, tn=512, tk=nex, out_dtypes=[F32],
                           name=f"ada_w_grad_{l}", pro_a=c_act, pro_b=to_bf16)[0])
    g_ada_w = jnp.stack(g_ada_w)
    d_ada_w, m_ada_w_n, v_ada_w_n = [a.reshape(ada_w.shape) for a in
                                     adamw(_rows2d(ada_w), _rows2d(m_ada_w), _rows2d(v_ada_w), _rows2d(g_ada_w), "adamw_ada_w")]

    smalls = {
        "ada_b": (ada_b, m_ada_b, v_ada_b, g_ada_b), "norm1_g": (norm1_g, m_norm1_g, v_norm1_g, g_list[0]),
        "gm_ln_g": (gm_ln_g, m_gm_ln_g, v_gm_ln_g, g_list[1]), "gm_ln_b": (gm_ln_b, m_gm_ln_b, v_gm_ln_b, g_list[2]),
        "gm_ws": (gm_ws, m_gm_ws, v_gm_ws, g_list[3]), "gm_bs": (gm_bs, m_gm_bs, v_gm_bs, g_list[4]),
        "gm_norm_g": (gm_norm_g, m_gm_norm_g, v_gm_norm_g, g_list[5]),
        "attn_sinks": (attn_sinks, m_attn_sinks, v_attn_sinks, g_list[6]),
        "attn_norm_g": (attn_norm_g, m_attn_norm_g, v_attn_norm_g, g_list[7]),
        "conv_w": (conv_w, m_conv_w, v_conv_w, g_conv_w), "conv_b": (conv_b, m_conv_b, v_conv_b, g_list[9]),
        "dt_bias": (dt_bias, m_dt_bias, v_dt_bias, g_list[10]), "a_log": (a_log, m_a_log, v_a_log, g_list[11]),
        "d_skip": (d_skip, m_d_skip, v_d_skip, g_list[12]),
        "ssm_norm_g": (ssm_norm_g, m_ssm_norm_g, v_ssm_norm_g, g_list[13]),
        "norm2_g": (norm2_g, m_norm2_g, v_norm2_g, g_list[14]),
        "final_norm_g": (final_norm_g, m_final_norm_g, v_final_norm_g, g_list[15]),
    }
    keys = list(smalls)
    wv, mv, vv_, gv = [_pack([smalls[k][i].reshape(smalls[k][0].shape) for k in keys]) for i in range(4)]
    sd_, sm_, sv_ = adamw(wv, mv, vv_, gv, "adamw_small")
    shp = [smalls[k][0].shape for k in keys]
    small_out = {k: (smalls[k][3].reshape(smalls[k][0].shape), a, b, cc)
                 for k, a, b, cc in zip(keys, _unpack(sd_, shp), _unpack(sm_, shp), _unpack(sv_, shp))}

    late = jnp.zeros((8, LANES), F32) + (sv_[0, 0] + v_ada_w_n[0, 0, 0])
    for bo in big_out:
        if bo is not None:
            late = late + bo[3][nl - 1, 0, 0]
    reduced[0] = rs_end(rs_state, late, "0")
    for i, (wt, mt, vt) in enumerate(big):
        big_out[i] = adamw_layer(wt, mt, vt, reduced[0][i], 0, big_out[i], f"adamw_big_{i}_0")

    out = {"ada_w": (g_ada_w, d_ada_w, m_ada_w_n, v_ada_w_n), "w_in": big_out[0], "w_out": big_out[1],
           "w_mlp1": big_out[2], "w_mlp2": big_out[3], **small_out}
    order = ["ada_w", "ada_b", "norm1_g", "w_in", "gm_ln_g", "gm_ln_b", "gm_ws", "gm_bs", "gm_norm_g", "attn_sinks",
             "attn_norm_g", "conv_w", "conv_b", "dt_bias", "a_log", "d_skip", "ssm_norm_g", "w_out", "norm2_g",
             "w_mlp1", "w_mlp2", "final_norm_g"]
    return (loss, grad_x, *[out[k][0] for k in order], *[out[k][1] for k in order],
            *[out[k][2] for k in order], *[out[k][3] for k in order])
```

```python
import functools
import math

import jax
import jax.numpy as jnp
import numpy as np
from jax import lax
from jax.experimental import pallas as pl
from jax.experimental.pallas import tpu as pltpu

F32 = jnp.float32
BF16 = jnp.bfloat16
HI = lax.Precision.HIGHEST
MESH = pl.DeviceIdType.MESH

CHUNK = 128
GM_HEADS, GM_HD = 4, 128
ATT_HEADS, ATT_KV, ATT_HD = 8, 2, 64
WINDOW = 128
SSM_HEADS, SSM_HD, SSM_GROUPS, SSM_STATE, CONV_K = 16, 64, 2, 128, 4
EPS = 1e-6
LN_EPS = 1e-5
NEG = -1e30
LANES = 128

GMW = GM_HEADS * GM_HD
ATW = ATT_HEADS * ATT_HD
KVW = ATT_KV * ATT_HD
SSW = SSM_HEADS * SSM_HD
BCW = SSM_GROUPS * SSM_STATE
CCH = SSW + 2 * BCW
GRW = SSW // SSM_GROUPS
IN_SIZES = (GMW, GMW, ATW, KVW, KVW, SSW, CCH, SSM_HEADS)
IN_W = sum(IN_SIZES)
OFF_XBC, OFF_Q, OFF_Z, OFF_U, OFF_V, OFF_K, OFF_VV, OFF_DT = 0, 1536, 2048, 3072, 3584, 4096, 4224, 4352
PW = 4608

ADAM_LR, ADAM_B1, ADAM_B2, ADAM_EPS, ADAM_WD, ADAM_STEP = 0.001, 0.9, 0.999, 1e-08, 0.01, 10

VMEM_LIMIT = 56 * 1024 * 1024


def _cp(sem=None):
    return pltpu.CompilerParams(dimension_semantics=sem, vmem_limit_bytes=VMEM_LIMIT)


_DN = {"nn": (((1,), (0,)), ((), ())), "nt": (((1,), (1,)), ((), ())), "tn": (((0,), (0,)), ((), ()))}


def _dot(form, a, b):
    return lax.dot_general(a.astype(BF16), b.astype(BF16), _DN[form], preferred_element_type=F32)


@jax.custom_vjp
def _nn(a, b):
    return _dot("nn", a, b)


@jax.custom_vjp
def _nt(a, b):
    return _dot("nt", a, b)


@jax.custom_vjp
def _tn(a, b):
    return _dot("tn", a, b)


_nn.defvjp(lambda a, b: (_dot("nn", a, b), (a, b)), lambda r, g: (_dot("nt", g, r[1]), _dot("tn", r[0], g)))
_nt.defvjp(lambda a, b: (_dot("nt", a, b), (a, b)), lambda r, g: (_dot("nn", g, r[1]), _dot("tn", g, r[0])))
_tn.defvjp(lambda a, b: (_dot("tn", a, b), (a, b)), lambda r, g: (_dot("nt", r[1], g), _dot("nn", r[0], g)))


def _hdot(a, b):
    return jnp.dot(a, b, precision=HI, preferred_element_type=F32)


def _silu(x):
    return x * (1.0 / (1.0 + jnp.exp(-x)))


def _softplus(x):
    return jnp.maximum(x, 0.0) + jnp.log1p(jnp.exp(-jnp.abs(x)))


def _gelu(x):
    return 0.5 * x * (1.0 + jnp.tanh(math.sqrt(2.0 / math.pi) * (x + 0.044715 * (x * x * x))))


def _rms(y, g):
    return y * lax.rsqrt(jnp.mean(y * y, axis=-1, keepdims=True) + EPS) * g


def _mm(form, a, b, *, dims, tm, tn, tk, out_dtypes, name, a_spec=None, b_spec=None, out_specs=None,
        out_shapes=None, extras=(), epi=None, pro_a=None, pro_b=None):
    m, n, k = dims
    tm, tn, tk = min(tm, m), min(tn, n), min(tk, k)
    assert m % tm == 0 and n % tn == 0 and k % tk == 0, (name, dims, tm, tn, tk)
    nk = k // tk
    if a_spec is None:
        a_spec = (pl.BlockSpec((tk, tm), lambda i, j, kk: (kk, i)) if form == "tn"
                  else pl.BlockSpec((tm, tk), lambda i, j, kk: (i, kk)))
    if b_spec is None:
        b_spec = (pl.BlockSpec((tn, tk), lambda i, j, kk: (j, kk)) if form == "nt"
                  else pl.BlockSpec((tk, tn), lambda i, j, kk: (kk, j)))
    n_out = len(out_dtypes)
    if out_specs is None:
        out_specs = [pl.BlockSpec((tm, tn), lambda i, j, kk: (i, j))] * n_out
    if out_shapes is None:
        out_shapes = [(m, n)] * n_out
    ne = len(extras)

    def body(*refs):
        a_ref, b_ref = refs[0], refs[1]
        ex = refs[2:2 + ne]
        outs = refs[2 + ne:2 + ne + n_out]

        def write(val):
            res = epi(val, *[e[...] for e in ex]) if epi is not None else (val,)
            for o, r in zip(outs, res):
                o[...] = r.astype(o.dtype)

        av = a_ref[...]
        if pro_a is not None:
            av = pro_a(av)
        bv = b_ref[...]
        if pro_b is not None:
            bv = pro_b(bv)
        part = lax.dot_general(av, bv, _DN[form], preferred_element_type=F32)
        if nk == 1:
            write(part)
        else:
            acc = refs[-1]
            kk = pl.program_id(2)

            @pl.when(kk == 0)
            def _():
                acc[...] = part

            @pl.when(kk > 0)
            def _():
                acc[...] += part

            @pl.when(kk == nk - 1)
            def _():
                write(acc[...])

    res = pl.pallas_call(
        body, name=name, grid=(m // tm, n // tn, nk),
        in_specs=[a_spec, b_spec] + [s for _, s in extras],
        out_specs=out_specs,
        out_shape=[jax.ShapeDtypeStruct(s, d) for s, d in zip(out_shapes, out_dtypes)],
        scratch_shapes=[pltpu.VMEM((tm, tn), F32)] if nk > 1 else [],
        compiler_params=_cp(("parallel", "parallel", "arbitrary")),
    )(a, b, *[e for e, _ in extras])
    return res


def _row_tile(s):
    return min(512, s)


def ln_mod_fwd(x, g, sc, sh, name):
    bsz, s, d = x.shape
    ts = _row_tile(s)

    def body(x_ref, g_ref, sc_ref, sh_ref, o_ref):
        xv = x_ref[...]
        r = lax.rsqrt(jnp.mean(xv * xv, axis=-1, keepdims=True) + EPS)
        o_ref[...] = ((xv * r * g_ref[...]) * (1.0 + sc_ref[...]) + sh_ref[...]).astype(o_ref.dtype)

    row = pl.BlockSpec((None, ts, d), lambda b, i: (b, i, 0))
    vec = pl.BlockSpec((None, 1, d), lambda b, i: (b, 0, 0))
    return pl.pallas_call(
        body, name=name, grid=(bsz, s // ts),
        in_specs=[row, pl.BlockSpec((1, d), lambda b, i: (0, 0)), vec, vec],
        out_specs=row, out_shape=jax.ShapeDtypeStruct(x.shape, BF16),
        compiler_params=_cp(("parallel", "parallel")),
    )(x, g, sc, sh)


def ln_mod_bwd(dh, x, dres, g, sc, name):
    bsz, s, d = x.shape
    ts = _row_tile(s)

    def body(dh_ref, x_ref, dres_ref, g_ref, sc_ref, dx_ref, dsc_ref, dsh_ref, dg_ref):
        b, i = pl.program_id(0), pl.program_id(1)
        xv, dhv, gv = x_ref[...], dh_ref[...], g_ref[...]
        r = lax.rsqrt(jnp.mean(xv * xv, axis=-1, keepdims=True) + EPS)
        xn = xv * r
        a = dhv * (1.0 + sc_ref[...])
        dxn = a * gv
        dx_ref[...] = dres_ref[...] + r * (dxn - xn * jnp.mean(dxn * xn, axis=-1, keepdims=True))
        p_sc = jnp.sum(dhv * (xn * gv), axis=0, keepdims=True)
        p_sh = jnp.sum(dhv, axis=0, keepdims=True)
        p_g = jnp.sum(a * xn, axis=0, keepdims=True)

        @pl.when(i == 0)
        def _():
            dsc_ref[...] = p_sc
            dsh_ref[...] = p_sh

        @pl.when(i > 0)
        def _():
            dsc_ref[...] += p_sc
            dsh_ref[...] += p_sh

        @pl.when((i == 0) & (b == 0))
        def _():
            dg_ref[...] = p_g

        @pl.when((i > 0) | (b > 0))
        def _():
            dg_ref[...] += p_g

    row = pl.BlockSpec((None, ts, d), lambda b, i: (b, i, 0))
    vec = pl.BlockSpec((None, 1, d), lambda b, i: (b, 0, 0))
    one = pl.BlockSpec((1, d), lambda b, i: (0, 0))
    return pl.pallas_call(
        body, name=name, grid=(bsz, s // ts),
        in_specs=[row, row, row, one, vec],
        out_specs=[row, vec, vec, one],
        out_shape=[jax.ShapeDtypeStruct(x.shape, F32), jax.ShapeDtypeStruct((bsz, 1, d), F32),
                   jax.ShapeDtypeStruct((bsz, 1, d), F32), jax.ShapeDtypeStruct((1, d), F32)],
        compiler_params=_cp(("arbitrary", "arbitrary")),
    )(dh, x, dres, g, sc)


def gate_bwd(dx, mm, gate, name):
    bsz, s, d = dx.shape
    ts = _row_tile(s)

    def body(dx_ref, m_ref, g_ref, dm_ref, dg_ref):
        i = pl.program_id(1)
        dxv = dx_ref[...]
        dm_ref[...] = (dxv * g_ref[...]).astype(dm_ref.dtype)
        p = jnp.sum(dxv * m_ref[...], axis=0, keepdims=True)

        @pl.when(i == 0)
        def _():
            dg_ref[...] = p

        @pl.when(i > 0)
        def _():
            dg_ref[...] += p

    row = pl.BlockSpec((None, ts, d), lambda b, i: (b, i, 0))
    vec = pl.BlockSpec((None, 1, d), lambda b, i: (b, 0, 0))
    return pl.pallas_call(
        body, name=name, grid=(bsz, s // ts),
        in_specs=[row, row, vec], out_specs=[row, vec],
        out_shape=[jax.ShapeDtypeStruct(dx.shape, BF16), jax.ShapeDtypeStruct((bsz, 1, d), F32)],
        compiler_params=_cp(("parallel", "arbitrary")),
    )(dx, mm, gate)


def loss_head(x, g, tgt, name):
    bsz, s, d = x.shape
    ts = _row_tile(s)

    def body(x_ref, g_ref, t_ref, dx_ref, dg_ref, l_ref):
        b, i = pl.program_id(0), pl.program_id(1)
        xv, gv = x_ref[...], g_ref[...]
        r = lax.rsqrt(jnp.mean(xv * xv, axis=-1, keepdims=True) + EPS)
        xn = xv * r
        e = xn * gv - t_ref[...]
        dy = e * (1.0 / d)
        dxn = dy * gv
        dx_ref[...] = r * (dxn - xn * jnp.mean(dxn * xn, axis=-1, keepdims=True))
        p_g = jnp.sum(dy * xn, axis=0, keepdims=True)
        p_l = jnp.zeros((1, LANES), F32) + jnp.sum(e * e) * (0.5 / d)
        first = (i == 0) & (b == 0)

        @pl.when(first)
        def _():
            dg_ref[...] = p_g
            l_ref[...] = p_l

        @pl.when(jnp.logical_not(first))
        def _():
            dg_ref[...] += p_g
            l_ref[...] += p_l

    row = pl.BlockSpec((None, ts, d), lambda b, i: (b, i, 0))
    one = pl.BlockSpec((1, d), lambda b, i: (0, 0))
    return pl.pallas_call(
        body, name=name, grid=(bsz, s // ts),
        in_specs=[row, one, row],
        out_specs=[row, one, pl.BlockSpec((1, LANES), lambda b, i: (0, 0))],
        out_shape=[jax.ShapeDtypeStruct(x.shape, F32), jax.ShapeDtypeStruct((1, d), F32),
                   jax.ShapeDtypeStruct((1, LANES), F32)],
        compiler_params=_cp(("arbitrary", "arbitrary")),
    )(x, g, tgt)


def _gmlp_chunk(u_raw, v_raw, ln_g, ln_b, w, bs_t, out_g):
    c = u_raw.shape[0]
    u, v = _gelu(u_raw), _gelu(v_raw)
    tril = lax.broadcasted_iota(jnp.int32, (c, c), 0) >= lax.broadcasted_iota(jnp.int32, (c, c), 1)
    ys = []
    for h in range(GM_HEADS):
        sl = slice(h * GM_HD, (h + 1) * GM_HD)
        vh = v[:, sl]
        xc = vh - jnp.mean(vh, axis=-1, keepdims=True)
        vn = xc * lax.rsqrt(jnp.mean(xc * xc, axis=-1, keepdims=True) + LN_EPS) * ln_g[:, sl] + ln_b[:, sl]
        gate = _nn(jnp.where(tril, w[h], 0.0), vn) + bs_t[:, h:h + 1]
        ys.append(u[:, sl] * gate)
    return _rms(jnp.concatenate(ys, axis=1), out_g)


def _gmlp_specs(bsz, nc):
    seg = lambda off: pl.BlockSpec((None, CHUNK, GMW), lambda b, c: (b, c, off // GMW))
    full = lambda shape: pl.BlockSpec(shape, lambda b, c: (0,) * len(shape))
    par = [full((1, GMW)), full((1, GMW)), full((GM_HEADS, CHUNK, CHUNK)), full((CHUNK, GM_HEADS)), full((1, GMW))]
    return seg, full, par


def gmlp_fwd(p, prm, name):
    bsz, s, _ = p.shape
    nc = s // CHUNK
    seg, _, par = _gmlp_specs(bsz, nc)

    def body(u_ref, v_ref, lg, lb, w, bt, og, o_ref):
        o_ref[...] = _gmlp_chunk(u_ref[...], v_ref[...], lg[...], lb[...], w[...], bt[...], og[...]).astype(o_ref.dtype)

    return pl.pallas_call(
        body, name=name, grid=(bsz, nc),
        in_specs=[seg(OFF_U), seg(OFF_V)] + par,
        out_specs=pl.BlockSpec((None, CHUNK, GMW), lambda b, c: (b, c, 0)),
        out_shape=jax.ShapeDtypeStruct((bsz, s, GMW), BF16),
        compiler_params=_cp(("parallel", "parallel")),
    )(p, p, *prm)


def _accumulate(first, refs, vals):
    @pl.when(first)
    def _():
        for r, v in zip(refs, vals):
            r[...] = v

    @pl.when(jnp.logical_not(first))
    def _():
        for r, v in zip(refs, vals):
            r[...] += v


def gmlp_bwd(p, dmix, prm, name):
    bsz, s, _ = p.shape
    nc = s // CHUNK
    seg, full, par = _gmlp_specs(bsz, nc)

    def body(u_ref, v_ref, do_ref, lg, lb, w, bt, og, du_ref, dv_ref, *dpar):
        first = (pl.program_id(0) == 0) & (pl.program_id(1) == 0)
        _, vjp = jax.vjp(_gmlp_chunk, u_ref[...], v_ref[...], lg[...], lb[...], w[...], bt[...], og[...])
        gr = vjp(do_ref[...])
        du_ref[...] = gr[0].astype(du_ref.dtype)
        dv_ref[...] = gr[1].astype(dv_ref.dtype)
        _accumulate(first, dpar, gr[2:])

    out_seg = pl.BlockSpec((None, CHUNK, GMW), lambda b, c: (b, c, 0))
    return pl.pallas_call(
        body, name=name, grid=(bsz, nc),
        in_specs=[seg(OFF_U), seg(OFF_V), out_seg] + par,
        out_specs=[out_seg, out_seg] + par,
        out_shape=[jax.ShapeDtypeStruct((bsz, s, GMW), BF16)] * 2 + [jax.ShapeDtypeStruct(x.shape, F32) for x in prm],
        compiler_params=_cp(("arbitrary", "arbitrary")),
    )(p, p, dmix, *prm)


@jax.custom_vjp
def _swap_lane_halves(x):
    return pltpu.roll(x, x.shape[1] // 2, 1)


_swap_lane_halves.defvjp(lambda x: (_swap_lane_halves(x), None), lambda _, g: (_swap_lane_halves(g),))


def _attn_block(q, kp, kc, vp, vc, sinks, out_g, has_prev):
    w = q.shape[0]
    k2 = jnp.concatenate([kp, kc], axis=0)
    v2 = jnp.concatenate([vp, vc], axis=0)
    qi = lax.broadcasted_iota(jnp.int32, (w, 2 * w), 0)
    kj = lax.broadcasted_iota(jnp.int32, (w, 2 * w), 1)
    diff = qi + w - kj
    grp = ATT_HEADS // ATT_KV
    valid = (diff >= 0) & (diff < w) & ((kj >= w) | has_prev)
    valid = jnp.concatenate([valid] * grp, axis=0)
    low = lax.broadcasted_iota(jnp.int32, k2.shape, 1) < ATT_HD
    outs = []
    for kv in range(ATT_KV):
        keep = low if kv == 0 else jnp.logical_not(low)
        k_own, v_own = jnp.where(keep, k2, 0.0), jnp.where(keep, v2, 0.0)
        k_oth, v_oth = _swap_lane_halves(k_own), _swap_lane_halves(v_own)
        k_lo, k_hi = (k_own, k_oth) if kv == 0 else (k_oth, k_own)
        v_lo, v_hi = (v_own, v_oth) if kv == 0 else (v_oth, v_own)
        heads = list(range(kv * grp, (kv + 1) * grp))
        blocks = sorted({h // 2 for h in heads})
        qb = jnp.concatenate([q[:, b * 2 * ATT_HD:(b + 1) * 2 * ATT_HD] for b in blocks], axis=0)
        sink = jnp.concatenate([jnp.broadcast_to(sinks[:, h:h + 1], (w, 1)) for h in heads[0::2] + heads[1::2]], axis=0)
        sc = jnp.concatenate([_nt(qb, k_lo), _nt(qb, k_hi)], axis=0) * (ATT_HD ** -0.5)
        sc = jnp.where(valid, sc, NEG)
        m = jnp.maximum(jnp.max(sc, axis=-1, keepdims=True), sink)
        e = jnp.exp(sc - m)
        pr = e / (jnp.sum(e, axis=-1, keepdims=True) + jnp.exp(sink - m))
        half = pr.shape[0] // 2
        o = _nn(pr[:half], v_lo) + _nn(pr[half:], v_hi)
        outs += [o[i * w:(i + 1) * w] for i in range(len(blocks))]
    return _rms(jnp.concatenate(outs, axis=1), out_g)


def attn_fwd(p, sinks, out_g, name):
    bsz, s, _ = p.shape
    nb = s // WINDOW

    def body(q_ref, kp_ref, kc_ref, vp_ref, vc_ref, s_ref, g_ref, o_ref):
        o_ref[...] = _attn_block(q_ref[...], kp_ref[...], kc_ref[...], vp_ref[...], vc_ref[...], s_ref[...],
                                 g_ref[...], pl.program_id(1) > 0).astype(o_ref.dtype)

    cur = lambda off: pl.BlockSpec((None, WINDOW, KVW), lambda b, n: (b, n, off // KVW))
    prev = lambda off: pl.BlockSpec((None, WINDOW, KVW), lambda b, n: (b, jnp.maximum(n - 1, 0), off // KVW))
    return pl.pallas_call(
        body, name=name, grid=(bsz, nb),
        in_specs=[pl.BlockSpec((None, WINDOW, ATW), lambda b, n: (b, n, OFF_Q // ATW)),
                  prev(OFF_K), cur(OFF_K), prev(OFF_VV), cur(OFF_VV),
                  pl.BlockSpec((1, ATT_HEADS), lambda b, n: (0, 0)), pl.BlockSpec((1, ATW), lambda b, n: (0, 0))],
        out_specs=pl.BlockSpec((None, WINDOW, ATW), lambda b, n: (b, n, 0)),
        out_shape=jax.ShapeDtypeStruct((bsz, s, ATW), BF16),
        compiler_params=_cp(("parallel", "parallel")),
    )(p, p, p, p, p, sinks, out_g)


def attn_bwd(p, dmix, sinks, out_g, name):
    bsz, s, _ = p.shape
    nb = s // WINDOW

    def body(q_ref, kp_ref, kc_ref, vp_ref, vc_ref, do_ref, s_ref, g_ref,
             dq_ref, dk_ref, dv_ref, ds_ref, dg_ref, ck, cv):
        b, n = pl.program_id(0), pl.program_id(1)

        @pl.when(n == 0)
        def _():
            ck[...] = jnp.zeros_like(ck)
            cv[...] = jnp.zeros_like(cv)

        @pl.when(n < nb)
        def _():
            fn = functools.partial(_attn_block, has_prev=n > 0)
            _, vjp = jax.vjp(fn, q_ref[...], kp_ref[...], kc_ref[...], vp_ref[...], vc_ref[...], s_ref[...], g_ref[...])
            dq, dkp, dkc, dvp, dvc, dsk, dgg = vjp(do_ref[...])
            dq_ref[...] = dq.astype(dq_ref.dtype)
            dk_ref[...] = (ck[...] + dkp).astype(dk_ref.dtype)
            dv_ref[...] = (cv[...] + dvp).astype(dv_ref.dtype)
            ck[...] = dkc
            cv[...] = dvc
            _accumulate((b == 0) & (n == 0), (ds_ref, dg_ref), (dsk, dgg))

        @pl.when(n == nb)
        def _():
            dk_ref[...] = ck[...].astype(dk_ref.dtype)
            dv_ref[...] = cv[...].astype(dv_ref.dtype)

    at = lambda n: jnp.minimum(n, nb - 1)
    cur = lambda off: pl.BlockSpec((None, WINDOW, KVW), lambda b, n: (b, at(n), off // KVW))
    prev = lambda off: pl.BlockSpec((None, WINDOW, KVW), lambda b, n: (b, jnp.maximum(at(n) - 1, 0), off // KVW))
    kv_out = pl.BlockSpec((None, WINDOW, KVW), lambda b, n: (b, jnp.maximum(n - 1, 0), 0))
    return pl.pallas_call(
        body, name=name, grid=(bsz, nb + 1),
        in_specs=[pl.BlockSpec((None, WINDOW, ATW), lambda b, n: (b, at(n), OFF_Q // ATW)),
                  prev(OFF_K), cur(OFF_K), prev(OFF_VV), cur(OFF_VV),
                  pl.BlockSpec((None, WINDOW, ATW), lambda b, n: (b, at(n), GMW // ATW)),
                  pl.BlockSpec((1, ATT_HEADS), lambda b, n: (0, 0)), pl.BlockSpec((1, ATW), lambda b, n: (0, 0))],
        out_specs=[pl.BlockSpec((None, WINDOW, ATW), lambda b, n: (b, at(n), 0)), kv_out, kv_out,
                   pl.BlockSpec((1, ATT_HEADS), lambda b, n: (0, 0)), pl.BlockSpec((1, ATW), lambda b, n: (0, 0))],
        out_shape=[jax.ShapeDtypeStruct((bsz, s, ATW), BF16), jax.ShapeDtypeStruct((bsz, s, KVW), BF16),
                   jax.ShapeDtypeStruct((bsz, s, KVW), BF16), jax.ShapeDtypeStruct((1, ATT_HEADS), F32),
                   jax.ShapeDtypeStruct((1, ATW), F32)],
        scratch_shapes=[pltpu.VMEM((WINDOW, KVW), F32), pltpu.VMEM((WINDOW, KVW), F32)],
        compiler_params=_cp(("arbitrary", "arbitrary")),
    )(p, p, p, p, p, dmix, sinks, out_g)


CONV_CT = 256


def _shift_down(x, j):
    if j == 0:
        return x
    rows = lax.broadcasted_iota(jnp.int32, x.shape, 0)
    return jnp.where(rows >= j, pltpu.roll(x, j, 0), 0.0)


def _shift_up(x, j):
    if j == 0:
        return x
    s = x.shape[0]
    rows = lax.broadcasted_iota(jnp.int32, x.shape, 0)
    return jnp.where(rows < s - j, pltpu.roll(x, s - j, 0), 0.0)


def conv_fwd(p, w, bias, name):
    bsz, s, _ = p.shape

    def body(x_ref, w_ref, b_ref, o_ref):
        xv, wv = x_ref[...], w_ref[...]
        pre = b_ref[...] + sum(wv[k:k + 1, :] * _shift_down(xv, CONV_K - 1 - k) for k in range(CONV_K))
        o_ref[...] = _silu(pre)

    blk = pl.BlockSpec((None, s, CONV_CT), lambda b, j: (b, 0, j))
    return pl.pallas_call(
        body, name=name, grid=(bsz, CCH // CONV_CT),
        in_specs=[blk, pl.BlockSpec((CONV_K, CONV_CT), lambda b, j: (0, j)), pl.BlockSpec((1, CONV_CT), lambda b, j: (0, j))],
        out_specs=blk, out_shape=jax.ShapeDtypeStruct((bsz, s, CCH), F32),
        compiler_params=_cp(("parallel", "parallel")),
    )(p, w, bias)


def conv_bwd(p, dxc, w, bias, name):
    bsz, s, _ = p.shape

    def body(x_ref, d_ref, w_ref, b_ref, dx_ref, dw_ref, db_ref):
        b = pl.program_id(1)
        xv, wv = x_ref[...], w_ref[...]
        xs = [_shift_down(xv, CONV_K - 1 - k) for k in range(CONV_K)]
        pre = b_ref[...] + sum(wv[k:k + 1, :] * xs[k] for k in range(CONV_K))
        sg = 1.0 / (1.0 + jnp.exp(-pre))
        dpre = d_ref[...] * (sg * (1.0 + pre * (1.0 - sg)))
        dx_ref[...] = sum(wv[k:k + 1, :] * _shift_up(dpre, CONV_K - 1 - k) for k in range(CONV_K)).astype(dx_ref.dtype)
        p_w = jnp.concatenate([jnp.sum(dpre * xs[k], axis=0, keepdims=True) for k in range(CONV_K)], axis=0)
        p_b = jnp.sum(dpre, axis=0, keepdims=True)
        _accumulate(b == 0, (dw_ref, db_ref), (p_w, p_b))

    blk = pl.BlockSpec((None, s, CONV_CT), lambda j, b: (b, 0, j))
    wsp = pl.BlockSpec((CONV_K, CONV_CT), lambda j, b: (0, j))
    bsp = pl.BlockSpec((1, CONV_CT), lambda j, b: (0, j))
    return pl.pallas_call(
        body, name=name, grid=(CCH // CONV_CT, bsz),
        in_specs=[blk, blk, wsp, bsp], out_specs=[blk, wsp, bsp],
        out_shape=[jax.ShapeDtypeStruct((bsz, s, CCH), BF16), jax.ShapeDtypeStruct((CONV_K, CCH), F32),
                   jax.ShapeDtypeStruct((1, CCH), F32)],
        compiler_params=_cp(("parallel", "arbitrary")),
    )(p, dxc, w, bias)


def _ssd_consts():
    c = CHUNK
    r = lax.broadcasted_iota(jnp.int32, (c, c), 0)
    q = lax.broadcasted_iota(jnp.int32, (c, c), 1)
    hrow = lax.broadcasted_iota(jnp.int32, (LANES, SSW), 0)
    hcol = lax.broadcasted_iota(jnp.int32, (LANES, SSW), 1) // SSM_HD
    expand = (hrow == hcol).astype(F32)
    return expand, (r >= q).astype(F32), (r <= q).astype(F32), r >= q


def _ssd_chunk(xc, dtr, z, prev_t, dt_bias, a_log, d_skip, norm_g):
    c = xc.shape[0]
    expand, tril1, triu1, causal = _ssd_consts()
    xs, bm, cm = xc[:, :SSW], xc[:, SSW:SSW + BCW], xc[:, SSW + BCW:]
    dt = _softplus(dtr + dt_bias)
    da = dt * (-jnp.exp(a_log))
    a_cs = _hdot(tril1, da)
    a_cs_t = _hdot(da.T, triu1)
    dt_e = _hdot(dt, expand)
    acs_e = _hdot(a_cs, expand)
    alast_e = acs_e[c - 1:c, :]
    dsk_e = _hdot(jnp.broadcast_to(d_skip, (8, LANES)), expand)[0:1, :]
    xdt = xs * dt_e
    hg = SSM_HEADS // SSM_GROUPS
    ys, new_t = [], []
    for g in range(SSM_GROUPS):
        bg = bm[:, g * SSM_STATE:(g + 1) * SSM_STATE]
        cg = cm[:, g * SSM_STATE:(g + 1) * SSM_STATE]
        sl = slice(g * GRW, (g + 1) * GRW)
        cb = _nt(cg, bg)
        xdt_g = xdt[:, sl]
        st = _tn(bg, xdt_g * jnp.exp(alast_e[:, sl] - acs_e[:, sl]))
        new_t.append(prev_t[:, sl] * jnp.exp(alast_e[:, sl]) + st)
        y_off = _nn(cg, prev_t[:, sl]) * jnp.exp(acs_e[:, sl])
        yd = []
        low = lax.broadcasted_iota(jnp.int32, (c, LANES), 1) < SSM_HD
        for pair in range(hg // 2):
            xp = xdt_g[:, pair * LANES:(pair + 1) * LANES]
            acc = None
            for side, xh in enumerate((jnp.where(low, xp, 0.0), jnp.where(low, 0.0, xp))):
                h = g * hg + 2 * pair + side
                decay = jnp.exp(jnp.where(causal, a_cs[:, h:h + 1] - a_cs_t[h:h + 1, :], NEG))
                part = _nn(cb * decay, xh)
                acc = part if acc is None else acc + part
            yd.append(acc)
        ys.append(jnp.concatenate(yd, axis=1) + y_off)
    y = (jnp.concatenate(ys, axis=1) + xs * dsk_e) * _silu(z)
    yn = [y[:, g * GRW:(g + 1) * GRW] * lax.rsqrt(jnp.mean(jnp.square(y[:, g * GRW:(g + 1) * GRW]), axis=-1, keepdims=True) + EPS)
          for g in range(SSM_GROUPS)]
    return jnp.concatenate(yn, axis=1) * norm_g, jnp.concatenate(new_t, axis=1)


def ssd_fwd(xc, p, prm, name):
    bsz, s, _ = p.shape
    nc = s // CHUNK

    def body(xc_ref, dt_ref, z_ref, db, al, dk, ng, o_ref, st_ref, state):
        @pl.when(pl.program_id(1) == 0)
        def _():
            state[...] = jnp.zeros_like(state)

        prev = state[...]
        st_ref[...] = prev
        out, new = _ssd_chunk(xc_ref[...], dt_ref[...], z_ref[...], prev, db[...], al[...], dk[...], ng[...])
        o_ref[...] = out.astype(o_ref.dtype)
        state[...] = new

    vec = pl.BlockSpec((1, LANES), lambda b, c: (0, 0))
    return pl.pallas_call(
        body, name=name, grid=(bsz, nc),
        in_specs=[pl.BlockSpec((None, CHUNK, CCH), lambda b, c: (b, c, 0)),
                  pl.BlockSpec((None, CHUNK, LANES), lambda b, c: (b, c, OFF_DT // LANES)),
                  pl.BlockSpec((None, CHUNK, SSW), lambda b, c: (b, c, OFF_Z // SSW)),
                  vec, vec, vec, pl.BlockSpec((1, SSW), lambda b, c: (0, 0))],
        out_specs=[pl.BlockSpec((None, CHUNK, SSW), lambda b, c: (b, c, 0)),
                   pl.BlockSpec((None, None, SSM_STATE, SSW), lambda b, c: (b, c, 0, 0))],
        out_shape=[jax.ShapeDtypeStruct((bsz, s, SSW), BF16), jax.ShapeDtypeStruct((bsz, nc, SSM_STATE, SSW), F32)],
        scratch_shapes=[pltpu.VMEM((SSM_STATE, SSW), F32)],
        compiler_params=_cp(("parallel", "arbitrary")),
    )(xc, p, p, *prm)


def ssd_bwd(xc, p, states, dmix, prm, name):
    bsz, s, _ = p.shape
    nc = s // CHUNK

    def body(xc_ref, dt_ref, z_ref, st_ref, do_ref, db, al, dk, ng, dxc_ref, ddt_ref, dz_ref, *rest):
        dpar, dstate = rest[:4], rest[4]
        b, c = pl.program_id(0), pl.program_id(1)

        @pl.when(c == 0)
        def _():
            dstate[...] = jnp.zeros_like(dstate)

        _, vjp = jax.vjp(_ssd_chunk, xc_ref[...], dt_ref[...], z_ref[...], st_ref[...], db[...], al[...], dk[...], ng[...])
        gr = vjp((do_ref[...], dstate[...]))
        dxc_ref[...] = gr[0]
        ddt_ref[...] = gr[1].astype(ddt_ref.dtype)
        dz_ref[...] = gr[2].astype(dz_ref.dtype)
        dstate[...] = gr[3]
        _accumulate((b == 0) & (c == 0), dpar, gr[4:])

    rv = lambda c: nc - 1 - c
    vec = pl.BlockSpec((1, LANES), lambda b, c: (0, 0))
    ngs = pl.BlockSpec((1, SSW), lambda b, c: (0, 0))
    return pl.pallas_call(
        body, name=name, grid=(bsz, nc),
        in_specs=[pl.BlockSpec((None, CHUNK, CCH), lambda b, c: (b, rv(c), 0)),
                  pl.BlockSpec((None, CHUNK, LANES), lambda b, c: (b, rv(c), OFF_DT // LANES)),
                  pl.BlockSpec((None, CHUNK, SSW), lambda b, c: (b, rv(c), OFF_Z // SSW)),
                  pl.BlockSpec((None, None, SSM_STATE, SSW), lambda b, c: (b, rv(c), 0, 0)),
                  pl.BlockSpec((None, CHUNK, SSW), lambda b, c: (b, rv(c), (GMW + ATW) // SSW)),
                  vec, vec, vec, ngs],
        out_specs=[pl.BlockSpec((None, CHUNK, CCH), lambda b, c: (b, rv(c), 0)),
                   pl.BlockSpec((None, CHUNK, LANES), lambda b, c: (b, rv(c), 0)),
                   pl.BlockSpec((None, CHUNK, SSW), lambda b, c: (b, rv(c), 0)),
                   vec, vec, vec, ngs],
        out_shape=[jax.ShapeDtypeStruct((bsz, s, CCH), F32), jax.ShapeDtypeStruct((bsz, s, LANES), BF16),
                   jax.ShapeDtypeStruct((bsz, s, SSW), BF16)] + [jax.ShapeDtypeStruct((1, LANES), F32)] * 3
                  + [jax.ShapeDtypeStruct((1, SSW), F32)],
        scratch_shapes=[pltpu.VMEM((SSM_STATE, SSW), F32)],
        compiler_params=_cp(("arbitrary", "arbitrary")),
    )(xc, p, p, states, dmix, *prm)


def _rows2d(a):
    return a.reshape(-1, a.shape[-1])


def _ew_tile(r, c):
    t = r
    while t * c > (1 << 20) and t % 16 == 0:
        t //= 2
    return t


def add_pair(g, theirs, core, name):
    k, r, c = g.shape
    h = r // 2
    tr = _ew_tile(h, c)
    nb = h // tr

    def body(c_ref, a_ref, b_ref, o_ref, ob_ref):
        s = a_ref[...] + b_ref[...]
        o_ref[...] = s
        ob_ref[...] = s.astype(ob_ref.dtype)

    blk = pl.BlockSpec((None, tr, c), lambda kk, i, cr: (kk, i, 0))
    return pl.pallas_call(
        body, name=name,
        grid_spec=pltpu.PrefetchScalarGridSpec(
            num_scalar_prefetch=1, grid=(k, nb),
            in_specs=[pl.BlockSpec((None, tr, c), lambda kk, i, cr: (kk, cr[0] * nb + i, 0)), blk],
            out_specs=[blk, blk]),
        out_shape=[jax.ShapeDtypeStruct(theirs.shape, F32), jax.ShapeDtypeStruct(theirs.shape, BF16)],
        compiler_params=_cp(("parallel", "parallel")),
    )(core.reshape(1).astype(jnp.int32), g, theirs)


def sum_own_recv(sums, recv, chip, name):
    _, h, c = sums.shape
    tr = _ew_tile(h, c)

    def body(k_ref, o_ref, r_ref, out_ref):
        s = o_ref[...]
        for j in range(3):
            s = s + r_ref[j].astype(F32)
        out_ref[...] = s

    return pl.pallas_call(
        body, name=name,
        grid_spec=pltpu.PrefetchScalarGridSpec(
            num_scalar_prefetch=1, grid=(h // tr,),
            in_specs=[pl.BlockSpec((None, tr, c), lambda i, kr: (kr[0], i, 0)),
                      pl.BlockSpec((3, tr, c), lambda i, kr: (0, i, 0))],
            out_specs=pl.BlockSpec((tr, c), lambda i, kr: (i, 0))),
        out_shape=jax.ShapeDtypeStruct((h, c), F32),
        compiler_params=_cp(("parallel",)),
    )(chip.reshape(1).astype(jnp.int32), sums, recv)


def _adam_math(w, m, v, g):
    mn = ADAM_B1 * m + (1.0 - ADAM_B1) * g
    vn = ADAM_B2 * v + (1.0 - ADAM_B2) * (g * g)
    mh = mn / (1.0 - ADAM_B1 ** ADAM_STEP)
    vh = vn / (1.0 - ADAM_B2 ** ADAM_STEP)
    return -ADAM_LR * (mh / (jnp.sqrt(vh) + ADAM_EPS) + ADAM_WD * w), mn, vn


def adamw_layer(w, m, v, g, layer, prev, name):
    nl, r, c = w.shape
    tr = _ew_tile(r, c * 2)

    def body(w_ref, m_ref, v_ref, g_ref, *rest):
        go_ref, d_ref, mo_ref, vo_ref = rest[-4:]
        gv = g_ref[...]
        dl, mn, vn = _adam_math(w_ref[...], m_ref[...], v_ref[...], gv)
        go_ref[...] = gv
        d_ref[...] = dl
        mo_ref[...] = mn
        vo_ref[...] = vn

    lay = pl.BlockSpec((None, tr, c), lambda i: (layer, i, 0))
    n_prev = 0 if prev is None else 4
    return pl.pallas_call(
        body, name=name, grid=(r // tr,),
        in_specs=[lay, lay, lay, pl.BlockSpec((tr, c), lambda i: (i, 0))] + [_ANY] * n_prev,
        out_specs=[lay] * 4, out_shape=[jax.ShapeDtypeStruct(w.shape, F32)] * 4,
        input_output_aliases={4 + i: i for i in range(n_prev)},
        compiler_params=_cp(("parallel",)),
    )(w, m, v, g, *(prev or ()))


def sum_devices(parts, name):
    n, r, c = parts.shape
    tr = _ew_tile(r, c * n)

    def body(p_ref, o_ref):
        s = p_ref[0]
        for j in range(1, n):
            s = s + p_ref[j]
        o_ref[...] = s

    return pl.pallas_call(
        body, name=name, grid=(r // tr,),
        in_specs=[pl.BlockSpec((n, tr, c), lambda i: (0, i, 0))],
        out_specs=pl.BlockSpec((tr, c), lambda i: (i, 0)),
        out_shape=jax.ShapeDtypeStruct((r, c), F32),
        compiler_params=_cp(("parallel",)),
    )(parts)


def adamw(w, m, v, g, name):
    r, c = w.shape
    tr = _ew_tile(r, c * 2)

    def body(w_ref, m_ref, v_ref, g_ref, d_ref, mo_ref, vo_ref):
        gv = g_ref[...]
        mn = ADAM_B1 * m_ref[...] + (1.0 - ADAM_B1) * gv
        vn = ADAM_B2 * v_ref[...] + (1.0 - ADAM_B2) * (gv * gv)
        mh = mn / (1.0 - ADAM_B1 ** ADAM_STEP)
        vh = vn / (1.0 - ADAM_B2 ** ADAM_STEP)
        d_ref[...] = -ADAM_LR * (mh / (jnp.sqrt(vh) + ADAM_EPS) + ADAM_WD * w_ref[...])
        mo_ref[...] = mn
        vo_ref[...] = vn

    blk = pl.BlockSpec((tr, c), lambda i: (i, 0))
    return pl.pallas_call(
        body, name=name, grid=(r // tr,), in_specs=[blk] * 4, out_specs=[blk] * 3,
        out_shape=[jax.ShapeDtypeStruct((r, c), F32)] * 3,
        compiler_params=_cp(("parallel",)),
    )(w, m, v, g)


def _place():
    x, y, c = lax.axis_index("x"), lax.axis_index("y"), lax.axis_index("c")
    chips = [(1 - x, y), (x, 1 - y), (1 - x, 1 - y)]
    return x, y, c, chips


def all_gather_small(v, name):
    r, w = v.shape

    def body(x_ref, out_ref, send_sems, recv_sems, local_sem):
        x, y, c, chips = _place()
        me, sibling = (x, y, c), (x, y, 1 - c)

        def rows(px, py, pc):
            return out_ref.at[pl.ds((4 * px + 2 * py + pc) * r, r), :]

        def copy(k, block, to, src=None):
            return pltpu.make_async_remote_copy(
                src_ref=rows(*block) if src is None else src, dst_ref=rows(*block),
                send_sem=send_sems.at[k], recv_sem=recv_sems.at[k], device_id=to, device_id_type=MESH)

        mine = pltpu.make_async_copy(x_ref, rows(*me), local_sem)
        mine.start()
        first = [copy(0, me, sibling, src=x_ref)]
        first += [copy(1 + j, me, (*chip, c), src=x_ref) for j, chip in enumerate(chips)]
        for cp in first:
            cp.start()
        passed = [copy(4 + j, (*chip, c), sibling) for j, chip in enumerate(chips)]
        for j, chip in enumerate(chips):
            copy(1 + j, (*chip, c), me).wait_recv()
            passed[j].start()
        copy(0, sibling, me).wait_recv()
        for j, chip in enumerate(chips):
            copy(4 + j, (*chip, 1 - c), me).wait_recv()
        for cp in first + passed:
            cp.wait_send()
        mine.wait()

    out = pl.pallas_call(
        body, name=name, out_shape=jax.ShapeDtypeStruct((8 * r, w), v.dtype),
        in_specs=[pl.BlockSpec(memory_space=pltpu.VMEM)], out_specs=pl.BlockSpec(memory_space=pltpu.VMEM),
        scratch_shapes=[pltpu.SemaphoreType.DMA((7,)), pltpu.SemaphoreType.DMA((7,)), pltpu.SemaphoreType.DMA],
        compiler_params=pltpu.CompilerParams(vmem_limit_bytes=VMEM_LIMIT),
    )(v)
    return out.reshape(8, r, w)


_HBM = pl.BlockSpec(memory_space=pltpu.HBM)


_SEM = pl.BlockSpec(memory_space=pltpu.SEMAPHORE)
_ANY = pl.BlockSpec(memory_space=pl.ANY)
_EFFECT = pltpu.SideEffectType.DATAFLOW_SIDE_EFFECTING


def _hbm(a):
    return pltpu.with_memory_space_constraint(a, pltpu.HBM)


def split_copy_start(srcs, land_shapes, copies, after, name):
    n, nl = len(srcs), len(land_shapes)
    ncopy = [0]

    def body(*refs):
        ins, lands = refs[:n], refs[n:n + nl]
        send_sems, recv_sems = refs[n + nl + 1], refs[n + nl + 2]
        token = refs[-1]
        x, y, c, chips = _place()
        for k, (src, dst, to) in enumerate(copies(x, y, c, chips, ins, lands)):
            pltpu.make_async_remote_copy(src_ref=src, dst_ref=dst, send_sem=send_sems.at[k], recv_sem=recv_sems.at[k],
                                         device_id=to, device_id_type=MESH).start()
        token[...] = jnp.zeros_like(token)

    ncopy[0] = len(copies(0, 0, 0, [(1, 0), (0, 1), (1, 1)], [None] * n, [None] * nl, count_only=True))
    k = ncopy[0]
    lands = [_hbm(lax.empty(s.shape, s.dtype)) for s in land_shapes]
    res = pl.pallas_call(
        body, name=name,
        out_shape=(pltpu.SemaphoreType.DMA((k,)), pltpu.SemaphoreType.DMA((k,)))
        + tuple(pltpu.HBM(s.shape, s.dtype) for s in srcs) + tuple(pltpu.HBM(s.shape, s.dtype) for s in land_shapes)
        + (jax.ShapeDtypeStruct((8, LANES), F32),),
        in_specs=[_HBM] * (n + nl) + [_ANY],
        out_specs=(_SEM, _SEM) + (_HBM,) * (n + nl) + (pl.BlockSpec(memory_space=pltpu.VMEM),),
        input_output_aliases={i: 2 + i for i in range(n + nl)},
        compiler_params=pltpu.CompilerParams(has_side_effects=_EFFECT),
    )(*[_hbm(s) for s in srcs], *lands, after)
    return res[0], res[1], list(res[2:2 + n]), list(res[2 + n:2 + n + nl]), res[-1]


def split_copy_wait(send_sems, recv_sems, srcs, lands, copies, after, name):
    n, nl = len(srcs), len(lands)

    def body(*refs):
        ins, lnd = refs[:n], refs[n:n + nl]
        ss, rs = refs[n + nl], refs[n + nl + 1]
        x, y, c, chips = _place()
        for k, (src, dst, to) in enumerate(copies(x, y, c, chips, ins, lnd, receive=True)):
            cp = pltpu.make_async_remote_copy(src_ref=src, dst_ref=dst, send_sem=ss.at[k], recv_sem=rs.at[k],
                                              device_id=to, device_id_type=MESH)
            cp.wait_send()
            cp.wait_recv()

    res = pl.pallas_call(
        body, name=name,
        out_shape=tuple(pltpu.HBM(s.shape, s.dtype) for s in srcs) + tuple(pltpu.HBM(s.shape, s.dtype) for s in lands),
        in_specs=[_HBM] * (n + nl) + [_SEM, _SEM, _ANY], out_specs=(_HBM,) * (n + nl),
        input_output_aliases={i: i for i in range(n + nl)},
        compiler_params=pltpu.CompilerParams(has_side_effects=_EFFECT),
    )(*srcs, *lands, send_sems, recv_sems, after)
    return list(res[:n]), list(res[n:])


def _gather_copies(x, y, c, chips, ins, lands, receive=False, count_only=False):
    out = []
    for i in range(len(ins)):
        for cx, cy in chips:
            if count_only:
                out.append(None)
                continue
            h = ins[i].shape[0] // 2
            rows = pl.ds(c * h, h)
            k_dst = (2 * cx + cy) if receive else (2 * x + y)
            out.append((ins[i].at[rows, :], lands[i].at[k_dst, rows, :], (cx, cy, c)))
    return out


def _scatter_copies(x, y, c, chips, ins, lands, receive=False, count_only=False):
    out = []
    for i in range(len(ins)):
        for j, (cx, cy) in enumerate(chips):
            if count_only:
                out.append(None)
                continue
            out.append((ins[i].at[2 * cx + cy], lands[i].at[j], (cx, cy, c)))
    return out


def forward_halves(lands, name):
    n = len(lands)

    def body(*refs):
        ins, outs = refs[:n], refs[n:2 * n]
        send_sems, recv_sems = refs[2 * n:]
        x, y, c, chips = _place()
        sibling = (x, y, 1 - c)
        sent = []
        for i in range(n):
            h = ins[i].shape[1] // 2
            for j, (cx, cy) in enumerate(chips):
                blk = ins[i].at[2 * cx + cy, pl.ds(c * h, h), :]
                sent.append(pltpu.make_async_remote_copy(
                    src_ref=blk, dst_ref=outs[i].at[2 * cx + cy, pl.ds(c * h, h), :], send_sem=send_sems.at[3 * i + j],
                    recv_sem=recv_sems.at[3 * i + j], device_id=sibling, device_id_type=MESH))
                sent[-1].start()
        for i in range(n):
            h = ins[i].shape[1] // 2
            for j, (cx, cy) in enumerate(chips):
                theirs = outs[i].at[2 * cx + cy, pl.ds((1 - c) * h, h), :]
                pltpu.make_async_remote_copy(
                    src_ref=theirs, dst_ref=theirs, send_sem=send_sems.at[3 * i + j], recv_sem=recv_sems.at[3 * i + j],
                    device_id=sibling, device_id_type=MESH).wait_recv()
        for cp in sent:
            cp.wait_send()

    return pl.pallas_call(
        body, name=name, out_shape=[jax.ShapeDtypeStruct(s.shape, s.dtype) for s in lands],
        in_specs=[_HBM] * n, out_specs=[_HBM] * n, input_output_aliases={i: i for i in range(n)},
        scratch_shapes=[pltpu.SemaphoreType.DMA((3 * n,)), pltpu.SemaphoreType.DMA((3 * n,))],
    )(*lands)


def swap_halves(grads, name):
    n = len(grads)

    def body(*refs):
        ins, theirs = refs[:n], refs[n:2 * n]
        send_sems, recv_sems = refs[2 * n:]
        x, y, c, _ = _place()
        sibling = (x, y, 1 - c)
        sent = []
        for i in range(n):
            h = ins[i].shape[1] // 2
            sent.append(pltpu.make_async_remote_copy(
                src_ref=ins[i].at[:, pl.ds((1 - c) * h, h), :], dst_ref=theirs[i],
                send_sem=send_sems.at[i], recv_sem=recv_sems.at[i], device_id=sibling, device_id_type=MESH))
            sent[-1].start()
        for cp in sent:
            cp.wait_recv()
        for cp in sent:
            cp.wait_send()

    half = [jax.ShapeDtypeStruct((4, g.shape[1] // 2, g.shape[2]), g.dtype) for g in grads]
    theirs = pl.pallas_call(
        body, name=name, out_shape=half, in_specs=[_HBM] * n, out_specs=[_HBM] * n,
        scratch_shapes=[pltpu.SemaphoreType.DMA((n,)), pltpu.SemaphoreType.DMA((n,))],
    )(*grads)
    return theirs


def join_halves(halves, name):
    n = len(halves)

    def body(*refs):
        ins, outs = refs[:n], refs[n:2 * n]
        send_sems, recv_sems = refs[2 * n:]
        x, y, c, _ = _place()
        sibling = (x, y, 1 - c)
        sent = []
        for i in range(n):
            h = ins[i].shape[0]
            sent.append(pltpu.make_async_remote_copy(
                src_ref=ins[i], dst_ref=outs[i].at[pl.ds(c * h, h), :], send_sem=send_sems.at[i],
                recv_sem=recv_sems.at[i], device_id=sibling, device_id_type=MESH))
            sent[-1].start()
        for i in range(n):
            h = ins[i].shape[0]
            pltpu.make_async_remote_copy(
                src_ref=ins[i], dst_ref=outs[i].at[pl.ds((1 - c) * h, h), :], send_sem=send_sems.at[i],
                recv_sem=recv_sems.at[i], device_id=sibling, device_id_type=MESH).wait_recv()
        for cp in sent:
            cp.wait_send()

    full = pl.pallas_call(
        body, name=name, out_shape=[jax.ShapeDtypeStruct((2 * s.shape[0], s.shape[1]), F32) for s in halves],
        in_specs=[_HBM] * n, out_specs=[_HBM] * n,
        scratch_shapes=[pltpu.SemaphoreType.DMA((n,)), pltpu.SemaphoreType.DMA((n,))],
    )(*halves)
    ci = lax.axis_index("c")
    return [lax.dynamic_update_slice(f, s, (ci * s.shape[0], 0)) for f, s in zip(full, halves)]


_PACK_ROWS = 8 * LANES


def _pack(arrs):
    flat = jnp.concatenate([a.reshape(-1).astype(F32) for a in arrs])
    pad = (-flat.shape[0]) % _PACK_ROWS
    return jnp.pad(flat, (0, pad)).reshape(-1, LANES)


def _unpack(flat, shapes):
    flat = flat.reshape(-1)
    out, off = [], 0
    for s in shapes:
        n = int(np.prod(s))
        out.append(flat[off:off + n].reshape(s))
        off += n
    return out


_SEGS = [(0, OFF_U, GMW), (GMW, OFF_V, GMW), (2 * GMW, OFF_Q, ATW), (2 * GMW + ATW, OFF_K, KVW),
         (2 * GMW + ATW + KVW, OFF_VV, KVW), (2 * GMW + ATW + 2 * KVW, OFF_Z, SSW),
         (2 * GMW + ATW + 2 * KVW + SSW, OFF_XBC, CCH), (IN_W - SSM_HEADS, OFF_DT, SSM_HEADS)]


def _win_to_kernel_layout(w):
    out = jnp.zeros((w.shape[0], PW), w.dtype)
    for src, dst, wd in _SEGS:
        out = lax.dynamic_update_slice(out, w[:, src:src + wd], (0, dst))
    return out


def _win_from_kernel_layout(w):
    return jnp.concatenate([w[:, dst:dst + wd] for _, dst, wd in _SEGS], axis=1)


def _relu2(a):
    r = jnp.maximum(a, 0)
    return r * r


def kernel(x, c, ada_w, ada_b, norm1_g, w_in, gm_ln_g, gm_ln_b, gm_ws, gm_bs, gm_norm_g, attn_sinks, attn_norm_g, conv_w, conv_b, dt_bias, a_log, d_skip, ssm_norm_g, w_out, norm2_g, w_mlp1, w_mlp2, final_norm_g, loss_target, m_ada_w, m_ada_b, m_norm1_g, m_w_in, m_gm_ln_g, m_gm_ln_b, m_gm_ws, m_gm_bs, m_gm_norm_g, m_attn_sinks, m_attn_norm_g, m_conv_w, m_conv_b, m_dt_bias, m_a_log, m_d_skip, m_ssm_norm_g, m_w_out, m_norm2_g, m_w_mlp1, m_w_mlp2, m_final_norm_g, v_ada_w, v_ada_b, v_norm1_g, v_w_in, v_gm_ln_g, v_gm_ln_b, v_gm_ws, v_gm_bs, v_gm_norm_g, v_attn_sinks, v_attn_norm_g, v_conv_w, v_conv_b, v_dt_bias, v_a_log, v_d_skip, v_ssm_norm_g, v_w_out, v_norm2_g, v_w_mlp1, v_w_mlp2, v_final_norm_g):
    nl = ada_w.shape[0]
    bl, s, d = x.shape
    t = bl * s
    dff4 = w_mlp1.shape[2]
    dff = 4 * dff4
    mod_w = ada_w.shape[2]
    cw_w = conv_w.shape[2]
    xi, yi, ci = lax.axis_index("x"), lax.axis_index("y"), lax.axis_index("c")
    chip = 2 * xi + yi
    dev = 2 * chip + ci
    nex = 8 * bl

    g0 = all_gather_small(_pack([c, conv_w]), "ag_c")
    g0 = g0.reshape(8, -1)
    c_all = g0[:, :bl * d].reshape(nex, d)
    cw_parts = g0[0::2, bl * d:bl * d + conv_w.size].reshape(4, nl, CONV_K, cw_w)
    conv_w_full = cw_parts.transpose(1, 2, 0, 3).reshape(nl, CONV_K, CCH)

    def c_act(a):
        return _silu(a).astype(BF16)

    def to_bf16(a):
        return a.astype(BF16)

    mod_parts = []
    for l in range(nl):
        bias = lax.dynamic_slice(ada_b[l].reshape(1, -1), (0, chip * mod_w), (1, mod_w))
        mod_parts.append(_mm("nn", c_all, ada_w, dims=(nex, mod_w, d), tm=nex, tn=512, tk=d, out_dtypes=[F32],
                             name=f"mod_{l}", pro_a=c_act, pro_b=to_bf16,
                             b_spec=pl.BlockSpec((None, d, 512), lambda i, j, kk, l=l: (l, kk, j)),
                             extras=[(bias, pl.BlockSpec((1, 512), lambda i, j, kk: (0, j)))],
                             epi=lambda acc, bv: (acc + bv,))[0])
    g1 = all_gather_small(_pack(mod_parts), "ag_mod").reshape(8, -1)
    mod_all = g1[0::2, :nl * nex * mod_w].reshape(4, nl, nex, mod_w).transpose(1, 2, 0, 3).reshape(nl, nex, 4 * mod_w)
    mod = lax.dynamic_slice(mod_all, (0, dev * bl, 0), (nl, bl, 4 * mod_w))
    mods = [[mod[l, :, i * d:(i + 1) * d].reshape(bl, 1, d) for i in range(6)] for l in range(nl)]

    shards = [[w_in[l].astype(BF16), w_out[l].astype(BF16), w_mlp1[l].astype(BF16), w_mlp2[l].astype(BF16)]
              for l in range(nl)]
    groups = [[shards[0][i]] for i in range(4)] + [shards[l] for l in range(1, nl)]
    pending, after = [], g1
    for gi, grp in enumerate(groups):
        ss, rs, srcs, lands, after = split_copy_start(
            grp, [jax.ShapeDtypeStruct((4,) + a.shape, a.dtype) for a in grp], _gather_copies, after, f"gather_start_{gi}")
        pending.append((ss, rs, srcs, lands))
    mods[0][0] = mods[0][0] + after[0, 0]

    def fetch(gi, behind):
        ss, rs, srcs, lands = pending[gi]
        srcs, lands = split_copy_wait(ss, rs, srcs, lands, _gather_copies, behind, f"gather_wait_{gi}")
        lands = forward_halves(lands, f"gather_pass_{gi}")
        return [lax.dynamic_update_slice(g, a[None], (chip, 0, 0)) for g, a in zip(lands, srcs)]

    def as_win(g):
        return _win_to_kernel_layout(g.transpose(1, 0, 2).reshape(d, IN_W))

    wfull = [None] * nl
    row = lambda a: a.reshape(1, -1)
    pad16 = lambda a: jnp.pad(a.reshape(1, -1), ((0, 0), (0, LANES - SSM_HEADS)))
    tm_res = min(1024, s)

    def residual(acc, xt, gt):
        return acc, xt + gt * acc

    def res_extras(xin, gate, tm=tm_res):
        return [(xin.reshape(t, d), pl.BlockSpec((tm, 512), lambda i, j, kk: (i, j))),
                (gate, pl.BlockSpec((None, 1, 512), lambda i, j, kk: (i * tm // s, 0, j)))]

    w1_blk = lambda tk, tn: pl.BlockSpec((None, tk, tn), lambda i, j, kk: (j // (dff4 // tn), kk, j % (dff4 // tn)))

    saved = []
    xcur = x
    for l in range(nl):
        sh1, sc1, gt1, sh2, sc2, gt2 = mods[l]
        if l == 0:
            win = as_win(fetch(0, mod)[0])
        else:
            g_in, g_out, w1, g_2 = fetch(3 + l, xcur)
            win, wout, w2 = as_win(g_in), g_out.reshape(-1, d), g_2.reshape(dff, d)
        prm_a = (row(gm_ln_g[l]), row(gm_ln_b[l]), gm_ws[l], gm_bs[l].T, row(gm_norm_g[l]))
        prm_b = (row(attn_sinks[l]), row(attn_norm_g[l]))
        prm_c = (pad16(dt_bias[l]), pad16(a_log[l]), pad16(d_skip[l]), row(ssm_norm_g[l]))
        h1 = ln_mod_fwd(xcur, row(norm1_g[l]), sc1, sh1, f"ln1_fwd_{l}")
        p = _mm("nn", h1.reshape(t, d), win, dims=(t, PW, d), tm=1024, tn=512, tk=d, out_dtypes=[F32],
                name=f"proj_in_{l}")[0].reshape(bl, s, PW)
        out_a = gmlp_fwd(p, prm_a, f"gmlp_fwd_{l}")
        out_b = attn_fwd(p, *prm_b, f"attn_fwd_{l}")
        xc = conv_fwd(p, conv_w_full[l], row(conv_b[l]), f"conv_fwd_{l}")
        out_c, states = ssd_fwd(xc, p, prm_c, f"ssd_fwd_{l}")
        mix = jnp.concatenate([out_a, out_b, out_c], axis=-1)
        if l == 0:
            wout = fetch(1, mix)[0].reshape(-1, d)
        mm1, x2 = _mm("nn", mix.reshape(t, d), wout, dims=(t, d, d), tm=tm_res, tn=512, tk=d, out_dtypes=[F32, F32],
                      name=f"proj_out_{l}", extras=res_extras(xcur, gt1), epi=residual)
        x2 = x2.reshape(bl, s, d)
        h2 = ln_mod_fwd(x2, row(norm2_g[l]), sc2, sh2, f"ln2_fwd_{l}")
        if l == 0:
            w1 = fetch(2, h2)[0]
        a1 = _mm("nn", h2.reshape(t, d), w1, dims=(t, dff, d), tm=1024, tn=512, tk=d, out_dtypes=[BF16],
                 name=f"mlp1_{l}", b_spec=w1_blk(d, 512))[0]
        if l == 0:
            w2 = fetch(3, a1)[0].reshape(dff, d)
        tm2 = min(512, s)
        mm2, x3 = _mm("nn", a1, w2, dims=(t, d, dff), tm=tm2, tn=512, tk=dff, out_dtypes=[F32, F32],
                      name=f"mlp2_{l}", extras=res_extras(x2, gt2, tm2), epi=residual, pro_a=_relu2)
        x3 = x3.reshape(bl, s, d)
        wfull[l] = (win, wout, w1, w2)
        saved.append((xcur, h1, p, xc, states, mix, mm1.reshape(bl, s, d), x2, h2, a1, mm2.reshape(bl, s, d),
                      prm_a, prm_b, prm_c))
        xcur = x3

    dx, d_final_g, loss_part = loss_head(xcur, row(final_norm_g), loss_target, "loss_head")
    loss = lax.psum(loss_part[0, 0], ("x", "y", "c"))

    def rs_begin(grads, tag, behind=None):
        theirs = swap_halves(grads, f"rs_swap_{tag}")
        sums = [add_pair(g, th, ci, f"rs_add_{tag}_{i}") for i, (g, th) in enumerate(zip(grads, theirs))]
        ss, rs, srcs, lands, token = split_copy_start(
            [sm[1] for sm in sums], [jax.ShapeDtypeStruct((3,) + sm[1].shape[1:], BF16) for sm in sums],
            _scatter_copies, sums[0][0] if behind is None else behind, f"rs_start_{tag}")
        return (ss, rs, srcs, lands, [sm[0] for sm in sums]), token

    def rs_end(state, behind, tag):
        ss, rs, srcs, lands, sums_f32 = state
        _, got = split_copy_wait(ss, rs, srcs, lands, _scatter_copies, behind, f"rs_wait_{tag}")
        halves = [sum_own_recv(sf, g, chip, f"rs_sum_{tag}_{i}") for i, (sf, g) in enumerate(zip(sums_f32, got))]
        return join_halves(halves, f"rs_join_{tag}")

    small_parts = [None] * nl
    dmods = [None] * nl
    reduced = [[None] * 4 for _ in range(nl)]
    pending_rs, rs_token = [], None
    part_slots = {"a": (0, 1), "m": (2, 3)}

    def finish(behind):
        for ll, part, state in pending_rs:
            for slot, blk in zip(part_slots[part], rs_end(state, behind, f"{ll}{part}")):
                reduced[ll][slot] = blk
        pending_rs.clear()

    for l in reversed(range(nl)):
        sh1, sc1, gt1, sh2, sc2, gt2 = mods[l]
        win, wout, w1, w2 = wfull[l]
        xin, h1, p, xc, states, mix, mm1, x2, h2, a1, mm2, prm_a, prm_b, prm_c = saved[l]
        if rs_token is not None:
            gt2 = gt2 + rs_token[0, 0]
        dm2, dgt2 = gate_bwd(dx, mm2, gt2, f"gate2_bwd_{l}")
        dm2 = dm2.reshape(t, d)
        da1 = _mm("nt", dm2, w2, dims=(t, dff, d), tm=1024, tn=512, tk=d, out_dtypes=[BF16], name=f"mlp2_dx_{l}",
                  extras=[(a1, pl.BlockSpec((1024 if t >= 1024 else t, 512), lambda i, j, kk: (i, j)))],
                  epi=lambda acc, av: (acc * (2.0 * jnp.maximum(av, 0).astype(F32)),))[0]
        dw2 = _mm("tn", a1, dm2, dims=(dff, d, t), tm=512, tn=2048, tk=2048, out_dtypes=[F32], name=f"mlp2_dw_{l}",
                  pro_a=_relu2)[0]
        dw1 = _mm("tn", h2.reshape(t, d), da1, dims=(d, dff, t), tm=512, tn=dff4, tk=2048, out_dtypes=[F32],
                  name=f"mlp1_dw_{l}", out_shapes=[(4, d, dff4)],
                  out_specs=[pl.BlockSpec((None, 512, dff4), lambda i, j, kk: (j, i, 0))])[0]
        mlp_state, mlp_token = rs_begin([dw1, dw2.reshape(4, dff4, d)], f"{l}m")
        sc2 = sc2 + mlp_token[0, 0]
        dh2 = _mm("nt", da1, w1, dims=(t, d, dff), tm=1024, tn=512, tk=2048, out_dtypes=[F32], name=f"mlp1_dx_{l}",
                  b_spec=pl.BlockSpec((None, 512, 2048 if dff4 >= 2048 else dff4),
                                      lambda i, j, kk: (kk // (dff4 // min(2048, dff4)), j, kk % (dff4 // min(2048, dff4)))))[0]
        dx2, dsc2, dsh2, dn2 = ln_mod_bwd(dh2.reshape(bl, s, d), x2, dx, row(norm2_g[l]), sc2, f"ln2_bwd_{l}")
        dm1, dgt1 = gate_bwd(dx2, mm1, gt1, f"gate1_bwd_{l}")
        dm1 = dm1.reshape(t, d)
        dmix = _mm("nt", dm1, wout, dims=(t, d, d), tm=1024, tn=512, tk=d, out_dtypes=[F32],
                   name=f"proj_out_dx_{l}")[0].reshape(bl, s, d)
        dwout = _mm("tn", mix.reshape(t, d), dm1, dims=(d, d, t), tm=512, tn=2048, tk=2048, out_dtypes=[F32],
                    name=f"proj_out_dw_{l}")[0]
        du, dv, dlg, dlb, dws, dbst, dgng = gmlp_bwd(p, dmix, prm_a, f"gmlp_bwd_{l}")
        dq, dk, dvv, dsinks, dang = attn_bwd(p, dmix, *prm_b, f"attn_bwd_{l}")
        dxc, ddt, dz, ddtb, dalog, ddsk, dsng = ssd_bwd(xc, p, states, dmix, prm_c, f"ssd_bwd_{l}")
        dxbc, dcw, dcb = conv_bwd(p, dxc, conv_w_full[l], row(conv_b[l]), f"conv_bwd_{l}")
        dp = jnp.concatenate([dxbc, dq, dz, du, dv, dk, dvv, ddt, jnp.zeros((bl, s, PW - OFF_DT - LANES), BF16)],
                             axis=-1).reshape(t, PW)
        dwin = _mm("tn", h1.reshape(t, d), dp, dims=(d, PW, t), tm=512, tn=PW // 3, tk=2048, out_dtypes=[F32],
                   name=f"proj_in_dw_{l}")[0]
        dh1 = _mm("nt", dp, win, dims=(t, d, PW), tm=1024, tn=512, tk=PW, out_dtypes=[F32],
                  name=f"proj_in_dx_{l}")[0]
        dx, dsc1, dsh1, dn1 = ln_mod_bwd(dh1.reshape(bl, s, d), xin, dx2, row(norm1_g[l]), sc1, f"ln1_bwd_{l}")
        dmods[l] = jnp.concatenate([dsh1, dsc1, dgt1, dsh2, dsc2, dgt2], axis=-1).reshape(bl, 6 * d)
        small_parts[l] = [dn1, dlg, dlb, dws, dbst.T, dgng, dsinks, dang, dcw, dcb, ddtb[:, :SSM_HEADS],
                          dalog[:, :SSM_HEADS], ddsk[:, :SSM_HEADS], dsng, dn2]
        dwin_blocks = _win_from_kernel_layout(dwin).reshape(d, 4, IN_W // 4).transpose(1, 0, 2)
        finish(dx)
        pending_rs.append((l, "m", mlp_state))
        mixer_grads = [dwin_blocks, dwout.reshape(4, d // 4, d)]
        if l > 0:
            state, rs_token = rs_begin(mixer_grads, f"{l}a")
            pending_rs.append((l, "a", state))
    grad_x = dx

    big = [(w_in, m_w_in, v_w_in), (w_out, m_w_out, v_w_out), (w_mlp1, m_w_mlp1, v_w_mlp1), (w_mlp2, m_w_mlp2, v_w_mlp2)]
    big_out = [None] * 4
    for l in reversed(range(1, nl)):
        for i, (wt, mt, vt) in enumerate(big):
            big_out[i] = adamw_layer(wt, mt, vt, reduced[l][i], l, big_out[i], f"adamw_big_{i}_{l}")

    small_names = [norm1_g, gm_ln_g, gm_ln_b, gm_ws, gm_bs, gm_norm_g, attn_sinks, attn_norm_g, None, conv_b, dt_bias,
                   a_log, d_skip, ssm_norm_g, norm2_g]
    n_small = len(small_names)
    per_param = [jnp.stack([small_parts[l][i].reshape(-1) for l in range(nl)]) for i in range(n_small)]
    small_vec = _pack(per_param + [d_final_g])
    rs_small = small_vec.shape[0]
    dmod_local = jnp.stack(dmods, axis=1)
    g2 = all_gather_small(jnp.concatenate([small_vec, _pack([dmod_local])], axis=0), "ag_small")
    state, rs_token = rs_begin(mixer_grads, "0a", behind=g2)
    pending_rs.append((0, "a", state))
    g2 = g2 + rs_token[0, 0]
    g_small = sum_devices(g2[:, :rs_small, :], "sum_small")
    dmod_all = g2[:, rs_small:, :].reshape(8, -1)[:, :bl * nl * 6 * d].reshape(nex, nl * 6 * d)
    g_ada_b = sum_devices(dmod_all.reshape(nex, -1, LANES), "sum_ada_b").reshape(nl, 6 * d)
    shapes = [(nl, int(np.prod(small_parts[0][i].shape))) for i in range(n_small)] + [(d,)]
    g_list = _unpack(g_small, shapes)
    g_conv_w = lax.dynamic_slice(g_list[8].reshape(nl, CONV_K, CCH), (0, 0, chip * cw_w), (nl, CONV_K, cw_w))

    dm_cols = lax.dynamic_slice(dmod_all.reshape(nex, nl, 6 * d), (0, 0, chip * mod_w), (nex, nl, mod_w))
    g_ada_w = _mm("tn", c_all, dm_cols.reshape(nex, nl * mod_w), dims=(d, nl * mod_w, nex), tm=512, tn=512, tk=nex,
                  out_dtypes=[F32], name="ada_w_grad", pro_a=c_act, pro_b=to_bf16, out_shapes=[(nl, d, mod_w)],
                  out_specs=[pl.BlockSpec((None, 512, 512), lambda i, j, kk: (j // (mod_w // 512), i, j % (mod_w // 512)))])[0]
    d_ada_w, m_ada_w_n, v_ada_w_n = [a.reshape(ada_w.shape) for a in
                                     adamw(_rows2d(ada_w), _rows2d(m_ada_w), _rows2d(v_ada_w), _rows2d(g_ada_w), "adamw_ada_w")]

    smalls = {
        "ada_b": (ada_b, m_ada_b, v_ada_b, g_ada_b), "norm1_g": (norm1_g, m_norm1_g, v_norm1_g, g_list[0]),
        "gm_ln_g": (gm_ln_g, m_gm_ln_g, v_gm_ln_g, g_list[1]), "gm_ln_b": (gm_ln_b, m_gm_ln_b, v_gm_ln_b, g_list[2]),
        "gm_ws": (gm_ws, m_gm_ws, v_gm_ws, g_list[3]), "gm_bs": (gm_bs, m_gm_bs, v_gm_bs, g_list[4]),
        "gm_norm_g": (gm_norm_g, m_gm_norm_g, v_gm_norm_g, g_list[5]),
        "attn_sinks": (attn_sinks, m_attn_sinks, v_attn_sinks, g_list[6]),
        "attn_norm_g": (attn_norm_g, m_attn_norm_g, v_attn_norm_g, g_list[7]),
        "conv_w": (conv_w, m_conv_w, v_conv_w, g_conv_w), "conv_b": (conv_b, m_conv_b, v_conv_b, g_list[9]),
        "dt_bias": (dt_bias, m_dt_bias, v_dt_bias, g_list[10]), "a_log": (a_log, m_a_log, v_a_log, g_list[11]),
        "d_skip": (d_skip, m_d_skip, v_d_skip, g_list[12]),
        "ssm_norm_g": (ssm_norm_g, m_ssm_norm_g, v_ssm_norm_g, g_list[13]),
        "norm2_g": (norm2_g, m_norm2_g, v_norm2_g, g_list[14]),
        "final_norm_g": (final_norm_g, m_final_norm_g, v_final_norm_g, g_list[15]),
    }
    keys = list(smalls)
    wv, mv, vv_, gv = [_pack([smalls[k][i].reshape(smalls[k][0].shape) for k in keys]) for i in range(4)]
    sd_, sm_, sv_ = adamw(wv, mv, vv_, gv, "adamw_small")
    shp = [smalls[k][0].shape for k in keys]
    small_out = {k: (smalls[k][3].reshape(smalls[k][0].shape), a, b, cc)
                 for k, a, b, cc in zip(keys, _unpack(sd_, shp), _unpack(sm_, shp), _unpack(sv_, shp))}

    late = jnp.zeros((8, LANES), F32) + (sv_[0, 0] + v_ada_w_n[0, 0, 0])
    for bo in big_out:
        if bo is not None:
            late = late + bo[3][nl - 1, 0, 0]
    finish(late)
    for i, (wt, mt, vt) in enumerate(big):
        big_out[i] = adamw_layer(wt, mt, vt, reduced[0][i], 0, big_out[i], f"adamw_big_{i}_0")

    out = {"ada_w": (g_ada_w, d_ada_w, m_ada_w_n, v_ada_w_n), "w_in": big_out[0], "w_out": big_out[1],
           "w_mlp1": big_out[2], "w_mlp2": big_out[3], **small_out}
    order = ["ada_w", "ada_b", "norm1_g", "w_in", "gm_ln_g", "gm_ln_b", "gm_ws", "gm_bs", "gm_norm_g", "attn_sinks",
             "attn_norm_g", "conv_w", "conv_b", "dt_bias", "a_log", "d_skip", "ssm_norm_g", "w_out", "norm2_g",
             "w_mlp1", "w_mlp2", "final_norm_g"]
    return (loss, grad_x, *[out[k][0] for k in order], *[out[k][1] for k in order],
            *[out[k][2] for k in order], *[out[k][3] for k in order])
```

```python
import functools
import math

import jax
import jax.numpy as jnp
import numpy as np
from jax import lax
from jax.experimental import pallas as pl
from jax.experimental.pallas import tpu as pltpu

F32 = jnp.float32
BF16 = jnp.bfloat16
HI = lax.Precision.HIGHEST
MESH = pl.DeviceIdType.MESH

CHUNK = 128
GM_HEADS, GM_HD = 4, 128
ATT_HEADS, ATT_KV, ATT_HD = 8, 2, 64
WINDOW = 128
SSM_HEADS, SSM_HD, SSM_GROUPS, SSM_STATE, CONV_K = 16, 64, 2, 128, 4
EPS = 1e-6
LN_EPS = 1e-5
NEG = -1e30
LANES = 128

GMW = GM_HEADS * GM_HD
ATW = ATT_HEADS * ATT_HD
KVW = ATT_KV * ATT_HD
SSW = SSM_HEADS * SSM_HD
BCW = SSM_GROUPS * SSM_STATE
CCH = SSW + 2 * BCW
GRW = SSW // SSM_GROUPS
IN_SIZES = (GMW, GMW, ATW, KVW, KVW, SSW, CCH, SSM_HEADS)
IN_W = sum(IN_SIZES)
OFF_XBC, OFF_Q, OFF_Z, OFF_U, OFF_V, OFF_K, OFF_VV, OFF_DT = 0, 1536, 2048, 3072, 3584, 4096, 4224, 4352
PW = 4608

ADAM_LR, ADAM_B1, ADAM_B2, ADAM_EPS, ADAM_WD, ADAM_STEP = 0.001, 0.9, 0.999, 1e-08, 0.01, 10

VMEM_LIMIT = 56 * 1024 * 1024


def _cp(sem=None):
    return pltpu.CompilerParams(dimension_semantics=sem, vmem_limit_bytes=VMEM_LIMIT)


_DN = {"nn": (((1,), (0,)), ((), ())), "nt": (((1,), (1,)), ((), ())), "tn": (((0,), (0,)), ((), ()))}


def _dot(form, a, b):
    return lax.dot_general(a.astype(BF16), b.astype(BF16), _DN[form], preferred_element_type=F32)


@jax.custom_vjp
def _nn(a, b):
    return _dot("nn", a, b)


@jax.custom_vjp
def _nt(a, b):
    return _dot("nt", a, b)


@jax.custom_vjp
def _tn(a, b):
    return _dot("tn", a, b)


_nn.defvjp(lambda a, b: (_dot("nn", a, b), (a, b)), lambda r, g: (_dot("nt", g, r[1]), _dot("tn", r[0], g)))
_nt.defvjp(lambda a, b: (_dot("nt", a, b), (a, b)), lambda r, g: (_dot("nn", g, r[1]), _dot("tn", g, r[0])))
_tn.defvjp(lambda a, b: (_dot("tn", a, b), (a, b)), lambda r, g: (_dot("nt", r[1], g), _dot("nn", r[0], g)))


def _hdot(a, b):
    return jnp.dot(a, b, precision=HI, preferred_element_type=F32)


def _silu(x):
    return x * (1.0 / (1.0 + jnp.exp(-x)))


def _softplus(x):
    return jnp.maximum(x, 0.0) + jnp.log1p(jnp.exp(-jnp.abs(x)))


def _gelu(x):
    return 0.5 * x * (1.0 + jnp.tanh(math.sqrt(2.0 / math.pi) * (x + 0.044715 * (x * x * x))))


def _rms(y, g):
    return y * lax.rsqrt(jnp.mean(y * y, axis=-1, keepdims=True) + EPS) * g


def _mm(form, a, b, *, dims, tm, tn, tk, out_dtypes, name, a_spec=None, b_spec=None, out_specs=None,
        out_shapes=None, extras=(), epi=None, pro_a=None, pro_b=None, behind=None):
    m, n, k = dims
    tm, tn, tk = min(tm, m), min(tn, n), min(tk, k)
    assert m % tm == 0 and n % tn == 0 and k % tk == 0, (name, dims, tm, tn, tk)
    nk = k // tk
    if a_spec is None:
        a_spec = (pl.BlockSpec((tk, tm), lambda i, j, kk: (kk, i)) if form == "tn"
                  else pl.BlockSpec((tm, tk), lambda i, j, kk: (i, kk)))
    if b_spec is None:
        b_spec = (pl.BlockSpec((tn, tk), lambda i, j, kk: (j, kk)) if form == "nt"
                  else pl.BlockSpec((tk, tn), lambda i, j, kk: (kk, j)))
    n_out = len(out_dtypes)
    if out_specs is None:
        out_specs = [pl.BlockSpec((tm, tn), lambda i, j, kk: (i, j))] * n_out
    if out_shapes is None:
        out_shapes = [(m, n)] * n_out
    ne = len(extras)
    n_behind = 0 if behind is None else 1

    def body(*refs):
        a_ref, b_ref = refs[0], refs[1]
        ex = refs[2:2 + ne]
        outs = refs[2 + ne + n_behind:2 + ne + n_behind + n_out]

        def write(val):
            res = epi(val, *[e[...] for e in ex]) if epi is not None else (val,)
            for o, r in zip(outs, res):
                o[...] = r.astype(o.dtype)

        av = a_ref[...]
        if pro_a is not None:
            av = pro_a(av)
        bv = b_ref[...]
        if pro_b is not None:
            bv = pro_b(bv)
        part = lax.dot_general(av, bv, _DN[form], preferred_element_type=F32)
        if nk == 1:
            write(part)
        else:
            acc = refs[-1]
            kk = pl.program_id(2)

            @pl.when(kk == 0)
            def _():
                acc[...] = part

            @pl.when(kk > 0)
            def _():
                acc[...] += part

            @pl.when(kk == nk - 1)
            def _():
                write(acc[...])

    res = pl.pallas_call(
        body, name=name, grid=(m // tm, n // tn, nk),
        in_specs=[a_spec, b_spec] + [s for _, s in extras] + [_ANY] * n_behind,
        out_specs=out_specs,
        out_shape=[jax.ShapeDtypeStruct(s, d) for s, d in zip(out_shapes, out_dtypes)],
        scratch_shapes=[pltpu.VMEM((tm, tn), F32)] if nk > 1 else [],
        compiler_params=_cp(("parallel", "parallel", "arbitrary")),
    )(a, b, *[e for e, _ in extras], *([behind] if n_behind else []))
    return res


def _row_tile(s):
    return min(512, s)


def ln_mod_fwd(x, g, sc, sh, name):
    bsz, s, d = x.shape
    ts = _row_tile(s)

    def body(x_ref, g_ref, sc_ref, sh_ref, o_ref):
        xv = x_ref[...]
        r = lax.rsqrt(jnp.mean(xv * xv, axis=-1, keepdims=True) + EPS)
        o_ref[...] = ((xv * r * g_ref[...]) * (1.0 + sc_ref[...]) + sh_ref[...]).astype(o_ref.dtype)

    row = pl.BlockSpec((None, ts, d), lambda b, i: (b, i, 0))
    vec = pl.BlockSpec((None, 1, d), lambda b, i: (b, 0, 0))
    return pl.pallas_call(
        body, name=name, grid=(bsz, s // ts),
        in_specs=[row, pl.BlockSpec((1, d), lambda b, i: (0, 0)), vec, vec],
        out_specs=row, out_shape=jax.ShapeDtypeStruct(x.shape, BF16),
        compiler_params=_cp(("parallel", "parallel")),
    )(x, g, sc, sh)


def ln_mod_bwd(dh, x, dres, g, sc, name):
    bsz, s, d = x.shape
    ts = _row_tile(s)

    def body(dh_ref, x_ref, dres_ref, g_ref, sc_ref, dx_ref, dsc_ref, dsh_ref, dg_ref):
        b, i = pl.program_id(0), pl.program_id(1)
        xv, dhv, gv = x_ref[...], dh_ref[...], g_ref[...]
        r = lax.rsqrt(jnp.mean(xv * xv, axis=-1, keepdims=True) + EPS)
        xn = xv * r
        a = dhv * (1.0 + sc_ref[...])
        dxn = a * gv
        dx_ref[...] = dres_ref[...] + r * (dxn - xn * jnp.mean(dxn * xn, axis=-1, keepdims=True))
        p_sc = jnp.sum(dhv * (xn * gv), axis=0, keepdims=True)
        p_sh = jnp.sum(dhv, axis=0, keepdims=True)
        p_g = jnp.sum(a * xn, axis=0, keepdims=True)

        @pl.when(i == 0)
        def _():
            dsc_ref[...] = p_sc
            dsh_ref[...] = p_sh

        @pl.when(i > 0)
        def _():
            dsc_ref[...] += p_sc
            dsh_ref[...] += p_sh

        @pl.when((i == 0) & (b == 0))
        def _():
            dg_ref[...] = p_g

        @pl.when((i > 0) | (b > 0))
        def _():
            dg_ref[...] += p_g

    row = pl.BlockSpec((None, ts, d), lambda b, i: (b, i, 0))
    vec = pl.BlockSpec((None, 1, d), lambda b, i: (b, 0, 0))
    one = pl.BlockSpec((1, d), lambda b, i: (0, 0))
    return pl.pallas_call(
        body, name=name, grid=(bsz, s // ts),
        in_specs=[row, row, row, one, vec],
        out_specs=[row, vec, vec, one],
        out_shape=[jax.ShapeDtypeStruct(x.shape, F32), jax.ShapeDtypeStruct((bsz, 1, d), F32),
                   jax.ShapeDtypeStruct((bsz, 1, d), F32), jax.ShapeDtypeStruct((1, d), F32)],
        compiler_params=_cp(("arbitrary", "arbitrary")),
    )(dh, x, dres, g, sc)


def gate_bwd(dx, mm, gate, name):
    bsz, s, d = dx.shape
    ts = _row_tile(s)

    def body(dx_ref, m_ref, g_ref, dm_ref, dg_ref):
        i = pl.program_id(1)
        dxv = dx_ref[...]
        dm_ref[...] = (dxv * g_ref[...]).astype(dm_ref.dtype)
        p = jnp.sum(dxv * m_ref[...], axis=0, keepdims=True)

        @pl.when(i == 0)
        def _():
            dg_ref[...] = p

        @pl.when(i > 0)
        def _():
            dg_ref[...] += p

    row = pl.BlockSpec((None, ts, d), lambda b, i: (b, i, 0))
    vec = pl.BlockSpec((None, 1, d), lambda b, i: (b, 0, 0))
    return pl.pallas_call(
        body, name=name, grid=(bsz, s // ts),
        in_specs=[row, row, vec], out_specs=[row, vec],
        out_shape=[jax.ShapeDtypeStruct(dx.shape, BF16), jax.ShapeDtypeStruct((bsz, 1, d), F32)],
        compiler_params=_cp(("parallel", "arbitrary")),
    )(dx, mm, gate)


def loss_head(x, g, tgt, name):
    bsz, s, d = x.shape
    ts = _row_tile(s)

    def body(x_ref, g_ref, t_ref, dx_ref, dg_ref, l_ref):
        b, i = pl.program_id(0), pl.program_id(1)
        xv, gv = x_ref[...], g_ref[...]
        r = lax.rsqrt(jnp.mean(xv * xv, axis=-1, keepdims=True) + EPS)
        xn = xv * r
        e = xn * gv - t_ref[...]
        dy = e * (1.0 / d)
        dxn = dy * gv
        dx_ref[...] = r * (dxn - xn * jnp.mean(dxn * xn, axis=-1, keepdims=True))
        p_g = jnp.sum(dy * xn, axis=0, keepdims=True)
        p_l = jnp.zeros((1, LANES), F32) + jnp.sum(e * e) * (0.5 / d)
        first = (i == 0) & (b == 0)

        @pl.when(first)
        def _():
            dg_ref[...] = p_g
            l_ref[...] = p_l

        @pl.when(jnp.logical_not(first))
        def _():
            dg_ref[...] += p_g
            l_ref[...] += p_l

    row = pl.BlockSpec((None, ts, d), lambda b, i: (b, i, 0))
    one = pl.BlockSpec((1, d), lambda b, i: (0, 0))
    return pl.pallas_call(
        body, name=name, grid=(bsz, s // ts),
        in_specs=[row, one, row],
        out_specs=[row, one, pl.BlockSpec((1, LANES), lambda b, i: (0, 0))],
        out_shape=[jax.ShapeDtypeStruct(x.shape, F32), jax.ShapeDtypeStruct((1, d), F32),
                   jax.ShapeDtypeStruct((1, LANES), F32)],
        compiler_params=_cp(("arbitrary", "arbitrary")),
    )(x, g, tgt)


def _gmlp_chunk(u_raw, v_raw, ln_g, ln_b, w, bs_t, out_g):
    c = u_raw.shape[0]
    u, v = _gelu(u_raw), _gelu(v_raw)
    tril = lax.broadcasted_iota(jnp.int32, (c, c), 0) >= lax.broadcasted_iota(jnp.int32, (c, c), 1)
    ys = []
    for h in range(GM_HEADS):
        sl = slice(h * GM_HD, (h + 1) * GM_HD)
        vh = v[:, sl]
        xc = vh - jnp.mean(vh, axis=-1, keepdims=True)
        vn = xc * lax.rsqrt(jnp.mean(xc * xc, axis=-1, keepdims=True) + LN_EPS) * ln_g[:, sl] + ln_b[:, sl]
        gate = _nn(jnp.where(tril, w[h], 0.0), vn) + bs_t[:, h:h + 1]
        ys.append(u[:, sl] * gate)
    return _rms(jnp.concatenate(ys, axis=1), out_g)


def _gmlp_specs(bsz, nc):
    seg = lambda off: pl.BlockSpec((None, CHUNK, GMW), lambda b, c: (b, c, off // GMW))
    full = lambda shape: pl.BlockSpec(shape, lambda b, c: (0,) * len(shape))
    par = [full((1, GMW)), full((1, GMW)), full((GM_HEADS, CHUNK, CHUNK)), full((CHUNK, GM_HEADS)), full((1, GMW))]
    return seg, full, par


def gmlp_fwd(p, prm, name):
    bsz, s, _ = p.shape
    nc = s // CHUNK
    seg, _, par = _gmlp_specs(bsz, nc)

    def body(u_ref, v_ref, lg, lb, w, bt, og, o_ref):
        o_ref[...] = _gmlp_chunk(u_ref[...], v_ref[...], lg[...], lb[...], w[...], bt[...], og[...]).astype(o_ref.dtype)

    return pl.pallas_call(
        body, name=name, grid=(bsz, nc),
        in_specs=[seg(OFF_U), seg(OFF_V)] + par,
        out_specs=pl.BlockSpec((None, CHUNK, GMW), lambda b, c: (b, c, 0)),
        out_shape=jax.ShapeDtypeStruct((bsz, s, GMW), BF16),
        compiler_params=_cp(("parallel", "parallel")),
    )(p, p, *prm)


def _accumulate(first, refs, vals):
    @pl.when(first)
    def _():
        for r, v in zip(refs, vals):
            r[...] = v

    @pl.when(jnp.logical_not(first))
    def _():
        for r, v in zip(refs, vals):
            r[...] += v


def gmlp_bwd(p, dmix, prm, name):
    bsz, s, _ = p.shape
    nc = s // CHUNK
    seg, full, par = _gmlp_specs(bsz, nc)

    def body(u_ref, v_ref, do_ref, lg, lb, w, bt, og, du_ref, dv_ref, *dpar):
        first = (pl.program_id(0) == 0) & (pl.program_id(1) == 0)
        _, vjp = jax.vjp(_gmlp_chunk, u_ref[...], v_ref[...], lg[...], lb[...], w[...], bt[...], og[...])
        gr = vjp(do_ref[...])
        du_ref[...] = gr[0].astype(du_ref.dtype)
        dv_ref[...] = gr[1].astype(dv_ref.dtype)
        _accumulate(first, dpar, gr[2:])

    out_seg = pl.BlockSpec((None, CHUNK, GMW), lambda b, c: (b, c, 0))
    return pl.pallas_call(
        body, name=name, grid=(bsz, nc),
        in_specs=[seg(OFF_U), seg(OFF_V), out_seg] + par,
        out_specs=[out_seg, out_seg] + par,
        out_shape=[jax.ShapeDtypeStruct((bsz, s, GMW), BF16)] * 2 + [jax.ShapeDtypeStruct(x.shape, F32) for x in prm],
        compiler_params=_cp(("arbitrary", "arbitrary")),
    )(p, p, dmix, *prm)


def _attn_block(q, kp, kc, vp, vc, sinks, out_g, has_prev):
    w = q.shape[0]
    k2 = jnp.concatenate([kp, kc], axis=0)
    v2 = jnp.concatenate([vp, vc], axis=0)
    qi = lax.broadcasted_iota(jnp.int32, (w, 2 * w), 0)
    kj = lax.broadcasted_iota(jnp.int32, (w, 2 * w), 1)
    diff = qi + w - kj
    grp = ATT_HEADS // ATT_KV
    valid = (diff >= 0) & (diff < w) & ((kj >= w) | has_prev)
    valid = jnp.concatenate([valid] * grp, axis=0)
    outs = []
    for kv in range(ATT_KV):
        kh = k2[:, kv * ATT_HD:(kv + 1) * ATT_HD]
        vh = v2[:, kv * ATT_HD:(kv + 1) * ATT_HD]
        heads = range(kv * grp, (kv + 1) * grp)
        qs = jnp.concatenate([q[:, h * ATT_HD:(h + 1) * ATT_HD] for h in heads], axis=0)
        sink = jnp.concatenate([jnp.broadcast_to(sinks[:, h:h + 1], (w, 1)) for h in heads], axis=0)
        sc = jnp.where(valid, _nt(qs, kh) * (ATT_HD ** -0.5), NEG)
        m = jnp.maximum(jnp.max(sc, axis=-1, keepdims=True), sink)
        e = jnp.exp(sc - m)
        pr = e / (jnp.sum(e, axis=-1, keepdims=True) + jnp.exp(sink - m))
        o = _nn(pr, vh)
        outs += [o[gi * w:(gi + 1) * w] for gi in range(grp)]
    return _rms(jnp.concatenate(outs, axis=1), out_g)


ATT_QB = 4


def _attn_tiles(s):
    qb = min(ATT_QB, s // WINDOW)
    return qb, qb * WINDOW, s // (qb * WINDOW)


def attn_fwd(p, sinks, out_g, name):
    bsz, s, _ = p.shape
    qb, rows, steps = _attn_tiles(s)

    def body(q_ref, kp_ref, kc_ref, vp_ref, vc_ref, s_ref, g_ref, o_ref):
        n = pl.program_id(1)
        for w in range(qb):
            sl = pl.ds(w * WINDOW, WINDOW)
            before = pl.ds((w - 1) * WINDOW, WINDOW)
            kp = kp_ref[...] if w == 0 else kc_ref[before, :]
            vp = vp_ref[...] if w == 0 else vc_ref[before, :]
            o_ref[sl, :] = _attn_block(q_ref[sl, :], kp, kc_ref[sl, :], vp, vc_ref[sl, :], s_ref[...], g_ref[...],
                                       (n > 0) if w == 0 else True).astype(o_ref.dtype)

    cur = lambda off: pl.BlockSpec((None, rows, KVW), lambda b, n: (b, n, off // KVW))
    prev = lambda off: pl.BlockSpec((None, WINDOW, KVW), lambda b, n: (b, jnp.maximum(n * qb - 1, 0), off // KVW))
    return pl.pallas_call(
        body, name=name, grid=(bsz, steps),
        in_specs=[pl.BlockSpec((None, rows, ATW), lambda b, n: (b, n, OFF_Q // ATW)),
                  prev(OFF_K), cur(OFF_K), prev(OFF_VV), cur(OFF_VV),
                  pl.BlockSpec((1, ATT_HEADS), lambda b, n: (0, 0)), pl.BlockSpec((1, ATW), lambda b, n: (0, 0))],
        out_specs=pl.BlockSpec((None, rows, ATW), lambda b, n: (b, n, 0)),
        out_shape=jax.ShapeDtypeStruct((bsz, s, ATW), BF16),
        compiler_params=_cp(("parallel", "parallel")),
    )(p, p, p, p, p, sinks, out_g)


def attn_bwd(p, dmix, sinks, out_g, name):
    bsz, s, _ = p.shape
    qb, rows, steps = _attn_tiles(s)
    last = pl.ds(rows - WINDOW, WINDOW)

    def body(q_ref, kp_ref, kc_ref, vp_ref, vc_ref, do_ref, s_ref, g_ref,
             dq_ref, dk_ref, dv_ref, ds_ref, dg_ref, ck, cv):
        b, n = pl.program_id(0), pl.program_id(1)

        @pl.when(n == 0)
        def _():
            ck[...] = jnp.zeros_like(ck)
            cv[...] = jnp.zeros_like(cv)

        @pl.when(n < steps)
        def _():
            grads = []
            for w in range(qb):
                sl = pl.ds(w * WINDOW, WINDOW)
                before = pl.ds((w - 1) * WINDOW, WINDOW)
                kp = kp_ref[...] if w == 0 else kc_ref[before, :]
                vp = vp_ref[...] if w == 0 else vc_ref[before, :]
                fn = functools.partial(_attn_block, has_prev=(n > 0) if w == 0 else True)
                _, vjp = jax.vjp(fn, q_ref[sl, :], kp, kc_ref[sl, :], vp, vc_ref[sl, :], s_ref[...], g_ref[...])
                grads.append(vjp(do_ref[sl, :]))
                dq_ref[sl, :] = grads[-1][0].astype(dq_ref.dtype)
            dk_ref[...] = ck[...].astype(dk_ref.dtype)
            dv_ref[...] = cv[...].astype(dv_ref.dtype)
            dk_ref[last, :] = (ck[last, :] + grads[0][1]).astype(dk_ref.dtype)
            dv_ref[last, :] = (cv[last, :] + grads[0][3]).astype(dv_ref.dtype)
            for w in range(qb):
                sl = pl.ds(w * WINDOW, WINDOW)
                ck[sl, :] = grads[w][2] + (grads[w + 1][1] if w + 1 < qb else 0.0)
                cv[sl, :] = grads[w][4] + (grads[w + 1][3] if w + 1 < qb else 0.0)
            dsk = functools.reduce(lambda u, v: u + v, [g[5] for g in grads])
            dgg = functools.reduce(lambda u, v: u + v, [g[6] for g in grads])
            _accumulate((b == 0) & (n == 0), (ds_ref, dg_ref), (dsk, dgg))

        @pl.when(n == steps)
        def _():
            dk_ref[...] = ck[...].astype(dk_ref.dtype)
            dv_ref[...] = cv[...].astype(dv_ref.dtype)

    at = lambda n: jnp.minimum(n, steps - 1)
    cur = lambda off: pl.BlockSpec((None, rows, KVW), lambda b, n: (b, at(n), off // KVW))
    prev = lambda off: pl.BlockSpec((None, WINDOW, KVW), lambda b, n: (b, jnp.maximum(at(n) * qb - 1, 0), off // KVW))
    kv_out = pl.BlockSpec((None, rows, KVW), lambda b, n: (b, jnp.maximum(n - 1, 0), 0))
    return pl.pallas_call(
        body, name=name, grid=(bsz, steps + 1),
        in_specs=[pl.BlockSpec((None, rows, ATW), lambda b, n: (b, at(n), OFF_Q // ATW)),
                  prev(OFF_K), cur(OFF_K), prev(OFF_VV), cur(OFF_VV),
                  pl.BlockSpec((None, rows, ATW), lambda b, n: (b, at(n), GMW // ATW)),
                  pl.BlockSpec((1, ATT_HEADS), lambda b, n: (0, 0)), pl.BlockSpec((1, ATW), lambda b, n: (0, 0))],
        out_specs=[pl.BlockSpec((None, rows, ATW), lambda b, n: (b, at(n), 0)), kv_out, kv_out,
                   pl.BlockSpec((1, ATT_HEADS), lambda b, n: (0, 0)), pl.BlockSpec((1, ATW), lambda b, n: (0, 0))],
        out_shape=[jax.ShapeDtypeStruct((bsz, s, ATW), BF16), jax.ShapeDtypeStruct((bsz, s, KVW), BF16),
                   jax.ShapeDtypeStruct((bsz, s, KVW), BF16), jax.ShapeDtypeStruct((1, ATT_HEADS), F32),
                   jax.ShapeDtypeStruct((1, ATW), F32)],
        scratch_shapes=[pltpu.VMEM((rows, KVW), F32), pltpu.VMEM((rows, KVW), F32)],
        compiler_params=_cp(("arbitrary", "arbitrary")),
    )(p, p, p, p, p, dmix, sinks, out_g)


CONV_CT = 256


def _shift_down(x, j):
    if j == 0:
        return x
    rows = lax.broadcasted_iota(jnp.int32, x.shape, 0)
    return jnp.where(rows >= j, pltpu.roll(x, j, 0), 0.0)


def _shift_up(x, j):
    if j == 0:
        return x
    s = x.shape[0]
    rows = lax.broadcasted_iota(jnp.int32, x.shape, 0)
    return jnp.where(rows < s - j, pltpu.roll(x, s - j, 0), 0.0)


def conv_fwd(p, w, bias, name):
    bsz, s, _ = p.shape

    def body(x_ref, w_ref, b_ref, o_ref):
        xv, wv = x_ref[...], w_ref[...]
        pre = b_ref[...] + sum(wv[k:k + 1, :] * _shift_down(xv, CONV_K - 1 - k) for k in range(CONV_K))
        o_ref[...] = _silu(pre)

    blk = pl.BlockSpec((None, s, CONV_CT), lambda b, j: (b, 0, j))
    return pl.pallas_call(
        body, name=name, grid=(bsz, CCH // CONV_CT),
        in_specs=[blk, pl.BlockSpec((CONV_K, CONV_CT), lambda b, j: (0, j)), pl.BlockSpec((1, CONV_CT), lambda b, j: (0, j))],
        out_specs=blk, out_shape=jax.ShapeDtypeStruct((bsz, s, CCH), F32),
        compiler_params=_cp(("parallel", "parallel")),
    )(p, w, bias)


def conv_bwd(p, dxc, w, bias, name):
    bsz, s, _ = p.shape

    def body(x_ref, d_ref, w_ref, b_ref, dx_ref, dw_ref, db_ref):
        b = pl.program_id(1)
        xv, wv = x_ref[...], w_ref[...]
        xs = [_shift_down(xv, CONV_K - 1 - k) for k in range(CONV_K)]
        pre = b_ref[...] + sum(wv[k:k + 1, :] * xs[k] for k in range(CONV_K))
        sg = 1.0 / (1.0 + jnp.exp(-pre))
        dpre = d_ref[...] * (sg * (1.0 + pre * (1.0 - sg)))
        dx_ref[...] = sum(wv[k:k + 1, :] * _shift_up(dpre, CONV_K - 1 - k) for k in range(CONV_K)).astype(dx_ref.dtype)
        p_w = jnp.concatenate([jnp.sum(dpre * xs[k], axis=0, keepdims=True) for k in range(CONV_K)], axis=0)
        p_b = jnp.sum(dpre, axis=0, keepdims=True)
        _accumulate(b == 0, (dw_ref, db_ref), (p_w, p_b))

    blk = pl.BlockSpec((None, s, CONV_CT), lambda j, b: (b, 0, j))
    wsp = pl.BlockSpec((CONV_K, CONV_CT), lambda j, b: (0, j))
    bsp = pl.BlockSpec((1, CONV_CT), lambda j, b: (0, j))
    return pl.pallas_call(
        body, name=name, grid=(CCH // CONV_CT, bsz),
        in_specs=[blk, blk, wsp, bsp], out_specs=[blk, wsp, bsp],
        out_shape=[jax.ShapeDtypeStruct((bsz, s, CCH), BF16), jax.ShapeDtypeStruct((CONV_K, CCH), F32),
                   jax.ShapeDtypeStruct((1, CCH), F32)],
        compiler_params=_cp(("parallel", "arbitrary")),
    )(p, dxc, w, bias)


def _ssd_consts():
    c = CHUNK
    r = lax.broadcasted_iota(jnp.int32, (c, c), 0)
    q = lax.broadcasted_iota(jnp.int32, (c, c), 1)
    hrow = lax.broadcasted_iota(jnp.int32, (LANES, SSW), 0)
    hcol = lax.broadcasted_iota(jnp.int32, (LANES, SSW), 1) // SSM_HD
    expand = (hrow == hcol).astype(F32)
    return expand, (r >= q).astype(F32), (r <= q).astype(F32), r >= q


def _ssd_chunk(xc, dtr, z, prev_t, dt_bias, a_log, d_skip, norm_g):
    c = xc.shape[0]
    expand, tril1, triu1, causal = _ssd_consts()
    xs, bm, cm = xc[:, :SSW], xc[:, SSW:SSW + BCW], xc[:, SSW + BCW:]
    dt = _softplus(dtr + dt_bias)
    da = dt * (-jnp.exp(a_log))
    a_cs = _hdot(tril1, da)
    a_cs_t = _hdot(da.T, triu1)
    dt_e = _hdot(dt, expand)
    acs_e = _hdot(a_cs, expand)
    alast_e = acs_e[c - 1:c, :]
    dsk_e = _hdot(jnp.broadcast_to(d_skip, (8, LANES)), expand)[0:1, :]
    xdt = xs * dt_e
    hg = SSM_HEADS // SSM_GROUPS
    ys, new_t = [], []
    for g in range(SSM_GROUPS):
        bg = bm[:, g * SSM_STATE:(g + 1) * SSM_STATE]
        cg = cm[:, g * SSM_STATE:(g + 1) * SSM_STATE]
        sl = slice(g * GRW, (g + 1) * GRW)
        cb = _nt(cg, bg)
        xdt_g = xdt[:, sl]
        st = _tn(bg, xdt_g * jnp.exp(alast_e[:, sl] - acs_e[:, sl]))
        new_t.append(prev_t[:, sl] * jnp.exp(alast_e[:, sl]) + st)
        y_off = _nn(cg, prev_t[:, sl]) * jnp.exp(acs_e[:, sl])
        yd = []
        low = lax.broadcasted_iota(jnp.int32, (c, LANES), 1) < SSM_HD
        for pair in range(hg // 2):
            xp = xdt_g[:, pair * LANES:(pair + 1) * LANES]
            acc = None
            for side, xh in enumerate((jnp.where(low, xp, 0.0), jnp.where(low, 0.0, xp))):
                h = g * hg + 2 * pair + side
                decay = jnp.exp(jnp.where(causal, a_cs[:, h:h + 1] - a_cs_t[h:h + 1, :], NEG))
                part = _nn(cb * decay, xh)
                acc = part if acc is None else acc + part
            yd.append(acc)
        ys.append(jnp.concatenate(yd, axis=1) + y_off)
    y = (jnp.concatenate(ys, axis=1) + xs * dsk_e) * _silu(z)
    yn = [y[:, g * GRW:(g + 1) * GRW] * lax.rsqrt(jnp.mean(jnp.square(y[:, g * GRW:(g + 1) * GRW]), axis=-1, keepdims=True) + EPS)
          for g in range(SSM_GROUPS)]
    return jnp.concatenate(yn, axis=1) * norm_g, jnp.concatenate(new_t, axis=1)


def ssd_fwd(xc, p, prm, name):
    bsz, s, _ = p.shape
    nc = s // CHUNK

    def body(xc_ref, dt_ref, z_ref, db, al, dk, ng, o_ref, st_ref, state):
        @pl.when(pl.program_id(1) == 0)
        def _():
            state[...] = jnp.zeros_like(state)

        prev = state[...]
        st_ref[...] = prev
        out, new = _ssd_chunk(xc_ref[...], dt_ref[...], z_ref[...], prev, db[...], al[...], dk[...], ng[...])
        o_ref[...] = out.astype(o_ref.dtype)
        state[...] = new

    vec = pl.BlockSpec((1, LANES), lambda b, c: (0, 0))
    return pl.pallas_call(
        body, name=name, grid=(bsz, nc),
        in_specs=[pl.BlockSpec((None, CHUNK, CCH), lambda b, c: (b, c, 0)),
                  pl.BlockSpec((None, CHUNK, LANES), lambda b, c: (b, c, OFF_DT // LANES)),
                  pl.BlockSpec((None, CHUNK, SSW), lambda b, c: (b, c, OFF_Z // SSW)),
                  vec, vec, vec, pl.BlockSpec((1, SSW), lambda b, c: (0, 0))],
        out_specs=[pl.BlockSpec((None, CHUNK, SSW), lambda b, c: (b, c, 0)),
                   pl.BlockSpec((None, None, SSM_STATE, SSW), lambda b, c: (b, c, 0, 0))],
        out_shape=[jax.ShapeDtypeStruct((bsz, s, SSW), BF16), jax.ShapeDtypeStruct((bsz, nc, SSM_STATE, SSW), F32)],
        scratch_shapes=[pltpu.VMEM((SSM_STATE, SSW), F32)],
        compiler_params=_cp(("parallel", "arbitrary")),
    )(xc, p, p, *prm)


def ssd_bwd(xc, p, states, dmix, prm, name):
    bsz, s, _ = p.shape
    nc = s // CHUNK

    def body(xc_ref, dt_ref, z_ref, st_ref, do_ref, db, al, dk, ng, dxc_ref, ddt_ref, dz_ref, *rest):
        dpar, dstate = rest[:4], rest[4]
        b, c = pl.program_id(0), pl.program_id(1)

        @pl.when(c == 0)
        def _():
            dstate[...] = jnp.zeros_like(dstate)

        _, vjp = jax.vjp(_ssd_chunk, xc_ref[...], dt_ref[...], z_ref[...], st_ref[...], db[...], al[...], dk[...], ng[...])
        gr = vjp((do_ref[...], dstate[...]))
        dxc_ref[...] = gr[0]
        ddt_ref[...] = gr[1].astype(ddt_ref.dtype)
        dz_ref[...] = gr[2].astype(dz_ref.dtype)
        dstate[...] = gr[3]
        _accumulate((b == 0) & (c == 0), dpar, gr[4:])

    rv = lambda c: nc - 1 - c
    vec = pl.BlockSpec((1, LANES), lambda b, c: (0, 0))
    ngs = pl.BlockSpec((1, SSW), lambda b, c: (0, 0))
    return pl.pallas_call(
        body, name=name, grid=(bsz, nc),
        in_specs=[pl.BlockSpec((None, CHUNK, CCH), lambda b, c: (b, rv(c), 0)),
                  pl.BlockSpec((None, CHUNK, LANES), lambda b, c: (b, rv(c), OFF_DT // LANES)),
                  pl.BlockSpec((None, CHUNK, SSW), lambda b, c: (b, rv(c), OFF_Z // SSW)),
                  pl.BlockSpec((None, None, SSM_STATE, SSW), lambda b, c: (b, rv(c), 0, 0)),
                  pl.BlockSpec((None, CHUNK, SSW), lambda b, c: (b, rv(c), (GMW + ATW) // SSW)),
                  vec, vec, vec, ngs],
        out_specs=[pl.BlockSpec((None, CHUNK, CCH), lambda b, c: (b, rv(c), 0)),
                   pl.BlockSpec((None, CHUNK, LANES), lambda b, c: (b, rv(c), 0)),
                   pl.BlockSpec((None, CHUNK, SSW), lambda b, c: (b, rv(c), 0)),
                   vec, vec, vec, ngs],
        out_shape=[jax.ShapeDtypeStruct((bsz, s, CCH), F32), jax.ShapeDtypeStruct((bsz, s, LANES), BF16),
                   jax.ShapeDtypeStruct((bsz, s, SSW), BF16)] + [jax.ShapeDtypeStruct((1, LANES), F32)] * 3
                  + [jax.ShapeDtypeStruct((1, SSW), F32)],
        scratch_shapes=[pltpu.VMEM((SSM_STATE, SSW), F32)],
        compiler_params=_cp(("arbitrary", "arbitrary")),
    )(xc, p, p, states, dmix, *prm)


def _rows2d(a):
    return a.reshape(-1, a.shape[-1])


def _ew_tile(r, c):
    t = r
    while t * c > (1 << 20) and t % 16 == 0:
        t //= 2
    return t


def add_pair(g, theirs, core, name):
    k, r, c = g.shape
    h = r // 2
    tr = _ew_tile(h, c)
    nb = h // tr

    def body(c_ref, a_ref, b_ref, o_ref, ob_ref):
        s = a_ref[...] + b_ref[...]
        o_ref[...] = s
        ob_ref[...] = s.astype(ob_ref.dtype)

    blk = pl.BlockSpec((None, tr, c), lambda kk, i, cr: (kk, i, 0))
    return pl.pallas_call(
        body, name=name,
        grid_spec=pltpu.PrefetchScalarGridSpec(
            num_scalar_prefetch=1, grid=(k, nb),
            in_specs=[pl.BlockSpec((None, tr, c), lambda kk, i, cr: (kk, cr[0] * nb + i, 0)), blk],
            out_specs=[blk, blk]),
        out_shape=[jax.ShapeDtypeStruct(theirs.shape, F32), jax.ShapeDtypeStruct(theirs.shape, BF16)],
        compiler_params=_cp(("parallel", "parallel")),
    )(core.reshape(1).astype(jnp.int32), g, theirs)


def sum_own_recv(sums, recv, chip, name):
    _, h, c = sums.shape
    tr = _ew_tile(h, c)

    def body(k_ref, o_ref, r_ref, out_ref):
        s = o_ref[...]
        for j in range(3):
            s = s + r_ref[j].astype(F32)
        out_ref[...] = s

    return pl.pallas_call(
        body, name=name,
        grid_spec=pltpu.PrefetchScalarGridSpec(
            num_scalar_prefetch=1, grid=(h // tr,),
            in_specs=[pl.BlockSpec((None, tr, c), lambda i, kr: (kr[0], i, 0)),
                      pl.BlockSpec((3, tr, c), lambda i, kr: (0, i, 0))],
            out_specs=pl.BlockSpec((tr, c), lambda i, kr: (i, 0))),
        out_shape=jax.ShapeDtypeStruct((h, c), F32),
        compiler_params=_cp(("parallel",)),
    )(chip.reshape(1).astype(jnp.int32), sums, recv)


def _adam_math(w, m, v, g):
    mn = ADAM_B1 * m + (1.0 - ADAM_B1) * g
    vn = ADAM_B2 * v + (1.0 - ADAM_B2) * (g * g)
    mh = mn / (1.0 - ADAM_B1 ** ADAM_STEP)
    vh = vn / (1.0 - ADAM_B2 ** ADAM_STEP)
    return -ADAM_LR * (mh / (jnp.sqrt(vh) + ADAM_EPS) + ADAM_WD * w), mn, vn


def adamw_layer(w, m, v, g, layer, prev, name):
    nl, r, c = w.shape
    tr = _ew_tile(r, c * 2)

    def body(w_ref, m_ref, v_ref, g_ref, *rest):
        go_ref, d_ref, mo_ref, vo_ref = rest[-4:]
        gv = g_ref[...]
        dl, mn, vn = _adam_math(w_ref[...], m_ref[...], v_ref[...], gv)
        go_ref[...] = gv
        d_ref[...] = dl
        mo_ref[...] = mn
        vo_ref[...] = vn

    lay = pl.BlockSpec((None, tr, c), lambda i: (layer, i, 0))
    n_prev = 0 if prev is None else 4
    return pl.pallas_call(
        body, name=name, grid=(r // tr,),
        in_specs=[lay, lay, lay, pl.BlockSpec((tr, c), lambda i: (i, 0))] + [_ANY] * n_prev,
        out_specs=[lay] * 4, out_shape=[jax.ShapeDtypeStruct(w.shape, F32)] * 4,
        input_output_aliases={4 + i: i for i in range(n_prev)},
        compiler_params=_cp(("parallel",)),
    )(w, m, v, g, *(prev or ()))


def sum_devices(parts, name):
    n, r, c = parts.shape
    tr = _ew_tile(r, c * n)

    def body(p_ref, o_ref):
        s = p_ref[0]
        for j in range(1, n):
            s = s + p_ref[j]
        o_ref[...] = s

    return pl.pallas_call(
        body, name=name, grid=(r // tr,),
        in_specs=[pl.BlockSpec((n, tr, c), lambda i: (0, i, 0))],
        out_specs=pl.BlockSpec((tr, c), lambda i: (i, 0)),
        out_shape=jax.ShapeDtypeStruct((r, c), F32),
        compiler_params=_cp(("parallel",)),
    )(parts)


def adamw(w, m, v, g, name):
    r, c = w.shape
    tr = _ew_tile(r, c * 2)

    def body(w_ref, m_ref, v_ref, g_ref, d_ref, mo_ref, vo_ref):
        gv = g_ref[...]
        mn = ADAM_B1 * m_ref[...] + (1.0 - ADAM_B1) * gv
        vn = ADAM_B2 * v_ref[...] + (1.0 - ADAM_B2) * (gv * gv)
        mh = mn / (1.0 - ADAM_B1 ** ADAM_STEP)
        vh = vn / (1.0 - ADAM_B2 ** ADAM_STEP)
        d_ref[...] = -ADAM_LR * (mh / (jnp.sqrt(vh) + ADAM_EPS) + ADAM_WD * w_ref[...])
        mo_ref[...] = mn
        vo_ref[...] = vn

    blk = pl.BlockSpec((tr, c), lambda i: (i, 0))
    return pl.pallas_call(
        body, name=name, grid=(r // tr,), in_specs=[blk] * 4, out_specs=[blk] * 3,
        out_shape=[jax.ShapeDtypeStruct((r, c), F32)] * 3,
        compiler_params=_cp(("parallel",)),
    )(w, m, v, g)


def _place():
    x, y, c = lax.axis_index("x"), lax.axis_index("y"), lax.axis_index("c")
    chips = [(1 - x, y), (x, 1 - y), (1 - x, 1 - y)]
    return x, y, c, chips


def all_gather_small(v, name):
    r, w = v.shape

    def body(x_ref, out_ref, send_sems, recv_sems, local_sem):
        x, y, c, chips = _place()
        me, sibling = (x, y, c), (x, y, 1 - c)

        def rows(px, py, pc):
            return out_ref.at[pl.ds((4 * px + 2 * py + pc) * r, r), :]

        def copy(k, block, to, src=None):
            return pltpu.make_async_remote_copy(
                src_ref=rows(*block) if src is None else src, dst_ref=rows(*block),
                send_sem=send_sems.at[k], recv_sem=recv_sems.at[k], device_id=to, device_id_type=MESH)

        mine = pltpu.make_async_copy(x_ref, rows(*me), local_sem)
        mine.start()
        first = [copy(0, me, sibling, src=x_ref)]
        first += [copy(1 + j, me, (*chip, c), src=x_ref) for j, chip in enumerate(chips)]
        for cp in first:
            cp.start()
        passed = [copy(4 + j, (*chip, c), sibling) for j, chip in enumerate(chips)]
        for j, chip in enumerate(chips):
            copy(1 + j, (*chip, c), me).wait_recv()
            passed[j].start()
        copy(0, sibling, me).wait_recv()
        for j, chip in enumerate(chips):
            copy(4 + j, (*chip, 1 - c), me).wait_recv()
        for cp in first + passed:
            cp.wait_send()
        mine.wait()

    out = pl.pallas_call(
        body, name=name, out_shape=jax.ShapeDtypeStruct((8 * r, w), v.dtype),
        in_specs=[pl.BlockSpec(memory_space=pltpu.VMEM)], out_specs=pl.BlockSpec(memory_space=pltpu.VMEM),
        scratch_shapes=[pltpu.SemaphoreType.DMA((7,)), pltpu.SemaphoreType.DMA((7,)), pltpu.SemaphoreType.DMA],
        compiler_params=pltpu.CompilerParams(vmem_limit_bytes=VMEM_LIMIT),
    )(v)
    return out.reshape(8, r, w)


_HBM = pl.BlockSpec(memory_space=pltpu.HBM)


_SEM = pl.BlockSpec(memory_space=pltpu.SEMAPHORE)
_ANY = pl.BlockSpec(memory_space=pl.ANY)
_EFFECT = pltpu.SideEffectType.DATAFLOW_SIDE_EFFECTING


def _hbm(a):
    return pltpu.with_memory_space_constraint(a, pltpu.HBM)


def split_copy_start(srcs, land_shapes, copies, after, name):
    n, nl = len(srcs), len(land_shapes)
    ncopy = [0]

    def body(*refs):
        ins, lands = refs[:n], refs[n:n + nl]
        send_sems, recv_sems = refs[n + nl + 1], refs[n + nl + 2]
        token = refs[-1]
        x, y, c, chips = _place()
        for k, (src, dst, to) in enumerate(copies(x, y, c, chips, ins, lands)):
            pltpu.make_async_remote_copy(src_ref=src, dst_ref=dst, send_sem=send_sems.at[k], recv_sem=recv_sems.at[k],
                                         device_id=to, device_id_type=MESH).start()
        token[...] = jnp.zeros_like(token)

    ncopy[0] = len(copies(0, 0, 0, [(1, 0), (0, 1), (1, 1)], [None] * n, [None] * nl, count_only=True))
    k = ncopy[0]
    lands = [_hbm(lax.empty(s.shape, s.dtype)) for s in land_shapes]
    res = pl.pallas_call(
        body, name=name,
        out_shape=(pltpu.SemaphoreType.DMA((k,)), pltpu.SemaphoreType.DMA((k,)))
        + tuple(pltpu.HBM(s.shape, s.dtype) for s in srcs) + tuple(pltpu.HBM(s.shape, s.dtype) for s in land_shapes)
        + (jax.ShapeDtypeStruct((8, LANES), F32),),
        in_specs=[_HBM] * (n + nl) + [_ANY],
        out_specs=(_SEM, _SEM) + (_HBM,) * (n + nl) + (pl.BlockSpec(memory_space=pltpu.VMEM),),
        input_output_aliases={i: 2 + i for i in range(n + nl)},
        compiler_params=pltpu.CompilerParams(has_side_effects=_EFFECT),
    )(*[_hbm(s) for s in srcs], *lands, after)
    return res[0], res[1], list(res[2:2 + n]), list(res[2 + n:2 + n + nl]), res[-1]


def split_copy_wait(send_sems, recv_sems, srcs, lands, copies, after, name):
    n, nl = len(srcs), len(lands)

    def body(*refs):
        ins, lnd = refs[:n], refs[n:n + nl]
        ss, rs = refs[n + nl], refs[n + nl + 1]
        x, y, c, chips = _place()
        for k, (src, dst, to) in enumerate(copies(x, y, c, chips, ins, lnd, receive=True)):
            cp = pltpu.make_async_remote_copy(src_ref=src, dst_ref=dst, send_sem=ss.at[k], recv_sem=rs.at[k],
                                              device_id=to, device_id_type=MESH)
            cp.wait_send()
            cp.wait_recv()

    res = pl.pallas_call(
        body, name=name,
        out_shape=tuple(pltpu.HBM(s.shape, s.dtype) for s in srcs) + tuple(pltpu.HBM(s.shape, s.dtype) for s in lands),
        in_specs=[_HBM] * (n + nl) + [_SEM, _SEM, _ANY], out_specs=(_HBM,) * (n + nl),
        input_output_aliases={i: i for i in range(n + nl)},
        compiler_params=pltpu.CompilerParams(has_side_effects=_EFFECT),
    )(*srcs, *lands, send_sems, recv_sems, after)
    return list(res[:n]), list(res[n:])


def _gather_copies(x, y, c, chips, ins, lands, receive=False, count_only=False):
    out = []
    for i in range(len(ins)):
        for cx, cy in chips:
            if count_only:
                out.append(None)
                continue
            h = ins[i].shape[0] // 2
            rows = pl.ds(c * h, h)
            k_dst = (2 * cx + cy) if receive else (2 * x + y)
            out.append((ins[i].at[rows, :], lands[i].at[k_dst, rows, :], (cx, cy, c)))
    for i in range(len(ins)):
        out.append(None if count_only else (ins[i], lands[i].at[2 * x + y], (x, y, 1 - c)))
    return out


def _swap_copies(x, y, c, chips, ins, lands, receive=False, count_only=False):
    out = []
    for i in range(len(ins)):
        if count_only:
            out.append(None)
            continue
        h = ins[i].shape[1] // 2
        out.append((ins[i].at[:, pl.ds((1 - c) * h, h), :], lands[i], (x, y, 1 - c)))
    return out


def _scatter_copies(x, y, c, chips, ins, lands, receive=False, count_only=False):
    out = []
    for i in range(len(ins)):
        for j, (cx, cy) in enumerate(chips):
            if count_only:
                out.append(None)
                continue
            out.append((ins[i].at[2 * cx + cy], lands[i].at[j], (cx, cy, c)))
    return out


def forward_halves(lands, name):
    n = len(lands)

    def body(*refs):
        ins, outs = refs[:n], refs[n:2 * n]
        send_sems, recv_sems = refs[2 * n:]
        x, y, c, chips = _place()
        sibling = (x, y, 1 - c)
        sent = []
        for i in range(n):
            h = ins[i].shape[1] // 2
            for j, (cx, cy) in enumerate(chips):
                blk = ins[i].at[2 * cx + cy, pl.ds(c * h, h), :]
                sent.append(pltpu.make_async_remote_copy(
                    src_ref=blk, dst_ref=outs[i].at[2 * cx + cy, pl.ds(c * h, h), :], send_sem=send_sems.at[3 * i + j],
                    recv_sem=recv_sems.at[3 * i + j], device_id=sibling, device_id_type=MESH))
                sent[-1].start()
        for i in range(n):
            h = ins[i].shape[1] // 2
            for j, (cx, cy) in enumerate(chips):
                theirs = outs[i].at[2 * cx + cy, pl.ds((1 - c) * h, h), :]
                pltpu.make_async_remote_copy(
                    src_ref=theirs, dst_ref=theirs, send_sem=send_sems.at[3 * i + j], recv_sem=recv_sems.at[3 * i + j],
                    device_id=sibling, device_id_type=MESH).wait_recv()
        for cp in sent:
            cp.wait_send()

    return pl.pallas_call(
        body, name=name, out_shape=[jax.ShapeDtypeStruct(s.shape, s.dtype) for s in lands],
        in_specs=[_HBM] * n, out_specs=[_HBM] * n, input_output_aliases={i: i for i in range(n)},
        scratch_shapes=[pltpu.SemaphoreType.DMA((3 * n,)), pltpu.SemaphoreType.DMA((3 * n,))],
    )(*lands)


def join_halves(halves, name):
    n = len(halves)

    def body(*refs):
        ins, outs = refs[:n], refs[n:2 * n]
        send_sems, recv_sems = refs[2 * n:]
        x, y, c, _ = _place()
        sibling = (x, y, 1 - c)
        sent = []
        for i in range(n):
            h = ins[i].shape[0]
            sent.append(pltpu.make_async_remote_copy(
                src_ref=ins[i], dst_ref=outs[i].at[pl.ds(c * h, h), :], send_sem=send_sems.at[i],
                recv_sem=recv_sems.at[i], device_id=sibling, device_id_type=MESH))
            sent[-1].start()
        for i in range(n):
            h = ins[i].shape[0]
            pltpu.make_async_remote_copy(
                src_ref=ins[i], dst_ref=outs[i].at[pl.ds((1 - c) * h, h), :], send_sem=send_sems.at[i],
                recv_sem=recv_sems.at[i], device_id=sibling, device_id_type=MESH).wait_recv()
        for cp in sent:
            cp.wait_send()

    full = pl.pallas_call(
        body, name=name, out_shape=[jax.ShapeDtypeStruct((2 * s.shape[0], s.shape[1]), F32) for s in halves],
        in_specs=[_HBM] * n, out_specs=[_HBM] * n,
        scratch_shapes=[pltpu.SemaphoreType.DMA((n,)), pltpu.SemaphoreType.DMA((n,))],
    )(*halves)
    ci = lax.axis_index("c")
    return [lax.dynamic_update_slice(f, s, (ci * s.shape[0], 0)) for f, s in zip(full, halves)]


_PACK_ROWS = 8 * LANES


def _pack(arrs):
    flat = jnp.concatenate([a.reshape(-1).astype(F32) for a in arrs])
    pad = (-flat.shape[0]) % _PACK_ROWS
    return jnp.pad(flat, (0, pad)).reshape(-1, LANES)


def _unpack(flat, shapes):
    flat = flat.reshape(-1)
    out, off = [], 0
    for s in shapes:
        n = int(np.prod(s))
        out.append(flat[off:off + n].reshape(s))
        off += n
    return out


_SEGS = [(0, OFF_U, GMW), (GMW, OFF_V, GMW), (2 * GMW, OFF_Q, ATW), (2 * GMW + ATW, OFF_K, KVW),
         (2 * GMW + ATW + KVW, OFF_VV, KVW), (2 * GMW + ATW + 2 * KVW, OFF_Z, SSW),
         (2 * GMW + ATW + 2 * KVW + SSW, OFF_XBC, CCH), (IN_W - SSM_HEADS, OFF_DT, SSM_HEADS)]


def _win_to_kernel_layout(w):
    out = jnp.zeros((w.shape[0], PW), w.dtype)
    for src, dst, wd in _SEGS:
        out = lax.dynamic_update_slice(out, w[:, src:src + wd], (0, dst))
    return out


def _win_from_kernel_layout(w):
    return jnp.concatenate([w[:, dst:dst + wd] for _, dst, wd in _SEGS], axis=1)


def _relu2(a):
    r = jnp.maximum(a, 0)
    return r * r


def kernel(x, c, ada_w, ada_b, norm1_g, w_in, gm_ln_g, gm_ln_b, gm_ws, gm_bs, gm_norm_g, attn_sinks, attn_norm_g, conv_w, conv_b, dt_bias, a_log, d_skip, ssm_norm_g, w_out, norm2_g, w_mlp1, w_mlp2, final_norm_g, loss_target, m_ada_w, m_ada_b, m_norm1_g, m_w_in, m_gm_ln_g, m_gm_ln_b, m_gm_ws, m_gm_bs, m_gm_norm_g, m_attn_sinks, m_attn_norm_g, m_conv_w, m_conv_b, m_dt_bias, m_a_log, m_d_skip, m_ssm_norm_g, m_w_out, m_norm2_g, m_w_mlp1, m_w_mlp2, m_final_norm_g, v_ada_w, v_ada_b, v_norm1_g, v_w_in, v_gm_ln_g, v_gm_ln_b, v_gm_ws, v_gm_bs, v_gm_norm_g, v_attn_sinks, v_attn_norm_g, v_conv_w, v_conv_b, v_dt_bias, v_a_log, v_d_skip, v_ssm_norm_g, v_w_out, v_norm2_g, v_w_mlp1, v_w_mlp2, v_final_norm_g):
    nl = ada_w.shape[0]
    bl, s, d = x.shape
    t = bl * s
    dff4 = w_mlp1.shape[2]
    dff = 4 * dff4
    mod_w = ada_w.shape[2]
    cw_w = conv_w.shape[2]
    xi, yi, ci = lax.axis_index("x"), lax.axis_index("y"), lax.axis_index("c")
    chip = 2 * xi + yi
    dev = 2 * chip + ci
    nex = 8 * bl

    g0 = all_gather_small(_pack([c, conv_w]), "ag_c")
    g0 = g0.reshape(8, -1)
    c_all = g0[:, :bl * d].reshape(nex, d)
    cw_parts = g0[0::2, bl * d:bl * d + conv_w.size].reshape(4, nl, CONV_K, cw_w)
    conv_w_full = cw_parts.transpose(1, 2, 0, 3).reshape(nl, CONV_K, CCH)

    def c_act(a):
        return _silu(a).astype(BF16)

    def to_bf16(a):
        return a.astype(BF16)

    mod_parts = []
    for l in range(nl):
        bias = lax.dynamic_slice(ada_b[l].reshape(1, -1), (0, chip * mod_w), (1, mod_w))
        mod_parts.append(_mm("nn", c_all, ada_w, dims=(nex, mod_w, d), tm=nex, tn=512, tk=d, out_dtypes=[F32],
                             name=f"mod_{l}", pro_a=c_act, pro_b=to_bf16,
                             b_spec=pl.BlockSpec((None, d, 512), lambda i, j, kk, l=l: (l, kk, j)),
                             extras=[(bias, pl.BlockSpec((1, 512), lambda i, j, kk: (0, j)))],
                             epi=lambda acc, bv: (acc + bv,))[0])
    g1 = all_gather_small(_pack(mod_parts), "ag_mod").reshape(8, -1)
    mod_all = g1[0::2, :nl * nex * mod_w].reshape(4, nl, nex, mod_w).transpose(1, 2, 0, 3).reshape(nl, nex, 4 * mod_w)
    mod = lax.dynamic_slice(mod_all, (0, dev * bl, 0), (nl, bl, 4 * mod_w))
    mods = [[mod[l, :, i * d:(i + 1) * d].reshape(bl, 1, d) for i in range(6)] for l in range(nl)]

    shards = [[w_in[l].astype(BF16), w_out[l].astype(BF16), w_mlp1[l].astype(BF16), w_mlp2[l].astype(BF16)]
              for l in range(nl)]
    groups = [[shards[0][i]] for i in range(4)] + [shards[l] for l in range(1, nl)]
    pending, after = [], g1
    for gi, grp in enumerate(groups):
        ss, rs, srcs, lands, after = split_copy_start(
            grp, [jax.ShapeDtypeStruct((4,) + a.shape, a.dtype) for a in grp], _gather_copies, after, f"gather_start_{gi}")
        pending.append((ss, rs, srcs, lands))
    mods[0][0] = mods[0][0] + after[0, 0]

    def fetch(gi, behind):
        ss, rs, srcs, lands = pending[gi]
        srcs, lands = split_copy_wait(ss, rs, srcs, lands, _gather_copies, behind, f"gather_wait_{gi}")
        return forward_halves(lands, f"gather_pass_{gi}")

    def as_win(g):
        return _win_to_kernel_layout(g.transpose(1, 0, 2).reshape(d, IN_W))

    wfull = [None] * nl
    row = lambda a: a.reshape(1, -1)
    pad16 = lambda a: jnp.pad(a.reshape(1, -1), ((0, 0), (0, LANES - SSM_HEADS)))
    tm_res = min(1024, s)

    def residual(acc, xt, gt):
        return acc, xt + gt * acc

    def res_extras(xin, gate, tm=tm_res):
        return [(xin.reshape(t, d), pl.BlockSpec((tm, 512), lambda i, j, kk: (i, j))),
                (gate, pl.BlockSpec((None, 1, 512), lambda i, j, kk: (i * tm // s, 0, j)))]

    w1_blk = lambda tk, tn: pl.BlockSpec((None, tk, tn), lambda i, j, kk: (j // (dff4 // tn), kk, j % (dff4 // tn)))

    saved = []
    xcur = x
    for l in range(nl):
        sh1, sc1, gt1, sh2, sc2, gt2 = mods[l]
        if l == 0:
            win = as_win(fetch(0, mod)[0])
        else:
            g_in, g_out, w1, g_2 = fetch(3 + l, xcur)
            win, wout, w2 = as_win(g_in), g_out.reshape(-1, d), g_2.reshape(dff, d)
        prm_a = (row(gm_ln_g[l]), row(gm_ln_b[l]), gm_ws[l], gm_bs[l].T, row(gm_norm_g[l]))
        prm_b = (row(attn_sinks[l]), row(attn_norm_g[l]))
        prm_c = (pad16(dt_bias[l]), pad16(a_log[l]), pad16(d_skip[l]), row(ssm_norm_g[l]))
        h1 = ln_mod_fwd(xcur, row(norm1_g[l]), sc1, sh1, f"ln1_fwd_{l}")
        p = _mm("nn", h1.reshape(t, d), win, dims=(t, PW, d), tm=1024, tn=512, tk=d, out_dtypes=[F32],
                name=f"proj_in_{l}")[0].reshape(bl, s, PW)
        out_a = gmlp_fwd(p, prm_a, f"gmlp_fwd_{l}")
        out_b = attn_fwd(p, *prm_b, f"attn_fwd_{l}")
        xc = conv_fwd(p, conv_w_full[l], row(conv_b[l]), f"conv_fwd_{l}")
        out_c, states = ssd_fwd(xc, p, prm_c, f"ssd_fwd_{l}")
        mix = jnp.concatenate([out_a, out_b, out_c], axis=-1)
        if l == 0:
            wout = fetch(1, mix)[0].reshape(-1, d)
        mm1, x2 = _mm("nn", mix.reshape(t, d), wout, dims=(t, d, d), tm=tm_res, tn=512, tk=d, out_dtypes=[F32, F32],
                      name=f"proj_out_{l}", extras=res_extras(xcur, gt1), epi=residual)
        x2 = x2.reshape(bl, s, d)
        h2 = ln_mod_fwd(x2, row(norm2_g[l]), sc2, sh2, f"ln2_fwd_{l}")
        if l == 0:
            w1 = fetch(2, h2)[0]
        a1 = _mm("nn", h2.reshape(t, d), w1, dims=(t, dff, d), tm=1024, tn=512, tk=d, out_dtypes=[BF16],
                 name=f"mlp1_{l}", b_spec=w1_blk(d, 512))[0]
        if l == 0:
            w2 = fetch(3, a1)[0].reshape(dff, d)
        tm2 = min(512, s)
        mm2, x3 = _mm("nn", a1, w2, dims=(t, d, dff), tm=tm2, tn=512, tk=dff, out_dtypes=[F32, F32],
                      name=f"mlp2_{l}", extras=res_extras(x2, gt2, tm2), epi=residual, pro_a=_relu2)
        x3 = x3.reshape(bl, s, d)
        wfull[l] = (win, wout, w1, w2)
        saved.append((xcur, h1, p, xc, states, mix, mm1.reshape(bl, s, d), x2, h2, a1, mm2.reshape(bl, s, d),
                      prm_a, prm_b, prm_c))
        xcur = x3

    dx, d_final_g, loss_part = loss_head(xcur, row(final_norm_g), loss_target, "loss_head")
    loss = lax.psum(loss_part[0, 0], ("x", "y", "c"))

    def rs_swap(grads, tag):
        ss, rs, srcs, lands, token = split_copy_start(
            grads, [jax.ShapeDtypeStruct((4, g.shape[1] // 2, g.shape[2]), F32) for g in grads],
            _swap_copies, grads[0], f"rs_swap_{tag}")
        return (ss, rs, srcs, lands), token

    def rs_begin(swap_state, tag, swapped_behind, start_behind=None):
        ss, rs, srcs, lands = swap_state
        grads, theirs = split_copy_wait(ss, rs, srcs, lands, _swap_copies, swapped_behind, f"rs_swapped_{tag}")
        sums = [add_pair(g, th, ci, f"rs_add_{tag}_{i}") for i, (g, th) in enumerate(zip(grads, theirs))]
        ss, rs, srcs, lands, token = split_copy_start(
            [sm[1] for sm in sums], [jax.ShapeDtypeStruct((3,) + sm[1].shape[1:], BF16) for sm in sums],
            _scatter_copies, sums[0][0] if start_behind is None else start_behind, f"rs_start_{tag}")
        return (ss, rs, srcs, lands, [sm[0] for sm in sums]), token

    def rs_end(state, behind, tag):
        ss, rs, srcs, lands, sums_f32 = state
        _, got = split_copy_wait(ss, rs, srcs, lands, _scatter_copies, behind, f"rs_wait_{tag}")
        halves = [sum_own_recv(sf, g, chip, f"rs_sum_{tag}_{i}") for i, (sf, g) in enumerate(zip(sums_f32, got))]
        return join_halves(halves, f"rs_join_{tag}")

    small_parts = [None] * nl
    dmods = [None] * nl
    reduced = [[None] * 4 for _ in range(nl)]
    pending_rs, rs_token = [], None
    part_slots = {"a": (0, 1), "m": (2, 3)}

    def finish(behind):
        for ll, part, state in pending_rs:
            for slot, blk in zip(part_slots[part], rs_end(state, behind, f"{ll}{part}")):
                reduced[ll][slot] = blk
        pending_rs.clear()

    for l in reversed(range(nl)):
        sh1, sc1, gt1, sh2, sc2, gt2 = mods[l]
        win, wout, w1, w2 = wfull[l]
        xin, h1, p, xc, states, mix, mm1, x2, h2, a1, mm2, prm_a, prm_b, prm_c = saved[l]
        if rs_token is not None:
            gt2 = gt2 + rs_token[0, 0]
        dm2, dgt2 = gate_bwd(dx, mm2, gt2, f"gate2_bwd_{l}")
        dm2 = dm2.reshape(t, d)
        da1 = _mm("nt", dm2, w2, dims=(t, dff, d), tm=1024, tn=512, tk=d, out_dtypes=[BF16], name=f"mlp2_dx_{l}",
                  extras=[(a1, pl.BlockSpec((1024 if t >= 1024 else t, 512), lambda i, j, kk: (i, j)))],
                  epi=lambda acc, av: (acc * (2.0 * jnp.maximum(av, 0).astype(F32)),))[0]
        dw2 = _mm("tn", a1, dm2, dims=(dff, d, t), tm=512, tn=2048, tk=2048, out_dtypes=[F32], name=f"mlp2_dw_{l}",
                  pro_a=_relu2)[0]
        dw1 = _mm("tn", h2.reshape(t, d), da1, dims=(d, dff, t), tm=512, tn=dff4, tk=2048, out_dtypes=[F32],
                  name=f"mlp1_dw_{l}", out_shapes=[(4, d, dff4)],
                  out_specs=[pl.BlockSpec((None, 512, dff4), lambda i, j, kk: (j, i, 0))])[0]
        swap_state, swap_token = rs_swap([dw1, dw2.reshape(4, dff4, d)], f"{l}m")
        dh2 = _mm("nt", da1, w1, dims=(t, d, dff), tm=1024, tn=512, tk=2048, out_dtypes=[F32], name=f"mlp1_dx_{l}",
                  b_spec=pl.BlockSpec((None, 512, 2048 if dff4 >= 2048 else dff4),
                                      lambda i, j, kk: (kk // (dff4 // min(2048, dff4)), j, kk % (dff4 // min(2048, dff4)))),
                  behind=swap_token)[0]
        mlp_state, mlp_token = rs_begin(swap_state, f"{l}m", dh2)
        sc2 = sc2 + mlp_token[0, 0]
        dx2, dsc2, dsh2, dn2 = ln_mod_bwd(dh2.reshape(bl, s, d), x2, dx, row(norm2_g[l]), sc2, f"ln2_bwd_{l}")
        dm1, dgt1 = gate_bwd(dx2, mm1, gt1, f"gate1_bwd_{l}")
        dm1 = dm1.reshape(t, d)
        dmix = _mm("nt", dm1, wout, dims=(t, d, d), tm=1024, tn=512, tk=d, out_dtypes=[F32],
                   name=f"proj_out_dx_{l}")[0].reshape(bl, s, d)
        dwout = _mm("tn", mix.reshape(t, d), dm1, dims=(d, d, t), tm=512, tn=2048, tk=2048, out_dtypes=[F32],
                    name=f"proj_out_dw_{l}")[0]
        du, dv, dlg, dlb, dws, dbst, dgng = gmlp_bwd(p, dmix, prm_a, f"gmlp_bwd_{l}")
        dq, dk, dvv, dsinks, dang = attn_bwd(p, dmix, *prm_b, f"attn_bwd_{l}")
        dxc, ddt, dz, ddtb, dalog, ddsk, dsng = ssd_bwd(xc, p, states, dmix, prm_c, f"ssd_bwd_{l}")
        dxbc, dcw, dcb = conv_bwd(p, dxc, conv_w_full[l], row(conv_b[l]), f"conv_bwd_{l}")
        dp = jnp.concatenate([dxbc, dq, dz, du, dv, dk, dvv, ddt, jnp.zeros((bl, s, PW - OFF_DT - LANES), BF16)],
                             axis=-1).reshape(t, PW)
        dwin = _mm("tn", h1.reshape(t, d), dp, dims=(d, PW, t), tm=512, tn=PW // 3, tk=2048, out_dtypes=[F32],
                   name=f"proj_in_dw_{l}")[0]
        dwin_blocks = _win_from_kernel_layout(dwin).reshape(d, 4, IN_W // 4).transpose(1, 0, 2)
        mixer_swap, swap_token = rs_swap([dwin_blocks, dwout.reshape(4, d // 4, d)], f"{l}a")
        dh1 = _mm("nt", dp, win, dims=(t, d, PW), tm=1024, tn=512, tk=PW, out_dtypes=[F32],
                  name=f"proj_in_dx_{l}", behind=swap_token)[0]
        dx, dsc1, dsh1, dn1 = ln_mod_bwd(dh1.reshape(bl, s, d), xin, dx2, row(norm1_g[l]), sc1, f"ln1_bwd_{l}")
        dmods[l] = jnp.concatenate([dsh1, dsc1, dgt1, dsh2, dsc2, dgt2], axis=-1).reshape(bl, 6 * d)
        small_parts[l] = [dn1, dlg, dlb, dws, dbst.T, dgng, dsinks, dang, dcw, dcb, ddtb[:, :SSM_HEADS],
                          dalog[:, :SSM_HEADS], ddsk[:, :SSM_HEADS], dsng, dn2]
        finish(dx)
        pending_rs.append((l, "m", mlp_state))
        if l > 0:
            state, rs_token = rs_begin(mixer_swap, f"{l}a", dx)
            pending_rs.append((l, "a", state))
    grad_x = dx

    big = [(w_in, m_w_in, v_w_in), (w_out, m_w_out, v_w_out), (w_mlp1, m_w_mlp1, v_w_mlp1), (w_mlp2, m_w_mlp2, v_w_mlp2)]
    big_out = [None] * 4
    for l in reversed(range(1, nl)):
        for i, (wt, mt, vt) in enumerate(big):
            big_out[i] = adamw_layer(wt, mt, vt, reduced[l][i], l, big_out[i], f"adamw_big_{i}_{l}")

    small_names = [norm1_g, gm_ln_g, gm_ln_b, gm_ws, gm_bs, gm_norm_g, attn_sinks, attn_norm_g, None, conv_b, dt_bias,
                   a_log, d_skip, ssm_norm_g, norm2_g]
    n_small = len(small_names)
    per_param = [jnp.stack([small_parts[l][i].reshape(-1) for l in range(nl)]) for i in range(n_small)]
    small_vec = _pack(per_param + [d_final_g])
    rs_small = small_vec.shape[0]
    dmod_local = jnp.stack(dmods, axis=1)
    g2 = all_gather_small(jnp.concatenate([small_vec, _pack([dmod_local])], axis=0), "ag_small")
    state, rs_token = rs_begin(mixer_swap, "0a", grad_x, start_behind=g2)
    pending_rs.append((0, "a", state))
    g2 = g2 + rs_token[0, 0]
    g_small = sum_devices(g2[:, :rs_small, :], "sum_small")
    dmod_all = g2[:, rs_small:, :].reshape(8, -1)[:, :bl * nl * 6 * d].reshape(nex, nl * 6 * d)
    g_ada_b = sum_devices(dmod_all.reshape(nex, -1, LANES), "sum_ada_b").reshape(nl, 6 * d)
    shapes = [(nl, int(np.prod(small_parts[0][i].shape))) for i in range(n_small)] + [(d,)]
    g_list = _unpack(g_small, shapes)
    g_conv_w = lax.dynamic_slice(g_list[8].reshape(nl, CONV_K, CCH), (0, 0, chip * cw_w), (nl, CONV_K, cw_w))

    dm_cols = lax.dynamic_slice(dmod_all.reshape(nex, nl, 6 * d), (0, 0, chip * mod_w), (nex, nl, mod_w))
    g_ada_w = _mm("tn", c_all, dm_cols.reshape(nex, nl * mod_w), dims=(d, nl * mod_w, nex), tm=512, tn=512, tk=nex,
                  out_dtypes=[F32], name="ada_w_grad", pro_a=c_act, pro_b=to_bf16, out_shapes=[(nl, d, mod_w)],
                  out_specs=[pl.BlockSpec((None, 512, 512), lambda i, j, kk: (j // (mod_w // 512), i, j % (mod_w // 512)))])[0]
    d_ada_w, m_ada_w_n, v_ada_w_n = [a.reshape(ada_w.shape) for a in
                                     adamw(_rows2d(ada_w), _rows2d(m_ada_w), _rows2d(v_ada_w), _rows2d(g_ada_w), "adamw_ada_w")]

    smalls = {
        "ada_b": (ada_b, m_ada_b, v_ada_b, g_ada_b), "norm1_g": (norm1_g, m_norm1_g, v_norm1_g, g_list[0]),
        "gm_ln_g": (gm_ln_g, m_gm_ln_g, v_gm_ln_g, g_list[1]), "gm_ln_b": (gm_ln_b, m_gm_ln_b, v_gm_ln_b, g_list[2]),
        "gm_ws": (gm_ws, m_gm_ws, v_gm_ws, g_list[3]), "gm_bs": (gm_bs, m_gm_bs, v_gm_bs, g_list[4]),
        "gm_norm_g": (gm_norm_g, m_gm_norm_g, v_gm_norm_g, g_list[5]),
        "attn_sinks": (attn_sinks, m_attn_sinks, v_attn_sinks, g_list[6]),
        "attn_norm_g": (attn_norm_g, m_attn_norm_g, v_attn_norm_g, g_list[7]),
        "conv_w": (conv_w, m_conv_w, v_conv_w, g_conv_w), "conv_b": (conv_b, m_conv_b, v_conv_b, g_list[9]),
        "dt_bias": (dt_bias, m_dt_bias, v_dt_bias, g_list[10]), "a_log": (a_log, m_a_log, v_a_log, g_list[11]),
        "d_skip": (d_skip, m_d_skip, v_d_skip, g_list[12]),
        "ssm_norm_g": (ssm_norm_g, m_ssm_norm_g, v_ssm_norm_g, g_list[13]),
        "norm2_g": (norm2_g, m_norm2_g, v_norm2_g, g_list[14]),
        "final_norm_g": (final_norm_g, m_final_norm_g, v_final_norm_g, g_list[15]),
    }
    keys = list(smalls)
    wv, mv, vv_, gv = [_pack([smalls[k][i].reshape(smalls[k][0].shape) for k in keys]) for i in range(4)]
    sd_, sm_, sv_ = adamw(wv, mv, vv_, gv, "adamw_small")
    shp = [smalls[k][0].shape for k in keys]
    small_out = {k: (smalls[k][3].reshape(smalls[k][0].shape), a, b, cc)
                 for k, a, b, cc in zip(keys, _unpack(sd_, shp), _unpack(sm_, shp), _unpack(sv_, shp))}

    late = jnp.zeros((8, LANES), F32) + (sv_[0, 0] + v_ada_w_n[0, 0, 0])
    for bo in big_out:
        if bo is not None:
            late = late + bo[3][nl - 1, 0, 0]
    finish(late)
    for i, (wt, mt, vt) in enumerate(big):
        big_out[i] = adamw_layer(wt, mt, vt, reduced[0][i], 0, big_out[i], f"adamw_big_{i}_0")

    out = {"ada_w": (g_ada_w, d_ada_w, m_ada_w_n, v_ada_w_n), "w_in": big_out[0], "w_out": big_out[1],
           "w_mlp1": big_out[2], "w_mlp2": big_out[3], **small_out}
    order = ["ada_w", "ada_b", "norm1_g", "w_in", "gm_ln_g", "gm_ln_b", "gm_ws", "gm_bs", "gm_norm_g", "attn_sinks",
             "attn_norm_g", "conv_w", "conv_b", "dt_bias", "a_log", "d_skip", "ssm_norm_g", "w_out", "norm2_g",
             "w_mlp1", "w_mlp2", "final_norm_g"]
    return (loss, grad_x, *[out[k][0] for k in order], *[out[k][1] for k in order],
            *[out[k][2] for k in order], *[out[k][3] for k in order])
```

```python
import functools
import math

import jax
import jax.numpy as jnp
import numpy as np
from jax import lax
from jax.experimental import pallas as pl
from jax.experimental.pallas import tpu as pltpu

F32 = jnp.float32
BF16 = jnp.bfloat16
HI = lax.Precision.HIGHEST
MESH = pl.DeviceIdType.MESH

CHUNK = 128
GM_HEADS, GM_HD = 4, 128
ATT_HEADS, ATT_KV, ATT_HD = 8, 2, 64
WINDOW = 128
SSM_HEADS, SSM_HD, SSM_GROUPS, SSM_STATE, CONV_K = 16, 64, 2, 128, 4
EPS = 1e-6
LN_EPS = 1e-5
NEG = -1e30
LANES = 128

GMW = GM_HEADS * GM_HD
ATW = ATT_HEADS * ATT_HD
KVW = ATT_KV * ATT_HD
SSW = SSM_HEADS * SSM_HD
BCW = SSM_GROUPS * SSM_STATE
CCH = SSW + 2 * BCW
GRW = SSW // SSM_GROUPS
IN_SIZES = (GMW, GMW, ATW, KVW, KVW, SSW, CCH, SSM_HEADS)
IN_W = sum(IN_SIZES)
OFF_XBC, OFF_Q, OFF_Z, OFF_U, OFF_V, OFF_K, OFF_VV, OFF_DT = 0, 1536, 2048, 3072, 3584, 4096, 4224, 4352
PW = 4608

ADAM_LR, ADAM_B1, ADAM_B2, ADAM_EPS, ADAM_WD, ADAM_STEP = 0.001, 0.9, 0.999, 1e-08, 0.01, 10

VMEM_LIMIT = 56 * 1024 * 1024


def _cp(sem=None):
    return pltpu.CompilerParams(dimension_semantics=sem, vmem_limit_bytes=VMEM_LIMIT)


_DN = {"nn": (((1,), (0,)), ((), ())), "nt": (((1,), (1,)), ((), ())), "tn": (((0,), (0,)), ((), ()))}


def _dot(form, a, b):
    return lax.dot_general(a.astype(BF16), b.astype(BF16), _DN[form], preferred_element_type=F32)


@jax.custom_vjp
def _nn(a, b):
    return _dot("nn", a, b)


@jax.custom_vjp
def _nt(a, b):
    return _dot("nt", a, b)


@jax.custom_vjp
def _tn(a, b):
    return _dot("tn", a, b)


_nn.defvjp(lambda a, b: (_dot("nn", a, b), (a, b)), lambda r, g: (_dot("nt", g, r[1]), _dot("tn", r[0], g)))
_nt.defvjp(lambda a, b: (_dot("nt", a, b), (a, b)), lambda r, g: (_dot("nn", g, r[1]), _dot("tn", g, r[0])))
_tn.defvjp(lambda a, b: (_dot("tn", a, b), (a, b)), lambda r, g: (_dot("nt", r[1], g), _dot("nn", r[0], g)))


def _hdot(a, b):
    return jnp.dot(a, b, precision=HI, preferred_element_type=F32)


def _silu(x):
    return x * (1.0 / (1.0 + jnp.exp(-x)))


def _softplus(x):
    return jnp.maximum(x, 0.0) + jnp.log1p(jnp.exp(-jnp.abs(x)))


def _gelu(x):
    return 0.5 * x * (1.0 + jnp.tanh(math.sqrt(2.0 / math.pi) * (x + 0.044715 * (x * x * x))))


def _rms(y, g):
    return y * lax.rsqrt(jnp.mean(y * y, axis=-1, keepdims=True) + EPS) * g


def _mm(form, a, b, *, dims, tm, tn, tk, out_dtypes, name, a_spec=None, b_spec=None, out_specs=None,
        out_shapes=None, extras=(), epi=None, pro_a=None, pro_b=None, behind=None):
    m, n, k = dims
    tm, tn, tk = min(tm, m), min(tn, n), min(tk, k)
    assert m % tm == 0 and n % tn == 0 and k % tk == 0, (name, dims, tm, tn, tk)
    nk = k // tk
    if a_spec is None:
        a_spec = (pl.BlockSpec((tk, tm), lambda i, j, kk: (kk, i)) if form == "tn"
                  else pl.BlockSpec((tm, tk), lambda i, j, kk: (i, kk)))
    if b_spec is None:
        b_spec = (pl.BlockSpec((tn, tk), lambda i, j, kk: (j, kk)) if form == "nt"
                  else pl.BlockSpec((tk, tn), lambda i, j, kk: (kk, j)))
    n_out = len(out_dtypes)
    if out_specs is None:
        out_specs = [pl.BlockSpec((tm, tn), lambda i, j, kk: (i, j))] * n_out
    if out_shapes is None:
        out_shapes = [(m, n)] * n_out
    ne = len(extras)
    n_behind = 0 if behind is None else 1

    def body(*refs):
        a_ref, b_ref = refs[0], refs[1]
        ex = refs[2:2 + ne]
        outs = refs[2 + ne + n_behind:2 + ne + n_behind + n_out]

        def write(val):
            res = epi(val, *[e[...] for e in ex]) if epi is not None else (val,)
            for o, r in zip(outs, res):
                o[...] = r.astype(o.dtype)

        av = a_ref[...]
        if pro_a is not None:
            av = pro_a(av)
        bv = b_ref[...]
        if pro_b is not None:
            bv = pro_b(bv)
        part = lax.dot_general(av, bv, _DN[form], preferred_element_type=F32)
        if nk == 1:
            write(part)
        else:
            acc = refs[-1]
            kk = pl.program_id(2)

            @pl.when(kk == 0)
            def _():
                acc[...] = part

            @pl.when(kk > 0)
            def _():
                acc[...] += part

            @pl.when(kk == nk - 1)
            def _():
                write(acc[...])

    res = pl.pallas_call(
        body, name=name, grid=(m // tm, n // tn, nk),
        in_specs=[a_spec, b_spec] + [s for _, s in extras] + [_ANY] * n_behind,
        out_specs=out_specs,
        out_shape=[jax.ShapeDtypeStruct(s, d) for s, d in zip(out_shapes, out_dtypes)],
        scratch_shapes=[pltpu.VMEM((tm, tn), F32)] if nk > 1 else [],
        compiler_params=_cp(("parallel", "parallel", "arbitrary")),
    )(a, b, *[e for e, _ in extras], *([behind] if n_behind else []))
    return res


def _row_tile(s):
    return min(512, s)


def ln_mod_fwd(x, g, sc, sh, name):
    bsz, s, d = x.shape
    ts = _row_tile(s)

    def body(x_ref, g_ref, sc_ref, sh_ref, o_ref):
        xv = x_ref[...]
        r = lax.rsqrt(jnp.mean(xv * xv, axis=-1, keepdims=True) + EPS)
        o_ref[...] = ((xv * r * g_ref[...]) * (1.0 + sc_ref[...]) + sh_ref[...]).astype(o_ref.dtype)

    row = pl.BlockSpec((None, ts, d), lambda b, i: (b, i, 0))
    vec = pl.BlockSpec((None, 1, d), lambda b, i: (b, 0, 0))
    return pl.pallas_call(
        body, name=name, grid=(bsz, s // ts),
        in_specs=[row, pl.BlockSpec((1, d), lambda b, i: (0, 0)), vec, vec],
        out_specs=row, out_shape=jax.ShapeDtypeStruct(x.shape, BF16),
        compiler_params=_cp(("parallel", "parallel")),
    )(x, g, sc, sh)


def ln_mod_bwd(dh, x, dres, g, sc, name):
    bsz, s, d = x.shape
    ts = _row_tile(s)

    def body(dh_ref, x_ref, dres_ref, g_ref, sc_ref, dx_ref, dsc_ref, dsh_ref, dg_ref):
        b, i = pl.program_id(0), pl.program_id(1)
        xv, dhv, gv = x_ref[...], dh_ref[...], g_ref[...]
        r = lax.rsqrt(jnp.mean(xv * xv, axis=-1, keepdims=True) + EPS)
        xn = xv * r
        a = dhv * (1.0 + sc_ref[...])
        dxn = a * gv
        dx_ref[...] = dres_ref[...] + r * (dxn - xn * jnp.mean(dxn * xn, axis=-1, keepdims=True))
        p_sc = jnp.sum(dhv * (xn * gv), axis=0, keepdims=True)
        p_sh = jnp.sum(dhv, axis=0, keepdims=True)
        p_g = jnp.sum(a * xn, axis=0, keepdims=True)

        @pl.when(i == 0)
        def _():
            dsc_ref[...] = p_sc
            dsh_ref[...] = p_sh

        @pl.when(i > 0)
        def _():
            dsc_ref[...] += p_sc
            dsh_ref[...] += p_sh

        @pl.when((i == 0) & (b == 0))
        def _():
            dg_ref[...] = p_g

        @pl.when((i > 0) | (b > 0))
        def _():
            dg_ref[...] += p_g

    row = pl.BlockSpec((None, ts, d), lambda b, i: (b, i, 0))
    vec = pl.BlockSpec((None, 1, d), lambda b, i: (b, 0, 0))
    one = pl.BlockSpec((1, d), lambda b, i: (0, 0))
    return pl.pallas_call(
        body, name=name, grid=(bsz, s // ts),
        in_specs=[row, row, row, one, vec],
        out_specs=[row, vec, vec, one],
        out_shape=[jax.ShapeDtypeStruct(x.shape, F32), jax.ShapeDtypeStruct((bsz, 1, d), F32),
                   jax.ShapeDtypeStruct((bsz, 1, d), F32), jax.ShapeDtypeStruct((1, d), F32)],
        compiler_params=_cp(("arbitrary", "arbitrary")),
    )(dh, x, dres, g, sc)


def gate_bwd(dx, mm, gate, name):
    bsz, s, d = dx.shape
    ts = _row_tile(s)

    def body(dx_ref, m_ref, g_ref, dm_ref, dg_ref):
        i = pl.program_id(1)
        dxv = dx_ref[...]
        dm_ref[...] = (dxv * g_ref[...]).astype(dm_ref.dtype)
        p = jnp.sum(dxv * m_ref[...], axis=0, keepdims=True)

        @pl.when(i == 0)
        def _():
            dg_ref[...] = p

        @pl.when(i > 0)
        def _():
            dg_ref[...] += p

    row = pl.BlockSpec((None, ts, d), lambda b, i: (b, i, 0))
    vec = pl.BlockSpec((None, 1, d), lambda b, i: (b, 0, 0))
    return pl.pallas_call(
        body, name=name, grid=(bsz, s // ts),
        in_specs=[row, row, vec], out_specs=[row, vec],
        out_shape=[jax.ShapeDtypeStruct(dx.shape, BF16), jax.ShapeDtypeStruct((bsz, 1, d), F32)],
        compiler_params=_cp(("parallel", "arbitrary")),
    )(dx, mm, gate)


def loss_head(x, g, tgt, name):
    bsz, s, d = x.shape
    ts = _row_tile(s)

    def body(x_ref, g_ref, t_ref, dx_ref, dg_ref, l_ref):
        b, i = pl.program_id(0), pl.program_id(1)
        xv, gv = x_ref[...], g_ref[...]
        r = lax.rsqrt(jnp.mean(xv * xv, axis=-1, keepdims=True) + EPS)
        xn = xv * r
        e = xn * gv - t_ref[...]
        dy = e * (1.0 / d)
        dxn = dy * gv
        dx_ref[...] = r * (dxn - xn * jnp.mean(dxn * xn, axis=-1, keepdims=True))
        p_g = jnp.sum(dy * xn, axis=0, keepdims=True)
        p_l = jnp.zeros((1, LANES), F32) + jnp.sum(e * e) * (0.5 / d)
        first = (i == 0) & (b == 0)

        @pl.when(first)
        def _():
            dg_ref[...] = p_g
            l_ref[...] = p_l

        @pl.when(jnp.logical_not(first))
        def _():
            dg_ref[...] += p_g
            l_ref[...] += p_l

    row = pl.BlockSpec((None, ts, d), lambda b, i: (b, i, 0))
    one = pl.BlockSpec((1, d), lambda b, i: (0, 0))
    return pl.pallas_call(
        body, name=name, grid=(bsz, s // ts),
        in_specs=[row, one, row],
        out_specs=[row, one, pl.BlockSpec((1, LANES), lambda b, i: (0, 0))],
        out_shape=[jax.ShapeDtypeStruct(x.shape, F32), jax.ShapeDtypeStruct((1, d), F32),
                   jax.ShapeDtypeStruct((1, LANES), F32)],
        compiler_params=_cp(("arbitrary", "arbitrary")),
    )(x, g, tgt)


def _gmlp_chunk(u_raw, v_raw, ln_g, ln_b, w, bs_t, out_g):
    c = u_raw.shape[0]
    u, v = _gelu(u_raw), _gelu(v_raw)
    tril = lax.broadcasted_iota(jnp.int32, (c, c), 0) >= lax.broadcasted_iota(jnp.int32, (c, c), 1)
    ys = []
    for h in range(GM_HEADS):
        sl = slice(h * GM_HD, (h + 1) * GM_HD)
        vh = v[:, sl]
        xc = vh - jnp.mean(vh, axis=-1, keepdims=True)
        vn = xc * lax.rsqrt(jnp.mean(xc * xc, axis=-1, keepdims=True) + LN_EPS) * ln_g[:, sl] + ln_b[:, sl]
        gate = _nn(jnp.where(tril, w[h], 0.0), vn) + bs_t[:, h:h + 1]
        ys.append(u[:, sl] * gate)
    return _rms(jnp.concatenate(ys, axis=1), out_g)


def _gmlp_specs(bsz, nc):
    seg = lambda off: pl.BlockSpec((None, CHUNK, GMW), lambda b, c: (b, c, off // GMW))
    full = lambda shape: pl.BlockSpec(shape, lambda b, c: (0,) * len(shape))
    par = [full((1, GMW)), full((1, GMW)), full((GM_HEADS, CHUNK, CHUNK)), full((CHUNK, GM_HEADS)), full((1, GMW))]
    return seg, full, par


def gmlp_fwd(p, prm, name):
    bsz, s, _ = p.shape
    nc = s // CHUNK
    seg, _, par = _gmlp_specs(bsz, nc)

    def body(u_ref, v_ref, lg, lb, w, bt, og, o_ref):
        o_ref[...] = _gmlp_chunk(u_ref[...], v_ref[...], lg[...], lb[...], w[...], bt[...], og[...]).astype(o_ref.dtype)

    return pl.pallas_call(
        body, name=name, grid=(bsz, nc),
        in_specs=[seg(OFF_U), seg(OFF_V)] + par,
        out_specs=pl.BlockSpec((None, CHUNK, GMW), lambda b, c: (b, c, 0)),
        out_shape=jax.ShapeDtypeStruct((bsz, s, GMW), BF16),
        compiler_params=_cp(("parallel", "parallel")),
    )(p, p, *prm)


def _accumulate(first, refs, vals):
    @pl.when(first)
    def _():
        for r, v in zip(refs, vals):
            r[...] = v

    @pl.when(jnp.logical_not(first))
    def _():
        for r, v in zip(refs, vals):
            r[...] += v


def gmlp_bwd(p, dmix, prm, name):
    bsz, s, _ = p.shape
    nc = s // CHUNK
    seg, full, par = _gmlp_specs(bsz, nc)

    def body(u_ref, v_ref, do_ref, lg, lb, w, bt, og, du_ref, dv_ref, *dpar):
        first = (pl.program_id(0) == 0) & (pl.program_id(1) == 0)
        _, vjp = jax.vjp(_gmlp_chunk, u_ref[...], v_ref[...], lg[...], lb[...], w[...], bt[...], og[...])
        gr = vjp(do_ref[...])
        du_ref[...] = gr[0].astype(du_ref.dtype)
        dv_ref[...] = gr[1].astype(dv_ref.dtype)
        _accumulate(first, dpar, gr[2:])

    out_seg = pl.BlockSpec((None, CHUNK, GMW), lambda b, c: (b, c, 0))
    return pl.pallas_call(
        body, name=name, grid=(bsz, nc),
        in_specs=[seg(OFF_U), seg(OFF_V), out_seg] + par,
        out_specs=[out_seg, out_seg] + par,
        out_shape=[jax.ShapeDtypeStruct((bsz, s, GMW), BF16)] * 2 + [jax.ShapeDtypeStruct(x.shape, F32) for x in prm],
        compiler_params=_cp(("arbitrary", "arbitrary")),
    )(p, p, dmix, *prm)


def _attn_block(q, kp, kc, vp, vc, sinks, out_g, has_prev):
    w = q.shape[0]
    k2 = jnp.concatenate([kp, kc], axis=0)
    v2 = jnp.concatenate([vp, vc], axis=0)
    qi = lax.broadcasted_iota(jnp.int32, (w, 2 * w), 0)
    kj = lax.broadcasted_iota(jnp.int32, (w, 2 * w), 1)
    diff = qi + w - kj
    grp = ATT_HEADS // ATT_KV
    valid = (diff >= 0) & (diff < w) & ((kj >= w) | has_prev)
    valid = jnp.concatenate([valid] * grp, axis=0)
    outs = []
    for kv in range(ATT_KV):
        kh = k2[:, kv * ATT_HD:(kv + 1) * ATT_HD]
        vh = v2[:, kv * ATT_HD:(kv + 1) * ATT_HD]
        heads = range(kv * grp, (kv + 1) * grp)
        qs = jnp.concatenate([q[:, h * ATT_HD:(h + 1) * ATT_HD] for h in heads], axis=0)
        sink = jnp.concatenate([jnp.broadcast_to(sinks[:, h:h + 1], (w, 1)) for h in heads], axis=0)
        sc = jnp.where(valid, _nt(qs, kh) * (ATT_HD ** -0.5), NEG)
        m = jnp.maximum(jnp.max(sc, axis=-1, keepdims=True), sink)
        e = jnp.exp(sc - m)
        pr = e / (jnp.sum(e, axis=-1, keepdims=True) + jnp.exp(sink - m))
        o = _nn(pr, vh)
        outs += [o[gi * w:(gi + 1) * w] for gi in range(grp)]
    return _rms(jnp.concatenate(outs, axis=1), out_g)


ATT_QB = 4


def _attn_tiles(s):
    qb = min(ATT_QB, s // WINDOW)
    return qb, qb * WINDOW, s // (qb * WINDOW)


def attn_fwd(p, sinks, out_g, name):
    bsz, s, _ = p.shape
    qb, rows, steps = _attn_tiles(s)

    def body(q_ref, kp_ref, kc_ref, vp_ref, vc_ref, s_ref, g_ref, o_ref):
        n = pl.program_id(1)
        for w in range(qb):
            sl = pl.ds(w * WINDOW, WINDOW)
            before = pl.ds((w - 1) * WINDOW, WINDOW)
            kp = kp_ref[...] if w == 0 else kc_ref[before, :]
            vp = vp_ref[...] if w == 0 else vc_ref[before, :]
            o_ref[sl, :] = _attn_block(q_ref[sl, :], kp, kc_ref[sl, :], vp, vc_ref[sl, :], s_ref[...], g_ref[...],
                                       (n > 0) if w == 0 else True).astype(o_ref.dtype)

    cur = lambda off: pl.BlockSpec((None, rows, KVW), lambda b, n: (b, n, off // KVW))
    prev = lambda off: pl.BlockSpec((None, WINDOW, KVW), lambda b, n: (b, jnp.maximum(n * qb - 1, 0), off // KVW))
    return pl.pallas_call(
        body, name=name, grid=(bsz, steps),
        in_specs=[pl.BlockSpec((None, rows, ATW), lambda b, n: (b, n, OFF_Q // ATW)),
                  prev(OFF_K), cur(OFF_K), prev(OFF_VV), cur(OFF_VV),
                  pl.BlockSpec((1, ATT_HEADS), lambda b, n: (0, 0)), pl.BlockSpec((1, ATW), lambda b, n: (0, 0))],
        out_specs=pl.BlockSpec((None, rows, ATW), lambda b, n: (b, n, 0)),
        out_shape=jax.ShapeDtypeStruct((bsz, s, ATW), BF16),
        compiler_params=_cp(("parallel", "parallel")),
    )(p, p, p, p, p, sinks, out_g)


def attn_bwd(p, dmix, sinks, out_g, name):
    bsz, s, _ = p.shape
    qb, rows, steps = _attn_tiles(s)
    last = pl.ds(rows - WINDOW, WINDOW)

    def body(q_ref, kp_ref, kc_ref, vp_ref, vc_ref, do_ref, s_ref, g_ref,
             dq_ref, dk_ref, dv_ref, ds_ref, dg_ref, ck, cv):
        b, n = pl.program_id(0), pl.program_id(1)

        @pl.when(n == 0)
        def _():
            ck[...] = jnp.zeros_like(ck)
            cv[...] = jnp.zeros_like(cv)

        @pl.when(n < steps)
        def _():
            grads = []
            for w in range(qb):
                sl = pl.ds(w * WINDOW, WINDOW)
                before = pl.ds((w - 1) * WINDOW, WINDOW)
                kp = kp_ref[...] if w == 0 else kc_ref[before, :]
                vp = vp_ref[...] if w == 0 else vc_ref[before, :]
                fn = functools.partial(_attn_block, has_prev=(n > 0) if w == 0 else True)
                _, vjp = jax.vjp(fn, q_ref[sl, :], kp, kc_ref[sl, :], vp, vc_ref[sl, :], s_ref[...], g_ref[...])
                grads.append(vjp(do_ref[sl, :]))
                dq_ref[sl, :] = grads[-1][0].astype(dq_ref.dtype)
            dk_ref[...] = ck[...].astype(dk_ref.dtype)
            dv_ref[...] = cv[...].astype(dv_ref.dtype)
            dk_ref[last, :] = (ck[last, :] + grads[0][1]).astype(dk_ref.dtype)
            dv_ref[last, :] = (cv[last, :] + grads[0][3]).astype(dv_ref.dtype)
            for w in range(qb):
                sl = pl.ds(w * WINDOW, WINDOW)
                ck[sl, :] = grads[w][2] + (grads[w + 1][1] if w + 1 < qb else 0.0)
                cv[sl, :] = grads[w][4] + (grads[w + 1][3] if w + 1 < qb else 0.0)
            dsk = functools.reduce(lambda u, v: u + v, [g[5] for g in grads])
            dgg = functools.reduce(lambda u, v: u + v, [g[6] for g in grads])
            _accumulate((b == 0) & (n == 0), (ds_ref, dg_ref), (dsk, dgg))

        @pl.when(n == steps)
        def _():
            dk_ref[...] = ck[...].astype(dk_ref.dtype)
            dv_ref[...] = cv[...].astype(dv_ref.dtype)

    at = lambda n: jnp.minimum(n, steps - 1)
    cur = lambda off: pl.BlockSpec((None, rows, KVW), lambda b, n: (b, at(n), off // KVW))
    prev = lambda off: pl.BlockSpec((None, WINDOW, KVW), lambda b, n: (b, jnp.maximum(at(n) * qb - 1, 0), off // KVW))
    kv_out = pl.BlockSpec((None, rows, KVW), lambda b, n: (b, jnp.maximum(n - 1, 0), 0))
    return pl.pallas_call(
        body, name=name, grid=(bsz, steps + 1),
        in_specs=[pl.BlockSpec((None, rows, ATW), lambda b, n: (b, at(n), OFF_Q // ATW)),
                  prev(OFF_K), cur(OFF_K), prev(OFF_VV), cur(OFF_VV),
                  pl.BlockSpec((None, rows, ATW), lambda b, n: (b, at(n), GMW // ATW)),
                  pl.BlockSpec((1, ATT_HEADS), lambda b, n: (0, 0)), pl.BlockSpec((1, ATW), lambda b, n: (0, 0))],
        out_specs=[pl.BlockSpec((None, rows, ATW), lambda b, n: (b, at(n), 0)), kv_out, kv_out,
                   pl.BlockSpec((1, ATT_HEADS), lambda b, n: (0, 0)), pl.BlockSpec((1, ATW), lambda b, n: (0, 0))],
        out_shape=[jax.ShapeDtypeStruct((bsz, s, ATW), BF16), jax.ShapeDtypeStruct((bsz, s, KVW), BF16),
                   jax.ShapeDtypeStruct((bsz, s, KVW), BF16), jax.ShapeDtypeStruct((1, ATT_HEADS), F32),
                   jax.ShapeDtypeStruct((1, ATW), F32)],
        scratch_shapes=[pltpu.VMEM((rows, KVW), F32), pltpu.VMEM((rows, KVW), F32)],
        compiler_params=_cp(("arbitrary", "arbitrary")),
    )(p, p, p, p, p, dmix, sinks, out_g)


CONV_CT = 256


def _shift_down(x, j):
    if j == 0:
        return x
    rows = lax.broadcasted_iota(jnp.int32, x.shape, 0)
    return jnp.where(rows >= j, pltpu.roll(x, j, 0), 0.0)


def _shift_up(x, j):
    if j == 0:
        return x
    s = x.shape[0]
    rows = lax.broadcasted_iota(jnp.int32, x.shape, 0)
    return jnp.where(rows < s - j, pltpu.roll(x, s - j, 0), 0.0)


def conv_fwd(p, w, bias, name):
    bsz, s, _ = p.shape

    def body(x_ref, w_ref, b_ref, o_ref):
        xv, wv = x_ref[...], w_ref[...]
        pre = b_ref[...] + sum(wv[k:k + 1, :] * _shift_down(xv, CONV_K - 1 - k) for k in range(CONV_K))
        o_ref[...] = _silu(pre)

    blk = pl.BlockSpec((None, s, CONV_CT), lambda b, j: (b, 0, j))
    return pl.pallas_call(
        body, name=name, grid=(bsz, CCH // CONV_CT),
        in_specs=[blk, pl.BlockSpec((CONV_K, CONV_CT), lambda b, j: (0, j)), pl.BlockSpec((1, CONV_CT), lambda b, j: (0, j))],
        out_specs=blk, out_shape=jax.ShapeDtypeStruct((bsz, s, CCH), F32),
        compiler_params=_cp(("parallel", "parallel")),
    )(p, w, bias)


def conv_bwd(p, dxc, w, bias, name):
    bsz, s, _ = p.shape

    def body(x_ref, d_ref, w_ref, b_ref, dx_ref, dw_ref, db_ref):
        b = pl.program_id(1)
        xv, wv = x_ref[...], w_ref[...]
        xs = [_shift_down(xv, CONV_K - 1 - k) for k in range(CONV_K)]
        pre = b_ref[...] + sum(wv[k:k + 1, :] * xs[k] for k in range(CONV_K))
        sg = 1.0 / (1.0 + jnp.exp(-pre))
        dpre = d_ref[...] * (sg * (1.0 + pre * (1.0 - sg)))
        dx_ref[...] = sum(wv[k:k + 1, :] * _shift_up(dpre, CONV_K - 1 - k) for k in range(CONV_K)).astype(dx_ref.dtype)
        p_w = jnp.concatenate([jnp.sum(dpre * xs[k], axis=0, keepdims=True) for k in range(CONV_K)], axis=0)
        p_b = jnp.sum(dpre, axis=0, keepdims=True)
        _accumulate(b == 0, (dw_ref, db_ref), (p_w, p_b))

    blk = pl.BlockSpec((None, s, CONV_CT), lambda j, b: (b, 0, j))
    wsp = pl.BlockSpec((CONV_K, CONV_CT), lambda j, b: (0, j))
    bsp = pl.BlockSpec((1, CONV_CT), lambda j, b: (0, j))
    return pl.pallas_call(
        body, name=name, grid=(CCH // CONV_CT, bsz),
        in_specs=[blk, blk, wsp, bsp], out_specs=[blk, wsp, bsp],
        out_shape=[jax.ShapeDtypeStruct((bsz, s, CCH), BF16), jax.ShapeDtypeStruct((CONV_K, CCH), F32),
                   jax.ShapeDtypeStruct((1, CCH), F32)],
        compiler_params=_cp(("parallel", "arbitrary")),
    )(p, dxc, w, bias)


def _ssd_consts():
    c = CHUNK
    r = lax.broadcasted_iota(jnp.int32, (c, c), 0)
    q = lax.broadcasted_iota(jnp.int32, (c, c), 1)
    hrow = lax.broadcasted_iota(jnp.int32, (LANES, SSW), 0)
    hcol = lax.broadcasted_iota(jnp.int32, (LANES, SSW), 1) // SSM_HD
    expand = (hrow == hcol).astype(F32)
    return expand, (r >= q).astype(F32), (r <= q).astype(F32), r >= q


def _ssd_chunk(xc, dtr, z, prev_t, dt_bias, a_log, d_skip, norm_g):
    c = xc.shape[0]
    expand, tril1, triu1, causal = _ssd_consts()
    xs, bm, cm = xc[:, :SSW], xc[:, SSW:SSW + BCW], xc[:, SSW + BCW:]
    dt = _softplus(dtr + dt_bias)
    da = dt * (-jnp.exp(a_log))
    a_cs = _hdot(tril1, da)
    a_cs_t = _hdot(da.T, triu1)
    dt_e = _hdot(dt, expand)
    acs_e = _hdot(a_cs, expand)
    alast_e = acs_e[c - 1:c, :]
    dsk_e = _hdot(jnp.broadcast_to(d_skip, (8, LANES)), expand)[0:1, :]
    xdt = xs * dt_e
    hg = SSM_HEADS // SSM_GROUPS
    ys, new_t = [], []
    for g in range(SSM_GROUPS):
        bg = bm[:, g * SSM_STATE:(g + 1) * SSM_STATE]
        cg = cm[:, g * SSM_STATE:(g + 1) * SSM_STATE]
        sl = slice(g * GRW, (g + 1) * GRW)
        cb = _nt(cg, bg)
        xdt_g = xdt[:, sl]
        st = _tn(bg, xdt_g * jnp.exp(alast_e[:, sl] - acs_e[:, sl]))
        new_t.append(prev_t[:, sl] * jnp.exp(alast_e[:, sl]) + st)
        y_off = _nn(cg, prev_t[:, sl]) * jnp.exp(acs_e[:, sl])
        yd = []
        low = lax.broadcasted_iota(jnp.int32, (c, LANES), 1) < SSM_HD
        for pair in range(hg // 2):
            xp = xdt_g[:, pair * LANES:(pair + 1) * LANES]
            acc = None
            for side, xh in enumerate((jnp.where(low, xp, 0.0), jnp.where(low, 0.0, xp))):
                h = g * hg + 2 * pair + side
                decay = jnp.exp(jnp.where(causal, a_cs[:, h:h + 1] - a_cs_t[h:h + 1, :], NEG))
                part = _nn(cb * decay, xh)
                acc = part if acc is None else acc + part
            yd.append(acc)
        ys.append(jnp.concatenate(yd, axis=1) + y_off)
    y = (jnp.concatenate(ys, axis=1) + xs * dsk_e) * _silu(z)
    yn = [y[:, g * GRW:(g + 1) * GRW] * lax.rsqrt(jnp.mean(jnp.square(y[:, g * GRW:(g + 1) * GRW]), axis=-1, keepdims=True) + EPS)
          for g in range(SSM_GROUPS)]
    return jnp.concatenate(yn, axis=1) * norm_g, jnp.concatenate(new_t, axis=1)


def ssd_fwd(xc, p, prm, name):
    bsz, s, _ = p.shape
    nc = s // CHUNK

    def body(xc_ref, dt_ref, z_ref, db, al, dk, ng, o_ref, st_ref, state):
        @pl.when(pl.program_id(0) == 0)
        def _():
            state[...] = jnp.zeros_like(state)

        for b in range(bsz):
            prev = state[b]
            st_ref[b, 0] = prev
            out, new = _ssd_chunk(xc_ref[b], dt_ref[b], z_ref[b], prev, db[...], al[...], dk[...], ng[...])
            o_ref[b] = out.astype(o_ref.dtype)
            state[b] = new

    vec = pl.BlockSpec((1, LANES), lambda c: (0, 0))
    return pl.pallas_call(
        body, name=name, grid=(nc,),
        in_specs=[pl.BlockSpec((bsz, CHUNK, CCH), lambda c: (0, c, 0)),
                  pl.BlockSpec((bsz, CHUNK, LANES), lambda c: (0, c, OFF_DT // LANES)),
                  pl.BlockSpec((bsz, CHUNK, SSW), lambda c: (0, c, OFF_Z // SSW)),
                  vec, vec, vec, pl.BlockSpec((1, SSW), lambda c: (0, 0))],
        out_specs=[pl.BlockSpec((bsz, CHUNK, SSW), lambda c: (0, c, 0)),
                   pl.BlockSpec((bsz, 1, SSM_STATE, SSW), lambda c: (0, c, 0, 0))],
        out_shape=[jax.ShapeDtypeStruct((bsz, s, SSW), BF16), jax.ShapeDtypeStruct((bsz, nc, SSM_STATE, SSW), F32)],
        scratch_shapes=[pltpu.VMEM((bsz, SSM_STATE, SSW), F32)],
        compiler_params=_cp(("arbitrary",)),
    )(xc, p, p, *prm)


def ssd_bwd(xc, p, states, dmix, prm, name):
    bsz, s, _ = p.shape
    nc = s // CHUNK

    def body(xc_ref, dt_ref, z_ref, st_ref, do_ref, db, al, dk, ng, dxc_ref, ddt_ref, dz_ref, *rest):
        dpar, dstate = rest[:4], rest[4]
        c = pl.program_id(0)

        @pl.when(c == 0)
        def _():
            dstate[...] = jnp.zeros_like(dstate)

        dpars = None
        for b in range(bsz):
            _, vjp = jax.vjp(_ssd_chunk, xc_ref[b], dt_ref[b], z_ref[b], st_ref[b, 0], db[...], al[...], dk[...], ng[...])
            gr = vjp((do_ref[b], dstate[b]))
            dxc_ref[b] = gr[0]
            ddt_ref[b] = gr[1].astype(ddt_ref.dtype)
            dz_ref[b] = gr[2].astype(dz_ref.dtype)
            dstate[b] = gr[3]
            dpars = gr[4:] if dpars is None else [u + v for u, v in zip(dpars, gr[4:])]
        _accumulate(c == 0, dpar, dpars)

    rv = lambda c: nc - 1 - c
    vec = pl.BlockSpec((1, LANES), lambda c: (0, 0))
    ngs = pl.BlockSpec((1, SSW), lambda c: (0, 0))
    return pl.pallas_call(
        body, name=name, grid=(nc,),
        in_specs=[pl.BlockSpec((bsz, CHUNK, CCH), lambda c: (0, rv(c), 0)),
                  pl.BlockSpec((bsz, CHUNK, LANES), lambda c: (0, rv(c), OFF_DT // LANES)),
                  pl.BlockSpec((bsz, CHUNK, SSW), lambda c: (0, rv(c), OFF_Z // SSW)),
                  pl.BlockSpec((bsz, 1, SSM_STATE, SSW), lambda c: (0, rv(c), 0, 0)),
                  pl.BlockSpec((bsz, CHUNK, SSW), lambda c: (0, rv(c), (GMW + ATW) // SSW)),
                  vec, vec, vec, ngs],
        out_specs=[pl.BlockSpec((bsz, CHUNK, CCH), lambda c: (0, rv(c), 0)),
                   pl.BlockSpec((bsz, CHUNK, LANES), lambda c: (0, rv(c), 0)),
                   pl.BlockSpec((bsz, CHUNK, SSW), lambda c: (0, rv(c), 0)),
                   vec, vec, vec, ngs],
        out_shape=[jax.ShapeDtypeStruct((bsz, s, CCH), F32), jax.ShapeDtypeStruct((bsz, s, LANES), BF16),
                   jax.ShapeDtypeStruct((bsz, s, SSW), BF16)] + [jax.ShapeDtypeStruct((1, LANES), F32)] * 3
                  + [jax.ShapeDtypeStruct((1, SSW), F32)],
        scratch_shapes=[pltpu.VMEM((bsz, SSM_STATE, SSW), F32)],
        compiler_params=_cp(("arbitrary",)),
    )(xc, p, p, states, dmix, *prm)


def _rows2d(a):
    return a.reshape(-1, a.shape[-1])


def _ew_tile(r, c):
    t = r
    while t * c > (1 << 20) and t % 16 == 0:
        t //= 2
    return t


def add_pair(g, theirs, core, name):
    k, r, c = g.shape
    h = r // 2
    tr = _ew_tile(h, c)
    nb = h // tr

    def body(c_ref, a_ref, b_ref, o_ref, ob_ref):
        s = a_ref[...] + b_ref[...]
        o_ref[...] = s
        ob_ref[...] = s.astype(ob_ref.dtype)

    blk = pl.BlockSpec((None, tr, c), lambda kk, i, cr: (kk, i, 0))
    return pl.pallas_call(
        body, name=name,
        grid_spec=pltpu.PrefetchScalarGridSpec(
            num_scalar_prefetch=1, grid=(k, nb),
            in_specs=[pl.BlockSpec((None, tr, c), lambda kk, i, cr: (kk, cr[0] * nb + i, 0)), blk],
            out_specs=[blk, blk]),
        out_shape=[jax.ShapeDtypeStruct(theirs.shape, F32), jax.ShapeDtypeStruct(theirs.shape, BF16)],
        compiler_params=_cp(("parallel", "parallel")),
    )(core.reshape(1).astype(jnp.int32), g, theirs)


def sum_own_recv(sums, recv, chip, core, name):
    _, h, c = sums.shape
    tr = _ew_tile(h, c)
    nb = h // tr

    def body(k_ref, o_ref, r_ref, out_ref):
        s = o_ref[...]
        for j in range(3):
            s = s + r_ref[j].astype(F32)
        out_ref[...] = s

    return pl.pallas_call(
        body, name=name,
        grid_spec=pltpu.PrefetchScalarGridSpec(
            num_scalar_prefetch=1, grid=(nb,),
            in_specs=[pl.BlockSpec((None, tr, c), lambda i, kr: (kr[0], i, 0)),
                      pl.BlockSpec((3, tr, c), lambda i, kr: (0, i, 0))],
            out_specs=pl.BlockSpec((tr, c), lambda i, kr: (kr[1] * nb + i, 0))),
        out_shape=jax.ShapeDtypeStruct((2 * h, c), F32),
        compiler_params=_cp(("parallel",)),
    )(jnp.stack([chip, core]).astype(jnp.int32), sums, recv)


def _adam_math(w, m, v, g):
    mn = ADAM_B1 * m + (1.0 - ADAM_B1) * g
    vn = ADAM_B2 * v + (1.0 - ADAM_B2) * (g * g)
    mh = mn / (1.0 - ADAM_B1 ** ADAM_STEP)
    vh = vn / (1.0 - ADAM_B2 ** ADAM_STEP)
    return -ADAM_LR * (mh / (jnp.sqrt(vh) + ADAM_EPS) + ADAM_WD * w), mn, vn


def adamw_layer(w, m, v, g, layer, prev, name):
    nl, r, c = w.shape
    tr = _ew_tile(r, c * 2)

    def body(w_ref, m_ref, v_ref, g_ref, *rest):
        go_ref, d_ref, mo_ref, vo_ref = rest[-4:]
        gv = g_ref[...]
        dl, mn, vn = _adam_math(w_ref[...], m_ref[...], v_ref[...], gv)
        go_ref[...] = gv
        d_ref[...] = dl
        mo_ref[...] = mn
        vo_ref[...] = vn

    lay = pl.BlockSpec((None, tr, c), lambda i: (layer, i, 0))
    n_prev = 0 if prev is None else 4
    return pl.pallas_call(
        body, name=name, grid=(r // tr,),
        in_specs=[lay, lay, lay, pl.BlockSpec((tr, c), lambda i: (i, 0))] + [_ANY] * n_prev,
        out_specs=[lay] * 4, out_shape=[jax.ShapeDtypeStruct(w.shape, F32)] * 4,
        input_output_aliases={4 + i: i for i in range(n_prev)},
        compiler_params=_cp(("parallel",)),
    )(w, m, v, g, *(prev or ()))


def sum_devices(parts, name):
    n, r, c = parts.shape
    tr = _ew_tile(r, c * n)

    def body(p_ref, o_ref):
        s = p_ref[0]
        for j in range(1, n):
            s = s + p_ref[j]
        o_ref[...] = s

    return pl.pallas_call(
        body, name=name, grid=(r // tr,),
        in_specs=[pl.BlockSpec((n, tr, c), lambda i: (0, i, 0))],
        out_specs=pl.BlockSpec((tr, c), lambda i: (i, 0)),
        out_shape=jax.ShapeDtypeStruct((r, c), F32),
        compiler_params=_cp(("parallel",)),
    )(parts)


def adamw(w, m, v, g, name):
    r, c = w.shape
    tr = _ew_tile(r, c * 2)

    def body(w_ref, m_ref, v_ref, g_ref, d_ref, mo_ref, vo_ref):
        gv = g_ref[...]
        mn = ADAM_B1 * m_ref[...] + (1.0 - ADAM_B1) * gv
        vn = ADAM_B2 * v_ref[...] + (1.0 - ADAM_B2) * (gv * gv)
        mh = mn / (1.0 - ADAM_B1 ** ADAM_STEP)
        vh = vn / (1.0 - ADAM_B2 ** ADAM_STEP)
        d_ref[...] = -ADAM_LR * (mh / (jnp.sqrt(vh) + ADAM_EPS) + ADAM_WD * w_ref[...])
        mo_ref[...] = mn
        vo_ref[...] = vn

    blk = pl.BlockSpec((tr, c), lambda i: (i, 0))
    return pl.pallas_call(
        body, name=name, grid=(r // tr,), in_specs=[blk] * 4, out_specs=[blk] * 3,
        out_shape=[jax.ShapeDtypeStruct((r, c), F32)] * 3,
        compiler_params=_cp(("parallel",)),
    )(w, m, v, g)


def _place():
    x, y, c = lax.axis_index("x"), lax.axis_index("y"), lax.axis_index("c")
    chips = [(1 - x, y), (x, 1 - y), (1 - x, 1 - y)]
    return x, y, c, chips


def all_gather_small(v, name):
    r, w = v.shape

    def body(x_ref, out_ref, send_sems, recv_sems, local_sem):
        x, y, c, chips = _place()
        me, sibling = (x, y, c), (x, y, 1 - c)

        def rows(px, py, pc):
            return out_ref.at[pl.ds((4 * px + 2 * py + pc) * r, r), :]

        def copy(k, block, to, src=None):
            return pltpu.make_async_remote_copy(
                src_ref=rows(*block) if src is None else src, dst_ref=rows(*block),
                send_sem=send_sems.at[k], recv_sem=recv_sems.at[k], device_id=to, device_id_type=MESH)

        mine = pltpu.make_async_copy(x_ref, rows(*me), local_sem)
        mine.start()
        first = [copy(0, me, sibling, src=x_ref)]
        first += [copy(1 + j, me, (*chip, c), src=x_ref) for j, chip in enumerate(chips)]
        for cp in first:
            cp.start()
        passed = [copy(4 + j, (*chip, c), sibling) for j, chip in enumerate(chips)]
        for j, chip in enumerate(chips):
            copy(1 + j, (*chip, c), me).wait_recv()
            passed[j].start()
        copy(0, sibling, me).wait_recv()
        for j, chip in enumerate(chips):
            copy(4 + j, (*chip, 1 - c), me).wait_recv()
        for cp in first + passed:
            cp.wait_send()
        mine.wait()

    out = pl.pallas_call(
        body, name=name, out_shape=jax.ShapeDtypeStruct((8 * r, w), v.dtype),
        in_specs=[pl.BlockSpec(memory_space=pltpu.VMEM)], out_specs=pl.BlockSpec(memory_space=pltpu.VMEM),
        scratch_shapes=[pltpu.SemaphoreType.DMA((7,)), pltpu.SemaphoreType.DMA((7,)), pltpu.SemaphoreType.DMA],
        compiler_params=pltpu.CompilerParams(vmem_limit_bytes=VMEM_LIMIT),
    )(v)
    return out.reshape(8, r, w)


_HBM = pl.BlockSpec(memory_space=pltpu.HBM)


_SEM = pl.BlockSpec(memory_space=pltpu.SEMAPHORE)
_ANY = pl.BlockSpec(memory_space=pl.ANY)
_EFFECT = pltpu.SideEffectType.DATAFLOW_SIDE_EFFECTING


def _hbm(a):
    return pltpu.with_memory_space_constraint(a, pltpu.HBM)


def split_copy_start(srcs, land_shapes, copies, after, name):
    n, nl = len(srcs), len(land_shapes)
    ncopy = [0]

    def body(*refs):
        ins, lands = refs[:n], refs[n:n + nl]
        send_sems, recv_sems = refs[n + nl + 1], refs[n + nl + 2]
        token = refs[-1]
        x, y, c, chips = _place()
        for k, (src, dst, to) in enumerate(copies(x, y, c, chips, ins, lands)):
            pltpu.make_async_remote_copy(src_ref=src, dst_ref=dst, send_sem=send_sems.at[k], recv_sem=recv_sems.at[k],
                                         device_id=to, device_id_type=MESH).start()
        token[...] = jnp.zeros_like(token)

    ncopy[0] = len(copies(0, 0, 0, [(1, 0), (0, 1), (1, 1)], [None] * n, [None] * nl, count_only=True))
    k = ncopy[0]
    lands = [_hbm(lax.empty(s.shape, s.dtype)) for s in land_shapes]
    res = pl.pallas_call(
        body, name=name,
        out_shape=(pltpu.SemaphoreType.DMA((k,)), pltpu.SemaphoreType.DMA((k,)))
        + tuple(pltpu.HBM(s.shape, s.dtype) for s in srcs) + tuple(pltpu.HBM(s.shape, s.dtype) for s in land_shapes)
        + (jax.ShapeDtypeStruct((8, LANES), F32),),
        in_specs=[_HBM] * (n + nl) + [_ANY],
        out_specs=(_SEM, _SEM) + (_HBM,) * (n + nl) + (pl.BlockSpec(memory_space=pltpu.VMEM),),
        input_output_aliases={i: 2 + i for i in range(n + nl)},
        compiler_params=pltpu.CompilerParams(has_side_effects=_EFFECT),
    )(*[_hbm(s) for s in srcs], *lands, after)
    return res[0], res[1], list(res[2:2 + n]), list(res[2 + n:2 + n + nl]), res[-1]


def split_copy_wait(send_sems, recv_sems, srcs, lands, copies, after, name):
    n, nl = len(srcs), len(lands)

    def body(*refs):
        ins, lnd = refs[:n], refs[n:n + nl]
        ss, rs = refs[n + nl], refs[n + nl + 1]
        x, y, c, chips = _place()
        for k, (src, dst, to) in enumerate(copies(x, y, c, chips, ins, lnd, receive=True)):
            cp = pltpu.make_async_remote_copy(src_ref=src, dst_ref=dst, send_sem=ss.at[k], recv_sem=rs.at[k],
                                              device_id=to, device_id_type=MESH)
            cp.wait_send()
            cp.wait_recv()

    res = pl.pallas_call(
        body, name=name,
        out_shape=tuple(pltpu.HBM(s.shape, s.dtype) for s in srcs) + tuple(pltpu.HBM(s.shape, s.dtype) for s in lands),
        in_specs=[_HBM] * (n + nl) + [_SEM, _SEM, _ANY], out_specs=(_HBM,) * (n + nl),
        input_output_aliases={i: i for i in range(n + nl)},
        compiler_params=pltpu.CompilerParams(has_side_effects=_EFFECT),
    )(*srcs, *lands, send_sems, recv_sems, after)
    return list(res[:n]), list(res[n:])


def _gather_copies(x, y, c, chips, ins, lands, receive=False, count_only=False):
    out = []
    for i in range(len(ins)):
        for cx, cy in chips:
            if count_only:
                out.append(None)
                continue
            h = ins[i].shape[0] // 2
            rows = pl.ds(c * h, h)
            k_dst = (2 * cx + cy) if receive else (2 * x + y)
            out.append((ins[i].at[rows, :], lands[i].at[k_dst, rows, :], (cx, cy, c)))
    for i in range(len(ins)):
        out.append(None if count_only else (ins[i], lands[i].at[2 * x + y], (x, y, 1 - c)))
    return out


def _swap_copies(x, y, c, chips, ins, lands, receive=False, count_only=False):
    out = []
    for i in range(len(ins)):
        if count_only:
            out.append(None)
            continue
        h = ins[i].shape[1] // 2
        out.append((ins[i].at[:, pl.ds((1 - c) * h, h), :], lands[i], (x, y, 1 - c)))
    return out


def _scatter_copies(x, y, c, chips, ins, lands, receive=False, count_only=False):
    out = []
    for i in range(len(ins)):
        for j, (cx, cy) in enumerate(chips):
            if count_only:
                out.append(None)
                continue
            out.append((ins[i].at[2 * cx + cy], lands[i].at[j], (cx, cy, c)))
    return out


def forward_halves(lands, name):
    n = len(lands)

    def body(*refs):
        ins, outs = refs[:n], refs[n:2 * n]
        send_sems, recv_sems = refs[2 * n:]
        x, y, c, chips = _place()
        sibling = (x, y, 1 - c)
        sent = []
        for i in range(n):
            h = ins[i].shape[1] // 2
            for j, (cx, cy) in enumerate(chips):
                blk = ins[i].at[2 * cx + cy, pl.ds(c * h, h), :]
                sent.append(pltpu.make_async_remote_copy(
                    src_ref=blk, dst_ref=outs[i].at[2 * cx + cy, pl.ds(c * h, h), :], send_sem=send_sems.at[3 * i + j],
                    recv_sem=recv_sems.at[3 * i + j], device_id=sibling, device_id_type=MESH))
                sent[-1].start()
        for i in range(n):
            h = ins[i].shape[1] // 2
            for j, (cx, cy) in enumerate(chips):
                theirs = outs[i].at[2 * cx + cy, pl.ds((1 - c) * h, h), :]
                pltpu.make_async_remote_copy(
                    src_ref=theirs, dst_ref=theirs, send_sem=send_sems.at[3 * i + j], recv_sem=recv_sems.at[3 * i + j],
                    device_id=sibling, device_id_type=MESH).wait_recv()
        for cp in sent:
            cp.wait_send()

    return pl.pallas_call(
        body, name=name, out_shape=[jax.ShapeDtypeStruct(s.shape, s.dtype) for s in lands],
        in_specs=[_HBM] * n, out_specs=[_HBM] * n, input_output_aliases={i: i for i in range(n)},
        scratch_shapes=[pltpu.SemaphoreType.DMA((3 * n,)), pltpu.SemaphoreType.DMA((3 * n,))],
    )(*lands)


def join_halves(halves, name):
    n = len(halves)

    def body(*refs):
        ins, outs = refs[:n], refs[n:2 * n]
        send_sems, recv_sems = refs[2 * n:]
        x, y, c, _ = _place()
        sibling = (x, y, 1 - c)
        sent = []
        for i in range(n):
            h = ins[i].shape[0] // 2
            sent.append(pltpu.make_async_remote_copy(
                src_ref=ins[i].at[pl.ds(c * h, h), :], dst_ref=outs[i].at[pl.ds(c * h, h), :], send_sem=send_sems.at[i],
                recv_sem=recv_sems.at[i], device_id=sibling, device_id_type=MESH))
            sent[-1].start()
        for i in range(n):
            h = ins[i].shape[0] // 2
            theirs = outs[i].at[pl.ds((1 - c) * h, h), :]
            pltpu.make_async_remote_copy(
                src_ref=theirs, dst_ref=theirs, send_sem=send_sems.at[i],
                recv_sem=recv_sems.at[i], device_id=sibling, device_id_type=MESH).wait_recv()
        for cp in sent:
            cp.wait_send()

    return pl.pallas_call(
        body, name=name, out_shape=[jax.ShapeDtypeStruct(s.shape, F32) for s in halves],
        in_specs=[_HBM] * n, out_specs=[_HBM] * n, input_output_aliases={i: i for i in range(n)},
        scratch_shapes=[pltpu.SemaphoreType.DMA((n,)), pltpu.SemaphoreType.DMA((n,))],
    )(*halves)


_PACK_ROWS = 8 * LANES


def _pack(arrs):
    flat = jnp.concatenate([a.reshape(-1).astype(F32) for a in arrs])
    pad = (-flat.shape[0]) % _PACK_ROWS
    return jnp.pad(flat, (0, pad)).reshape(-1, LANES)


def _unpack(flat, shapes):
    flat = flat.reshape(-1)
    out, off = [], 0
    for s in shapes:
        n = int(np.prod(s))
        out.append(flat[off:off + n].reshape(s))
        off += n
    return out


_SEGS = [(0, OFF_U, GMW), (GMW, OFF_V, GMW), (2 * GMW, OFF_Q, ATW), (2 * GMW + ATW, OFF_K, KVW),
         (2 * GMW + ATW + KVW, OFF_VV, KVW), (2 * GMW + ATW + 2 * KVW, OFF_Z, SSW),
         (2 * GMW + ATW + 2 * KVW + SSW, OFF_XBC, CCH), (IN_W - SSM_HEADS, OFF_DT, SSM_HEADS)]


def _win_to_kernel_layout(w):
    out = jnp.zeros((w.shape[0], PW), w.dtype)
    for src, dst, wd in _SEGS:
        out = lax.dynamic_update_slice(out, w[:, src:src + wd], (0, dst))
    return out


def _win_from_kernel_layout(w):
    return jnp.concatenate([w[:, dst:dst + wd] for _, dst, wd in _SEGS], axis=1)


def _relu2(a):
    r = jnp.maximum(a, 0)
    return r * r


def kernel(x, c, ada_w, ada_b, norm1_g, w_in, gm_ln_g, gm_ln_b, gm_ws, gm_bs, gm_norm_g, attn_sinks, attn_norm_g, conv_w, conv_b, dt_bias, a_log, d_skip, ssm_norm_g, w_out, norm2_g, w_mlp1, w_mlp2, final_norm_g, loss_target, m_ada_w, m_ada_b, m_norm1_g, m_w_in, m_gm_ln_g, m_gm_ln_b, m_gm_ws, m_gm_bs, m_gm_norm_g, m_attn_sinks, m_attn_norm_g, m_conv_w, m_conv_b, m_dt_bias, m_a_log, m_d_skip, m_ssm_norm_g, m_w_out, m_norm2_g, m_w_mlp1, m_w_mlp2, m_final_norm_g, v_ada_w, v_ada_b, v_norm1_g, v_w_in, v_gm_ln_g, v_gm_ln_b, v_gm_ws, v_gm_bs, v_gm_norm_g, v_attn_sinks, v_attn_norm_g, v_conv_w, v_conv_b, v_dt_bias, v_a_log, v_d_skip, v_ssm_norm_g, v_w_out, v_norm2_g, v_w_mlp1, v_w_mlp2, v_final_norm_g):
    nl = ada_w.shape[0]
    bl, s, d = x.shape
    t = bl * s
    dff4 = w_mlp1.shape[2]
    dff = 4 * dff4
    mod_w = ada_w.shape[2]
    cw_w = conv_w.shape[2]
    xi, yi, ci = lax.axis_index("x"), lax.axis_index("y"), lax.axis_index("c")
    chip = 2 * xi + yi
    dev = 2 * chip + ci
    nex = 8 * bl

    g0 = all_gather_small(_pack([c, conv_w]), "ag_c")
    g0 = g0.reshape(8, -1)
    c_all = g0[:, :bl * d].reshape(nex, d)
    cw_parts = g0[0::2, bl * d:bl * d + conv_w.size].reshape(4, nl, CONV_K, cw_w)
    conv_w_full = cw_parts.transpose(1, 2, 0, 3).reshape(nl, CONV_K, CCH)

    def c_act(a):
        return _silu(a).astype(BF16)

    def to_bf16(a):
        return a.astype(BF16)

    mod_parts = []
    for l in range(nl):
        bias = lax.dynamic_slice(ada_b[l].reshape(1, -1), (0, chip * mod_w), (1, mod_w))
        mod_parts.append(_mm("nn", c_all, ada_w, dims=(nex, mod_w, d), tm=nex, tn=512, tk=d, out_dtypes=[F32],
                             name=f"mod_{l}", pro_a=c_act, pro_b=to_bf16,
                             b_spec=pl.BlockSpec((None, d, 512), lambda i, j, kk, l=l: (l, kk, j)),
                             extras=[(bias, pl.BlockSpec((1, 512), lambda i, j, kk: (0, j)))],
                             epi=lambda acc, bv: (acc + bv,))[0])
    g1 = all_gather_small(_pack(mod_parts), "ag_mod").reshape(8, -1)
    mod_all = g1[0::2, :nl * nex * mod_w].reshape(4, nl, nex, mod_w).transpose(1, 2, 0, 3).reshape(nl, nex, 4 * mod_w)
    mod = lax.dynamic_slice(mod_all, (0, dev * bl, 0), (nl, bl, 4 * mod_w))
    mods = [[mod[l, :, i * d:(i + 1) * d].reshape(bl, 1, d) for i in range(6)] for l in range(nl)]

    shards = [[w_in[l].astype(BF16), w_out[l].astype(BF16), w_mlp1[l].astype(BF16), w_mlp2[l].astype(BF16)]
              for l in range(nl)]
    groups = [[shards[0][i]] for i in range(4)] + [shards[l] for l in range(1, nl)]
    pending, after = [], g1
    for gi, grp in enumerate(groups):
        ss, rs, srcs, lands, after = split_copy_start(
            grp, [jax.ShapeDtypeStruct((4,) + a.shape, a.dtype) for a in grp], _gather_copies, after, f"gather_start_{gi}")
        pending.append((ss, rs, srcs, lands))
    mods[0][0] = mods[0][0] + after[0, 0]

    def fetch(gi, behind):
        ss, rs, srcs, lands = pending[gi]
        srcs, lands = split_copy_wait(ss, rs, srcs, lands, _gather_copies, behind, f"gather_wait_{gi}")
        return forward_halves(lands, f"gather_pass_{gi}")

    def as_win(g):
        return _win_to_kernel_layout(g.transpose(1, 0, 2).reshape(d, IN_W))

    wfull = [None] * nl
    row = lambda a: a.reshape(1, -1)
    pad16 = lambda a: jnp.pad(a.reshape(1, -1), ((0, 0), (0, LANES - SSM_HEADS)))
    tm_res = min(1024, s)

    def residual(acc, xt, gt):
        return acc, xt + gt * acc

    def res_extras(xin, gate, tm=tm_res):
        return [(xin.reshape(t, d), pl.BlockSpec((tm, 512), lambda i, j, kk: (i, j))),
                (gate, pl.BlockSpec((None, 1, 512), lambda i, j, kk: (i * tm // s, 0, j)))]

    w1_blk = lambda tk, tn: pl.BlockSpec((None, tk, tn), lambda i, j, kk: (j // (dff4 // tn), kk, j % (dff4 // tn)))

    saved = []
    xcur = x
    for l in range(nl):
        sh1, sc1, gt1, sh2, sc2, gt2 = mods[l]
        if l == 0:
            win = as_win(fetch(0, mod)[0])
        else:
            g_in, g_out, w1, g_2 = fetch(3 + l, xcur)
            win, wout, w2 = as_win(g_in), g_out.reshape(-1, d), g_2.reshape(dff, d)
        prm_a = (row(gm_ln_g[l]), row(gm_ln_b[l]), gm_ws[l], gm_bs[l].T, row(gm_norm_g[l]))
        prm_b = (row(attn_sinks[l]), row(attn_norm_g[l]))
        prm_c = (pad16(dt_bias[l]), pad16(a_log[l]), pad16(d_skip[l]), row(ssm_norm_g[l]))
        h1 = ln_mod_fwd(xcur, row(norm1_g[l]), sc1, sh1, f"ln1_fwd_{l}")
        p = _mm("nn", h1.reshape(t, d), win, dims=(t, PW, d), tm=1024, tn=512, tk=d, out_dtypes=[F32],
                name=f"proj_in_{l}")[0].reshape(bl, s, PW)
        out_a = gmlp_fwd(p, prm_a, f"gmlp_fwd_{l}")
        out_b = attn_fwd(p, *prm_b, f"attn_fwd_{l}")
        xc = conv_fwd(p, conv_w_full[l], row(conv_b[l]), f"conv_fwd_{l}")
        out_c, states = ssd_fwd(xc, p, prm_c, f"ssd_fwd_{l}")
        mix = jnp.concatenate([out_a, out_b, out_c], axis=-1)
        if l == 0:
            wout = fetch(1, mix)[0].reshape(-1, d)
        mm1, x2 = _mm("nn", mix.reshape(t, d), wout, dims=(t, d, d), tm=tm_res, tn=512, tk=d, out_dtypes=[F32, F32],
                      name=f"proj_out_{l}", extras=res_extras(xcur, gt1), epi=residual)
        x2 = x2.reshape(bl, s, d)
        h2 = ln_mod_fwd(x2, row(norm2_g[l]), sc2, sh2, f"ln2_fwd_{l}")
        if l == 0:
            w1 = fetch(2, h2)[0]
        a1 = _mm("nn", h2.reshape(t, d), w1, dims=(t, dff, d), tm=1024, tn=512, tk=d, out_dtypes=[BF16],
                 name=f"mlp1_{l}", b_spec=w1_blk(d, 512))[0]
        if l == 0:
            w2 = fetch(3, a1)[0].reshape(dff, d)
        tm2 = min(512, s)
        mm2, x3 = _mm("nn", a1, w2, dims=(t, d, dff), tm=tm2, tn=512, tk=dff, out_dtypes=[F32, F32],
                      name=f"mlp2_{l}", extras=res_extras(x2, gt2, tm2), epi=residual, pro_a=_relu2)
        x3 = x3.reshape(bl, s, d)
        wfull[l] = (win, wout, w1, w2)
        saved.append((xcur, h1, p, xc, states, mix, mm1.reshape(bl, s, d), x2, h2, a1, mm2.reshape(bl, s, d),
                      prm_a, prm_b, prm_c))
        xcur = x3

    dx, d_final_g, loss_part = loss_head(xcur, row(final_norm_g), loss_target, "loss_head")
    loss = lax.psum(loss_part[0, 0], ("x", "y", "c"))

    def rs_swap(grads, tag):
        ss, rs, srcs, lands, token = split_copy_start(
            grads, [jax.ShapeDtypeStruct((4, g.shape[1] // 2, g.shape[2]), F32) for g in grads],
            _swap_copies, grads[0], f"rs_swap_{tag}")
        return (ss, rs, srcs, lands), token

    def rs_begin(swap_state, tag, swapped_behind, start_behind=None):
        ss, rs, srcs, lands = swap_state
        grads, theirs = split_copy_wait(ss, rs, srcs, lands, _swap_copies, swapped_behind, f"rs_swapped_{tag}")
        sums = [add_pair(g, th, ci, f"rs_add_{tag}_{i}") for i, (g, th) in enumerate(zip(grads, theirs))]
        ss, rs, srcs, lands, token = split_copy_start(
            [sm[1] for sm in sums], [jax.ShapeDtypeStruct((3,) + sm[1].shape[1:], BF16) for sm in sums],
            _scatter_copies, sums[0][0] if start_behind is None else start_behind, f"rs_start_{tag}")
        return (ss, rs, srcs, lands, [sm[0] for sm in sums]), token

    def rs_end(state, behind, tag):
        ss, rs, srcs, lands, sums_f32 = state
        _, got = split_copy_wait(ss, rs, srcs, lands, _scatter_copies, behind, f"rs_wait_{tag}")
        halves = [sum_own_recv(sf, g, chip, ci, f"rs_sum_{tag}_{i}") for i, (sf, g) in enumerate(zip(sums_f32, got))]
        return join_halves(halves, f"rs_join_{tag}")

    small_parts = [None] * nl
    dmods = [None] * nl
    reduced = [[None] * 4 for _ in range(nl)]
    pending_rs, rs_token = [], None
    part_slots = {"a": (0, 1), "m": (2, 3)}

    def finish(behind):
        for ll, part, state in pending_rs:
            for slot, blk in zip(part_slots[part], rs_end(state, behind, f"{ll}{part}")):
                reduced[ll][slot] = blk
        pending_rs.clear()

    for l in reversed(range(nl)):
        sh1, sc1, gt1, sh2, sc2, gt2 = mods[l]
        win, wout, w1, w2 = wfull[l]
        xin, h1, p, xc, states, mix, mm1, x2, h2, a1, mm2, prm_a, prm_b, prm_c = saved[l]
        if rs_token is not None:
            gt2 = gt2 + rs_token[0, 0]
        dm2, dgt2 = gate_bwd(dx, mm2, gt2, f"gate2_bwd_{l}")
        dm2 = dm2.reshape(t, d)
        da1 = _mm("nt", dm2, w2, dims=(t, dff, d), tm=1024, tn=512, tk=d, out_dtypes=[BF16], name=f"mlp2_dx_{l}",
                  extras=[(a1, pl.BlockSpec((1024 if t >= 1024 else t, 512), lambda i, j, kk: (i, j)))],
                  epi=lambda acc, av: (acc * (2.0 * jnp.maximum(av, 0).astype(F32)),))[0]
        dw2 = _mm("tn", a1, dm2, dims=(dff, d, t), tm=512, tn=d, tk=2048, out_dtypes=[F32], name=f"mlp2_dw_{l}",
                  pro_a=_relu2, out_shapes=[(4, dff4, d)],
                  out_specs=[pl.BlockSpec((None, 512, d), lambda i, j, kk: (i // (dff4 // 512), i % (dff4 // 512), 0))])[0]
        dw1 = _mm("tn", h2.reshape(t, d), da1, dims=(d, dff, t), tm=512, tn=dff4, tk=2048, out_dtypes=[F32],
                  name=f"mlp1_dw_{l}", out_shapes=[(4, d, dff4)],
                  out_specs=[pl.BlockSpec((None, 512, dff4), lambda i, j, kk: (j, i, 0))])[0]
        swap_state, swap_token = rs_swap([dw1, dw2], f"{l}m")
        dh2 = _mm("nt", da1, w1, dims=(t, d, dff), tm=1024, tn=512, tk=2048, out_dtypes=[F32], name=f"mlp1_dx_{l}",
                  b_spec=pl.BlockSpec((None, 512, 2048 if dff4 >= 2048 else dff4),
                                      lambda i, j, kk: (kk // (dff4 // min(2048, dff4)), j, kk % (dff4 // min(2048, dff4)))),
                  behind=swap_token)[0]
        mlp_state, mlp_token = rs_begin(swap_state, f"{l}m", dh2)
        sc2 = sc2 + mlp_token[0, 0]
        dx2, dsc2, dsh2, dn2 = ln_mod_bwd(dh2.reshape(bl, s, d), x2, dx, row(norm2_g[l]), sc2, f"ln2_bwd_{l}")
        dm1, dgt1 = gate_bwd(dx2, mm1, gt1, f"gate1_bwd_{l}")
        dm1 = dm1.reshape(t, d)
        dmix = _mm("nt", dm1, wout, dims=(t, d, d), tm=1024, tn=512, tk=d, out_dtypes=[F32],
                   name=f"proj_out_dx_{l}")[0].reshape(bl, s, d)
        dwout = _mm("tn", mix.reshape(t, d), dm1, dims=(d, d, t), tm=512, tn=d, tk=2048, out_dtypes=[F32],
                    name=f"proj_out_dw_{l}", out_shapes=[(4, d // 4, d)],
                    out_specs=[pl.BlockSpec((None, 512, d), lambda i, j, kk: (i // (d // 4 // 512), i % (d // 4 // 512), 0))])[0]
        du, dv, dlg, dlb, dws, dbst, dgng = gmlp_bwd(p, dmix, prm_a, f"gmlp_bwd_{l}")
        dq, dk, dvv, dsinks, dang = attn_bwd(p, dmix, *prm_b, f"attn_bwd_{l}")
        dxc, ddt, dz, ddtb, dalog, ddsk, dsng = ssd_bwd(xc, p, states, dmix, prm_c, f"ssd_bwd_{l}")
        dxbc, dcw, dcb = conv_bwd(p, dxc, conv_w_full[l], row(conv_b[l]), f"conv_bwd_{l}")
        dp = jnp.concatenate([dxbc, dq, dz, du, dv, dk, dvv, ddt, jnp.zeros((bl, s, PW - OFF_DT - LANES), BF16)],
                             axis=-1).reshape(t, PW)
        dwin = _mm("tn", h1.reshape(t, d), dp, dims=(d, PW, t), tm=512, tn=PW // 3, tk=2048, out_dtypes=[F32],
                   name=f"proj_in_dw_{l}")[0]
        dwin_blocks = _win_from_kernel_layout(dwin).reshape(d, 4, IN_W // 4).transpose(1, 0, 2)
        mixer_swap, swap_token = rs_swap([dwin_blocks, dwout], f"{l}a")
        dh1 = _mm("nt", dp, win, dims=(t, d, PW), tm=1024, tn=512, tk=PW, out_dtypes=[F32],
                  name=f"proj_in_dx_{l}", behind=swap_token)[0]
        dx, dsc1, dsh1, dn1 = ln_mod_bwd(dh1.reshape(bl, s, d), xin, dx2, row(norm1_g[l]), sc1, f"ln1_bwd_{l}")
        dmods[l] = jnp.concatenate([dsh1, dsc1, dgt1, dsh2, dsc2, dgt2], axis=-1).reshape(bl, 6 * d)
        small_parts[l] = [dn1, dlg, dlb, dws, dbst.T, dgng, dsinks, dang, dcw, dcb, ddtb[:, :SSM_HEADS],
                          dalog[:, :SSM_HEADS], ddsk[:, :SSM_HEADS], dsng, dn2]
        finish(dx)
        pending_rs.append((l, "m", mlp_state))
        if l > 0:
            state, rs_token = rs_begin(mixer_swap, f"{l}a", dx)
            pending_rs.append((l, "a", state))
    grad_x = dx

    big = [(w_in, m_w_in, v_w_in), (w_out, m_w_out, v_w_out), (w_mlp1, m_w_mlp1, v_w_mlp1), (w_mlp2, m_w_mlp2, v_w_mlp2)]
    big_out = [None] * 4
    for l in reversed(range(1, nl)):
        for i, (wt, mt, vt) in enumerate(big):
            big_out[i] = adamw_layer(wt, mt, vt, reduced[l][i], l, big_out[i], f"adamw_big_{i}_{l}")

    small_names = [norm1_g, gm_ln_g, gm_ln_b, gm_ws, gm_bs, gm_norm_g, attn_sinks, attn_norm_g, None, conv_b, dt_bias,
                   a_log, d_skip, ssm_norm_g, norm2_g]
    n_small = len(small_names)
    per_param = [jnp.stack([small_parts[l][i].reshape(-1) for l in range(nl)]) for i in range(n_small)]
    small_vec = _pack(per_param + [d_final_g])
    rs_small = small_vec.shape[0]
    dmod_local = jnp.stack(dmods, axis=1)
    g2 = all_gather_small(jnp.concatenate([small_vec, _pack([dmod_local])], axis=0), "ag_small")
    state, rs_token = rs_begin(mixer_swap, "0a", grad_x, start_behind=g2)
    pending_rs.append((0, "a", state))
    g2 = g2 + rs_token[0, 0]
    g_small = sum_devices(g2[:, :rs_small, :], "sum_small")
    dmod_all = g2[:, rs_small:, :].reshape(8, -1)[:, :bl * nl * 6 * d].reshape(nex, nl * 6 * d)
    g_ada_b = sum_devices(dmod_all.reshape(nex, -1, LANES), "sum_ada_b").reshape(nl, 6 * d)
    shapes = [(nl, int(np.prod(small_parts[0][i].shape))) for i in range(n_small)] + [(d,)]
    g_list = _unpack(g_small, shapes)
    g_conv_w = lax.dynamic_slice(g_list[8].reshape(nl, CONV_K, CCH), (0, 0, chip * cw_w), (nl, CONV_K, cw_w))

    dm_cols = lax.dynamic_slice(dmod_all.reshape(nex, nl, 6 * d), (0, 0, chip * mod_w), (nex, nl, mod_w))
    g_ada_w = _mm("tn", c_all, dm_cols.reshape(nex, nl * mod_w), dims=(d, nl * mod_w, nex), tm=512, tn=512, tk=nex,
                  out_dtypes=[F32], name="ada_w_grad", pro_a=c_act, pro_b=to_bf16, out_shapes=[(nl, d, mod_w)],
                  out_specs=[pl.BlockSpec((None, 512, 512), lambda i, j, kk: (j // (mod_w // 512), i, j % (mod_w // 512)))])[0]
    d_ada_w, m_ada_w_n, v_ada_w_n = [a.reshape(ada_w.shape) for a in
                                     adamw(_rows2d(ada_w), _rows2d(m_ada_w), _rows2d(v_ada_w), _rows2d(g_ada_w), "adamw_ada_w")]

    smalls = {
        "ada_b": (ada_b, m_ada_b, v_ada_b, g_ada_b), "norm1_g": (norm1_g, m_norm1_g, v_norm1_g, g_list[0]),
        "gm_ln_g": (gm_ln_g, m_gm_ln_g, v_gm_ln_g, g_list[1]), "gm_ln_b": (gm_ln_b, m_gm_ln_b, v_gm_ln_b, g_list[2]),
        "gm_ws": (gm_ws, m_gm_ws, v_gm_ws, g_list[3]), "gm_bs": (gm_bs, m_gm_bs, v_gm_bs, g_list[4]),
        "gm_norm_g": (gm_norm_g, m_gm_norm_g, v_gm_norm_g, g_list[5]),
        "attn_sinks": (attn_sinks, m_attn_sinks, v_attn_sinks, g_list[6]),
        "attn_norm_g": (attn_norm_g, m_attn_norm_g, v_attn_norm_g, g_list[7]),
        "conv_w": (conv_w, m_conv_w, v_conv_w, g_conv_w), "conv_b": (conv_b, m_conv_b, v_conv_b, g_list[9]),
        "dt_bias": (dt_bias, m_dt_bias, v_dt_bias, g_list[10]), "a_log": (a_log, m_a_log, v_a_log, g_list[11]),
        "d_skip": (d_skip, m_d_skip, v_d_skip, g_list[12]),
        "ssm_norm_g": (ssm_norm_g, m_ssm_norm_g, v_ssm_norm_g, g_list[13]),
        "norm2_g": (norm2_g, m_norm2_g, v_norm2_g, g_list[14]),
        "final_norm_g": (final_norm_g, m_final_norm_g, v_final_norm_g, g_list[15]),
    }
    keys = list(smalls)
    wv, mv, vv_, gv = [_pack([smalls[k][i].reshape(smalls[k][0].shape) for k in keys]) for i in range(4)]
    sd_, sm_, sv_ = adamw(wv, mv, vv_, gv, "adamw_small")
    shp = [smalls[k][0].shape for k in keys]
    small_out = {k: (smalls[k][3].reshape(smalls[k][0].shape), a, b, cc)
                 for k, a, b, cc in zip(keys, _unpack(sd_, shp), _unpack(sm_, shp), _unpack(sv_, shp))}

    late = jnp.zeros((8, LANES), F32) + (sv_[0, 0] + v_ada_w_n[0, 0, 0])
    for bo in big_out:
        if bo is not None:
            late = late + bo[3][nl - 1, 0, 0]
    finish(late)
    for i, (wt, mt, vt) in enumerate(big):
        big_out[i] = adamw_layer(wt, mt, vt, reduced[0][i], 0, big_out[i], f"adamw_big_{i}_0")

    out = {"ada_w": (g_ada_w, d_ada_w, m_ada_w_n, v_ada_w_n), "w_in": big_out[0], "w_out": big_out[1],
           "w_mlp1": big_out[2], "w_mlp2": big_out[3], **small_out}
    order = ["ada_w", "ada_b", "norm1_g", "w_in", "gm_ln_g", "gm_ln_b", "gm_ws", "gm_bs", "gm_norm_g", "attn_sinks",
             "attn_norm_g", "conv_w", "conv_b", "dt_bias", "a_log", "d_skip", "ssm_norm_g", "w_out", "norm2_g",
             "w_mlp1", "w_mlp2", "final_norm_g"]
    return (loss, grad_x, *[out[k][0] for k in order], *[out[k][1] for k in order],
            *[out[k][2] for k in order], *[out[k][3] for k in order])
```

```python
import functools
import math

import jax
import jax.numpy as jnp
import numpy as np
from jax import lax
from jax.experimental import pallas as pl
from jax.experimental.pallas import tpu as pltpu

F32 = jnp.float32
BF16 = jnp.bfloat16
HI = lax.Precision.HIGHEST
MESH = pl.DeviceIdType.MESH

CHUNK = 128
GM_HEADS, GM_HD = 4, 128
ATT_HEADS, ATT_KV, ATT_HD = 8, 2, 64
WINDOW = 128
SSM_HEADS, SSM_HD, SSM_GROUPS, SSM_STATE, CONV_K = 16, 64, 2, 128, 4
EPS = 1e-6
LN_EPS = 1e-5
NEG = -1e30
LANES = 128

GMW = GM_HEADS * GM_HD
ATW = ATT_HEADS * ATT_HD
KVW = ATT_KV * ATT_HD
SSW = SSM_HEADS * SSM_HD
BCW = SSM_GROUPS * SSM_STATE
CCH = SSW + 2 * BCW
GRW = SSW // SSM_GROUPS
IN_SIZES = (GMW, GMW, ATW, KVW, KVW, SSW, CCH, SSM_HEADS)
IN_W = sum(IN_SIZES)
OFF_XBC, OFF_Q, OFF_Z, OFF_U, OFF_V, OFF_K, OFF_VV, OFF_DT = 0, 1536, 2048, 3072, 3584, 4096, 4224, 4352
PW = 4608

ADAM_LR, ADAM_B1, ADAM_B2, ADAM_EPS, ADAM_WD, ADAM_STEP = 0.001, 0.9, 0.999, 1e-08, 0.01, 10

VMEM_LIMIT = 56 * 1024 * 1024


def _cp(sem=None):
    return pltpu.CompilerParams(dimension_semantics=sem, vmem_limit_bytes=VMEM_LIMIT)


_DN = {"nn": (((1,), (0,)), ((), ())), "nt": (((1,), (1,)), ((), ())), "tn": (((0,), (0,)), ((), ()))}


def _dot(form, a, b):
    return lax.dot_general(a.astype(BF16), b.astype(BF16), _DN[form], preferred_element_type=F32)


@jax.custom_vjp
def _nn(a, b):
    return _dot("nn", a, b)


@jax.custom_vjp
def _nt(a, b):
    return _dot("nt", a, b)


@jax.custom_vjp
def _tn(a, b):
    return _dot("tn", a, b)


_nn.defvjp(lambda a, b: (_dot("nn", a, b), (a, b)), lambda r, g: (_dot("nt", g, r[1]), _dot("tn", r[0], g)))
_nt.defvjp(lambda a, b: (_dot("nt", a, b), (a, b)), lambda r, g: (_dot("nn", g, r[1]), _dot("tn", g, r[0])))
_tn.defvjp(lambda a, b: (_dot("tn", a, b), (a, b)), lambda r, g: (_dot("nt", r[1], g), _dot("nn", r[0], g)))


def _hdot(a, b):
    return jnp.dot(a, b, precision=HI, preferred_element_type=F32)


def _silu(x):
    return x * (1.0 / (1.0 + jnp.exp(-x)))


def _softplus(x):
    return jnp.maximum(x, 0.0) + jnp.log1p(jnp.exp(-jnp.abs(x)))


def _gelu(x):
    return 0.5 * x * (1.0 + jnp.tanh(math.sqrt(2.0 / math.pi) * (x + 0.044715 * (x * x * x))))


def _rms(y, g):
    return y * lax.rsqrt(jnp.mean(y * y, axis=-1, keepdims=True) + EPS) * g


def _mm(form, a, b, *, dims, tm, tn, tk, out_dtypes, name, a_spec=None, b_spec=None, out_specs=None,
        out_shapes=None, extras=(), epi=None, pro_a=None, pro_b=None, behind=None):
    m, n, k = dims
    tm, tn, tk = min(tm, m), min(tn, n), min(tk, k)
    assert m % tm == 0 and n % tn == 0 and k % tk == 0, (name, dims, tm, tn, tk)
    nk = k // tk
    if a_spec is None:
        a_spec = (pl.BlockSpec((tk, tm), lambda i, j, kk: (kk, i)) if form == "tn"
                  else pl.BlockSpec((tm, tk), lambda i, j, kk: (i, kk)))
    if b_spec is None:
        b_spec = (pl.BlockSpec((tn, tk), lambda i, j, kk: (j, kk)) if form == "nt"
                  else pl.BlockSpec((tk, tn), lambda i, j, kk: (kk, j)))
    n_out = len(out_dtypes)
    if out_specs is None:
        out_specs = [pl.BlockSpec((tm, tn), lambda i, j, kk: (i, j))] * n_out
    if out_shapes is None:
        out_shapes = [(m, n)] * n_out
    ne = len(extras)
    n_behind = 0 if behind is None else 1

    def body(*refs):
        a_ref, b_ref = refs[0], refs[1]
        ex = refs[2:2 + ne]
        outs = refs[2 + ne + n_behind:2 + ne + n_behind + n_out]

        def write(val):
            res = epi(val, *[e[...] for e in ex]) if epi is not None else (val,)
            for o, r in zip(outs, res):
                o[...] = r.astype(o.dtype)

        av = a_ref[...]
        if pro_a is not None:
            av = pro_a(av)
        bv = b_ref[...]
        if pro_b is not None:
            bv = pro_b(bv)
        part = lax.dot_general(av, bv, _DN[form], preferred_element_type=F32)
        if nk == 1:
            write(part)
        else:
            acc = refs[-1]
            kk = pl.program_id(2)

            @pl.when(kk == 0)
            def _():
                acc[...] = part

            @pl.when(kk > 0)
            def _():
                acc[...] += part

            @pl.when(kk == nk - 1)
            def _():
                write(acc[...])

    res = pl.pallas_call(
        body, name=name, grid=(m // tm, n // tn, nk),
        in_specs=[a_spec, b_spec] + [s for _, s in extras] + [_ANY] * n_behind,
        out_specs=out_specs,
        out_shape=[jax.ShapeDtypeStruct(s, d) for s, d in zip(out_shapes, out_dtypes)],
        scratch_shapes=[pltpu.VMEM((tm, tn), F32)] if nk > 1 else [],
        compiler_params=_cp(("parallel", "parallel", "arbitrary")),
    )(a, b, *[e for e, _ in extras], *([behind] if n_behind else []))
    return res


def _mm_nt_blocked(a, b, *, tm, tn, name, behind=None):
    m = a.shape[0]
    nparts, n, f = b.shape
    tm, tn = min(tm, m), min(tn, n)
    n_behind = 0 if behind is None else 1

    def body(a_ref, *rest):
        b_refs, o_ref = rest[:nparts], rest[nparts + n_behind]
        acc = None
        for k in range(nparts):
            part = lax.dot_general(a_ref[:, k * f:(k + 1) * f], b_refs[k][...], _DN["nt"], preferred_element_type=F32)
            acc = part if acc is None else acc + part
        o_ref[...] = acc

    return pl.pallas_call(
        body, name=name, grid=(m // tm, n // tn),
        in_specs=[pl.BlockSpec((tm, nparts * f), lambda i, j: (i, 0))]
        + [pl.BlockSpec((None, tn, f), lambda i, j, k=k: (k, j, 0)) for k in range(nparts)] + [_ANY] * n_behind,
        out_specs=pl.BlockSpec((tm, tn), lambda i, j: (i, j)),
        out_shape=jax.ShapeDtypeStruct((m, n), F32),
        compiler_params=_cp(("parallel", "parallel")),
    )(a, *([b] * nparts), *([behind] if n_behind else []))


def _row_tile(s):
    return min(512, s)


def ln_mod_fwd(x, g, sc, sh, name):
    bsz, s, d = x.shape
    ts = _row_tile(s)

    def body(x_ref, g_ref, sc_ref, sh_ref, o_ref):
        xv = x_ref[...]
        r = lax.rsqrt(jnp.mean(xv * xv, axis=-1, keepdims=True) + EPS)
        o_ref[...] = ((xv * r * g_ref[...]) * (1.0 + sc_ref[...]) + sh_ref[...]).astype(o_ref.dtype)

    row = pl.BlockSpec((None, ts, d), lambda b, i: (b, i, 0))
    vec = pl.BlockSpec((None, 1, d), lambda b, i: (b, 0, 0))
    return pl.pallas_call(
        body, name=name, grid=(bsz, s // ts),
        in_specs=[row, pl.BlockSpec((1, d), lambda b, i: (0, 0)), vec, vec],
        out_specs=row, out_shape=jax.ShapeDtypeStruct(x.shape, BF16),
        compiler_params=_cp(("parallel", "parallel")),
    )(x, g, sc, sh)


def ln_mod_bwd(dh, x, dres, g, sc, name):
    bsz, s, d = x.shape
    ts = _row_tile(s)

    def body(dh_ref, x_ref, dres_ref, g_ref, sc_ref, dx_ref, dsc_ref, dsh_ref, dg_ref):
        b, i = pl.program_id(0), pl.program_id(1)
        xv, dhv, gv = x_ref[...], dh_ref[...], g_ref[...]
        r = lax.rsqrt(jnp.mean(xv * xv, axis=-1, keepdims=True) + EPS)
        xn = xv * r
        a = dhv * (1.0 + sc_ref[...])
        dxn = a * gv
        dx_ref[...] = dres_ref[...] + r * (dxn - xn * jnp.mean(dxn * xn, axis=-1, keepdims=True))
        p_sc = jnp.sum(dhv * (xn * gv), axis=0, keepdims=True)
        p_sh = jnp.sum(dhv, axis=0, keepdims=True)
        p_g = jnp.sum(a * xn, axis=0, keepdims=True)

        @pl.when(i == 0)
        def _():
            dsc_ref[...] = p_sc
            dsh_ref[...] = p_sh

        @pl.when(i > 0)
        def _():
            dsc_ref[...] += p_sc
            dsh_ref[...] += p_sh

        @pl.when((i == 0) & (b == 0))
        def _():
            dg_ref[...] = p_g

        @pl.when((i > 0) | (b > 0))
        def _():
            dg_ref[...] += p_g

    row = pl.BlockSpec((None, ts, d), lambda b, i: (b, i, 0))
    vec = pl.BlockSpec((None, 1, d), lambda b, i: (b, 0, 0))
    one = pl.BlockSpec((1, d), lambda b, i: (0, 0))
    return pl.pallas_call(
        body, name=name, grid=(bsz, s // ts),
        in_specs=[row, row, row, one, vec],
        out_specs=[row, vec, vec, one],
        out_shape=[jax.ShapeDtypeStruct(x.shape, F32), jax.ShapeDtypeStruct((bsz, 1, d), F32),
                   jax.ShapeDtypeStruct((bsz, 1, d), F32), jax.ShapeDtypeStruct((1, d), F32)],
        compiler_params=_cp(("arbitrary", "arbitrary")),
    )(dh, x, dres, g, sc)


def gate_bwd(dx, mm, gate, name):
    bsz, s, d = dx.shape
    ts = _row_tile(s)

    def body(dx_ref, m_ref, g_ref, dm_ref, dg_ref):
        i = pl.program_id(1)
        dxv = dx_ref[...]
        dm_ref[...] = (dxv * g_ref[...]).astype(dm_ref.dtype)
        p = jnp.sum(dxv * m_ref[...], axis=0, keepdims=True)

        @pl.when(i == 0)
        def _():
            dg_ref[...] = p

        @pl.when(i > 0)
        def _():
            dg_ref[...] += p

    row = pl.BlockSpec((None, ts, d), lambda b, i: (b, i, 0))
    vec = pl.BlockSpec((None, 1, d), lambda b, i: (b, 0, 0))
    return pl.pallas_call(
        body, name=name, grid=(bsz, s // ts),
        in_specs=[row, row, vec], out_specs=[row, vec],
        out_shape=[jax.ShapeDtypeStruct(dx.shape, BF16), jax.ShapeDtypeStruct((bsz, 1, d), F32)],
        compiler_params=_cp(("parallel", "arbitrary")),
    )(dx, mm, gate)


def loss_head(x, g, tgt, name):
    bsz, s, d = x.shape
    ts = _row_tile(s)

    def body(x_ref, g_ref, t_ref, dx_ref, dg_ref, l_ref):
        b, i = pl.program_id(0), pl.program_id(1)
        xv, gv = x_ref[...], g_ref[...]
        r = lax.rsqrt(jnp.mean(xv * xv, axis=-1, keepdims=True) + EPS)
        xn = xv * r
        e = xn * gv - t_ref[...]
        dy = e * (1.0 / d)
        dxn = dy * gv
        dx_ref[...] = r * (dxn - xn * jnp.mean(dxn * xn, axis=-1, keepdims=True))
        p_g = jnp.sum(dy * xn, axis=0, keepdims=True)
        p_l = jnp.zeros((1, LANES), F32) + jnp.sum(e * e) * (0.5 / d)
        first = (i == 0) & (b == 0)

        @pl.when(first)
        def _():
            dg_ref[...] = p_g
            l_ref[...] = p_l

        @pl.when(jnp.logical_not(first))
        def _():
            dg_ref[...] += p_g
            l_ref[...] += p_l

    row = pl.BlockSpec((None, ts, d), lambda b, i: (b, i, 0))
    one = pl.BlockSpec((1, d), lambda b, i: (0, 0))
    return pl.pallas_call(
        body, name=name, grid=(bsz, s // ts),
        in_specs=[row, one, row],
        out_specs=[row, one, pl.BlockSpec((1, LANES), lambda b, i: (0, 0))],
        out_shape=[jax.ShapeDtypeStruct(x.shape, F32), jax.ShapeDtypeStruct((1, d), F32),
                   jax.ShapeDtypeStruct((1, LANES), F32)],
        compiler_params=_cp(("arbitrary", "arbitrary")),
    )(x, g, tgt)


def _gmlp_chunk(u_raw, v_raw, ln_g, ln_b, w, bs_t, out_g):
    c = u_raw.shape[0]
    u, v = _gelu(u_raw), _gelu(v_raw)
    tril = lax.broadcasted_iota(jnp.int32, (c, c), 0) >= lax.broadcasted_iota(jnp.int32, (c, c), 1)
    ys = []
    for h in range(GM_HEADS):
        sl = slice(h * GM_HD, (h + 1) * GM_HD)
        vh = v[:, sl]
        xc = vh - jnp.mean(vh, axis=-1, keepdims=True)
        vn = xc * lax.rsqrt(jnp.mean(xc * xc, axis=-1, keepdims=True) + LN_EPS) * ln_g[:, sl] + ln_b[:, sl]
        gate = _nn(jnp.where(tril, w[h], 0.0), vn) + bs_t[:, h:h + 1]
        ys.append(u[:, sl] * gate)
    return _rms(jnp.concatenate(ys, axis=1), out_g)


def _gmlp_specs(bsz, nc):
    seg = lambda off: pl.BlockSpec((None, CHUNK, GMW), lambda b, c: (b, c, off // GMW))
    full = lambda shape: pl.BlockSpec(shape, lambda b, c: (0,) * len(shape))
    par = [full((1, GMW)), full((1, GMW)), full((GM_HEADS, CHUNK, CHUNK)), full((CHUNK, GM_HEADS)), full((1, GMW))]
    return seg, full, par


def gmlp_fwd(p, prm, name):
    bsz, s, _ = p.shape
    nc = s // CHUNK
    seg, _, par = _gmlp_specs(bsz, nc)

    def body(u_ref, v_ref, lg, lb, w, bt, og, o_ref):
        o_ref[...] = _gmlp_chunk(u_ref[...], v_ref[...], lg[...], lb[...], w[...], bt[...], og[...]).astype(o_ref.dtype)

    return pl.pallas_call(
        body, name=name, grid=(bsz, nc),
        in_specs=[seg(OFF_U), seg(OFF_V)] + par,
        out_specs=pl.BlockSpec((None, CHUNK, GMW), lambda b, c: (b, c, 0)),
        out_shape=jax.ShapeDtypeStruct((bsz, s, GMW), BF16),
        compiler_params=_cp(("parallel", "parallel")),
    )(p, p, *prm)


def _accumulate(first, refs, vals):
    @pl.when(first)
    def _():
        for r, v in zip(refs, vals):
            r[...] = v

    @pl.when(jnp.logical_not(first))
    def _():
        for r, v in zip(refs, vals):
            r[...] += v


def gmlp_bwd(p, dmix, prm, name):
    bsz, s, _ = p.shape
    nc = s // CHUNK
    seg, full, par = _gmlp_specs(bsz, nc)

    def body(u_ref, v_ref, do_ref, lg, lb, w, bt, og, du_ref, dv_ref, *dpar):
        first = (pl.program_id(0) == 0) & (pl.program_id(1) == 0)
        _, vjp = jax.vjp(_gmlp_chunk, u_ref[...], v_ref[...], lg[...], lb[...], w[...], bt[...], og[...])
        gr = vjp(do_ref[...])
        du_ref[...] = gr[0].astype(du_ref.dtype)
        dv_ref[...] = gr[1].astype(dv_ref.dtype)
        _accumulate(first, dpar, gr[2:])

    out_seg = pl.BlockSpec((None, CHUNK, GMW), lambda b, c: (b, c, 0))
    return pl.pallas_call(
        body, name=name, grid=(bsz, nc),
        in_specs=[seg(OFF_U), seg(OFF_V), out_seg] + par,
        out_specs=[out_seg, out_seg] + par,
        out_shape=[jax.ShapeDtypeStruct((bsz, s, GMW), BF16)] * 2 + [jax.ShapeDtypeStruct(x.shape, F32) for x in prm],
        compiler_params=_cp(("arbitrary", "arbitrary")),
    )(p, p, dmix, *prm)


def _attn_block(q, kp, kc, vp, vc, sinks, out_g, has_prev):
    w = q.shape[0]
    k2 = jnp.concatenate([kp, kc], axis=0)
    v2 = jnp.concatenate([vp, vc], axis=0)
    qi = lax.broadcasted_iota(jnp.int32, (w, 2 * w), 0)
    kj = lax.broadcasted_iota(jnp.int32, (w, 2 * w), 1)
    diff = qi + w - kj
    grp = ATT_HEADS // ATT_KV
    valid = (diff >= 0) & (diff < w) & ((kj >= w) | has_prev)
    valid = jnp.concatenate([valid] * grp, axis=0)
    outs = []
    for kv in range(ATT_KV):
        kh = k2[:, kv * ATT_HD:(kv + 1) * ATT_HD]
        vh = v2[:, kv * ATT_HD:(kv + 1) * ATT_HD]
        heads = range(kv * grp, (kv + 1) * grp)
        qs = jnp.concatenate([q[:, h * ATT_HD:(h + 1) * ATT_HD] for h in heads], axis=0)
        sink = jnp.concatenate([jnp.broadcast_to(sinks[:, h:h + 1], (w, 1)) for h in heads], axis=0)
        sc = jnp.where(valid, _nt(qs, kh) * (ATT_HD ** -0.5), NEG)
        m = jnp.maximum(jnp.max(sc, axis=-1, keepdims=True), sink)
        e = jnp.exp(sc - m)
        pr = e / (jnp.sum(e, axis=-1, keepdims=True) + jnp.exp(sink - m))
        o = _nn(pr, vh)
        outs += [o[gi * w:(gi + 1) * w] for gi in range(grp)]
    return _rms(jnp.concatenate(outs, axis=1), out_g)


ATT_QB = 4


def _attn_tiles(s):
    qb = min(ATT_QB, s // WINDOW)
    return qb, qb * WINDOW, s // (qb * WINDOW)


def attn_fwd(p, sinks, out_g, name):
    bsz, s, _ = p.shape
    qb, rows, steps = _attn_tiles(s)

    def body(q_ref, kp_ref, kc_ref, vp_ref, vc_ref, s_ref, g_ref, o_ref):
        n = pl.program_id(1)
        for w in range(qb):
            sl = pl.ds(w * WINDOW, WINDOW)
            before = pl.ds((w - 1) * WINDOW, WINDOW)
            kp = kp_ref[...] if w == 0 else kc_ref[before, :]
            vp = vp_ref[...] if w == 0 else vc_ref[before, :]
            o_ref[sl, :] = _attn_block(q_ref[sl, :], kp, kc_ref[sl, :], vp, vc_ref[sl, :], s_ref[...], g_ref[...],
                                       (n > 0) if w == 0 else True).astype(o_ref.dtype)

    cur = lambda off: pl.BlockSpec((None, rows, KVW), lambda b, n: (b, n, off // KVW))
    prev = lambda off: pl.BlockSpec((None, WINDOW, KVW), lambda b, n: (b, jnp.maximum(n * qb - 1, 0), off // KVW))
    return pl.pallas_call(
        body, name=name, grid=(bsz, steps),
        in_specs=[pl.BlockSpec((None, rows, ATW), lambda b, n: (b, n, OFF_Q // ATW)),
                  prev(OFF_K), cur(OFF_K), prev(OFF_VV), cur(OFF_VV),
                  pl.BlockSpec((1, ATT_HEADS), lambda b, n: (0, 0)), pl.BlockSpec((1, ATW), lambda b, n: (0, 0))],
        out_specs=pl.BlockSpec((None, rows, ATW), lambda b, n: (b, n, 0)),
        out_shape=jax.ShapeDtypeStruct((bsz, s, ATW), BF16),
        compiler_params=_cp(("parallel", "parallel")),
    )(p, p, p, p, p, sinks, out_g)


def attn_bwd(p, dmix, sinks, out_g, name):
    bsz, s, _ = p.shape
    qb, rows, steps = _attn_tiles(s)
    last = pl.ds(rows - WINDOW, WINDOW)

    def body(q_ref, kp_ref, kc_ref, vp_ref, vc_ref, do_ref, s_ref, g_ref,
             dq_ref, dk_ref, dv_ref, ds_ref, dg_ref, ck, cv):
        b, n = pl.program_id(0), pl.program_id(1)

        @pl.when(n == 0)
        def _():
            ck[...] = jnp.zeros_like(ck)
            cv[...] = jnp.zeros_like(cv)

        @pl.when(n < steps)
        def _():
            grads = []
            for w in range(qb):
                sl = pl.ds(w * WINDOW, WINDOW)
                before = pl.ds((w - 1) * WINDOW, WINDOW)
                kp = kp_ref[...] if w == 0 else kc_ref[before, :]
                vp = vp_ref[...] if w == 0 else vc_ref[before, :]
                fn = functools.partial(_attn_block, has_prev=(n > 0) if w == 0 else True)
                _, vjp = jax.vjp(fn, q_ref[sl, :], kp, kc_ref[sl, :], vp, vc_ref[sl, :], s_ref[...], g_ref[...])
                grads.append(vjp(do_ref[sl, :]))
                dq_ref[sl, :] = grads[-1][0].astype(dq_ref.dtype)
            dk_ref[...] = ck[...].astype(dk_ref.dtype)
            dv_ref[...] = cv[...].astype(dv_ref.dtype)
            dk_ref[last, :] = (ck[last, :] + grads[0][1]).astype(dk_ref.dtype)
            dv_ref[last, :] = (cv[last, :] + grads[0][3]).astype(dv_ref.dtype)
            for w in range(qb):
                sl = pl.ds(w * WINDOW, WINDOW)
                ck[sl, :] = grads[w][2] + (grads[w + 1][1] if w + 1 < qb else 0.0)
                cv[sl, :] = grads[w][4] + (grads[w + 1][3] if w + 1 < qb else 0.0)
            dsk = functools.reduce(lambda u, v: u + v, [g[5] for g in grads])
            dgg = functools.reduce(lambda u, v: u + v, [g[6] for g in grads])
            _accumulate((b == 0) & (n == 0), (ds_ref, dg_ref), (dsk, dgg))

        @pl.when(n == steps)
        def _():
            dk_ref[...] = ck[...].astype(dk_ref.dtype)
            dv_ref[...] = cv[...].astype(dv_ref.dtype)

    at = lambda n: jnp.minimum(n, steps - 1)
    cur = lambda off: pl.BlockSpec((None, rows, KVW), lambda b, n: (b, at(n), off // KVW))
    prev = lambda off: pl.BlockSpec((None, WINDOW, KVW), lambda b, n: (b, jnp.maximum(at(n) * qb - 1, 0), off // KVW))
    kv_out = pl.BlockSpec((None, rows, KVW), lambda b, n: (b, jnp.maximum(n - 1, 0), 0))
    return pl.pallas_call(
        body, name=name, grid=(bsz, steps + 1),
        in_specs=[pl.BlockSpec((None, rows, ATW), lambda b, n: (b, at(n), OFF_Q // ATW)),
                  prev(OFF_K), cur(OFF_K), prev(OFF_VV), cur(OFF_VV),
                  pl.BlockSpec((None, rows, ATW), lambda b, n: (b, at(n), GMW // ATW)),
                  pl.BlockSpec((1, ATT_HEADS), lambda b, n: (0, 0)), pl.BlockSpec((1, ATW), lambda b, n: (0, 0))],
        out_specs=[pl.BlockSpec((None, rows, ATW), lambda b, n: (b, at(n), 0)), kv_out, kv_out,
                   pl.BlockSpec((1, ATT_HEADS), lambda b, n: (0, 0)), pl.BlockSpec((1, ATW), lambda b, n: (0, 0))],
        out_shape=[jax.ShapeDtypeStruct((bsz, s, ATW), BF16), jax.ShapeDtypeStruct((bsz, s, KVW), BF16),
                   jax.ShapeDtypeStruct((bsz, s, KVW), BF16), jax.ShapeDtypeStruct((1, ATT_HEADS), F32),
                   jax.ShapeDtypeStruct((1, ATW), F32)],
        scratch_shapes=[pltpu.VMEM((rows, KVW), F32), pltpu.VMEM((rows, KVW), F32)],
        compiler_params=_cp(("arbitrary", "arbitrary")),
    )(p, p, p, p, p, dmix, sinks, out_g)


CONV_CT = 256


def _shift_down(x, j):
    if j == 0:
        return x
    rows = lax.broadcasted_iota(jnp.int32, x.shape, 0)
    return jnp.where(rows >= j, pltpu.roll(x, j, 0), 0.0)


def _shift_up(x, j):
    if j == 0:
        return x
    s = x.shape[0]
    rows = lax.broadcasted_iota(jnp.int32, x.shape, 0)
    return jnp.where(rows < s - j, pltpu.roll(x, s - j, 0), 0.0)


def conv_fwd(p, w, bias, name):
    bsz, s, _ = p.shape

    def body(x_ref, w_ref, b_ref, o_ref):
        xv, wv = x_ref[...], w_ref[...]
        pre = b_ref[...] + sum(wv[k:k + 1, :] * _shift_down(xv, CONV_K - 1 - k) for k in range(CONV_K))
        o_ref[...] = _silu(pre)

    blk = pl.BlockSpec((None, s, CONV_CT), lambda b, j: (b, 0, j))
    return pl.pallas_call(
        body, name=name, grid=(bsz, CCH // CONV_CT),
        in_specs=[blk, pl.BlockSpec((CONV_K, CONV_CT), lambda b, j: (0, j)), pl.BlockSpec((1, CONV_CT), lambda b, j: (0, j))],
        out_specs=blk, out_shape=jax.ShapeDtypeStruct((bsz, s, CCH), F32),
        compiler_params=_cp(("parallel", "parallel")),
    )(p, w, bias)


def conv_bwd(p, dxc, w, bias, name):
    bsz, s, _ = p.shape

    def body(x_ref, d_ref, w_ref, b_ref, dx_ref, dw_ref, db_ref):
        b = pl.program_id(1)
        xv, wv = x_ref[...], w_ref[...]
        xs = [_shift_down(xv, CONV_K - 1 - k) for k in range(CONV_K)]
        pre = b_ref[...] + sum(wv[k:k + 1, :] * xs[k] for k in range(CONV_K))
        sg = 1.0 / (1.0 + jnp.exp(-pre))
        dpre = d_ref[...] * (sg * (1.0 + pre * (1.0 - sg)))
        dx_ref[...] = sum(wv[k:k + 1, :] * _shift_up(dpre, CONV_K - 1 - k) for k in range(CONV_K)).astype(dx_ref.dtype)
        p_w = jnp.concatenate([jnp.sum(dpre * xs[k], axis=0, keepdims=True) for k in range(CONV_K)], axis=0)
        p_b = jnp.sum(dpre, axis=0, keepdims=True)
        _accumulate(b == 0, (dw_ref, db_ref), (p_w, p_b))

    blk = pl.BlockSpec((None, s, CONV_CT), lambda j, b: (b, 0, j))
    wsp = pl.BlockSpec((CONV_K, CONV_CT), lambda j, b: (0, j))
    bsp = pl.BlockSpec((1, CONV_CT), lambda j, b: (0, j))
    return pl.pallas_call(
        body, name=name, grid=(CCH // CONV_CT, bsz),
        in_specs=[blk, blk, wsp, bsp], out_specs=[blk, wsp, bsp],
        out_shape=[jax.ShapeDtypeStruct((bsz, s, CCH), BF16), jax.ShapeDtypeStruct((CONV_K, CCH), F32),
                   jax.ShapeDtypeStruct((1, CCH), F32)],
        compiler_params=_cp(("parallel", "arbitrary")),
    )(p, dxc, w, bias)


def _ssd_consts():
    c = CHUNK
    r = lax.broadcasted_iota(jnp.int32, (c, c), 0)
    q = lax.broadcasted_iota(jnp.int32, (c, c), 1)
    hrow = lax.broadcasted_iota(jnp.int32, (LANES, SSW), 0)
    hcol = lax.broadcasted_iota(jnp.int32, (LANES, SSW), 1) // SSM_HD
    expand = (hrow == hcol).astype(F32)
    return expand, (r >= q).astype(F32), (r <= q).astype(F32), r >= q


def _ssd_chunk(xc, dtr, z, prev_t, dt_bias, a_log, d_skip, norm_g):
    c = xc.shape[0]
    expand, tril1, triu1, causal = _ssd_consts()
    xs, bm, cm = xc[:, :SSW], xc[:, SSW:SSW + BCW], xc[:, SSW + BCW:]
    dt = _softplus(dtr + dt_bias)
    da = dt * (-jnp.exp(a_log))
    a_cs = _hdot(tril1, da)
    a_cs_t = _hdot(da.T, triu1)
    dt_e = _hdot(dt, expand)
    acs_e = _hdot(a_cs, expand)
    alast_e = acs_e[c - 1:c, :]
    dsk_e = _hdot(jnp.broadcast_to(d_skip, (8, LANES)), expand)[0:1, :]
    xdt = xs * dt_e
    hg = SSM_HEADS // SSM_GROUPS
    ys, new_t = [], []
    for g in range(SSM_GROUPS):
        bg = bm[:, g * SSM_STATE:(g + 1) * SSM_STATE]
        cg = cm[:, g * SSM_STATE:(g + 1) * SSM_STATE]
        sl = slice(g * GRW, (g + 1) * GRW)
        cb = _nt(cg, bg)
        xdt_g = xdt[:, sl]
        st = _tn(bg, xdt_g * jnp.exp(alast_e[:, sl] - acs_e[:, sl]))
        new_t.append(prev_t[:, sl] * jnp.exp(alast_e[:, sl]) + st)
        y_off = _nn(cg, prev_t[:, sl]) * jnp.exp(acs_e[:, sl])
        yd = []
        low = lax.broadcasted_iota(jnp.int32, (c, LANES), 1) < SSM_HD
        for pair in range(hg // 2):
            xp = xdt_g[:, pair * LANES:(pair + 1) * LANES]
            acc = None
            for side, xh in enumerate((jnp.where(low, xp, 0.0), jnp.where(low, 0.0, xp))):
                h = g * hg + 2 * pair + side
                decay = jnp.exp(jnp.where(causal, a_cs[:, h:h + 1] - a_cs_t[h:h + 1, :], NEG))
                part = _nn(cb * decay, xh)
                acc = part if acc is None else acc + part
            yd.append(acc)
        ys.append(jnp.concatenate(yd, axis=1) + y_off)
    y = (jnp.concatenate(ys, axis=1) + xs * dsk_e) * _silu(z)
    yn = [y[:, g * GRW:(g + 1) * GRW] * lax.rsqrt(jnp.mean(jnp.square(y[:, g * GRW:(g + 1) * GRW]), axis=-1, keepdims=True) + EPS)
          for g in range(SSM_GROUPS)]
    return jnp.concatenate(yn, axis=1) * norm_g, jnp.concatenate(new_t, axis=1)


def ssd_fwd(xc, p, prm, name):
    bsz, s, _ = p.shape
    nc = s // CHUNK

    def body(xc_ref, dt_ref, z_ref, db, al, dk, ng, o_ref, st_ref, state):
        @pl.when(pl.program_id(0) == 0)
        def _():
            state[...] = jnp.zeros_like(state)

        for b in range(bsz):
            prev = state[b]
            st_ref[b, 0] = prev
            out, new = _ssd_chunk(xc_ref[b], dt_ref[b], z_ref[b], prev, db[...], al[...], dk[...], ng[...])
            o_ref[b] = out.astype(o_ref.dtype)
            state[b] = new

    vec = pl.BlockSpec((1, LANES), lambda c: (0, 0))
    return pl.pallas_call(
        body, name=name, grid=(nc,),
        in_specs=[pl.BlockSpec((bsz, CHUNK, CCH), lambda c: (0, c, 0)),
                  pl.BlockSpec((bsz, CHUNK, LANES), lambda c: (0, c, OFF_DT // LANES)),
                  pl.BlockSpec((bsz, CHUNK, SSW), lambda c: (0, c, OFF_Z // SSW)),
                  vec, vec, vec, pl.BlockSpec((1, SSW), lambda c: (0, 0))],
        out_specs=[pl.BlockSpec((bsz, CHUNK, SSW), lambda c: (0, c, 0)),
                   pl.BlockSpec((bsz, 1, SSM_STATE, SSW), lambda c: (0, c, 0, 0))],
        out_shape=[jax.ShapeDtypeStruct((bsz, s, SSW), BF16), jax.ShapeDtypeStruct((bsz, nc, SSM_STATE, SSW), F32)],
        scratch_shapes=[pltpu.VMEM((bsz, SSM_STATE, SSW), F32)],
        compiler_params=_cp(("arbitrary",)),
    )(xc, p, p, *prm)


def ssd_bwd(xc, p, states, dmix, prm, name):
    bsz, s, _ = p.shape
    nc = s // CHUNK

    def body(xc_ref, dt_ref, z_ref, st_ref, do_ref, db, al, dk, ng, dxc_ref, ddt_ref, dz_ref, *rest):
        dpar, dstate = rest[:4], rest[4]
        c = pl.program_id(0)

        @pl.when(c == 0)
        def _():
            dstate[...] = jnp.zeros_like(dstate)

        dpars = None
        for b in range(bsz):
            _, vjp = jax.vjp(_ssd_chunk, xc_ref[b], dt_ref[b], z_ref[b], st_ref[b, 0], db[...], al[...], dk[...], ng[...])
            gr = vjp((do_ref[b], dstate[b]))
            dxc_ref[b] = gr[0]
            ddt_ref[b] = gr[1].astype(ddt_ref.dtype)
            dz_ref[b] = gr[2].astype(dz_ref.dtype)
            dstate[b] = gr[3]
            dpars = gr[4:] if dpars is None else [u + v for u, v in zip(dpars, gr[4:])]
        _accumulate(c == 0, dpar, dpars)

    rv = lambda c: nc - 1 - c
    vec = pl.BlockSpec((1, LANES), lambda c: (0, 0))
    ngs = pl.BlockSpec((1, SSW), lambda c: (0, 0))
    return pl.pallas_call(
        body, name=name, grid=(nc,),
        in_specs=[pl.BlockSpec((bsz, CHUNK, CCH), lambda c: (0, rv(c), 0)),
                  pl.BlockSpec((bsz, CHUNK, LANES), lambda c: (0, rv(c), OFF_DT // LANES)),
                  pl.BlockSpec((bsz, CHUNK, SSW), lambda c: (0, rv(c), OFF_Z // SSW)),
                  pl.BlockSpec((bsz, 1, SSM_STATE, SSW), lambda c: (0, rv(c), 0, 0)),
                  pl.BlockSpec((bsz, CHUNK, SSW), lambda c: (0, rv(c), (GMW + ATW) // SSW)),
                  vec, vec, vec, ngs],
        out_specs=[pl.BlockSpec((bsz, CHUNK, CCH), lambda c: (0, rv(c), 0)),
                   pl.BlockSpec((bsz, CHUNK, LANES), lambda c: (0, rv(c), 0)),
                   pl.BlockSpec((bsz, CHUNK, SSW), lambda c: (0, rv(c), 0)),
                   vec, vec, vec, ngs],
        out_shape=[jax.ShapeDtypeStruct((bsz, s, CCH), F32), jax.ShapeDtypeStruct((bsz, s, LANES), BF16),
                   jax.ShapeDtypeStruct((bsz, s, SSW), BF16)] + [jax.ShapeDtypeStruct((1, LANES), F32)] * 3
                  + [jax.ShapeDtypeStruct((1, SSW), F32)],
        scratch_shapes=[pltpu.VMEM((bsz, SSM_STATE, SSW), F32)],
        compiler_params=_cp(("arbitrary",)),
    )(xc, p, p, states, dmix, *prm)


def _rows2d(a):
    return a.reshape(-1, a.shape[-1])


def _ew_tile(r, c):
    t = r
    while t * c > (1 << 20) and t % 16 == 0:
        t //= 2
    return t


def add_pair(g, theirs, core, name):
    k, r, c = g.shape
    h = r // 2
    tr = _ew_tile(h, c)
    nb = h // tr

    def body(c_ref, a_ref, b_ref, o_ref, ob_ref):
        s = a_ref[...] + b_ref[...]
        o_ref[...] = s
        ob_ref[...] = s.astype(ob_ref.dtype)

    blk = pl.BlockSpec((None, tr, c), lambda kk, i, cr: (kk, i, 0))
    return pl.pallas_call(
        body, name=name,
        grid_spec=pltpu.PrefetchScalarGridSpec(
            num_scalar_prefetch=1, grid=(k, nb),
            in_specs=[pl.BlockSpec((None, tr, c), lambda kk, i, cr: (kk, cr[0] * nb + i, 0)), blk],
            out_specs=[blk, blk]),
        out_shape=[jax.ShapeDtypeStruct(theirs.shape, F32), jax.ShapeDtypeStruct(theirs.shape, BF16)],
        compiler_params=_cp(("parallel", "parallel")),
    )(core.reshape(1).astype(jnp.int32), g, theirs)


def sum_own_recv(sums, recv, chip, core, name):
    _, h, c = sums.shape
    tr = _ew_tile(h, c)
    nb = h // tr

    def body(k_ref, o_ref, r_ref, out_ref):
        s = o_ref[...]
        for j in range(3):
            s = s + r_ref[j].astype(F32)
        out_ref[...] = s

    return pl.pallas_call(
        body, name=name,
        grid_spec=pltpu.PrefetchScalarGridSpec(
            num_scalar_prefetch=1, grid=(nb,),
            in_specs=[pl.BlockSpec((None, tr, c), lambda i, kr: (kr[0], i, 0)),
                      pl.BlockSpec((3, tr, c), lambda i, kr: (0, i, 0))],
            out_specs=pl.BlockSpec((tr, c), lambda i, kr: (kr[1] * nb + i, 0))),
        out_shape=jax.ShapeDtypeStruct((2 * h, c), F32),
        compiler_params=_cp(("parallel",)),
    )(jnp.stack([chip, core]).astype(jnp.int32), sums, recv)


def _adam_math(w, m, v, g):
    mn = ADAM_B1 * m + (1.0 - ADAM_B1) * g
    vn = ADAM_B2 * v + (1.0 - ADAM_B2) * (g * g)
    mh = mn / (1.0 - ADAM_B1 ** ADAM_STEP)
    vh = vn / (1.0 - ADAM_B2 ** ADAM_STEP)
    return -ADAM_LR * (mh / (jnp.sqrt(vh) + ADAM_EPS) + ADAM_WD * w), mn, vn


def adamw_minor_rows(w, m, v, g, name):
    r, nl, c = w.shape
    tr = max(t for t in range(1, r + 1) if r % t == 0 and t * nl * c <= (1 << 19))

    def body(w_ref, m_ref, v_ref, g_ref, d_ref, mo_ref, vo_ref):
        d_ref[...], mo_ref[...], vo_ref[...] = _adam_math(w_ref[...], m_ref[...], v_ref[...], g_ref[...])

    blk = pl.BlockSpec((tr, nl, c), lambda i: (i, 0, 0))
    return pl.pallas_call(
        body, name=name, grid=(r // tr,), in_specs=[blk] * 4, out_specs=[blk] * 3,
        out_shape=[jax.ShapeDtypeStruct(w.shape, F32)] * 3, compiler_params=_cp(("parallel",)),
    )(w, m, v, g)


def adamw_layer(w, m, v, g, layer, prev, name):
    nl, r, c = w.shape
    tr = _ew_tile(r, c * 2)

    def body(w_ref, m_ref, v_ref, g_ref, *rest):
        go_ref, d_ref, mo_ref, vo_ref = rest[-4:]
        gv = g_ref[...]
        dl, mn, vn = _adam_math(w_ref[...], m_ref[...], v_ref[...], gv)
        go_ref[...] = gv
        d_ref[...] = dl
        mo_ref[...] = mn
        vo_ref[...] = vn

    lay = pl.BlockSpec((None, tr, c), lambda i: (layer, i, 0))
    n_prev = 0 if prev is None else 4
    return pl.pallas_call(
        body, name=name, grid=(r // tr,),
        in_specs=[lay, lay, lay, pl.BlockSpec((tr, c), lambda i: (i, 0))] + [_ANY] * n_prev,
        out_specs=[lay] * 4, out_shape=[jax.ShapeDtypeStruct(w.shape, F32)] * 4,
        input_output_aliases={4 + i: i for i in range(n_prev)},
        compiler_params=_cp(("parallel",)),
    )(w, m, v, g, *(prev or ()))


def sum_devices(parts, name):
    n, r, c = parts.shape
    tr = _ew_tile(r, c * n)

    def body(p_ref, o_ref):
        s = p_ref[0]
        for j in range(1, n):
            s = s + p_ref[j]
        o_ref[...] = s

    return pl.pallas_call(
        body, name=name, grid=(r // tr,),
        in_specs=[pl.BlockSpec((n, tr, c), lambda i: (0, i, 0))],
        out_specs=pl.BlockSpec((tr, c), lambda i: (i, 0)),
        out_shape=jax.ShapeDtypeStruct((r, c), F32),
        compiler_params=_cp(("parallel",)),
    )(parts)


def adamw(w, m, v, g, name):
    r, c = w.shape
    tr = _ew_tile(r, c * 2)

    def body(w_ref, m_ref, v_ref, g_ref, d_ref, mo_ref, vo_ref):
        gv = g_ref[...]
        mn = ADAM_B1 * m_ref[...] + (1.0 - ADAM_B1) * gv
        vn = ADAM_B2 * v_ref[...] + (1.0 - ADAM_B2) * (gv * gv)
        mh = mn / (1.0 - ADAM_B1 ** ADAM_STEP)
        vh = vn / (1.0 - ADAM_B2 ** ADAM_STEP)
        d_ref[...] = -ADAM_LR * (mh / (jnp.sqrt(vh) + ADAM_EPS) + ADAM_WD * w_ref[...])
        mo_ref[...] = mn
        vo_ref[...] = vn

    blk = pl.BlockSpec((tr, c), lambda i: (i, 0))
    return pl.pallas_call(
        body, name=name, grid=(r // tr,), in_specs=[blk] * 4, out_specs=[blk] * 3,
        out_shape=[jax.ShapeDtypeStruct((r, c), F32)] * 3,
        compiler_params=_cp(("parallel",)),
    )(w, m, v, g)


def _place():
    x, y, c = lax.axis_index("x"), lax.axis_index("y"), lax.axis_index("c")
    chips = [(1 - x, y), (x, 1 - y), (1 - x, 1 - y)]
    return x, y, c, chips


def all_gather_small(v, name):
    r, w = v.shape

    def body(x_ref, out_ref, send_sems, recv_sems, local_sem):
        x, y, c, chips = _place()
        me, sibling = (x, y, c), (x, y, 1 - c)

        def rows(px, py, pc):
            return out_ref.at[pl.ds((4 * px + 2 * py + pc) * r, r), :]

        def copy(k, block, to, src=None):
            return pltpu.make_async_remote_copy(
                src_ref=rows(*block) if src is None else src, dst_ref=rows(*block),
                send_sem=send_sems.at[k], recv_sem=recv_sems.at[k], device_id=to, device_id_type=MESH)

        mine = pltpu.make_async_copy(x_ref, rows(*me), local_sem)
        mine.start()
        first = [copy(0, me, sibling, src=x_ref)]
        first += [copy(1 + j, me, (*chip, c), src=x_ref) for j, chip in enumerate(chips)]
        for cp in first:
            cp.start()
        passed = [copy(4 + j, (*chip, c), sibling) for j, chip in enumerate(chips)]
        for j, chip in enumerate(chips):
            copy(1 + j, (*chip, c), me).wait_recv()
            passed[j].start()
        copy(0, sibling, me).wait_recv()
        for j, chip in enumerate(chips):
            copy(4 + j, (*chip, 1 - c), me).wait_recv()
        for cp in first + passed:
            cp.wait_send()
        mine.wait()

    out = pl.pallas_call(
        body, name=name, out_shape=jax.ShapeDtypeStruct((8 * r, w), v.dtype),
        in_specs=[pl.BlockSpec(memory_space=pltpu.VMEM)], out_specs=pl.BlockSpec(memory_space=pltpu.VMEM),
        scratch_shapes=[pltpu.SemaphoreType.DMA((7,)), pltpu.SemaphoreType.DMA((7,)), pltpu.SemaphoreType.DMA],
        compiler_params=pltpu.CompilerParams(vmem_limit_bytes=VMEM_LIMIT),
    )(v)
    return out.reshape(8, r, w)


_HBM = pl.BlockSpec(memory_space=pltpu.HBM)


_SEM = pl.BlockSpec(memory_space=pltpu.SEMAPHORE)
_ANY = pl.BlockSpec(memory_space=pl.ANY)
_EFFECT = pltpu.SideEffectType.DATAFLOW_SIDE_EFFECTING


def _hbm(a):
    return pltpu.with_memory_space_constraint(a, pltpu.HBM)


def split_copy_start(srcs, land_shapes, copies, after, name):
    n, nl = len(srcs), len(land_shapes)
    n_after = 0 if after is None else 1
    ncopy = [0]

    def body(*refs):
        ins, lands = refs[:n], refs[n:n + nl]
        send_sems, recv_sems = refs[n + nl + n_after], refs[n + nl + n_after + 1]
        token = refs[-1]
        x, y, c, chips = _place()
        for k, (src, dst, to) in enumerate(copies(x, y, c, chips, ins, lands)):
            pltpu.make_async_remote_copy(src_ref=src, dst_ref=dst, send_sem=send_sems.at[k], recv_sem=recv_sems.at[k],
                                         device_id=to, device_id_type=MESH).start()
        token[...] = jnp.zeros_like(token)

    ncopy[0] = len(copies(0, 0, 0, [(1, 0), (0, 1), (1, 1)], [None] * n, [None] * nl, count_only=True))
    k = ncopy[0]
    lands = [_hbm(lax.empty(s.shape, s.dtype)) for s in land_shapes]
    res = pl.pallas_call(
        body, name=name,
        out_shape=(pltpu.SemaphoreType.DMA((k,)), pltpu.SemaphoreType.DMA((k,)))
        + tuple(pltpu.HBM(s.shape, s.dtype) for s in srcs) + tuple(pltpu.HBM(s.shape, s.dtype) for s in land_shapes)
        + (jax.ShapeDtypeStruct((8, LANES), F32),),
        in_specs=[_HBM] * (n + nl) + [_ANY] * n_after,
        out_specs=(_SEM, _SEM) + (_HBM,) * (n + nl) + (pl.BlockSpec(memory_space=pltpu.VMEM),),
        input_output_aliases={i: 2 + i for i in range(n + nl)},
        compiler_params=pltpu.CompilerParams(has_side_effects=_EFFECT),
    )(*[_hbm(s) for s in srcs], *lands, *([after] if n_after else []))
    return res[0], res[1], list(res[2:2 + n]), list(res[2 + n:2 + n + nl]), res[-1]


def split_copy_wait(send_sems, recv_sems, srcs, lands, copies, after, name):
    n, nl = len(srcs), len(lands)

    def body(*refs):
        ins, lnd = refs[:n], refs[n:n + nl]
        ss, rs = refs[n + nl], refs[n + nl + 1]
        x, y, c, chips = _place()
        for k, (src, dst, to) in enumerate(copies(x, y, c, chips, ins, lnd, receive=True)):
            cp = pltpu.make_async_remote_copy(src_ref=src, dst_ref=dst, send_sem=ss.at[k], recv_sem=rs.at[k],
                                              device_id=to, device_id_type=MESH)
            cp.wait_send()
            cp.wait_recv()

    res = pl.pallas_call(
        body, name=name,
        out_shape=tuple(pltpu.HBM(s.shape, s.dtype) for s in srcs) + tuple(pltpu.HBM(s.shape, s.dtype) for s in lands),
        in_specs=[_HBM] * (n + nl) + [_SEM, _SEM, _ANY], out_specs=(_HBM,) * (n + nl),
        input_output_aliases={i: i for i in range(n + nl)},
        compiler_params=pltpu.CompilerParams(has_side_effects=_EFFECT),
    )(*srcs, *lands, send_sems, recv_sems, after)
    return list(res[:n]), list(res[n:])


def _gather_copies(x, y, c, chips, ins, lands, receive=False, count_only=False):
    out = []
    for i in range(len(ins)):
        for cx, cy in chips:
            if count_only:
                out.append(None)
                continue
            h = ins[i].shape[0] // 2
            rows = pl.ds(c * h, h)
            k_dst = (2 * cx + cy) if receive else (2 * x + y)
            out.append((ins[i].at[rows, :], lands[i].at[k_dst, rows, :], (cx, cy, c)))
    for i in range(len(ins)):
        out.append(None if count_only else (ins[i], lands[i].at[2 * x + y], (x, y, 1 - c)))
    return out


def _swap_copies(x, y, c, chips, ins, lands, receive=False, count_only=False):
    out = []
    for i in range(len(ins)):
        if count_only:
            out.append(None)
            continue
        h = ins[i].shape[1] // 2
        out.append((ins[i].at[:, pl.ds((1 - c) * h, h), :], lands[i], (x, y, 1 - c)))
    return out


def _scatter_copies(x, y, c, chips, ins, lands, receive=False, count_only=False):
    out = []
    for i in range(len(ins)):
        for j, (cx, cy) in enumerate(chips):
            if count_only:
                out.append(None)
                continue
            out.append((ins[i].at[2 * cx + cy], lands[i].at[j], (cx, cy, c)))
    return out


def forward_halves(lands, name):
    n = len(lands)

    def body(*refs):
        ins, outs = refs[:n], refs[n:2 * n]
        send_sems, recv_sems = refs[2 * n:]
        x, y, c, chips = _place()
        sibling = (x, y, 1 - c)
        sent = []
        for i in range(n):
            h = ins[i].shape[1] // 2
            for j, (cx, cy) in enumerate(chips):
                blk = ins[i].at[2 * cx + cy, pl.ds(c * h, h), :]
                sent.append(pltpu.make_async_remote_copy(
                    src_ref=blk, dst_ref=outs[i].at[2 * cx + cy, pl.ds(c * h, h), :], send_sem=send_sems.at[3 * i + j],
                    recv_sem=recv_sems.at[3 * i + j], device_id=sibling, device_id_type=MESH))
                sent[-1].start()
        for i in range(n):
            h = ins[i].shape[1] // 2
            for j, (cx, cy) in enumerate(chips):
                theirs = outs[i].at[2 * cx + cy, pl.ds((1 - c) * h, h), :]
                pltpu.make_async_remote_copy(
                    src_ref=theirs, dst_ref=theirs, send_sem=send_sems.at[3 * i + j], recv_sem=recv_sems.at[3 * i + j],
                    device_id=sibling, device_id_type=MESH).wait_recv()
        for cp in sent:
            cp.wait_send()

    return pl.pallas_call(
        body, name=name, out_shape=[jax.ShapeDtypeStruct(s.shape, s.dtype) for s in lands],
        in_specs=[_HBM] * n, out_specs=[_HBM] * n, input_output_aliases={i: i for i in range(n)},
        scratch_shapes=[pltpu.SemaphoreType.DMA((3 * n,)), pltpu.SemaphoreType.DMA((3 * n,))],
    )(*lands)


def join_halves(halves, name):
    n = len(halves)

    def body(*refs):
        ins, outs = refs[:n], refs[n:2 * n]
        send_sems, recv_sems = refs[2 * n:]
        x, y, c, _ = _place()
        sibling = (x, y, 1 - c)
        sent = []
        for i in range(n):
            h = ins[i].shape[0] // 2
            sent.append(pltpu.make_async_remote_copy(
                src_ref=ins[i].at[pl.ds(c * h, h), :], dst_ref=outs[i].at[pl.ds(c * h, h), :], send_sem=send_sems.at[i],
                recv_sem=recv_sems.at[i], device_id=sibling, device_id_type=MESH))
            sent[-1].start()
        for i in range(n):
            h = ins[i].shape[0] // 2
            theirs = outs[i].at[pl.ds((1 - c) * h, h), :]
            pltpu.make_async_remote_copy(
                src_ref=theirs, dst_ref=theirs, send_sem=send_sems.at[i],
                recv_sem=recv_sems.at[i], device_id=sibling, device_id_type=MESH).wait_recv()
        for cp in sent:
            cp.wait_send()

    return pl.pallas_call(
        body, name=name, out_shape=[jax.ShapeDtypeStruct(s.shape, F32) for s in halves],
        in_specs=[_HBM] * n, out_specs=[_HBM] * n, input_output_aliases={i: i for i in range(n)},
        scratch_shapes=[pltpu.SemaphoreType.DMA((n,)), pltpu.SemaphoreType.DMA((n,))],
    )(*halves)


_PACK_ROWS = 8 * LANES


def _pack(arrs):
    flat = jnp.concatenate([a.reshape(-1).astype(F32) for a in arrs])
    pad = (-flat.shape[0]) % _PACK_ROWS
    return jnp.pad(flat, (0, pad)).reshape(-1, LANES)


def _unpack(flat, shapes):
    flat = flat.reshape(-1)
    out, off = [], 0
    for s in shapes:
        n = int(np.prod(s))
        out.append(flat[off:off + n].reshape(s))
        off += n
    return out


_SEGS = [(0, OFF_U, GMW), (GMW, OFF_V, GMW), (2 * GMW, OFF_Q, ATW), (2 * GMW + ATW, OFF_K, KVW),
         (2 * GMW + ATW + KVW, OFF_VV, KVW), (2 * GMW + ATW + 2 * KVW, OFF_Z, SSW),
         (2 * GMW + ATW + 2 * KVW + SSW, OFF_XBC, CCH), (IN_W - SSM_HEADS, OFF_DT, SSM_HEADS)]


def _win_to_kernel_layout(w):
    out = jnp.zeros((w.shape[0], PW), w.dtype)
    for src, dst, wd in _SEGS:
        out = lax.dynamic_update_slice(out, w[:, src:src + wd], (0, dst))
    return out


def _win_from_kernel_layout(w):
    return jnp.concatenate([w[:, dst:dst + wd] for _, dst, wd in _SEGS], axis=1)


def _relu2(a):
    r = jnp.maximum(a, 0)
    return r * r


def kernel(x, c, ada_w, ada_b, norm1_g, w_in, gm_ln_g, gm_ln_b, gm_ws, gm_bs, gm_norm_g, attn_sinks, attn_norm_g, conv_w, conv_b, dt_bias, a_log, d_skip, ssm_norm_g, w_out, norm2_g, w_mlp1, w_mlp2, final_norm_g, loss_target, m_ada_w, m_ada_b, m_norm1_g, m_w_in, m_gm_ln_g, m_gm_ln_b, m_gm_ws, m_gm_bs, m_gm_norm_g, m_attn_sinks, m_attn_norm_g, m_conv_w, m_conv_b, m_dt_bias, m_a_log, m_d_skip, m_ssm_norm_g, m_w_out, m_norm2_g, m_w_mlp1, m_w_mlp2, m_final_norm_g, v_ada_w, v_ada_b, v_norm1_g, v_w_in, v_gm_ln_g, v_gm_ln_b, v_gm_ws, v_gm_bs, v_gm_norm_g, v_attn_sinks, v_attn_norm_g, v_conv_w, v_conv_b, v_dt_bias, v_a_log, v_d_skip, v_ssm_norm_g, v_w_out, v_norm2_g, v_w_mlp1, v_w_mlp2, v_final_norm_g):
    nl = ada_w.shape[0]
    bl, s, d = x.shape
    t = bl * s
    dff4 = w_mlp1.shape[2]
    dff = 4 * dff4
    mod_w = ada_w.shape[2]
    cw_w = conv_w.shape[2]
    xi, yi, ci = lax.axis_index("x"), lax.axis_index("y"), lax.axis_index("c")
    chip = 2 * xi + yi
    dev = 2 * chip + ci
    nex = 8 * bl

    g0 = all_gather_small(_pack([c, conv_w]), "ag_c")
    g0 = g0.reshape(8, -1)
    c_all = g0[:, :bl * d].reshape(nex, d)
    cw_parts = g0[0::2, bl * d:bl * d + conv_w.size].reshape(4, nl, CONV_K, cw_w)
    conv_w_full = cw_parts.transpose(1, 2, 0, 3).reshape(nl, CONV_K, CCH)

    def c_act(a):
        return _silu(a).astype(BF16)

    def to_bf16(a):
        return a.astype(BF16)

    mod_parts = []
    for l in range(nl):
        bias = lax.dynamic_slice(ada_b[l].reshape(1, -1), (0, chip * mod_w), (1, mod_w))
        mod_parts.append(_mm("nn", c_all, ada_w, dims=(nex, mod_w, d), tm=nex, tn=512, tk=d, out_dtypes=[F32],
                             name=f"mod_{l}", pro_a=c_act, pro_b=to_bf16,
                             b_spec=pl.BlockSpec((None, d, 512), lambda i, j, kk, l=l: (l, kk, j)),
                             extras=[(bias, pl.BlockSpec((1, 512), lambda i, j, kk: (0, j)))],
                             epi=lambda acc, bv: (acc + bv,))[0])
    g1 = all_gather_small(_pack(mod_parts), "ag_mod").reshape(8, -1)
    mod_all = g1[0::2, :nl * nex * mod_w].reshape(4, nl, nex, mod_w).transpose(1, 2, 0, 3).reshape(nl, nex, 4 * mod_w)
    mod = lax.dynamic_slice(mod_all, (0, dev * bl, 0), (nl, bl, 4 * mod_w))
    mods = [[mod[l, :, i * d:(i + 1) * d].reshape(bl, 1, d) for i in range(6)] for l in range(nl)]

    shards = [[w_in[l].astype(BF16), w_out[l].astype(BF16), w_mlp1[l].astype(BF16), w_mlp2[l].astype(BF16)]
              for l in range(nl)]
    groups = [[shards[0][i]] for i in range(4)] + [shards[l] for l in range(1, nl)]
    pending, after = [], g1
    for gi, grp in enumerate(groups):
        ss, rs, srcs, lands, after = split_copy_start(
            grp, [jax.ShapeDtypeStruct((4,) + a.shape, a.dtype) for a in grp], _gather_copies, after, f"gather_start_{gi}")
        pending.append((ss, rs, srcs, lands))
    mods[0][0] = mods[0][0] + after[0, 0]

    def fetch(gi, behind):
        ss, rs, srcs, lands = pending[gi]
        srcs, lands = split_copy_wait(ss, rs, srcs, lands, _gather_copies, behind, f"gather_wait_{gi}")
        return forward_halves(lands, f"gather_pass_{gi}")

    def as_win(g):
        return _win_to_kernel_layout(g.transpose(1, 0, 2).reshape(d, IN_W))

    wfull = [None] * nl
    row = lambda a: a.reshape(1, -1)
    pad16 = lambda a: jnp.pad(a.reshape(1, -1), ((0, 0), (0, LANES - SSM_HEADS)))
    tm_res = min(1024, s)

    def residual(acc, xt, gt):
        return acc, xt + gt * acc

    def res_extras(xin, gate, tm=tm_res):
        return [(xin.reshape(t, d), pl.BlockSpec((tm, 512), lambda i, j, kk: (i, j))),
                (gate, pl.BlockSpec((None, 1, 512), lambda i, j, kk: (i * tm // s, 0, j)))]

    w1_blk = lambda tk, tn: pl.BlockSpec((None, tk, tn), lambda i, j, kk: (j // (dff4 // tn), kk, j % (dff4 // tn)))

    saved = []
    xcur = x
    for l in range(nl):
        sh1, sc1, gt1, sh2, sc2, gt2 = mods[l]
        if l == 0:
            win = as_win(fetch(0, mod)[0])
        else:
            g_in, g_out, w1, g_2 = fetch(3 + l, xcur)
            win, wout, w2 = as_win(g_in), g_out.reshape(-1, d), g_2.reshape(dff, d)
        prm_a = (row(gm_ln_g[l]), row(gm_ln_b[l]), gm_ws[l], gm_bs[l].T, row(gm_norm_g[l]))
        prm_b = (row(attn_sinks[l]), row(attn_norm_g[l]))
        prm_c = (pad16(dt_bias[l]), pad16(a_log[l]), pad16(d_skip[l]), row(ssm_norm_g[l]))
        h1 = ln_mod_fwd(xcur, row(norm1_g[l]), sc1, sh1, f"ln1_fwd_{l}")
        p = _mm("nn", h1.reshape(t, d), win, dims=(t, PW, d), tm=1024, tn=512, tk=d, out_dtypes=[F32],
                name=f"proj_in_{l}")[0].reshape(bl, s, PW)
        out_a = gmlp_fwd(p, prm_a, f"gmlp_fwd_{l}")
        out_b = attn_fwd(p, *prm_b, f"attn_fwd_{l}")
        xc = conv_fwd(p, conv_w_full[l], row(conv_b[l]), f"conv_fwd_{l}")
        out_c, states = ssd_fwd(xc, p, prm_c, f"ssd_fwd_{l}")
        mix = jnp.concatenate([out_a, out_b, out_c], axis=-1)
        if l == 0:
            wout = fetch(1, mix)[0].reshape(-1, d)
        mm1, x2 = _mm("nn", mix.reshape(t, d), wout, dims=(t, d, d), tm=tm_res, tn=512, tk=d, out_dtypes=[F32, F32],
                      name=f"proj_out_{l}", extras=res_extras(xcur, gt1), epi=residual)
        x2 = x2.reshape(bl, s, d)
        h2 = ln_mod_fwd(x2, row(norm2_g[l]), sc2, sh2, f"ln2_fwd_{l}")
        if l == 0:
            w1 = fetch(2, h2)[0]
        a1 = _mm("nn", h2.reshape(t, d), w1, dims=(t, dff, d), tm=1024, tn=512, tk=d, out_dtypes=[BF16],
                 name=f"mlp1_{l}", b_spec=w1_blk(d, 512))[0]
        if l == 0:
            w2 = fetch(3, a1)[0].reshape(dff, d)
        tm2 = min(512, s)
        mm2, x3 = _mm("nn", a1, w2, dims=(t, d, dff), tm=tm2, tn=512, tk=dff, out_dtypes=[F32, F32],
                      name=f"mlp2_{l}", extras=res_extras(x2, gt2, tm2), epi=residual, pro_a=_relu2)
        x3 = x3.reshape(bl, s, d)
        wfull[l] = (win, wout, w1, w2)
        saved.append((xcur, h1, p, xc, states, mix, mm1.reshape(bl, s, d), x2, h2, a1, mm2.reshape(bl, s, d),
                      prm_a, prm_b, prm_c))
        xcur = x3

    dx, d_final_g, loss_part = loss_head(xcur, row(final_norm_g), loss_target, "loss_head")
    loss = lax.psum(loss_part[0, 0], ("x", "y", "c"))

    def rs_swap(grads, tag):
        ss, rs, srcs, lands, token = split_copy_start(
            grads, [jax.ShapeDtypeStruct((4, g.shape[1] // 2, g.shape[2]), F32) for g in grads],
            _swap_copies, None, f"rs_swap_{tag}")
        return (ss, rs, srcs, lands), token

    def rs_begin(swap_state, tag, swapped_behind, start_behind=None):
        ss, rs, srcs, lands = swap_state
        grads, theirs = split_copy_wait(ss, rs, srcs, lands, _swap_copies, swapped_behind, f"rs_swapped_{tag}")
        sums = [add_pair(g, th, ci, f"rs_add_{tag}_{i}") for i, (g, th) in enumerate(zip(grads, theirs))]
        ss, rs, srcs, lands, token = split_copy_start(
            [sm[1] for sm in sums], [jax.ShapeDtypeStruct((3,) + sm[1].shape[1:], BF16) for sm in sums],
            _scatter_copies, sums[0][0] if start_behind is None else start_behind, f"rs_start_{tag}")
        return (ss, rs, srcs, lands, [sm[0] for sm in sums]), token

    def rs_end(state, behind, tag):
        ss, rs, srcs, lands, sums_f32 = state
        _, got = split_copy_wait(ss, rs, srcs, lands, _scatter_copies, behind, f"rs_wait_{tag}")
        halves = [sum_own_recv(sf, g, chip, ci, f"rs_sum_{tag}_{i}") for i, (sf, g) in enumerate(zip(sums_f32, got))]
        return join_halves(halves, f"rs_join_{tag}")

    small_parts = [None] * nl
    dmods = [None] * nl
    reduced = [[None] * 4 for _ in range(nl)]
    pending_rs, rs_token = [], None
    part_slots = {"a": (0, 1), "m": (2, 3)}

    def finish(behind):
        for ll, part, state in pending_rs:
            for slot, blk in zip(part_slots[part], rs_end(state, behind, f"{ll}{part}")):
                reduced[ll][slot] = blk
        pending_rs.clear()

    for l in reversed(range(nl)):
        sh1, sc1, gt1, sh2, sc2, gt2 = mods[l]
        win, wout, w1, w2 = wfull[l]
        xin, h1, p, xc, states, mix, mm1, x2, h2, a1, mm2, prm_a, prm_b, prm_c = saved[l]
        if rs_token is not None:
            gt2 = gt2 + rs_token[0, 0]
        dm2, dgt2 = gate_bwd(dx, mm2, gt2, f"gate2_bwd_{l}")
        dm2 = dm2.reshape(t, d)
        da1 = _mm("nt", dm2, w2, dims=(t, dff, d), tm=1024, tn=512, tk=d, out_dtypes=[BF16], name=f"mlp2_dx_{l}",
                  extras=[(a1, pl.BlockSpec((1024 if t >= 1024 else t, 512), lambda i, j, kk: (i, j)))],
                  epi=lambda acc, av: (acc * (2.0 * jnp.maximum(av, 0).astype(F32)),))[0]
        dw2 = _mm("tn", a1, dm2, dims=(dff, d, t), tm=512, tn=d, tk=2048, out_dtypes=[F32], name=f"mlp2_dw_{l}",
                  pro_a=_relu2, out_shapes=[(4, dff4, d)],
                  out_specs=[pl.BlockSpec((None, 512, d), lambda i, j, kk: (i // (dff4 // 512), i % (dff4 // 512), 0))])[0]
        dw1 = _mm("tn", h2.reshape(t, d), da1, dims=(d, dff, t), tm=512, tn=dff4, tk=2048, out_dtypes=[F32],
                  name=f"mlp1_dw_{l}", out_shapes=[(4, d, dff4)],
                  out_specs=[pl.BlockSpec((None, 512, dff4), lambda i, j, kk: (j, i, 0))])[0]
        swap_state, swap_token = rs_swap([dw1, dw2], f"{l}m")
        dh2 = _mm_nt_blocked(da1, w1, tm=512, tn=512, name=f"mlp1_dx_{l}", behind=swap_token)
        mlp_state, mlp_token = rs_begin(swap_state, f"{l}m", dh2)
        sc2 = sc2 + mlp_token[0, 0]
        dx2, dsc2, dsh2, dn2 = ln_mod_bwd(dh2.reshape(bl, s, d), x2, dx, row(norm2_g[l]), sc2, f"ln2_bwd_{l}")
        dm1, dgt1 = gate_bwd(dx2, mm1, gt1, f"gate1_bwd_{l}")
        dm1 = dm1.reshape(t, d)
        dmix = _mm("nt", dm1, wout, dims=(t, d, d), tm=1024, tn=512, tk=d, out_dtypes=[F32],
                   name=f"proj_out_dx_{l}")[0].reshape(bl, s, d)
        dwout = _mm("tn", mix.reshape(t, d), dm1, dims=(d, d, t), tm=512, tn=d, tk=2048, out_dtypes=[F32],
                    name=f"proj_out_dw_{l}", out_shapes=[(4, d // 4, d)],
                    out_specs=[pl.BlockSpec((None, 512, d), lambda i, j, kk: (i // (d // 4 // 512), i % (d // 4 // 512), 0))])[0]
        du, dv, dlg, dlb, dws, dbst, dgng = gmlp_bwd(p, dmix, prm_a, f"gmlp_bwd_{l}")
        dq, dk, dvv, dsinks, dang = attn_bwd(p, dmix, *prm_b, f"attn_bwd_{l}")
        dxc, ddt, dz, ddtb, dalog, ddsk, dsng = ssd_bwd(xc, p, states, dmix, prm_c, f"ssd_bwd_{l}")
        dxbc, dcw, dcb = conv_bwd(p, dxc, conv_w_full[l], row(conv_b[l]), f"conv_bwd_{l}")
        dp = jnp.concatenate([dxbc, dq, dz, du, dv, dk, dvv, ddt, jnp.zeros((bl, s, PW - OFF_DT - LANES), BF16)],
                             axis=-1).reshape(t, PW)
        dwin = _mm("tn", h1.reshape(t, d), dp, dims=(d, PW, t), tm=512, tn=PW // 3, tk=2048, out_dtypes=[F32],
                   name=f"proj_in_dw_{l}")[0]
        dwin_blocks = _win_from_kernel_layout(dwin).reshape(d, 4, IN_W // 4).transpose(1, 0, 2)
        mixer_swap, swap_token = rs_swap([dwin_blocks, dwout], f"{l}a")
        dh1 = _mm("nt", dp, win, dims=(t, d, PW), tm=1024, tn=512, tk=PW, out_dtypes=[F32],
                  name=f"proj_in_dx_{l}", behind=swap_token)[0]
        dx, dsc1, dsh1, dn1 = ln_mod_bwd(dh1.reshape(bl, s, d), xin, dx2, row(norm1_g[l]), sc1, f"ln1_bwd_{l}")
        dmods[l] = jnp.concatenate([dsh1, dsc1, dgt1, dsh2, dsc2, dgt2], axis=-1).reshape(bl, 6 * d)
        small_parts[l] = [dn1, dlg, dlb, dws, dbst.T, dgng, dsinks, dang, dcw, dcb, ddtb[:, :SSM_HEADS],
                          dalog[:, :SSM_HEADS], ddsk[:, :SSM_HEADS], dsng, dn2]
        finish(dx)
        pending_rs.append((l, "m", mlp_state))
        if l > 0:
            state, rs_token = rs_begin(mixer_swap, f"{l}a", dx)
            pending_rs.append((l, "a", state))
    grad_x = dx

    big = [(w_in, m_w_in, v_w_in), (w_out, m_w_out, v_w_out), (w_mlp1, m_w_mlp1, v_w_mlp1), (w_mlp2, m_w_mlp2, v_w_mlp2)]
    big_out = [None] * 4
    for l in reversed(range(1, nl)):
        for i, (wt, mt, vt) in enumerate(big):
            if i > 0:
                big_out[i] = adamw_layer(wt, mt, vt, reduced[l][i], l, big_out[i], f"adamw_big_{i}_{l}")

    small_names = [norm1_g, gm_ln_g, gm_ln_b, gm_ws, gm_bs, gm_norm_g, attn_sinks, attn_norm_g, None, conv_b, dt_bias,
                   a_log, d_skip, ssm_norm_g, norm2_g]
    n_small = len(small_names)
    per_param = [jnp.stack([small_parts[l][i].reshape(-1) for l in range(nl)]) for i in range(n_small)]
    small_vec = _pack(per_param + [d_final_g])
    rs_small = small_vec.shape[0]
    dmod_local = jnp.stack(dmods, axis=1)
    g2 = all_gather_small(jnp.concatenate([small_vec, _pack([dmod_local])], axis=0), "ag_small")
    state, rs_token = rs_begin(mixer_swap, "0a", grad_x, start_behind=g2)
    pending_rs.append((0, "a", state))
    g2 = g2 + rs_token[0, 0]
    g_small = sum_devices(g2[:, :rs_small, :], "sum_small")
    dmod_all = g2[:, rs_small:, :].reshape(8, -1)[:, :bl * nl * 6 * d].reshape(nex, nl * 6 * d)
    g_ada_b = sum_devices(dmod_all.reshape(nex, -1, LANES), "sum_ada_b").reshape(nl, 6 * d)
    shapes = [(nl, int(np.prod(small_parts[0][i].shape))) for i in range(n_small)] + [(d,)]
    g_list = _unpack(g_small, shapes)
    g_conv_w = lax.dynamic_slice(g_list[8].reshape(nl, CONV_K, CCH), (0, 0, chip * cw_w), (nl, CONV_K, cw_w))

    dm_cols = lax.dynamic_slice(dmod_all.reshape(nex, nl, 6 * d), (0, 0, chip * mod_w), (nex, nl, mod_w))
    g_ada_w = _mm("tn", c_all, dm_cols.reshape(nex, nl * mod_w), dims=(d, nl * mod_w, nex), tm=512, tn=512, tk=nex,
                  out_dtypes=[F32], name="ada_w_grad", pro_a=c_act, pro_b=to_bf16, out_shapes=[(nl, d, mod_w)],
                  out_specs=[pl.BlockSpec((None, 512, 512), lambda i, j, kk: (j // (mod_w // 512), i, j % (mod_w // 512)))])[0]
    d_ada_w, m_ada_w_n, v_ada_w_n = [a.reshape(ada_w.shape) for a in
                                     adamw(_rows2d(ada_w), _rows2d(m_ada_w), _rows2d(v_ada_w), _rows2d(g_ada_w), "adamw_ada_w")]

    smalls = {
        "ada_b": (ada_b, m_ada_b, v_ada_b, g_ada_b), "norm1_g": (norm1_g, m_norm1_g, v_norm1_g, g_list[0]),
        "gm_ln_g": (gm_ln_g, m_gm_ln_g, v_gm_ln_g, g_list[1]), "gm_ln_b": (gm_ln_b, m_gm_ln_b, v_gm_ln_b, g_list[2]),
        "gm_ws": (gm_ws, m_gm_ws, v_gm_ws, g_list[3]), "gm_bs": (gm_bs, m_gm_bs, v_gm_bs, g_list[4]),
        "gm_norm_g": (gm_norm_g, m_gm_norm_g, v_gm_norm_g, g_list[5]),
        "attn_sinks": (attn_sinks, m_attn_sinks, v_attn_sinks, g_list[6]),
        "attn_norm_g": (attn_norm_g, m_attn_norm_g, v_attn_norm_g, g_list[7]),
        "conv_w": (conv_w, m_conv_w, v_conv_w, g_conv_w), "conv_b": (conv_b, m_conv_b, v_conv_b, g_list[9]),
        "dt_bias": (dt_bias, m_dt_bias, v_dt_bias, g_list[10]), "a_log": (a_log, m_a_log, v_a_log, g_list[11]),
        "d_skip": (d_skip, m_d_skip, v_d_skip, g_list[12]),
        "ssm_norm_g": (ssm_norm_g, m_ssm_norm_g, v_ssm_norm_g, g_list[13]),
        "norm2_g": (norm2_g, m_norm2_g, v_norm2_g, g_list[14]),
        "final_norm_g": (final_norm_g, m_final_norm_g, v_final_norm_g, g_list[15]),
    }
    keys = list(smalls)
    wv, mv, vv_, gv = [_pack([smalls[k][i].reshape(smalls[k][0].shape) for k in keys]) for i in range(4)]
    sd_, sm_, sv_ = adamw(wv, mv, vv_, gv, "adamw_small")
    shp = [smalls[k][0].shape for k in keys]
    small_out = {k: (smalls[k][3].reshape(smalls[k][0].shape), a, b, cc)
                 for k, a, b, cc in zip(keys, _unpack(sd_, shp), _unpack(sm_, shp), _unpack(sv_, shp))}

    late = jnp.zeros((8, LANES), F32) + (sv_[0, 0] + v_ada_w_n[0, 0, 0])
    for bo in big_out:
        if bo is not None:
            late = late + bo[3][nl - 1, 0, 0]
    finish(late)
    for i, (wt, mt, vt) in enumerate(big):
        if i > 0:
            big_out[i] = adamw_layer(wt, mt, vt, reduced[0][i], 0, big_out[i], f"adamw_big_{i}_0")
    minor_first = lambda a: jnp.transpose(a, (2, 0, 1))
    g_in = jnp.stack([reduced[l][0].T for l in range(nl)], axis=1)
    back = lambda a: jnp.transpose(a, (1, 2, 0))
    big_out[0] = [back(a) for a in [g_in, *adamw_minor_rows(minor_first(w_in), minor_first(m_w_in),
                                                            minor_first(v_w_in), g_in, "adamw_w_in")]]

    out = {"ada_w": (g_ada_w, d_ada_w, m_ada_w_n, v_ada_w_n), "w_in": big_out[0], "w_out": big_out[1],
           "w_mlp1": big_out[2], "w_mlp2": big_out[3], **small_out}
    order = ["ada_w", "ada_b", "norm1_g", "w_in", "gm_ln_g", "gm_ln_b", "gm_ws", "gm_bs", "gm_norm_g", "attn_sinks",
             "attn_norm_g", "conv_w", "conv_b", "dt_bias", "a_log", "d_skip", "ssm_norm_g", "w_out", "norm2_g",
             "w_mlp1", "w_mlp2", "final_norm_g"]
    return (loss, grad_x, *[out[k][0] for k in order], *[out[k][1] for k in order],
            *[out[k][2] for k in order], *[out[k][3] for k in order])
```

```python
import functools
import math

import jax
import jax.numpy as jnp
import numpy as np
from jax import lax
from jax.experimental import pallas as pl
from jax.experimental.pallas import tpu as pltpu

F32 = jnp.float32
BF16 = jnp.bfloat16
HI = lax.Precision.HIGHEST
MESH = pl.DeviceIdType.MESH

CHUNK = 128
GM_HEADS, GM_HD = 4, 128
ATT_HEADS, ATT_KV, ATT_HD = 8, 2, 64
WINDOW = 128
SSM_HEADS, SSM_HD, SSM_GROUPS, SSM_STATE, CONV_K = 16, 64, 2, 128, 4
EPS = 1e-6
LN_EPS = 1e-5
NEG = -1e30
LANES = 128

GMW = GM_HEADS * GM_HD
ATW = ATT_HEADS * ATT_HD
KVW = ATT_KV * ATT_HD
SSW = SSM_HEADS * SSM_HD
BCW = SSM_GROUPS * SSM_STATE
CCH = SSW + 2 * BCW
GRW = SSW // SSM_GROUPS
IN_SIZES = (GMW, GMW, ATW, KVW, KVW, SSW, CCH, SSM_HEADS)
IN_W = sum(IN_SIZES)
OFF_U, OFF_V, OFF_Q, OFF_K, OFF_VV, OFF_Z, OFF_XBC, OFF_DT = 0, 512, 1024, 1536, 1664, 1792, 2816, 4352
ZB = 256
PW = 4608

ADAM_LR, ADAM_B1, ADAM_B2, ADAM_EPS, ADAM_WD, ADAM_STEP = 0.001, 0.9, 0.999, 1e-08, 0.01, 10

VMEM_LIMIT = 56 * 1024 * 1024


def _cp(sem=None):
    return pltpu.CompilerParams(dimension_semantics=sem, vmem_limit_bytes=VMEM_LIMIT)


_DN = {"nn": (((1,), (0,)), ((), ())), "nt": (((1,), (1,)), ((), ())), "tn": (((0,), (0,)), ((), ()))}


def _dot(form, a, b):
    return lax.dot_general(a.astype(BF16), b.astype(BF16), _DN[form], preferred_element_type=F32)


@jax.custom_vjp
def _nn(a, b):
    return _dot("nn", a, b)


@jax.custom_vjp
def _nt(a, b):
    return _dot("nt", a, b)


@jax.custom_vjp
def _tn(a, b):
    return _dot("tn", a, b)


_nn.defvjp(lambda a, b: (_dot("nn", a, b), (a, b)), lambda r, g: (_dot("nt", g, r[1]), _dot("tn", r[0], g)))
_nt.defvjp(lambda a, b: (_dot("nt", a, b), (a, b)), lambda r, g: (_dot("nn", g, r[1]), _dot("tn", g, r[0])))
_tn.defvjp(lambda a, b: (_dot("tn", a, b), (a, b)), lambda r, g: (_dot("nt", r[1], g), _dot("nn", r[0], g)))


def _hdot(a, b):
    return jnp.dot(a, b, precision=HI, preferred_element_type=F32)


def _silu(x):
    return x * (1.0 / (1.0 + jnp.exp(-x)))


def _softplus(x):
    return jnp.maximum(x, 0.0) + jnp.log1p(jnp.exp(-jnp.abs(x)))


def _gelu(x):
    return 0.5 * x * (1.0 + jnp.tanh(math.sqrt(2.0 / math.pi) * (x + 0.044715 * (x * x * x))))


def _rms(y, g):
    return y * lax.rsqrt(jnp.mean(y * y, axis=-1, keepdims=True) + EPS) * g


def _mm(form, a, b, *, dims, tm, tn, tk, out_dtypes, name, a_spec=None, b_spec=None, out_specs=None,
        out_shapes=None, extras=(), epi=None, pro_a=None, pro_b=None, behind=None):
    m, n, k = dims
    tm, tn, tk = min(tm, m), min(tn, n), min(tk, k)
    assert m % tm == 0 and n % tn == 0 and k % tk == 0, (name, dims, tm, tn, tk)
    nk = k // tk
    if a_spec is None:
        a_spec = (pl.BlockSpec((tk, tm), lambda i, j, kk: (kk, i)) if form == "tn"
                  else pl.BlockSpec((tm, tk), lambda i, j, kk: (i, kk)))
    if b_spec is None:
        b_spec = (pl.BlockSpec((tn, tk), lambda i, j, kk: (j, kk)) if form == "nt"
                  else pl.BlockSpec((tk, tn), lambda i, j, kk: (kk, j)))
    n_out = len(out_dtypes)
    if out_specs is None:
        out_specs = [pl.BlockSpec((tm, tn), lambda i, j, kk: (i, j))] * n_out
    if out_shapes is None:
        out_shapes = [(m, n)] * n_out
    ne = len(extras)
    n_behind = 0 if behind is None else 1

    def body(*refs):
        a_ref, b_ref = refs[0], refs[1]
        ex = refs[2:2 + ne]
        outs = refs[2 + ne + n_behind:2 + ne + n_behind + n_out]

        def write(val):
            res = epi(val, *[e[...] for e in ex]) if epi is not None else (val,)
            for o, r in zip(outs, res):
                o[...] = r.astype(o.dtype)

        av = a_ref[...]
        if pro_a is not None:
            av = pro_a(av)
        bv = b_ref[...]
        if pro_b is not None:
            bv = pro_b(bv)
        part = lax.dot_general(av, bv, _DN[form], preferred_element_type=F32)
        if nk == 1:
            write(part)
        else:
            acc = refs[-1]
            kk = pl.program_id(2)

            @pl.when(kk == 0)
            def _():
                acc[...] = part

            @pl.when(kk > 0)
            def _():
                acc[...] += part

            @pl.when(kk == nk - 1)
            def _():
                write(acc[...])

    res = pl.pallas_call(
        body, name=name, grid=(m // tm, n // tn, nk),
        in_specs=[a_spec, b_spec] + [s for _, s in extras] + [_ANY] * n_behind,
        out_specs=out_specs,
        out_shape=[jax.ShapeDtypeStruct(s, d) for s, d in zip(out_shapes, out_dtypes)],
        scratch_shapes=[pltpu.VMEM((tm, tn), F32)] if nk > 1 else [],
        compiler_params=_cp(("parallel", "parallel", "arbitrary")),
    )(a, b, *[e for e, _ in extras], *([behind] if n_behind else []))
    return res


def _mm_nt_blocked(a, b, *, tm, tn, name, behind=None):
    m = a.shape[0]
    nparts, n, f = b.shape
    tm, tn = min(tm, m), min(tn, n)
    n_behind = 0 if behind is None else 1

    def body(a_ref, *rest):
        b_refs, o_ref = rest[:nparts], rest[nparts + n_behind]
        acc = None
        for k in range(nparts):
            part = lax.dot_general(a_ref[:, k * f:(k + 1) * f], b_refs[k][...], _DN["nt"], preferred_element_type=F32)
            acc = part if acc is None else acc + part
        o_ref[...] = acc

    return pl.pallas_call(
        body, name=name, grid=(m // tm, n // tn),
        in_specs=[pl.BlockSpec((tm, nparts * f), lambda i, j: (i, 0))]
        + [pl.BlockSpec((None, tn, f), lambda i, j, k=k: (k, j, 0)) for k in range(nparts)] + [_ANY] * n_behind,
        out_specs=pl.BlockSpec((tm, tn), lambda i, j: (i, j)),
        out_shape=jax.ShapeDtypeStruct((m, n), F32),
        compiler_params=_cp(("parallel", "parallel")),
    )(a, *([b] * nparts), *([behind] if n_behind else []))


def _row_tile(s):
    return min(512, s)


def ln_mod_fwd(x, g, sc, sh, name):
    bsz, s, d = x.shape
    ts = _row_tile(s)

    def body(x_ref, g_ref, sc_ref, sh_ref, o_ref):
        xv = x_ref[...]
        r = lax.rsqrt(jnp.mean(xv * xv, axis=-1, keepdims=True) + EPS)
        o_ref[...] = ((xv * r * g_ref[...]) * (1.0 + sc_ref[...]) + sh_ref[...]).astype(o_ref.dtype)

    row = pl.BlockSpec((None, ts, d), lambda b, i: (b, i, 0))
    vec = pl.BlockSpec((None, 1, d), lambda b, i: (b, 0, 0))
    return pl.pallas_call(
        body, name=name, grid=(bsz, s // ts),
        in_specs=[row, pl.BlockSpec((1, d), lambda b, i: (0, 0)), vec, vec],
        out_specs=row, out_shape=jax.ShapeDtypeStruct(x.shape, BF16),
        compiler_params=_cp(("parallel", "parallel")),
    )(x, g, sc, sh)


def ln_mod_bwd(dh, x, dres, g, sc, name):
    bsz, s, d = x.shape
    ts = _row_tile(s)

    def body(dh_ref, x_ref, dres_ref, g_ref, sc_ref, dx_ref, dsc_ref, dsh_ref, dg_ref):
        b, i = pl.program_id(0), pl.program_id(1)
        xv, dhv, gv = x_ref[...], dh_ref[...], g_ref[...]
        r = lax.rsqrt(jnp.mean(xv * xv, axis=-1, keepdims=True) + EPS)
        xn = xv * r
        a = dhv * (1.0 + sc_ref[...])
        dxn = a * gv
        dx_ref[...] = dres_ref[...] + r * (dxn - xn * jnp.mean(dxn * xn, axis=-1, keepdims=True))
        p_sc = jnp.sum(dhv * (xn * gv), axis=0, keepdims=True)
        p_sh = jnp.sum(dhv, axis=0, keepdims=True)
        p_g = jnp.sum(a * xn, axis=0, keepdims=True)

        @pl.when(i == 0)
        def _():
            dsc_ref[...] = p_sc
            dsh_ref[...] = p_sh

        @pl.when(i > 0)
        def _():
            dsc_ref[...] += p_sc
            dsh_ref[...] += p_sh

        @pl.when((i == 0) & (b == 0))
        def _():
            dg_ref[...] = p_g

        @pl.when((i > 0) | (b > 0))
        def _():
            dg_ref[...] += p_g

    row = pl.BlockSpec((None, ts, d), lambda b, i: (b, i, 0))
    vec = pl.BlockSpec((None, 1, d), lambda b, i: (b, 0, 0))
    one = pl.BlockSpec((1, d), lambda b, i: (0, 0))
    return pl.pallas_call(
        body, name=name, grid=(bsz, s // ts),
        in_specs=[row, row, row, one, vec],
        out_specs=[row, vec, vec, one],
        out_shape=[jax.ShapeDtypeStruct(x.shape, F32), jax.ShapeDtypeStruct((bsz, 1, d), F32),
                   jax.ShapeDtypeStruct((bsz, 1, d), F32), jax.ShapeDtypeStruct((1, d), F32)],
        compiler_params=_cp(("arbitrary", "arbitrary")),
    )(dh, x, dres, g, sc)


def gate_bwd(dx, mm, gate, name):
    bsz, s, d = dx.shape
    ts = _row_tile(s)

    def body(dx_ref, m_ref, g_ref, dm_ref, dg_ref):
        i = pl.program_id(1)
        dxv = dx_ref[...]
        dm_ref[...] = (dxv * g_ref[...]).astype(dm_ref.dtype)
        p = jnp.sum(dxv * m_ref[...], axis=0, keepdims=True)

        @pl.when(i == 0)
        def _():
            dg_ref[...] = p

        @pl.when(i > 0)
        def _():
            dg_ref[...] += p

    row = pl.BlockSpec((None, ts, d), lambda b, i: (b, i, 0))
    vec = pl.BlockSpec((None, 1, d), lambda b, i: (b, 0, 0))
    return pl.pallas_call(
        body, name=name, grid=(bsz, s // ts),
        in_specs=[row, row, vec], out_specs=[row, vec],
        out_shape=[jax.ShapeDtypeStruct(dx.shape, BF16), jax.ShapeDtypeStruct((bsz, 1, d), F32)],
        compiler_params=_cp(("parallel", "arbitrary")),
    )(dx, mm, gate)


def loss_head(x, g, tgt, name):
    bsz, s, d = x.shape
    ts = _row_tile(s)

    def body(x_ref, g_ref, t_ref, dx_ref, dg_ref, l_ref):
        b, i = pl.program_id(0), pl.program_id(1)
        xv, gv = x_ref[...], g_ref[...]
        r = lax.rsqrt(jnp.mean(xv * xv, axis=-1, keepdims=True) + EPS)
        xn = xv * r
        e = xn * gv - t_ref[...]
        dy = e * (1.0 / d)
        dxn = dy * gv
        dx_ref[...] = r * (dxn - xn * jnp.mean(dxn * xn, axis=-1, keepdims=True))
        p_g = jnp.sum(dy * xn, axis=0, keepdims=True)
        p_l = jnp.zeros((1, LANES), F32) + jnp.sum(e * e) * (0.5 / d)
        first = (i == 0) & (b == 0)

        @pl.when(first)
        def _():
            dg_ref[...] = p_g
            l_ref[...] = p_l

        @pl.when(jnp.logical_not(first))
        def _():
            dg_ref[...] += p_g
            l_ref[...] += p_l

    row = pl.BlockSpec((None, ts, d), lambda b, i: (b, i, 0))
    one = pl.BlockSpec((1, d), lambda b, i: (0, 0))
    return pl.pallas_call(
        body, name=name, grid=(bsz, s // ts),
        in_specs=[row, one, row],
        out_specs=[row, one, pl.BlockSpec((1, LANES), lambda b, i: (0, 0))],
        out_shape=[jax.ShapeDtypeStruct(x.shape, F32), jax.ShapeDtypeStruct((1, d), F32),
                   jax.ShapeDtypeStruct((1, LANES), F32)],
        compiler_params=_cp(("arbitrary", "arbitrary")),
    )(x, g, tgt)


def _gmlp_chunk(u_raw, v_raw, ln_g, ln_b, w, bs_t, out_g):
    c = u_raw.shape[0]
    u, v = _gelu(u_raw), _gelu(v_raw)
    tril = lax.broadcasted_iota(jnp.int32, (c, c), 0) >= lax.broadcasted_iota(jnp.int32, (c, c), 1)
    ys = []
    for h in range(GM_HEADS):
        sl = slice(h * GM_HD, (h + 1) * GM_HD)
        vh = v[:, sl]
        xc = vh - jnp.mean(vh, axis=-1, keepdims=True)
        vn = xc * lax.rsqrt(jnp.mean(xc * xc, axis=-1, keepdims=True) + LN_EPS) * ln_g[:, sl] + ln_b[:, sl]
        gate = _nn(jnp.where(tril, w[h], 0.0), vn) + bs_t[:, h:h + 1]
        ys.append(u[:, sl] * gate)
    return _rms(jnp.concatenate(ys, axis=1), out_g)


def _gmlp_specs(bsz, nc):
    seg = lambda off: pl.BlockSpec((None, CHUNK, GMW), lambda b, c: (b, c, off // GMW))
    full = lambda shape: pl.BlockSpec(shape, lambda b, c: (0,) * len(shape))
    par = [full((1, GMW)), full((1, GMW)), full((GM_HEADS, CHUNK, CHUNK)), full((CHUNK, GM_HEADS)), full((1, GMW))]
    return seg, full, par


def gmlp_fwd(p, prm, name):
    bsz, s, _ = p.shape
    nc = s // CHUNK
    seg, _, par = _gmlp_specs(bsz, nc)

    def body(u_ref, v_ref, lg, lb, w, bt, og, o_ref):
        o_ref[...] = _gmlp_chunk(u_ref[...], v_ref[...], lg[...], lb[...], w[...], bt[...], og[...]).astype(o_ref.dtype)

    return pl.pallas_call(
        body, name=name, grid=(bsz, nc),
        in_specs=[seg(OFF_U), seg(OFF_V)] + par,
        out_specs=pl.BlockSpec((None, CHUNK, GMW), lambda b, c: (b, c, 0)),
        out_shape=jax.ShapeDtypeStruct((bsz, s, GMW), BF16),
        compiler_params=_cp(("parallel", "parallel")),
    )(p, p, *prm)


def _accumulate(first, refs, vals):
    @pl.when(first)
    def _():
        for r, v in zip(refs, vals):
            r[...] = v

    @pl.when(jnp.logical_not(first))
    def _():
        for r, v in zip(refs, vals):
            r[...] += v


def gmlp_bwd(p, dmix, prm, name):
    bsz, s, _ = p.shape
    nc = s // CHUNK
    seg, full, par = _gmlp_specs(bsz, nc)

    def body(u_ref, v_ref, do_ref, lg, lb, w, bt, og, du_ref, dv_ref, *dpar):
        first = (pl.program_id(0) == 0) & (pl.program_id(1) == 0)
        _, vjp = jax.vjp(_gmlp_chunk, u_ref[...], v_ref[...], lg[...], lb[...], w[...], bt[...], og[...])
        gr = vjp(do_ref[...])
        du_ref[...] = gr[0].astype(du_ref.dtype)
        dv_ref[...] = gr[1].astype(dv_ref.dtype)
        _accumulate(first, dpar, gr[2:])

    out_seg = pl.BlockSpec((None, CHUNK, GMW), lambda b, c: (b, c, 0))
    return pl.pallas_call(
        body, name=name, grid=(bsz, nc),
        in_specs=[seg(OFF_U), seg(OFF_V), out_seg] + par,
        out_specs=[out_seg, out_seg] + par,
        out_shape=[jax.ShapeDtypeStruct((bsz, s, GMW), BF16)] * 2 + [jax.ShapeDtypeStruct(x.shape, F32) for x in prm],
        compiler_params=_cp(("arbitrary", "arbitrary")),
    )(p, p, dmix, *prm)


def _attn_block(q, kp, kc, vp, vc, sinks, out_g, has_prev):
    w = q.shape[0]
    k2 = jnp.concatenate([kp, kc], axis=0)
    v2 = jnp.concatenate([vp, vc], axis=0)
    qi = lax.broadcasted_iota(jnp.int32, (w, 2 * w), 0)
    kj = lax.broadcasted_iota(jnp.int32, (w, 2 * w), 1)
    diff = qi + w - kj
    grp = ATT_HEADS // ATT_KV
    valid = (diff >= 0) & (diff < w) & ((kj >= w) | has_prev)
    valid = jnp.concatenate([valid] * grp, axis=0)
    outs = []
    for kv in range(ATT_KV):
        kh = k2[:, kv * ATT_HD:(kv + 1) * ATT_HD]
        vh = v2[:, kv * ATT_HD:(kv + 1) * ATT_HD]
        heads = range(kv * grp, (kv + 1) * grp)
        qs = jnp.concatenate([q[:, h * ATT_HD:(h + 1) * ATT_HD] for h in heads], axis=0)
        sink = jnp.concatenate([jnp.broadcast_to(sinks[:, h:h + 1], (w, 1)) for h in heads], axis=0)
        sc = jnp.where(valid, _nt(qs, kh) * (ATT_HD ** -0.5), NEG)
        m = jnp.maximum(jnp.max(sc, axis=-1, keepdims=True), sink)
        e = jnp.exp(sc - m)
        pr = e / (jnp.sum(e, axis=-1, keepdims=True) + jnp.exp(sink - m))
        o = _nn(pr, vh)
        outs += [o[gi * w:(gi + 1) * w] for gi in range(grp)]
    return _rms(jnp.concatenate(outs, axis=1), out_g)


ATT_QB = 4


def _attn_tiles(s):
    qb = min(ATT_QB, s // WINDOW)
    return qb, qb * WINDOW, s // (qb * WINDOW)


def attn_fwd(p, sinks, out_g, name):
    bsz, s, _ = p.shape
    qb, rows, steps = _attn_tiles(s)

    def body(q_ref, kp_ref, kc_ref, vp_ref, vc_ref, s_ref, g_ref, o_ref):
        n = pl.program_id(1)
        for w in range(qb):
            sl = pl.ds(w * WINDOW, WINDOW)
            before = pl.ds((w - 1) * WINDOW, WINDOW)
            kp = kp_ref[...] if w == 0 else kc_ref[before, :]
            vp = vp_ref[...] if w == 0 else vc_ref[before, :]
            o_ref[sl, :] = _attn_block(q_ref[sl, :], kp, kc_ref[sl, :], vp, vc_ref[sl, :], s_ref[...], g_ref[...],
                                       (n > 0) if w == 0 else True).astype(o_ref.dtype)

    cur = lambda off: pl.BlockSpec((None, rows, KVW), lambda b, n: (b, n, off // KVW))
    prev = lambda off: pl.BlockSpec((None, WINDOW, KVW), lambda b, n: (b, jnp.maximum(n * qb - 1, 0), off // KVW))
    return pl.pallas_call(
        body, name=name, grid=(bsz, steps),
        in_specs=[pl.BlockSpec((None, rows, ATW), lambda b, n: (b, n, OFF_Q // ATW)),
                  prev(OFF_K), cur(OFF_K), prev(OFF_VV), cur(OFF_VV),
                  pl.BlockSpec((1, ATT_HEADS), lambda b, n: (0, 0)), pl.BlockSpec((1, ATW), lambda b, n: (0, 0))],
        out_specs=pl.BlockSpec((None, rows, ATW), lambda b, n: (b, n, 0)),
        out_shape=jax.ShapeDtypeStruct((bsz, s, ATW), BF16),
        compiler_params=_cp(("parallel", "parallel")),
    )(p, p, p, p, p, sinks, out_g)


def attn_bwd(p, dmix, sinks, out_g, name):
    bsz, s, _ = p.shape
    qb, rows, steps = _attn_tiles(s)
    last = pl.ds(rows - WINDOW, WINDOW)

    def body(q_ref, kp_ref, kc_ref, vp_ref, vc_ref, do_ref, s_ref, g_ref,
             dq_ref, dk_ref, dv_ref, ds_ref, dg_ref, ck, cv):
        b, n = pl.program_id(0), pl.program_id(1)

        @pl.when(n == 0)
        def _():
            ck[...] = jnp.zeros_like(ck)
            cv[...] = jnp.zeros_like(cv)

        @pl.when(n < steps)
        def _():
            grads = []
            for w in range(qb):
                sl = pl.ds(w * WINDOW, WINDOW)
                before = pl.ds((w - 1) * WINDOW, WINDOW)
                kp = kp_ref[...] if w == 0 else kc_ref[before, :]
                vp = vp_ref[...] if w == 0 else vc_ref[before, :]
                fn = functools.partial(_attn_block, has_prev=(n > 0) if w == 0 else True)
                _, vjp = jax.vjp(fn, q_ref[sl, :], kp, kc_ref[sl, :], vp, vc_ref[sl, :], s_ref[...], g_ref[...])
                grads.append(vjp(do_ref[sl, :]))
                dq_ref[sl, :] = grads[-1][0].astype(dq_ref.dtype)
            dk_ref[...] = ck[...].astype(dk_ref.dtype)
            dv_ref[...] = cv[...].astype(dv_ref.dtype)
            dk_ref[last, :] = (ck[last, :] + grads[0][1]).astype(dk_ref.dtype)
            dv_ref[last, :] = (cv[last, :] + grads[0][3]).astype(dv_ref.dtype)
            for w in range(qb):
                sl = pl.ds(w * WINDOW, WINDOW)
                ck[sl, :] = grads[w][2] + (grads[w + 1][1] if w + 1 < qb else 0.0)
                cv[sl, :] = grads[w][4] + (grads[w + 1][3] if w + 1 < qb else 0.0)
            dsk = functools.reduce(lambda u, v: u + v, [g[5] for g in grads])
            dgg = functools.reduce(lambda u, v: u + v, [g[6] for g in grads])
            _accumulate((b == 0) & (n == 0), (ds_ref, dg_ref), (dsk, dgg))

        @pl.when(n == steps)
        def _():
            dk_ref[...] = ck[...].astype(dk_ref.dtype)
            dv_ref[...] = cv[...].astype(dv_ref.dtype)

    at = lambda n: jnp.minimum(n, steps - 1)
    cur = lambda off: pl.BlockSpec((None, rows, KVW), lambda b, n: (b, at(n), off // KVW))
    prev = lambda off: pl.BlockSpec((None, WINDOW, KVW), lambda b, n: (b, jnp.maximum(at(n) * qb - 1, 0), off // KVW))
    kv_out = pl.BlockSpec((None, rows, KVW), lambda b, n: (b, jnp.maximum(n - 1, 0), 0))
    return pl.pallas_call(
        body, name=name, grid=(bsz, steps + 1),
        in_specs=[pl.BlockSpec((None, rows, ATW), lambda b, n: (b, at(n), OFF_Q // ATW)),
                  prev(OFF_K), cur(OFF_K), prev(OFF_VV), cur(OFF_VV),
                  pl.BlockSpec((None, rows, ATW), lambda b, n: (b, at(n), GMW // ATW)),
                  pl.BlockSpec((1, ATT_HEADS), lambda b, n: (0, 0)), pl.BlockSpec((1, ATW), lambda b, n: (0, 0))],
        out_specs=[pl.BlockSpec((None, rows, ATW), lambda b, n: (b, at(n), 0)), kv_out, kv_out,
                   pl.BlockSpec((1, ATT_HEADS), lambda b, n: (0, 0)), pl.BlockSpec((1, ATW), lambda b, n: (0, 0))],
        out_shape=[jax.ShapeDtypeStruct((bsz, s, ATW), BF16), jax.ShapeDtypeStruct((bsz, s, KVW), BF16),
                   jax.ShapeDtypeStruct((bsz, s, KVW), BF16), jax.ShapeDtypeStruct((1, ATT_HEADS), F32),
                   jax.ShapeDtypeStruct((1, ATW), F32)],
        scratch_shapes=[pltpu.VMEM((rows, KVW), F32), pltpu.VMEM((rows, KVW), F32)],
        compiler_params=_cp(("arbitrary", "arbitrary")),
    )(p, p, p, p, p, dmix, sinks, out_g)


CONV_CT = 256


def _shift_down(x, j):
    if j == 0:
        return x
    rows = lax.broadcasted_iota(jnp.int32, x.shape, 0)
    return jnp.where(rows >= j, pltpu.roll(x, j, 0), 0.0)


def _shift_up(x, j):
    if j == 0:
        return x
    s = x.shape[0]
    rows = lax.broadcasted_iota(jnp.int32, x.shape, 0)
    return jnp.where(rows < s - j, pltpu.roll(x, s - j, 0), 0.0)


def conv_fwd(p, w, bias, name):
    bsz, s, _ = p.shape

    def body(x_ref, w_ref, b_ref, o_ref):
        xv, wv = x_ref[...], w_ref[...]
        pre = b_ref[...] + sum(wv[k:k + 1, :] * _shift_down(xv, CONV_K - 1 - k) for k in range(CONV_K))
        o_ref[...] = _silu(pre)

    blk = pl.BlockSpec((None, s, CONV_CT), lambda b, j: (b, 0, j))
    src = pl.BlockSpec((None, s, CONV_CT), lambda b, j: (b, 0, OFF_XBC // CONV_CT + j))
    return pl.pallas_call(
        body, name=name, grid=(bsz, CCH // CONV_CT),
        in_specs=[src, pl.BlockSpec((CONV_K, CONV_CT), lambda b, j: (0, j)), pl.BlockSpec((1, CONV_CT), lambda b, j: (0, j))],
        out_specs=blk, out_shape=jax.ShapeDtypeStruct((bsz, s, CCH), F32),
        compiler_params=_cp(("parallel", "parallel")),
    )(p, w, bias)


def conv_bwd(p, dxc, w, bias, name):
    bsz, s, _ = p.shape

    def body(x_ref, d_ref, w_ref, b_ref, dx_ref, dw_ref, db_ref):
        b = pl.program_id(1)
        xv, wv = x_ref[...], w_ref[...]
        xs = [_shift_down(xv, CONV_K - 1 - k) for k in range(CONV_K)]
        pre = b_ref[...] + sum(wv[k:k + 1, :] * xs[k] for k in range(CONV_K))
        sg = 1.0 / (1.0 + jnp.exp(-pre))
        dpre = d_ref[...] * (sg * (1.0 + pre * (1.0 - sg)))
        dx_ref[...] = sum(wv[k:k + 1, :] * _shift_up(dpre, CONV_K - 1 - k) for k in range(CONV_K)).astype(dx_ref.dtype)
        p_w = jnp.concatenate([jnp.sum(dpre * xs[k], axis=0, keepdims=True) for k in range(CONV_K)], axis=0)
        p_b = jnp.sum(dpre, axis=0, keepdims=True)
        _accumulate(b == 0, (dw_ref, db_ref), (p_w, p_b))

    blk = pl.BlockSpec((None, s, CONV_CT), lambda j, b: (b, 0, j))
    src = pl.BlockSpec((None, s, CONV_CT), lambda j, b: (b, 0, OFF_XBC // CONV_CT + j))
    wsp = pl.BlockSpec((CONV_K, CONV_CT), lambda j, b: (0, j))
    bsp = pl.BlockSpec((1, CONV_CT), lambda j, b: (0, j))
    return pl.pallas_call(
        body, name=name, grid=(CCH // CONV_CT, bsz),
        in_specs=[src, blk, wsp, bsp], out_specs=[blk, wsp, bsp],
        out_shape=[jax.ShapeDtypeStruct((bsz, s, CCH), BF16), jax.ShapeDtypeStruct((CONV_K, CCH), F32),
                   jax.ShapeDtypeStruct((1, CCH), F32)],
        compiler_params=_cp(("parallel", "arbitrary")),
    )(p, dxc, w, bias)


def _ssd_consts():
    c = CHUNK
    r = lax.broadcasted_iota(jnp.int32, (c, c), 0)
    q = lax.broadcasted_iota(jnp.int32, (c, c), 1)
    hrow = lax.broadcasted_iota(jnp.int32, (LANES, SSW), 0)
    hcol = lax.broadcasted_iota(jnp.int32, (LANES, SSW), 1) // SSM_HD
    expand = (hrow == hcol).astype(F32)
    return expand, (r >= q).astype(F32), (r <= q).astype(F32), r >= q


def _ssd_chunk(xc, dtr, z, prev_t, dt_bias, a_log, d_skip, norm_g):
    c = xc.shape[0]
    expand, tril1, triu1, causal = _ssd_consts()
    xs, bm, cm = xc[:, :SSW], xc[:, SSW:SSW + BCW], xc[:, SSW + BCW:]
    dt = _softplus(dtr + dt_bias)
    da = dt * (-jnp.exp(a_log))
    a_cs = _hdot(tril1, da)
    a_cs_t = _hdot(da.T, triu1)
    dt_e = _hdot(dt, expand)
    acs_e = _hdot(a_cs, expand)
    alast_e = acs_e[c - 1:c, :]
    dsk_e = _hdot(jnp.broadcast_to(d_skip, (8, LANES)), expand)[0:1, :]
    xdt = xs * dt_e
    hg = SSM_HEADS // SSM_GROUPS
    ys, new_t = [], []
    for g in range(SSM_GROUPS):
        bg = bm[:, g * SSM_STATE:(g + 1) * SSM_STATE]
        cg = cm[:, g * SSM_STATE:(g + 1) * SSM_STATE]
        sl = slice(g * GRW, (g + 1) * GRW)
        cb = _nt(cg, bg)
        xdt_g = xdt[:, sl]
        st = _tn(bg, xdt_g * jnp.exp(alast_e[:, sl] - acs_e[:, sl]))
        new_t.append(prev_t[:, sl] * jnp.exp(alast_e[:, sl]) + st)
        y_off = _nn(cg, prev_t[:, sl]) * jnp.exp(acs_e[:, sl])
        yd = []
        low = lax.broadcasted_iota(jnp.int32, (c, LANES), 1) < SSM_HD
        for pair in range(hg // 2):
            xp = xdt_g[:, pair * LANES:(pair + 1) * LANES]
            acc = None
            for side, xh in enumerate((jnp.where(low, xp, 0.0), jnp.where(low, 0.0, xp))):
                h = g * hg + 2 * pair + side
                decay = jnp.exp(jnp.where(causal, a_cs[:, h:h + 1] - a_cs_t[h:h + 1, :], NEG))
                part = _nn(cb * decay, xh)
                acc = part if acc is None else acc + part
            yd.append(acc)
        ys.append(jnp.concatenate(yd, axis=1) + y_off)
    y = (jnp.concatenate(ys, axis=1) + xs * dsk_e) * _silu(z)
    yn = [y[:, g * GRW:(g + 1) * GRW] * lax.rsqrt(jnp.mean(jnp.square(y[:, g * GRW:(g + 1) * GRW]), axis=-1, keepdims=True) + EPS)
          for g in range(SSM_GROUPS)]
    return jnp.concatenate(yn, axis=1) * norm_g, jnp.concatenate(new_t, axis=1)


def ssd_fwd(xc, p, prm, name):
    bsz, s, _ = p.shape
    nc = s // CHUNK

    def body(xc_ref, dt_ref, *rest):
        z_refs, (db, al, dk, ng, o_ref, st_ref, state) = rest[:SSW // ZB], rest[SSW // ZB:]
        @pl.when(pl.program_id(0) == 0)
        def _():
            state[...] = jnp.zeros_like(state)

        for b in range(bsz):
            prev = state[b]
            st_ref[b, 0] = prev
            zb = jnp.concatenate([r[b] for r in z_refs], axis=1)
            out, new = _ssd_chunk(xc_ref[b], dt_ref[b], zb, prev, db[...], al[...], dk[...], ng[...])
            o_ref[b] = out.astype(o_ref.dtype)
            state[b] = new

    vec = pl.BlockSpec((1, LANES), lambda c: (0, 0))
    return pl.pallas_call(
        body, name=name, grid=(nc,),
        in_specs=[pl.BlockSpec((bsz, CHUNK, CCH), lambda c: (0, c, 0)),
                  pl.BlockSpec((bsz, CHUNK, LANES), lambda c: (0, c, OFF_DT // LANES)),
                  *[pl.BlockSpec((bsz, CHUNK, ZB), lambda c, i=i: (0, c, OFF_Z // ZB + i)) for i in range(SSW // ZB)],
                  vec, vec, vec, pl.BlockSpec((1, SSW), lambda c: (0, 0))],
        out_specs=[pl.BlockSpec((bsz, CHUNK, SSW), lambda c: (0, c, 0)),
                   pl.BlockSpec((bsz, 1, SSM_STATE, SSW), lambda c: (0, c, 0, 0))],
        out_shape=[jax.ShapeDtypeStruct((bsz, s, SSW), BF16), jax.ShapeDtypeStruct((bsz, nc, SSM_STATE, SSW), F32)],
        scratch_shapes=[pltpu.VMEM((bsz, SSM_STATE, SSW), F32)],
        compiler_params=_cp(("arbitrary",)),
    )(xc, p, *([p] * (SSW // ZB)), *prm)


def ssd_bwd(xc, p, states, dmix, prm, name):
    bsz, s, _ = p.shape
    nc = s // CHUNK

    def body(xc_ref, dt_ref, *rest):
        z_refs, (st_ref, do_ref, db, al, dk, ng, dxc_ref, ddt_ref, dz_ref) = rest[:SSW // ZB], rest[SSW // ZB:SSW // ZB + 9]
        rest = rest[SSW // ZB + 9:]
        dpar, dstate = rest[:4], rest[4]
        c = pl.program_id(0)

        @pl.when(c == 0)
        def _():
            dstate[...] = jnp.zeros_like(dstate)

        dpars = None
        for b in range(bsz):
            zb = jnp.concatenate([r[b] for r in z_refs], axis=1)
            _, vjp = jax.vjp(_ssd_chunk, xc_ref[b], dt_ref[b], zb, st_ref[b, 0], db[...], al[...], dk[...], ng[...])
            gr = vjp((do_ref[b], dstate[b]))
            dxc_ref[b] = gr[0]
            ddt_ref[b] = gr[1].astype(ddt_ref.dtype)
            dz_ref[b] = gr[2].astype(dz_ref.dtype)
            dstate[b] = gr[3]
            dpars = gr[4:] if dpars is None else [u + v for u, v in zip(dpars, gr[4:])]
        _accumulate(c == 0, dpar, dpars)

    rv = lambda c: nc - 1 - c
    vec = pl.BlockSpec((1, LANES), lambda c: (0, 0))
    ngs = pl.BlockSpec((1, SSW), lambda c: (0, 0))
    return pl.pallas_call(
        body, name=name, grid=(nc,),
        in_specs=[pl.BlockSpec((bsz, CHUNK, CCH), lambda c: (0, rv(c), 0)),
                  pl.BlockSpec((bsz, CHUNK, LANES), lambda c: (0, rv(c), OFF_DT // LANES)),
                  *[pl.BlockSpec((bsz, CHUNK, ZB), lambda c, i=i: (0, rv(c), OFF_Z // ZB + i)) for i in range(SSW // ZB)],
                  pl.BlockSpec((bsz, 1, SSM_STATE, SSW), lambda c: (0, rv(c), 0, 0)),
                  pl.BlockSpec((bsz, CHUNK, SSW), lambda c: (0, rv(c), (GMW + ATW) // SSW)),
                  vec, vec, vec, ngs],
        out_specs=[pl.BlockSpec((bsz, CHUNK, CCH), lambda c: (0, rv(c), 0)),
                   pl.BlockSpec((bsz, CHUNK, LANES), lambda c: (0, rv(c), 0)),
                   pl.BlockSpec((bsz, CHUNK, SSW), lambda c: (0, rv(c), 0)),
                   vec, vec, vec, ngs],
        out_shape=[jax.ShapeDtypeStruct((bsz, s, CCH), F32), jax.ShapeDtypeStruct((bsz, s, LANES), BF16),
                   jax.ShapeDtypeStruct((bsz, s, SSW), BF16)] + [jax.ShapeDtypeStruct((1, LANES), F32)] * 3
                  + [jax.ShapeDtypeStruct((1, SSW), F32)],
        scratch_shapes=[pltpu.VMEM((bsz, SSM_STATE, SSW), F32)],
        compiler_params=_cp(("arbitrary",)),
    )(xc, p, *([p] * (SSW // ZB)), states, dmix, *prm)


def _rows2d(a):
    return a.reshape(-1, a.shape[-1])


def _ew_tile(r, c):
    t = r
    while t * c > (1 << 20) and t % 16 == 0:
        t //= 2
    return t


def add_pair(g, theirs, core, name):
    k, r, c = g.shape
    h = r // 2
    tr = _ew_tile(h, c)
    nb = h // tr

    def body(c_ref, a_ref, b_ref, o_ref, ob_ref):
        s = a_ref[...] + b_ref[...]
        o_ref[...] = s
        ob_ref[...] = s.astype(ob_ref.dtype)

    blk = pl.BlockSpec((None, tr, c), lambda kk, i, cr: (kk, i, 0))
    return pl.pallas_call(
        body, name=name,
        grid_spec=pltpu.PrefetchScalarGridSpec(
            num_scalar_prefetch=1, grid=(k, nb),
            in_specs=[pl.BlockSpec((None, tr, c), lambda kk, i, cr: (kk, cr[0] * nb + i, 0)), blk],
            out_specs=[blk, blk]),
        out_shape=[jax.ShapeDtypeStruct(theirs.shape, F32), jax.ShapeDtypeStruct(theirs.shape, BF16)],
        compiler_params=_cp(("parallel", "parallel")),
    )(core.reshape(1).astype(jnp.int32), g, theirs)


def sum_own_recv(sums, recv, chip, core, name):
    _, h, c = sums.shape
    tr = _ew_tile(h, c)
    nb = h // tr

    def body(k_ref, o_ref, r_ref, out_ref):
        s = o_ref[...]
        for j in range(3):
            s = s + r_ref[j].astype(F32)
        out_ref[...] = s

    return pl.pallas_call(
        body, name=name,
        grid_spec=pltpu.PrefetchScalarGridSpec(
            num_scalar_prefetch=1, grid=(nb,),
            in_specs=[pl.BlockSpec((None, tr, c), lambda i, kr: (kr[0], i, 0)),
                      pl.BlockSpec((3, tr, c), lambda i, kr: (0, i, 0))],
            out_specs=pl.BlockSpec((tr, c), lambda i, kr: (kr[1] * nb + i, 0))),
        out_shape=jax.ShapeDtypeStruct((2 * h, c), F32),
        compiler_params=_cp(("parallel",)),
    )(jnp.stack([chip, core]).astype(jnp.int32), sums, recv)


def _adam_math(w, m, v, g):
    mn = ADAM_B1 * m + (1.0 - ADAM_B1) * g
    vn = ADAM_B2 * v + (1.0 - ADAM_B2) * (g * g)
    mh = mn / (1.0 - ADAM_B1 ** ADAM_STEP)
    vh = vn / (1.0 - ADAM_B2 ** ADAM_STEP)
    return -ADAM_LR * (mh / (jnp.sqrt(vh) + ADAM_EPS) + ADAM_WD * w), mn, vn


def adamw_minor_rows(w, m, v, g, name):
    r, nl, c = w.shape
    tr = max(t for t in range(1, r + 1) if r % t == 0 and t * nl * c <= (1 << 19))

    def body(w_ref, m_ref, v_ref, g_ref, d_ref, mo_ref, vo_ref):
        d_ref[...], mo_ref[...], vo_ref[...] = _adam_math(w_ref[...], m_ref[...], v_ref[...], g_ref[...])

    blk = pl.BlockSpec((tr, nl, c), lambda i: (i, 0, 0))
    return pl.pallas_call(
        body, name=name, grid=(r // tr,), in_specs=[blk] * 4, out_specs=[blk] * 3,
        out_shape=[jax.ShapeDtypeStruct(w.shape, F32)] * 3, compiler_params=_cp(("parallel",)),
    )(w, m, v, g)


def adamw_layer(w, m, v, g, layer, prev, name):
    nl, r, c = w.shape
    tr = _ew_tile(r, c * 2)

    def body(w_ref, m_ref, v_ref, g_ref, *rest):
        go_ref, d_ref, mo_ref, vo_ref = rest[-4:]
        gv = g_ref[...]
        dl, mn, vn = _adam_math(w_ref[...], m_ref[...], v_ref[...], gv)
        go_ref[...] = gv
        d_ref[...] = dl
        mo_ref[...] = mn
        vo_ref[...] = vn

    lay = pl.BlockSpec((None, tr, c), lambda i: (layer, i, 0))
    n_prev = 0 if prev is None else 4
    return pl.pallas_call(
        body, name=name, grid=(r // tr,),
        in_specs=[lay, lay, lay, pl.BlockSpec((tr, c), lambda i: (i, 0))] + [_ANY] * n_prev,
        out_specs=[lay] * 4, out_shape=[jax.ShapeDtypeStruct(w.shape, F32)] * 4,
        input_output_aliases={4 + i: i for i in range(n_prev)},
        compiler_params=_cp(("parallel",)),
    )(w, m, v, g, *(prev or ()))


def sum_devices(parts, name):
    n, r, c = parts.shape
    tr = _ew_tile(r, c * n)

    def body(p_ref, o_ref):
        s = p_ref[0]
        for j in range(1, n):
            s = s + p_ref[j]
        o_ref[...] = s

    return pl.pallas_call(
        body, name=name, grid=(r // tr,),
        in_specs=[pl.BlockSpec((n, tr, c), lambda i: (0, i, 0))],
        out_specs=pl.BlockSpec((tr, c), lambda i: (i, 0)),
        out_shape=jax.ShapeDtypeStruct((r, c), F32),
        compiler_params=_cp(("parallel",)),
    )(parts)


def adamw(w, m, v, g, name):
    r, c = w.shape
    tr = _ew_tile(r, c * 2)

    def body(w_ref, m_ref, v_ref, g_ref, d_ref, mo_ref, vo_ref):
        gv = g_ref[...]
        mn = ADAM_B1 * m_ref[...] + (1.0 - ADAM_B1) * gv
        vn = ADAM_B2 * v_ref[...] + (1.0 - ADAM_B2) * (gv * gv)
        mh = mn / (1.0 - ADAM_B1 ** ADAM_STEP)
        vh = vn / (1.0 - ADAM_B2 ** ADAM_STEP)
        d_ref[...] = -ADAM_LR * (mh / (jnp.sqrt(vh) + ADAM_EPS) + ADAM_WD * w_ref[...])
        mo_ref[...] = mn
        vo_ref[...] = vn

    blk = pl.BlockSpec((tr, c), lambda i: (i, 0))
    return pl.pallas_call(
        body, name=name, grid=(r // tr,), in_specs=[blk] * 4, out_specs=[blk] * 3,
        out_shape=[jax.ShapeDtypeStruct((r, c), F32)] * 3,
        compiler_params=_cp(("parallel",)),
    )(w, m, v, g)


def _place():
    x, y, c = lax.axis_index("x"), lax.axis_index("y"), lax.axis_index("c")
    chips = [(1 - x, y), (x, 1 - y), (1 - x, 1 - y)]
    return x, y, c, chips


def all_gather_small(v, name):
    r, w = v.shape

    def body(x_ref, out_ref, send_sems, recv_sems, local_sem):
        x, y, c, chips = _place()
        me, sibling = (x, y, c), (x, y, 1 - c)

        def rows(px, py, pc):
            return out_ref.at[pl.ds((4 * px + 2 * py + pc) * r, r), :]

        def copy(k, block, to, src=None):
            return pltpu.make_async_remote_copy(
                src_ref=rows(*block) if src is None else src, dst_ref=rows(*block),
                send_sem=send_sems.at[k], recv_sem=recv_sems.at[k], device_id=to, device_id_type=MESH)

        mine = pltpu.make_async_copy(x_ref, rows(*me), local_sem)
        mine.start()
        first = [copy(0, me, sibling, src=x_ref)]
        first += [copy(1 + j, me, (*chip, c), src=x_ref) for j, chip in enumerate(chips)]
        for cp in first:
            cp.start()
        passed = [copy(4 + j, (*chip, c), sibling) for j, chip in enumerate(chips)]
        for j, chip in enumerate(chips):
            copy(1 + j, (*chip, c), me).wait_recv()
            passed[j].start()
        copy(0, sibling, me).wait_recv()
        for j, chip in enumerate(chips):
            copy(4 + j, (*chip, 1 - c), me).wait_recv()
        for cp in first + passed:
            cp.wait_send()
        mine.wait()

    out = pl.pallas_call(
        body, name=name, out_shape=jax.ShapeDtypeStruct((8 * r, w), v.dtype),
        in_specs=[pl.BlockSpec(memory_space=pltpu.VMEM)], out_specs=pl.BlockSpec(memory_space=pltpu.VMEM),
        scratch_shapes=[pltpu.SemaphoreType.DMA((7,)), pltpu.SemaphoreType.DMA((7,)), pltpu.SemaphoreType.DMA],
        compiler_params=pltpu.CompilerParams(vmem_limit_bytes=VMEM_LIMIT),
    )(v)
    return out.reshape(8, r, w)


_HBM = pl.BlockSpec(memory_space=pltpu.HBM)


_SEM = pl.BlockSpec(memory_space=pltpu.SEMAPHORE)
_ANY = pl.BlockSpec(memory_space=pl.ANY)
_EFFECT = pltpu.SideEffectType.DATAFLOW_SIDE_EFFECTING


def _hbm(a):
    return pltpu.with_memory_space_constraint(a, pltpu.HBM)


def split_copy_start(srcs, land_shapes, copies, after, name):
    n, nl = len(srcs), len(land_shapes)
    n_after = 0 if after is None else 1
    ncopy = [0]

    def body(*refs):
        ins, lands = refs[:n], refs[n:n + nl]
        send_sems, recv_sems = refs[n + nl + n_after], refs[n + nl + n_after + 1]
        token = refs[-1]
        x, y, c, chips = _place()
        for k, (src, dst, to) in enumerate(copies(x, y, c, chips, ins, lands)):
            pltpu.make_async_remote_copy(src_ref=src, dst_ref=dst, send_sem=send_sems.at[k], recv_sem=recv_sems.at[k],
                                         device_id=to, device_id_type=MESH).start()
        token[...] = jnp.zeros_like(token)

    ncopy[0] = len(copies(0, 0, 0, [(1, 0), (0, 1), (1, 1)], [None] * n, [None] * nl, count_only=True))
    k = ncopy[0]
    lands = [_hbm(lax.empty(s.shape, s.dtype)) for s in land_shapes]
    res = pl.pallas_call(
        body, name=name,
        out_shape=(pltpu.SemaphoreType.DMA((k,)), pltpu.SemaphoreType.DMA((k,)))
        + tuple(pltpu.HBM(s.shape, s.dtype) for s in srcs) + tuple(pltpu.HBM(s.shape, s.dtype) for s in land_shapes)
        + (jax.ShapeDtypeStruct((8, LANES), F32),),
        in_specs=[_HBM] * (n + nl) + [_ANY] * n_after,
        out_specs=(_SEM, _SEM) + (_HBM,) * (n + nl) + (pl.BlockSpec(memory_space=pltpu.VMEM),),
        input_output_aliases={i: 2 + i for i in range(n + nl)},
        compiler_params=pltpu.CompilerParams(has_side_effects=_EFFECT),
    )(*[_hbm(s) for s in srcs], *lands, *([after] if n_after else []))
    return res[0], res[1], list(res[2:2 + n]), list(res[2 + n:2 + n + nl]), res[-1]


def split_copy_wait(send_sems, recv_sems, srcs, lands, copies, after, name):
    n, nl = len(srcs), len(lands)

    def body(*refs):
        ins, lnd = refs[:n], refs[n:n + nl]
        ss, rs = refs[n + nl], refs[n + nl + 1]
        x, y, c, chips = _place()
        for k, (src, dst, to) in enumerate(copies(x, y, c, chips, ins, lnd, receive=True)):
            cp = pltpu.make_async_remote_copy(src_ref=src, dst_ref=dst, send_sem=ss.at[k], recv_sem=rs.at[k],
                                              device_id=to, device_id_type=MESH)
            cp.wait_send()
            cp.wait_recv()

    res = pl.pallas_call(
        body, name=name,
        out_shape=tuple(pltpu.HBM(s.shape, s.dtype) for s in srcs) + tuple(pltpu.HBM(s.shape, s.dtype) for s in lands),
        in_specs=[_HBM] * (n + nl) + [_SEM, _SEM, _ANY], out_specs=(_HBM,) * (n + nl),
        input_output_aliases={i: i for i in range(n + nl)},
        compiler_params=pltpu.CompilerParams(has_side_effects=_EFFECT),
    )(*srcs, *lands, send_sems, recv_sems, after)
    return list(res[:n]), list(res[n:])


def _gather_copies(x, y, c, chips, ins, lands, receive=False, count_only=False):
    out = []
    for i in range(len(ins)):
        for cx, cy in chips:
            if count_only:
                out.append(None)
                continue
            h = ins[i].shape[0] // 2
            rows = pl.ds(c * h, h)
            k_dst = (2 * cx + cy) if receive else (2 * x + y)
            out.append((ins[i].at[rows, :], lands[i].at[k_dst, rows, :], (cx, cy, c)))
    for i in range(len(ins)):
        out.append(None if count_only else (ins[i], lands[i].at[2 * x + y], (x, y, 1 - c)))
    return out


def _swap_copies(x, y, c, chips, ins, lands, receive=False, count_only=False):
    out = []
    for i in range(len(ins)):
        if count_only:
            out.append(None)
            continue
        h = ins[i].shape[1] // 2
        out.append((ins[i].at[:, pl.ds((1 - c) * h, h), :], lands[i], (x, y, 1 - c)))
    return out


def _scatter_copies(x, y, c, chips, ins, lands, receive=False, count_only=False):
    out = []
    for i in range(len(ins)):
        for j, (cx, cy) in enumerate(chips):
            if count_only:
                out.append(None)
                continue
            out.append((ins[i].at[2 * cx + cy], lands[i].at[j], (cx, cy, c)))
    return out


def forward_halves(lands, name):
    n = len(lands)

    def body(*refs):
        ins, outs = refs[:n], refs[n:2 * n]
        send_sems, recv_sems = refs[2 * n:]
        x, y, c, chips = _place()
        sibling = (x, y, 1 - c)
        sent = []
        for i in range(n):
            h = ins[i].shape[1] // 2
            for j, (cx, cy) in enumerate(chips):
                blk = ins[i].at[2 * cx + cy, pl.ds(c * h, h), :]
                sent.append(pltpu.make_async_remote_copy(
                    src_ref=blk, dst_ref=outs[i].at[2 * cx + cy, pl.ds(c * h, h), :], send_sem=send_sems.at[3 * i + j],
                    recv_sem=recv_sems.at[3 * i + j], device_id=sibling, device_id_type=MESH))
                sent[-1].start()
        for i in range(n):
            h = ins[i].shape[1] // 2
            for j, (cx, cy) in enumerate(chips):
                theirs = outs[i].at[2 * cx + cy, pl.ds((1 - c) * h, h), :]
                pltpu.make_async_remote_copy(
                    src_ref=theirs, dst_ref=theirs, send_sem=send_sems.at[3 * i + j], recv_sem=recv_sems.at[3 * i + j],
                    device_id=sibling, device_id_type=MESH).wait_recv()
        for cp in sent:
            cp.wait_send()

    return pl.pallas_call(
        body, name=name, out_shape=[jax.ShapeDtypeStruct(s.shape, s.dtype) for s in lands],
        in_specs=[_HBM] * n, out_specs=[_HBM] * n, input_output_aliases={i: i for i in range(n)},
        scratch_shapes=[pltpu.SemaphoreType.DMA((3 * n,)), pltpu.SemaphoreType.DMA((3 * n,))],
    )(*lands)


def join_halves(halves, name):
    n = len(halves)

    def body(*refs):
        ins, outs = refs[:n], refs[n:2 * n]
        send_sems, recv_sems = refs[2 * n:]
        x, y, c, _ = _place()
        sibling = (x, y, 1 - c)
        sent = []
        for i in range(n):
            h = ins[i].shape[0] // 2
            sent.append(pltpu.make_async_remote_copy(
                src_ref=ins[i].at[pl.ds(c * h, h), :], dst_ref=outs[i].at[pl.ds(c * h, h), :], send_sem=send_sems.at[i],
                recv_sem=recv_sems.at[i], device_id=sibling, device_id_type=MESH))
            sent[-1].start()
        for i in range(n):
            h = ins[i].shape[0] // 2
            theirs = outs[i].at[pl.ds((1 - c) * h, h), :]
            pltpu.make_async_remote_copy(
                src_ref=theirs, dst_ref=theirs, send_sem=send_sems.at[i],
                recv_sem=recv_sems.at[i], device_id=sibling, device_id_type=MESH).wait_recv()
        for cp in sent:
            cp.wait_send()

    return pl.pallas_call(
        body, name=name, out_shape=[jax.ShapeDtypeStruct(s.shape, F32) for s in halves],
        in_specs=[_HBM] * n, out_specs=[_HBM] * n, input_output_aliases={i: i for i in range(n)},
        scratch_shapes=[pltpu.SemaphoreType.DMA((n,)), pltpu.SemaphoreType.DMA((n,))],
    )(*halves)


_PACK_ROWS = 8 * LANES


def _pack(arrs):
    flat = jnp.concatenate([a.reshape(-1).astype(F32) for a in arrs])
    pad = (-flat.shape[0]) % _PACK_ROWS
    return jnp.pad(flat, (0, pad)).reshape(-1, LANES)


def _unpack(flat, shapes):
    flat = flat.reshape(-1)
    out, off = [], 0
    for s in shapes:
        n = int(np.prod(s))
        out.append(flat[off:off + n].reshape(s))
        off += n
    return out


def _win_from_blocks(g):
    d = g.shape[1]
    return jnp.pad(g.transpose(1, 0, 2).reshape(d, IN_W), ((0, 0), (0, PW - IN_W)))


def _win_to_blocks(w):
    return w[:, :IN_W].reshape(w.shape[0], 4, IN_W // 4).transpose(1, 0, 2)


def _relu2(a):
    r = jnp.maximum(a, 0)
    return r * r


def kernel(x, c, ada_w, ada_b, norm1_g, w_in, gm_ln_g, gm_ln_b, gm_ws, gm_bs, gm_norm_g, attn_sinks, attn_norm_g, conv_w, conv_b, dt_bias, a_log, d_skip, ssm_norm_g, w_out, norm2_g, w_mlp1, w_mlp2, final_norm_g, loss_target, m_ada_w, m_ada_b, m_norm1_g, m_w_in, m_gm_ln_g, m_gm_ln_b, m_gm_ws, m_gm_bs, m_gm_norm_g, m_attn_sinks, m_attn_norm_g, m_conv_w, m_conv_b, m_dt_bias, m_a_log, m_d_skip, m_ssm_norm_g, m_w_out, m_norm2_g, m_w_mlp1, m_w_mlp2, m_final_norm_g, v_ada_w, v_ada_b, v_norm1_g, v_w_in, v_gm_ln_g, v_gm_ln_b, v_gm_ws, v_gm_bs, v_gm_norm_g, v_attn_sinks, v_attn_norm_g, v_conv_w, v_conv_b, v_dt_bias, v_a_log, v_d_skip, v_ssm_norm_g, v_w_out, v_norm2_g, v_w_mlp1, v_w_mlp2, v_final_norm_g):
    nl = ada_w.shape[0]
    bl, s, d = x.shape
    t = bl * s
    dff4 = w_mlp1.shape[2]
    dff = 4 * dff4
    mod_w = ada_w.shape[2]
    cw_w = conv_w.shape[2]
    xi, yi, ci = lax.axis_index("x"), lax.axis_index("y"), lax.axis_index("c")
    chip = 2 * xi + yi
    dev = 2 * chip + ci
    nex = 8 * bl

    shards = [[w_in[l].astype(BF16), w_out[l].astype(BF16), w_mlp1[l].astype(BF16), w_mlp2[l].astype(BF16)]
              for l in range(nl)]
    groups = [[shards[0][i]] for i in range(4)] + [shards[l] for l in range(1, nl)]

    def start_gather(gi, behind):
        ss, rs, srcs, lands, token = split_copy_start(
            groups[gi], [jax.ShapeDtypeStruct((4,) + a.shape, a.dtype) for a in groups[gi]], _gather_copies, behind,
            f"gather_start_{gi}")
        return (ss, rs, srcs, lands), token

    first_gather, first_token = start_gather(0, None)
    g0 = all_gather_small(_pack([c, conv_w]) + first_token[0, 0], "ag_c")
    g0 = g0.reshape(8, -1)
    c_all = g0[:, :bl * d].reshape(nex, d)
    cw_parts = g0[0::2, bl * d:bl * d + conv_w.size].reshape(4, nl, CONV_K, cw_w)
    conv_w_full = cw_parts.transpose(1, 2, 0, 3).reshape(nl, CONV_K, CCH)

    def c_act(a):
        return _silu(a).astype(BF16)

    def to_bf16(a):
        return a.astype(BF16)

    mod_parts = []
    for l in range(nl):
        bias = lax.dynamic_slice(ada_b[l].reshape(1, -1), (0, chip * mod_w), (1, mod_w))
        mod_parts.append(_mm("nn", c_all, ada_w, dims=(nex, mod_w, d), tm=nex, tn=512, tk=d, out_dtypes=[F32],
                             name=f"mod_{l}", pro_a=c_act, pro_b=to_bf16,
                             b_spec=pl.BlockSpec((None, d, 512), lambda i, j, kk, l=l: (l, kk, j)),
                             extras=[(bias, pl.BlockSpec((1, 512), lambda i, j, kk: (0, j)))],
                             epi=lambda acc, bv: (acc + bv,))[0])
    g1 = all_gather_small(_pack(mod_parts), "ag_mod").reshape(8, -1)
    mod_all = g1[0::2, :nl * nex * mod_w].reshape(4, nl, nex, mod_w).transpose(1, 2, 0, 3).reshape(nl, nex, 4 * mod_w)
    mod = lax.dynamic_slice(mod_all, (0, dev * bl, 0), (nl, bl, 4 * mod_w))
    mods = [[mod[l, :, i * d:(i + 1) * d].reshape(bl, 1, d) for i in range(6)] for l in range(nl)]

    pending, after = [first_gather], g1
    for gi in range(1, len(groups)):
        state, after = start_gather(gi, after)
        pending.append(state)
    mods[0][0] = mods[0][0] + after[0, 0]

    def fetch(gi, behind):
        ss, rs, srcs, lands = pending[gi]
        srcs, lands = split_copy_wait(ss, rs, srcs, lands, _gather_copies, behind, f"gather_wait_{gi}")
        return forward_halves(lands, f"gather_pass_{gi}")

    as_win = _win_from_blocks

    wfull = [None] * nl
    row = lambda a: a.reshape(1, -1)
    pad16 = lambda a: jnp.pad(a.reshape(1, -1), ((0, 0), (0, LANES - SSM_HEADS)))
    tm_res = min(1024, s)

    def residual(acc, xt, gt):
        return acc, xt + gt * acc

    def res_extras(xin, gate, tm=tm_res):
        return [(xin.reshape(t, d), pl.BlockSpec((tm, 512), lambda i, j, kk: (i, j))),
                (gate, pl.BlockSpec((None, 1, 512), lambda i, j, kk: (i * tm // s, 0, j)))]

    w1_blk = lambda tk, tn: pl.BlockSpec((None, tk, tn), lambda i, j, kk: (j // (dff4 // tn), kk, j % (dff4 // tn)))

    saved = []
    xcur = x
    for l in range(nl):
        sh1, sc1, gt1, sh2, sc2, gt2 = mods[l]
        if l == 0:
            win = as_win(fetch(0, mod)[0])
        else:
            g_in, g_out, w1, g_2 = fetch(3 + l, xcur)
            win, wout, w2 = as_win(g_in), g_out.reshape(-1, d), g_2.reshape(dff, d)
        prm_a = (row(gm_ln_g[l]), row(gm_ln_b[l]), gm_ws[l], gm_bs[l].T, row(gm_norm_g[l]))
        prm_b = (row(attn_sinks[l]), row(attn_norm_g[l]))
        prm_c = (pad16(dt_bias[l]), pad16(a_log[l]), pad16(d_skip[l]), row(ssm_norm_g[l]))
        h1 = ln_mod_fwd(xcur, row(norm1_g[l]), sc1, sh1, f"ln1_fwd_{l}")
        p = _mm("nn", h1.reshape(t, d), win, dims=(t, PW, d), tm=1024, tn=512, tk=d, out_dtypes=[F32],
                name=f"proj_in_{l}")[0].reshape(bl, s, PW)
        out_a = gmlp_fwd(p, prm_a, f"gmlp_fwd_{l}")
        out_b = attn_fwd(p, *prm_b, f"attn_fwd_{l}")
        xc = conv_fwd(p, conv_w_full[l], row(conv_b[l]), f"conv_fwd_{l}")
        out_c, states = ssd_fwd(xc, p, prm_c, f"ssd_fwd_{l}")
        mix = jnp.concatenate([out_a, out_b, out_c], axis=-1)
        if l == 0:
            wout = fetch(1, mix)[0].reshape(-1, d)
        mm1, x2 = _mm("nn", mix.reshape(t, d), wout, dims=(t, d, d), tm=tm_res, tn=512, tk=d, out_dtypes=[F32, F32],
                      name=f"proj_out_{l}", extras=res_extras(xcur, gt1), epi=residual)
        x2 = x2.reshape(bl, s, d)
        h2 = ln_mod_fwd(x2, row(norm2_g[l]), sc2, sh2, f"ln2_fwd_{l}")
        if l == 0:
            w1 = fetch(2, h2)[0]
        a1 = _mm("nn", h2.reshape(t, d), w1, dims=(t, dff, d), tm=1024, tn=512, tk=d, out_dtypes=[BF16],
                 name=f"mlp1_{l}", b_spec=w1_blk(d, 512))[0]
        if l == 0:
            w2 = fetch(3, a1)[0].reshape(dff, d)
        tm2 = min(512, s)
        mm2, x3 = _mm("nn", a1, w2, dims=(t, d, dff), tm=tm2, tn=512, tk=dff, out_dtypes=[F32, F32],
                      name=f"mlp2_{l}", extras=res_extras(x2, gt2, tm2), epi=residual, pro_a=_relu2)
        x3 = x3.reshape(bl, s, d)
        wfull[l] = (win, wout, w1, w2)
        saved.append((xcur, h1, p, xc, states, mix, mm1.reshape(bl, s, d), x2, h2, a1, mm2.reshape(bl, s, d),
                      prm_a, prm_b, prm_c))
        xcur = x3

    dx, d_final_g, loss_part = loss_head(xcur, row(final_norm_g), loss_target, "loss_head")
    loss = lax.psum(loss_part[0, 0], ("x", "y", "c"))

    def rs_swap(grads, tag):
        ss, rs, srcs, lands, token = split_copy_start(
            grads, [jax.ShapeDtypeStruct((4, g.shape[1] // 2, g.shape[2]), F32) for g in grads],
            _swap_copies, None, f"rs_swap_{tag}")
        return (ss, rs, srcs, lands), token

    def rs_begin(swap_state, tag, swapped_behind, start_behind=None):
        ss, rs, srcs, lands = swap_state
        grads, theirs = split_copy_wait(ss, rs, srcs, lands, _swap_copies, swapped_behind, f"rs_swapped_{tag}")
        sums = [add_pair(g, th, ci, f"rs_add_{tag}_{i}") for i, (g, th) in enumerate(zip(grads, theirs))]
        ss, rs, srcs, lands, token = split_copy_start(
            [sm[1] for sm in sums], [jax.ShapeDtypeStruct((3,) + sm[1].shape[1:], BF16) for sm in sums],
            _scatter_copies, sums[0][0] if start_behind is None else start_behind, f"rs_start_{tag}")
        return (ss, rs, srcs, lands, [sm[0] for sm in sums]), token

    def rs_end(state, behind, tag):
        ss, rs, srcs, lands, sums_f32 = state
        _, got = split_copy_wait(ss, rs, srcs, lands, _scatter_copies, behind, f"rs_wait_{tag}")
        halves = [sum_own_recv(sf, g, chip, ci, f"rs_sum_{tag}_{i}") for i, (sf, g) in enumerate(zip(sums_f32, got))]
        return join_halves(halves, f"rs_join_{tag}")

    small_parts = [None] * nl
    dmods = [None] * nl
    reduced = [[None] * 4 for _ in range(nl)]
    pending_rs, rs_token = [], None
    part_slots = {"a": (0, 1), "m": (2, 3)}

    def finish(behind):
        for ll, part, state in pending_rs:
            for slot, blk in zip(part_slots[part], rs_end(state, behind, f"{ll}{part}")):
                reduced[ll][slot] = blk
        pending_rs.clear()

    for l in reversed(range(nl)):
        sh1, sc1, gt1, sh2, sc2, gt2 = mods[l]
        win, wout, w1, w2 = wfull[l]
        xin, h1, p, xc, states, mix, mm1, x2, h2, a1, mm2, prm_a, prm_b, prm_c = saved[l]
        if rs_token is not None:
            gt2 = gt2 + rs_token[0, 0]
        dm2, dgt2 = gate_bwd(dx, mm2, gt2, f"gate2_bwd_{l}")
        dm2 = dm2.reshape(t, d)
        da1 = _mm("nt", dm2, w2, dims=(t, dff, d), tm=1024, tn=512, tk=d, out_dtypes=[BF16], name=f"mlp2_dx_{l}",
                  extras=[(a1, pl.BlockSpec((1024 if t >= 1024 else t, 512), lambda i, j, kk: (i, j)))],
                  epi=lambda acc, av: (acc * (2.0 * jnp.maximum(av, 0).astype(F32)),))[0]
        dw2 = _mm("tn", a1, dm2, dims=(dff, d, t), tm=512, tn=d, tk=2048, out_dtypes=[F32], name=f"mlp2_dw_{l}",
                  pro_a=_relu2, out_shapes=[(4, dff4, d)],
                  out_specs=[pl.BlockSpec((None, 512, d), lambda i, j, kk: (i // (dff4 // 512), i % (dff4 // 512), 0))])[0]
        dw1 = _mm("tn", h2.reshape(t, d), da1, dims=(d, dff, t), tm=512, tn=dff4, tk=2048, out_dtypes=[F32],
                  name=f"mlp1_dw_{l}", out_shapes=[(4, d, dff4)],
                  out_specs=[pl.BlockSpec((None, 512, dff4), lambda i, j, kk: (j, i, 0))])[0]
        swap_state, swap_token = rs_swap([dw1, dw2], f"{l}m")
        dh2 = _mm_nt_blocked(da1, w1, tm=512, tn=512, name=f"mlp1_dx_{l}", behind=swap_token)
        mlp_state, mlp_token = rs_begin(swap_state, f"{l}m", dh2)
        sc2 = sc2 + mlp_token[0, 0]
        dx2, dsc2, dsh2, dn2 = ln_mod_bwd(dh2.reshape(bl, s, d), x2, dx, row(norm2_g[l]), sc2, f"ln2_bwd_{l}")
        dm1, dgt1 = gate_bwd(dx2, mm1, gt1, f"gate1_bwd_{l}")
        dm1 = dm1.reshape(t, d)
        dmix = _mm("nt", dm1, wout, dims=(t, d, d), tm=1024, tn=512, tk=d, out_dtypes=[F32],
                   name=f"proj_out_dx_{l}")[0].reshape(bl, s, d)
        dwout = _mm("tn", mix.reshape(t, d), dm1, dims=(d, d, t), tm=512, tn=d, tk=2048, out_dtypes=[F32],
                    name=f"proj_out_dw_{l}", out_shapes=[(4, d // 4, d)],
                    out_specs=[pl.BlockSpec((None, 512, d), lambda i, j, kk: (i // (d // 4 // 512), i % (d // 4 // 512), 0))])[0]
        du, dv, dlg, dlb, dws, dbst, dgng = gmlp_bwd(p, dmix, prm_a, f"gmlp_bwd_{l}")
        dq, dk, dvv, dsinks, dang = attn_bwd(p, dmix, *prm_b, f"attn_bwd_{l}")
        dxc, ddt, dz, ddtb, dalog, ddsk, dsng = ssd_bwd(xc, p, states, dmix, prm_c, f"ssd_bwd_{l}")
        dxbc, dcw, dcb = conv_bwd(p, dxc, conv_w_full[l], row(conv_b[l]), f"conv_bwd_{l}")
        dp = jnp.concatenate([du, dv, dq, dk, dvv, dz, dxbc, ddt, jnp.zeros((bl, s, PW - OFF_DT - LANES), BF16)],
                             axis=-1).reshape(t, PW)
        dwin = _mm("tn", h1.reshape(t, d), dp, dims=(d, PW, t), tm=512, tn=PW // 3, tk=2048, out_dtypes=[F32],
                   name=f"proj_in_dw_{l}")[0]
        dwin_blocks = _win_to_blocks(dwin)
        mixer_swap, swap_token = rs_swap([dwin_blocks, dwout], f"{l}a")
        dh1 = _mm("nt", dp, win, dims=(t, d, PW), tm=1024, tn=512, tk=PW, out_dtypes=[F32],
                  name=f"proj_in_dx_{l}", behind=swap_token)[0]
        dx, dsc1, dsh1, dn1 = ln_mod_bwd(dh1.reshape(bl, s, d), xin, dx2, row(norm1_g[l]), sc1, f"ln1_bwd_{l}")
        dmods[l] = jnp.concatenate([dsh1, dsc1, dgt1, dsh2, dsc2, dgt2], axis=-1).reshape(bl, 6 * d)
        small_parts[l] = [dn1, dlg, dlb, dws, dbst.T, dgng, dsinks, dang, dcw, dcb, ddtb[:, :SSM_HEADS],
                          dalog[:, :SSM_HEADS], ddsk[:, :SSM_HEADS], dsng, dn2]
        finish(dx)
        pending_rs.append((l, "m", mlp_state))
        if l > 0:
            state, rs_token = rs_begin(mixer_swap, f"{l}a", dx)
            pending_rs.append((l, "a", state))
    grad_x = dx

    big = [(w_in, m_w_in, v_w_in), (w_out, m_w_out, v_w_out), (w_mlp1, m_w_mlp1, v_w_mlp1), (w_mlp2, m_w_mlp2, v_w_mlp2)]
    big_out = [None] * 4
    for l in reversed(range(1, nl)):
        for i, (wt, mt, vt) in enumerate(big):
            if i > 0:
                big_out[i] = adamw_layer(wt, mt, vt, reduced[l][i], l, big_out[i], f"adamw_big_{i}_{l}")

    small_names = [norm1_g, gm_ln_g, gm_ln_b, gm_ws, gm_bs, gm_norm_g, attn_sinks, attn_norm_g, None, conv_b, dt_bias,
                   a_log, d_skip, ssm_norm_g, norm2_g]
    n_small = len(small_names)
    per_param = [jnp.stack([small_parts[l][i].reshape(-1) for l in range(nl)]) for i in range(n_small)]
    small_vec = _pack(per_param + [d_final_g])
    rs_small = small_vec.shape[0]
    dmod_local = jnp.stack(dmods, axis=1)
    g2 = all_gather_small(jnp.concatenate([small_vec, _pack([dmod_local])], axis=0), "ag_small")
    state, rs_token = rs_begin(mixer_swap, "0a", grad_x, start_behind=g2)
    pending_rs.append((0, "a", state))
    g2 = g2 + rs_token[0, 0]
    g_small = sum_devices(g2[:, :rs_small, :], "sum_small")
    dmod_all = g2[:, rs_small:, :].reshape(8, -1)[:, :bl * nl * 6 * d].reshape(nex, nl * 6 * d)
    g_ada_b = sum_devices(dmod_all.reshape(nex, -1, LANES), "sum_ada_b").reshape(nl, 6 * d)
    shapes = [(nl, int(np.prod(small_parts[0][i].shape))) for i in range(n_small)] + [(d,)]
    g_list = _unpack(g_small, shapes)
    g_conv_w = lax.dynamic_slice(g_list[8].reshape(nl, CONV_K, CCH), (0, 0, chip * cw_w), (nl, CONV_K, cw_w))

    dm_cols = lax.dynamic_slice(dmod_all.reshape(nex, nl, 6 * d), (0, 0, chip * mod_w), (nex, nl, mod_w))
    g_ada_w = _mm("tn", c_all, dm_cols.reshape(nex, nl * mod_w), dims=(d, nl * mod_w, nex), tm=512, tn=512, tk=nex,
                  out_dtypes=[F32], name="ada_w_grad", pro_a=c_act, pro_b=to_bf16, out_shapes=[(nl, d, mod_w)],
                  out_specs=[pl.BlockSpec((None, 512, 512), lambda i, j, kk: (j // (mod_w // 512), i, j % (mod_w // 512)))])[0]
    d_ada_w, m_ada_w_n, v_ada_w_n = [a.reshape(ada_w.shape) for a in
                                     adamw(_rows2d(ada_w), _rows2d(m_ada_w), _rows2d(v_ada_w), _rows2d(g_ada_w), "adamw_ada_w")]

    smalls = {
        "ada_b": (ada_b, m_ada_b, v_ada_b, g_ada_b), "norm1_g": (norm1_g, m_norm1_g, v_norm1_g, g_list[0]),
        "gm_ln_g": (gm_ln_g, m_gm_ln_g, v_gm_ln_g, g_list[1]), "gm_ln_b": (gm_ln_b, m_gm_ln_b, v_gm_ln_b, g_list[2]),
        "gm_ws": (gm_ws, m_gm_ws, v_gm_ws, g_list[3]), "gm_bs": (gm_bs, m_gm_bs, v_gm_bs, g_list[4]),
        "gm_norm_g": (gm_norm_g, m_gm_norm_g, v_gm_norm_g, g_list[5]),
        "attn_sinks": (attn_sinks, m_attn_sinks, v_attn_sinks, g_list[6]),
        "attn_norm_g": (attn_norm_g, m_attn_norm_g, v_attn_norm_g, g_list[7]),
        "conv_w": (conv_w, m_conv_w, v_conv_w, g_conv_w), "conv_b": (conv_b, m_conv_b, v_conv_b, g_list[9]),
        "dt_bias": (dt_bias, m_dt_bias, v_dt_bias, g_list[10]), "a_log": (a_log, m_a_log, v_a_log, g_list[11]),
        "d_skip": (d_skip, m_d_skip, v_d_skip, g_list[12]),
        "ssm_norm_g": (ssm_norm_g, m_ssm_norm_g, v_ssm_norm_g, g_list[13]),
        "norm2_g": (norm2_g, m_norm2_g, v_norm2_g, g_list[14]),
        "final_norm_g": (final_norm_g, m_final_norm_g, v_final_norm_g, g_list[15]),
    }
    keys = list(smalls)
    wv, mv, vv_, gv = [_pack([smalls[k][i].reshape(smalls[k][0].shape) for k in keys]) for i in range(4)]
    sd_, sm_, sv_ = adamw(wv, mv, vv_, gv, "adamw_small")
    shp = [smalls[k][0].shape for k in keys]
    small_out = {k: (smalls[k][3].reshape(smalls[k][0].shape), a, b, cc)
                 for k, a, b, cc in zip(keys, _unpack(sd_, shp), _unpack(sm_, shp), _unpack(sv_, shp))}

    late = jnp.zeros((8, LANES), F32) + (sv_[0, 0] + v_ada_w_n[0, 0, 0])
    for bo in big_out:
        if bo is not None:
            late = late + bo[3][nl - 1, 0, 0]
    finish(late)
    for i, (wt, mt, vt) in enumerate(big):
        if i > 0:
            big_out[i] = adamw_layer(wt, mt, vt, reduced[0][i], 0, big_out[i], f"adamw_big_{i}_0")
    minor_first = lambda a: jnp.transpose(a, (2, 0, 1))
    g_in = jnp.stack([reduced[l][0].T for l in range(nl)], axis=1)
    back = lambda a: jnp.transpose(a, (1, 2, 0))
    big_out[0] = [back(a) for a in [g_in, *adamw_minor_rows(minor_first(w_in), minor_first(m_w_in),
                                                            minor_first(v_w_in), g_in, "adamw_w_in")]]

    out = {"ada_w": (g_ada_w, d_ada_w, m_ada_w_n, v_ada_w_n), "w_in": big_out[0], "w_out": big_out[1],
           "w_mlp1": big_out[2], "w_mlp2": big_out[3], **small_out}
    order = ["ada_w", "ada_b", "norm1_g", "w_in", "gm_ln_g", "gm_ln_b", "gm_ws", "gm_bs", "gm_norm_g", "attn_sinks",
             "attn_norm_g", "conv_w", "conv_b", "dt_bias", "a_log", "d_skip", "ssm_norm_g", "w_out", "norm2_g",
             "w_mlp1", "w_mlp2", "final_norm_g"]
    return (loss, grad_x, *[out[k][0] for k in order], *[out[k][1] for k in order],
            *[out[k][2] for k in order], *[out[k][3] for k in order])
```

```python
import functools
import math

import jax
import jax.numpy as jnp
import numpy as np
from jax import lax
from jax.experimental import pallas as pl
from jax.experimental.pallas import tpu as pltpu

F32 = jnp.float32
BF16 = jnp.bfloat16
HI = lax.Precision.HIGHEST
MESH = pl.DeviceIdType.MESH

CHUNK = 128
GM_HEADS, GM_HD = 4, 128
ATT_HEADS, ATT_KV, ATT_HD = 8, 2, 64
WINDOW = 128
SSM_HEADS, SSM_HD, SSM_GROUPS, SSM_STATE, CONV_K = 16, 64, 2, 128, 4
EPS = 1e-6
LN_EPS = 1e-5
NEG = -1e30
LANES = 128

GMW = GM_HEADS * GM_HD
ATW = ATT_HEADS * ATT_HD
KVW = ATT_KV * ATT_HD
SSW = SSM_HEADS * SSM_HD
BCW = SSM_GROUPS * SSM_STATE
CCH = SSW + 2 * BCW
GRW = SSW // SSM_GROUPS
IN_SIZES = (GMW, GMW, ATW, KVW, KVW, SSW, CCH, SSM_HEADS)
IN_W = sum(IN_SIZES)
OFF_U, OFF_V, OFF_Q, OFF_K, OFF_VV, OFF_Z, OFF_XBC, OFF_DT = 0, 512, 1024, 1536, 1664, 1792, 2816, 4352
ZB = 256
PW = 4608

ADAM_LR, ADAM_B1, ADAM_B2, ADAM_EPS, ADAM_WD, ADAM_STEP = 0.001, 0.9, 0.999, 1e-08, 0.01, 10

VMEM_LIMIT = 56 * 1024 * 1024


def _cp(sem=None):
    return pltpu.CompilerParams(dimension_semantics=sem, vmem_limit_bytes=VMEM_LIMIT)


_DN = {"nn": (((1,), (0,)), ((), ())), "nt": (((1,), (1,)), ((), ())), "tn": (((0,), (0,)), ((), ()))}


def _dot(form, a, b):
    return lax.dot_general(a.astype(BF16), b.astype(BF16), _DN[form], preferred_element_type=F32)


@jax.custom_vjp
def _nn(a, b):
    return _dot("nn", a, b)


@jax.custom_vjp
def _nt(a, b):
    return _dot("nt", a, b)


@jax.custom_vjp
def _tn(a, b):
    return _dot("tn", a, b)


_nn.defvjp(lambda a, b: (_dot("nn", a, b), (a, b)), lambda r, g: (_dot("nt", g, r[1]), _dot("tn", r[0], g)))
_nt.defvjp(lambda a, b: (_dot("nt", a, b), (a, b)), lambda r, g: (_dot("nn", g, r[1]), _dot("tn", g, r[0])))
_tn.defvjp(lambda a, b: (_dot("tn", a, b), (a, b)), lambda r, g: (_dot("nt", r[1], g), _dot("nn", r[0], g)))


def _hdot(a, b):
    return jnp.dot(a, b, precision=HI, preferred_element_type=F32)


def _silu(x):
    return x * (1.0 / (1.0 + jnp.exp(-x)))


def _softplus(x):
    return jnp.maximum(x, 0.0) + jnp.log1p(jnp.exp(-jnp.abs(x)))


def _gelu(x):
    return 0.5 * x * (1.0 + jnp.tanh(math.sqrt(2.0 / math.pi) * (x + 0.044715 * (x * x * x))))


def _rms(y, g):
    return y * lax.rsqrt(jnp.mean(y * y, axis=-1, keepdims=True) + EPS) * g


def _mm(form, a, b, *, dims, tm, tn, tk, out_dtypes, name, a_spec=None, b_spec=None, out_specs=None,
        out_shapes=None, extras=(), epi=None, pro_a=None, pro_b=None, behind=None):
    m, n, k = dims
    tm, tn, tk = min(tm, m), min(tn, n), min(tk, k)
    assert m % tm == 0 and n % tn == 0 and k % tk == 0, (name, dims, tm, tn, tk)
    nk = k // tk
    if a_spec is None:
        a_spec = (pl.BlockSpec((tk, tm), lambda i, j, kk: (kk, i)) if form == "tn"
                  else pl.BlockSpec((tm, tk), lambda i, j, kk: (i, kk)))
    if b_spec is None:
        b_spec = (pl.BlockSpec((tn, tk), lambda i, j, kk: (j, kk)) if form == "nt"
                  else pl.BlockSpec((tk, tn), lambda i, j, kk: (kk, j)))
    n_out = len(out_dtypes)
    if out_specs is None:
        out_specs = [pl.BlockSpec((tm, tn), lambda i, j, kk: (i, j))] * n_out
    if out_shapes is None:
        out_shapes = [(m, n)] * n_out
    ne = len(extras)
    n_behind = 0 if behind is None else 1

    def body(*refs):
        a_ref, b_ref = refs[0], refs[1]
        ex = refs[2:2 + ne]
        outs = refs[2 + ne + n_behind:2 + ne + n_behind + n_out]

        def write(val):
            res = epi(val, *[e[...] for e in ex]) if epi is not None else (val,)
            for o, r in zip(outs, res):
                o[...] = r.astype(o.dtype)

        av = a_ref[...]
        if pro_a is not None:
            av = pro_a(av)
        bv = b_ref[...]
        if pro_b is not None:
            bv = pro_b(bv)
        part = lax.dot_general(av, bv, _DN[form], preferred_element_type=F32)
        if nk == 1:
            write(part)
        else:
            acc = refs[-1]
            kk = pl.program_id(2)

            @pl.when(kk == 0)
            def _():
                acc[...] = part

            @pl.when(kk > 0)
            def _():
                acc[...] += part

            @pl.when(kk == nk - 1)
            def _():
                write(acc[...])

    res = pl.pallas_call(
        body, name=name, grid=(m // tm, n // tn, nk),
        in_specs=[a_spec, b_spec] + [s for _, s in extras] + [_ANY] * n_behind,
        out_specs=out_specs,
        out_shape=[jax.ShapeDtypeStruct(s, d) for s, d in zip(out_shapes, out_dtypes)],
        scratch_shapes=[pltpu.VMEM((tm, tn), F32)] if nk > 1 else [],
        compiler_params=_cp(("parallel", "parallel", "arbitrary")),
    )(a, b, *[e for e, _ in extras], *([behind] if n_behind else []))
    return res


def _mm_nt_blocked(a, b, *, tm, tn, name, behind=None):
    m = a.shape[0]
    nparts, n, f = b.shape
    tm, tn = min(tm, m), min(tn, n)
    n_behind = 0 if behind is None else 1

    def body(a_ref, *rest):
        b_refs, o_ref = rest[:nparts], rest[nparts + n_behind]
        acc = None
        for k in range(nparts):
            part = lax.dot_general(a_ref[:, k * f:(k + 1) * f], b_refs[k][...], _DN["nt"], preferred_element_type=F32)
            acc = part if acc is None else acc + part
        o_ref[...] = acc

    return pl.pallas_call(
        body, name=name, grid=(m // tm, n // tn),
        in_specs=[pl.BlockSpec((tm, nparts * f), lambda i, j: (i, 0))]
        + [pl.BlockSpec((None, tn, f), lambda i, j, k=k: (k, j, 0)) for k in range(nparts)] + [_ANY] * n_behind,
        out_specs=pl.BlockSpec((tm, tn), lambda i, j: (i, j)),
        out_shape=jax.ShapeDtypeStruct((m, n), F32),
        compiler_params=_cp(("parallel", "parallel")),
    )(a, *([b] * nparts), *([behind] if n_behind else []))


def _row_tile(s):
    return min(512, s)


def ln_mod_fwd(x, g, sc, sh, name):
    bsz, s, d = x.shape
    ts = _row_tile(s)

    def body(x_ref, g_ref, sc_ref, sh_ref, o_ref):
        xv = x_ref[...]
        r = lax.rsqrt(jnp.mean(xv * xv, axis=-1, keepdims=True) + EPS)
        o_ref[...] = ((xv * r * g_ref[...]) * (1.0 + sc_ref[...]) + sh_ref[...]).astype(o_ref.dtype)

    row = pl.BlockSpec((None, ts, d), lambda b, i: (b, i, 0))
    vec = pl.BlockSpec((None, 1, d), lambda b, i: (b, 0, 0))
    return pl.pallas_call(
        body, name=name, grid=(bsz, s // ts),
        in_specs=[row, pl.BlockSpec((1, d), lambda b, i: (0, 0)), vec, vec],
        out_specs=row, out_shape=jax.ShapeDtypeStruct(x.shape, BF16),
        compiler_params=_cp(("parallel", "parallel")),
    )(x, g, sc, sh)


def ln_mod_bwd(dh, x, dres, g, sc, name):
    bsz, s, d = x.shape
    ts = _row_tile(s)

    def body(dh_ref, x_ref, dres_ref, g_ref, sc_ref, dx_ref, dsc_ref, dsh_ref, dg_ref):
        b, i = pl.program_id(0), pl.program_id(1)
        xv, dhv, gv = x_ref[...], dh_ref[...], g_ref[...]
        r = lax.rsqrt(jnp.mean(xv * xv, axis=-1, keepdims=True) + EPS)
        xn = xv * r
        a = dhv * (1.0 + sc_ref[...])
        dxn = a * gv
        dx_ref[...] = dres_ref[...] + r * (dxn - xn * jnp.mean(dxn * xn, axis=-1, keepdims=True))
        p_sc = jnp.sum(dhv * (xn * gv), axis=0, keepdims=True)
        p_sh = jnp.sum(dhv, axis=0, keepdims=True)
        p_g = jnp.sum(a * xn, axis=0, keepdims=True)

        @pl.when(i == 0)
        def _():
            dsc_ref[...] = p_sc
            dsh_ref[...] = p_sh

        @pl.when(i > 0)
        def _():
            dsc_ref[...] += p_sc
            dsh_ref[...] += p_sh

        @pl.when((i == 0) & (b == 0))
        def _():
            dg_ref[...] = p_g

        @pl.when((i > 0) | (b > 0))
        def _():
            dg_ref[...] += p_g

    row = pl.BlockSpec((None, ts, d), lambda b, i: (b, i, 0))
    vec = pl.BlockSpec((None, 1, d), lambda b, i: (b, 0, 0))
    one = pl.BlockSpec((1, d), lambda b, i: (0, 0))
    return pl.pallas_call(
        body, name=name, grid=(bsz, s // ts),
        in_specs=[row, row, row, one, vec],
        out_specs=[row, vec, vec, one],
        out_shape=[jax.ShapeDtypeStruct(x.shape, F32), jax.ShapeDtypeStruct((bsz, 1, d), F32),
                   jax.ShapeDtypeStruct((bsz, 1, d), F32), jax.ShapeDtypeStruct((1, d), F32)],
        compiler_params=_cp(("arbitrary", "arbitrary")),
    )(dh, x, dres, g, sc)


def gate_bwd(dx, mm, gate, name):
    bsz, s, d = dx.shape
    ts = _row_tile(s)

    def body(dx_ref, m_ref, g_ref, dm_ref, dg_ref):
        i = pl.program_id(1)
        dxv = dx_ref[...]
        dm_ref[...] = (dxv * g_ref[...]).astype(dm_ref.dtype)
        p = jnp.sum(dxv * m_ref[...], axis=0, keepdims=True)

        @pl.when(i == 0)
        def _():
            dg_ref[...] = p

        @pl.when(i > 0)
        def _():
            dg_ref[...] += p

    row = pl.BlockSpec((None, ts, d), lambda b, i: (b, i, 0))
    vec = pl.BlockSpec((None, 1, d), lambda b, i: (b, 0, 0))
    return pl.pallas_call(
        body, name=name, grid=(bsz, s // ts),
        in_specs=[row, row, vec], out_specs=[row, vec],
        out_shape=[jax.ShapeDtypeStruct(dx.shape, BF16), jax.ShapeDtypeStruct((bsz, 1, d), F32)],
        compiler_params=_cp(("parallel", "arbitrary")),
    )(dx, mm, gate)


def loss_head(x, g, tgt, name):
    bsz, s, d = x.shape
    ts = _row_tile(s)

    def body(x_ref, g_ref, t_ref, dx_ref, dg_ref, l_ref):
        b, i = pl.program_id(0), pl.program_id(1)
        xv, gv = x_ref[...], g_ref[...]
        r = lax.rsqrt(jnp.mean(xv * xv, axis=-1, keepdims=True) + EPS)
        xn = xv * r
        e = xn * gv - t_ref[...]
        dy = e * (1.0 / d)
        dxn = dy * gv
        dx_ref[...] = r * (dxn - xn * jnp.mean(dxn * xn, axis=-1, keepdims=True))
        p_g = jnp.sum(dy * xn, axis=0, keepdims=True)
        p_l = jnp.zeros((1, LANES), F32) + jnp.sum(e * e) * (0.5 / d)
        first = (i == 0) & (b == 0)

        @pl.when(first)
        def _():
            dg_ref[...] = p_g
            l_ref[...] = p_l

        @pl.when(jnp.logical_not(first))
        def _():
            dg_ref[...] += p_g
            l_ref[...] += p_l

    row = pl.BlockSpec((None, ts, d), lambda b, i: (b, i, 0))
    one = pl.BlockSpec((1, d), lambda b, i: (0, 0))
    return pl.pallas_call(
        body, name=name, grid=(bsz, s // ts),
        in_specs=[row, one, row],
        out_specs=[row, one, pl.BlockSpec((1, LANES), lambda b, i: (0, 0))],
        out_shape=[jax.ShapeDtypeStruct(x.shape, F32), jax.ShapeDtypeStruct((1, d), F32),
                   jax.ShapeDtypeStruct((1, LANES), F32)],
        compiler_params=_cp(("arbitrary", "arbitrary")),
    )(x, g, tgt)


def _gmlp_chunk(u_raw, v_raw, ln_g, ln_b, w, bs_t, out_g):
    c = u_raw.shape[0]
    u, v = _gelu(u_raw), _gelu(v_raw)
    tril = lax.broadcasted_iota(jnp.int32, (c, c), 0) >= lax.broadcasted_iota(jnp.int32, (c, c), 1)
    ys = []
    for h in range(GM_HEADS):
        sl = slice(h * GM_HD, (h + 1) * GM_HD)
        vh = v[:, sl]
        xc = vh - jnp.mean(vh, axis=-1, keepdims=True)
        vn = xc * lax.rsqrt(jnp.mean(xc * xc, axis=-1, keepdims=True) + LN_EPS) * ln_g[:, sl] + ln_b[:, sl]
        gate = _nn(jnp.where(tril, w[h], 0.0), vn) + bs_t[:, h:h + 1]
        ys.append(u[:, sl] * gate)
    return _rms(jnp.concatenate(ys, axis=1), out_g)


def _gmlp_specs(bsz, nc):
    seg = lambda off: pl.BlockSpec((None, CHUNK, GMW), lambda b, c: (b, c, off // GMW))
    full = lambda shape: pl.BlockSpec(shape, lambda b, c: (0,) * len(shape))
    par = [full((1, GMW)), full((1, GMW)), full((GM_HEADS, CHUNK, CHUNK)), full((CHUNK, GM_HEADS)), full((1, GMW))]
    return seg, full, par


def gmlp_fwd(p, prm, name):
    bsz, s, _ = p.shape
    nc = s // CHUNK
    seg, _, par = _gmlp_specs(bsz, nc)

    def body(u_ref, v_ref, lg, lb, w, bt, og, o_ref):
        o_ref[...] = _gmlp_chunk(u_ref[...], v_ref[...], lg[...], lb[...], w[...], bt[...], og[...]).astype(o_ref.dtype)

    return pl.pallas_call(
        body, name=name, grid=(bsz, nc),
        in_specs=[seg(OFF_U), seg(OFF_V)] + par,
        out_specs=pl.BlockSpec((None, CHUNK, GMW), lambda b, c: (b, c, 0)),
        out_shape=jax.ShapeDtypeStruct((bsz, s, GMW), BF16),
        compiler_params=_cp(("parallel", "parallel")),
    )(p, p, *prm)


def _accumulate(first, refs, vals):
    @pl.when(first)
    def _():
        for r, v in zip(refs, vals):
            r[...] = v

    @pl.when(jnp.logical_not(first))
    def _():
        for r, v in zip(refs, vals):
            r[...] += v


def gmlp_bwd(p, dmix, prm, name):
    bsz, s, _ = p.shape
    nc = s // CHUNK
    seg, full, par = _gmlp_specs(bsz, nc)

    def body(u_ref, v_ref, do_ref, lg, lb, w, bt, og, du_ref, dv_ref, *dpar):
        first = (pl.program_id(0) == 0) & (pl.program_id(1) == 0)
        _, vjp = jax.vjp(_gmlp_chunk, u_ref[...], v_ref[...], lg[...], lb[...], w[...], bt[...], og[...])
        gr = vjp(do_ref[...])
        du_ref[...] = gr[0].astype(du_ref.dtype)
        dv_ref[...] = gr[1].astype(dv_ref.dtype)
        _accumulate(first, dpar, gr[2:])

    out_seg = pl.BlockSpec((None, CHUNK, GMW), lambda b, c: (b, c, 0))
    return pl.pallas_call(
        body, name=name, grid=(bsz, nc),
        in_specs=[seg(OFF_U), seg(OFF_V), out_seg] + par,
        out_specs=[out_seg, out_seg] + par,
        out_shape=[jax.ShapeDtypeStruct((bsz, s, GMW), BF16)] * 2 + [jax.ShapeDtypeStruct(x.shape, F32) for x in prm],
        compiler_params=_cp(("arbitrary", "arbitrary")),
    )(p, p, dmix, *prm)


def _attn_block(q, kp, kc, vp, vc, sinks, out_g, has_prev):
    w = q.shape[0]
    k2 = jnp.concatenate([kp, kc], axis=0)
    v2 = jnp.concatenate([vp, vc], axis=0)
    qi = lax.broadcasted_iota(jnp.int32, (w, 2 * w), 0)
    kj = lax.broadcasted_iota(jnp.int32, (w, 2 * w), 1)
    diff = qi + w - kj
    grp = ATT_HEADS // ATT_KV
    valid = (diff >= 0) & (diff < w) & ((kj >= w) | has_prev)
    valid = jnp.concatenate([valid] * grp, axis=0)
    outs = []
    for kv in range(ATT_KV):
        kh = k2[:, kv * ATT_HD:(kv + 1) * ATT_HD]
        vh = v2[:, kv * ATT_HD:(kv + 1) * ATT_HD]
        heads = range(kv * grp, (kv + 1) * grp)
        qs = jnp.concatenate([q[:, h * ATT_HD:(h + 1) * ATT_HD] for h in heads], axis=0)
        sink = jnp.concatenate([jnp.broadcast_to(sinks[:, h:h + 1], (w, 1)) for h in heads], axis=0)
        sc = jnp.where(valid, _nt(qs, kh) * (ATT_HD ** -0.5), NEG)
        m = jnp.maximum(jnp.max(sc, axis=-1, keepdims=True), sink)
        e = jnp.exp(sc - m)
        pr = e / (jnp.sum(e, axis=-1, keepdims=True) + jnp.exp(sink - m))
        o = _nn(pr, vh)
        outs += [o[gi * w:(gi + 1) * w] for gi in range(grp)]
    return _rms(jnp.concatenate(outs, axis=1), out_g)


ATT_QB = 4


def _attn_tiles(s):
    qb = min(ATT_QB, s // WINDOW)
    return qb, qb * WINDOW, s // (qb * WINDOW)


def attn_fwd(p, sinks, out_g, name):
    bsz, s, _ = p.shape
    qb, rows, steps = _attn_tiles(s)

    def body(q_ref, kp_ref, kc_ref, vp_ref, vc_ref, s_ref, g_ref, o_ref):
        n = pl.program_id(1)
        for w in range(qb):
            sl = pl.ds(w * WINDOW, WINDOW)
            before = pl.ds((w - 1) * WINDOW, WINDOW)
            kp = kp_ref[...] if w == 0 else kc_ref[before, :]
            vp = vp_ref[...] if w == 0 else vc_ref[before, :]
            o_ref[sl, :] = _attn_block(q_ref[sl, :], kp, kc_ref[sl, :], vp, vc_ref[sl, :], s_ref[...], g_ref[...],
                                       (n > 0) if w == 0 else True).astype(o_ref.dtype)

    cur = lambda off: pl.BlockSpec((None, rows, KVW), lambda b, n: (b, n, off // KVW))
    prev = lambda off: pl.BlockSpec((None, WINDOW, KVW), lambda b, n: (b, jnp.maximum(n * qb - 1, 0), off // KVW))
    return pl.pallas_call(
        body, name=name, grid=(bsz, steps),
        in_specs=[pl.BlockSpec((None, rows, ATW), lambda b, n: (b, n, OFF_Q // ATW)),
                  prev(OFF_K), cur(OFF_K), prev(OFF_VV), cur(OFF_VV),
                  pl.BlockSpec((1, ATT_HEADS), lambda b, n: (0, 0)), pl.BlockSpec((1, ATW), lambda b, n: (0, 0))],
        out_specs=pl.BlockSpec((None, rows, ATW), lambda b, n: (b, n, 0)),
        out_shape=jax.ShapeDtypeStruct((bsz, s, ATW), BF16),
        compiler_params=_cp(("parallel", "parallel")),
    )(p, p, p, p, p, sinks, out_g)


def attn_bwd(p, dmix, sinks, out_g, name):
    bsz, s, _ = p.shape
    qb, rows, steps = _attn_tiles(s)
    last = pl.ds(rows - WINDOW, WINDOW)

    def body(q_ref, kp_ref, kc_ref, vp_ref, vc_ref, do_ref, s_ref, g_ref,
             dq_ref, dk_ref, dv_ref, ds_ref, dg_ref, ck, cv):
        b, n = pl.program_id(0), pl.program_id(1)

        @pl.when(n == 0)
        def _():
            ck[...] = jnp.zeros_like(ck)
            cv[...] = jnp.zeros_like(cv)

        @pl.when(n < steps)
        def _():
            grads = []
            for w in range(qb):
                sl = pl.ds(w * WINDOW, WINDOW)
                before = pl.ds((w - 1) * WINDOW, WINDOW)
                kp = kp_ref[...] if w == 0 else kc_ref[before, :]
                vp = vp_ref[...] if w == 0 else vc_ref[before, :]
                fn = functools.partial(_attn_block, has_prev=(n > 0) if w == 0 else True)
                _, vjp = jax.vjp(fn, q_ref[sl, :], kp, kc_ref[sl, :], vp, vc_ref[sl, :], s_ref[...], g_ref[...])
                grads.append(vjp(do_ref[sl, :]))
                dq_ref[sl, :] = grads[-1][0].astype(dq_ref.dtype)
            dk_ref[...] = ck[...].astype(dk_ref.dtype)
            dv_ref[...] = cv[...].astype(dv_ref.dtype)
            dk_ref[last, :] = (ck[last, :] + grads[0][1]).astype(dk_ref.dtype)
            dv_ref[last, :] = (cv[last, :] + grads[0][3]).astype(dv_ref.dtype)
            for w in range(qb):
                sl = pl.ds(w * WINDOW, WINDOW)
                ck[sl, :] = grads[w][2] + (grads[w + 1][1] if w + 1 < qb else 0.0)
                cv[sl, :] = grads[w][4] + (grads[w + 1][3] if w + 1 < qb else 0.0)
            dsk = functools.reduce(lambda u, v: u + v, [g[5] for g in grads])
            dgg = functools.reduce(lambda u, v: u + v, [g[6] for g in grads])
            _accumulate((b == 0) & (n == 0), (ds_ref, dg_ref), (dsk, dgg))

        @pl.when(n == steps)
        def _():
            dk_ref[...] = ck[...].astype(dk_ref.dtype)
            dv_ref[...] = cv[...].astype(dv_ref.dtype)

    at = lambda n: jnp.minimum(n, steps - 1)
    cur = lambda off: pl.BlockSpec((None, rows, KVW), lambda b, n: (b, at(n), off // KVW))
    prev = lambda off: pl.BlockSpec((None, WINDOW, KVW), lambda b, n: (b, jnp.maximum(at(n) * qb - 1, 0), off // KVW))
    kv_out = pl.BlockSpec((None, rows, KVW), lambda b, n: (b, jnp.maximum(n - 1, 0), 0))
    return pl.pallas_call(
        body, name=name, grid=(bsz, steps + 1),
        in_specs=[pl.BlockSpec((None, rows, ATW), lambda b, n: (b, at(n), OFF_Q // ATW)),
                  prev(OFF_K), cur(OFF_K), prev(OFF_VV), cur(OFF_VV),
                  pl.BlockSpec((None, rows, ATW), lambda b, n: (b, at(n), GMW // ATW)),
                  pl.BlockSpec((1, ATT_HEADS), lambda b, n: (0, 0)), pl.BlockSpec((1, ATW), lambda b, n: (0, 0))],
        out_specs=[pl.BlockSpec((None, rows, ATW), lambda b, n: (b, at(n), 0)), kv_out, kv_out,
                   pl.BlockSpec((1, ATT_HEADS), lambda b, n: (0, 0)), pl.BlockSpec((1, ATW), lambda b, n: (0, 0))],
        out_shape=[jax.ShapeDtypeStruct((bsz, s, ATW), BF16), jax.ShapeDtypeStruct((bsz, s, KVW), BF16),
                   jax.ShapeDtypeStruct((bsz, s, KVW), BF16), jax.ShapeDtypeStruct((1, ATT_HEADS), F32),
                   jax.ShapeDtypeStruct((1, ATW), F32)],
        scratch_shapes=[pltpu.VMEM((rows, KVW), F32), pltpu.VMEM((rows, KVW), F32)],
        compiler_params=_cp(("arbitrary", "arbitrary")),
    )(p, p, p, p, p, dmix, sinks, out_g)


CONV_CT = 256


def _shift_down(x, j):
    if j == 0:
        return x
    rows = lax.broadcasted_iota(jnp.int32, x.shape, 0)
    return jnp.where(rows >= j, pltpu.roll(x, j, 0), 0.0)


def _shift_up(x, j):
    if j == 0:
        return x
    s = x.shape[0]
    rows = lax.broadcasted_iota(jnp.int32, x.shape, 0)
    return jnp.where(rows < s - j, pltpu.roll(x, s - j, 0), 0.0)


def conv_fwd(p, w, bias, name):
    bsz, s, _ = p.shape

    def body(x_ref, w_ref, b_ref, o_ref):
        xv, wv = x_ref[...], w_ref[...]
        pre = b_ref[...] + sum(wv[k:k + 1, :] * _shift_down(xv, CONV_K - 1 - k) for k in range(CONV_K))
        o_ref[...] = _silu(pre)

    blk = pl.BlockSpec((None, s, CONV_CT), lambda b, j: (b, 0, j))
    src = pl.BlockSpec((None, s, CONV_CT), lambda b, j: (b, 0, OFF_XBC // CONV_CT + j))
    return pl.pallas_call(
        body, name=name, grid=(bsz, CCH // CONV_CT),
        in_specs=[src, pl.BlockSpec((CONV_K, CONV_CT), lambda b, j: (0, j)), pl.BlockSpec((1, CONV_CT), lambda b, j: (0, j))],
        out_specs=blk, out_shape=jax.ShapeDtypeStruct((bsz, s, CCH), F32),
        compiler_params=_cp(("parallel", "parallel")),
    )(p, w, bias)


def conv_bwd(p, dxc, w, bias, name):
    bsz, s, _ = p.shape

    def body(x_ref, d_ref, w_ref, b_ref, dx_ref, dw_ref, db_ref):
        b = pl.program_id(1)
        xv, wv = x_ref[...], w_ref[...]
        xs = [_shift_down(xv, CONV_K - 1 - k) for k in range(CONV_K)]
        pre = b_ref[...] + sum(wv[k:k + 1, :] * xs[k] for k in range(CONV_K))
        sg = 1.0 / (1.0 + jnp.exp(-pre))
        dpre = d_ref[...] * (sg * (1.0 + pre * (1.0 - sg)))
        dx_ref[...] = sum(wv[k:k + 1, :] * _shift_up(dpre, CONV_K - 1 - k) for k in range(CONV_K)).astype(dx_ref.dtype)
        p_w = jnp.concatenate([jnp.sum(dpre * xs[k], axis=0, keepdims=True) for k in range(CONV_K)], axis=0)
        p_b = jnp.sum(dpre, axis=0, keepdims=True)
        _accumulate(b == 0, (dw_ref, db_ref), (p_w, p_b))

    blk = pl.BlockSpec((None, s, CONV_CT), lambda j, b: (b, 0, j))
    src = pl.BlockSpec((None, s, CONV_CT), lambda j, b: (b, 0, OFF_XBC // CONV_CT + j))
    wsp = pl.BlockSpec((CONV_K, CONV_CT), lambda j, b: (0, j))
    bsp = pl.BlockSpec((1, CONV_CT), lambda j, b: (0, j))
    return pl.pallas_call(
        body, name=name, grid=(CCH // CONV_CT, bsz),
        in_specs=[src, blk, wsp, bsp], out_specs=[blk, wsp, bsp],
        out_shape=[jax.ShapeDtypeStruct((bsz, s, CCH), BF16), jax.ShapeDtypeStruct((CONV_K, CCH), F32),
                   jax.ShapeDtypeStruct((1, CCH), F32)],
        compiler_params=_cp(("parallel", "arbitrary")),
    )(p, dxc, w, bias)


def _ssd_consts():
    c = CHUNK
    r = lax.broadcasted_iota(jnp.int32, (c, c), 0)
    q = lax.broadcasted_iota(jnp.int32, (c, c), 1)
    hrow = lax.broadcasted_iota(jnp.int32, (LANES, SSW), 0)
    hcol = lax.broadcasted_iota(jnp.int32, (LANES, SSW), 1) // SSM_HD
    expand = (hrow == hcol).astype(F32)
    return expand, (r >= q).astype(F32), (r <= q).astype(F32), r >= q


def _ssd_chunk(xc, dtr, z, prev_t, dt_bias, a_log, d_skip, norm_g):
    c = xc.shape[0]
    expand, tril1, triu1, causal = _ssd_consts()
    xs, bm, cm = xc[:, :SSW], xc[:, SSW:SSW + BCW], xc[:, SSW + BCW:]
    dt = _softplus(dtr + dt_bias)
    da = dt * (-jnp.exp(a_log))
    a_cs = _hdot(tril1, da)
    a_cs_t = _hdot(da.T, triu1)
    dt_e = _hdot(dt, expand)
    acs_e = _hdot(a_cs, expand)
    alast_e = acs_e[c - 1:c, :]
    dsk_e = _hdot(jnp.broadcast_to(d_skip, (8, LANES)), expand)[0:1, :]
    xdt = xs * dt_e
    hg = SSM_HEADS // SSM_GROUPS
    ys, new_t = [], []
    for g in range(SSM_GROUPS):
        bg = bm[:, g * SSM_STATE:(g + 1) * SSM_STATE]
        cg = cm[:, g * SSM_STATE:(g + 1) * SSM_STATE]
        sl = slice(g * GRW, (g + 1) * GRW)
        cb = _nt(cg, bg)
        xdt_g = xdt[:, sl]
        st = _tn(bg, xdt_g * jnp.exp(alast_e[:, sl] - acs_e[:, sl]))
        new_t.append(prev_t[:, sl] * jnp.exp(alast_e[:, sl]) + st)
        y_off = _nn(cg, prev_t[:, sl]) * jnp.exp(acs_e[:, sl])
        yd = []
        low = lax.broadcasted_iota(jnp.int32, (c, LANES), 1) < SSM_HD
        for pair in range(hg // 2):
            xp = xdt_g[:, pair * LANES:(pair + 1) * LANES]
            acc = None
            for side, xh in enumerate((jnp.where(low, xp, 0.0), jnp.where(low, 0.0, xp))):
                h = g * hg + 2 * pair + side
                decay = jnp.exp(jnp.where(causal, a_cs[:, h:h + 1] - a_cs_t[h:h + 1, :], NEG))
                part = _nn(cb * decay, xh)
                acc = part if acc is None else acc + part
            yd.append(acc)
        ys.append(jnp.concatenate(yd, axis=1) + y_off)
    y = (jnp.concatenate(ys, axis=1) + xs * dsk_e) * _silu(z)
    yn = [y[:, g * GRW:(g + 1) * GRW] * lax.rsqrt(jnp.mean(jnp.square(y[:, g * GRW:(g + 1) * GRW]), axis=-1, keepdims=True) + EPS)
          for g in range(SSM_GROUPS)]
    return jnp.concatenate(yn, axis=1) * norm_g, jnp.concatenate(new_t, axis=1)


def ssd_fwd(xc, p, prm, name):
    bsz, s, _ = p.shape
    nc = s // CHUNK

    def body(xc_ref, dt_ref, *rest):
        z_refs, (db, al, dk, ng, o_ref, st_ref, state) = rest[:SSW // ZB], rest[SSW // ZB:]
        @pl.when(pl.program_id(0) == 0)
        def _():
            state[...] = jnp.zeros_like(state)

        for b in range(bsz):
            prev = state[b]
            st_ref[b, 0] = prev
            zb = jnp.concatenate([r[b] for r in z_refs], axis=1)
            out, new = _ssd_chunk(xc_ref[b], dt_ref[b], zb, prev, db[...], al[...], dk[...], ng[...])
            o_ref[b] = out.astype(o_ref.dtype)
            state[b] = new

    vec = pl.BlockSpec((1, LANES), lambda c: (0, 0))
    return pl.pallas_call(
        body, name=name, grid=(nc,),
        in_specs=[pl.BlockSpec((bsz, CHUNK, CCH), lambda c: (0, c, 0)),
                  pl.BlockSpec((bsz, CHUNK, LANES), lambda c: (0, c, OFF_DT // LANES)),
                  *[pl.BlockSpec((bsz, CHUNK, ZB), lambda c, i=i: (0, c, OFF_Z // ZB + i)) for i in range(SSW // ZB)],
                  vec, vec, vec, pl.BlockSpec((1, SSW), lambda c: (0, 0))],
        out_specs=[pl.BlockSpec((bsz, CHUNK, SSW), lambda c: (0, c, 0)),
                   pl.BlockSpec((bsz, 1, SSM_STATE, SSW), lambda c: (0, c, 0, 0))],
        out_shape=[jax.ShapeDtypeStruct((bsz, s, SSW), BF16), jax.ShapeDtypeStruct((bsz, nc, SSM_STATE, SSW), F32)],
        scratch_shapes=[pltpu.VMEM((bsz, SSM_STATE, SSW), F32)],
        compiler_params=_cp(("arbitrary",)),
    )(xc, p, *([p] * (SSW // ZB)), *prm)


def ssd_bwd(xc, p, states, dmix, prm, name):
    bsz, s, _ = p.shape
    nc = s // CHUNK

    def body(xc_ref, dt_ref, *rest):
        z_refs, (st_ref, do_ref, db, al, dk, ng, dxc_ref, ddt_ref, dz_ref) = rest[:SSW // ZB], rest[SSW // ZB:SSW // ZB + 9]
        rest = rest[SSW // ZB + 9:]
        dpar, dstate = rest[:4], rest[4]
        c = pl.program_id(0)

        @pl.when(c == 0)
        def _():
            dstate[...] = jnp.zeros_like(dstate)

        dpars = None
        for b in range(bsz):
            zb = jnp.concatenate([r[b] for r in z_refs], axis=1)
            _, vjp = jax.vjp(_ssd_chunk, xc_ref[b], dt_ref[b], zb, st_ref[b, 0], db[...], al[...], dk[...], ng[...])
            gr = vjp((do_ref[b], dstate[b]))
            dxc_ref[b] = gr[0]
            ddt_ref[b] = gr[1].astype(ddt_ref.dtype)
            dz_ref[b] = gr[2].astype(dz_ref.dtype)
            dstate[b] = gr[3]
            dpars = gr[4:] if dpars is None else [u + v for u, v in zip(dpars, gr[4:])]
        _accumulate(c == 0, dpar, dpars)

    rv = lambda c: nc - 1 - c
    vec = pl.BlockSpec((1, LANES), lambda c: (0, 0))
    ngs = pl.BlockSpec((1, SSW), lambda c: (0, 0))
    return pl.pallas_call(
        body, name=name, grid=(nc,),
        in_specs=[pl.BlockSpec((bsz, CHUNK, CCH), lambda c: (0, rv(c), 0)),
                  pl.BlockSpec((bsz, CHUNK, LANES), lambda c: (0, rv(c), OFF_DT // LANES)),
                  *[pl.BlockSpec((bsz, CHUNK, ZB), lambda c, i=i: (0, rv(c), OFF_Z // ZB + i)) for i in range(SSW // ZB)],
                  pl.BlockSpec((bsz, 1, SSM_STATE, SSW), lambda c: (0, rv(c), 0, 0)),
                  pl.BlockSpec((bsz, CHUNK, SSW), lambda c: (0, rv(c), (GMW + ATW) // SSW)),
                  vec, vec, vec, ngs],
        out_specs=[pl.BlockSpec((bsz, CHUNK, CCH), lambda c: (0, rv(c), 0)),
                   pl.BlockSpec((bsz, CHUNK, LANES), lambda c: (0, rv(c), 0)),
                   pl.BlockSpec((bsz, CHUNK, SSW), lambda c: (0, rv(c), 0)),
                   vec, vec, vec, ngs],
        out_shape=[jax.ShapeDtypeStruct((bsz, s, CCH), F32), jax.ShapeDtypeStruct((bsz, s, LANES), BF16),
                   jax.ShapeDtypeStruct((bsz, s, SSW), BF16)] + [jax.ShapeDtypeStruct((1, LANES), F32)] * 3
                  + [jax.ShapeDtypeStruct((1, SSW), F32)],
        scratch_shapes=[pltpu.VMEM((bsz, SSM_STATE, SSW), F32)],
        compiler_params=_cp(("arbitrary",)),
    )(xc, p, *([p] * (SSW // ZB)), states, dmix, *prm)


def _rows2d(a):
    return a.reshape(-1, a.shape[-1])


def _ew_tile(r, c):
    t = r
    while t * c > (1 << 20) and t % 16 == 0:
        t //= 2
    return t


def add_pair(g, theirs, chip, core, name):
    k, r, c = g.shape
    h = r // 2
    tr = _ew_tile(h, c)
    nb = h // tr

    def body(s_ref, a_ref, b_ref, own_ref, ob_ref):
        s = a_ref[...] + b_ref[...]
        ob_ref[...] = s.astype(ob_ref.dtype)

        @pl.when(pl.program_id(1) == s_ref[1])
        def _():
            own_ref[...] = s

    blk = pl.BlockSpec((None, tr, c), lambda i, kk, sr: (kk, i, 0))
    return pl.pallas_call(
        body, name=name,
        grid_spec=pltpu.PrefetchScalarGridSpec(
            num_scalar_prefetch=1, grid=(nb, k),
            in_specs=[pl.BlockSpec((None, tr, c), lambda i, kk, sr: (kk, sr[0] * nb + i, 0)), blk],
            out_specs=[pl.BlockSpec((tr, c), lambda i, kk, sr: (i, 0)), blk]),
        out_shape=[jax.ShapeDtypeStruct((h, c), F32), jax.ShapeDtypeStruct(theirs.shape, BF16)],
        compiler_params=_cp(("parallel", "arbitrary")),
    )(jnp.stack([core, chip]).astype(jnp.int32), g, theirs)


def sum_own_recv(own, recv, core, name):
    h, c = own.shape
    tr = _ew_tile(h, c)
    nb = h // tr

    def body(k_ref, o_ref, r_ref, out_ref):
        s = o_ref[...]
        for j in range(3):
            s = s + r_ref[j].astype(F32)
        out_ref[...] = s

    return pl.pallas_call(
        body, name=name,
        grid_spec=pltpu.PrefetchScalarGridSpec(
            num_scalar_prefetch=1, grid=(nb,),
            in_specs=[pl.BlockSpec((tr, c), lambda i, kr: (i, 0)),
                      pl.BlockSpec((3, tr, c), lambda i, kr: (0, i, 0))],
            out_specs=pl.BlockSpec((tr, c), lambda i, kr: (kr[0] * nb + i, 0))),
        out_shape=jax.ShapeDtypeStruct((2 * h, c), F32),
        compiler_params=_cp(("parallel",)),
    )(core.reshape(1).astype(jnp.int32), own, recv)


def _adam_math(w, m, v, g):
    mn = ADAM_B1 * m + (1.0 - ADAM_B1) * g
    vn = ADAM_B2 * v + (1.0 - ADAM_B2) * (g * g)
    mh = mn / (1.0 - ADAM_B1 ** ADAM_STEP)
    vh = vn / (1.0 - ADAM_B2 ** ADAM_STEP)
    return -ADAM_LR * (mh / (jnp.sqrt(vh) + ADAM_EPS) + ADAM_WD * w), mn, vn


def adamw_minor_rows(w, m, v, g, name):
    r, nl, c = w.shape
    tr = max(t for t in range(1, r + 1) if r % t == 0 and t * nl * c <= (1 << 19))

    def body(w_ref, m_ref, v_ref, g_ref, d_ref, mo_ref, vo_ref):
        d_ref[...], mo_ref[...], vo_ref[...] = _adam_math(w_ref[...], m_ref[...], v_ref[...], g_ref[...])

    blk = pl.BlockSpec((tr, nl, c), lambda i: (i, 0, 0))
    return pl.pallas_call(
        body, name=name, grid=(r // tr,), in_specs=[blk] * 4, out_specs=[blk] * 3,
        out_shape=[jax.ShapeDtypeStruct(w.shape, F32)] * 3, compiler_params=_cp(("parallel",)),
    )(w, m, v, g)


def adamw_layer(w, m, v, g, layer, prev, name):
    nl, r, c = w.shape
    tr = _ew_tile(r, c * 2)

    def body(w_ref, m_ref, v_ref, g_ref, *rest):
        go_ref, d_ref, mo_ref, vo_ref = rest[-4:]
        gv = g_ref[...]
        dl, mn, vn = _adam_math(w_ref[...], m_ref[...], v_ref[...], gv)
        go_ref[...] = gv
        d_ref[...] = dl
        mo_ref[...] = mn
        vo_ref[...] = vn

    lay = pl.BlockSpec((None, tr, c), lambda i: (layer, i, 0))
    n_prev = 0 if prev is None else 4
    return pl.pallas_call(
        body, name=name, grid=(r // tr,),
        in_specs=[lay, lay, lay, pl.BlockSpec((tr, c), lambda i: (i, 0))] + [_ANY] * n_prev,
        out_specs=[lay] * 4, out_shape=[jax.ShapeDtypeStruct(w.shape, F32)] * 4,
        input_output_aliases={4 + i: i for i in range(n_prev)},
        compiler_params=_cp(("parallel",)),
    )(w, m, v, g, *(prev or ()))


def sum_devices(parts, name):
    n, r, c = parts.shape
    tr = _ew_tile(r, c * n)

    def body(p_ref, o_ref):
        s = p_ref[0]
        for j in range(1, n):
            s = s + p_ref[j]
        o_ref[...] = s

    return pl.pallas_call(
        body, name=name, grid=(r // tr,),
        in_specs=[pl.BlockSpec((n, tr, c), lambda i: (0, i, 0))],
        out_specs=pl.BlockSpec((tr, c), lambda i: (i, 0)),
        out_shape=jax.ShapeDtypeStruct((r, c), F32),
        compiler_params=_cp(("parallel",)),
    )(parts)


def adamw(w, m, v, g, name):
    r, c = w.shape
    tr = _ew_tile(r, c * 2)

    def body(w_ref, m_ref, v_ref, g_ref, d_ref, mo_ref, vo_ref):
        gv = g_ref[...]
        mn = ADAM_B1 * m_ref[...] + (1.0 - ADAM_B1) * gv
        vn = ADAM_B2 * v_ref[...] + (1.0 - ADAM_B2) * (gv * gv)
        mh = mn / (1.0 - ADAM_B1 ** ADAM_STEP)
        vh = vn / (1.0 - ADAM_B2 ** ADAM_STEP)
        d_ref[...] = -ADAM_LR * (mh / (jnp.sqrt(vh) + ADAM_EPS) + ADAM_WD * w_ref[...])
        mo_ref[...] = mn
        vo_ref[...] = vn

    blk = pl.BlockSpec((tr, c), lambda i: (i, 0))
    return pl.pallas_call(
        body, name=name, grid=(r // tr,), in_specs=[blk] * 4, out_specs=[blk] * 3,
        out_shape=[jax.ShapeDtypeStruct((r, c), F32)] * 3,
        compiler_params=_cp(("parallel",)),
    )(w, m, v, g)


def _place():
    x, y, c = lax.axis_index("x"), lax.axis_index("y"), lax.axis_index("c")
    chips = [(1 - x, y), (x, 1 - y), (1 - x, 1 - y)]
    return x, y, c, chips


def all_gather_small(v, name):
    r, w = v.shape

    def body(x_ref, out_ref, send_sems, recv_sems, local_sem):
        x, y, c, chips = _place()
        me, sibling = (x, y, c), (x, y, 1 - c)

        def rows(px, py, pc):
            return out_ref.at[pl.ds((4 * px + 2 * py + pc) * r, r), :]

        def copy(k, block, to, src=None):
            return pltpu.make_async_remote_copy(
                src_ref=rows(*block) if src is None else src, dst_ref=rows(*block),
                send_sem=send_sems.at[k], recv_sem=recv_sems.at[k], device_id=to, device_id_type=MESH)

        mine = pltpu.make_async_copy(x_ref, rows(*me), local_sem)
        mine.start()
        first = [copy(0, me, sibling, src=x_ref)]
        first += [copy(1 + j, me, (*chip, c), src=x_ref) for j, chip in enumerate(chips)]
        for cp in first:
            cp.start()
        passed = [copy(4 + j, (*chip, c), sibling) for j, chip in enumerate(chips)]
        for j, chip in enumerate(chips):
            copy(1 + j, (*chip, c), me).wait_recv()
            passed[j].start()
        copy(0, sibling, me).wait_recv()
        for j, chip in enumerate(chips):
            copy(4 + j, (*chip, 1 - c), me).wait_recv()
        for cp in first + passed:
            cp.wait_send()
        mine.wait()

    out = pl.pallas_call(
        body, name=name, out_shape=jax.ShapeDtypeStruct((8 * r, w), v.dtype),
        in_specs=[pl.BlockSpec(memory_space=pltpu.VMEM)], out_specs=pl.BlockSpec(memory_space=pltpu.VMEM),
        scratch_shapes=[pltpu.SemaphoreType.DMA((7,)), pltpu.SemaphoreType.DMA((7,)), pltpu.SemaphoreType.DMA],
        compiler_params=pltpu.CompilerParams(vmem_limit_bytes=VMEM_LIMIT),
    )(v)
    return out.reshape(8, r, w)


_HBM = pl.BlockSpec(memory_space=pltpu.HBM)


_SEM = pl.BlockSpec(memory_space=pltpu.SEMAPHORE)
_ANY = pl.BlockSpec(memory_space=pl.ANY)
_EFFECT = pltpu.SideEffectType.DATAFLOW_SIDE_EFFECTING


def _hbm(a):
    return pltpu.with_memory_space_constraint(a, pltpu.HBM)


def split_copy_start(srcs, land_shapes, copies, after, name):
    n, nl = len(srcs), len(land_shapes)
    n_after = 0 if after is None else 1
    ncopy = [0]

    def body(*refs):
        ins, lands = refs[:n], refs[n:n + nl]
        send_sems, recv_sems = refs[n + nl + n_after], refs[n + nl + n_after + 1]
        token = refs[-1]
        x, y, c, chips = _place()
        for k, (src, dst, to) in enumerate(copies(x, y, c, chips, ins, lands)):
            pltpu.make_async_remote_copy(src_ref=src, dst_ref=dst, send_sem=send_sems.at[k], recv_sem=recv_sems.at[k],
                                         device_id=to, device_id_type=MESH).start()
        token[...] = jnp.zeros_like(token)

    ncopy[0] = len(copies(0, 0, 0, [(1, 0), (0, 1), (1, 1)], [None] * n, [None] * nl, count_only=True))
    k = ncopy[0]
    lands = [_hbm(lax.empty(s.shape, s.dtype)) for s in land_shapes]
    res = pl.pallas_call(
        body, name=name,
        out_shape=(pltpu.SemaphoreType.DMA((k,)), pltpu.SemaphoreType.DMA((k,)))
        + tuple(pltpu.HBM(s.shape, s.dtype) for s in srcs) + tuple(pltpu.HBM(s.shape, s.dtype) for s in land_shapes)
        + (jax.ShapeDtypeStruct((8, LANES), F32),),
        in_specs=[_HBM] * (n + nl) + [_ANY] * n_after,
        out_specs=(_SEM, _SEM) + (_HBM,) * (n + nl) + (pl.BlockSpec(memory_space=pltpu.VMEM),),
        input_output_aliases={i: 2 + i for i in range(n + nl)},
        compiler_params=pltpu.CompilerParams(has_side_effects=_EFFECT),
    )(*[_hbm(s) for s in srcs], *lands, *([after] if n_after else []))
    return res[0], res[1], list(res[2:2 + n]), list(res[2 + n:2 + n + nl]), res[-1]


def split_copy_wait(send_sems, recv_sems, srcs, lands, copies, after, name):
    n, nl = len(srcs), len(lands)

    def body(*refs):
        ins, lnd = refs[:n], refs[n:n + nl]
        ss, rs = refs[n + nl], refs[n + nl + 1]
        x, y, c, chips = _place()
        for k, (src, dst, to) in enumerate(copies(x, y, c, chips, ins, lnd, receive=True)):
            cp = pltpu.make_async_remote_copy(src_ref=src, dst_ref=dst, send_sem=ss.at[k], recv_sem=rs.at[k],
                                              device_id=to, device_id_type=MESH)
            cp.wait_send()
            cp.wait_recv()

    res = pl.pallas_call(
        body, name=name,
        out_shape=tuple(pltpu.HBM(s.shape, s.dtype) for s in srcs) + tuple(pltpu.HBM(s.shape, s.dtype) for s in lands),
        in_specs=[_HBM] * (n + nl) + [_SEM, _SEM, _ANY], out_specs=(_HBM,) * (n + nl),
        input_output_aliases={i: i for i in range(n + nl)},
        compiler_params=pltpu.CompilerParams(has_side_effects=_EFFECT),
    )(*srcs, *lands, send_sems, recv_sems, after)
    return list(res[:n]), list(res[n:])


def _gather_copies(x, y, c, chips, ins, lands, receive=False, count_only=False):
    out = []
    for i in range(len(ins)):
        for cx, cy in chips:
            if count_only:
                out.append(None)
                continue
            h = ins[i].shape[0] // 2
            rows = pl.ds(c * h, h)
            k_dst = (2 * cx + cy) if receive else (2 * x + y)
            out.append((ins[i].at[rows, :], lands[i].at[k_dst, rows, :], (cx, cy, c)))
    for i in range(len(ins)):
        out.append(None if count_only else (ins[i], lands[i].at[2 * x + y], (x, y, 1 - c)))
    return out


def _swap_copies(x, y, c, chips, ins, lands, receive=False, count_only=False):
    out = []
    for i in range(len(ins)):
        if count_only:
            out.append(None)
            continue
        h = ins[i].shape[1] // 2
        out.append((ins[i].at[:, pl.ds((1 - c) * h, h), :], lands[i], (x, y, 1 - c)))
    return out


def _scatter_copies(x, y, c, chips, ins, lands, receive=False, count_only=False):
    out = []
    for i in range(len(ins)):
        for j, (cx, cy) in enumerate(chips):
            if count_only:
                out.append(None)
                continue
            out.append((ins[i].at[2 * cx + cy], lands[i].at[j], (cx, cy, c)))
    return out


def forward_halves(lands, name):
    n = len(lands)

    def body(*refs):
        ins, outs = refs[:n], refs[n:2 * n]
        send_sems, recv_sems = refs[2 * n:]
        x, y, c, chips = _place()
        sibling = (x, y, 1 - c)
        sent = []
        for i in range(n):
            h = ins[i].shape[1] // 2
            for j, (cx, cy) in enumerate(chips):
                blk = ins[i].at[2 * cx + cy, pl.ds(c * h, h), :]
                sent.append(pltpu.make_async_remote_copy(
                    src_ref=blk, dst_ref=outs[i].at[2 * cx + cy, pl.ds(c * h, h), :], send_sem=send_sems.at[3 * i + j],
                    recv_sem=recv_sems.at[3 * i + j], device_id=sibling, device_id_type=MESH))
                sent[-1].start()
        for i in range(n):
            h = ins[i].shape[1] // 2
            for j, (cx, cy) in enumerate(chips):
                theirs = outs[i].at[2 * cx + cy, pl.ds((1 - c) * h, h), :]
                pltpu.make_async_remote_copy(
                    src_ref=theirs, dst_ref=theirs, send_sem=send_sems.at[3 * i + j], recv_sem=recv_sems.at[3 * i + j],
                    device_id=sibling, device_id_type=MESH).wait_recv()
        for cp in sent:
            cp.wait_send()

    return pl.pallas_call(
        body, name=name, out_shape=[jax.ShapeDtypeStruct(s.shape, s.dtype) for s in lands],
        in_specs=[_HBM] * n, out_specs=[_HBM] * n, input_output_aliases={i: i for i in range(n)},
        scratch_shapes=[pltpu.SemaphoreType.DMA((3 * n,)), pltpu.SemaphoreType.DMA((3 * n,))],
    )(*lands)


def join_halves(halves, name):
    n = len(halves)

    def body(*refs):
        ins, outs = refs[:n], refs[n:2 * n]
        send_sems, recv_sems = refs[2 * n:]
        x, y, c, _ = _place()
        sibling = (x, y, 1 - c)
        sent = []
        for i in range(n):
            h = ins[i].shape[0] // 2
            sent.append(pltpu.make_async_remote_copy(
                src_ref=ins[i].at[pl.ds(c * h, h), :], dst_ref=outs[i].at[pl.ds(c * h, h), :], send_sem=send_sems.at[i],
                recv_sem=recv_sems.at[i], device_id=sibling, device_id_type=MESH))
            sent[-1].start()
        for i in range(n):
            h = ins[i].shape[0] // 2
            theirs = outs[i].at[pl.ds((1 - c) * h, h), :]
            pltpu.make_async_remote_copy(
                src_ref=theirs, dst_ref=theirs, send_sem=send_sems.at[i],
                recv_sem=recv_sems.at[i], device_id=sibling, device_id_type=MESH).wait_recv()
        for cp in sent:
            cp.wait_send()

    return pl.pallas_call(
        body, name=name, out_shape=[jax.ShapeDtypeStruct(s.shape, F32) for s in halves],
        in_specs=[_HBM] * n, out_specs=[_HBM] * n, input_output_aliases={i: i for i in range(n)},
        scratch_shapes=[pltpu.SemaphoreType.DMA((n,)), pltpu.SemaphoreType.DMA((n,))],
    )(*halves)


_PACK_ROWS = 8 * LANES


def _pack(arrs):
    flat = jnp.concatenate([a.reshape(-1).astype(F32) for a in arrs])
    pad = (-flat.shape[0]) % _PACK_ROWS
    return jnp.pad(flat, (0, pad)).reshape(-1, LANES)


def _unpack(flat, shapes):
    flat = flat.reshape(-1)
    out, off = [], 0
    for s in shapes:
        n = int(np.prod(s))
        out.append(flat[off:off + n].reshape(s))
        off += n
    return out


def _win_from_blocks(g):
    d = g.shape[1]
    return jnp.pad(g.transpose(1, 0, 2).reshape(d, IN_W), ((0, 0), (0, PW - IN_W)))


def _win_to_blocks(w):
    return w[:, :IN_W].reshape(w.shape[0], 4, IN_W // 4).transpose(1, 0, 2)


def _relu2(a):
    r = jnp.maximum(a, 0)
    return r * r


def kernel(x, c, ada_w, ada_b, norm1_g, w_in, gm_ln_g, gm_ln_b, gm_ws, gm_bs, gm_norm_g, attn_sinks, attn_norm_g, conv_w, conv_b, dt_bias, a_log, d_skip, ssm_norm_g, w_out, norm2_g, w_mlp1, w_mlp2, final_norm_g, loss_target, m_ada_w, m_ada_b, m_norm1_g, m_w_in, m_gm_ln_g, m_gm_ln_b, m_gm_ws, m_gm_bs, m_gm_norm_g, m_attn_sinks, m_attn_norm_g, m_conv_w, m_conv_b, m_dt_bias, m_a_log, m_d_skip, m_ssm_norm_g, m_w_out, m_norm2_g, m_w_mlp1, m_w_mlp2, m_final_norm_g, v_ada_w, v_ada_b, v_norm1_g, v_w_in, v_gm_ln_g, v_gm_ln_b, v_gm_ws, v_gm_bs, v_gm_norm_g, v_attn_sinks, v_attn_norm_g, v_conv_w, v_conv_b, v_dt_bias, v_a_log, v_d_skip, v_ssm_norm_g, v_w_out, v_norm2_g, v_w_mlp1, v_w_mlp2, v_final_norm_g):
    nl = ada_w.shape[0]
    bl, s, d = x.shape
    t = bl * s
    dff4 = w_mlp1.shape[2]
    dff = 4 * dff4
    mod_w = ada_w.shape[2]
    cw_w = conv_w.shape[2]
    xi, yi, ci = lax.axis_index("x"), lax.axis_index("y"), lax.axis_index("c")
    chip = 2 * xi + yi
    dev = 2 * chip + ci
    nex = 8 * bl

    shards = [[w_in[l].astype(BF16), w_out[l].astype(BF16), w_mlp1[l].astype(BF16), w_mlp2[l].astype(BF16)]
              for l in range(nl)]
    groups = [[shards[0][i]] for i in range(4)] + [shards[l] for l in range(1, nl)]

    def start_gather(gi, behind):
        ss, rs, srcs, lands, token = split_copy_start(
            groups[gi], [jax.ShapeDtypeStruct((4,) + a.shape, a.dtype) for a in groups[gi]], _gather_copies, behind,
            f"gather_start_{gi}")
        return (ss, rs, srcs, lands), token

    first_gather, first_token = start_gather(0, None)
    g0 = all_gather_small(_pack([c, conv_w]) + first_token[0, 0], "ag_c")
    g0 = g0.reshape(8, -1)
    c_all = g0[:, :bl * d].reshape(nex, d)
    cw_parts = g0[0::2, bl * d:bl * d + conv_w.size].reshape(4, nl, CONV_K, cw_w)
    conv_w_full = cw_parts.transpose(1, 2, 0, 3).reshape(nl, CONV_K, CCH)

    def c_act(a):
        return _silu(a).astype(BF16)

    def to_bf16(a):
        return a.astype(BF16)

    mod_parts = []
    for l in range(nl):
        bias = lax.dynamic_slice(ada_b[l].reshape(1, -1), (0, chip * mod_w), (1, mod_w))
        mod_parts.append(_mm("nn", c_all, ada_w, dims=(nex, mod_w, d), tm=nex, tn=512, tk=d, out_dtypes=[F32],
                             name=f"mod_{l}", pro_a=c_act, pro_b=to_bf16,
                             b_spec=pl.BlockSpec((None, d, 512), lambda i, j, kk, l=l: (l, kk, j)),
                             extras=[(bias, pl.BlockSpec((1, 512), lambda i, j, kk: (0, j)))],
                             epi=lambda acc, bv: (acc + bv,))[0])
    g1 = all_gather_small(_pack(mod_parts), "ag_mod").reshape(8, -1)
    mod_all = g1[0::2, :nl * nex * mod_w].reshape(4, nl, nex, mod_w).transpose(1, 2, 0, 3).reshape(nl, nex, 4 * mod_w)
    mod = lax.dynamic_slice(mod_all, (0, dev * bl, 0), (nl, bl, 4 * mod_w))
    mods = [[mod[l, :, i * d:(i + 1) * d].reshape(bl, 1, d) for i in range(6)] for l in range(nl)]

    pending, after = [first_gather], g1
    for gi in range(1, len(groups)):
        state, after = start_gather(gi, after)
        pending.append(state)
    mods[0][0] = mods[0][0] + after[0, 0]

    def fetch(gi, behind):
        ss, rs, srcs, lands = pending[gi]
        srcs, lands = split_copy_wait(ss, rs, srcs, lands, _gather_copies, behind, f"gather_wait_{gi}")
        return forward_halves(lands, f"gather_pass_{gi}")

    as_win = _win_from_blocks

    wfull = [None] * nl
    row = lambda a: a.reshape(1, -1)
    pad16 = lambda a: jnp.pad(a.reshape(1, -1), ((0, 0), (0, LANES - SSM_HEADS)))
    tm_res = min(1024, s)

    def residual(acc, xt, gt):
        return acc, xt + gt * acc

    def res_extras(xin, gate, tm=tm_res):
        return [(xin.reshape(t, d), pl.BlockSpec((tm, 512), lambda i, j, kk: (i, j))),
                (gate, pl.BlockSpec((None, 1, 512), lambda i, j, kk: (i * tm // s, 0, j)))]

    w1_blk = lambda tk, tn: pl.BlockSpec((None, tk, tn), lambda i, j, kk: (j // (dff4 // tn), kk, j % (dff4 // tn)))

    saved = []
    xcur = x
    for l in range(nl):
        sh1, sc1, gt1, sh2, sc2, gt2 = mods[l]
        if l == 0:
            win = as_win(fetch(0, mod)[0])
        else:
            g_in, g_out, w1, g_2 = fetch(3 + l, xcur)
            win, wout, w2 = as_win(g_in), g_out.reshape(-1, d), g_2.reshape(dff, d)
        prm_a = (row(gm_ln_g[l]), row(gm_ln_b[l]), gm_ws[l], gm_bs[l].T, row(gm_norm_g[l]))
        prm_b = (row(attn_sinks[l]), row(attn_norm_g[l]))
        prm_c = (pad16(dt_bias[l]), pad16(a_log[l]), pad16(d_skip[l]), row(ssm_norm_g[l]))
        h1 = ln_mod_fwd(xcur, row(norm1_g[l]), sc1, sh1, f"ln1_fwd_{l}")
        p = _mm("nn", h1.reshape(t, d), win, dims=(t, PW, d), tm=1024, tn=512, tk=d, out_dtypes=[F32],
                name=f"proj_in_{l}")[0].reshape(bl, s, PW)
        out_a = gmlp_fwd(p, prm_a, f"gmlp_fwd_{l}")
        out_b = attn_fwd(p, *prm_b, f"attn_fwd_{l}")
        xc = conv_fwd(p, conv_w_full[l], row(conv_b[l]), f"conv_fwd_{l}")
        out_c, states = ssd_fwd(xc, p, prm_c, f"ssd_fwd_{l}")
        mix = jnp.concatenate([out_a, out_b, out_c], axis=-1)
        if l == 0:
            wout = fetch(1, mix)[0].reshape(-1, d)
        mm1, x2 = _mm("nn", mix.reshape(t, d), wout, dims=(t, d, d), tm=tm_res, tn=512, tk=d, out_dtypes=[F32, F32],
                      name=f"proj_out_{l}", extras=res_extras(xcur, gt1), epi=residual)
        x2 = x2.reshape(bl, s, d)
        h2 = ln_mod_fwd(x2, row(norm2_g[l]), sc2, sh2, f"ln2_fwd_{l}")
        if l == 0:
            w1 = fetch(2, h2)[0]
        a1 = _mm("nn", h2.reshape(t, d), w1, dims=(t, dff, d), tm=1024, tn=512, tk=d, out_dtypes=[BF16],
                 name=f"mlp1_{l}", b_spec=w1_blk(d, 512))[0]
        if l == 0:
            w2 = fetch(3, a1)[0].reshape(dff, d)
        tm2 = min(512, s)
        mm2, x3 = _mm("nn", a1, w2, dims=(t, d, dff), tm=tm2, tn=512, tk=dff, out_dtypes=[F32, F32],
                      name=f"mlp2_{l}", extras=res_extras(x2, gt2, tm2), epi=residual, pro_a=_relu2)
        x3 = x3.reshape(bl, s, d)
        wfull[l] = (win, wout, w1, w2)
        saved.append((xcur, h1, p, xc, states, mix, mm1.reshape(bl, s, d), x2, h2, a1, mm2.reshape(bl, s, d),
                      prm_a, prm_b, prm_c))
        xcur = x3

    dx, d_final_g, loss_part = loss_head(xcur, row(final_norm_g), loss_target, "loss_head")
    loss = lax.psum(loss_part[0, 0], ("x", "y", "c"))

    def rs_swap(grads, tag):
        ss, rs, srcs, lands, token = split_copy_start(
            grads, [jax.ShapeDtypeStruct((4, g.shape[1] // 2, g.shape[2]), F32) for g in grads],
            _swap_copies, None, f"rs_swap_{tag}")
        return (ss, rs, srcs, lands), token

    def rs_begin(swap_state, tag, swapped_behind, start_behind=None):
        ss, rs, srcs, lands = swap_state
        grads, theirs = split_copy_wait(ss, rs, srcs, lands, _swap_copies, swapped_behind, f"rs_swapped_{tag}")
        sums = [add_pair(g, th, chip, ci, f"rs_add_{tag}_{i}") for i, (g, th) in enumerate(zip(grads, theirs))]
        ss, rs, srcs, lands, token = split_copy_start(
            [sm[1] for sm in sums], [jax.ShapeDtypeStruct((3,) + sm[1].shape[1:], BF16) for sm in sums],
            _scatter_copies, sums[0][0] if start_behind is None else start_behind, f"rs_start_{tag}")
        return (ss, rs, srcs, lands, [sm[0] for sm in sums]), token

    def rs_end(state, behind, tag):
        ss, rs, srcs, lands, sums_f32 = state
        _, got = split_copy_wait(ss, rs, srcs, lands, _scatter_copies, behind, f"rs_wait_{tag}")
        halves = [sum_own_recv(sf, g, ci, f"rs_sum_{tag}_{i}") for i, (sf, g) in enumerate(zip(sums_f32, got))]
        return join_halves(halves, f"rs_join_{tag}")

    small_parts = [None] * nl
    dmods = [None] * nl
    reduced = [[None] * 4 for _ in range(nl)]
    pending_rs, rs_token = [], None
    part_slots = {"a": (0, 1), "m": (2, 3)}

    def finish(behind):
        for ll, part, state in pending_rs:
            for slot, blk in zip(part_slots[part], rs_end(state, behind, f"{ll}{part}")):
                reduced[ll][slot] = blk
        pending_rs.clear()

    for l in reversed(range(nl)):
        sh1, sc1, gt1, sh2, sc2, gt2 = mods[l]
        win, wout, w1, w2 = wfull[l]
        xin, h1, p, xc, states, mix, mm1, x2, h2, a1, mm2, prm_a, prm_b, prm_c = saved[l]
        if rs_token is not None:
            gt2 = gt2 + rs_token[0, 0]
        dm2, dgt2 = gate_bwd(dx, mm2, gt2, f"gate2_bwd_{l}")
        dm2 = dm2.reshape(t, d)
        da1 = _mm("nt", dm2, w2, dims=(t, dff, d), tm=1024, tn=512, tk=d, out_dtypes=[BF16], name=f"mlp2_dx_{l}",
                  extras=[(a1, pl.BlockSpec((1024 if t >= 1024 else t, 512), lambda i, j, kk: (i, j)))],
                  epi=lambda acc, av: (acc * (2.0 * jnp.maximum(av, 0).astype(F32)),))[0]
        dw2 = _mm("tn", a1, dm2, dims=(dff, d, t), tm=512, tn=d, tk=2048, out_dtypes=[F32], name=f"mlp2_dw_{l}",
                  pro_a=_relu2, out_shapes=[(4, dff4, d)],
                  out_specs=[pl.BlockSpec((None, 512, d), lambda i, j, kk: (i // (dff4 // 512), i % (dff4 // 512), 0))])[0]
        dw1 = _mm("tn", h2.reshape(t, d), da1, dims=(d, dff, t), tm=512, tn=dff4, tk=2048, out_dtypes=[F32],
                  name=f"mlp1_dw_{l}", out_shapes=[(4, d, dff4)],
                  out_specs=[pl.BlockSpec((None, 512, dff4), lambda i, j, kk: (j, i, 0))])[0]
        swap_state, swap_token = rs_swap([dw1, dw2], f"{l}m")
        dh2 = _mm_nt_blocked(da1, w1, tm=512, tn=512, name=f"mlp1_dx_{l}", behind=swap_token)
        mlp_state, mlp_token = rs_begin(swap_state, f"{l}m", dh2)
        sc2 = sc2 + mlp_token[0, 0]
        dx2, dsc2, dsh2, dn2 = ln_mod_bwd(dh2.reshape(bl, s, d), x2, dx, row(norm2_g[l]), sc2, f"ln2_bwd_{l}")
        dm1, dgt1 = gate_bwd(dx2, mm1, gt1, f"gate1_bwd_{l}")
        dm1 = dm1.reshape(t, d)
        dmix = _mm("nt", dm1, wout, dims=(t, d, d), tm=1024, tn=512, tk=d, out_dtypes=[F32],
                   name=f"proj_out_dx_{l}")[0].reshape(bl, s, d)
        dwout = _mm("tn", mix.reshape(t, d), dm1, dims=(d, d, t), tm=512, tn=d, tk=2048, out_dtypes=[F32],
                    name=f"proj_out_dw_{l}", out_shapes=[(4, d // 4, d)],
                    out_specs=[pl.BlockSpec((None, 512, d), lambda i, j, kk: (i // (d // 4 // 512), i % (d // 4 // 512), 0))])[0]
        du, dv, dlg, dlb, dws, dbst, dgng = gmlp_bwd(p, dmix, prm_a, f"gmlp_bwd_{l}")
        dq, dk, dvv, dsinks, dang = attn_bwd(p, dmix, *prm_b, f"attn_bwd_{l}")
        dxc, ddt, dz, ddtb, dalog, ddsk, dsng = ssd_bwd(xc, p, states, dmix, prm_c, f"ssd_bwd_{l}")
        dxbc, dcw, dcb = conv_bwd(p, dxc, conv_w_full[l], row(conv_b[l]), f"conv_bwd_{l}")
        dp = jnp.concatenate([du, dv, dq, dk, dvv, dz, dxbc, ddt, jnp.zeros((bl, s, PW - OFF_DT - LANES), BF16)],
                             axis=-1).reshape(t, PW)
        dwin = _mm("tn", h1.reshape(t, d), dp, dims=(d, PW, t), tm=512, tn=PW // 2, tk=2048, out_dtypes=[F32],
                   name=f"proj_in_dw_{l}")[0]
        dwin_blocks = _win_to_blocks(dwin)
        mixer_swap, swap_token = rs_swap([dwin_blocks, dwout], f"{l}a")
        dh1 = _mm("nt", dp, win, dims=(t, d, PW), tm=1024, tn=512, tk=PW, out_dtypes=[F32],
                  name=f"proj_in_dx_{l}", behind=swap_token)[0]
        dx, dsc1, dsh1, dn1 = ln_mod_bwd(dh1.reshape(bl, s, d), xin, dx2, row(norm1_g[l]), sc1, f"ln1_bwd_{l}")
        dmods[l] = jnp.concatenate([dsh1, dsc1, dgt1, dsh2, dsc2, dgt2], axis=-1).reshape(bl, 6 * d)
        small_parts[l] = [dn1, dlg, dlb, dws, dbst.T, dgng, dsinks, dang, dcw, dcb, ddtb[:, :SSM_HEADS],
                          dalog[:, :SSM_HEADS], ddsk[:, :SSM_HEADS], dsng, dn2]
        finish(dx)
        pending_rs.append((l, "m", mlp_state))
        if l > 0:
            state, rs_token = rs_begin(mixer_swap, f"{l}a", dx)
            pending_rs.append((l, "a", state))
    grad_x = dx

    big = [(w_in, m_w_in, v_w_in), (w_out, m_w_out, v_w_out), (w_mlp1, m_w_mlp1, v_w_mlp1), (w_mlp2, m_w_mlp2, v_w_mlp2)]
    big_out = [None] * 4
    for l in reversed(range(1, nl)):
        for i, (wt, mt, vt) in enumerate(big):
            if i > 0:
                big_out[i] = adamw_layer(wt, mt, vt, reduced[l][i], l, big_out[i], f"adamw_big_{i}_{l}")

    small_names = [norm1_g, gm_ln_g, gm_ln_b, gm_ws, gm_bs, gm_norm_g, attn_sinks, attn_norm_g, None, conv_b, dt_bias,
                   a_log, d_skip, ssm_norm_g, norm2_g]
    n_small = len(small_names)
    per_param = [jnp.stack([small_parts[l][i].reshape(-1) for l in range(nl)]) for i in range(n_small)]
    small_vec = _pack(per_param + [d_final_g])
    rs_small = small_vec.shape[0]
    dmod_local = jnp.stack(dmods, axis=1)
    g2 = all_gather_small(jnp.concatenate([small_vec, _pack([dmod_local])], axis=0), "ag_small")
    state, rs_token = rs_begin(mixer_swap, "0a", grad_x, start_behind=g2)
    pending_rs.append((0, "a", state))
    g2 = g2 + rs_token[0, 0]
    g_small = sum_devices(g2[:, :rs_small, :], "sum_small")
    dmod_all = g2[:, rs_small:, :].reshape(8, -1)[:, :bl * nl * 6 * d].reshape(nex, nl * 6 * d)
    g_ada_b = sum_devices(dmod_all.reshape(nex, -1, LANES), "sum_ada_b").reshape(nl, 6 * d)
    shapes = [(nl, int(np.prod(small_parts[0][i].shape))) for i in range(n_small)] + [(d,)]
    g_list = _unpack(g_small, shapes)
    g_conv_w = lax.dynamic_slice(g_list[8].reshape(nl, CONV_K, CCH), (0, 0, chip * cw_w), (nl, CONV_K, cw_w))

    dm_cols = lax.dynamic_slice(dmod_all.reshape(nex, nl, 6 * d), (0, 0, chip * mod_w), (nex, nl, mod_w))
    g_ada_w = _mm("tn", c_all, dm_cols.reshape(nex, nl * mod_w), dims=(d, nl * mod_w, nex), tm=512, tn=512, tk=nex,
                  out_dtypes=[F32], name="ada_w_grad", pro_a=c_act, pro_b=to_bf16, out_shapes=[(nl, d, mod_w)],
                  out_specs=[pl.BlockSpec((None, 512, 512), lambda i, j, kk: (j // (mod_w // 512), i, j % (mod_w // 512)))])[0]
    d_ada_w, m_ada_w_n, v_ada_w_n = [a.reshape(ada_w.shape) for a in
                                     adamw(_rows2d(ada_w), _rows2d(m_ada_w), _rows2d(v_ada_w), _rows2d(g_ada_w), "adamw_ada_w")]

    smalls = {
        "ada_b": (ada_b, m_ada_b, v_ada_b, g_ada_b), "norm1_g": (norm1_g, m_norm1_g, v_norm1_g, g_list[0]),
        "gm_ln_g": (gm_ln_g, m_gm_ln_g, v_gm_ln_g, g_list[1]), "gm_ln_b": (gm_ln_b, m_gm_ln_b, v_gm_ln_b, g_list[2]),
        "gm_ws": (gm_ws, m_gm_ws, v_gm_ws, g_list[3]), "gm_bs": (gm_bs, m_gm_bs, v_gm_bs, g_list[4]),
        "gm_norm_g": (gm_norm_g, m_gm_norm_g, v_gm_norm_g, g_list[5]),
        "attn_sinks": (attn_sinks, m_attn_sinks, v_attn_sinks, g_list[6]),
        "attn_norm_g": (attn_norm_g, m_attn_norm_g, v_attn_norm_g, g_list[7]),
        "conv_w": (conv_w, m_conv_w, v_conv_w, g_conv_w), "conv_b": (conv_b, m_conv_b, v_conv_b, g_list[9]),
        "dt_bias": (dt_bias, m_dt_bias, v_dt_bias, g_list[10]), "a_log": (a_log, m_a_log, v_a_log, g_list[11]),
        "d_skip": (d_skip, m_d_skip, v_d_skip, g_list[12]),
        "ssm_norm_g": (ssm_norm_g, m_ssm_norm_g, v_ssm_norm_g, g_list[13]),
        "norm2_g": (norm2_g, m_norm2_g, v_norm2_g, g_list[14]),
        "final_norm_g": (final_norm_g, m_final_norm_g, v_final_norm_g, g_list[15]),
    }
    keys = list(smalls)
    wv, mv, vv_, gv = [_pack([smalls[k][i].reshape(smalls[k][0].shape) for k in keys]) for i in range(4)]
    sd_, sm_, sv_ = adamw(wv, mv, vv_, gv, "adamw_small")
    shp = [smalls[k][0].shape for k in keys]
    small_out = {k: (smalls[k][3].reshape(smalls[k][0].shape), a, b, cc)
                 for k, a, b, cc in zip(keys, _unpack(sd_, shp), _unpack(sm_, shp), _unpack(sv_, shp))}

    late = jnp.zeros((8, LANES), F32) + (sv_[0, 0] + v_ada_w_n[0, 0, 0])
    for bo in big_out:
        if bo is not None:
            late = late + bo[3][nl - 1, 0, 0]
    finish(late)
    for i, (wt, mt, vt) in enumerate(big):
        if i > 0:
            big_out[i] = adamw_layer(wt, mt, vt, reduced[0][i], 0, big_out[i], f"adamw_big_{i}_0")
    minor_first = lambda a: jnp.transpose(a, (2, 0, 1))
    g_in = jnp.stack([reduced[l][0].T for l in range(nl)], axis=1)
    back = lambda a: jnp.transpose(a, (1, 2, 0))
    big_out[0] = [back(a) for a in [g_in, *adamw_minor_rows(minor_first(w_in), minor_first(m_w_in),
                                                            minor_first(v_w_in), g_in, "adamw_w_in")]]

    out = {"ada_w": (g_ada_w, d_ada_w, m_ada_w_n, v_ada_w_n), "w_in": big_out[0], "w_out": big_out[1],
           "w_mlp1": big_out[2], "w_mlp2": big_out[3], **small_out}
    order = ["ada_w", "ada_b", "norm1_g", "w_in", "gm_ln_g", "gm_ln_b", "gm_ws", "gm_bs", "gm_norm_g", "attn_sinks",
             "attn_norm_g", "conv_w", "conv_b", "dt_bias", "a_log", "d_skip", "ssm_norm_g", "w_out", "norm2_g",
             "w_mlp1", "w_mlp2", "final_norm_g"]
    return (loss, grad_x, *[out[k][0] for k in order], *[out[k][1] for k in order],
            *[out[k][2] for k in order], *[out[k][3] for k in order])
```

```python
import functools
import math

import jax
import jax.numpy as jnp
import numpy as np
from jax import lax
from jax.experimental import pallas as pl
from jax.experimental.pallas import tpu as pltpu

F32 = jnp.float32
BF16 = jnp.bfloat16
HI = lax.Precision.HIGHEST
MESH = pl.DeviceIdType.MESH

CHUNK = 128
GM_HEADS, GM_HD = 4, 128
ATT_HEADS, ATT_KV, ATT_HD = 8, 2, 64
WINDOW = 128
SSM_HEADS, SSM_HD, SSM_GROUPS, SSM_STATE, CONV_K = 16, 64, 2, 128, 4
EPS = 1e-6
LN_EPS = 1e-5
NEG = -1e30
LANES = 128

GMW = GM_HEADS * GM_HD
ATW = ATT_HEADS * ATT_HD
KVW = ATT_KV * ATT_HD
SSW = SSM_HEADS * SSM_HD
BCW = SSM_GROUPS * SSM_STATE
CCH = SSW + 2 * BCW
GRW = SSW // SSM_GROUPS
IN_SIZES = (GMW, GMW, ATW, KVW, KVW, SSW, CCH, SSM_HEADS)
IN_W = sum(IN_SIZES)
OFF_U, OFF_V, OFF_Q, OFF_K, OFF_VV, OFF_Z, OFF_XBC, OFF_DT = 0, 512, 1024, 1536, 1664, 1792, 2816, 4352
ZB = 256
PW = 4608

ADAM_LR, ADAM_B1, ADAM_B2, ADAM_EPS, ADAM_WD, ADAM_STEP = 0.001, 0.9, 0.999, 1e-08, 0.01, 10

VMEM_LIMIT = 56 * 1024 * 1024


def _cp(sem=None):
    return pltpu.CompilerParams(dimension_semantics=sem, vmem_limit_bytes=VMEM_LIMIT)


_DN = {"nn": (((1,), (0,)), ((), ())), "nt": (((1,), (1,)), ((), ())), "tn": (((0,), (0,)), ((), ()))}


def _dot(form, a, b):
    return lax.dot_general(a.astype(BF16), b.astype(BF16), _DN[form], preferred_element_type=F32)


@jax.custom_vjp
def _nn(a, b):
    return _dot("nn", a, b)


@jax.custom_vjp
def _nt(a, b):
    return _dot("nt", a, b)


@jax.custom_vjp
def _tn(a, b):
    return _dot("tn", a, b)


_nn.defvjp(lambda a, b: (_dot("nn", a, b), (a, b)), lambda r, g: (_dot("nt", g, r[1]), _dot("tn", r[0], g)))
_nt.defvjp(lambda a, b: (_dot("nt", a, b), (a, b)), lambda r, g: (_dot("nn", g, r[1]), _dot("tn", g, r[0])))
_tn.defvjp(lambda a, b: (_dot("tn", a, b), (a, b)), lambda r, g: (_dot("nt", r[1], g), _dot("nn", r[0], g)))


def _hdot(a, b):
    return jnp.dot(a, b, precision=HI, preferred_element_type=F32)


def _silu(x):
    return x * (1.0 / (1.0 + jnp.exp(-x)))


def _softplus(x):
    return jnp.maximum(x, 0.0) + jnp.log1p(jnp.exp(-jnp.abs(x)))


def _gelu(x):
    return 0.5 * x * (1.0 + jnp.tanh(math.sqrt(2.0 / math.pi) * (x + 0.044715 * (x * x * x))))


def _rms(y, g):
    return y * lax.rsqrt(jnp.mean(y * y, axis=-1, keepdims=True) + EPS) * g


def _mm(form, a, b, *, dims, tm, tn, tk, out_dtypes, name, a_spec=None, b_spec=None, out_specs=None,
        out_shapes=None, extras=(), epi=None, pro_a=None, pro_b=None, behind=None):
    m, n, k = dims
    tm, tn, tk = min(tm, m), min(tn, n), min(tk, k)
    assert m % tm == 0 and n % tn == 0 and k % tk == 0, (name, dims, tm, tn, tk)
    nk = k // tk
    if a_spec is None:
        a_spec = (pl.BlockSpec((tk, tm), lambda i, j, kk: (kk, i)) if form == "tn"
                  else pl.BlockSpec((tm, tk), lambda i, j, kk: (i, kk)))
    if b_spec is None:
        b_spec = (pl.BlockSpec((tn, tk), lambda i, j, kk: (j, kk)) if form == "nt"
                  else pl.BlockSpec((tk, tn), lambda i, j, kk: (kk, j)))
    n_out = len(out_dtypes)
    if out_specs is None:
        out_specs = [pl.BlockSpec((tm, tn), lambda i, j, kk: (i, j))] * n_out
    if out_shapes is None:
        out_shapes = [(m, n)] * n_out
    ne = len(extras)
    n_behind = 0 if behind is None else 1

    def body(*refs):
        a_ref, b_ref = refs[0], refs[1]
        ex = refs[2:2 + ne]
        outs = refs[2 + ne + n_behind:2 + ne + n_behind + n_out]

        def write(val):
            res = epi(val, *[e[...] for e in ex]) if epi is not None else (val,)
            for o, r in zip(outs, res):
                o[...] = r.astype(o.dtype)

        av = a_ref[...]
        if pro_a is not None:
            av = pro_a(av)
        bv = b_ref[...]
        if pro_b is not None:
            bv = pro_b(bv)
        part = lax.dot_general(av, bv, _DN[form], preferred_element_type=F32)
        if nk == 1:
            write(part)
        else:
            acc = refs[-1]
            kk = pl.program_id(2)

            @pl.when(kk == 0)
            def _():
                acc[...] = part

            @pl.when(kk > 0)
            def _():
                acc[...] += part

            @pl.when(kk == nk - 1)
            def _():
                write(acc[...])

    res = pl.pallas_call(
        body, name=name, grid=(m // tm, n // tn, nk),
        in_specs=[a_spec, b_spec] + [s for _, s in extras] + [_ANY] * n_behind,
        out_specs=out_specs,
        out_shape=[jax.ShapeDtypeStruct(s, d) for s, d in zip(out_shapes, out_dtypes)],
        scratch_shapes=[pltpu.VMEM((tm, tn), F32)] if nk > 1 else [],
        compiler_params=_cp(("parallel", "parallel", "arbitrary")),
    )(a, b, *[e for e, _ in extras], *([behind] if n_behind else []))
    return res


def _mm_nt_blocked(a, b, *, tm, tn, name, behind=None):
    m = a.shape[0]
    nparts, n, f = b.shape
    tm, tn = min(tm, m), min(tn, n)
    n_behind = 0 if behind is None else 1

    def body(a_ref, *rest):
        b_refs, o_ref = rest[:nparts], rest[nparts + n_behind]
        acc = None
        for k in range(nparts):
            part = lax.dot_general(a_ref[:, k * f:(k + 1) * f], b_refs[k][...], _DN["nt"], preferred_element_type=F32)
            acc = part if acc is None else acc + part
        o_ref[...] = acc

    return pl.pallas_call(
        body, name=name, grid=(m // tm, n // tn),
        in_specs=[pl.BlockSpec((tm, nparts * f), lambda i, j: (i, 0))]
        + [pl.BlockSpec((None, tn, f), lambda i, j, k=k: (k, j, 0)) for k in range(nparts)] + [_ANY] * n_behind,
        out_specs=pl.BlockSpec((tm, tn), lambda i, j: (i, j)),
        out_shape=jax.ShapeDtypeStruct((m, n), F32),
        compiler_params=_cp(("parallel", "parallel")),
    )(a, *([b] * nparts), *([behind] if n_behind else []))


def _row_tile(s):
    return min(512, s)


def ln_mod_fwd(x, g, sc, sh, name):
    bsz, s, d = x.shape
    ts = _row_tile(s)

    def body(x_ref, g_ref, sc_ref, sh_ref, o_ref):
        xv = x_ref[...]
        r = lax.rsqrt(jnp.mean(xv * xv, axis=-1, keepdims=True) + EPS)
        o_ref[...] = ((xv * r * g_ref[...]) * (1.0 + sc_ref[...]) + sh_ref[...]).astype(o_ref.dtype)

    row = pl.BlockSpec((None, ts, d), lambda b, i: (b, i, 0))
    vec = pl.BlockSpec((None, 1, d), lambda b, i: (b, 0, 0))
    return pl.pallas_call(
        body, name=name, grid=(bsz, s // ts),
        in_specs=[row, pl.BlockSpec((1, d), lambda b, i: (0, 0)), vec, vec],
        out_specs=row, out_shape=jax.ShapeDtypeStruct(x.shape, BF16),
        compiler_params=_cp(("parallel", "parallel")),
    )(x, g, sc, sh)


def ln_mod_bwd(dh, x, dres, g, sc, name):
    bsz, s, d = x.shape
    ts = _row_tile(s)

    def body(dh_ref, x_ref, dres_ref, g_ref, sc_ref, dx_ref, dsc_ref, dsh_ref, dg_ref):
        b, i = pl.program_id(0), pl.program_id(1)
        xv, dhv, gv = x_ref[...], dh_ref[...], g_ref[...]
        r = lax.rsqrt(jnp.mean(xv * xv, axis=-1, keepdims=True) + EPS)
        xn = xv * r
        a = dhv * (1.0 + sc_ref[...])
        dxn = a * gv
        dx_ref[...] = dres_ref[...] + r * (dxn - xn * jnp.mean(dxn * xn, axis=-1, keepdims=True))
        p_sc = jnp.sum(dhv * (xn * gv), axis=0, keepdims=True)
        p_sh = jnp.sum(dhv, axis=0, keepdims=True)
        p_g = jnp.sum(a * xn, axis=0, keepdims=True)

        @pl.when(i == 0)
        def _():
            dsc_ref[...] = p_sc
            dsh_ref[...] = p_sh

        @pl.when(i > 0)
        def _():
            dsc_ref[...] += p_sc
            dsh_ref[...] += p_sh

        @pl.when((i == 0) & (b == 0))
        def _():
            dg_ref[...] = p_g

        @pl.when((i > 0) | (b > 0))
        def _():
            dg_ref[...] += p_g

    row = pl.BlockSpec((None, ts, d), lambda b, i: (b, i, 0))
    vec = pl.BlockSpec((None, 1, d), lambda b, i: (b, 0, 0))
    one = pl.BlockSpec((1, d), lambda b, i: (0, 0))
    return pl.pallas_call(
        body, name=name, grid=(bsz, s // ts),
        in_specs=[row, row, row, one, vec],
        out_specs=[row, vec, vec, one],
        out_shape=[jax.ShapeDtypeStruct(x.shape, F32), jax.ShapeDtypeStruct((bsz, 1, d), F32),
                   jax.ShapeDtypeStruct((bsz, 1, d), F32), jax.ShapeDtypeStruct((1, d), F32)],
        compiler_params=_cp(("arbitrary", "arbitrary")),
    )(dh, x, dres, g, sc)


def gate_bwd(dx, mm, gate, name):
    bsz, s, d = dx.shape
    ts = _row_tile(s)

    def body(dx_ref, m_ref, g_ref, dm_ref, dg_ref):
        i = pl.program_id(1)
        dxv = dx_ref[...]
        dm_ref[...] = (dxv * g_ref[...]).astype(dm_ref.dtype)
        p = jnp.sum(dxv * m_ref[...], axis=0, keepdims=True)

        @pl.when(i == 0)
        def _():
            dg_ref[...] = p

        @pl.when(i > 0)
        def _():
            dg_ref[...] += p

    row = pl.BlockSpec((None, ts, d), lambda b, i: (b, i, 0))
    vec = pl.BlockSpec((None, 1, d), lambda b, i: (b, 0, 0))
    return pl.pallas_call(
        body, name=name, grid=(bsz, s // ts),
        in_specs=[row, row, vec], out_specs=[row, vec],
        out_shape=[jax.ShapeDtypeStruct(dx.shape, BF16), jax.ShapeDtypeStruct((bsz, 1, d), F32)],
        compiler_params=_cp(("parallel", "arbitrary")),
    )(dx, mm, gate)


def loss_head(x, g, tgt, name):
    bsz, s, d = x.shape
    ts = _row_tile(s)

    def body(x_ref, g_ref, t_ref, dx_ref, dg_ref, l_ref):
        b, i = pl.program_id(0), pl.program_id(1)
        xv, gv = x_ref[...], g_ref[...]
        r = lax.rsqrt(jnp.mean(xv * xv, axis=-1, keepdims=True) + EPS)
        xn = xv * r
        e = xn * gv - t_ref[...]
        dy = e * (1.0 / d)
        dxn = dy * gv
        dx_ref[...] = r * (dxn - xn * jnp.mean(dxn * xn, axis=-1, keepdims=True))
        p_g = jnp.sum(dy * xn, axis=0, keepdims=True)
        p_l = jnp.zeros((1, LANES), F32) + jnp.sum(e * e) * (0.5 / d)
        first = (i == 0) & (b == 0)

        @pl.when(first)
        def _():
            dg_ref[...] = p_g
            l_ref[...] = p_l

        @pl.when(jnp.logical_not(first))
        def _():
            dg_ref[...] += p_g
            l_ref[...] += p_l

    row = pl.BlockSpec((None, ts, d), lambda b, i: (b, i, 0))
    one = pl.BlockSpec((1, d), lambda b, i: (0, 0))
    return pl.pallas_call(
        body, name=name, grid=(bsz, s // ts),
        in_specs=[row, one, row],
        out_specs=[row, one, pl.BlockSpec((1, LANES), lambda b, i: (0, 0))],
        out_shape=[jax.ShapeDtypeStruct(x.shape, F32), jax.ShapeDtypeStruct((1, d), F32),
                   jax.ShapeDtypeStruct((1, LANES), F32)],
        compiler_params=_cp(("arbitrary", "arbitrary")),
    )(x, g, tgt)


def _gmlp_chunk(u_raw, v_raw, ln_g, ln_b, w, bs_t, out_g):
    c = u_raw.shape[0]
    u, v = _gelu(u_raw), _gelu(v_raw)
    tril = lax.broadcasted_iota(jnp.int32, (c, c), 0) >= lax.broadcasted_iota(jnp.int32, (c, c), 1)
    ys = []
    for h in range(GM_HEADS):
        sl = slice(h * GM_HD, (h + 1) * GM_HD)
        vh = v[:, sl]
        xc = vh - jnp.mean(vh, axis=-1, keepdims=True)
        vn = xc * lax.rsqrt(jnp.mean(xc * xc, axis=-1, keepdims=True) + LN_EPS) * ln_g[:, sl] + ln_b[:, sl]
        gate = _nn(jnp.where(tril, w[h], 0.0), vn) + bs_t[:, h:h + 1]
        ys.append(u[:, sl] * gate)
    return _rms(jnp.concatenate(ys, axis=1), out_g)


def _gmlp_specs(bsz, nc):
    seg = lambda off: pl.BlockSpec((None, CHUNK, GMW), lambda b, c: (b, c, off // GMW))
    full = lambda shape: pl.BlockSpec(shape, lambda b, c: (0,) * len(shape))
    par = [full((1, GMW)), full((1, GMW)), full((GM_HEADS, CHUNK, CHUNK)), full((CHUNK, GM_HEADS)), full((1, GMW))]
    return seg, full, par


def gmlp_fwd(p, prm, name):
    bsz, s, _ = p.shape
    nc = s // CHUNK
    seg, _, par = _gmlp_specs(bsz, nc)

    def body(u_ref, v_ref, lg, lb, w, bt, og, o_ref):
        o_ref[...] = _gmlp_chunk(u_ref[...], v_ref[...], lg[...], lb[...], w[...], bt[...], og[...]).astype(o_ref.dtype)

    return pl.pallas_call(
        body, name=name, grid=(bsz, nc),
        in_specs=[seg(OFF_U), seg(OFF_V)] + par,
        out_specs=pl.BlockSpec((None, CHUNK, GMW), lambda b, c: (b, c, 0)),
        out_shape=jax.ShapeDtypeStruct((bsz, s, GMW), BF16),
        compiler_params=_cp(("parallel", "parallel")),
    )(p, p, *prm)


def _accumulate(first, refs, vals):
    @pl.when(first)
    def _():
        for r, v in zip(refs, vals):
            r[...] = v

    @pl.when(jnp.logical_not(first))
    def _():
        for r, v in zip(refs, vals):
            r[...] += v


def gmlp_bwd(p, dmix, prm, name):
    bsz, s, _ = p.shape
    nc = s // CHUNK
    seg, full, par = _gmlp_specs(bsz, nc)

    def body(u_ref, v_ref, do_ref, lg, lb, w, bt, og, du_ref, dv_ref, *dpar):
        first = (pl.program_id(0) == 0) & (pl.program_id(1) == 0)
        _, vjp = jax.vjp(_gmlp_chunk, u_ref[...], v_ref[...], lg[...], lb[...], w[...], bt[...], og[...])
        gr = vjp(do_ref[...])
        du_ref[...] = gr[0].astype(du_ref.dtype)
        dv_ref[...] = gr[1].astype(dv_ref.dtype)
        _accumulate(first, dpar, gr[2:])

    out_seg = pl.BlockSpec((None, CHUNK, GMW), lambda b, c: (b, c, 0))
    return pl.pallas_call(
        body, name=name, grid=(bsz, nc),
        in_specs=[seg(OFF_U), seg(OFF_V), out_seg] + par,
        out_specs=[out_seg, out_seg] + par,
        out_shape=[jax.ShapeDtypeStruct((bsz, s, GMW), BF16)] * 2 + [jax.ShapeDtypeStruct(x.shape, F32) for x in prm],
        compiler_params=_cp(("arbitrary", "arbitrary")),
    )(p, p, dmix, *prm)


def _attn_block(q, kp, kc, vp, vc, sinks, out_g, has_prev):
    w = q.shape[0]
    k2 = jnp.concatenate([kp, kc], axis=0)
    v2 = jnp.concatenate([vp, vc], axis=0)
    qi = lax.broadcasted_iota(jnp.int32, (w, 2 * w), 0)
    kj = lax.broadcasted_iota(jnp.int32, (w, 2 * w), 1)
    diff = qi + w - kj
    grp = ATT_HEADS // ATT_KV
    valid = (diff >= 0) & (diff < w) & ((kj >= w) | has_prev)
    valid = jnp.concatenate([valid] * grp, axis=0)
    outs = []
    for kv in range(ATT_KV):
        kh = k2[:, kv * ATT_HD:(kv + 1) * ATT_HD]
        vh = v2[:, kv * ATT_HD:(kv + 1) * ATT_HD]
        heads = range(kv * grp, (kv + 1) * grp)
        qs = jnp.concatenate([q[:, h * ATT_HD:(h + 1) * ATT_HD] for h in heads], axis=0)
        sink = jnp.concatenate([jnp.broadcast_to(sinks[:, h:h + 1], (w, 1)) for h in heads], axis=0)
        sc = jnp.where(valid, _nt(qs, kh) * (ATT_HD ** -0.5), NEG)
        m = jnp.maximum(jnp.max(sc, axis=-1, keepdims=True), sink)
        e = jnp.exp(sc - m)
        pr = e / (jnp.sum(e, axis=-1, keepdims=True) + jnp.exp(sink - m))
        o = _nn(pr, vh)
        outs += [o[gi * w:(gi + 1) * w] for gi in range(grp)]
    return _rms(jnp.concatenate(outs, axis=1), out_g)


ATT_QB_FWD, ATT_QB_BWD = 8, 4


def _attn_tiles(s, windows=ATT_QB_BWD):
    qb = min(windows, s // WINDOW)
    return qb, qb * WINDOW, s // (qb * WINDOW)


def attn_fwd(p, sinks, out_g, name):
    bsz, s, _ = p.shape
    qb, rows, steps = _attn_tiles(s, ATT_QB_FWD)

    def body(q_ref, kp_ref, kc_ref, vp_ref, vc_ref, s_ref, g_ref, o_ref):
        n = pl.program_id(1)
        for w in range(qb):
            sl = pl.ds(w * WINDOW, WINDOW)
            before = pl.ds((w - 1) * WINDOW, WINDOW)
            kp = kp_ref[...] if w == 0 else kc_ref[before, :]
            vp = vp_ref[...] if w == 0 else vc_ref[before, :]
            o_ref[sl, :] = _attn_block(q_ref[sl, :], kp, kc_ref[sl, :], vp, vc_ref[sl, :], s_ref[...], g_ref[...],
                                       (n > 0) if w == 0 else True).astype(o_ref.dtype)

    cur = lambda off: pl.BlockSpec((None, rows, KVW), lambda b, n: (b, n, off // KVW))
    prev = lambda off: pl.BlockSpec((None, WINDOW, KVW), lambda b, n: (b, jnp.maximum(n * qb - 1, 0), off // KVW))
    return pl.pallas_call(
        body, name=name, grid=(bsz, steps),
        in_specs=[pl.BlockSpec((None, rows, ATW), lambda b, n: (b, n, OFF_Q // ATW)),
                  prev(OFF_K), cur(OFF_K), prev(OFF_VV), cur(OFF_VV),
                  pl.BlockSpec((1, ATT_HEADS), lambda b, n: (0, 0)), pl.BlockSpec((1, ATW), lambda b, n: (0, 0))],
        out_specs=pl.BlockSpec((None, rows, ATW), lambda b, n: (b, n, 0)),
        out_shape=jax.ShapeDtypeStruct((bsz, s, ATW), BF16),
        compiler_params=_cp(("parallel", "parallel")),
    )(p, p, p, p, p, sinks, out_g)


def attn_bwd(p, dmix, sinks, out_g, name):
    bsz, s, _ = p.shape
    qb, rows, steps = _attn_tiles(s)
    last = pl.ds(rows - WINDOW, WINDOW)

    def body(q_ref, kp_ref, kc_ref, vp_ref, vc_ref, do_ref, s_ref, g_ref,
             dq_ref, dk_ref, dv_ref, ds_ref, dg_ref, ck, cv):
        b, n = pl.program_id(0), pl.program_id(1)

        @pl.when(n == 0)
        def _():
            ck[...] = jnp.zeros_like(ck)
            cv[...] = jnp.zeros_like(cv)

        @pl.when(n < steps)
        def _():
            grads = []
            for w in range(qb):
                sl = pl.ds(w * WINDOW, WINDOW)
                before = pl.ds((w - 1) * WINDOW, WINDOW)
                kp = kp_ref[...] if w == 0 else kc_ref[before, :]
                vp = vp_ref[...] if w == 0 else vc_ref[before, :]
                fn = functools.partial(_attn_block, has_prev=(n > 0) if w == 0 else True)
                _, vjp = jax.vjp(fn, q_ref[sl, :], kp, kc_ref[sl, :], vp, vc_ref[sl, :], s_ref[...], g_ref[...])
                grads.append(vjp(do_ref[sl, :]))
                dq_ref[sl, :] = grads[-1][0].astype(dq_ref.dtype)
            dk_ref[...] = ck[...].astype(dk_ref.dtype)
            dv_ref[...] = cv[...].astype(dv_ref.dtype)
            dk_ref[last, :] = (ck[last, :] + grads[0][1]).astype(dk_ref.dtype)
            dv_ref[last, :] = (cv[last, :] + grads[0][3]).astype(dv_ref.dtype)
            for w in range(qb):
                sl = pl.ds(w * WINDOW, WINDOW)
                ck[sl, :] = grads[w][2] + (grads[w + 1][1] if w + 1 < qb else 0.0)
                cv[sl, :] = grads[w][4] + (grads[w + 1][3] if w + 1 < qb else 0.0)
            dsk = functools.reduce(lambda u, v: u + v, [g[5] for g in grads])
            dgg = functools.reduce(lambda u, v: u + v, [g[6] for g in grads])
            _accumulate((b == 0) & (n == 0), (ds_ref, dg_ref), (dsk, dgg))

        @pl.when(n == steps)
        def _():
            dk_ref[...] = ck[...].astype(dk_ref.dtype)
            dv_ref[...] = cv[...].astype(dv_ref.dtype)

    at = lambda n: jnp.minimum(n, steps - 1)
    cur = lambda off: pl.BlockSpec((None, rows, KVW), lambda b, n: (b, at(n), off // KVW))
    prev = lambda off: pl.BlockSpec((None, WINDOW, KVW), lambda b, n: (b, jnp.maximum(at(n) * qb - 1, 0), off // KVW))
    kv_out = pl.BlockSpec((None, rows, KVW), lambda b, n: (b, jnp.maximum(n - 1, 0), 0))
    return pl.pallas_call(
        body, name=name, grid=(bsz, steps + 1),
        in_specs=[pl.BlockSpec((None, rows, ATW), lambda b, n: (b, at(n), OFF_Q // ATW)),
                  prev(OFF_K), cur(OFF_K), prev(OFF_VV), cur(OFF_VV),
                  pl.BlockSpec((None, rows, ATW), lambda b, n: (b, at(n), GMW // ATW)),
                  pl.BlockSpec((1, ATT_HEADS), lambda b, n: (0, 0)), pl.BlockSpec((1, ATW), lambda b, n: (0, 0))],
        out_specs=[pl.BlockSpec((None, rows, ATW), lambda b, n: (b, at(n), 0)), kv_out, kv_out,
                   pl.BlockSpec((1, ATT_HEADS), lambda b, n: (0, 0)), pl.BlockSpec((1, ATW), lambda b, n: (0, 0))],
        out_shape=[jax.ShapeDtypeStruct((bsz, s, ATW), BF16), jax.ShapeDtypeStruct((bsz, s, KVW), BF16),
                   jax.ShapeDtypeStruct((bsz, s, KVW), BF16), jax.ShapeDtypeStruct((1, ATT_HEADS), F32),
                   jax.ShapeDtypeStruct((1, ATW), F32)],
        scratch_shapes=[pltpu.VMEM((rows, KVW), F32), pltpu.VMEM((rows, KVW), F32)],
        compiler_params=_cp(("arbitrary", "arbitrary")),
    )(p, p, p, p, p, dmix, sinks, out_g)


CONV_CT = 256


def _shift_down(x, j):
    if j == 0:
        return x
    rows = lax.broadcasted_iota(jnp.int32, x.shape, 0)
    return jnp.where(rows >= j, pltpu.roll(x, j, 0), 0.0)


def _shift_up(x, j):
    if j == 0:
        return x
    s = x.shape[0]
    rows = lax.broadcasted_iota(jnp.int32, x.shape, 0)
    return jnp.where(rows < s - j, pltpu.roll(x, s - j, 0), 0.0)


def conv_fwd(p, w, bias, name):
    bsz, s, _ = p.shape

    def body(x_ref, w_ref, b_ref, o_ref):
        xv, wv = x_ref[...], w_ref[...]
        pre = b_ref[...] + sum(wv[k:k + 1, :] * _shift_down(xv, CONV_K - 1 - k) for k in range(CONV_K))
        o_ref[...] = _silu(pre)

    blk = pl.BlockSpec((None, s, CONV_CT), lambda b, j: (b, 0, j))
    src = pl.BlockSpec((None, s, CONV_CT), lambda b, j: (b, 0, OFF_XBC // CONV_CT + j))
    return pl.pallas_call(
        body, name=name, grid=(bsz, CCH // CONV_CT),
        in_specs=[src, pl.BlockSpec((CONV_K, CONV_CT), lambda b, j: (0, j)), pl.BlockSpec((1, CONV_CT), lambda b, j: (0, j))],
        out_specs=blk, out_shape=jax.ShapeDtypeStruct((bsz, s, CCH), F32),
        compiler_params=_cp(("parallel", "parallel")),
    )(p, w, bias)


def conv_bwd(p, dxc, w, bias, name):
    bsz, s, _ = p.shape

    def body(x_ref, d_ref, w_ref, b_ref, dx_ref, dw_ref, db_ref):
        b = pl.program_id(1)
        xv, wv = x_ref[...], w_ref[...]
        xs = [_shift_down(xv, CONV_K - 1 - k) for k in range(CONV_K)]
        pre = b_ref[...] + sum(wv[k:k + 1, :] * xs[k] for k in range(CONV_K))
        sg = 1.0 / (1.0 + jnp.exp(-pre))
        dpre = d_ref[...] * (sg * (1.0 + pre * (1.0 - sg)))
        dx_ref[...] = sum(wv[k:k + 1, :] * _shift_up(dpre, CONV_K - 1 - k) for k in range(CONV_K)).astype(dx_ref.dtype)
        p_w = jnp.concatenate([jnp.sum(dpre * xs[k], axis=0, keepdims=True) for k in range(CONV_K)], axis=0)
        p_b = jnp.sum(dpre, axis=0, keepdims=True)
        _accumulate(b == 0, (dw_ref, db_ref), (p_w, p_b))

    blk = pl.BlockSpec((None, s, CONV_CT), lambda j, b: (b, 0, j))
    src = pl.BlockSpec((None, s, CONV_CT), lambda j, b: (b, 0, OFF_XBC // CONV_CT + j))
    wsp = pl.BlockSpec((CONV_K, CONV_CT), lambda j, b: (0, j))
    bsp = pl.BlockSpec((1, CONV_CT), lambda j, b: (0, j))
    return pl.pallas_call(
        body, name=name, grid=(CCH // CONV_CT, bsz),
        in_specs=[src, blk, wsp, bsp], out_specs=[blk, wsp, bsp],
        out_shape=[jax.ShapeDtypeStruct((bsz, s, CCH), BF16), jax.ShapeDtypeStruct((CONV_K, CCH), F32),
                   jax.ShapeDtypeStruct((1, CCH), F32)],
        compiler_params=_cp(("parallel", "arbitrary")),
    )(p, dxc, w, bias)


def _ssd_consts():
    c = CHUNK
    r = lax.broadcasted_iota(jnp.int32, (c, c), 0)
    q = lax.broadcasted_iota(jnp.int32, (c, c), 1)
    hrow = lax.broadcasted_iota(jnp.int32, (LANES, SSW), 0)
    hcol = lax.broadcasted_iota(jnp.int32, (LANES, SSW), 1) // SSM_HD
    expand = (hrow == hcol).astype(F32)
    return expand, (r >= q).astype(F32), (r <= q).astype(F32), r >= q


def _ssd_chunk(xc, dtr, z, prev_t, dt_bias, a_log, d_skip, norm_g):
    c = xc.shape[0]
    expand, tril1, triu1, causal = _ssd_consts()
    xs, bm, cm = xc[:, :SSW], xc[:, SSW:SSW + BCW], xc[:, SSW + BCW:]
    dt = _softplus(dtr + dt_bias)
    da = dt * (-jnp.exp(a_log))
    a_cs = _hdot(tril1, da)
    a_cs_t = _hdot(da.T, triu1)
    dt_e = _hdot(dt, expand)
    acs_e = _hdot(a_cs, expand)
    alast_e = acs_e[c - 1:c, :]
    dsk_e = _hdot(jnp.broadcast_to(d_skip, (8, LANES)), expand)[0:1, :]
    xdt = xs * dt_e
    hg = SSM_HEADS // SSM_GROUPS
    ys, new_t = [], []
    for g in range(SSM_GROUPS):
        bg = bm[:, g * SSM_STATE:(g + 1) * SSM_STATE]
        cg = cm[:, g * SSM_STATE:(g + 1) * SSM_STATE]
        sl = slice(g * GRW, (g + 1) * GRW)
        cb = _nt(cg, bg)
        xdt_g = xdt[:, sl]
        st = _tn(bg, xdt_g * jnp.exp(alast_e[:, sl] - acs_e[:, sl]))
        new_t.append(prev_t[:, sl] * jnp.exp(alast_e[:, sl]) + st)
        y_off = _nn(cg, prev_t[:, sl]) * jnp.exp(acs_e[:, sl])
        yd = []
        low = lax.broadcasted_iota(jnp.int32, (c, LANES), 1) < SSM_HD
        for pair in range(hg // 2):
            xp = xdt_g[:, pair * LANES:(pair + 1) * LANES]
            acc = None
            for side, xh in enumerate((jnp.where(low, xp, 0.0), jnp.where(low, 0.0, xp))):
                h = g * hg + 2 * pair + side
                decay = jnp.exp(jnp.where(causal, a_cs[:, h:h + 1] - a_cs_t[h:h + 1, :], NEG))
                part = _nn(cb * decay, xh)
                acc = part if acc is None else acc + part
            yd.append(acc)
        ys.append(jnp.concatenate(yd, axis=1) + y_off)
    y = (jnp.concatenate(ys, axis=1) + xs * dsk_e) * _silu(z)
    yn = [y[:, g * GRW:(g + 1) * GRW] * lax.rsqrt(jnp.mean(jnp.square(y[:, g * GRW:(g + 1) * GRW]), axis=-1, keepdims=True) + EPS)
          for g in range(SSM_GROUPS)]
    return jnp.concatenate(yn, axis=1) * norm_g, jnp.concatenate(new_t, axis=1)


def ssd_fwd(xc, p, prm, name):
    bsz, s, _ = p.shape
    nc = s // CHUNK

    def body(xc_ref, dt_ref, *rest):
        z_refs, (db, al, dk, ng, o_ref, st_ref, state) = rest[:SSW // ZB], rest[SSW // ZB:]
        @pl.when(pl.program_id(0) == 0)
        def _():
            state[...] = jnp.zeros_like(state)

        for b in range(bsz):
            prev = state[b]
            st_ref[b, 0] = prev
            zb = jnp.concatenate([r[b] for r in z_refs], axis=1)
            out, new = _ssd_chunk(xc_ref[b], dt_ref[b], zb, prev, db[...], al[...], dk[...], ng[...])
            o_ref[b] = out.astype(o_ref.dtype)
            state[b] = new

    vec = pl.BlockSpec((1, LANES), lambda c: (0, 0))
    return pl.pallas_call(
        body, name=name, grid=(nc,),
        in_specs=[pl.BlockSpec((bsz, CHUNK, CCH), lambda c: (0, c, 0)),
                  pl.BlockSpec((bsz, CHUNK, LANES), lambda c: (0, c, OFF_DT // LANES)),
                  *[pl.BlockSpec((bsz, CHUNK, ZB), lambda c, i=i: (0, c, OFF_Z // ZB + i)) for i in range(SSW // ZB)],
                  vec, vec, vec, pl.BlockSpec((1, SSW), lambda c: (0, 0))],
        out_specs=[pl.BlockSpec((bsz, CHUNK, SSW), lambda c: (0, c, 0)),
                   pl.BlockSpec((bsz, 1, SSM_STATE, SSW), lambda c: (0, c, 0, 0))],
        out_shape=[jax.ShapeDtypeStruct((bsz, s, SSW), BF16), jax.ShapeDtypeStruct((bsz, nc, SSM_STATE, SSW), F32)],
        scratch_shapes=[pltpu.VMEM((bsz, SSM_STATE, SSW), F32)],
        compiler_params=_cp(("arbitrary",)),
    )(xc, p, *([p] * (SSW // ZB)), *prm)


def ssd_bwd(xc, p, states, dmix, prm, name):
    bsz, s, _ = p.shape
    nc = s // CHUNK

    def body(xc_ref, dt_ref, *rest):
        z_refs, (st_ref, do_ref, db, al, dk, ng, dxc_ref, ddt_ref, dz_ref) = rest[:SSW // ZB], rest[SSW // ZB:SSW // ZB + 9]
        rest = rest[SSW // ZB + 9:]
        dpar, dstate = rest[:4], rest[4]
        c = pl.program_id(0)

        @pl.when(c == 0)
        def _():
            dstate[...] = jnp.zeros_like(dstate)

        dpars = None
        for b in range(bsz):
            zb = jnp.concatenate([r[b] for r in z_refs], axis=1)
            _, vjp = jax.vjp(_ssd_chunk, xc_ref[b], dt_ref[b], zb, st_ref[b, 0], db[...], al[...], dk[...], ng[...])
            gr = vjp((do_ref[b], dstate[b]))
            dxc_ref[b] = gr[0]
            ddt_ref[b] = gr[1].astype(ddt_ref.dtype)
            dz_ref[b] = gr[2].astype(dz_ref.dtype)
            dstate[b] = gr[3]
            dpars = gr[4:] if dpars is None else [u + v for u, v in zip(dpars, gr[4:])]
        _accumulate(c == 0, dpar, dpars)

    rv = lambda c: nc - 1 - c
    vec = pl.BlockSpec((1, LANES), lambda c: (0, 0))
    ngs = pl.BlockSpec((1, SSW), lambda c: (0, 0))
    return pl.pallas_call(
        body, name=name, grid=(nc,),
        in_specs=[pl.BlockSpec((bsz, CHUNK, CCH), lambda c: (0, rv(c), 0)),
                  pl.BlockSpec((bsz, CHUNK, LANES), lambda c: (0, rv(c), OFF_DT // LANES)),
                  *[pl.BlockSpec((bsz, CHUNK, ZB), lambda c, i=i: (0, rv(c), OFF_Z // ZB + i)) for i in range(SSW // ZB)],
                  pl.BlockSpec((bsz, 1, SSM_STATE, SSW), lambda c: (0, rv(c), 0, 0)),
                  pl.BlockSpec((bsz, CHUNK, SSW), lambda c: (0, rv(c), (GMW + ATW) // SSW)),
                  vec, vec, vec, ngs],
        out_specs=[pl.BlockSpec((bsz, CHUNK, CCH), lambda c: (0, rv(c), 0)),
                   pl.BlockSpec((bsz, CHUNK, LANES), lambda c: (0, rv(c), 0)),
                   pl.BlockSpec((bsz, CHUNK, SSW), lambda c: (0, rv(c), 0)),
                   vec, vec, vec, ngs],
        out_shape=[jax.ShapeDtypeStruct((bsz, s, CCH), F32), jax.ShapeDtypeStruct((bsz, s, LANES), BF16),
                   jax.ShapeDtypeStruct((bsz, s, SSW), BF16)] + [jax.ShapeDtypeStruct((1, LANES), F32)] * 3
                  + [jax.ShapeDtypeStruct((1, SSW), F32)],
        scratch_shapes=[pltpu.VMEM((bsz, SSM_STATE, SSW), F32)],
        compiler_params=_cp(("arbitrary",)),
    )(xc, p, *([p] * (SSW // ZB)), states, dmix, *prm)


def _rows2d(a):
    return a.reshape(-1, a.shape[-1])


def _ew_tile(r, c):
    t = r
    while t * c > (1 << 20) and t % 16 == 0:
        t //= 2
    return t


def add_pair(g, theirs, chip, core, name):
    k, r, c = g.shape
    h = r // 2
    tr = _ew_tile(h, c)
    nb = h // tr

    def body(s_ref, a_ref, b_ref, own_ref, ob_ref):
        s = a_ref[...] + b_ref[...]
        ob_ref[...] = s.astype(ob_ref.dtype)

        @pl.when(pl.program_id(1) == s_ref[1])
        def _():
            own_ref[...] = s

    blk = pl.BlockSpec((None, tr, c), lambda i, kk, sr: (kk, i, 0))
    return pl.pallas_call(
        body, name=name,
        grid_spec=pltpu.PrefetchScalarGridSpec(
            num_scalar_prefetch=1, grid=(nb, k),
            in_specs=[pl.BlockSpec((None, tr, c), lambda i, kk, sr: (kk, sr[0] * nb + i, 0)), blk],
            out_specs=[pl.BlockSpec((tr, c), lambda i, kk, sr: (i, 0)), blk]),
        out_shape=[jax.ShapeDtypeStruct((h, c), F32), jax.ShapeDtypeStruct(theirs.shape, BF16)],
        compiler_params=_cp(("parallel", "arbitrary")),
    )(jnp.stack([core, chip]).astype(jnp.int32), g, theirs)


def sum_own_recv(own, recv, core, name):
    h, c = own.shape
    tr = _ew_tile(h, c)
    nb = h // tr

    def body(k_ref, o_ref, r_ref, out_ref):
        s = o_ref[...]
        for j in range(3):
            s = s + r_ref[j].astype(F32)
        out_ref[...] = s

    return pl.pallas_call(
        body, name=name,
        grid_spec=pltpu.PrefetchScalarGridSpec(
            num_scalar_prefetch=1, grid=(nb,),
            in_specs=[pl.BlockSpec((tr, c), lambda i, kr: (i, 0)),
                      pl.BlockSpec((3, tr, c), lambda i, kr: (0, i, 0))],
            out_specs=pl.BlockSpec((tr, c), lambda i, kr: (kr[0] * nb + i, 0))),
        out_shape=jax.ShapeDtypeStruct((2 * h, c), F32),
        compiler_params=_cp(("parallel",)),
    )(core.reshape(1).astype(jnp.int32), own, recv)


def _adam_math(w, m, v, g):
    mn = ADAM_B1 * m + (1.0 - ADAM_B1) * g
    vn = ADAM_B2 * v + (1.0 - ADAM_B2) * (g * g)
    mh = mn / (1.0 - ADAM_B1 ** ADAM_STEP)
    vh = vn / (1.0 - ADAM_B2 ** ADAM_STEP)
    return -ADAM_LR * (mh / (jnp.sqrt(vh) + ADAM_EPS) + ADAM_WD * w), mn, vn


def adamw_minor_rows(w, m, v, g, name):
    r, nl, c = w.shape
    tr = max(t for t in range(1, r + 1) if r % t == 0 and t * nl * c <= (1 << 19))

    def body(w_ref, m_ref, v_ref, g_ref, d_ref, mo_ref, vo_ref):
        d_ref[...], mo_ref[...], vo_ref[...] = _adam_math(w_ref[...], m_ref[...], v_ref[...], g_ref[...])

    blk = pl.BlockSpec((tr, nl, c), lambda i: (i, 0, 0))
    return pl.pallas_call(
        body, name=name, grid=(r // tr,), in_specs=[blk] * 4, out_specs=[blk] * 3,
        out_shape=[jax.ShapeDtypeStruct(w.shape, F32)] * 3, compiler_params=_cp(("parallel",)),
    )(w, m, v, g)


def adamw_layer(w, m, v, g, layer, prev, name):
    nl, r, c = w.shape
    tr = _ew_tile(r, c * 2)

    def body(w_ref, m_ref, v_ref, g_ref, *rest):
        go_ref, d_ref, mo_ref, vo_ref = rest[-4:]
        gv = g_ref[...]
        dl, mn, vn = _adam_math(w_ref[...], m_ref[...], v_ref[...], gv)
        go_ref[...] = gv
        d_ref[...] = dl
        mo_ref[...] = mn
        vo_ref[...] = vn

    lay = pl.BlockSpec((None, tr, c), lambda i: (layer, i, 0))
    n_prev = 0 if prev is None else 4
    return pl.pallas_call(
        body, name=name, grid=(r // tr,),
        in_specs=[lay, lay, lay, pl.BlockSpec((tr, c), lambda i: (i, 0))] + [_ANY] * n_prev,
        out_specs=[lay] * 4, out_shape=[jax.ShapeDtypeStruct(w.shape, F32)] * 4,
        input_output_aliases={4 + i: i for i in range(n_prev)},
        compiler_params=_cp(("parallel",)),
    )(w, m, v, g, *(prev or ()))


def sum_devices(parts, name):
    n, r, c = parts.shape
    tr = _ew_tile(r, c * n)

    def body(p_ref, o_ref):
        s = p_ref[0]
        for j in range(1, n):
            s = s + p_ref[j]
        o_ref[...] = s

    return pl.pallas_call(
        body, name=name, grid=(r // tr,),
        in_specs=[pl.BlockSpec((n, tr, c), lambda i: (0, i, 0))],
        out_specs=pl.BlockSpec((tr, c), lambda i: (i, 0)),
        out_shape=jax.ShapeDtypeStruct((r, c), F32),
        compiler_params=_cp(("parallel",)),
    )(parts)


def adamw(w, m, v, g, name):
    r, c = w.shape
    tr = _ew_tile(r, c * 2)

    def body(w_ref, m_ref, v_ref, g_ref, d_ref, mo_ref, vo_ref):
        gv = g_ref[...]
        mn = ADAM_B1 * m_ref[...] + (1.0 - ADAM_B1) * gv
        vn = ADAM_B2 * v_ref[...] + (1.0 - ADAM_B2) * (gv * gv)
        mh = mn / (1.0 - ADAM_B1 ** ADAM_STEP)
        vh = vn / (1.0 - ADAM_B2 ** ADAM_STEP)
        d_ref[...] = -ADAM_LR * (mh / (jnp.sqrt(vh) + ADAM_EPS) + ADAM_WD * w_ref[...])
        mo_ref[...] = mn
        vo_ref[...] = vn

    blk = pl.BlockSpec((tr, c), lambda i: (i, 0))
    return pl.pallas_call(
        body, name=name, grid=(r // tr,), in_specs=[blk] * 4, out_specs=[blk] * 3,
        out_shape=[jax.ShapeDtypeStruct((r, c), F32)] * 3,
        compiler_params=_cp(("parallel",)),
    )(w, m, v, g)


def _place():
    x, y, c = lax.axis_index("x"), lax.axis_index("y"), lax.axis_index("c")
    chips = [(1 - x, y), (x, 1 - y), (1 - x, 1 - y)]
    return x, y, c, chips


def all_gather_small(v, name):
    r, w = v.shape

    def body(x_ref, out_ref, send_sems, recv_sems, local_sem):
        x, y, c, chips = _place()
        me, sibling = (x, y, c), (x, y, 1 - c)

        def rows(px, py, pc):
            return out_ref.at[pl.ds((4 * px + 2 * py + pc) * r, r), :]

        def copy(k, block, to, src=None):
            return pltpu.make_async_remote_copy(
                src_ref=rows(*block) if src is None else src, dst_ref=rows(*block),
                send_sem=send_sems.at[k], recv_sem=recv_sems.at[k], device_id=to, device_id_type=MESH)

        mine = pltpu.make_async_copy(x_ref, rows(*me), local_sem)
        mine.start()
        first = [copy(0, me, sibling, src=x_ref)]
        first += [copy(1 + j, me, (*chip, c), src=x_ref) for j, chip in enumerate(chips)]
        for cp in first:
            cp.start()
        passed = [copy(4 + j, (*chip, c), sibling) for j, chip in enumerate(chips)]
        for j, chip in enumerate(chips):
            copy(1 + j, (*chip, c), me).wait_recv()
            passed[j].start()
        copy(0, sibling, me).wait_recv()
        for j, chip in enumerate(chips):
            copy(4 + j, (*chip, 1 - c), me).wait_recv()
        for cp in first + passed:
            cp.wait_send()
        mine.wait()

    out = pl.pallas_call(
        body, name=name, out_shape=jax.ShapeDtypeStruct((8 * r, w), v.dtype),
        in_specs=[pl.BlockSpec(memory_space=pltpu.VMEM)], out_specs=pl.BlockSpec(memory_space=pltpu.VMEM),
        scratch_shapes=[pltpu.SemaphoreType.DMA((7,)), pltpu.SemaphoreType.DMA((7,)), pltpu.SemaphoreType.DMA],
        compiler_params=pltpu.CompilerParams(vmem_limit_bytes=VMEM_LIMIT),
    )(v)
    return out.reshape(8, r, w)


_HBM = pl.BlockSpec(memory_space=pltpu.HBM)


_SEM = pl.BlockSpec(memory_space=pltpu.SEMAPHORE)
_ANY = pl.BlockSpec(memory_space=pl.ANY)
_EFFECT = pltpu.SideEffectType.DATAFLOW_SIDE_EFFECTING


def _hbm(a):
    return pltpu.with_memory_space_constraint(a, pltpu.HBM)


def split_copy_start(srcs, land_shapes, copies, after, name):
    n, nl = len(srcs), len(land_shapes)
    n_after = 0 if after is None else 1
    ncopy = [0]

    def body(*refs):
        ins, lands = refs[:n], refs[n:n + nl]
        send_sems, recv_sems = refs[n + nl + n_after], refs[n + nl + n_after + 1]
        token = refs[-1]
        x, y, c, chips = _place()
        for k, (src, dst, to) in enumerate(copies(x, y, c, chips, ins, lands)):
            pltpu.make_async_remote_copy(src_ref=src, dst_ref=dst, send_sem=send_sems.at[k], recv_sem=recv_sems.at[k],
                                         device_id=to, device_id_type=MESH).start()
        token[...] = jnp.zeros_like(token)

    ncopy[0] = len(copies(0, 0, 0, [(1, 0), (0, 1), (1, 1)], [None] * n, [None] * nl, count_only=True))
    k = ncopy[0]
    lands = [_hbm(lax.empty(s.shape, s.dtype)) for s in land_shapes]
    res = pl.pallas_call(
        body, name=name,
        out_shape=(pltpu.SemaphoreType.DMA((k,)), pltpu.SemaphoreType.DMA((k,)))
        + tuple(pltpu.HBM(s.shape, s.dtype) for s in srcs) + tuple(pltpu.HBM(s.shape, s.dtype) for s in land_shapes)
        + (jax.ShapeDtypeStruct((8, LANES), F32),),
        in_specs=[_HBM] * (n + nl) + [_ANY] * n_after,
        out_specs=(_SEM, _SEM) + (_HBM,) * (n + nl) + (pl.BlockSpec(memory_space=pltpu.VMEM),),
        input_output_aliases={i: 2 + i for i in range(n + nl)},
        compiler_params=pltpu.CompilerParams(has_side_effects=_EFFECT),
    )(*[_hbm(s) for s in srcs], *lands, *([after] if n_after else []))
    return res[0], res[1], list(res[2:2 + n]), list(res[2 + n:2 + n + nl]), res[-1]


def split_copy_wait(send_sems, recv_sems, srcs, lands, copies, after, name):
    n, nl = len(srcs), len(lands)

    def body(*refs):
        ins, lnd = refs[:n], refs[n:n + nl]
        ss, rs = refs[n + nl], refs[n + nl + 1]
        x, y, c, chips = _place()
        for k, (src, dst, to) in enumerate(copies(x, y, c, chips, ins, lnd, receive=True)):
            cp = pltpu.make_async_remote_copy(src_ref=src, dst_ref=dst, send_sem=ss.at[k], recv_sem=rs.at[k],
                                              device_id=to, device_id_type=MESH)
            cp.wait_send()
            cp.wait_recv()

    res = pl.pallas_call(
        body, name=name,
        out_shape=tuple(pltpu.HBM(s.shape, s.dtype) for s in srcs) + tuple(pltpu.HBM(s.shape, s.dtype) for s in lands),
        in_specs=[_HBM] * (n + nl) + [_SEM, _SEM, _ANY], out_specs=(_HBM,) * (n + nl),
        input_output_aliases={i: i for i in range(n + nl)},
        compiler_params=pltpu.CompilerParams(has_side_effects=_EFFECT),
    )(*srcs, *lands, send_sems, recv_sems, after)
    return list(res[:n]), list(res[n:])


def _gather_copies(x, y, c, chips, ins, lands, receive=False, count_only=False):
    out = []
    for i in range(len(ins)):
        for cx, cy in chips:
            if count_only:
                out.append(None)
                continue
            h = ins[i].shape[0] // 2
            rows = pl.ds(c * h, h)
            k_dst = (2 * cx + cy) if receive else (2 * x + y)
            out.append((ins[i].at[rows, :], lands[i].at[k_dst, rows, :], (cx, cy, c)))
    for i in range(len(ins)):
        out.append(None if count_only else (ins[i], lands[i].at[2 * x + y], (x, y, 1 - c)))
    return out


def _swap_copies(x, y, c, chips, ins, lands, receive=False, count_only=False):
    out = []
    for i in range(len(ins)):
        if count_only:
            out.append(None)
            continue
        h = ins[i].shape[1] // 2
        out.append((ins[i].at[:, pl.ds((1 - c) * h, h), :], lands[i], (x, y, 1 - c)))
    return out


def _scatter_copies(x, y, c, chips, ins, lands, receive=False, count_only=False):
    out = []
    for i in range(len(ins)):
        for j, (cx, cy) in enumerate(chips):
            if count_only:
                out.append(None)
                continue
            out.append((ins[i].at[2 * cx + cy], lands[i].at[j], (cx, cy, c)))
    return out


def forward_halves(lands, name):
    n = len(lands)

    def body(*refs):
        ins, outs = refs[:n], refs[n:2 * n]
        send_sems, recv_sems = refs[2 * n:]
        x, y, c, chips = _place()
        sibling = (x, y, 1 - c)
        sent = []
        for i in range(n):
            h = ins[i].shape[1] // 2
            for j, (cx, cy) in enumerate(chips):
                blk = ins[i].at[2 * cx + cy, pl.ds(c * h, h), :]
                sent.append(pltpu.make_async_remote_copy(
                    src_ref=blk, dst_ref=outs[i].at[2 * cx + cy, pl.ds(c * h, h), :], send_sem=send_sems.at[3 * i + j],
                    recv_sem=recv_sems.at[3 * i + j], device_id=sibling, device_id_type=MESH))
                sent[-1].start()
        for i in range(n):
            h = ins[i].shape[1] // 2
            for j, (cx, cy) in enumerate(chips):
                theirs = outs[i].at[2 * cx + cy, pl.ds((1 - c) * h, h), :]
                pltpu.make_async_remote_copy(
                    src_ref=theirs, dst_ref=theirs, send_sem=send_sems.at[3 * i + j], recv_sem=recv_sems.at[3 * i + j],
                    device_id=sibling, device_id_type=MESH).wait_recv()
        for cp in sent:
            cp.wait_send()

    return pl.pallas_call(
        body, name=name, out_shape=[jax.ShapeDtypeStruct(s.shape, s.dtype) for s in lands],
        in_specs=[_HBM] * n, out_specs=[_HBM] * n, input_output_aliases={i: i for i in range(n)},
        scratch_shapes=[pltpu.SemaphoreType.DMA((3 * n,)), pltpu.SemaphoreType.DMA((3 * n,))],
    )(*lands)


def join_halves(halves, name):
    n = len(halves)

    def body(*refs):
        ins, outs = refs[:n], refs[n:2 * n]
        send_sems, recv_sems = refs[2 * n:]
        x, y, c, _ = _place()
        sibling = (x, y, 1 - c)
        sent = []
        for i in range(n):
            h = ins[i].shape[0] // 2
            sent.append(pltpu.make_async_remote_copy(
                src_ref=ins[i].at[pl.ds(c * h, h), :], dst_ref=outs[i].at[pl.ds(c * h, h), :], send_sem=send_sems.at[i],
                recv_sem=recv_sems.at[i], device_id=sibling, device_id_type=MESH))
            sent[-1].start()
        for i in range(n):
            h = ins[i].shape[0] // 2
            theirs = outs[i].at[pl.ds((1 - c) * h, h), :]
            pltpu.make_async_remote_copy(
                src_ref=theirs, dst_ref=theirs, send_sem=send_sems.at[i],
                recv_sem=recv_sems.at[i], device_id=sibling, device_id_type=MESH).wait_recv()
        for cp in sent:
            cp.wait_send()

    return pl.pallas_call(
        body, name=name, out_shape=[jax.ShapeDtypeStruct(s.shape, F32) for s in halves],
        in_specs=[_HBM] * n, out_specs=[_HBM] * n, input_output_aliases={i: i for i in range(n)},
        scratch_shapes=[pltpu.SemaphoreType.DMA((n,)), pltpu.SemaphoreType.DMA((n,))],
    )(*halves)


_PACK_ROWS = 8 * LANES


def _pack(arrs):
    flat = jnp.concatenate([a.reshape(-1).astype(F32) for a in arrs])
    pad = (-flat.shape[0]) % _PACK_ROWS
    return jnp.pad(flat, (0, pad)).reshape(-1, LANES)


def _unpack(flat, shapes):
    flat = flat.reshape(-1)
    out, off = [], 0
    for s in shapes:
        n = int(np.prod(s))
        out.append(flat[off:off + n].reshape(s))
        off += n
    return out


def _win_from_blocks(g):
    d = g.shape[1]
    return jnp.pad(g.transpose(1, 0, 2).reshape(d, IN_W), ((0, 0), (0, PW - IN_W)))


def _win_to_blocks(w):
    return w[:, :IN_W].reshape(w.shape[0], 4, IN_W // 4).transpose(1, 0, 2)


def _relu2(a):
    r = jnp.maximum(a, 0)
    return r * r


def kernel(x, c, ada_w, ada_b, norm1_g, w_in, gm_ln_g, gm_ln_b, gm_ws, gm_bs, gm_norm_g, attn_sinks, attn_norm_g, conv_w, conv_b, dt_bias, a_log, d_skip, ssm_norm_g, w_out, norm2_g, w_mlp1, w_mlp2, final_norm_g, loss_target, m_ada_w, m_ada_b, m_norm1_g, m_w_in, m_gm_ln_g, m_gm_ln_b, m_gm_ws, m_gm_bs, m_gm_norm_g, m_attn_sinks, m_attn_norm_g, m_conv_w, m_conv_b, m_dt_bias, m_a_log, m_d_skip, m_ssm_norm_g, m_w_out, m_norm2_g, m_w_mlp1, m_w_mlp2, m_final_norm_g, v_ada_w, v_ada_b, v_norm1_g, v_w_in, v_gm_ln_g, v_gm_ln_b, v_gm_ws, v_gm_bs, v_gm_norm_g, v_attn_sinks, v_attn_norm_g, v_conv_w, v_conv_b, v_dt_bias, v_a_log, v_d_skip, v_ssm_norm_g, v_w_out, v_norm2_g, v_w_mlp1, v_w_mlp2, v_final_norm_g):
    nl = ada_w.shape[0]
    bl, s, d = x.shape
    t = bl * s
    dff4 = w_mlp1.shape[2]
    dff = 4 * dff4
    mod_w = ada_w.shape[2]
    cw_w = conv_w.shape[2]
    xi, yi, ci = lax.axis_index("x"), lax.axis_index("y"), lax.axis_index("c")
    chip = 2 * xi + yi
    dev = 2 * chip + ci
    nex = 8 * bl

    shards = [[w_in[l].astype(BF16), w_out[l].astype(BF16), w_mlp1[l].astype(BF16), w_mlp2[l].astype(BF16)]
              for l in range(nl)]
    groups = [[shards[0][i]] for i in range(4)] + [shards[l] for l in range(1, nl)]

    def start_gather(gi, behind):
        ss, rs, srcs, lands, token = split_copy_start(
            groups[gi], [jax.ShapeDtypeStruct((4,) + a.shape, a.dtype) for a in groups[gi]], _gather_copies, behind,
            f"gather_start_{gi}")
        return (ss, rs, srcs, lands), token

    first_gather, first_token = start_gather(0, None)
    g0 = all_gather_small(_pack([c, conv_w]) + first_token[0, 0], "ag_c")
    g0 = g0.reshape(8, -1)
    c_all = g0[:, :bl * d].reshape(nex, d)
    cw_parts = g0[0::2, bl * d:bl * d + conv_w.size].reshape(4, nl, CONV_K, cw_w)
    conv_w_full = cw_parts.transpose(1, 2, 0, 3).reshape(nl, CONV_K, CCH)

    def c_act(a):
        return _silu(a).astype(BF16)

    def to_bf16(a):
        return a.astype(BF16)

    mod_parts = []
    for l in range(nl):
        bias = lax.dynamic_slice(ada_b[l].reshape(1, -1), (0, chip * mod_w), (1, mod_w))
        mod_parts.append(_mm("nn", c_all, ada_w, dims=(nex, mod_w, d), tm=nex, tn=512, tk=d, out_dtypes=[F32],
                             name=f"mod_{l}", pro_a=c_act, pro_b=to_bf16,
                             b_spec=pl.BlockSpec((None, d, 512), lambda i, j, kk, l=l: (l, kk, j)),
                             extras=[(bias, pl.BlockSpec((1, 512), lambda i, j, kk: (0, j)))],
                             epi=lambda acc, bv: (acc + bv,))[0])
    g1 = all_gather_small(_pack(mod_parts), "ag_mod").reshape(8, -1)
    mod_all = g1[0::2, :nl * nex * mod_w].reshape(4, nl, nex, mod_w).transpose(1, 2, 0, 3).reshape(nl, nex, 4 * mod_w)
    mod = lax.dynamic_slice(mod_all, (0, dev * bl, 0), (nl, bl, 4 * mod_w))
    mods = [[mod[l, :, i * d:(i + 1) * d].reshape(bl, 1, d) for i in range(6)] for l in range(nl)]

    pending, after = [first_gather], g1
    for gi in range(1, len(groups)):
        state, after = start_gather(gi, after)
        pending.append(state)
    mods[0][0] = mods[0][0] + after[0, 0]

    def fetch(gi, behind):
        ss, rs, srcs, lands = pending[gi]
        srcs, lands = split_copy_wait(ss, rs, srcs, lands, _gather_copies, behind, f"gather_wait_{gi}")
        return forward_halves(lands, f"gather_pass_{gi}")

    as_win = _win_from_blocks

    wfull = [None] * nl
    row = lambda a: a.reshape(1, -1)
    pad16 = lambda a: jnp.pad(a.reshape(1, -1), ((0, 0), (0, LANES - SSM_HEADS)))
    tm_res = min(1024, s)

    def residual(acc, xt, gt):
        return acc, xt + gt * acc

    def res_extras(xin, gate, tm=tm_res):
        return [(xin.reshape(t, d), pl.BlockSpec((tm, 512), lambda i, j, kk: (i, j))),
                (gate, pl.BlockSpec((None, 1, 512), lambda i, j, kk: (i * tm // s, 0, j)))]

    w1_blk = lambda tk, tn: pl.BlockSpec((None, tk, tn), lambda i, j, kk: (j // (dff4 // tn), kk, j % (dff4 // tn)))

    saved = []
    xcur = x
    for l in range(nl):
        sh1, sc1, gt1, sh2, sc2, gt2 = mods[l]
        if l == 0:
            win = as_win(fetch(0, mod)[0])
        else:
            g_in, g_out, w1, g_2 = fetch(3 + l, xcur)
            win, wout, w2 = as_win(g_in), g_out.reshape(-1, d), g_2.reshape(dff, d)
        prm_a = (row(gm_ln_g[l]), row(gm_ln_b[l]), gm_ws[l], gm_bs[l].T, row(gm_norm_g[l]))
        prm_b = (row(attn_sinks[l]), row(attn_norm_g[l]))
        prm_c = (pad16(dt_bias[l]), pad16(a_log[l]), pad16(d_skip[l]), row(ssm_norm_g[l]))
        h1 = ln_mod_fwd(xcur, row(norm1_g[l]), sc1, sh1, f"ln1_fwd_{l}")
        p = _mm("nn", h1.reshape(t, d), win, dims=(t, PW, d), tm=1024, tn=PW // 3, tk=d, out_dtypes=[F32],
                name=f"proj_in_{l}")[0].reshape(bl, s, PW)
        out_a = gmlp_fwd(p, prm_a, f"gmlp_fwd_{l}")
        out_b = attn_fwd(p, *prm_b, f"attn_fwd_{l}")
        xc = conv_fwd(p, conv_w_full[l], row(conv_b[l]), f"conv_fwd_{l}")
        out_c, states = ssd_fwd(xc, p, prm_c, f"ssd_fwd_{l}")
        mix = jnp.concatenate([out_a, out_b, out_c], axis=-1)
        if l == 0:
            wout = fetch(1, mix)[0].reshape(-1, d)
        mm1, x2 = _mm("nn", mix.reshape(t, d), wout, dims=(t, d, d), tm=tm_res, tn=512, tk=d, out_dtypes=[F32, F32],
                      name=f"proj_out_{l}", extras=res_extras(xcur, gt1), epi=residual)
        x2 = x2.reshape(bl, s, d)
        h2 = ln_mod_fwd(x2, row(norm2_g[l]), sc2, sh2, f"ln2_fwd_{l}")
        if l == 0:
            w1 = fetch(2, h2)[0]
        a1 = _mm("nn", h2.reshape(t, d), w1, dims=(t, dff, d), tm=1024, tn=1024, tk=d, out_dtypes=[BF16],
                 name=f"mlp1_{l}", b_spec=w1_blk(d, 1024))[0]
        if l == 0:
            w2 = fetch(3, a1)[0].reshape(dff, d)
        tm2 = min(512, s)
        mm2, x3 = _mm("nn", a1, w2, dims=(t, d, dff), tm=tm2, tn=512, tk=dff, out_dtypes=[F32, F32],
                      name=f"mlp2_{l}", extras=res_extras(x2, gt2, tm2), epi=residual, pro_a=_relu2)
        x3 = x3.reshape(bl, s, d)
        wfull[l] = (win, wout, w1, w2)
        saved.append((xcur, h1, p, xc, states, mix, mm1.reshape(bl, s, d), x2, h2, a1, mm2.reshape(bl, s, d),
                      prm_a, prm_b, prm_c))
        xcur = x3

    dx, d_final_g, loss_part = loss_head(xcur, row(final_norm_g), loss_target, "loss_head")
    loss = lax.psum(loss_part[0, 0], ("x", "y", "c"))

    def rs_swap(grads, tag):
        ss, rs, srcs, lands, token = split_copy_start(
            grads, [jax.ShapeDtypeStruct((4, g.shape[1] // 2, g.shape[2]), F32) for g in grads],
            _swap_copies, None, f"rs_swap_{tag}")
        return (ss, rs, srcs, lands), token

    def rs_begin(swap_state, tag, swapped_behind, start_behind=None):
        ss, rs, srcs, lands = swap_state
        grads, theirs = split_copy_wait(ss, rs, srcs, lands, _swap_copies, swapped_behind, f"rs_swapped_{tag}")
        sums = [add_pair(g, th, chip, ci, f"rs_add_{tag}_{i}") for i, (g, th) in enumerate(zip(grads, theirs))]
        ss, rs, srcs, lands, token = split_copy_start(
            [sm[1] for sm in sums], [jax.ShapeDtypeStruct((3,) + sm[1].shape[1:], BF16) for sm in sums],
            _scatter_copies, sums[0][0] if start_behind is None else start_behind, f"rs_start_{tag}")
        return (ss, rs, srcs, lands, [sm[0] for sm in sums]), token

    def rs_end(state, behind, tag):
        ss, rs, srcs, lands, sums_f32 = state
        _, got = split_copy_wait(ss, rs, srcs, lands, _scatter_copies, behind, f"rs_wait_{tag}")
        halves = [sum_own_recv(sf, g, ci, f"rs_sum_{tag}_{i}") for i, (sf, g) in enumerate(zip(sums_f32, got))]
        return join_halves(halves, f"rs_join_{tag}")

    small_parts = [None] * nl
    dmods = [None] * nl
    reduced = [[None] * 4 for _ in range(nl)]
    pending_rs, rs_token = [], None
    part_slots = {"a": (0, 1), "m": (2, 3)}

    def finish(behind):
        for ll, part, state in pending_rs:
            for slot, blk in zip(part_slots[part], rs_end(state, behind, f"{ll}{part}")):
                reduced[ll][slot] = blk
        pending_rs.clear()

    for l in reversed(range(nl)):
        sh1, sc1, gt1, sh2, sc2, gt2 = mods[l]
        win, wout, w1, w2 = wfull[l]
        xin, h1, p, xc, states, mix, mm1, x2, h2, a1, mm2, prm_a, prm_b, prm_c = saved[l]
        if rs_token is not None:
            gt2 = gt2 + rs_token[0, 0]
        dm2, dgt2 = gate_bwd(dx, mm2, gt2, f"gate2_bwd_{l}")
        dm2 = dm2.reshape(t, d)
        da1 = _mm("nt", dm2, w2, dims=(t, dff, d), tm=1024, tn=1024, tk=d, out_dtypes=[BF16], name=f"mlp2_dx_{l}",
                  extras=[(a1, pl.BlockSpec((1024 if t >= 1024 else t, 1024), lambda i, j, kk: (i, j)))],
                  epi=lambda acc, av: (acc * (2.0 * jnp.maximum(av, 0).astype(F32)),))[0]
        dw2 = _mm("tn", a1, dm2, dims=(dff, d, t), tm=512, tn=d, tk=2048, out_dtypes=[F32], name=f"mlp2_dw_{l}",
                  pro_a=_relu2, out_shapes=[(4, dff4, d)],
                  out_specs=[pl.BlockSpec((None, 512, d), lambda i, j, kk: (i // (dff4 // 512), i % (dff4 // 512), 0))])[0]
        dw1 = _mm("tn", h2.reshape(t, d), da1, dims=(d, dff, t), tm=512, tn=dff4, tk=2048, out_dtypes=[F32],
                  name=f"mlp1_dw_{l}", out_shapes=[(4, d, dff4)],
                  out_specs=[pl.BlockSpec((None, 512, dff4), lambda i, j, kk: (j, i, 0))])[0]
        swap_state, swap_token = rs_swap([dw1, dw2], f"{l}m")
        dh2 = _mm_nt_blocked(da1, w1, tm=512, tn=512, name=f"mlp1_dx_{l}", behind=swap_token)
        mlp_state, mlp_token = rs_begin(swap_state, f"{l}m", dh2)
        sc2 = sc2 + mlp_token[0, 0]
        dx2, dsc2, dsh2, dn2 = ln_mod_bwd(dh2.reshape(bl, s, d), x2, dx, row(norm2_g[l]), sc2, f"ln2_bwd_{l}")
        dm1, dgt1 = gate_bwd(dx2, mm1, gt1, f"gate1_bwd_{l}")
        dm1 = dm1.reshape(t, d)
        dmix = _mm("nt", dm1, wout, dims=(t, d, d), tm=1024, tn=1024, tk=d, out_dtypes=[F32],
                   name=f"proj_out_dx_{l}")[0].reshape(bl, s, d)
        dwout = _mm("tn", mix.reshape(t, d), dm1, dims=(d, d, t), tm=512, tn=d, tk=2048, out_dtypes=[F32],
                    name=f"proj_out_dw_{l}", out_shapes=[(4, d // 4, d)],
                    out_specs=[pl.BlockSpec((None, 512, d), lambda i, j, kk: (i // (d // 4 // 512), i % (d // 4 // 512), 0))])[0]
        du, dv, dlg, dlb, dws, dbst, dgng = gmlp_bwd(p, dmix, prm_a, f"gmlp_bwd_{l}")
        dq, dk, dvv, dsinks, dang = attn_bwd(p, dmix, *prm_b, f"attn_bwd_{l}")
        dxc, ddt, dz, ddtb, dalog, ddsk, dsng = ssd_bwd(xc, p, states, dmix, prm_c, f"ssd_bwd_{l}")
        dxbc, dcw, dcb = conv_bwd(p, dxc, conv_w_full[l], row(conv_b[l]), f"conv_bwd_{l}")
        dp = jnp.concatenate([du, dv, dq, dk, dvv, dz, dxbc, ddt, jnp.zeros((bl, s, PW - OFF_DT - LANES), BF16)],
                             axis=-1).reshape(t, PW)
        dwin = _mm("tn", h1.reshape(t, d), dp, dims=(d, PW, t), tm=512, tn=PW // 2, tk=2048, out_dtypes=[F32],
                   name=f"proj_in_dw_{l}")[0]
        dwin_blocks = _win_to_blocks(dwin)
        mixer_swap, swap_token = rs_swap([dwin_blocks, dwout], f"{l}a")
        dh1 = _mm("nt", dp, win, dims=(t, d, PW), tm=1024, tn=512, tk=PW, out_dtypes=[F32],
                  name=f"proj_in_dx_{l}", behind=swap_token)[0]
        dx, dsc1, dsh1, dn1 = ln_mod_bwd(dh1.reshape(bl, s, d), xin, dx2, row(norm1_g[l]), sc1, f"ln1_bwd_{l}")
        dmods[l] = jnp.concatenate([dsh1, dsc1, dgt1, dsh2, dsc2, dgt2], axis=-1).reshape(bl, 6 * d)
        small_parts[l] = [dn1, dlg, dlb, dws, dbst.T, dgng, dsinks, dang, dcw, dcb, ddtb[:, :SSM_HEADS],
                          dalog[:, :SSM_HEADS], ddsk[:, :SSM_HEADS], dsng, dn2]
        finish(dx)
        pending_rs.append((l, "m", mlp_state))
        if l > 0:
            state, rs_token = rs_begin(mixer_swap, f"{l}a", dx)
            pending_rs.append((l, "a", state))
    grad_x = dx

    big = [(w_in, m_w_in, v_w_in), (w_out, m_w_out, v_w_out), (w_mlp1, m_w_mlp1, v_w_mlp1), (w_mlp2, m_w_mlp2, v_w_mlp2)]
    big_out = [None] * 4
    for l in reversed(range(1, nl)):
        for i, (wt, mt, vt) in enumerate(big):
            if i > 0:
                big_out[i] = adamw_layer(wt, mt, vt, reduced[l][i], l, big_out[i], f"adamw_big_{i}_{l}")

    small_names = [norm1_g, gm_ln_g, gm_ln_b, gm_ws, gm_bs, gm_norm_g, attn_sinks, attn_norm_g, None, conv_b, dt_bias,
                   a_log, d_skip, ssm_norm_g, norm2_g]
    n_small = len(small_names)
    per_param = [jnp.stack([small_parts[l][i].reshape(-1) for l in range(nl)]) for i in range(n_small)]
    small_vec = _pack(per_param + [d_final_g])
    rs_small = small_vec.shape[0]
    dmod_local = jnp.stack(dmods, axis=1)
    g2 = all_gather_small(jnp.concatenate([small_vec, _pack([dmod_local])], axis=0), "ag_small")
    state, rs_token = rs_begin(mixer_swap, "0a", grad_x, start_behind=g2)
    pending_rs.append((0, "a", state))
    g2 = g2 + rs_token[0, 0]
    g_small = sum_devices(g2[:, :rs_small, :], "sum_small")
    dmod_all = g2[:, rs_small:, :].reshape(8, -1)[:, :bl * nl * 6 * d].reshape(nex, nl * 6 * d)
    g_ada_b = sum_devices(dmod_all.reshape(nex, -1, LANES), "sum_ada_b").reshape(nl, 6 * d)
    shapes = [(nl, int(np.prod(small_parts[0][i].shape))) for i in range(n_small)] + [(d,)]
    g_list = _unpack(g_small, shapes)
    g_conv_w = lax.dynamic_slice(g_list[8].reshape(nl, CONV_K, CCH), (0, 0, chip * cw_w), (nl, CONV_K, cw_w))

    dm_cols = lax.dynamic_slice(dmod_all.reshape(nex, nl, 6 * d), (0, 0, chip * mod_w), (nex, nl, mod_w))
    g_ada_w = _mm("tn", c_all, dm_cols.reshape(nex, nl * mod_w), dims=(d, nl * mod_w, nex), tm=512, tn=512, tk=nex,
                  out_dtypes=[F32], name="ada_w_grad", pro_a=c_act, pro_b=to_bf16, out_shapes=[(nl, d, mod_w)],
                  out_specs=[pl.BlockSpec((None, 512, 512), lambda i, j, kk: (j // (mod_w // 512), i, j % (mod_w // 512)))])[0]
    d_ada_w, m_ada_w_n, v_ada_w_n = [a.reshape(ada_w.shape) for a in
                                     adamw(_rows2d(ada_w), _rows2d(m_ada_w), _rows2d(v_ada_w), _rows2d(g_ada_w), "adamw_ada_w")]

    smalls = {
        "ada_b": (ada_b, m_ada_b, v_ada_b, g_ada_b), "norm1_g": (norm1_g, m_norm1_g, v_norm1_g, g_list[0]),
        "gm_ln_g": (gm_ln_g, m_gm_ln_g, v_gm_ln_g, g_list[1]), "gm_ln_b": (gm_ln_b, m_gm_ln_b, v_gm_ln_b, g_list[2]),
        "gm_ws": (gm_ws, m_gm_ws, v_gm_ws, g_list[3]), "gm_bs": (gm_bs, m_gm_bs, v_gm_bs, g_list[4]),
        "gm_norm_g": (gm_norm_g, m_gm_norm_g, v_gm_norm_g, g_list[5]),
        "attn_sinks": (attn_sinks, m_attn_sinks, v_attn_sinks, g_list[6]),
        "attn_norm_g": (attn_norm_g, m_attn_norm_g, v_attn_norm_g, g_list[7]),
        "conv_w": (conv_w, m_conv_w, v_conv_w, g_conv_w), "conv_b": (conv_b, m_conv_b, v_conv_b, g_list[9]),
        "dt_bias": (dt_bias, m_dt_bias, v_dt_bias, g_list[10]), "a_log": (a_log, m_a_log, v_a_log, g_list[11]),
        "d_skip": (d_skip, m_d_skip, v_d_skip, g_list[12]),
        "ssm_norm_g": (ssm_norm_g, m_ssm_norm_g, v_ssm_norm_g, g_list[13]),
        "norm2_g": (norm2_g, m_norm2_g, v_norm2_g, g_list[14]),
        "final_norm_g": (final_norm_g, m_final_norm_g, v_final_norm_g, g_list[15]),
    }
    keys = list(smalls)
    wv, mv, vv_, gv = [_pack([smalls[k][i].reshape(smalls[k][0].shape) for k in keys]) for i in range(4)]
    sd_, sm_, sv_ = adamw(wv, mv, vv_, gv, "adamw_small")
    shp = [smalls[k][0].shape for k in keys]
    small_out = {k: (smalls[k][3].reshape(smalls[k][0].shape), a, b, cc)
                 for k, a, b, cc in zip(keys, _unpack(sd_, shp), _unpack(sm_, shp), _unpack(sv_, shp))}

    late = jnp.zeros((8, LANES), F32) + (sv_[0, 0] + v_ada_w_n[0, 0, 0])
    for bo in big_out:
        if bo is not None:
            late = late + bo[3][nl - 1, 0, 0]
    finish(late)
    for i, (wt, mt, vt) in enumerate(big):
        if i > 0:
            big_out[i] = adamw_layer(wt, mt, vt, reduced[0][i], 0, big_out[i], f"adamw_big_{i}_0")
    minor_first = lambda a: jnp.transpose(a, (2, 0, 1))
    g_in = jnp.stack([reduced[l][0].T for l in range(nl)], axis=1)
    back = lambda a: jnp.transpose(a, (1, 2, 0))
    big_out[0] = [back(a) for a in [g_in, *adamw_minor_rows(minor_first(w_in), minor_first(m_w_in),
                                                            minor_first(v_w_in), g_in, "adamw_w_in")]]

    out = {"ada_w": (g_ada_w, d_ada_w, m_ada_w_n, v_ada_w_n), "w_in": big_out[0], "w_out": big_out[1],
           "w_mlp1": big_out[2], "w_mlp2": big_out[3], **small_out}
    order = ["ada_w", "ada_b", "norm1_g", "w_in", "gm_ln_g", "gm_ln_b", "gm_ws", "gm_bs", "gm_norm_g", "attn_sinks",
             "attn_norm_g", "conv_w", "conv_b", "dt_bias", "a_log", "d_skip", "ssm_norm_g", "w_out", "norm2_g",
             "w_mlp1", "w_mlp2", "final_norm_g"]
    return (loss, grad_x, *[out[k][0] for k in order], *[out[k][1] for k in order],
            *[out[k][2] for k in order], *[out[k][3] for k in order])
```

```python
import functools
import math

import jax
import jax.numpy as jnp
import numpy as np
from jax import lax
from jax.experimental import pallas as pl
from jax.experimental.pallas import tpu as pltpu

F32 = jnp.float32
BF16 = jnp.bfloat16
HI = lax.Precision.HIGHEST
MESH = pl.DeviceIdType.MESH

CHUNK = 128
GM_HEADS, GM_HD = 4, 128
ATT_HEADS, ATT_KV, ATT_HD = 8, 2, 64
WINDOW = 128
SSM_HEADS, SSM_HD, SSM_GROUPS, SSM_STATE, CONV_K = 16, 64, 2, 128, 4
EPS = 1e-6
LN_EPS = 1e-5
NEG = -1e30
LANES = 128

GMW = GM_HEADS * GM_HD
ATW = ATT_HEADS * ATT_HD
KVW = ATT_KV * ATT_HD
SSW = SSM_HEADS * SSM_HD
BCW = SSM_GROUPS * SSM_STATE
CCH = SSW + 2 * BCW
GRW = SSW // SSM_GROUPS
IN_SIZES = (GMW, GMW, ATW, KVW, KVW, SSW, CCH, SSM_HEADS)
IN_W = sum(IN_SIZES)
OFF_U, OFF_V, OFF_Q, OFF_K, OFF_VV, OFF_Z, OFF_XBC, OFF_DT = 0, 512, 1024, 1536, 1664, 1792, 2816, 4352
ZB = 256
PW = 4608

ADAM_LR, ADAM_B1, ADAM_B2, ADAM_EPS, ADAM_WD, ADAM_STEP = 0.001, 0.9, 0.999, 1e-08, 0.01, 10

VMEM_LIMIT = 56 * 1024 * 1024


def _cp(sem=None):
    return pltpu.CompilerParams(dimension_semantics=sem, vmem_limit_bytes=VMEM_LIMIT)


_DN = {"nn": (((1,), (0,)), ((), ())), "nt": (((1,), (1,)), ((), ())), "tn": (((0,), (0,)), ((), ()))}


def _dot(form, a, b):
    return lax.dot_general(a.astype(BF16), b.astype(BF16), _DN[form], preferred_element_type=F32)


@jax.custom_vjp
def _nn(a, b):
    return _dot("nn", a, b)


@jax.custom_vjp
def _nt(a, b):
    return _dot("nt", a, b)


@jax.custom_vjp
def _tn(a, b):
    return _dot("tn", a, b)


_nn.defvjp(lambda a, b: (_dot("nn", a, b), (a, b)), lambda r, g: (_dot("nt", g, r[1]), _dot("tn", r[0], g)))
_nt.defvjp(lambda a, b: (_dot("nt", a, b), (a, b)), lambda r, g: (_dot("nn", g, r[1]), _dot("tn", g, r[0])))
_tn.defvjp(lambda a, b: (_dot("tn", a, b), (a, b)), lambda r, g: (_dot("nt", r[1], g), _dot("nn", r[0], g)))


def _hdot(a, b):
    return jnp.dot(a, b, precision=HI, preferred_element_type=F32)


def _silu(x):
    return x * (1.0 / (1.0 + jnp.exp(-x)))


def _softplus(x):
    return jnp.maximum(x, 0.0) + jnp.log1p(jnp.exp(-jnp.abs(x)))


def _gelu(x):
    return 0.5 * x * (1.0 + jnp.tanh(math.sqrt(2.0 / math.pi) * (x + 0.044715 * (x * x * x))))


def _rms(y, g):
    return y * lax.rsqrt(jnp.mean(y * y, axis=-1, keepdims=True) + EPS) * g


def _mm(form, a, b, *, dims, tm, tn, tk, out_dtypes, name, a_spec=None, b_spec=None, out_specs=None,
        out_shapes=None, extras=(), epi=None, pro_a=None, pro_b=None, behind=None):
    m, n, k = dims
    tm, tn, tk = min(tm, m), min(tn, n), min(tk, k)
    assert m % tm == 0 and n % tn == 0 and k % tk == 0, (name, dims, tm, tn, tk)
    nk = k // tk
    if a_spec is None:
        a_spec = (pl.BlockSpec((tk, tm), lambda i, j, kk: (kk, i)) if form == "tn"
                  else pl.BlockSpec((tm, tk), lambda i, j, kk: (i, kk)))
    if b_spec is None:
        b_spec = (pl.BlockSpec((tn, tk), lambda i, j, kk: (j, kk)) if form == "nt"
                  else pl.BlockSpec((tk, tn), lambda i, j, kk: (kk, j)))
    n_out = len(out_dtypes)
    if out_specs is None:
        out_specs = [pl.BlockSpec((tm, tn), lambda i, j, kk: (i, j))] * n_out
    if out_shapes is None:
        out_shapes = [(m, n)] * n_out
    ne = len(extras)
    n_behind = 0 if behind is None else 1

    def body(*refs):
        a_ref, b_ref = refs[0], refs[1]
        ex = refs[2:2 + ne]
        outs = refs[2 + ne + n_behind:2 + ne + n_behind + n_out]

        def write(val):
            res = epi(val, *[e[...] for e in ex]) if epi is not None else (val,)
            for o, r in zip(outs, res):
                o[...] = r.astype(o.dtype)

        av = a_ref[...]
        if pro_a is not None:
            av = pro_a(av)
        bv = b_ref[...]
        if pro_b is not None:
            bv = pro_b(bv)
        part = lax.dot_general(av, bv, _DN[form], preferred_element_type=F32)
        if nk == 1:
            write(part)
        else:
            acc = refs[-1]
            kk = pl.program_id(2)

            @pl.when(kk == 0)
            def _():
                acc[...] = part

            @pl.when(kk > 0)
            def _():
                acc[...] += part

            @pl.when(kk == nk - 1)
            def _():
                write(acc[...])

    res = pl.pallas_call(
        body, name=name, grid=(m // tm, n // tn, nk),
        in_specs=[a_spec, b_spec] + [s for _, s in extras] + [_ANY] * n_behind,
        out_specs=out_specs,
        out_shape=[jax.ShapeDtypeStruct(s, d) for s, d in zip(out_shapes, out_dtypes)],
        scratch_shapes=[pltpu.VMEM((tm, tn), F32)] if nk > 1 else [],
        compiler_params=_cp(("parallel", "parallel", "arbitrary")),
    )(a, b, *[e for e, _ in extras], *([behind] if n_behind else []))
    return res


def _mm_nt_blocked(a, b, *, tm, tn, name, behind=None):
    m = a.shape[0]
    nparts, n, f = b.shape
    tm, tn = min(tm, m), min(tn, n)
    n_behind = 0 if behind is None else 1

    def body(a_ref, *rest):
        b_refs, o_ref = rest[:nparts], rest[nparts + n_behind]
        acc = None
        for k in range(nparts):
            part = lax.dot_general(a_ref[:, k * f:(k + 1) * f], b_refs[k][...], _DN["nt"], preferred_element_type=F32)
            acc = part if acc is None else acc + part
        o_ref[...] = acc

    return pl.pallas_call(
        body, name=name, grid=(m // tm, n // tn),
        in_specs=[pl.BlockSpec((tm, nparts * f), lambda i, j: (i, 0))]
        + [pl.BlockSpec((None, tn, f), lambda i, j, k=k: (k, j, 0)) for k in range(nparts)] + [_ANY] * n_behind,
        out_specs=pl.BlockSpec((tm, tn), lambda i, j: (i, j)),
        out_shape=jax.ShapeDtypeStruct((m, n), F32),
        compiler_params=_cp(("parallel", "parallel")),
    )(a, *([b] * nparts), *([behind] if n_behind else []))


def _row_tile(s):
    return min(512, s)


def ln_mod_fwd(x, g, sc, sh, name):
    bsz, s, d = x.shape
    ts = _row_tile(s)

    def body(x_ref, g_ref, sc_ref, sh_ref, o_ref):
        xv = x_ref[...]
        r = lax.rsqrt(jnp.mean(xv * xv, axis=-1, keepdims=True) + EPS)
        o_ref[...] = ((xv * r * g_ref[...]) * (1.0 + sc_ref[...]) + sh_ref[...]).astype(o_ref.dtype)

    row = pl.BlockSpec((None, ts, d), lambda b, i: (b, i, 0))
    vec = pl.BlockSpec((None, 1, d), lambda b, i: (b, 0, 0))
    return pl.pallas_call(
        body, name=name, grid=(bsz, s // ts),
        in_specs=[row, pl.BlockSpec((1, d), lambda b, i: (0, 0)), vec, vec],
        out_specs=row, out_shape=jax.ShapeDtypeStruct(x.shape, BF16),
        compiler_params=_cp(("parallel", "parallel")),
    )(x, g, sc, sh)


def ln_mod_bwd(dh, x, dres, g, sc, name):
    bsz, s, d = x.shape
    ts = _row_tile(s)

    def body(dh_ref, x_ref, dres_ref, g_ref, sc_ref, dx_ref, dsc_ref, dsh_ref, dg_ref):
        b, i = pl.program_id(0), pl.program_id(1)
        xv, dhv, gv = x_ref[...], dh_ref[...], g_ref[...]
        r = lax.rsqrt(jnp.mean(xv * xv, axis=-1, keepdims=True) + EPS)
        xn = xv * r
        a = dhv * (1.0 + sc_ref[...])
        dxn = a * gv
        dx_ref[...] = dres_ref[...] + r * (dxn - xn * jnp.mean(dxn * xn, axis=-1, keepdims=True))
        p_sc = jnp.sum(dhv * (xn * gv), axis=0, keepdims=True)
        p_sh = jnp.sum(dhv, axis=0, keepdims=True)
        p_g = jnp.sum(a * xn, axis=0, keepdims=True)

        @pl.when(i == 0)
        def _():
            dsc_ref[...] = p_sc
            dsh_ref[...] = p_sh

        @pl.when(i > 0)
        def _():
            dsc_ref[...] += p_sc
            dsh_ref[...] += p_sh

        @pl.when((i == 0) & (b == 0))
        def _():
            dg_ref[...] = p_g

        @pl.when((i > 0) | (b > 0))
        def _():
            dg_ref[...] += p_g

    row = pl.BlockSpec((None, ts, d), lambda b, i: (b, i, 0))
    vec = pl.BlockSpec((None, 1, d), lambda b, i: (b, 0, 0))
    one = pl.BlockSpec((1, d), lambda b, i: (0, 0))
    return pl.pallas_call(
        body, name=name, grid=(bsz, s // ts),
        in_specs=[row, row, row, one, vec],
        out_specs=[row, vec, vec, one],
        out_shape=[jax.ShapeDtypeStruct(x.shape, F32), jax.ShapeDtypeStruct((bsz, 1, d), F32),
                   jax.ShapeDtypeStruct((bsz, 1, d), F32), jax.ShapeDtypeStruct((1, d), F32)],
        compiler_params=_cp(("arbitrary", "arbitrary")),
    )(dh, x, dres, g, sc)


def gate_bwd(dx, mm, gate, name):
    bsz, s, d = dx.shape
    ts = _row_tile(s)

    def body(dx_ref, m_ref, g_ref, dm_ref, dg_ref):
        i = pl.program_id(1)
        dxv = dx_ref[...]
        dm_ref[...] = (dxv * g_ref[...]).astype(dm_ref.dtype)
        p = jnp.sum(dxv * m_ref[...], axis=0, keepdims=True)

        @pl.when(i == 0)
        def _():
            dg_ref[...] = p

        @pl.when(i > 0)
        def _():
            dg_ref[...] += p

    row = pl.BlockSpec((None, ts, d), lambda b, i: (b, i, 0))
    vec = pl.BlockSpec((None, 1, d), lambda b, i: (b, 0, 0))
    return pl.pallas_call(
        body, name=name, grid=(bsz, s // ts),
        in_specs=[row, row, vec], out_specs=[row, vec],
        out_shape=[jax.ShapeDtypeStruct(dx.shape, BF16), jax.ShapeDtypeStruct((bsz, 1, d), F32)],
        compiler_params=_cp(("parallel", "arbitrary")),
    )(dx, mm, gate)


def loss_head(x, g, tgt, name):
    bsz, s, d = x.shape
    ts = _row_tile(s)

    def body(x_ref, g_ref, t_ref, dx_ref, dg_ref, l_ref):
        b, i = pl.program_id(0), pl.program_id(1)
        xv, gv = x_ref[...], g_ref[...]
        r = lax.rsqrt(jnp.mean(xv * xv, axis=-1, keepdims=True) + EPS)
        xn = xv * r
        e = xn * gv - t_ref[...]
        dy = e * (1.0 / d)
        dxn = dy * gv
        dx_ref[...] = r * (dxn - xn * jnp.mean(dxn * xn, axis=-1, keepdims=True))
        p_g = jnp.sum(dy * xn, axis=0, keepdims=True)
        p_l = jnp.zeros((1, LANES), F32) + jnp.sum(e * e) * (0.5 / d)
        first = (i == 0) & (b == 0)

        @pl.when(first)
        def _():
            dg_ref[...] = p_g
            l_ref[...] = p_l

        @pl.when(jnp.logical_not(first))
        def _():
            dg_ref[...] += p_g
            l_ref[...] += p_l

    row = pl.BlockSpec((None, ts, d), lambda b, i: (b, i, 0))
    one = pl.BlockSpec((1, d), lambda b, i: (0, 0))
    return pl.pallas_call(
        body, name=name, grid=(bsz, s // ts),
        in_specs=[row, one, row],
        out_specs=[row, one, pl.BlockSpec((1, LANES), lambda b, i: (0, 0))],
        out_shape=[jax.ShapeDtypeStruct(x.shape, F32), jax.ShapeDtypeStruct((1, d), F32),
                   jax.ShapeDtypeStruct((1, LANES), F32)],
        compiler_params=_cp(("arbitrary", "arbitrary")),
    )(x, g, tgt)


def _gmlp_chunk(u_raw, v_raw, ln_g, ln_b, w, bs_t, out_g):
    c = u_raw.shape[0]
    u, v = _gelu(u_raw), _gelu(v_raw)
    tril = lax.broadcasted_iota(jnp.int32, (c, c), 0) >= lax.broadcasted_iota(jnp.int32, (c, c), 1)
    ys = []
    for h in range(GM_HEADS):
        sl = slice(h * GM_HD, (h + 1) * GM_HD)
        vh = v[:, sl]
        xc = vh - jnp.mean(vh, axis=-1, keepdims=True)
        vn = xc * lax.rsqrt(jnp.mean(xc * xc, axis=-1, keepdims=True) + LN_EPS) * ln_g[:, sl] + ln_b[:, sl]
        gate = _nn(jnp.where(tril, w[h], 0.0), vn) + bs_t[:, h:h + 1]
        ys.append(u[:, sl] * gate)
    return _rms(jnp.concatenate(ys, axis=1), out_g)


def _gmlp_specs(bsz, nc):
    seg = lambda off: pl.BlockSpec((None, CHUNK, GMW), lambda b, c: (b, c, off // GMW))
    full = lambda shape: pl.BlockSpec(shape, lambda b, c: (0,) * len(shape))
    par = [full((1, GMW)), full((1, GMW)), full((GM_HEADS, CHUNK, CHUNK)), full((CHUNK, GM_HEADS)), full((1, GMW))]
    return seg, full, par


def gmlp_fwd(p, prm, name):
    bsz, s, _ = p.shape
    nc = s // CHUNK
    seg, _, par = _gmlp_specs(bsz, nc)

    def body(u_ref, v_ref, lg, lb, w, bt, og, o_ref):
        o_ref[...] = _gmlp_chunk(u_ref[...], v_ref[...], lg[...], lb[...], w[...], bt[...], og[...]).astype(o_ref.dtype)

    return pl.pallas_call(
        body, name=name, grid=(bsz, nc),
        in_specs=[seg(OFF_U), seg(OFF_V)] + par,
        out_specs=pl.BlockSpec((None, CHUNK, GMW), lambda b, c: (b, c, 0)),
        out_shape=jax.ShapeDtypeStruct((bsz, s, GMW), BF16),
        compiler_params=_cp(("parallel", "parallel")),
    )(p, p, *prm)


def _accumulate(first, refs, vals):
    @pl.when(first)
    def _():
        for r, v in zip(refs, vals):
            r[...] = v

    @pl.when(jnp.logical_not(first))
    def _():
        for r, v in zip(refs, vals):
            r[...] += v


def gmlp_bwd(p, dmix, prm, name):
    bsz, s, _ = p.shape
    nc = s // CHUNK
    seg, full, par = _gmlp_specs(bsz, nc)

    def body(u_ref, v_ref, do_ref, lg, lb, w, bt, og, du_ref, dv_ref, *dpar):
        first = (pl.program_id(0) == 0) & (pl.program_id(1) == 0)
        _, vjp = jax.vjp(_gmlp_chunk, u_ref[...], v_ref[...], lg[...], lb[...], w[...], bt[...], og[...])
        gr = vjp(do_ref[...])
        du_ref[...] = gr[0].astype(du_ref.dtype)
        dv_ref[...] = gr[1].astype(dv_ref.dtype)
        _accumulate(first, dpar, gr[2:])

    out_seg = pl.BlockSpec((None, CHUNK, GMW), lambda b, c: (b, c, 0))
    return pl.pallas_call(
        body, name=name, grid=(bsz, nc),
        in_specs=[seg(OFF_U), seg(OFF_V), out_seg] + par,
        out_specs=[out_seg, out_seg] + par,
        out_shape=[jax.ShapeDtypeStruct((bsz, s, GMW), BF16)] * 2 + [jax.ShapeDtypeStruct(x.shape, F32) for x in prm],
        compiler_params=_cp(("arbitrary", "arbitrary")),
    )(p, p, dmix, *prm)


def _attn_block(q, kp, kc, vp, vc, sinks, out_g, has_prev):
    w = q.shape[0]
    k2 = jnp.concatenate([kp, kc], axis=0)
    v2 = jnp.concatenate([vp, vc], axis=0)
    qi = lax.broadcasted_iota(jnp.int32, (w, 2 * w), 0)
    kj = lax.broadcasted_iota(jnp.int32, (w, 2 * w), 1)
    diff = qi + w - kj
    grp = ATT_HEADS // ATT_KV
    valid = (diff >= 0) & (diff < w) & ((kj >= w) | has_prev)
    valid = jnp.concatenate([valid] * grp, axis=0)
    outs = []
    for kv in range(ATT_KV):
        kh = k2[:, kv * ATT_HD:(kv + 1) * ATT_HD]
        vh = v2[:, kv * ATT_HD:(kv + 1) * ATT_HD]
        heads = range(kv * grp, (kv + 1) * grp)
        qs = jnp.concatenate([q[:, h * ATT_HD:(h + 1) * ATT_HD] for h in heads], axis=0)
        sink = jnp.concatenate([jnp.broadcast_to(sinks[:, h:h + 1], (w, 1)) for h in heads], axis=0)
        sc = jnp.where(valid, _nt(qs, kh) * (ATT_HD ** -0.5), NEG)
        m = jnp.maximum(jnp.max(sc, axis=-1, keepdims=True), sink)
        e = jnp.exp(sc - m)
        pr = e / (jnp.sum(e, axis=-1, keepdims=True) + jnp.exp(sink - m))
        o = _nn(pr, vh)
        outs += [o[gi * w:(gi + 1) * w] for gi in range(grp)]
    return _rms(jnp.concatenate(outs, axis=1), out_g)


ATT_QB_FWD, ATT_QB_BWD = 8, 4


def _attn_tiles(s, windows=ATT_QB_BWD):
    qb = min(windows, s // WINDOW)
    return qb, qb * WINDOW, s // (qb * WINDOW)


def attn_fwd(p, sinks, out_g, name):
    bsz, s, _ = p.shape
    qb, rows, steps = _attn_tiles(s, ATT_QB_FWD)

    def body(q_ref, kp_ref, kc_ref, vp_ref, vc_ref, s_ref, g_ref, o_ref):
        n = pl.program_id(1)
        for w in range(qb):
            sl = pl.ds(w * WINDOW, WINDOW)
            before = pl.ds((w - 1) * WINDOW, WINDOW)
            kp = kp_ref[...] if w == 0 else kc_ref[before, :]
            vp = vp_ref[...] if w == 0 else vc_ref[before, :]
            o_ref[sl, :] = _attn_block(q_ref[sl, :], kp, kc_ref[sl, :], vp, vc_ref[sl, :], s_ref[...], g_ref[...],
                                       (n > 0) if w == 0 else True).astype(o_ref.dtype)

    cur = lambda off: pl.BlockSpec((None, rows, KVW), lambda b, n: (b, n, off // KVW))
    prev = lambda off: pl.BlockSpec((None, WINDOW, KVW), lambda b, n: (b, jnp.maximum(n * qb - 1, 0), off // KVW))
    return pl.pallas_call(
        body, name=name, grid=(bsz, steps),
        in_specs=[pl.BlockSpec((None, rows, ATW), lambda b, n: (b, n, OFF_Q // ATW)),
                  prev(OFF_K), cur(OFF_K), prev(OFF_VV), cur(OFF_VV),
                  pl.BlockSpec((1, ATT_HEADS), lambda b, n: (0, 0)), pl.BlockSpec((1, ATW), lambda b, n: (0, 0))],
        out_specs=pl.BlockSpec((None, rows, ATW), lambda b, n: (b, n, 0)),
        out_shape=jax.ShapeDtypeStruct((bsz, s, ATW), BF16),
        compiler_params=_cp(("parallel", "parallel")),
    )(p, p, p, p, p, sinks, out_g)


def attn_bwd(p, dmix, sinks, out_g, name):
    bsz, s, _ = p.shape
    qb, rows, steps = _attn_tiles(s)
    last = pl.ds(rows - WINDOW, WINDOW)

    def body(q_ref, kp_ref, kc_ref, vp_ref, vc_ref, do_ref, s_ref, g_ref,
             dq_ref, dk_ref, dv_ref, ds_ref, dg_ref, ck, cv):
        b, n = pl.program_id(0), pl.program_id(1)

        @pl.when(n == 0)
        def _():
            ck[...] = jnp.zeros_like(ck)
            cv[...] = jnp.zeros_like(cv)

        @pl.when(n < steps)
        def _():
            grads = []
            for w in range(qb):
                sl = pl.ds(w * WINDOW, WINDOW)
                before = pl.ds((w - 1) * WINDOW, WINDOW)
                kp = kp_ref[...] if w == 0 else kc_ref[before, :]
                vp = vp_ref[...] if w == 0 else vc_ref[before, :]
                fn = functools.partial(_attn_block, has_prev=(n > 0) if w == 0 else True)
                _, vjp = jax.vjp(fn, q_ref[sl, :], kp, kc_ref[sl, :], vp, vc_ref[sl, :], s_ref[...], g_ref[...])
                grads.append(vjp(do_ref[sl, :]))
                dq_ref[sl, :] = grads[-1][0].astype(dq_ref.dtype)
            dk_ref[...] = ck[...].astype(dk_ref.dtype)
            dv_ref[...] = cv[...].astype(dv_ref.dtype)
            dk_ref[last, :] = (ck[last, :] + grads[0][1]).astype(dk_ref.dtype)
            dv_ref[last, :] = (cv[last, :] + grads[0][3]).astype(dv_ref.dtype)
            for w in range(qb):
                sl = pl.ds(w * WINDOW, WINDOW)
                ck[sl, :] = grads[w][2] + (grads[w + 1][1] if w + 1 < qb else 0.0)
                cv[sl, :] = grads[w][4] + (grads[w + 1][3] if w + 1 < qb else 0.0)
            dsk = functools.reduce(lambda u, v: u + v, [g[5] for g in grads])
            dgg = functools.reduce(lambda u, v: u + v, [g[6] for g in grads])
            _accumulate((b == 0) & (n == 0), (ds_ref, dg_ref), (dsk, dgg))

        @pl.when(n == steps)
        def _():
            dk_ref[...] = ck[...].astype(dk_ref.dtype)
            dv_ref[...] = cv[...].astype(dv_ref.dtype)

    at = lambda n: jnp.minimum(n, steps - 1)
    cur = lambda off: pl.BlockSpec((None, rows, KVW), lambda b, n: (b, at(n), off // KVW))
    prev = lambda off: pl.BlockSpec((None, WINDOW, KVW), lambda b, n: (b, jnp.maximum(at(n) * qb - 1, 0), off // KVW))
    kv_out = pl.BlockSpec((None, rows, KVW), lambda b, n: (b, jnp.maximum(n - 1, 0), 0))
    return pl.pallas_call(
        body, name=name, grid=(bsz, steps + 1),
        in_specs=[pl.BlockSpec((None, rows, ATW), lambda b, n: (b, at(n), OFF_Q // ATW)),
                  prev(OFF_K), cur(OFF_K), prev(OFF_VV), cur(OFF_VV),
                  pl.BlockSpec((None, rows, ATW), lambda b, n: (b, at(n), GMW // ATW)),
                  pl.BlockSpec((1, ATT_HEADS), lambda b, n: (0, 0)), pl.BlockSpec((1, ATW), lambda b, n: (0, 0))],
        out_specs=[pl.BlockSpec((None, rows, ATW), lambda b, n: (b, at(n), 0)), kv_out, kv_out,
                   pl.BlockSpec((1, ATT_HEADS), lambda b, n: (0, 0)), pl.BlockSpec((1, ATW), lambda b, n: (0, 0))],
        out_shape=[jax.ShapeDtypeStruct((bsz, s, ATW), BF16), jax.ShapeDtypeStruct((bsz, s, KVW), BF16),
                   jax.ShapeDtypeStruct((bsz, s, KVW), BF16), jax.ShapeDtypeStruct((1, ATT_HEADS), F32),
                   jax.ShapeDtypeStruct((1, ATW), F32)],
        scratch_shapes=[pltpu.VMEM((rows, KVW), F32), pltpu.VMEM((rows, KVW), F32)],
        compiler_params=_cp(("arbitrary", "arbitrary")),
    )(p, p, p, p, p, dmix, sinks, out_g)


CONV_CT = 256


def _shift_down(x, j):
    if j == 0:
        return x
    rows = lax.broadcasted_iota(jnp.int32, x.shape, 0)
    return jnp.where(rows >= j, pltpu.roll(x, j, 0), 0.0)


def _shift_up(x, j):
    if j == 0:
        return x
    s = x.shape[0]
    rows = lax.broadcasted_iota(jnp.int32, x.shape, 0)
    return jnp.where(rows < s - j, pltpu.roll(x, s - j, 0), 0.0)


def conv_fwd(p, w, bias, name):
    bsz, s, _ = p.shape

    def body(x_ref, w_ref, b_ref, o_ref):
        xv, wv = x_ref[...], w_ref[...]
        pre = b_ref[...] + sum(wv[k:k + 1, :] * _shift_down(xv, CONV_K - 1 - k) for k in range(CONV_K))
        o_ref[...] = _silu(pre)

    blk = pl.BlockSpec((None, s, CONV_CT), lambda b, j: (b, 0, j))
    src = pl.BlockSpec((None, s, CONV_CT), lambda b, j: (b, 0, OFF_XBC // CONV_CT + j))
    return pl.pallas_call(
        body, name=name, grid=(bsz, CCH // CONV_CT),
        in_specs=[src, pl.BlockSpec((CONV_K, CONV_CT), lambda b, j: (0, j)), pl.BlockSpec((1, CONV_CT), lambda b, j: (0, j))],
        out_specs=blk, out_shape=jax.ShapeDtypeStruct((bsz, s, CCH), F32),
        compiler_params=_cp(("parallel", "parallel")),
    )(p, w, bias)


def conv_bwd(p, dxc, w, bias, name):
    bsz, s, _ = p.shape

    def body(x_ref, d_ref, w_ref, b_ref, dx_ref, dw_ref, db_ref):
        b = pl.program_id(1)
        xv, wv = x_ref[...], w_ref[...]
        xs = [_shift_down(xv, CONV_K - 1 - k) for k in range(CONV_K)]
        pre = b_ref[...] + sum(wv[k:k + 1, :] * xs[k] for k in range(CONV_K))
        sg = 1.0 / (1.0 + jnp.exp(-pre))
        dpre = d_ref[...] * (sg * (1.0 + pre * (1.0 - sg)))
        dx_ref[...] = sum(wv[k:k + 1, :] * _shift_up(dpre, CONV_K - 1 - k) for k in range(CONV_K)).astype(dx_ref.dtype)
        p_w = jnp.concatenate([jnp.sum(dpre * xs[k], axis=0, keepdims=True) for k in range(CONV_K)], axis=0)
        p_b = jnp.sum(dpre, axis=0, keepdims=True)
        _accumulate(b == 0, (dw_ref, db_ref), (p_w, p_b))

    blk = pl.BlockSpec((None, s, CONV_CT), lambda j, b: (b, 0, j))
    src = pl.BlockSpec((None, s, CONV_CT), lambda j, b: (b, 0, OFF_XBC // CONV_CT + j))
    wsp = pl.BlockSpec((CONV_K, CONV_CT), lambda j, b: (0, j))
    bsp = pl.BlockSpec((1, CONV_CT), lambda j, b: (0, j))
    return pl.pallas_call(
        body, name=name, grid=(CCH // CONV_CT, bsz),
        in_specs=[src, blk, wsp, bsp], out_specs=[blk, wsp, bsp],
        out_shape=[jax.ShapeDtypeStruct((bsz, s, CCH), BF16), jax.ShapeDtypeStruct((CONV_K, CCH), F32),
                   jax.ShapeDtypeStruct((1, CCH), F32)],
        compiler_params=_cp(("parallel", "arbitrary")),
    )(p, dxc, w, bias)


def _ssd_consts():
    c = CHUNK
    r = lax.broadcasted_iota(jnp.int32, (c, c), 0)
    q = lax.broadcasted_iota(jnp.int32, (c, c), 1)
    hrow = lax.broadcasted_iota(jnp.int32, (LANES, SSW), 0)
    hcol = lax.broadcasted_iota(jnp.int32, (LANES, SSW), 1) // SSM_HD
    expand = (hrow == hcol).astype(F32)
    return expand, (r >= q).astype(F32), (r <= q).astype(F32), r >= q


def _ssd_chunk(xc, dtr, z, prev_t, dt_bias, a_log, d_skip, norm_g):
    c = xc.shape[0]
    expand, tril1, triu1, causal = _ssd_consts()
    xs, bm, cm = xc[:, :SSW], xc[:, SSW:SSW + BCW], xc[:, SSW + BCW:]
    dt = _softplus(dtr + dt_bias)
    da = dt * (-jnp.exp(a_log))
    a_cs = _hdot(tril1, da)
    a_cs_t = _hdot(da.T, triu1)
    dt_e = _hdot(dt, expand)
    acs_e = _hdot(a_cs, expand)
    alast_e = acs_e[c - 1:c, :]
    dsk_e = _hdot(jnp.broadcast_to(d_skip, (8, LANES)), expand)[0:1, :]
    xdt = xs * dt_e
    hg = SSM_HEADS // SSM_GROUPS
    ys, new_t = [], []
    for g in range(SSM_GROUPS):
        bg = bm[:, g * SSM_STATE:(g + 1) * SSM_STATE]
        cg = cm[:, g * SSM_STATE:(g + 1) * SSM_STATE]
        sl = slice(g * GRW, (g + 1) * GRW)
        cb = _nt(cg, bg)
        xdt_g = xdt[:, sl]
        st = _tn(bg, xdt_g * jnp.exp(alast_e[:, sl] - acs_e[:, sl]))
        new_t.append(prev_t[:, sl] * jnp.exp(alast_e[:, sl]) + st)
        y_off = _nn(cg, prev_t[:, sl]) * jnp.exp(acs_e[:, sl])
        yd = []
        low = lax.broadcasted_iota(jnp.int32, (c, LANES), 1) < SSM_HD
        for pair in range(hg // 2):
            xp = xdt_g[:, pair * LANES:(pair + 1) * LANES]
            acc = None
            for side, xh in enumerate((jnp.where(low, xp, 0.0), jnp.where(low, 0.0, xp))):
                h = g * hg + 2 * pair + side
                decay = jnp.exp(jnp.where(causal, a_cs[:, h:h + 1] - a_cs_t[h:h + 1, :], NEG))
                part = _nn(cb * decay, xh)
                acc = part if acc is None else acc + part
            yd.append(acc)
        ys.append(jnp.concatenate(yd, axis=1) + y_off)
    y = (jnp.concatenate(ys, axis=1) + xs * dsk_e) * _silu(z)
    yn = [y[:, g * GRW:(g + 1) * GRW] * lax.rsqrt(jnp.mean(jnp.square(y[:, g * GRW:(g + 1) * GRW]), axis=-1, keepdims=True) + EPS)
          for g in range(SSM_GROUPS)]
    return jnp.concatenate(yn, axis=1) * norm_g, jnp.concatenate(new_t, axis=1)


def ssd_fwd(xc, p, prm, name):
    bsz, s, _ = p.shape
    nc = s // CHUNK

    def body(xc_ref, dt_ref, *rest):
        z_refs, (db, al, dk, ng, o_ref, st_ref, state) = rest[:SSW // ZB], rest[SSW // ZB:]
        @pl.when(pl.program_id(0) == 0)
        def _():
            state[...] = jnp.zeros_like(state)

        for b in range(bsz):
            prev = state[b]
            st_ref[b, 0] = prev
            zb = jnp.concatenate([r[b] for r in z_refs], axis=1)
            out, new = _ssd_chunk(xc_ref[b], dt_ref[b], zb, prev, db[...], al[...], dk[...], ng[...])
            o_ref[b] = out.astype(o_ref.dtype)
            state[b] = new

    vec = pl.BlockSpec((1, LANES), lambda c: (0, 0))
    return pl.pallas_call(
        body, name=name, grid=(nc,),
        in_specs=[pl.BlockSpec((bsz, CHUNK, CCH), lambda c: (0, c, 0)),
                  pl.BlockSpec((bsz, CHUNK, LANES), lambda c: (0, c, OFF_DT // LANES)),
                  *[pl.BlockSpec((bsz, CHUNK, ZB), lambda c, i=i: (0, c, OFF_Z // ZB + i)) for i in range(SSW // ZB)],
                  vec, vec, vec, pl.BlockSpec((1, SSW), lambda c: (0, 0))],
        out_specs=[pl.BlockSpec((bsz, CHUNK, SSW), lambda c: (0, c, 0)),
                   pl.BlockSpec((bsz, 1, SSM_STATE, SSW), lambda c: (0, c, 0, 0))],
        out_shape=[jax.ShapeDtypeStruct((bsz, s, SSW), BF16), jax.ShapeDtypeStruct((bsz, nc, SSM_STATE, SSW), F32)],
        scratch_shapes=[pltpu.VMEM((bsz, SSM_STATE, SSW), F32)],
        compiler_params=_cp(("arbitrary",)),
    )(xc, p, *([p] * (SSW // ZB)), *prm)


def ssd_bwd(xc, p, states, dmix, prm, name):
    bsz, s, _ = p.shape
    nc = s // CHUNK

    def body(xc_ref, dt_ref, *rest):
        z_refs, (st_ref, do_ref, db, al, dk, ng, dxc_ref, ddt_ref, dz_ref) = rest[:SSW // ZB], rest[SSW // ZB:SSW // ZB + 9]
        rest = rest[SSW // ZB + 9:]
        dpar, dstate = rest[:4], rest[4]
        c = pl.program_id(0)

        @pl.when(c == 0)
        def _():
            dstate[...] = jnp.zeros_like(dstate)

        dpars = None
        for b in range(bsz):
            zb = jnp.concatenate([r[b] for r in z_refs], axis=1)
            _, vjp = jax.vjp(_ssd_chunk, xc_ref[b], dt_ref[b], zb, st_ref[b, 0], db[...], al[...], dk[...], ng[...])
            gr = vjp((do_ref[b], dstate[b]))
            dxc_ref[b] = gr[0]
            ddt_ref[b] = gr[1].astype(ddt_ref.dtype)
            dz_ref[b] = gr[2].astype(dz_ref.dtype)
            dstate[b] = gr[3]
            dpars = gr[4:] if dpars is None else [u + v for u, v in zip(dpars, gr[4:])]
        _accumulate(c == 0, dpar, dpars)

    rv = lambda c: nc - 1 - c
    vec = pl.BlockSpec((1, LANES), lambda c: (0, 0))
    ngs = pl.BlockSpec((1, SSW), lambda c: (0, 0))
    return pl.pallas_call(
        body, name=name, grid=(nc,),
        in_specs=[pl.BlockSpec((bsz, CHUNK, CCH), lambda c: (0, rv(c), 0)),
                  pl.BlockSpec((bsz, CHUNK, LANES), lambda c: (0, rv(c), OFF_DT // LANES)),
                  *[pl.BlockSpec((bsz, CHUNK, ZB), lambda c, i=i: (0, rv(c), OFF_Z // ZB + i)) for i in range(SSW // ZB)],
                  pl.BlockSpec((bsz, 1, SSM_STATE, SSW), lambda c: (0, rv(c), 0, 0)),
                  pl.BlockSpec((bsz, CHUNK, SSW), lambda c: (0, rv(c), (GMW + ATW) // SSW)),
                  vec, vec, vec, ngs],
        out_specs=[pl.BlockSpec((bsz, CHUNK, CCH), lambda c: (0, rv(c), 0)),
                   pl.BlockSpec((bsz, CHUNK, LANES), lambda c: (0, rv(c), 0)),
                   pl.BlockSpec((bsz, CHUNK, SSW), lambda c: (0, rv(c), 0)),
                   vec, vec, vec, ngs],
        out_shape=[jax.ShapeDtypeStruct((bsz, s, CCH), F32), jax.ShapeDtypeStruct((bsz, s, LANES), BF16),
                   jax.ShapeDtypeStruct((bsz, s, SSW), BF16)] + [jax.ShapeDtypeStruct((1, LANES), F32)] * 3
                  + [jax.ShapeDtypeStruct((1, SSW), F32)],
        scratch_shapes=[pltpu.VMEM((bsz, SSM_STATE, SSW), F32)],
        compiler_params=_cp(("arbitrary",)),
    )(xc, p, *([p] * (SSW // ZB)), states, dmix, *prm)


def _rows2d(a):
    return a.reshape(-1, a.shape[-1])


def _ew_tile(r, c):
    t = r
    while t * c > (1 << 20) and t % 16 == 0:
        t //= 2
    return t


def add_pair(g, theirs, chip, core, name):
    k, r, c = g.shape
    h = r // 2
    tr = _ew_tile(h, c)
    nb = h // tr

    def body(s_ref, a_ref, b_ref, own_ref, ob_ref):
        s = a_ref[...] + b_ref[...]
        ob_ref[...] = s.astype(ob_ref.dtype)

        @pl.when(pl.program_id(1) == s_ref[1])
        def _():
            own_ref[...] = s

    blk = pl.BlockSpec((None, tr, c), lambda i, kk, sr: (kk, i, 0))
    return pl.pallas_call(
        body, name=name,
        grid_spec=pltpu.PrefetchScalarGridSpec(
            num_scalar_prefetch=1, grid=(nb, k),
            in_specs=[pl.BlockSpec((None, tr, c), lambda i, kk, sr: (kk, sr[0] * nb + i, 0)), blk],
            out_specs=[pl.BlockSpec((tr, c), lambda i, kk, sr: (i, 0)), blk]),
        out_shape=[jax.ShapeDtypeStruct((h, c), F32), jax.ShapeDtypeStruct(theirs.shape, BF16)],
        compiler_params=_cp(("parallel", "arbitrary")),
    )(jnp.stack([core, chip]).astype(jnp.int32), g, theirs)


def sum_own_recv(own, recv, core, name):
    h, c = own.shape
    tr = _ew_tile(h, c)
    nb = h // tr

    def body(k_ref, o_ref, r_ref, out_ref):
        s = o_ref[...]
        for j in range(3):
            s = s + r_ref[j].astype(F32)
        out_ref[...] = s

    return pl.pallas_call(
        body, name=name,
        grid_spec=pltpu.PrefetchScalarGridSpec(
            num_scalar_prefetch=1, grid=(nb,),
            in_specs=[pl.BlockSpec((tr, c), lambda i, kr: (i, 0)),
                      pl.BlockSpec((3, tr, c), lambda i, kr: (0, i, 0))],
            out_specs=pl.BlockSpec((tr, c), lambda i, kr: (kr[0] * nb + i, 0))),
        out_shape=jax.ShapeDtypeStruct((2 * h, c), F32),
        compiler_params=_cp(("parallel",)),
    )(core.reshape(1).astype(jnp.int32), own, recv)


def _adam_math(w, m, v, g):
    mn = ADAM_B1 * m + (1.0 - ADAM_B1) * g
    vn = ADAM_B2 * v + (1.0 - ADAM_B2) * (g * g)
    mh = mn / (1.0 - ADAM_B1 ** ADAM_STEP)
    vh = vn / (1.0 - ADAM_B2 ** ADAM_STEP)
    return -ADAM_LR * (mh / (jnp.sqrt(vh) + ADAM_EPS) + ADAM_WD * w), mn, vn


def adamw_minor_rows(w, m, v, g, name):
    r, nl, c = w.shape
    tr = max(t for t in range(1, r + 1) if r % t == 0 and t * nl * c <= (1 << 19))

    def body(w_ref, m_ref, v_ref, g_ref, d_ref, mo_ref, vo_ref):
        d_ref[...], mo_ref[...], vo_ref[...] = _adam_math(w_ref[...], m_ref[...], v_ref[...], g_ref[...])

    blk = pl.BlockSpec((tr, nl, c), lambda i: (i, 0, 0))
    return pl.pallas_call(
        body, name=name, grid=(r // tr,), in_specs=[blk] * 4, out_specs=[blk] * 3,
        out_shape=[jax.ShapeDtypeStruct(w.shape, F32)] * 3, compiler_params=_cp(("parallel",)),
    )(w, m, v, g)


def adamw_layer(w, m, v, g, layer, prev, name):
    nl, r, c = w.shape
    tr = _ew_tile(r, c * 2)

    def body(w_ref, m_ref, v_ref, g_ref, *rest):
        go_ref, d_ref, mo_ref, vo_ref = rest[-4:]
        gv = g_ref[...]
        dl, mn, vn = _adam_math(w_ref[...], m_ref[...], v_ref[...], gv)
        go_ref[...] = gv
        d_ref[...] = dl
        mo_ref[...] = mn
        vo_ref[...] = vn

    lay = pl.BlockSpec((None, tr, c), lambda i: (layer, i, 0))
    n_prev = 0 if prev is None else 4
    return pl.pallas_call(
        body, name=name, grid=(r // tr,),
        in_specs=[lay, lay, lay, pl.BlockSpec((tr, c), lambda i: (i, 0))] + [_ANY] * n_prev,
        out_specs=[lay] * 4, out_shape=[jax.ShapeDtypeStruct(w.shape, F32)] * 4,
        input_output_aliases={4 + i: i for i in range(n_prev)},
        compiler_params=_cp(("parallel",)),
    )(w, m, v, g, *(prev or ()))


def sum_devices(parts, name):
    n, r, c = parts.shape
    tr = _ew_tile(r, c * n)

    def body(p_ref, o_ref):
        s = p_ref[0]
        for j in range(1, n):
            s = s + p_ref[j]
        o_ref[...] = s

    return pl.pallas_call(
        body, name=name, grid=(r // tr,),
        in_specs=[pl.BlockSpec((n, tr, c), lambda i: (0, i, 0))],
        out_specs=pl.BlockSpec((tr, c), lambda i: (i, 0)),
        out_shape=jax.ShapeDtypeStruct((r, c), F32),
        compiler_params=_cp(("parallel",)),
    )(parts)


def adamw(w, m, v, g, name):
    r, c = w.shape
    tr = _ew_tile(r, c * 2)

    def body(w_ref, m_ref, v_ref, g_ref, d_ref, mo_ref, vo_ref):
        gv = g_ref[...]
        mn = ADAM_B1 * m_ref[...] + (1.0 - ADAM_B1) * gv
        vn = ADAM_B2 * v_ref[...] + (1.0 - ADAM_B2) * (gv * gv)
        mh = mn / (1.0 - ADAM_B1 ** ADAM_STEP)
        vh = vn / (1.0 - ADAM_B2 ** ADAM_STEP)
        d_ref[...] = -ADAM_LR * (mh / (jnp.sqrt(vh) + ADAM_EPS) + ADAM_WD * w_ref[...])
        mo_ref[...] = mn
        vo_ref[...] = vn

    blk = pl.BlockSpec((tr, c), lambda i: (i, 0))
    return pl.pallas_call(
        body, name=name, grid=(r // tr,), in_specs=[blk] * 4, out_specs=[blk] * 3,
        out_shape=[jax.ShapeDtypeStruct((r, c), F32)] * 3,
        compiler_params=_cp(("parallel",)),
    )(w, m, v, g)


def _place():
    x, y, c = lax.axis_index("x"), lax.axis_index("y"), lax.axis_index("c")
    chips = [(1 - x, y), (x, 1 - y), (1 - x, 1 - y)]
    return x, y, c, chips


def all_gather_small(v, name):
    r, w = v.shape

    def body(x_ref, out_ref, send_sems, recv_sems, local_sem):
        x, y, c, chips = _place()
        me, sibling = (x, y, c), (x, y, 1 - c)

        def rows(px, py, pc):
            return out_ref.at[pl.ds((4 * px + 2 * py + pc) * r, r), :]

        def copy(k, block, to, src=None):
            return pltpu.make_async_remote_copy(
                src_ref=rows(*block) if src is None else src, dst_ref=rows(*block),
                send_sem=send_sems.at[k], recv_sem=recv_sems.at[k], device_id=to, device_id_type=MESH)

        mine = pltpu.make_async_copy(x_ref, rows(*me), local_sem)
        mine.start()
        first = [copy(0, me, sibling, src=x_ref)]
        first += [copy(1 + j, me, (*chip, c), src=x_ref) for j, chip in enumerate(chips)]
        for cp in first:
            cp.start()
        passed = [copy(4 + j, (*chip, c), sibling) for j, chip in enumerate(chips)]
        for j, chip in enumerate(chips):
            copy(1 + j, (*chip, c), me).wait_recv()
            passed[j].start()
        copy(0, sibling, me).wait_recv()
        for j, chip in enumerate(chips):
            copy(4 + j, (*chip, 1 - c), me).wait_recv()
        for cp in first + passed:
            cp.wait_send()
        mine.wait()

    out = pl.pallas_call(
        body, name=name, out_shape=jax.ShapeDtypeStruct((8 * r, w), v.dtype),
        in_specs=[pl.BlockSpec(memory_space=pltpu.VMEM)], out_specs=pl.BlockSpec(memory_space=pltpu.VMEM),
        scratch_shapes=[pltpu.SemaphoreType.DMA((7,)), pltpu.SemaphoreType.DMA((7,)), pltpu.SemaphoreType.DMA],
        compiler_params=pltpu.CompilerParams(vmem_limit_bytes=VMEM_LIMIT),
    )(v)
    return out.reshape(8, r, w)


_HBM = pl.BlockSpec(memory_space=pltpu.HBM)


_SEM = pl.BlockSpec(memory_space=pltpu.SEMAPHORE)
_ANY = pl.BlockSpec(memory_space=pl.ANY)
_EFFECT = pltpu.SideEffectType.DATAFLOW_SIDE_EFFECTING


def _hbm(a):
    return pltpu.with_memory_space_constraint(a, pltpu.HBM)


def split_copy_start(srcs, land_shapes, copies, after, name):
    n, nl = len(srcs), len(land_shapes)
    n_after = 0 if after is None else 1
    ncopy = [0]

    def body(*refs):
        ins, lands = refs[:n], refs[n:n + nl]
        send_sems, recv_sems = refs[n + nl + n_after], refs[n + nl + n_after + 1]
        token = refs[-1]
        x, y, c, chips = _place()
        for k, (src, dst, to) in enumerate(copies(x, y, c, chips, ins, lands)):
            pltpu.make_async_remote_copy(src_ref=src, dst_ref=dst, send_sem=send_sems.at[k], recv_sem=recv_sems.at[k],
                                         device_id=to, device_id_type=MESH).start()
        token[...] = jnp.zeros_like(token)

    ncopy[0] = len(copies(0, 0, 0, [(1, 0), (0, 1), (1, 1)], [None] * n, [None] * nl, count_only=True))
    k = ncopy[0]
    lands = [_hbm(lax.empty(s.shape, s.dtype)) for s in land_shapes]
    res = pl.pallas_call(
        body, name=name,
        out_shape=(pltpu.SemaphoreType.DMA((k,)), pltpu.SemaphoreType.DMA((k,)))
        + tuple(pltpu.HBM(s.shape, s.dtype) for s in srcs) + tuple(pltpu.HBM(s.shape, s.dtype) for s in land_shapes)
        + (jax.ShapeDtypeStruct((8, LANES), F32),),
        in_specs=[_HBM] * (n + nl) + [_ANY] * n_after,
        out_specs=(_SEM, _SEM) + (_HBM,) * (n + nl) + (pl.BlockSpec(memory_space=pltpu.VMEM),),
        input_output_aliases={i: 2 + i for i in range(n + nl)},
        compiler_params=pltpu.CompilerParams(has_side_effects=_EFFECT),
    )(*[_hbm(s) for s in srcs], *lands, *([after] if n_after else []))
    return res[0], res[1], list(res[2:2 + n]), list(res[2 + n:2 + n + nl]), res[-1]


def split_copy_wait(send_sems, recv_sems, srcs, lands, copies, after, name):
    n, nl = len(srcs), len(lands)

    def body(*refs):
        ins, lnd = refs[:n], refs[n:n + nl]
        ss, rs = refs[n + nl], refs[n + nl + 1]
        x, y, c, chips = _place()
        for k, (src, dst, to) in enumerate(copies(x, y, c, chips, ins, lnd, receive=True)):
            cp = pltpu.make_async_remote_copy(src_ref=src, dst_ref=dst, send_sem=ss.at[k], recv_sem=rs.at[k],
                                              device_id=to, device_id_type=MESH)
            cp.wait_send()
            cp.wait_recv()

    res = pl.pallas_call(
        body, name=name,
        out_shape=tuple(pltpu.HBM(s.shape, s.dtype) for s in srcs) + tuple(pltpu.HBM(s.shape, s.dtype) for s in lands),
        in_specs=[_HBM] * (n + nl) + [_SEM, _SEM, _ANY], out_specs=(_HBM,) * (n + nl),
        input_output_aliases={i: i for i in range(n + nl)},
        compiler_params=pltpu.CompilerParams(has_side_effects=_EFFECT),
    )(*srcs, *lands, send_sems, recv_sems, after)
    return list(res[:n]), list(res[n:])


def _gather_copies(x, y, c, chips, ins, lands, receive=False, count_only=False):
    out = []
    for i in range(len(ins)):
        for cx, cy in chips:
            if count_only:
                out.append(None)
                continue
            h = ins[i].shape[0] // 2
            rows = pl.ds(c * h, h)
            k_dst = (2 * cx + cy) if receive else (2 * x + y)
            out.append((ins[i].at[rows, :], lands[i].at[k_dst, rows, :], (cx, cy, c)))
    for i in range(len(ins)):
        out.append(None if count_only else (ins[i], lands[i].at[2 * x + y], (x, y, 1 - c)))
    return out


def _swap_copies(x, y, c, chips, ins, lands, receive=False, count_only=False):
    out = []
    for i in range(len(ins)):
        if count_only:
            out.append(None)
            continue
        h = ins[i].shape[1] // 2
        out.append((ins[i].at[:, pl.ds((1 - c) * h, h), :], lands[i], (x, y, 1 - c)))
    return out


def _scatter_copies(x, y, c, chips, ins, lands, receive=False, count_only=False):
    out = []
    for i in range(len(ins)):
        for j, (cx, cy) in enumerate(chips):
            if count_only:
                out.append(None)
                continue
            out.append((ins[i].at[2 * cx + cy], lands[i].at[j], (cx, cy, c)))
    return out


def forward_halves(lands, name):
    n = len(lands)

    def body(*refs):
        ins, outs = refs[:n], refs[n:2 * n]
        send_sems, recv_sems = refs[2 * n:]
        x, y, c, chips = _place()
        sibling = (x, y, 1 - c)
        sent = []
        for i in range(n):
            h = ins[i].shape[1] // 2
            for j, (cx, cy) in enumerate(chips):
                blk = ins[i].at[2 * cx + cy, pl.ds(c * h, h), :]
                sent.append(pltpu.make_async_remote_copy(
                    src_ref=blk, dst_ref=outs[i].at[2 * cx + cy, pl.ds(c * h, h), :], send_sem=send_sems.at[3 * i + j],
                    recv_sem=recv_sems.at[3 * i + j], device_id=sibling, device_id_type=MESH))
                sent[-1].start()
        for i in range(n):
            h = ins[i].shape[1] // 2
            for j, (cx, cy) in enumerate(chips):
                theirs = outs[i].at[2 * cx + cy, pl.ds((1 - c) * h, h), :]
                pltpu.make_async_remote_copy(
                    src_ref=theirs, dst_ref=theirs, send_sem=send_sems.at[3 * i + j], recv_sem=recv_sems.at[3 * i + j],
                    device_id=sibling, device_id_type=MESH).wait_recv()
        for cp in sent:
            cp.wait_send()

    return pl.pallas_call(
        body, name=name, out_shape=[jax.ShapeDtypeStruct(s.shape, s.dtype) for s in lands],
        in_specs=[_HBM] * n, out_specs=[_HBM] * n, input_output_aliases={i: i for i in range(n)},
        scratch_shapes=[pltpu.SemaphoreType.DMA((3 * n,)), pltpu.SemaphoreType.DMA((3 * n,))],
    )(*lands)


def join_halves(halves, name):
    n = len(halves)

    def body(*refs):
        ins, outs = refs[:n], refs[n:2 * n]
        send_sems, recv_sems = refs[2 * n:]
        x, y, c, _ = _place()
        sibling = (x, y, 1 - c)
        sent = []
        for i in range(n):
            h = ins[i].shape[0] // 2
            sent.append(pltpu.make_async_remote_copy(
                src_ref=ins[i].at[pl.ds(c * h, h), :], dst_ref=outs[i].at[pl.ds(c * h, h), :], send_sem=send_sems.at[i],
                recv_sem=recv_sems.at[i], device_id=sibling, device_id_type=MESH))
            sent[-1].start()
        for i in range(n):
            h = ins[i].shape[0] // 2
            theirs = outs[i].at[pl.ds((1 - c) * h, h), :]
            pltpu.make_async_remote_copy(
                src_ref=theirs, dst_ref=theirs, send_sem=send_sems.at[i],
                recv_sem=recv_sems.at[i], device_id=sibling, device_id_type=MESH).wait_recv()
        for cp in sent:
            cp.wait_send()

    return pl.pallas_call(
        body, name=name, out_shape=[jax.ShapeDtypeStruct(s.shape, F32) for s in halves],
        in_specs=[_HBM] * n, out_specs=[_HBM] * n, input_output_aliases={i: i for i in range(n)},
        scratch_shapes=[pltpu.SemaphoreType.DMA((n,)), pltpu.SemaphoreType.DMA((n,))],
    )(*halves)


_PACK_ROWS = 8 * LANES


def _pack(arrs):
    flat = jnp.concatenate([a.reshape(-1).astype(F32) for a in arrs])
    pad = (-flat.shape[0]) % _PACK_ROWS
    return jnp.pad(flat, (0, pad)).reshape(-1, LANES)


def _unpack(flat, shapes):
    flat = flat.reshape(-1)
    out, off = [], 0
    for s in shapes:
        n = int(np.prod(s))
        out.append(flat[off:off + n].reshape(s))
        off += n
    return out


def _win_from_blocks(g):
    d = g.shape[1]
    return jnp.pad(g.transpose(1, 0, 2).reshape(d, IN_W), ((0, 0), (0, PW - IN_W)))


def _win_to_blocks(w):
    return w[:, :IN_W].reshape(w.shape[0], 4, IN_W // 4).transpose(1, 0, 2)


def _relu2(a):
    r = jnp.maximum(a, 0)
    return r * r


def kernel(x, c, ada_w, ada_b, norm1_g, w_in, gm_ln_g, gm_ln_b, gm_ws, gm_bs, gm_norm_g, attn_sinks, attn_norm_g, conv_w, conv_b, dt_bias, a_log, d_skip, ssm_norm_g, w_out, norm2_g, w_mlp1, w_mlp2, final_norm_g, loss_target, m_ada_w, m_ada_b, m_norm1_g, m_w_in, m_gm_ln_g, m_gm_ln_b, m_gm_ws, m_gm_bs, m_gm_norm_g, m_attn_sinks, m_attn_norm_g, m_conv_w, m_conv_b, m_dt_bias, m_a_log, m_d_skip, m_ssm_norm_g, m_w_out, m_norm2_g, m_w_mlp1, m_w_mlp2, m_final_norm_g, v_ada_w, v_ada_b, v_norm1_g, v_w_in, v_gm_ln_g, v_gm_ln_b, v_gm_ws, v_gm_bs, v_gm_norm_g, v_attn_sinks, v_attn_norm_g, v_conv_w, v_conv_b, v_dt_bias, v_a_log, v_d_skip, v_ssm_norm_g, v_w_out, v_norm2_g, v_w_mlp1, v_w_mlp2, v_final_norm_g):
    nl = ada_w.shape[0]
    bl, s, d = x.shape
    t = bl * s
    dff4 = w_mlp1.shape[2]
    dff = 4 * dff4
    mod_w = ada_w.shape[2]
    cw_w = conv_w.shape[2]
    xi, yi, ci = lax.axis_index("x"), lax.axis_index("y"), lax.axis_index("c")
    chip = 2 * xi + yi
    dev = 2 * chip + ci
    nex = 8 * bl

    shards = [[w_in[l].astype(BF16), w_out[l].astype(BF16), w_mlp1[l].astype(BF16), w_mlp2[l].astype(BF16)]
              for l in range(nl)]
    groups = [[shards[0][i]] for i in range(4)] + [shards[l] for l in range(1, nl)]

    def start_gather(gi, behind):
        ss, rs, srcs, lands, token = split_copy_start(
            groups[gi], [jax.ShapeDtypeStruct((4,) + a.shape, a.dtype) for a in groups[gi]], _gather_copies, behind,
            f"gather_start_{gi}")
        return (ss, rs, srcs, lands), token

    first_gather, first_token = start_gather(0, None)
    g0 = all_gather_small(_pack([c, conv_w]) + first_token[0, 0], "ag_c")
    g0 = g0.reshape(8, -1)
    c_all = g0[:, :bl * d].reshape(nex, d)
    cw_parts = g0[0::2, bl * d:bl * d + conv_w.size].reshape(4, nl, CONV_K, cw_w)
    conv_w_full = cw_parts.transpose(1, 2, 0, 3).reshape(nl, CONV_K, CCH)

    def c_act(a):
        return _silu(a).astype(BF16)

    def to_bf16(a):
        return a.astype(BF16)

    mod_parts = []
    for l in range(nl):
        bias = lax.dynamic_slice(ada_b[l].reshape(1, -1), (0, chip * mod_w), (1, mod_w))
        mod_parts.append(_mm("nn", c_all, ada_w, dims=(nex, mod_w, d), tm=nex, tn=512, tk=d, out_dtypes=[F32],
                             name=f"mod_{l}", pro_a=c_act, pro_b=to_bf16,
                             b_spec=pl.BlockSpec((None, d, 512), lambda i, j, kk, l=l: (l, kk, j)),
                             extras=[(bias, pl.BlockSpec((1, 512), lambda i, j, kk: (0, j)))],
                             epi=lambda acc, bv: (acc + bv,))[0])
    g1 = all_gather_small(_pack(mod_parts), "ag_mod").reshape(8, -1)
    mod_all = g1[0::2, :nl * nex * mod_w].reshape(4, nl, nex, mod_w).transpose(1, 2, 0, 3).reshape(nl, nex, 4 * mod_w)
    mod = lax.dynamic_slice(mod_all, (0, dev * bl, 0), (nl, bl, 4 * mod_w))
    mods = [[mod[l, :, i * d:(i + 1) * d].reshape(bl, 1, d) for i in range(6)] for l in range(nl)]

    pending, after = [first_gather], g1
    for gi in range(1, len(groups)):
        state, after = start_gather(gi, after)
        pending.append(state)
    mods[0][0] = mods[0][0] + after[0, 0]

    def fetch(gi, behind):
        ss, rs, srcs, lands = pending[gi]
        srcs, lands = split_copy_wait(ss, rs, srcs, lands, _gather_copies, behind, f"gather_wait_{gi}")
        return forward_halves(lands, f"gather_pass_{gi}")

    as_win = _win_from_blocks

    wfull = [None] * nl
    row = lambda a: a.reshape(1, -1)
    pad16 = lambda a: jnp.pad(a.reshape(1, -1), ((0, 0), (0, LANES - SSM_HEADS)))
    tm_res = min(1024, s)

    def residual(acc, xt, gt):
        return acc, xt + gt * acc

    def res_extras(xin, gate, tm=tm_res, tn=512):
        return [(xin.reshape(t, d), pl.BlockSpec((tm, tn), lambda i, j, kk: (i, j))),
                (gate, pl.BlockSpec((None, 1, tn), lambda i, j, kk: (i * tm // s, 0, j)))]

    w1_blk = lambda tk, tn: pl.BlockSpec((None, tk, tn), lambda i, j, kk: (j // (dff4 // tn), kk, j % (dff4 // tn)))

    saved = []
    xcur = x
    for l in range(nl):
        sh1, sc1, gt1, sh2, sc2, gt2 = mods[l]
        if l == 0:
            win = as_win(fetch(0, mod)[0])
        else:
            g_in, g_out, w1, g_2 = fetch(3 + l, xcur)
            win, wout, w2 = as_win(g_in), g_out.reshape(-1, d), g_2.reshape(dff, d)
        prm_a = (row(gm_ln_g[l]), row(gm_ln_b[l]), gm_ws[l], gm_bs[l].T, row(gm_norm_g[l]))
        prm_b = (row(attn_sinks[l]), row(attn_norm_g[l]))
        prm_c = (pad16(dt_bias[l]), pad16(a_log[l]), pad16(d_skip[l]), row(ssm_norm_g[l]))
        h1 = ln_mod_fwd(xcur, row(norm1_g[l]), sc1, sh1, f"ln1_fwd_{l}")
        p = _mm("nn", h1.reshape(t, d), win, dims=(t, PW, d), tm=1024, tn=PW // 3, tk=d, out_dtypes=[F32],
                name=f"proj_in_{l}")[0].reshape(bl, s, PW)
        out_a = gmlp_fwd(p, prm_a, f"gmlp_fwd_{l}")
        out_b = attn_fwd(p, *prm_b, f"attn_fwd_{l}")
        xc = conv_fwd(p, conv_w_full[l], row(conv_b[l]), f"conv_fwd_{l}")
        out_c, states = ssd_fwd(xc, p, prm_c, f"ssd_fwd_{l}")
        mix = jnp.concatenate([out_a, out_b, out_c], axis=-1)
        if l == 0:
            wout = fetch(1, mix)[0].reshape(-1, d)
        mm1, x2 = _mm("nn", mix.reshape(t, d), wout, dims=(t, d, d), tm=tm_res, tn=1024, tk=d, out_dtypes=[F32, F32],
                      name=f"proj_out_{l}", extras=res_extras(xcur, gt1, tm_res, 1024), epi=residual)
        x2 = x2.reshape(bl, s, d)
        h2 = ln_mod_fwd(x2, row(norm2_g[l]), sc2, sh2, f"ln2_fwd_{l}")
        if l == 0:
            w1 = fetch(2, h2)[0]
        a1 = _mm("nn", h2.reshape(t, d), w1, dims=(t, dff, d), tm=1024, tn=2048, tk=d, out_dtypes=[BF16],
                 name=f"mlp1_{l}", b_spec=w1_blk(d, 2048))[0]
        if l == 0:
            w2 = fetch(3, a1)[0].reshape(dff, d)
        tm2 = min(512, s)
        mm2, x3 = _mm("nn", a1, w2, dims=(t, d, dff), tm=tm2, tn=512, tk=dff, out_dtypes=[F32, F32],
                      name=f"mlp2_{l}", extras=res_extras(x2, gt2, tm2), epi=residual, pro_a=_relu2)
        x3 = x3.reshape(bl, s, d)
        wfull[l] = (win, wout, w1, w2)
        saved.append((xcur, h1, p, xc, states, mix, mm1.reshape(bl, s, d), x2, h2, a1, mm2.reshape(bl, s, d),
                      prm_a, prm_b, prm_c))
        xcur = x3

    dx, d_final_g, loss_part = loss_head(xcur, row(final_norm_g), loss_target, "loss_head")
    loss = lax.psum(loss_part[0, 0], ("x", "y", "c"))

    def rs_swap(grads, tag):
        ss, rs, srcs, lands, token = split_copy_start(
            grads, [jax.ShapeDtypeStruct((4, g.shape[1] // 2, g.shape[2]), F32) for g in grads],
            _swap_copies, None, f"rs_swap_{tag}")
        return (ss, rs, srcs, lands), token

    def rs_begin(swap_state, tag, swapped_behind, start_behind=None):
        ss, rs, srcs, lands = swap_state
        grads, theirs = split_copy_wait(ss, rs, srcs, lands, _swap_copies, swapped_behind, f"rs_swapped_{tag}")
        sums = [add_pair(g, th, chip, ci, f"rs_add_{tag}_{i}") for i, (g, th) in enumerate(zip(grads, theirs))]
        ss, rs, srcs, lands, token = split_copy_start(
            [sm[1] for sm in sums], [jax.ShapeDtypeStruct((3,) + sm[1].shape[1:], BF16) for sm in sums],
            _scatter_copies, sums[0][0] if start_behind is None else start_behind, f"rs_start_{tag}")
        return (ss, rs, srcs, lands, [sm[0] for sm in sums]), token

    def rs_end(state, behind, tag):
        ss, rs, srcs, lands, sums_f32 = state
        _, got = split_copy_wait(ss, rs, srcs, lands, _scatter_copies, behind, f"rs_wait_{tag}")
        halves = [sum_own_recv(sf, g, ci, f"rs_sum_{tag}_{i}") for i, (sf, g) in enumerate(zip(sums_f32, got))]
        return join_halves(halves, f"rs_join_{tag}")

    small_parts = [None] * nl
    dmods = [None] * nl
    reduced = [[None] * 4 for _ in range(nl)]
    pending_rs, rs_token = [], None
    part_slots = {"a": (0, 1), "m": (2, 3)}

    def finish(behind):
        for ll, part, state in pending_rs:
            for slot, blk in zip(part_slots[part], rs_end(state, behind, f"{ll}{part}")):
                reduced[ll][slot] = blk
        pending_rs.clear()

    for l in reversed(range(nl)):
        sh1, sc1, gt1, sh2, sc2, gt2 = mods[l]
        win, wout, w1, w2 = wfull[l]
        xin, h1, p, xc, states, mix, mm1, x2, h2, a1, mm2, prm_a, prm_b, prm_c = saved[l]
        if rs_token is not None:
            gt2 = gt2 + rs_token[0, 0]
        dm2, dgt2 = gate_bwd(dx, mm2, gt2, f"gate2_bwd_{l}")
        dm2 = dm2.reshape(t, d)
        da1 = _mm("nt", dm2, w2, dims=(t, dff, d), tm=1024, tn=2048, tk=d, out_dtypes=[BF16], name=f"mlp2_dx_{l}",
                  extras=[(a1, pl.BlockSpec((1024 if t >= 1024 else t, 2048), lambda i, j, kk: (i, j)))],
                  epi=lambda acc, av: (acc * (2.0 * jnp.maximum(av, 0).astype(F32)),))[0]
        dw2 = _mm("tn", a1, dm2, dims=(dff, d, t), tm=512, tn=d, tk=2048, out_dtypes=[F32], name=f"mlp2_dw_{l}",
                  pro_a=_relu2, out_shapes=[(4, dff4, d)],
                  out_specs=[pl.BlockSpec((None, 512, d), lambda i, j, kk: (i // (dff4 // 512), i % (dff4 // 512), 0))])[0]
        dw1 = _mm("tn", h2.reshape(t, d), da1, dims=(d, dff, t), tm=512, tn=dff4, tk=2048, out_dtypes=[F32],
                  name=f"mlp1_dw_{l}", out_shapes=[(4, d, dff4)],
                  out_specs=[pl.BlockSpec((None, 512, dff4), lambda i, j, kk: (j, i, 0))])[0]
        swap_state, swap_token = rs_swap([dw1, dw2], f"{l}m")
        dh2 = _mm_nt_blocked(da1, w1, tm=512, tn=512, name=f"mlp1_dx_{l}", behind=swap_token)
        mlp_state, mlp_token = rs_begin(swap_state, f"{l}m", dh2)
        sc2 = sc2 + mlp_token[0, 0]
        dx2, dsc2, dsh2, dn2 = ln_mod_bwd(dh2.reshape(bl, s, d), x2, dx, row(norm2_g[l]), sc2, f"ln2_bwd_{l}")
        dm1, dgt1 = gate_bwd(dx2, mm1, gt1, f"gate1_bwd_{l}")
        dm1 = dm1.reshape(t, d)
        dmix = _mm("nt", dm1, wout, dims=(t, d, d), tm=1024, tn=2048, tk=d, out_dtypes=[F32],
                   name=f"proj_out_dx_{l}")[0].reshape(bl, s, d)
        dwout = _mm("tn", mix.reshape(t, d), dm1, dims=(d, d, t), tm=512, tn=d, tk=2048, out_dtypes=[F32],
                    name=f"proj_out_dw_{l}", out_shapes=[(4, d // 4, d)],
                    out_specs=[pl.BlockSpec((None, 512, d), lambda i, j, kk: (i // (d // 4 // 512), i % (d // 4 // 512), 0))])[0]
        du, dv, dlg, dlb, dws, dbst, dgng = gmlp_bwd(p, dmix, prm_a, f"gmlp_bwd_{l}")
        dq, dk, dvv, dsinks, dang = attn_bwd(p, dmix, *prm_b, f"attn_bwd_{l}")
        dxc, ddt, dz, ddtb, dalog, ddsk, dsng = ssd_bwd(xc, p, states, dmix, prm_c, f"ssd_bwd_{l}")
        dxbc, dcw, dcb = conv_bwd(p, dxc, conv_w_full[l], row(conv_b[l]), f"conv_bwd_{l}")
        dp = jnp.concatenate([du, dv, dq, dk, dvv, dz, dxbc, ddt, jnp.zeros((bl, s, PW - OFF_DT - LANES), BF16)],
                             axis=-1).reshape(t, PW)
        dwin = _mm("tn", h1.reshape(t, d), dp, dims=(d, PW, t), tm=512, tn=PW // 2, tk=2048, out_dtypes=[F32],
                   name=f"proj_in_dw_{l}")[0]
        dwin_blocks = _win_to_blocks(dwin)
        mixer_swap, swap_token = rs_swap([dwin_blocks, dwout], f"{l}a")
        dh1 = _mm("nt", dp, win, dims=(t, d, PW), tm=1024, tn=512, tk=PW, out_dtypes=[F32],
                  name=f"proj_in_dx_{l}", behind=swap_token)[0]
        dx, dsc1, dsh1, dn1 = ln_mod_bwd(dh1.reshape(bl, s, d), xin, dx2, row(norm1_g[l]), sc1, f"ln1_bwd_{l}")
        dmods[l] = jnp.concatenate([dsh1, dsc1, dgt1, dsh2, dsc2, dgt2], axis=-1).reshape(bl, 6 * d)
        small_parts[l] = [dn1, dlg, dlb, dws, dbst.T, dgng, dsinks, dang, dcw, dcb, ddtb[:, :SSM_HEADS],
                          dalog[:, :SSM_HEADS], ddsk[:, :SSM_HEADS], dsng, dn2]
        finish(dx)
        pending_rs.append((l, "m", mlp_state))
        if l > 0:
            state, rs_token = rs_begin(mixer_swap, f"{l}a", dx)
            pending_rs.append((l, "a", state))
    grad_x = dx

    big = [(w_in, m_w_in, v_w_in), (w_out, m_w_out, v_w_out), (w_mlp1, m_w_mlp1, v_w_mlp1), (w_mlp2, m_w_mlp2, v_w_mlp2)]
    big_out = [None] * 4
    for l in reversed(range(1, nl)):
        for i, (wt, mt, vt) in enumerate(big):
            if i > 0:
                big_out[i] = adamw_layer(wt, mt, vt, reduced[l][i], l, big_out[i], f"adamw_big_{i}_{l}")

    small_names = [norm1_g, gm_ln_g, gm_ln_b, gm_ws, gm_bs, gm_norm_g, attn_sinks, attn_norm_g, None, conv_b, dt_bias,
                   a_log, d_skip, ssm_norm_g, norm2_g]
    n_small = len(small_names)
    per_param = [jnp.stack([small_parts[l][i].reshape(-1) for l in range(nl)]) for i in range(n_small)]
    small_vec = _pack(per_param + [d_final_g])
    rs_small = small_vec.shape[0]
    dmod_local = jnp.stack(dmods, axis=1)
    g2 = all_gather_small(jnp.concatenate([small_vec, _pack([dmod_local])], axis=0), "ag_small")
    state, rs_token = rs_begin(mixer_swap, "0a", grad_x, start_behind=g2)
    pending_rs.append((0, "a", state))
    g2 = g2 + rs_token[0, 0]
    g_small = sum_devices(g2[:, :rs_small, :], "sum_small")
    dmod_all = g2[:, rs_small:, :].reshape(8, -1)[:, :bl * nl * 6 * d].reshape(nex, nl * 6 * d)
    g_ada_b = sum_devices(dmod_all.reshape(nex, -1, LANES), "sum_ada_b").reshape(nl, 6 * d)
    shapes = [(nl, int(np.prod(small_parts[0][i].shape))) for i in range(n_small)] + [(d,)]
    g_list = _unpack(g_small, shapes)
    g_conv_w = lax.dynamic_slice(g_list[8].reshape(nl, CONV_K, CCH), (0, 0, chip * cw_w), (nl, CONV_K, cw_w))

    dm_cols = lax.dynamic_slice(dmod_all.reshape(nex, nl, 6 * d), (0, 0, chip * mod_w), (nex, nl, mod_w))
    g_ada_w = _mm("tn", c_all, dm_cols.reshape(nex, nl * mod_w), dims=(d, nl * mod_w, nex), tm=512, tn=512, tk=nex,
                  out_dtypes=[F32], name="ada_w_grad", pro_a=c_act, pro_b=to_bf16, out_shapes=[(nl, d, mod_w)],
                  out_specs=[pl.BlockSpec((None, 512, 512), lambda i, j, kk: (j // (mod_w // 512), i, j % (mod_w // 512)))])[0]
    d_ada_w, m_ada_w_n, v_ada_w_n = [a.reshape(ada_w.shape) for a in
                                     adamw(_rows2d(ada_w), _rows2d(m_ada_w), _rows2d(v_ada_w), _rows2d(g_ada_w), "adamw_ada_w")]

    smalls = {
        "ada_b": (ada_b, m_ada_b, v_ada_b, g_ada_b), "norm1_g": (norm1_g, m_norm1_g, v_norm1_g, g_list[0]),
        "gm_ln_g": (gm_ln_g, m_gm_ln_g, v_gm_ln_g, g_list[1]), "gm_ln_b": (gm_ln_b, m_gm_ln_b, v_gm_ln_b, g_list[2]),
        "gm_ws": (gm_ws, m_gm_ws, v_gm_ws, g_list[3]), "gm_bs": (gm_bs, m_gm_bs, v_gm_bs, g_list[4]),
        "gm_norm_g": (gm_norm_g, m_gm_norm_g, v_gm_norm_g, g_list[5]),
        "attn_sinks": (attn_sinks, m_attn_sinks, v_attn_sinks, g_list[6]),
        "attn_norm_g": (attn_norm_g, m_attn_norm_g, v_attn_norm_g, g_list[7]),
        "conv_w": (conv_w, m_conv_w, v_conv_w, g_conv_w), "conv_b": (conv_b, m_conv_b, v_conv_b, g_list[9]),
        "dt_bias": (dt_bias, m_dt_bias, v_dt_bias, g_list[10]), "a_log": (a_log, m_a_log, v_a_log, g_list[11]),
        "d_skip": (d_skip, m_d_skip, v_d_skip, g_list[12]),
        "ssm_norm_g": (ssm_norm_g, m_ssm_norm_g, v_ssm_norm_g, g_list[13]),
        "norm2_g": (norm2_g, m_norm2_g, v_norm2_g, g_list[14]),
        "final_norm_g": (final_norm_g, m_final_norm_g, v_final_norm_g, g_list[15]),
    }
    keys = list(smalls)
    wv, mv, vv_, gv = [_pack([smalls[k][i].reshape(smalls[k][0].shape) for k in keys]) for i in range(4)]
    sd_, sm_, sv_ = adamw(wv, mv, vv_, gv, "adamw_small")
    shp = [smalls[k][0].shape for k in keys]
    small_out = {k: (smalls[k][3].reshape(smalls[k][0].shape), a, b, cc)
                 for k, a, b, cc in zip(keys, _unpack(sd_, shp), _unpack(sm_, shp), _unpack(sv_, shp))}

    late = jnp.zeros((8, LANES), F32) + (sv_[0, 0] + v_ada_w_n[0, 0, 0])
    for bo in big_out:
        if bo is not None:
            late = late + bo[3][nl - 1, 0, 0]
    finish(late)
    for i, (wt, mt, vt) in enumerate(big):
        if i > 0:
            big_out[i] = adamw_layer(wt, mt, vt, reduced[0][i], 0, big_out[i], f"adamw_big_{i}_0")
    minor_first = lambda a: jnp.transpose(a, (2, 0, 1))
    g_in = jnp.stack([reduced[l][0].T for l in range(nl)], axis=1)
    back = lambda a: jnp.transpose(a, (1, 2, 0))
    big_out[0] = [back(a) for a in [g_in, *adamw_minor_rows(minor_first(w_in), minor_first(m_w_in),
                                                            minor_first(v_w_in), g_in, "adamw_w_in")]]

    out = {"ada_w": (g_ada_w, d_ada_w, m_ada_w_n, v_ada_w_n), "w_in": big_out[0], "w_out": big_out[1],
           "w_mlp1": big_out[2], "w_mlp2": big_out[3], **small_out}
    order = ["ada_w", "ada_b", "norm1_g", "w_in", "gm_ln_g", "gm_ln_b", "gm_ws", "gm_bs", "gm_norm_g", "attn_sinks",
             "attn_norm_g", "conv_w", "conv_b", "dt_bias", "a_log", "d_skip", "ssm_norm_g", "w_out", "norm2_g",
             "w_mlp1", "w_mlp2", "final_norm_g"]
    return (loss, grad_x, *[out[k][0] for k in order], *[out[k][1] for k in order],
            *[out[k][2] for k in order], *[out[k][3] for k in order])
```

```python
import functools
import math

import jax
import jax.numpy as jnp
import numpy as np
from jax import lax
from jax.experimental import pallas as pl
from jax.experimental.pallas import tpu as pltpu

F32 = jnp.float32
BF16 = jnp.bfloat16
HI = lax.Precision.HIGHEST
MESH = pl.DeviceIdType.MESH

CHUNK = 128
GM_HEADS, GM_HD = 4, 128
ATT_HEADS, ATT_KV, ATT_HD = 8, 2, 64
WINDOW = 128
SSM_HEADS, SSM_HD, SSM_GROUPS, SSM_STATE, CONV_K = 16, 64, 2, 128, 4
EPS = 1e-6
LN_EPS = 1e-5
NEG = -1e30
LANES = 128

GMW = GM_HEADS * GM_HD
ATW = ATT_HEADS * ATT_HD
KVW = ATT_KV * ATT_HD
SSW = SSM_HEADS * SSM_HD
BCW = SSM_GROUPS * SSM_STATE
CCH = SSW + 2 * BCW
GRW = SSW // SSM_GROUPS
IN_SIZES = (GMW, GMW, ATW, KVW, KVW, SSW, CCH, SSM_HEADS)
IN_W = sum(IN_SIZES)
OFF_U, OFF_V, OFF_Q, OFF_K, OFF_VV, OFF_Z, OFF_XBC, OFF_DT = 0, 512, 1024, 1536, 1664, 1792, 2816, 4352
ZB = 256
PW = 4608

ADAM_LR, ADAM_B1, ADAM_B2, ADAM_EPS, ADAM_WD, ADAM_STEP = 0.001, 0.9, 0.999, 1e-08, 0.01, 10

VMEM_LIMIT = 56 * 1024 * 1024


def _cp(sem=None):
    return pltpu.CompilerParams(dimension_semantics=sem, vmem_limit_bytes=VMEM_LIMIT)


_DN = {"nn": (((1,), (0,)), ((), ())), "nt": (((1,), (1,)), ((), ())), "tn": (((0,), (0,)), ((), ()))}


def _dot(form, a, b):
    return lax.dot_general(a.astype(BF16), b.astype(BF16), _DN[form], preferred_element_type=F32)


@jax.custom_vjp
def _nn(a, b):
    return _dot("nn", a, b)


@jax.custom_vjp
def _nt(a, b):
    return _dot("nt", a, b)


@jax.custom_vjp
def _tn(a, b):
    return _dot("tn", a, b)


_nn.defvjp(lambda a, b: (_dot("nn", a, b), (a, b)), lambda r, g: (_dot("nt", g, r[1]), _dot("tn", r[0], g)))
_nt.defvjp(lambda a, b: (_dot("nt", a, b), (a, b)), lambda r, g: (_dot("nn", g, r[1]), _dot("tn", g, r[0])))
_tn.defvjp(lambda a, b: (_dot("tn", a, b), (a, b)), lambda r, g: (_dot("nt", r[1], g), _dot("nn", r[0], g)))


def _hdot(a, b):
    return jnp.dot(a, b, precision=HI, preferred_element_type=F32)


def _silu(x):
    return x * (1.0 / (1.0 + jnp.exp(-x)))


def _softplus(x):
    return jnp.maximum(x, 0.0) + jnp.log1p(jnp.exp(-jnp.abs(x)))


def _gelu(x):
    return 0.5 * x * (1.0 + jnp.tanh(math.sqrt(2.0 / math.pi) * (x + 0.044715 * (x * x * x))))


def _rms(y, g):
    return y * lax.rsqrt(jnp.mean(y * y, axis=-1, keepdims=True) + EPS) * g


def _mm(form, a, b, *, dims, tm, tn, tk, out_dtypes, name, a_spec=None, b_spec=None, out_specs=None,
        out_shapes=None, extras=(), epi=None, pro_a=None, pro_b=None, behind=None):
    m, n, k = dims
    tm, tn, tk = min(tm, m), min(tn, n), min(tk, k)
    assert m % tm == 0 and n % tn == 0 and k % tk == 0, (name, dims, tm, tn, tk)
    nk = k // tk
    if a_spec is None:
        a_spec = (pl.BlockSpec((tk, tm), lambda i, j, kk: (kk, i)) if form == "tn"
                  else pl.BlockSpec((tm, tk), lambda i, j, kk: (i, kk)))
    if b_spec is None:
        b_spec = (pl.BlockSpec((tn, tk), lambda i, j, kk: (j, kk)) if form == "nt"
                  else pl.BlockSpec((tk, tn), lambda i, j, kk: (kk, j)))
    n_out = len(out_dtypes)
    if out_specs is None:
        out_specs = [pl.BlockSpec((tm, tn), lambda i, j, kk: (i, j))] * n_out
    if out_shapes is None:
        out_shapes = [(m, n)] * n_out
    ne = len(extras)
    n_behind = 0 if behind is None else 1

    def body(*refs):
        a_ref, b_ref = refs[0], refs[1]
        ex = refs[2:2 + ne]
        outs = refs[2 + ne + n_behind:2 + ne + n_behind + n_out]

        def write(val):
            res = epi(val, *[e[...] for e in ex]) if epi is not None else (val,)
            for o, r in zip(outs, res):
                o[...] = r.astype(o.dtype)

        av = a_ref[...]
        if pro_a is not None:
            av = pro_a(av)
        bv = b_ref[...]
        if pro_b is not None:
            bv = pro_b(bv)
        part = lax.dot_general(av, bv, _DN[form], preferred_element_type=F32)
        if nk == 1:
            write(part)
        else:
            acc = refs[-1]
            kk = pl.program_id(2)

            @pl.when(kk == 0)
            def _():
                acc[...] = part

            @pl.when(kk > 0)
            def _():
                acc[...] += part

            @pl.when(kk == nk - 1)
            def _():
                write(acc[...])

    res = pl.pallas_call(
        body, name=name, grid=(m // tm, n // tn, nk),
        in_specs=[a_spec, b_spec] + [s for _, s in extras] + [_ANY] * n_behind,
        out_specs=out_specs,
        out_shape=[jax.ShapeDtypeStruct(s, d) for s, d in zip(out_shapes, out_dtypes)],
        scratch_shapes=[pltpu.VMEM((tm, tn), F32)] if nk > 1 else [],
        compiler_params=_cp(("parallel", "parallel", "arbitrary")),
    )(a, b, *[e for e, _ in extras], *([behind] if n_behind else []))
    return res


def _mm_nt_blocked(a, b, *, tm, tn, name, behind=None):
    m = a.shape[0]
    nparts, n, f = b.shape
    tm, tn = min(tm, m), min(tn, n)
    n_behind = 0 if behind is None else 1

    def body(a_ref, *rest):
        b_refs, o_ref = rest[:nparts], rest[nparts + n_behind]
        acc = None
        for k in range(nparts):
            part = lax.dot_general(a_ref[:, k * f:(k + 1) * f], b_refs[k][...], _DN["nt"], preferred_element_type=F32)
            acc = part if acc is None else acc + part
        o_ref[...] = acc

    return pl.pallas_call(
        body, name=name, grid=(m // tm, n // tn),
        in_specs=[pl.BlockSpec((tm, nparts * f), lambda i, j: (i, 0))]
        + [pl.BlockSpec((None, tn, f), lambda i, j, k=k: (k, j, 0)) for k in range(nparts)] + [_ANY] * n_behind,
        out_specs=pl.BlockSpec((tm, tn), lambda i, j: (i, j)),
        out_shape=jax.ShapeDtypeStruct((m, n), F32),
        compiler_params=_cp(("parallel", "parallel")),
    )(a, *([b] * nparts), *([behind] if n_behind else []))


def _row_tile(s):
    return min(512, s)


def ln_mod_fwd(x, g, sc, sh, name):
    bsz, s, d = x.shape
    ts = _row_tile(s)

    def body(x_ref, g_ref, sc_ref, sh_ref, o_ref):
        xv = x_ref[...]
        r = lax.rsqrt(jnp.mean(xv * xv, axis=-1, keepdims=True) + EPS)
        o_ref[...] = ((xv * r * g_ref[...]) * (1.0 + sc_ref[...]) + sh_ref[...]).astype(o_ref.dtype)

    row = pl.BlockSpec((None, ts, d), lambda b, i: (b, i, 0))
    vec = pl.BlockSpec((None, 1, d), lambda b, i: (b, 0, 0))
    return pl.pallas_call(
        body, name=name, grid=(bsz, s // ts),
        in_specs=[row, pl.BlockSpec((1, d), lambda b, i: (0, 0)), vec, vec],
        out_specs=row, out_shape=jax.ShapeDtypeStruct(x.shape, BF16),
        compiler_params=_cp(("parallel", "parallel")),
    )(x, g, sc, sh)


def ln_mod_bwd(dh, x, dres, g, sc, name):
    bsz, s, d = x.shape
    ts = _row_tile(s)

    def body(dh_ref, x_ref, dres_ref, g_ref, sc_ref, dx_ref, dsc_ref, dsh_ref, dg_ref):
        b, i = pl.program_id(0), pl.program_id(1)
        xv, dhv, gv = x_ref[...], dh_ref[...], g_ref[...]
        r = lax.rsqrt(jnp.mean(xv * xv, axis=-1, keepdims=True) + EPS)
        xn = xv * r
        a = dhv * (1.0 + sc_ref[...])
        dxn = a * gv
        dx_ref[...] = dres_ref[...] + r * (dxn - xn * jnp.mean(dxn * xn, axis=-1, keepdims=True))
        p_sc = jnp.sum(dhv * (xn * gv), axis=0, keepdims=True)
        p_sh = jnp.sum(dhv, axis=0, keepdims=True)
        p_g = jnp.sum(a * xn, axis=0, keepdims=True)

        @pl.when(i == 0)
        def _():
            dsc_ref[...] = p_sc
            dsh_ref[...] = p_sh

        @pl.when(i > 0)
        def _():
            dsc_ref[...] += p_sc
            dsh_ref[...] += p_sh

        @pl.when((i == 0) & (b == 0))
        def _():
            dg_ref[...] = p_g

        @pl.when((i > 0) | (b > 0))
        def _():
            dg_ref[...] += p_g

    row = pl.BlockSpec((None, ts, d), lambda b, i: (b, i, 0))
    vec = pl.BlockSpec((None, 1, d), lambda b, i: (b, 0, 0))
    one = pl.BlockSpec((1, d), lambda b, i: (0, 0))
    return pl.pallas_call(
        body, name=name, grid=(bsz, s // ts),
        in_specs=[row, row, row, one, vec],
        out_specs=[row, vec, vec, one],
        out_shape=[jax.ShapeDtypeStruct(x.shape, F32), jax.ShapeDtypeStruct((bsz, 1, d), F32),
                   jax.ShapeDtypeStruct((bsz, 1, d), F32), jax.ShapeDtypeStruct((1, d), F32)],
        compiler_params=_cp(("arbitrary", "arbitrary")),
    )(dh, x, dres, g, sc)


def gate_bwd(dx, mm, gate, name):
    bsz, s, d = dx.shape
    ts = _row_tile(s)

    def body(dx_ref, m_ref, g_ref, dm_ref, dg_ref):
        i = pl.program_id(1)
        dxv = dx_ref[...]
        dm_ref[...] = (dxv * g_ref[...]).astype(dm_ref.dtype)
        p = jnp.sum(dxv * m_ref[...], axis=0, keepdims=True)

        @pl.when(i == 0)
        def _():
            dg_ref[...] = p

        @pl.when(i > 0)
        def _():
            dg_ref[...] += p

    row = pl.BlockSpec((None, ts, d), lambda b, i: (b, i, 0))
    vec = pl.BlockSpec((None, 1, d), lambda b, i: (b, 0, 0))
    return pl.pallas_call(
        body, name=name, grid=(bsz, s // ts),
        in_specs=[row, row, vec], out_specs=[row, vec],
        out_shape=[jax.ShapeDtypeStruct(dx.shape, BF16), jax.ShapeDtypeStruct((bsz, 1, d), F32)],
        compiler_params=_cp(("parallel", "arbitrary")),
    )(dx, mm, gate)


def loss_head(x, g, tgt, name):
    bsz, s, d = x.shape
    ts = _row_tile(s)

    def body(x_ref, g_ref, t_ref, dx_ref, dg_ref, l_ref):
        b, i = pl.program_id(0), pl.program_id(1)
        xv, gv = x_ref[...], g_ref[...]
        r = lax.rsqrt(jnp.mean(xv * xv, axis=-1, keepdims=True) + EPS)
        xn = xv * r
        e = xn * gv - t_ref[...]
        dy = e * (1.0 / d)
        dxn = dy * gv
        dx_ref[...] = r * (dxn - xn * jnp.mean(dxn * xn, axis=-1, keepdims=True))
        p_g = jnp.sum(dy * xn, axis=0, keepdims=True)
        p_l = jnp.zeros((1, LANES), F32) + jnp.sum(e * e) * (0.5 / d)
        first = (i == 0) & (b == 0)

        @pl.when(first)
        def _():
            dg_ref[...] = p_g
            l_ref[...] = p_l

        @pl.when(jnp.logical_not(first))
        def _():
            dg_ref[...] += p_g
            l_ref[...] += p_l

    row = pl.BlockSpec((None, ts, d), lambda b, i: (b, i, 0))
    one = pl.BlockSpec((1, d), lambda b, i: (0, 0))
    return pl.pallas_call(
        body, name=name, grid=(bsz, s // ts),
        in_specs=[row, one, row],
        out_specs=[row, one, pl.BlockSpec((1, LANES), lambda b, i: (0, 0))],
        out_shape=[jax.ShapeDtypeStruct(x.shape, F32), jax.ShapeDtypeStruct((1, d), F32),
                   jax.ShapeDtypeStruct((1, LANES), F32)],
        compiler_params=_cp(("arbitrary", "arbitrary")),
    )(x, g, tgt)


def _gmlp_chunk(u_raw, v_raw, ln_g, ln_b, w, bs_t, out_g):
    c = u_raw.shape[0]
    u, v = _gelu(u_raw), _gelu(v_raw)
    tril = lax.broadcasted_iota(jnp.int32, (c, c), 0) >= lax.broadcasted_iota(jnp.int32, (c, c), 1)
    ys = []
    for h in range(GM_HEADS):
        sl = slice(h * GM_HD, (h + 1) * GM_HD)
        vh = v[:, sl]
        xc = vh - jnp.mean(vh, axis=-1, keepdims=True)
        vn = xc * lax.rsqrt(jnp.mean(xc * xc, axis=-1, keepdims=True) + LN_EPS) * ln_g[:, sl] + ln_b[:, sl]
        gate = _nn(jnp.where(tril, w[h], 0.0), vn) + bs_t[:, h:h + 1]
        ys.append(u[:, sl] * gate)
    return _rms(jnp.concatenate(ys, axis=1), out_g)


GM_CB = 4


def _gmlp_specs(s):
    cb = min(GM_CB, s // CHUNK)
    rows = cb * CHUNK
    seg = lambda off: pl.BlockSpec((None, rows, GMW), lambda b, c: (b, c, off // GMW))
    full = lambda shape: pl.BlockSpec(shape, lambda b, c: (0,) * len(shape))
    par = [full((1, GMW)), full((1, GMW)), full((GM_HEADS, CHUNK, CHUNK)), full((CHUNK, GM_HEADS)), full((1, GMW))]
    return cb, rows, seg, par


def gmlp_fwd(p, prm, name):
    bsz, s, _ = p.shape
    cb, rows, seg, par = _gmlp_specs(s)

    def body(u_ref, v_ref, lg, lb, w, bt, og, o_ref):
        for k in range(cb):
            sl = pl.ds(k * CHUNK, CHUNK)
            o_ref[sl, :] = _gmlp_chunk(u_ref[sl, :], v_ref[sl, :], lg[...], lb[...], w[...], bt[...],
                                       og[...]).astype(o_ref.dtype)

    return pl.pallas_call(
        body, name=name, grid=(bsz, s // rows),
        in_specs=[seg(OFF_U), seg(OFF_V)] + par,
        out_specs=pl.BlockSpec((None, rows, GMW), lambda b, c: (b, c, 0)),
        out_shape=jax.ShapeDtypeStruct((bsz, s, GMW), BF16),
        compiler_params=_cp(("parallel", "parallel")),
    )(p, p, *prm)


def _accumulate(first, refs, vals):
    @pl.when(first)
    def _():
        for r, v in zip(refs, vals):
            r[...] = v

    @pl.when(jnp.logical_not(first))
    def _():
        for r, v in zip(refs, vals):
            r[...] += v


def gmlp_bwd(p, dmix, prm, name):
    bsz, s, _ = p.shape
    cb, rows, seg, par = _gmlp_specs(s)

    def body(u_ref, v_ref, do_ref, lg, lb, w, bt, og, du_ref, dv_ref, *dpar):
        first = (pl.program_id(0) == 0) & (pl.program_id(1) == 0)
        dpars = None
        for k in range(cb):
            sl = pl.ds(k * CHUNK, CHUNK)
            _, vjp = jax.vjp(_gmlp_chunk, u_ref[sl, :], v_ref[sl, :], lg[...], lb[...], w[...], bt[...], og[...])
            gr = vjp(do_ref[sl, :])
            du_ref[sl, :] = gr[0].astype(du_ref.dtype)
            dv_ref[sl, :] = gr[1].astype(dv_ref.dtype)
            dpars = gr[2:] if dpars is None else [a + b for a, b in zip(dpars, gr[2:])]
        _accumulate(first, dpar, dpars)

    out_seg = pl.BlockSpec((None, rows, GMW), lambda b, c: (b, c, 0))
    return pl.pallas_call(
        body, name=name, grid=(bsz, s // rows),
        in_specs=[seg(OFF_U), seg(OFF_V), out_seg] + par,
        out_specs=[out_seg, out_seg] + par,
        out_shape=[jax.ShapeDtypeStruct((bsz, s, GMW), BF16)] * 2 + [jax.ShapeDtypeStruct(x.shape, F32) for x in prm],
        compiler_params=_cp(("arbitrary", "arbitrary")),
    )(p, p, dmix, *prm)


def _attn_block(q, kp, kc, vp, vc, sinks, out_g, has_prev):
    w = q.shape[0]
    k2 = jnp.concatenate([kp, kc], axis=0)
    v2 = jnp.concatenate([vp, vc], axis=0)
    qi = lax.broadcasted_iota(jnp.int32, (w, 2 * w), 0)
    kj = lax.broadcasted_iota(jnp.int32, (w, 2 * w), 1)
    diff = qi + w - kj
    grp = ATT_HEADS // ATT_KV
    valid = (diff >= 0) & (diff < w) & ((kj >= w) | has_prev)
    valid = jnp.concatenate([valid] * grp, axis=0)
    outs = []
    for kv in range(ATT_KV):
        kh = k2[:, kv * ATT_HD:(kv + 1) * ATT_HD]
        vh = v2[:, kv * ATT_HD:(kv + 1) * ATT_HD]
        heads = range(kv * grp, (kv + 1) * grp)
        qs = jnp.concatenate([q[:, h * ATT_HD:(h + 1) * ATT_HD] for h in heads], axis=0)
        sink = jnp.concatenate([jnp.broadcast_to(sinks[:, h:h + 1], (w, 1)) for h in heads], axis=0)
        sc = jnp.where(valid, _nt(qs, kh) * (ATT_HD ** -0.5), NEG)
        m = jnp.maximum(jnp.max(sc, axis=-1, keepdims=True), sink)
        e = jnp.exp(sc - m)
        pr = e / (jnp.sum(e, axis=-1, keepdims=True) + jnp.exp(sink - m))
        o = _nn(pr, vh)
        outs += [o[gi * w:(gi + 1) * w] for gi in range(grp)]
    return _rms(jnp.concatenate(outs, axis=1), out_g)


ATT_QB_FWD, ATT_QB_BWD = 8, 4


def _attn_tiles(s, windows=ATT_QB_BWD):
    qb = min(windows, s // WINDOW)
    return qb, qb * WINDOW, s // (qb * WINDOW)


def attn_fwd(p, sinks, out_g, name):
    bsz, s, _ = p.shape
    qb, rows, steps = _attn_tiles(s, ATT_QB_FWD)

    def body(q_ref, kp_ref, kc_ref, vp_ref, vc_ref, s_ref, g_ref, o_ref):
        n = pl.program_id(1)
        for w in range(qb):
            sl = pl.ds(w * WINDOW, WINDOW)
            before = pl.ds((w - 1) * WINDOW, WINDOW)
            kp = kp_ref[...] if w == 0 else kc_ref[before, :]
            vp = vp_ref[...] if w == 0 else vc_ref[before, :]
            o_ref[sl, :] = _attn_block(q_ref[sl, :], kp, kc_ref[sl, :], vp, vc_ref[sl, :], s_ref[...], g_ref[...],
                                       (n > 0) if w == 0 else True).astype(o_ref.dtype)

    cur = lambda off: pl.BlockSpec((None, rows, KVW), lambda b, n: (b, n, off // KVW))
    prev = lambda off: pl.BlockSpec((None, WINDOW, KVW), lambda b, n: (b, jnp.maximum(n * qb - 1, 0), off // KVW))
    return pl.pallas_call(
        body, name=name, grid=(bsz, steps),
        in_specs=[pl.BlockSpec((None, rows, ATW), lambda b, n: (b, n, OFF_Q // ATW)),
                  prev(OFF_K), cur(OFF_K), prev(OFF_VV), cur(OFF_VV),
                  pl.BlockSpec((1, ATT_HEADS), lambda b, n: (0, 0)), pl.BlockSpec((1, ATW), lambda b, n: (0, 0))],
        out_specs=pl.BlockSpec((None, rows, ATW), lambda b, n: (b, n, 0)),
        out_shape=jax.ShapeDtypeStruct((bsz, s, ATW), BF16),
        compiler_params=_cp(("parallel", "parallel")),
    )(p, p, p, p, p, sinks, out_g)


def attn_bwd(p, dmix, sinks, out_g, name):
    bsz, s, _ = p.shape
    qb, rows, steps = _attn_tiles(s)
    last = pl.ds(rows - WINDOW, WINDOW)

    def body(q_ref, kp_ref, kc_ref, vp_ref, vc_ref, do_ref, s_ref, g_ref,
             dq_ref, dk_ref, dv_ref, ds_ref, dg_ref, ck, cv):
        b, n = pl.program_id(0), pl.program_id(1)

        @pl.when(n == 0)
        def _():
            ck[...] = jnp.zeros_like(ck)
            cv[...] = jnp.zeros_like(cv)

        @pl.when(n < steps)
        def _():
            grads = []
            for w in range(qb):
                sl = pl.ds(w * WINDOW, WINDOW)
                before = pl.ds((w - 1) * WINDOW, WINDOW)
                kp = kp_ref[...] if w == 0 else kc_ref[before, :]
                vp = vp_ref[...] if w == 0 else vc_ref[before, :]
                fn = functools.partial(_attn_block, has_prev=(n > 0) if w == 0 else True)
                _, vjp = jax.vjp(fn, q_ref[sl, :], kp, kc_ref[sl, :], vp, vc_ref[sl, :], s_ref[...], g_ref[...])
                grads.append(vjp(do_ref[sl, :]))
                dq_ref[sl, :] = grads[-1][0].astype(dq_ref.dtype)
            dk_ref[...] = ck[...].astype(dk_ref.dtype)
            dv_ref[...] = cv[...].astype(dv_ref.dtype)
            dk_ref[last, :] = (ck[last, :] + grads[0][1]).astype(dk_ref.dtype)
            dv_ref[last, :] = (cv[last, :] + grads[0][3]).astype(dv_ref.dtype)
            for w in range(qb):
                sl = pl.ds(w * WINDOW, WINDOW)
                ck[sl, :] = grads[w][2] + (grads[w + 1][1] if w + 1 < qb else 0.0)
                cv[sl, :] = grads[w][4] + (grads[w + 1][3] if w + 1 < qb else 0.0)
            dsk = functools.reduce(lambda u, v: u + v, [g[5] for g in grads])
            dgg = functools.reduce(lambda u, v: u + v, [g[6] for g in grads])
            _accumulate((b == 0) & (n == 0), (ds_ref, dg_ref), (dsk, dgg))

        @pl.when(n == steps)
        def _():
            dk_ref[...] = ck[...].astype(dk_ref.dtype)
            dv_ref[...] = cv[...].astype(dv_ref.dtype)

    at = lambda n: jnp.minimum(n, steps - 1)
    cur = lambda off: pl.BlockSpec((None, rows, KVW), lambda b, n: (b, at(n), off // KVW))
    prev = lambda off: pl.BlockSpec((None, WINDOW, KVW), lambda b, n: (b, jnp.maximum(at(n) * qb - 1, 0), off // KVW))
    kv_out = pl.BlockSpec((None, rows, KVW), lambda b, n: (b, jnp.maximum(n - 1, 0), 0))
    return pl.pallas_call(
        body, name=name, grid=(bsz, steps + 1),
        in_specs=[pl.BlockSpec((None, rows, ATW), lambda b, n: (b, at(n), OFF_Q // ATW)),
                  prev(OFF_K), cur(OFF_K), prev(OFF_VV), cur(OFF_VV),
                  pl.BlockSpec((None, rows, ATW), lambda b, n: (b, at(n), GMW // ATW)),
                  pl.BlockSpec((1, ATT_HEADS), lambda b, n: (0, 0)), pl.BlockSpec((1, ATW), lambda b, n: (0, 0))],
        out_specs=[pl.BlockSpec((None, rows, ATW), lambda b, n: (b, at(n), 0)), kv_out, kv_out,
                   pl.BlockSpec((1, ATT_HEADS), lambda b, n: (0, 0)), pl.BlockSpec((1, ATW), lambda b, n: (0, 0))],
        out_shape=[jax.ShapeDtypeStruct((bsz, s, ATW), BF16), jax.ShapeDtypeStruct((bsz, s, KVW), BF16),
                   jax.ShapeDtypeStruct((bsz, s, KVW), BF16), jax.ShapeDtypeStruct((1, ATT_HEADS), F32),
                   jax.ShapeDtypeStruct((1, ATW), F32)],
        scratch_shapes=[pltpu.VMEM((rows, KVW), F32), pltpu.VMEM((rows, KVW), F32)],
        compiler_params=_cp(("arbitrary", "arbitrary")),
    )(p, p, p, p, p, dmix, sinks, out_g)


CONV_CT = 256


def _shift_down(x, j):
    if j == 0:
        return x
    rows = lax.broadcasted_iota(jnp.int32, x.shape, 0)
    return jnp.where(rows >= j, pltpu.roll(x, j, 0), 0.0)


def _shift_up(x, j):
    if j == 0:
        return x
    s = x.shape[0]
    rows = lax.broadcasted_iota(jnp.int32, x.shape, 0)
    return jnp.where(rows < s - j, pltpu.roll(x, s - j, 0), 0.0)


def conv_fwd(p, w, bias, name):
    bsz, s, _ = p.shape

    def body(x_ref, w_ref, b_ref, o_ref):
        xv, wv = x_ref[...], w_ref[...]
        pre = b_ref[...] + sum(wv[k:k + 1, :] * _shift_down(xv, CONV_K - 1 - k) for k in range(CONV_K))
        o_ref[...] = _silu(pre)

    blk = pl.BlockSpec((None, s, CONV_CT), lambda b, j: (b, 0, j))
    src = pl.BlockSpec((None, s, CONV_CT), lambda b, j: (b, 0, OFF_XBC // CONV_CT + j))
    return pl.pallas_call(
        body, name=name, grid=(bsz, CCH // CONV_CT),
        in_specs=[src, pl.BlockSpec((CONV_K, CONV_CT), lambda b, j: (0, j)), pl.BlockSpec((1, CONV_CT), lambda b, j: (0, j))],
        out_specs=blk, out_shape=jax.ShapeDtypeStruct((bsz, s, CCH), F32),
        compiler_params=_cp(("parallel", "parallel")),
    )(p, w, bias)


def conv_bwd(p, dxc, w, bias, name):
    bsz, s, _ = p.shape

    def body(x_ref, d_ref, w_ref, b_ref, dx_ref, dw_ref, db_ref):
        b = pl.program_id(1)
        xv, wv = x_ref[...], w_ref[...]
        xs = [_shift_down(xv, CONV_K - 1 - k) for k in range(CONV_K)]
        pre = b_ref[...] + sum(wv[k:k + 1, :] * xs[k] for k in range(CONV_K))
        sg = 1.0 / (1.0 + jnp.exp(-pre))
        dpre = d_ref[...] * (sg * (1.0 + pre * (1.0 - sg)))
        dx_ref[...] = sum(wv[k:k + 1, :] * _shift_up(dpre, CONV_K - 1 - k) for k in range(CONV_K)).astype(dx_ref.dtype)
        p_w = jnp.concatenate([jnp.sum(dpre * xs[k], axis=0, keepdims=True) for k in range(CONV_K)], axis=0)
        p_b = jnp.sum(dpre, axis=0, keepdims=True)
        _accumulate(b == 0, (dw_ref, db_ref), (p_w, p_b))

    blk = pl.BlockSpec((None, s, CONV_CT), lambda j, b: (b, 0, j))
    src = pl.BlockSpec((None, s, CONV_CT), lambda j, b: (b, 0, OFF_XBC // CONV_CT + j))
    wsp = pl.BlockSpec((CONV_K, CONV_CT), lambda j, b: (0, j))
    bsp = pl.BlockSpec((1, CONV_CT), lambda j, b: (0, j))
    return pl.pallas_call(
        body, name=name, grid=(CCH // CONV_CT, bsz),
        in_specs=[src, blk, wsp, bsp], out_specs=[blk, wsp, bsp],
        out_shape=[jax.ShapeDtypeStruct((bsz, s, CCH), BF16), jax.ShapeDtypeStruct((CONV_K, CCH), F32),
                   jax.ShapeDtypeStruct((1, CCH), F32)],
        compiler_params=_cp(("parallel", "arbitrary")),
    )(p, dxc, w, bias)


def _ssd_consts():
    c = CHUNK
    r = lax.broadcasted_iota(jnp.int32, (c, c), 0)
    q = lax.broadcasted_iota(jnp.int32, (c, c), 1)
    hrow = lax.broadcasted_iota(jnp.int32, (LANES, SSW), 0)
    hcol = lax.broadcasted_iota(jnp.int32, (LANES, SSW), 1) // SSM_HD
    expand = (hrow == hcol).astype(F32)
    return expand, (r >= q).astype(F32), (r <= q).astype(F32), r >= q


def _ssd_chunk(xc, dtr, z, prev_t, dt_bias, a_log, d_skip, norm_g):
    c = xc.shape[0]
    expand, tril1, triu1, causal = _ssd_consts()
    xs, bm, cm = xc[:, :SSW], xc[:, SSW:SSW + BCW], xc[:, SSW + BCW:]
    dt = _softplus(dtr + dt_bias)
    da = dt * (-jnp.exp(a_log))
    a_cs = _hdot(tril1, da)
    a_cs_t = _hdot(da.T, triu1)
    dt_e = _hdot(dt, expand)
    acs_e = _hdot(a_cs, expand)
    alast_e = acs_e[c - 1:c, :]
    dsk_e = _hdot(jnp.broadcast_to(d_skip, (8, LANES)), expand)[0:1, :]
    xdt = xs * dt_e
    hg = SSM_HEADS // SSM_GROUPS
    ys, new_t = [], []
    for g in range(SSM_GROUPS):
        bg = bm[:, g * SSM_STATE:(g + 1) * SSM_STATE]
        cg = cm[:, g * SSM_STATE:(g + 1) * SSM_STATE]
        sl = slice(g * GRW, (g + 1) * GRW)
        cb = _nt(cg, bg)
        xdt_g = xdt[:, sl]
        st = _tn(bg, xdt_g * jnp.exp(alast_e[:, sl] - acs_e[:, sl]))
        new_t.append(prev_t[:, sl] * jnp.exp(alast_e[:, sl]) + st)
        y_off = _nn(cg, prev_t[:, sl]) * jnp.exp(acs_e[:, sl])
        yd = []
        low = lax.broadcasted_iota(jnp.int32, (c, LANES), 1) < SSM_HD
        for pair in range(hg // 2):
            xp = xdt_g[:, pair * LANES:(pair + 1) * LANES]
            acc = None
            for side, xh in enumerate((jnp.where(low, xp, 0.0), jnp.where(low, 0.0, xp))):
                h = g * hg + 2 * pair + side
                decay = jnp.exp(jnp.where(causal, a_cs[:, h:h + 1] - a_cs_t[h:h + 1, :], NEG))
                part = _nn(cb * decay, xh)
                acc = part if acc is None else acc + part
            yd.append(acc)
        ys.append(jnp.concatenate(yd, axis=1) + y_off)
    y = (jnp.concatenate(ys, axis=1) + xs * dsk_e) * _silu(z)
    yn = [y[:, g * GRW:(g + 1) * GRW] * lax.rsqrt(jnp.mean(jnp.square(y[:, g * GRW:(g + 1) * GRW]), axis=-1, keepdims=True) + EPS)
          for g in range(SSM_GROUPS)]
    return jnp.concatenate(yn, axis=1) * norm_g, jnp.concatenate(new_t, axis=1)


def ssd_fwd(xc, p, prm, name):
    bsz, s, _ = p.shape
    nc = s // CHUNK

    def body(xc_ref, dt_ref, *rest):
        z_refs, (db, al, dk, ng, o_ref, st_ref, state) = rest[:SSW // ZB], rest[SSW // ZB:]
        @pl.when(pl.program_id(0) == 0)
        def _():
            state[...] = jnp.zeros_like(state)

        for b in range(bsz):
            prev = state[b]
            st_ref[b, 0] = prev
            zb = jnp.concatenate([r[b] for r in z_refs], axis=1)
            out, new = _ssd_chunk(xc_ref[b], dt_ref[b], zb, prev, db[...], al[...], dk[...], ng[...])
            o_ref[b] = out.astype(o_ref.dtype)
            state[b] = new

    vec = pl.BlockSpec((1, LANES), lambda c: (0, 0))
    return pl.pallas_call(
        body, name=name, grid=(nc,),
        in_specs=[pl.BlockSpec((bsz, CHUNK, CCH), lambda c: (0, c, 0)),
                  pl.BlockSpec((bsz, CHUNK, LANES), lambda c: (0, c, OFF_DT // LANES)),
                  *[pl.BlockSpec((bsz, CHUNK, ZB), lambda c, i=i: (0, c, OFF_Z // ZB + i)) for i in range(SSW // ZB)],
                  vec, vec, vec, pl.BlockSpec((1, SSW), lambda c: (0, 0))],
        out_specs=[pl.BlockSpec((bsz, CHUNK, SSW), lambda c: (0, c, 0)),
                   pl.BlockSpec((bsz, 1, SSM_STATE, SSW), lambda c: (0, c, 0, 0))],
        out_shape=[jax.ShapeDtypeStruct((bsz, s, SSW), BF16), jax.ShapeDtypeStruct((bsz, nc, SSM_STATE, SSW), F32)],
        scratch_shapes=[pltpu.VMEM((bsz, SSM_STATE, SSW), F32)],
        compiler_params=_cp(("arbitrary",)),
    )(xc, p, *([p] * (SSW // ZB)), *prm)


def ssd_bwd(xc, p, states, dmix, prm, name):
    bsz, s, _ = p.shape
    nc = s // CHUNK

    def body(xc_ref, dt_ref, *rest):
        z_refs, (st_ref, do_ref, db, al, dk, ng, dxc_ref, ddt_ref, dz_ref) = rest[:SSW // ZB], rest[SSW // ZB:SSW // ZB + 9]
        rest = rest[SSW // ZB + 9:]
        dpar, dstate = rest[:4], rest[4]
        c = pl.program_id(0)

        @pl.when(c == 0)
        def _():
            dstate[...] = jnp.zeros_like(dstate)

        dpars = None
        for b in range(bsz):
            zb = jnp.concatenate([r[b] for r in z_refs], axis=1)
            _, vjp = jax.vjp(_ssd_chunk, xc_ref[b], dt_ref[b], zb, st_ref[b, 0], db[...], al[...], dk[...], ng[...])
            gr = vjp((do_ref[b], dstate[b]))
            dxc_ref[b] = gr[0]
            ddt_ref[b] = gr[1].astype(ddt_ref.dtype)
            dz_ref[b] = gr[2].astype(dz_ref.dtype)
            dstate[b] = gr[3]
            dpars = gr[4:] if dpars is None else [u + v for u, v in zip(dpars, gr[4:])]
        _accumulate(c == 0, dpar, dpars)

    rv = lambda c: nc - 1 - c
    vec = pl.BlockSpec((1, LANES), lambda c: (0, 0))
    ngs = pl.BlockSpec((1, SSW), lambda c: (0, 0))
    return pl.pallas_call(
        body, name=name, grid=(nc,),
        in_specs=[pl.BlockSpec((bsz, CHUNK, CCH), lambda c: (0, rv(c), 0)),
                  pl.BlockSpec((bsz, CHUNK, LANES), lambda c: (0, rv(c), OFF_DT // LANES)),
                  *[pl.BlockSpec((bsz, CHUNK, ZB), lambda c, i=i: (0, rv(c), OFF_Z // ZB + i)) for i in range(SSW // ZB)],
                  pl.BlockSpec((bsz, 1, SSM_STATE, SSW), lambda c: (0, rv(c), 0, 0)),
                  pl.BlockSpec((bsz, CHUNK, SSW), lambda c: (0, rv(c), (GMW + ATW) // SSW)),
                  vec, vec, vec, ngs],
        out_specs=[pl.BlockSpec((bsz, CHUNK, CCH), lambda c: (0, rv(c), 0)),
                   pl.BlockSpec((bsz, CHUNK, LANES), lambda c: (0, rv(c), 0)),
                   pl.BlockSpec((bsz, CHUNK, SSW), lambda c: (0, rv(c), 0)),
                   vec, vec, vec, ngs],
        out_shape=[jax.ShapeDtypeStruct((bsz, s, CCH), F32), jax.ShapeDtypeStruct((bsz, s, LANES), BF16),
                   jax.ShapeDtypeStruct((bsz, s, SSW), BF16)] + [jax.ShapeDtypeStruct((1, LANES), F32)] * 3
                  + [jax.ShapeDtypeStruct((1, SSW), F32)],
        scratch_shapes=[pltpu.VMEM((bsz, SSM_STATE, SSW), F32)],
        compiler_params=_cp(("arbitrary",)),
    )(xc, p, *([p] * (SSW // ZB)), states, dmix, *prm)


def _rows2d(a):
    return a.reshape(-1, a.shape[-1])


def _ew_tile(r, c):
    t = r
    while t * c > (1 << 20) and t % 16 == 0:
        t //= 2
    return t


def add_pair(g, theirs, chip, core, name):
    k, r, c = g.shape
    h = r // 2
    tr = _ew_tile(h, c)
    nb = h // tr

    def body(s_ref, a_ref, b_ref, own_ref, ob_ref):
        s = a_ref[...] + b_ref[...]
        ob_ref[...] = s.astype(ob_ref.dtype)

        @pl.when(pl.program_id(1) == s_ref[1])
        def _():
            own_ref[...] = s

    blk = pl.BlockSpec((None, tr, c), lambda i, kk, sr: (kk, i, 0))
    return pl.pallas_call(
        body, name=name,
        grid_spec=pltpu.PrefetchScalarGridSpec(
            num_scalar_prefetch=1, grid=(nb, k),
            in_specs=[pl.BlockSpec((None, tr, c), lambda i, kk, sr: (kk, sr[0] * nb + i, 0)), blk],
            out_specs=[pl.BlockSpec((tr, c), lambda i, kk, sr: (i, 0)), blk]),
        out_shape=[jax.ShapeDtypeStruct((h, c), F32), jax.ShapeDtypeStruct(theirs.shape, BF16)],
        compiler_params=_cp(("parallel", "arbitrary")),
    )(jnp.stack([core, chip]).astype(jnp.int32), g, theirs)


def sum_own_recv(own, recv, core, name):
    h, c = own.shape
    tr = _ew_tile(h, c)
    nb = h // tr

    def body(k_ref, o_ref, r_ref, out_ref):
        s = o_ref[...]
        for j in range(3):
            s = s + r_ref[j].astype(F32)
        out_ref[...] = s

    return pl.pallas_call(
        body, name=name,
        grid_spec=pltpu.PrefetchScalarGridSpec(
            num_scalar_prefetch=1, grid=(nb,),
            in_specs=[pl.BlockSpec((tr, c), lambda i, kr: (i, 0)),
                      pl.BlockSpec((3, tr, c), lambda i, kr: (0, i, 0))],
            out_specs=pl.BlockSpec((tr, c), lambda i, kr: (kr[0] * nb + i, 0))),
        out_shape=jax.ShapeDtypeStruct((2 * h, c), F32),
        compiler_params=_cp(("parallel",)),
    )(core.reshape(1).astype(jnp.int32), own, recv)


def _adam_math(w, m, v, g):
    mn = ADAM_B1 * m + (1.0 - ADAM_B1) * g
    vn = ADAM_B2 * v + (1.0 - ADAM_B2) * (g * g)
    mh = mn / (1.0 - ADAM_B1 ** ADAM_STEP)
    vh = vn / (1.0 - ADAM_B2 ** ADAM_STEP)
    return -ADAM_LR * (mh / (jnp.sqrt(vh) + ADAM_EPS) + ADAM_WD * w), mn, vn


def adamw_minor_rows(w, m, v, g, name):
    r, nl, c = w.shape
    tr = max(t for t in range(1, r + 1) if r % t == 0 and t * nl * c <= (1 << 19))

    def body(w_ref, m_ref, v_ref, g_ref, d_ref, mo_ref, vo_ref):
        d_ref[...], mo_ref[...], vo_ref[...] = _adam_math(w_ref[...], m_ref[...], v_ref[...], g_ref[...])

    blk = pl.BlockSpec((tr, nl, c), lambda i: (i, 0, 0))
    return pl.pallas_call(
        body, name=name, grid=(r // tr,), in_specs=[blk] * 4, out_specs=[blk] * 3,
        out_shape=[jax.ShapeDtypeStruct(w.shape, F32)] * 3, compiler_params=_cp(("parallel",)),
    )(w, m, v, g)


def adamw_layer(w, m, v, g, layer, prev, name):
    nl, r, c = w.shape
    tr = _ew_tile(r, c * 2)

    def body(w_ref, m_ref, v_ref, g_ref, *rest):
        go_ref, d_ref, mo_ref, vo_ref = rest[-4:]
        gv = g_ref[...]
        dl, mn, vn = _adam_math(w_ref[...], m_ref[...], v_ref[...], gv)
        go_ref[...] = gv
        d_ref[...] = dl
        mo_ref[...] = mn
        vo_ref[...] = vn

    lay = pl.BlockSpec((None, tr, c), lambda i: (layer, i, 0))
    n_prev = 0 if prev is None else 4
    return pl.pallas_call(
        body, name=name, grid=(r // tr,),
        in_specs=[lay, lay, lay, pl.BlockSpec((tr, c), lambda i: (i, 0))] + [_ANY] * n_prev,
        out_specs=[lay] * 4, out_shape=[jax.ShapeDtypeStruct(w.shape, F32)] * 4,
        input_output_aliases={4 + i: i for i in range(n_prev)},
        compiler_params=_cp(("parallel",)),
    )(w, m, v, g, *(prev or ()))


def sum_devices(parts, name):
    n, r, c = parts.shape
    tr = _ew_tile(r, c * n)

    def body(p_ref, o_ref):
        s = p_ref[0]
        for j in range(1, n):
            s = s + p_ref[j]
        o_ref[...] = s

    return pl.pallas_call(
        body, name=name, grid=(r // tr,),
        in_specs=[pl.BlockSpec((n, tr, c), lambda i: (0, i, 0))],
        out_specs=pl.BlockSpec((tr, c), lambda i: (i, 0)),
        out_shape=jax.ShapeDtypeStruct((r, c), F32),
        compiler_params=_cp(("parallel",)),
    )(parts)


def adamw(w, m, v, g, name):
    r, c = w.shape
    tr = _ew_tile(r, c * 2)

    def body(w_ref, m_ref, v_ref, g_ref, d_ref, mo_ref, vo_ref):
        gv = g_ref[...]
        mn = ADAM_B1 * m_ref[...] + (1.0 - ADAM_B1) * gv
        vn = ADAM_B2 * v_ref[...] + (1.0 - ADAM_B2) * (gv * gv)
        mh = mn / (1.0 - ADAM_B1 ** ADAM_STEP)
        vh = vn / (1.0 - ADAM_B2 ** ADAM_STEP)
        d_ref[...] = -ADAM_LR * (mh / (jnp.sqrt(vh) + ADAM_EPS) + ADAM_WD * w_ref[...])
        mo_ref[...] = mn
        vo_ref[...] = vn

    blk = pl.BlockSpec((tr, c), lambda i: (i, 0))
    return pl.pallas_call(
        body, name=name, grid=(r // tr,), in_specs=[blk] * 4, out_specs=[blk] * 3,
        out_shape=[jax.ShapeDtypeStruct((r, c), F32)] * 3,
        compiler_params=_cp(("parallel",)),
    )(w, m, v, g)


def _place():
    x, y, c = lax.axis_index("x"), lax.axis_index("y"), lax.axis_index("c")
    chips = [(1 - x, y), (x, 1 - y), (1 - x, 1 - y)]
    return x, y, c, chips


def all_gather_small(v, name):
    r, w = v.shape

    def body(x_ref, out_ref, send_sems, recv_sems, local_sem):
        x, y, c, chips = _place()
        me, sibling = (x, y, c), (x, y, 1 - c)

        def rows(px, py, pc):
            return out_ref.at[pl.ds((4 * px + 2 * py + pc) * r, r), :]

        def copy(k, block, to, src=None):
            return pltpu.make_async_remote_copy(
                src_ref=rows(*block) if src is None else src, dst_ref=rows(*block),
                send_sem=send_sems.at[k], recv_sem=recv_sems.at[k], device_id=to, device_id_type=MESH)

        mine = pltpu.make_async_copy(x_ref, rows(*me), local_sem)
        mine.start()
        first = [copy(0, me, sibling, src=x_ref)]
        first += [copy(1 + j, me, (*chip, c), src=x_ref) for j, chip in enumerate(chips)]
        for cp in first:
            cp.start()
        passed = [copy(4 + j, (*chip, c), sibling) for j, chip in enumerate(chips)]
        for j, chip in enumerate(chips):
            copy(1 + j, (*chip, c), me).wait_recv()
            passed[j].start()
        copy(0, sibling, me).wait_recv()
        for j, chip in enumerate(chips):
            copy(4 + j, (*chip, 1 - c), me).wait_recv()
        for cp in first + passed:
            cp.wait_send()
        mine.wait()

    out = pl.pallas_call(
        body, name=name, out_shape=jax.ShapeDtypeStruct((8 * r, w), v.dtype),
        in_specs=[pl.BlockSpec(memory_space=pltpu.VMEM)], out_specs=pl.BlockSpec(memory_space=pltpu.VMEM),
        scratch_shapes=[pltpu.SemaphoreType.DMA((7,)), pltpu.SemaphoreType.DMA((7,)), pltpu.SemaphoreType.DMA],
        compiler_params=pltpu.CompilerParams(vmem_limit_bytes=VMEM_LIMIT),
    )(v)
    return out.reshape(8, r, w)


_HBM = pl.BlockSpec(memory_space=pltpu.HBM)


_SEM = pl.BlockSpec(memory_space=pltpu.SEMAPHORE)
_ANY = pl.BlockSpec(memory_space=pl.ANY)
_EFFECT = pltpu.SideEffectType.DATAFLOW_SIDE_EFFECTING


def _hbm(a):
    return pltpu.with_memory_space_constraint(a, pltpu.HBM)


def split_copy_start(srcs, land_shapes, copies, after, name):
    n, nl = len(srcs), len(land_shapes)
    n_after = 0 if after is None else 1
    ncopy = [0]

    def body(*refs):
        ins, lands = refs[:n], refs[n:n + nl]
        send_sems, recv_sems = refs[n + nl + n_after], refs[n + nl + n_after + 1]
        token = refs[-1]
        x, y, c, chips = _place()
        for k, (src, dst, to) in enumerate(copies(x, y, c, chips, ins, lands)):
            pltpu.make_async_remote_copy(src_ref=src, dst_ref=dst, send_sem=send_sems.at[k], recv_sem=recv_sems.at[k],
                                         device_id=to, device_id_type=MESH).start()
        token[...] = jnp.zeros_like(token)

    ncopy[0] = len(copies(0, 0, 0, [(1, 0), (0, 1), (1, 1)], [None] * n, [None] * nl, count_only=True))
    k = ncopy[0]
    lands = [_hbm(lax.empty(s.shape, s.dtype)) for s in land_shapes]
    res = pl.pallas_call(
        body, name=name,
        out_shape=(pltpu.SemaphoreType.DMA((k,)), pltpu.SemaphoreType.DMA((k,)))
        + tuple(pltpu.HBM(s.shape, s.dtype) for s in srcs) + tuple(pltpu.HBM(s.shape, s.dtype) for s in land_shapes)
        + (jax.ShapeDtypeStruct((8, LANES), F32),),
        in_specs=[_HBM] * (n + nl) + [_ANY] * n_after,
        out_specs=(_SEM, _SEM) + (_HBM,) * (n + nl) + (pl.BlockSpec(memory_space=pltpu.VMEM),),
        input_output_aliases={i: 2 + i for i in range(n + nl)},
        compiler_params=pltpu.CompilerParams(has_side_effects=_EFFECT),
    )(*[_hbm(s) for s in srcs], *lands, *([after] if n_after else []))
    return res[0], res[1], list(res[2:2 + n]), list(res[2 + n:2 + n + nl]), res[-1]


def split_copy_wait(send_sems, recv_sems, srcs, lands, copies, after, name):
    n, nl = len(srcs), len(lands)

    def body(*refs):
        ins, lnd = refs[:n], refs[n:n + nl]
        ss, rs = refs[n + nl], refs[n + nl + 1]
        x, y, c, chips = _place()
        for k, (src, dst, to) in enumerate(copies(x, y, c, chips, ins, lnd, receive=True)):
            cp = pltpu.make_async_remote_copy(src_ref=src, dst_ref=dst, send_sem=ss.at[k], recv_sem=rs.at[k],
                                              device_id=to, device_id_type=MESH)
            cp.wait_send()
            cp.wait_recv()

    res = pl.pallas_call(
        body, name=name,
        out_shape=tuple(pltpu.HBM(s.shape, s.dtype) for s in srcs) + tuple(pltpu.HBM(s.shape, s.dtype) for s in lands),
        in_specs=[_HBM] * (n + nl) + [_SEM, _SEM, _ANY], out_specs=(_HBM,) * (n + nl),
        input_output_aliases={i: i for i in range(n + nl)},
        compiler_params=pltpu.CompilerParams(has_side_effects=_EFFECT),
    )(*srcs, *lands, send_sems, recv_sems, after)
    return list(res[:n]), list(res[n:])


def _gather_copies(x, y, c, chips, ins, lands, receive=False, count_only=False):
    out = []
    for i in range(len(ins)):
        for cx, cy in chips:
            if count_only:
                out.append(None)
                continue
            h = ins[i].shape[0] // 2
            rows = pl.ds(c * h, h)
            k_dst = (2 * cx + cy) if receive else (2 * x + y)
            out.append((ins[i].at[rows, :], lands[i].at[k_dst, rows, :], (cx, cy, c)))
    for i in range(len(ins)):
        out.append(None if count_only else (ins[i], lands[i].at[2 * x + y], (x, y, 1 - c)))
    return out


def _swap_copies(x, y, c, chips, ins, lands, receive=False, count_only=False):
    out = []
    for i in range(len(ins)):
        if count_only:
            out.append(None)
            continue
        h = ins[i].shape[1] // 2
        out.append((ins[i].at[:, pl.ds((1 - c) * h, h), :], lands[i], (x, y, 1 - c)))
    return out


def _scatter_copies(x, y, c, chips, ins, lands, receive=False, count_only=False):
    out = []
    for i in range(len(ins)):
        for j, (cx, cy) in enumerate(chips):
            if count_only:
                out.append(None)
                continue
            out.append((ins[i].at[2 * cx + cy], lands[i].at[j], (cx, cy, c)))
    return out


def forward_halves(lands, name):
    n = len(lands)

    def body(*refs):
        ins, outs = refs[:n], refs[n:2 * n]
        send_sems, recv_sems = refs[2 * n:]
        x, y, c, chips = _place()
        sibling = (x, y, 1 - c)
        sent = []
        for i in range(n):
            h = ins[i].shape[1] // 2
            for j, (cx, cy) in enumerate(chips):
                blk = ins[i].at[2 * cx + cy, pl.ds(c * h, h), :]
                sent.append(pltpu.make_async_remote_copy(
                    src_ref=blk, dst_ref=outs[i].at[2 * cx + cy, pl.ds(c * h, h), :], send_sem=send_sems.at[3 * i + j],
                    recv_sem=recv_sems.at[3 * i + j], device_id=sibling, device_id_type=MESH))
                sent[-1].start()
        for i in range(n):
            h = ins[i].shape[1] // 2
            for j, (cx, cy) in enumerate(chips):
                theirs = outs[i].at[2 * cx + cy, pl.ds((1 - c) * h, h), :]
                pltpu.make_async_remote_copy(
                    src_ref=theirs, dst_ref=theirs, send_sem=send_sems.at[3 * i + j], recv_sem=recv_sems.at[3 * i + j],
                    device_id=sibling, device_id_type=MESH).wait_recv()
        for cp in sent:
            cp.wait_send()

    return pl.pallas_call(
        body, name=name, out_shape=[jax.ShapeDtypeStruct(s.shape, s.dtype) for s in lands],
        in_specs=[_HBM] * n, out_specs=[_HBM] * n, input_output_aliases={i: i for i in range(n)},
        scratch_shapes=[pltpu.SemaphoreType.DMA((3 * n,)), pltpu.SemaphoreType.DMA((3 * n,))],
    )(*lands)


def join_halves(halves, name):
    n = len(halves)

    def body(*refs):
        ins, outs = refs[:n], refs[n:2 * n]
        send_sems, recv_sems = refs[2 * n:]
        x, y, c, _ = _place()
        sibling = (x, y, 1 - c)
        sent = []
        for i in range(n):
            h = ins[i].shape[0] // 2
            sent.append(pltpu.make_async_remote_copy(
                src_ref=ins[i].at[pl.ds(c * h, h), :], dst_ref=outs[i].at[pl.ds(c * h, h), :], send_sem=send_sems.at[i],
                recv_sem=recv_sems.at[i], device_id=sibling, device_id_type=MESH))
            sent[-1].start()
        for i in range(n):
            h = ins[i].shape[0] // 2
            theirs = outs[i].at[pl.ds((1 - c) * h, h), :]
            pltpu.make_async_remote_copy(
                src_ref=theirs, dst_ref=theirs, send_sem=send_sems.at[i],
                recv_sem=recv_sems.at[i], device_id=sibling, device_id_type=MESH).wait_recv()
        for cp in sent:
            cp.wait_send()

    return pl.pallas_call(
        body, name=name, out_shape=[jax.ShapeDtypeStruct(s.shape, F32) for s in halves],
        in_specs=[_HBM] * n, out_specs=[_HBM] * n, input_output_aliases={i: i for i in range(n)},
        scratch_shapes=[pltpu.SemaphoreType.DMA((n,)), pltpu.SemaphoreType.DMA((n,))],
    )(*halves)


_PACK_ROWS = 8 * LANES


def _pack(arrs):
    flat = jnp.concatenate([a.reshape(-1).astype(F32) for a in arrs])
    pad = (-flat.shape[0]) % _PACK_ROWS
    return jnp.pad(flat, (0, pad)).reshape(-1, LANES)


def _unpack(flat, shapes):
    flat = flat.reshape(-1)
    out, off = [], 0
    for s in shapes:
        n = int(np.prod(s))
        out.append(flat[off:off + n].reshape(s))
        off += n
    return out


def _win_from_blocks(g):
    d = g.shape[1]
    return jnp.pad(g.transpose(1, 0, 2).reshape(d, IN_W), ((0, 0), (0, PW - IN_W)))


def _win_to_blocks(w):
    return w[:, :IN_W].reshape(w.shape[0], 4, IN_W // 4).transpose(1, 0, 2)


def _relu2(a):
    r = jnp.maximum(a, 0)
    return r * r


def kernel(x, c, ada_w, ada_b, norm1_g, w_in, gm_ln_g, gm_ln_b, gm_ws, gm_bs, gm_norm_g, attn_sinks, attn_norm_g, conv_w, conv_b, dt_bias, a_log, d_skip, ssm_norm_g, w_out, norm2_g, w_mlp1, w_mlp2, final_norm_g, loss_target, m_ada_w, m_ada_b, m_norm1_g, m_w_in, m_gm_ln_g, m_gm_ln_b, m_gm_ws, m_gm_bs, m_gm_norm_g, m_attn_sinks, m_attn_norm_g, m_conv_w, m_conv_b, m_dt_bias, m_a_log, m_d_skip, m_ssm_norm_g, m_w_out, m_norm2_g, m_w_mlp1, m_w_mlp2, m_final_norm_g, v_ada_w, v_ada_b, v_norm1_g, v_w_in, v_gm_ln_g, v_gm_ln_b, v_gm_ws, v_gm_bs, v_gm_norm_g, v_attn_sinks, v_attn_norm_g, v_conv_w, v_conv_b, v_dt_bias, v_a_log, v_d_skip, v_ssm_norm_g, v_w_out, v_norm2_g, v_w_mlp1, v_w_mlp2, v_final_norm_g):
    nl = ada_w.shape[0]
    bl, s, d = x.shape
    t = bl * s
    dff4 = w_mlp1.shape[2]
    dff = 4 * dff4
    mod_w = ada_w.shape[2]
    cw_w = conv_w.shape[2]
    xi, yi, ci = lax.axis_index("x"), lax.axis_index("y"), lax.axis_index("c")
    chip = 2 * xi + yi
    dev = 2 * chip + ci
    nex = 8 * bl

    shards = [[w_in[l].astype(BF16), w_out[l].astype(BF16), w_mlp1[l].astype(BF16), w_mlp2[l].astype(BF16)]
              for l in range(nl)]
    groups = [[shards[0][i]] for i in range(4)] + [shards[l] for l in range(1, nl)]

    def start_gather(gi, behind):
        ss, rs, srcs, lands, token = split_copy_start(
            groups[gi], [jax.ShapeDtypeStruct((4,) + a.shape, a.dtype) for a in groups[gi]], _gather_copies, behind,
            f"gather_start_{gi}")
        return (ss, rs, srcs, lands), token

    first_gather, first_token = start_gather(0, None)
    g0 = all_gather_small(_pack([c, conv_w]) + first_token[0, 0], "ag_c")
    g0 = g0.reshape(8, -1)
    c_all = g0[:, :bl * d].reshape(nex, d)
    cw_parts = g0[0::2, bl * d:bl * d + conv_w.size].reshape(4, nl, CONV_K, cw_w)
    conv_w_full = cw_parts.transpose(1, 2, 0, 3).reshape(nl, CONV_K, CCH)

    def c_act(a):
        return _silu(a).astype(BF16)

    def to_bf16(a):
        return a.astype(BF16)

    mod_parts = []
    for l in range(nl):
        bias = lax.dynamic_slice(ada_b[l].reshape(1, -1), (0, chip * mod_w), (1, mod_w))
        mod_parts.append(_mm("nn", c_all, ada_w, dims=(nex, mod_w, d), tm=nex, tn=512, tk=d, out_dtypes=[F32],
                             name=f"mod_{l}", pro_a=c_act, pro_b=to_bf16,
                             b_spec=pl.BlockSpec((None, d, 512), lambda i, j, kk, l=l: (l, kk, j)),
                             extras=[(bias, pl.BlockSpec((1, 512), lambda i, j, kk: (0, j)))],
                             epi=lambda acc, bv: (acc + bv,))[0])
    g1 = all_gather_small(_pack(mod_parts), "ag_mod").reshape(8, -1)
    mod_all = g1[0::2, :nl * nex * mod_w].reshape(4, nl, nex, mod_w).transpose(1, 2, 0, 3).reshape(nl, nex, 4 * mod_w)
    mod = lax.dynamic_slice(mod_all, (0, dev * bl, 0), (nl, bl, 4 * mod_w))
    mods = [[mod[l, :, i * d:(i + 1) * d].reshape(bl, 1, d) for i in range(6)] for l in range(nl)]

    pending, after = [first_gather], g1
    for gi in range(1, len(groups)):
        state, after = start_gather(gi, after)
        pending.append(state)
    mods[0][0] = mods[0][0] + after[0, 0]

    def fetch(gi, behind):
        ss, rs, srcs, lands = pending[gi]
        srcs, lands = split_copy_wait(ss, rs, srcs, lands, _gather_copies, behind, f"gather_wait_{gi}")
        return forward_halves(lands, f"gather_pass_{gi}")

    as_win = _win_from_blocks

    wfull = [None] * nl
    row = lambda a: a.reshape(1, -1)
    pad16 = lambda a: jnp.pad(a.reshape(1, -1), ((0, 0), (0, LANES - SSM_HEADS)))
    tm_res = min(1024, s)

    def residual(acc, xt, gt):
        return acc, xt + gt * acc

    def res_extras(xin, gate, tm=tm_res, tn=512):
        return [(xin.reshape(t, d), pl.BlockSpec((tm, tn), lambda i, j, kk: (i, j))),
                (gate, pl.BlockSpec((None, 1, tn), lambda i, j, kk: (i * tm // s, 0, j)))]

    w1_blk = lambda tk, tn: pl.BlockSpec((None, tk, tn), lambda i, j, kk: (j // (dff4 // tn), kk, j % (dff4 // tn)))

    saved = []
    xcur = x
    for l in range(nl):
        sh1, sc1, gt1, sh2, sc2, gt2 = mods[l]
        if l == 0:
            win = as_win(fetch(0, mod)[0])
        else:
            g_in, g_out, w1, g_2 = fetch(3 + l, xcur)
            win, wout, w2 = as_win(g_in), g_out.reshape(-1, d), g_2.reshape(dff, d)
        prm_a = (row(gm_ln_g[l]), row(gm_ln_b[l]), gm_ws[l], gm_bs[l].T, row(gm_norm_g[l]))
        prm_b = (row(attn_sinks[l]), row(attn_norm_g[l]))
        prm_c = (pad16(dt_bias[l]), pad16(a_log[l]), pad16(d_skip[l]), row(ssm_norm_g[l]))
        h1 = ln_mod_fwd(xcur, row(norm1_g[l]), sc1, sh1, f"ln1_fwd_{l}")
        p = _mm("nn", h1.reshape(t, d), win, dims=(t, PW, d), tm=1024, tn=PW // 3, tk=d, out_dtypes=[F32],
                name=f"proj_in_{l}")[0].reshape(bl, s, PW)
        out_a = gmlp_fwd(p, prm_a, f"gmlp_fwd_{l}")
        out_b = attn_fwd(p, *prm_b, f"attn_fwd_{l}")
        xc = conv_fwd(p, conv_w_full[l], row(conv_b[l]), f"conv_fwd_{l}")
        out_c, states = ssd_fwd(xc, p, prm_c, f"ssd_fwd_{l}")
        mix = jnp.concatenate([out_a, out_b, out_c], axis=-1)
        if l == 0:
            wout = fetch(1, mix)[0].reshape(-1, d)
        mm1, x2 = _mm("nn", mix.reshape(t, d), wout, dims=(t, d, d), tm=tm_res, tn=1024, tk=d, out_dtypes=[F32, F32],
                      name=f"proj_out_{l}", extras=res_extras(xcur, gt1, tm_res, 1024), epi=residual)
        x2 = x2.reshape(bl, s, d)
        h2 = ln_mod_fwd(x2, row(norm2_g[l]), sc2, sh2, f"ln2_fwd_{l}")
        if l == 0:
            w1 = fetch(2, h2)[0]
        a1 = _mm("nn", h2.reshape(t, d), w1, dims=(t, dff, d), tm=1024, tn=2048, tk=d, out_dtypes=[BF16],
                 name=f"mlp1_{l}", b_spec=w1_blk(d, 2048))[0]
        if l == 0:
            w2 = fetch(3, a1)[0].reshape(dff, d)
        tm2 = min(512, s)
        mm2, x3 = _mm("nn", a1, w2, dims=(t, d, dff), tm=tm2, tn=512, tk=dff, out_dtypes=[F32, F32],
                      name=f"mlp2_{l}", extras=res_extras(x2, gt2, tm2), epi=residual, pro_a=_relu2)
        x3 = x3.reshape(bl, s, d)
        wfull[l] = (win, wout, w1, w2)
        saved.append((xcur, h1, p, xc, states, mix, mm1.reshape(bl, s, d), x2, h2, a1, mm2.reshape(bl, s, d),
                      prm_a, prm_b, prm_c))
        xcur = x3

    dx, d_final_g, loss_part = loss_head(xcur, row(final_norm_g), loss_target, "loss_head")
    loss = lax.psum(loss_part[0, 0], ("x", "y", "c"))

    def rs_swap(grads, tag):
        ss, rs, srcs, lands, token = split_copy_start(
            grads, [jax.ShapeDtypeStruct((4, g.shape[1] // 2, g.shape[2]), F32) for g in grads],
            _swap_copies, None, f"rs_swap_{tag}")
        return (ss, rs, srcs, lands), token

    def rs_begin(swap_state, tag, swapped_behind, start_behind=None):
        ss, rs, srcs, lands = swap_state
        grads, theirs = split_copy_wait(ss, rs, srcs, lands, _swap_copies, swapped_behind, f"rs_swapped_{tag}")
        sums = [add_pair(g, th, chip, ci, f"rs_add_{tag}_{i}") for i, (g, th) in enumerate(zip(grads, theirs))]
        ss, rs, srcs, lands, token = split_copy_start(
            [sm[1] for sm in sums], [jax.ShapeDtypeStruct((3,) + sm[1].shape[1:], BF16) for sm in sums],
            _scatter_copies, sums[0][0] if start_behind is None else start_behind, f"rs_start_{tag}")
        return (ss, rs, srcs, lands, [sm[0] for sm in sums]), token

    def rs_end(state, behind, tag):
        ss, rs, srcs, lands, sums_f32 = state
        _, got = split_copy_wait(ss, rs, srcs, lands, _scatter_copies, behind, f"rs_wait_{tag}")
        halves = [sum_own_recv(sf, g, ci, f"rs_sum_{tag}_{i}") for i, (sf, g) in enumerate(zip(sums_f32, got))]
        return join_halves(halves, f"rs_join_{tag}")

    small_parts = [None] * nl
    dmods = [None] * nl
    reduced = [[None] * 4 for _ in range(nl)]
    pending_rs, rs_token = [], None
    part_slots = {"a": (0, 1), "m": (2, 3)}

    def finish(behind):
        for ll, part, state in pending_rs:
            for slot, blk in zip(part_slots[part], rs_end(state, behind, f"{ll}{part}")):
                reduced[ll][slot] = blk
        pending_rs.clear()

    for l in reversed(range(nl)):
        sh1, sc1, gt1, sh2, sc2, gt2 = mods[l]
        win, wout, w1, w2 = wfull[l]
        xin, h1, p, xc, states, mix, mm1, x2, h2, a1, mm2, prm_a, prm_b, prm_c = saved[l]
        if rs_token is not None:
            gt2 = gt2 + rs_token[0, 0]
        dm2, dgt2 = gate_bwd(dx, mm2, gt2, f"gate2_bwd_{l}")
        dm2 = dm2.reshape(t, d)
        da1 = _mm("nt", dm2, w2, dims=(t, dff, d), tm=1024, tn=2048, tk=d, out_dtypes=[BF16], name=f"mlp2_dx_{l}",
                  extras=[(a1, pl.BlockSpec((1024 if t >= 1024 else t, 2048), lambda i, j, kk: (i, j)))],
                  epi=lambda acc, av: (acc * (2.0 * jnp.maximum(av, 0).astype(F32)),))[0]
        dw2 = _mm("tn", a1, dm2, dims=(dff, d, t), tm=512, tn=d, tk=2048, out_dtypes=[F32], name=f"mlp2_dw_{l}",
                  pro_a=_relu2, out_shapes=[(4, dff4, d)],
                  out_specs=[pl.BlockSpec((None, 512, d), lambda i, j, kk: (i // (dff4 // 512), i % (dff4 // 512), 0))])[0]
        dw1 = _mm("tn", h2.reshape(t, d), da1, dims=(d, dff, t), tm=512, tn=dff4, tk=2048, out_dtypes=[F32],
                  name=f"mlp1_dw_{l}", out_shapes=[(4, d, dff4)],
                  out_specs=[pl.BlockSpec((None, 512, dff4), lambda i, j, kk: (j, i, 0))])[0]
        swap_state, swap_token = rs_swap([dw1, dw2], f"{l}m")
        dh2 = _mm_nt_blocked(da1, w1, tm=512, tn=512, name=f"mlp1_dx_{l}", behind=swap_token)
        mlp_state, mlp_token = rs_begin(swap_state, f"{l}m", dh2)
        sc2 = sc2 + mlp_token[0, 0]
        dx2, dsc2, dsh2, dn2 = ln_mod_bwd(dh2.reshape(bl, s, d), x2, dx, row(norm2_g[l]), sc2, f"ln2_bwd_{l}")
        dm1, dgt1 = gate_bwd(dx2, mm1, gt1, f"gate1_bwd_{l}")
        dm1 = dm1.reshape(t, d)
        dmix = _mm("nt", dm1, wout, dims=(t, d, d), tm=1024, tn=2048, tk=d, out_dtypes=[F32],
                   name=f"proj_out_dx_{l}")[0].reshape(bl, s, d)
        dwout = _mm("tn", mix.reshape(t, d), dm1, dims=(d, d, t), tm=512, tn=d, tk=2048, out_dtypes=[F32],
                    name=f"proj_out_dw_{l}", out_shapes=[(4, d // 4, d)],
                    out_specs=[pl.BlockSpec((None, 512, d), lambda i, j, kk: (i // (d // 4 // 512), i % (d // 4 // 512), 0))])[0]
        du, dv, dlg, dlb, dws, dbst, dgng = gmlp_bwd(p, dmix, prm_a, f"gmlp_bwd_{l}")
        dq, dk, dvv, dsinks, dang = attn_bwd(p, dmix, *prm_b, f"attn_bwd_{l}")
        dxc, ddt, dz, ddtb, dalog, ddsk, dsng = ssd_bwd(xc, p, states, dmix, prm_c, f"ssd_bwd_{l}")
        dxbc, dcw, dcb = conv_bwd(p, dxc, conv_w_full[l], row(conv_b[l]), f"conv_bwd_{l}")
        dp = jnp.concatenate([du, dv, dq, dk, dvv, dz, dxbc, ddt, jnp.zeros((bl, s, PW - OFF_DT - LANES), BF16)],
                             axis=-1).reshape(t, PW)
        dwin = _mm("tn", h1.reshape(t, d), dp, dims=(d, PW, t), tm=512, tn=PW // 2, tk=2048, out_dtypes=[F32],
                   name=f"proj_in_dw_{l}")[0]
        dwin_blocks = _win_to_blocks(dwin)
        mixer_swap, swap_token = rs_swap([dwin_blocks, dwout], f"{l}a")
        dh1 = _mm("nt", dp, win, dims=(t, d, PW), tm=1024, tn=512, tk=PW, out_dtypes=[F32],
                  name=f"proj_in_dx_{l}", behind=swap_token)[0]
        dx, dsc1, dsh1, dn1 = ln_mod_bwd(dh1.reshape(bl, s, d), xin, dx2, row(norm1_g[l]), sc1, f"ln1_bwd_{l}")
        dmods[l] = jnp.concatenate([dsh1, dsc1, dgt1, dsh2, dsc2, dgt2], axis=-1).reshape(bl, 6 * d)
        small_parts[l] = [dn1, dlg, dlb, dws, dbst.T, dgng, dsinks, dang, dcw, dcb, ddtb[:, :SSM_HEADS],
                          dalog[:, :SSM_HEADS], ddsk[:, :SSM_HEADS], dsng, dn2]
        finish(dx)
        pending_rs.append((l, "m", mlp_state))
        if l > 0:
            state, rs_token = rs_begin(mixer_swap, f"{l}a", dx)
            pending_rs.append((l, "a", state))
    grad_x = dx

    big = [(w_in, m_w_in, v_w_in), (w_out, m_w_out, v_w_out), (w_mlp1, m_w_mlp1, v_w_mlp1), (w_mlp2, m_w_mlp2, v_w_mlp2)]
    big_out = [None] * 4
    for l in reversed(range(1, nl)):
        for i, (wt, mt, vt) in enumerate(big):
            if i > 0:
                big_out[i] = adamw_layer(wt, mt, vt, reduced[l][i], l, big_out[i], f"adamw_big_{i}_{l}")

    small_names = [norm1_g, gm_ln_g, gm_ln_b, gm_ws, gm_bs, gm_norm_g, attn_sinks, attn_norm_g, None, conv_b, dt_bias,
                   a_log, d_skip, ssm_norm_g, norm2_g]
    n_small = len(small_names)
    per_param = [jnp.stack([small_parts[l][i].reshape(-1) for l in range(nl)]) for i in range(n_small)]
    small_vec = _pack(per_param + [d_final_g])
    rs_small = small_vec.shape[0]
    dmod_local = jnp.stack(dmods, axis=1)
    g2 = all_gather_small(jnp.concatenate([small_vec, _pack([dmod_local])], axis=0), "ag_small")
    state, rs_token = rs_begin(mixer_swap, "0a", grad_x, start_behind=g2)
    pending_rs.append((0, "a", state))
    g2 = g2 + rs_token[0, 0]
    g_small = sum_devices(g2[:, :rs_small, :], "sum_small")
    dmod_all = g2[:, rs_small:, :].reshape(8, -1)[:, :bl * nl * 6 * d].reshape(nex, nl * 6 * d)
    g_ada_b = sum_devices(dmod_all.reshape(nex, -1, LANES), "sum_ada_b").reshape(nl, 6 * d)
    shapes = [(nl, int(np.prod(small_parts[0][i].shape))) for i in range(n_small)] + [(d,)]
    g_list = _unpack(g_small, shapes)
    g_conv_w = lax.dynamic_slice(g_list[8].reshape(nl, CONV_K, CCH), (0, 0, chip * cw_w), (nl, CONV_K, cw_w))

    dm_cols = lax.dynamic_slice(dmod_all.reshape(nex, nl, 6 * d), (0, 0, chip * mod_w), (nex, nl, mod_w))
    g_ada_w = _mm("tn", c_all, dm_cols.reshape(nex, nl * mod_w), dims=(d, nl * mod_w, nex), tm=512, tn=512, tk=nex,
                  out_dtypes=[F32], name="ada_w_grad", pro_a=c_act, pro_b=to_bf16, out_shapes=[(nl, d, mod_w)],
                  out_specs=[pl.BlockSpec((None, 512, 512), lambda i, j, kk: (j // (mod_w // 512), i, j % (mod_w // 512)))])[0]
    d_ada_w, m_ada_w_n, v_ada_w_n = [a.reshape(ada_w.shape) for a in
                                     adamw(_rows2d(ada_w), _rows2d(m_ada_w), _rows2d(v_ada_w), _rows2d(g_ada_w), "adamw_ada_w")]

    smalls = {
        "ada_b": (ada_b, m_ada_b, v_ada_b, g_ada_b), "norm1_g": (norm1_g, m_norm1_g, v_norm1_g, g_list[0]),
        "gm_ln_g": (gm_ln_g, m_gm_ln_g, v_gm_ln_g, g_list[1]), "gm_ln_b": (gm_ln_b, m_gm_ln_b, v_gm_ln_b, g_list[2]),
        "gm_ws": (gm_ws, m_gm_ws, v_gm_ws, g_list[3]), "gm_bs": (gm_bs, m_gm_bs, v_gm_bs, g_list[4]),
        "gm_norm_g": (gm_norm_g, m_gm_norm_g, v_gm_norm_g, g_list[5]),
        "attn_sinks": (attn_sinks, m_attn_sinks, v_attn_sinks, g_list[6]),
        "attn_norm_g": (attn_norm_g, m_attn_norm_g, v_attn_norm_g, g_list[7]),
        "conv_w": (conv_w, m_conv_w, v_conv_w, g_conv_w), "conv_b": (conv_b, m_conv_b, v_conv_b, g_list[9]),
        "dt_bias": (dt_bias, m_dt_bias, v_dt_bias, g_list[10]), "a_log": (a_log, m_a_log, v_a_log, g_list[11]),
        "d_skip": (d_skip, m_d_skip, v_d_skip, g_list[12]),
        "ssm_norm_g": (ssm_norm_g, m_ssm_norm_g, v_ssm_norm_g, g_list[13]),
        "norm2_g": (norm2_g, m_norm2_g, v_norm2_g, g_list[14]),
        "final_norm_g": (final_norm_g, m_final_norm_g, v_final_norm_g, g_list[15]),
    }
    keys = list(smalls)
    wv, mv, vv_, gv = [_pack([smalls[k][i].reshape(smalls[k][0].shape) for k in keys]) for i in range(4)]
    sd_, sm_, sv_ = adamw(wv, mv, vv_, gv, "adamw_small")
    shp = [smalls[k][0].shape for k in keys]
    small_out = {k: (smalls[k][3].reshape(smalls[k][0].shape), a, b, cc)
                 for k, a, b, cc in zip(keys, _unpack(sd_, shp), _unpack(sm_, shp), _unpack(sv_, shp))}

    late = jnp.zeros((8, LANES), F32) + (sv_[0, 0] + v_ada_w_n[0, 0, 0])
    for bo in big_out:
        if bo is not None:
            late = late + bo[3][nl - 1, 0, 0]
    finish(late)
    for i, (wt, mt, vt) in enumerate(big):
        if i > 0:
            big_out[i] = adamw_layer(wt, mt, vt, reduced[0][i], 0, big_out[i], f"adamw_big_{i}_0")
    minor_first = lambda a: jnp.transpose(a, (2, 0, 1))
    g_in = jnp.stack([reduced[l][0].T for l in range(nl)], axis=1)
    back = lambda a: jnp.transpose(a, (1, 2, 0))
    big_out[0] = [back(a) for a in [g_in, *adamw_minor_rows(minor_first(w_in), minor_first(m_w_in),
                                                            minor_first(v_w_in), g_in, "adamw_w_in")]]

    out = {"ada_w": (g_ada_w, d_ada_w, m_ada_w_n, v_ada_w_n), "w_in": big_out[0], "w_out": big_out[1],
           "w_mlp1": big_out[2], "w_mlp2": big_out[3], **small_out}
    order = ["ada_w", "ada_b", "norm1_g", "w_in", "gm_ln_g", "gm_ln_b", "gm_ws", "gm_bs", "gm_norm_g", "attn_sinks",
             "attn_norm_g", "conv_w", "conv_b", "dt_bias", "a_log", "d_skip", "ssm_norm_g", "w_out", "norm2_g",
             "w_mlp1", "w_mlp2", "final_norm_g"]
    return (loss, grad_x, *[out[k][0] for k in order], *[out[k][1] for k in order],
            *[out[k][2] for k in order], *[out[k][3] for k in order])
```

```python
import functools
import math

import jax
import jax.numpy as jnp
import numpy as np
from jax import lax
from jax.experimental import pallas as pl
from jax.experimental.pallas import tpu as pltpu

F32 = jnp.float32
BF16 = jnp.bfloat16
HI = lax.Precision.HIGHEST
MESH = pl.DeviceIdType.MESH

CHUNK = 128
GM_HEADS, GM_HD = 4, 128
ATT_HEADS, ATT_KV, ATT_HD = 8, 2, 64
WINDOW = 128
SSM_HEADS, SSM_HD, SSM_GROUPS, SSM_STATE, CONV_K = 16, 64, 2, 128, 4
EPS = 1e-6
LN_EPS = 1e-5
NEG = -1e30
LANES = 128

GMW = GM_HEADS * GM_HD
ATW = ATT_HEADS * ATT_HD
KVW = ATT_KV * ATT_HD
SSW = SSM_HEADS * SSM_HD
BCW = SSM_GROUPS * SSM_STATE
CCH = SSW + 2 * BCW
GRW = SSW // SSM_GROUPS
IN_SIZES = (GMW, GMW, ATW, KVW, KVW, SSW, CCH, SSM_HEADS)
IN_W = sum(IN_SIZES)
OFF_U, OFF_V, OFF_Q, OFF_K, OFF_VV, OFF_Z, OFF_XBC, OFF_DT = 0, 512, 1024, 1536, 1664, 1792, 2816, 4352
ZB = 256
PW = 4608

ADAM_LR, ADAM_B1, ADAM_B2, ADAM_EPS, ADAM_WD, ADAM_STEP = 0.001, 0.9, 0.999, 1e-08, 0.01, 10

VMEM_LIMIT = 56 * 1024 * 1024


def _cp(sem=None):
    return pltpu.CompilerParams(dimension_semantics=sem, vmem_limit_bytes=VMEM_LIMIT)


_DN = {"nn": (((1,), (0,)), ((), ())), "nt": (((1,), (1,)), ((), ())), "tn": (((0,), (0,)), ((), ()))}


def _dot(form, a, b):
    return lax.dot_general(a.astype(BF16), b.astype(BF16), _DN[form], preferred_element_type=F32)


@jax.custom_vjp
def _nn(a, b):
    return _dot("nn", a, b)


@jax.custom_vjp
def _nt(a, b):
    return _dot("nt", a, b)


@jax.custom_vjp
def _tn(a, b):
    return _dot("tn", a, b)


_nn.defvjp(lambda a, b: (_dot("nn", a, b), (a, b)), lambda r, g: (_dot("nt", g, r[1]), _dot("tn", r[0], g)))
_nt.defvjp(lambda a, b: (_dot("nt", a, b), (a, b)), lambda r, g: (_dot("nn", g, r[1]), _dot("tn", g, r[0])))
_tn.defvjp(lambda a, b: (_dot("tn", a, b), (a, b)), lambda r, g: (_dot("nt", r[1], g), _dot("nn", r[0], g)))


def _hdot(a, b):
    return jnp.dot(a, b, precision=HI, preferred_element_type=F32)


def _silu(x):
    return x * (1.0 / (1.0 + jnp.exp(-x)))


def _softplus(x):
    return jnp.maximum(x, 0.0) + jnp.log1p(jnp.exp(-jnp.abs(x)))


def _gelu(x):
    return 0.5 * x * (1.0 + jnp.tanh(math.sqrt(2.0 / math.pi) * (x + 0.044715 * (x * x * x))))


def _rms(y, g):
    return y * lax.rsqrt(jnp.mean(y * y, axis=-1, keepdims=True) + EPS) * g


def _mm(form, a, b, *, dims, tm, tn, tk, out_dtypes, name, a_spec=None, b_spec=None, out_specs=None,
        out_shapes=None, extras=(), epi=None, pro_a=None, pro_b=None, behind=None):
    m, n, k = dims
    tm, tn, tk = min(tm, m), min(tn, n), min(tk, k)
    assert m % tm == 0 and n % tn == 0 and k % tk == 0, (name, dims, tm, tn, tk)
    nk = k // tk
    if a_spec is None:
        a_spec = (pl.BlockSpec((tk, tm), lambda i, j, kk: (kk, i)) if form == "tn"
                  else pl.BlockSpec((tm, tk), lambda i, j, kk: (i, kk)))
    if b_spec is None:
        b_spec = (pl.BlockSpec((tn, tk), lambda i, j, kk: (j, kk)) if form == "nt"
                  else pl.BlockSpec((tk, tn), lambda i, j, kk: (kk, j)))
    n_out = len(out_dtypes)
    if out_specs is None:
        out_specs = [pl.BlockSpec((tm, tn), lambda i, j, kk: (i, j))] * n_out
    if out_shapes is None:
        out_shapes = [(m, n)] * n_out
    ne = len(extras)
    n_behind = 0 if behind is None else 1

    def body(*refs):
        a_ref, b_ref = refs[0], refs[1]
        ex = refs[2:2 + ne]
        outs = refs[2 + ne + n_behind:2 + ne + n_behind + n_out]

        def write(val):
            res = epi(val, *[e[...] for e in ex]) if epi is not None else (val,)
            for o, r in zip(outs, res):
                o[...] = r.astype(o.dtype)

        av = a_ref[...]
        if pro_a is not None:
            av = pro_a(av)
        bv = b_ref[...]
        if pro_b is not None:
            bv = pro_b(bv)
        part = lax.dot_general(av, bv, _DN[form], preferred_element_type=F32)
        if nk == 1:
            write(part)
        else:
            acc = refs[-1]
            kk = pl.program_id(2)

            @pl.when(kk == 0)
            def _():
                acc[...] = part

            @pl.when(kk > 0)
            def _():
                acc[...] += part

            @pl.when(kk == nk - 1)
            def _():
                write(acc[...])

    res = pl.pallas_call(
        body, name=name, grid=(m // tm, n // tn, nk),
        in_specs=[a_spec, b_spec] + [s for _, s in extras] + [_ANY] * n_behind,
        out_specs=out_specs,
        out_shape=[jax.ShapeDtypeStruct(s, d) for s, d in zip(out_shapes, out_dtypes)],
        scratch_shapes=[pltpu.VMEM((tm, tn), F32)] if nk > 1 else [],
        compiler_params=_cp(("parallel", "parallel", "arbitrary")),
    )(a, b, *[e for e, _ in extras], *([behind] if n_behind else []))
    return res


def _mm_nt_blocked(a, b, *, tm, tn, name, behind=None):
    m = a.shape[0]
    nparts, n, f = b.shape
    tm, tn = min(tm, m), min(tn, n)
    n_behind = 0 if behind is None else 1

    def body(a_ref, *rest):
        b_refs, o_ref = rest[:nparts], rest[nparts + n_behind]
        acc = None
        for k in range(nparts):
            part = lax.dot_general(a_ref[:, k * f:(k + 1) * f], b_refs[k][...], _DN["nt"], preferred_element_type=F32)
            acc = part if acc is None else acc + part
        o_ref[...] = acc

    return pl.pallas_call(
        body, name=name, grid=(m // tm, n // tn),
        in_specs=[pl.BlockSpec((tm, nparts * f), lambda i, j: (i, 0))]
        + [pl.BlockSpec((None, tn, f), lambda i, j, k=k: (k, j, 0)) for k in range(nparts)] + [_ANY] * n_behind,
        out_specs=pl.BlockSpec((tm, tn), lambda i, j: (i, j)),
        out_shape=jax.ShapeDtypeStruct((m, n), F32),
        compiler_params=_cp(("parallel", "parallel")),
    )(a, *([b] * nparts), *([behind] if n_behind else []))


def _row_tile(s):
    return min(512, s)


def ln_mod_fwd(x, g, sc, sh, name):
    bsz, s, d = x.shape
    ts = _row_tile(s)

    def body(x_ref, g_ref, sc_ref, sh_ref, o_ref):
        xv = x_ref[...]
        r = lax.rsqrt(jnp.mean(xv * xv, axis=-1, keepdims=True) + EPS)
        o_ref[...] = ((xv * r * g_ref[...]) * (1.0 + sc_ref[...]) + sh_ref[...]).astype(o_ref.dtype)

    row = pl.BlockSpec((None, ts, d), lambda b, i: (b, i, 0))
    vec = pl.BlockSpec((None, 1, d), lambda b, i: (b, 0, 0))
    return pl.pallas_call(
        body, name=name, grid=(bsz, s // ts),
        in_specs=[row, pl.BlockSpec((1, d), lambda b, i: (0, 0)), vec, vec],
        out_specs=row, out_shape=jax.ShapeDtypeStruct(x.shape, BF16),
        compiler_params=_cp(("parallel", "parallel")),
    )(x, g, sc, sh)


def ln_mod_bwd(dh, x, dres, g, sc, name):
    bsz, s, d = x.shape
    ts = _row_tile(s)

    def body(dh_ref, x_ref, dres_ref, g_ref, sc_ref, dx_ref, dsc_ref, dsh_ref, dg_ref):
        b, i = pl.program_id(0), pl.program_id(1)
        xv, dhv, gv = x_ref[...], dh_ref[...], g_ref[...]
        r = lax.rsqrt(jnp.mean(xv * xv, axis=-1, keepdims=True) + EPS)
        xn = xv * r
        a = dhv * (1.0 + sc_ref[...])
        dxn = a * gv
        dx_ref[...] = dres_ref[...] + r * (dxn - xn * jnp.mean(dxn * xn, axis=-1, keepdims=True))
        p_sc = jnp.sum(dhv * (xn * gv), axis=0, keepdims=True)
        p_sh = jnp.sum(dhv, axis=0, keepdims=True)
        p_g = jnp.sum(a * xn, axis=0, keepdims=True)

        @pl.when(i == 0)
        def _():
            dsc_ref[...] = p_sc
            dsh_ref[...] = p_sh

        @pl.when(i > 0)
        def _():
            dsc_ref[...] += p_sc
            dsh_ref[...] += p_sh

        @pl.when((i == 0) & (b == 0))
        def _():
            dg_ref[...] = p_g

        @pl.when((i > 0) | (b > 0))
        def _():
            dg_ref[...] += p_g

    row = pl.BlockSpec((None, ts, d), lambda b, i: (b, i, 0))
    vec = pl.BlockSpec((None, 1, d), lambda b, i: (b, 0, 0))
    one = pl.BlockSpec((1, d), lambda b, i: (0, 0))
    return pl.pallas_call(
        body, name=name, grid=(bsz, s // ts),
        in_specs=[row, row, row, one, vec],
        out_specs=[row, vec, vec, one],
        out_shape=[jax.ShapeDtypeStruct(x.shape, F32), jax.ShapeDtypeStruct((bsz, 1, d), F32),
                   jax.ShapeDtypeStruct((bsz, 1, d), F32), jax.ShapeDtypeStruct((1, d), F32)],
        compiler_params=_cp(("arbitrary", "arbitrary")),
    )(dh, x, dres, g, sc)


def gate_bwd(dx, mm, gate, name):
    bsz, s, d = dx.shape
    ts = _row_tile(s)

    def body(dx_ref, m_ref, g_ref, dm_ref, dg_ref):
        i = pl.program_id(1)
        dxv = dx_ref[...]
        dm_ref[...] = (dxv * g_ref[...]).astype(dm_ref.dtype)
        p = jnp.sum(dxv * m_ref[...], axis=0, keepdims=True)

        @pl.when(i == 0)
        def _():
            dg_ref[...] = p

        @pl.when(i > 0)
        def _():
            dg_ref[...] += p

    row = pl.BlockSpec((None, ts, d), lambda b, i: (b, i, 0))
    vec = pl.BlockSpec((None, 1, d), lambda b, i: (b, 0, 0))
    return pl.pallas_call(
        body, name=name, grid=(bsz, s // ts),
        in_specs=[row, row, vec], out_specs=[row, vec],
        out_shape=[jax.ShapeDtypeStruct(dx.shape, BF16), jax.ShapeDtypeStruct((bsz, 1, d), F32)],
        compiler_params=_cp(("parallel", "arbitrary")),
    )(dx, mm, gate)


def loss_head(x, g, tgt, name):
    bsz, s, d = x.shape
    ts = _row_tile(s)

    def body(x_ref, g_ref, t_ref, dx_ref, dg_ref, l_ref):
        b, i = pl.program_id(0), pl.program_id(1)
        xv, gv = x_ref[...], g_ref[...]
        r = lax.rsqrt(jnp.mean(xv * xv, axis=-1, keepdims=True) + EPS)
        xn = xv * r
        e = xn * gv - t_ref[...]
        dy = e * (1.0 / d)
        dxn = dy * gv
        dx_ref[...] = r * (dxn - xn * jnp.mean(dxn * xn, axis=-1, keepdims=True))
        p_g = jnp.sum(dy * xn, axis=0, keepdims=True)
        p_l = jnp.zeros((1, LANES), F32) + jnp.sum(e * e) * (0.5 / d)
        first = (i == 0) & (b == 0)

        @pl.when(first)
        def _():
            dg_ref[...] = p_g
            l_ref[...] = p_l

        @pl.when(jnp.logical_not(first))
        def _():
            dg_ref[...] += p_g
            l_ref[...] += p_l

    row = pl.BlockSpec((None, ts, d), lambda b, i: (b, i, 0))
    one = pl.BlockSpec((1, d), lambda b, i: (0, 0))
    return pl.pallas_call(
        body, name=name, grid=(bsz, s // ts),
        in_specs=[row, one, row],
        out_specs=[row, one, pl.BlockSpec((1, LANES), lambda b, i: (0, 0))],
        out_shape=[jax.ShapeDtypeStruct(x.shape, F32), jax.ShapeDtypeStruct((1, d), F32),
                   jax.ShapeDtypeStruct((1, LANES), F32)],
        compiler_params=_cp(("arbitrary", "arbitrary")),
    )(x, g, tgt)


def _gmlp_chunk(u_raw, v_raw, ln_g, ln_b, w, bs_t, out_g):
    c = u_raw.shape[0]
    u, v = _gelu(u_raw), _gelu(v_raw)
    tril = lax.broadcasted_iota(jnp.int32, (c, c), 0) >= lax.broadcasted_iota(jnp.int32, (c, c), 1)
    ys = []
    for h in range(GM_HEADS):
        sl = slice(h * GM_HD, (h + 1) * GM_HD)
        vh = v[:, sl]
        xc = vh - jnp.mean(vh, axis=-1, keepdims=True)
        vn = xc * lax.rsqrt(jnp.mean(xc * xc, axis=-1, keepdims=True) + LN_EPS) * ln_g[:, sl] + ln_b[:, sl]
        gate = _nn(jnp.where(tril, w[h], 0.0), vn) + bs_t[:, h:h + 1]
        ys.append(u[:, sl] * gate)
    return _rms(jnp.concatenate(ys, axis=1), out_g)


def _gmlp_specs(bsz, nc):
    seg = lambda off: pl.BlockSpec((None, CHUNK, GMW), lambda b, c: (b, c, off // GMW))
    full = lambda shape: pl.BlockSpec(shape, lambda b, c: (0,) * len(shape))
    par = [full((1, GMW)), full((1, GMW)), full((GM_HEADS, CHUNK, CHUNK)), full((CHUNK, GM_HEADS)), full((1, GMW))]
    return seg, full, par


def gmlp_fwd(p, prm, name):
    bsz, s, _ = p.shape
    nc = s // CHUNK
    seg, _, par = _gmlp_specs(bsz, nc)

    def body(u_ref, v_ref, lg, lb, w, bt, og, o_ref):
        o_ref[...] = _gmlp_chunk(u_ref[...], v_ref[...], lg[...], lb[...], w[...], bt[...], og[...]).astype(o_ref.dtype)

    return pl.pallas_call(
        body, name=name, grid=(bsz, nc),
        in_specs=[seg(OFF_U), seg(OFF_V)] + par,
        out_specs=pl.BlockSpec((None, CHUNK, GMW), lambda b, c: (b, c, 0)),
        out_shape=jax.ShapeDtypeStruct((bsz, s, GMW), BF16),
        compiler_params=_cp(("parallel", "parallel")),
    )(p, p, *prm)


def _accumulate(first, refs, vals):
    @pl.when(first)
    def _():
        for r, v in zip(refs, vals):
            r[...] = v

    @pl.when(jnp.logical_not(first))
    def _():
        for r, v in zip(refs, vals):
            r[...] += v


def gmlp_bwd(p, dmix, prm, name):
    bsz, s, _ = p.shape
    nc = s // CHUNK
    seg, full, par = _gmlp_specs(bsz, nc)

    def body(u_ref, v_ref, do_ref, lg, lb, w, bt, og, du_ref, dv_ref, *dpar):
        first = (pl.program_id(0) == 0) & (pl.program_id(1) == 0)
        _, vjp = jax.vjp(_gmlp_chunk, u_ref[...], v_ref[...], lg[...], lb[...], w[...], bt[...], og[...])
        gr = vjp(do_ref[...])
        du_ref[...] = gr[0].astype(du_ref.dtype)
        dv_ref[...] = gr[1].astype(dv_ref.dtype)
        _accumulate(first, dpar, gr[2:])

    out_seg = pl.BlockSpec((None, CHUNK, GMW), lambda b, c: (b, c, 0))
    return pl.pallas_call(
        body, name=name, grid=(bsz, nc),
        in_specs=[seg(OFF_U), seg(OFF_V), out_seg] + par,
        out_specs=[out_seg, out_seg] + par,
        out_shape=[jax.ShapeDtypeStruct((bsz, s, GMW), BF16)] * 2 + [jax.ShapeDtypeStruct(x.shape, F32) for x in prm],
        compiler_params=_cp(("arbitrary", "arbitrary")),
    )(p, p, dmix, *prm)


def _attn_block(q, kp, kc, vp, vc, sinks, out_g, has_prev):
    w = q.shape[0]
    k2 = jnp.concatenate([kp, kc], axis=0)
    v2 = jnp.concatenate([vp, vc], axis=0)
    qi = lax.broadcasted_iota(jnp.int32, (w, 2 * w), 0)
    kj = lax.broadcasted_iota(jnp.int32, (w, 2 * w), 1)
    diff = qi + w - kj
    grp = ATT_HEADS // ATT_KV
    valid = (diff >= 0) & (diff < w) & ((kj >= w) | has_prev)
    valid = jnp.concatenate([valid] * grp, axis=0)
    outs = []
    for kv in range(ATT_KV):
        kh = k2[:, kv * ATT_HD:(kv + 1) * ATT_HD]
        vh = v2[:, kv * ATT_HD:(kv + 1) * ATT_HD]
        heads = range(kv * grp, (kv + 1) * grp)
        qs = jnp.concatenate([q[:, h * ATT_HD:(h + 1) * ATT_HD] for h in heads], axis=0)
        sink = jnp.concatenate([jnp.broadcast_to(sinks[:, h:h + 1], (w, 1)) for h in heads], axis=0)
        sc = jnp.where(valid, _nt(qs, kh) * (ATT_HD ** -0.5), NEG)
        m = jnp.maximum(jnp.max(sc, axis=-1, keepdims=True), sink)
        e = jnp.exp(sc - m)
        pr = e / (jnp.sum(e, axis=-1, keepdims=True) + jnp.exp(sink - m))
        o = _nn(pr, vh)
        outs += [o[gi * w:(gi + 1) * w] for gi in range(grp)]
    return _rms(jnp.concatenate(outs, axis=1), out_g)


ATT_QB_FWD, ATT_QB_BWD = 8, 4


def _attn_tiles(s, windows=ATT_QB_BWD):
    qb = min(windows, s // WINDOW)
    return qb, qb * WINDOW, s // (qb * WINDOW)


def attn_fwd(p, sinks, out_g, name):
    bsz, s, _ = p.shape
    qb, rows, steps = _attn_tiles(s, ATT_QB_FWD)

    def body(q_ref, kp_ref, kc_ref, vp_ref, vc_ref, s_ref, g_ref, o_ref):
        n = pl.program_id(1)
        for w in range(qb):
            sl = pl.ds(w * WINDOW, WINDOW)
            before = pl.ds((w - 1) * WINDOW, WINDOW)
            kp = kp_ref[...] if w == 0 else kc_ref[before, :]
            vp = vp_ref[...] if w == 0 else vc_ref[before, :]
            o_ref[sl, :] = _attn_block(q_ref[sl, :], kp, kc_ref[sl, :], vp, vc_ref[sl, :], s_ref[...], g_ref[...],
                                       (n > 0) if w == 0 else True).astype(o_ref.dtype)

    cur = lambda off: pl.BlockSpec((None, rows, KVW), lambda b, n: (b, n, off // KVW))
    prev = lambda off: pl.BlockSpec((None, WINDOW, KVW), lambda b, n: (b, jnp.maximum(n * qb - 1, 0), off // KVW))
    return pl.pallas_call(
        body, name=name, grid=(bsz, steps),
        in_specs=[pl.BlockSpec((None, rows, ATW), lambda b, n: (b, n, OFF_Q // ATW)),
                  prev(OFF_K), cur(OFF_K), prev(OFF_VV), cur(OFF_VV),
                  pl.BlockSpec((1, ATT_HEADS), lambda b, n: (0, 0)), pl.BlockSpec((1, ATW), lambda b, n: (0, 0))],
        out_specs=pl.BlockSpec((None, rows, ATW), lambda b, n: (b, n, 0)),
        out_shape=jax.ShapeDtypeStruct((bsz, s, ATW), BF16),
        compiler_params=_cp(("parallel", "parallel")),
    )(p, p, p, p, p, sinks, out_g)


def attn_bwd(p, dmix, sinks, out_g, name):
    bsz, s, _ = p.shape
    qb, rows, steps = _attn_tiles(s)
    last = pl.ds(rows - WINDOW, WINDOW)

    def body(q_ref, kp_ref, kc_ref, vp_ref, vc_ref, do_ref, s_ref, g_ref,
             dq_ref, dk_ref, dv_ref, ds_ref, dg_ref, ck, cv):
        b, n = pl.program_id(0), pl.program_id(1)

        @pl.when(n == 0)
        def _():
            ck[...] = jnp.zeros_like(ck)
            cv[...] = jnp.zeros_like(cv)

        @pl.when(n < steps)
        def _():
            grads = []
            for w in range(qb):
                sl = pl.ds(w * WINDOW, WINDOW)
                before = pl.ds((w - 1) * WINDOW, WINDOW)
                kp = kp_ref[...] if w == 0 else kc_ref[before, :]
                vp = vp_ref[...] if w == 0 else vc_ref[before, :]
                fn = functools.partial(_attn_block, has_prev=(n > 0) if w == 0 else True)
                _, vjp = jax.vjp(fn, q_ref[sl, :], kp, kc_ref[sl, :], vp, vc_ref[sl, :], s_ref[...], g_ref[...])
                grads.append(vjp(do_ref[sl, :]))
                dq_ref[sl, :] = grads[-1][0].astype(dq_ref.dtype)
            dk_ref[...] = ck[...].astype(dk_ref.dtype)
            dv_ref[...] = cv[...].astype(dv_ref.dtype)
            dk_ref[last, :] = (ck[last, :] + grads[0][1]).astype(dk_ref.dtype)
            dv_ref[last, :] = (cv[last, :] + grads[0][3]).astype(dv_ref.dtype)
            for w in range(qb):
                sl = pl.ds(w * WINDOW, WINDOW)
                ck[sl, :] = grads[w][2] + (grads[w + 1][1] if w + 1 < qb else 0.0)
                cv[sl, :] = grads[w][4] + (grads[w + 1][3] if w + 1 < qb else 0.0)
            dsk = functools.reduce(lambda u, v: u + v, [g[5] for g in grads])
            dgg = functools.reduce(lambda u, v: u + v, [g[6] for g in grads])
            _accumulate((b == 0) & (n == 0), (ds_ref, dg_ref), (dsk, dgg))

        @pl.when(n == steps)
        def _():
            dk_ref[...] = ck[...].astype(dk_ref.dtype)
            dv_ref[...] = cv[...].astype(dv_ref.dtype)

    at = lambda n: jnp.minimum(n, steps - 1)
    cur = lambda off: pl.BlockSpec((None, rows, KVW), lambda b, n: (b, at(n), off // KVW))
    prev = lambda off: pl.BlockSpec((None, WINDOW, KVW), lambda b, n: (b, jnp.maximum(at(n) * qb - 1, 0), off // KVW))
    kv_out = pl.BlockSpec((None, rows, KVW), lambda b, n: (b, jnp.maximum(n - 1, 0), 0))
    return pl.pallas_call(
        body, name=name, grid=(bsz, steps + 1),
        in_specs=[pl.BlockSpec((None, rows, ATW), lambda b, n: (b, at(n), OFF_Q // ATW)),
                  prev(OFF_K), cur(OFF_K), prev(OFF_VV), cur(OFF_VV),
                  pl.BlockSpec((None, rows, ATW), lambda b, n: (b, at(n), GMW // ATW)),
                  pl.BlockSpec((1, ATT_HEADS), lambda b, n: (0, 0)), pl.BlockSpec((1, ATW), lambda b, n: (0, 0))],
        out_specs=[pl.BlockSpec((None, rows, ATW), lambda b, n: (b, at(n), 0)), kv_out, kv_out,
                   pl.BlockSpec((1, ATT_HEADS), lambda b, n: (0, 0)), pl.BlockSpec((1, ATW), lambda b, n: (0, 0))],
        out_shape=[jax.ShapeDtypeStruct((bsz, s, ATW), BF16), jax.ShapeDtypeStruct((bsz, s, KVW), BF16),
                   jax.ShapeDtypeStruct((bsz, s, KVW), BF16), jax.ShapeDtypeStruct((1, ATT_HEADS), F32),
                   jax.ShapeDtypeStruct((1, ATW), F32)],
        scratch_shapes=[pltpu.VMEM((rows, KVW), F32), pltpu.VMEM((rows, KVW), F32)],
        compiler_params=_cp(("arbitrary", "arbitrary")),
    )(p, p, p, p, p, dmix, sinks, out_g)


CONV_CT = 256


def _shift_down(x, j):
    if j == 0:
        return x
    rows = lax.broadcasted_iota(jnp.int32, x.shape, 0)
    return jnp.where(rows >= j, pltpu.roll(x, j, 0), 0.0)


def _shift_up(x, j):
    if j == 0:
        return x
    s = x.shape[0]
    rows = lax.broadcasted_iota(jnp.int32, x.shape, 0)
    return jnp.where(rows < s - j, pltpu.roll(x, s - j, 0), 0.0)


def conv_fwd(p, w, bias, name):
    bsz, s, _ = p.shape

    def body(x_ref, w_ref, b_ref, o_ref):
        xv, wv = x_ref[...], w_ref[...]
        pre = b_ref[...] + sum(wv[k:k + 1, :] * _shift_down(xv, CONV_K - 1 - k) for k in range(CONV_K))
        o_ref[...] = _silu(pre)

    blk = pl.BlockSpec((None, s, CONV_CT), lambda b, j: (b, 0, j))
    src = pl.BlockSpec((None, s, CONV_CT), lambda b, j: (b, 0, OFF_XBC // CONV_CT + j))
    return pl.pallas_call(
        body, name=name, grid=(bsz, CCH // CONV_CT),
        in_specs=[src, pl.BlockSpec((CONV_K, CONV_CT), lambda b, j: (0, j)), pl.BlockSpec((1, CONV_CT), lambda b, j: (0, j))],
        out_specs=blk, out_shape=jax.ShapeDtypeStruct((bsz, s, CCH), F32),
        compiler_params=_cp(("parallel", "parallel")),
    )(p, w, bias)


def conv_bwd(p, dxc, w, bias, name):
    bsz, s, _ = p.shape

    def body(x_ref, d_ref, w_ref, b_ref, dx_ref, dw_ref, db_ref):
        b = pl.program_id(1)
        xv, wv = x_ref[...], w_ref[...]
        xs = [_shift_down(xv, CONV_K - 1 - k) for k in range(CONV_K)]
        pre = b_ref[...] + sum(wv[k:k + 1, :] * xs[k] for k in range(CONV_K))
        sg = 1.0 / (1.0 + jnp.exp(-pre))
        dpre = d_ref[...] * (sg * (1.0 + pre * (1.0 - sg)))
        dx_ref[...] = sum(wv[k:k + 1, :] * _shift_up(dpre, CONV_K - 1 - k) for k in range(CONV_K)).astype(dx_ref.dtype)
        p_w = jnp.concatenate([jnp.sum(dpre * xs[k], axis=0, keepdims=True) for k in range(CONV_K)], axis=0)
        p_b = jnp.sum(dpre, axis=0, keepdims=True)
        _accumulate(b == 0, (dw_ref, db_ref), (p_w, p_b))

    blk = pl.BlockSpec((None, s, CONV_CT), lambda j, b: (b, 0, j))
    src = pl.BlockSpec((None, s, CONV_CT), lambda j, b: (b, 0, OFF_XBC // CONV_CT + j))
    wsp = pl.BlockSpec((CONV_K, CONV_CT), lambda j, b: (0, j))
    bsp = pl.BlockSpec((1, CONV_CT), lambda j, b: (0, j))
    return pl.pallas_call(
        body, name=name, grid=(CCH // CONV_CT, bsz),
        in_specs=[src, blk, wsp, bsp], out_specs=[blk, wsp, bsp],
        out_shape=[jax.ShapeDtypeStruct((bsz, s, CCH), BF16), jax.ShapeDtypeStruct((CONV_K, CCH), F32),
                   jax.ShapeDtypeStruct((1, CCH), F32)],
        compiler_params=_cp(("parallel", "arbitrary")),
    )(p, dxc, w, bias)


def _ssd_consts():
    c = CHUNK
    r = lax.broadcasted_iota(jnp.int32, (c, c), 0)
    q = lax.broadcasted_iota(jnp.int32, (c, c), 1)
    hrow = lax.broadcasted_iota(jnp.int32, (LANES, SSW), 0)
    hcol = lax.broadcasted_iota(jnp.int32, (LANES, SSW), 1) // SSM_HD
    expand = (hrow == hcol).astype(F32)
    return expand, (r >= q).astype(F32), (r <= q).astype(F32), r >= q


def _ssd_chunk(xc, dtr, z, prev_t, dt_bias, a_log, d_skip, norm_g):
    c = xc.shape[0]
    expand, tril1, triu1, causal = _ssd_consts()
    xs, bm, cm = xc[:, :SSW], xc[:, SSW:SSW + BCW], xc[:, SSW + BCW:]
    dt = _softplus(dtr + dt_bias)
    da = dt * (-jnp.exp(a_log))
    a_cs = _hdot(tril1, da)
    a_cs_t = _hdot(da.T, triu1)
    dt_e = _hdot(dt, expand)
    acs_e = _hdot(a_cs, expand)
    alast_e = acs_e[c - 1:c, :]
    dsk_e = _hdot(jnp.broadcast_to(d_skip, (8, LANES)), expand)[0:1, :]
    xdt = xs * dt_e
    hg = SSM_HEADS // SSM_GROUPS
    ys, new_t = [], []
    for g in range(SSM_GROUPS):
        bg = bm[:, g * SSM_STATE:(g + 1) * SSM_STATE]
        cg = cm[:, g * SSM_STATE:(g + 1) * SSM_STATE]
        sl = slice(g * GRW, (g + 1) * GRW)
        cb = _nt(cg, bg)
        xdt_g = xdt[:, sl]
        st = _tn(bg, xdt_g * jnp.exp(alast_e[:, sl] - acs_e[:, sl]))
        new_t.append(prev_t[:, sl] * jnp.exp(alast_e[:, sl]) + st)
        y_off = _nn(cg, prev_t[:, sl]) * jnp.exp(acs_e[:, sl])
        yd = []
        low = lax.broadcasted_iota(jnp.int32, (c, LANES), 1) < SSM_HD
        for pair in range(hg // 2):
            xp = xdt_g[:, pair * LANES:(pair + 1) * LANES]
            acc = None
            for side, xh in enumerate((jnp.where(low, xp, 0.0), jnp.where(low, 0.0, xp))):
                h = g * hg + 2 * pair + side
                decay = jnp.exp(jnp.where(causal, a_cs[:, h:h + 1] - a_cs_t[h:h + 1, :], NEG))
                part = _nn(cb * decay, xh)
                acc = part if acc is None else acc + part
            yd.append(acc)
        ys.append(jnp.concatenate(yd, axis=1) + y_off)
    y = (jnp.concatenate(ys, axis=1) + xs * dsk_e) * _silu(z)
    yn = [y[:, g * GRW:(g + 1) * GRW] * lax.rsqrt(jnp.mean(jnp.square(y[:, g * GRW:(g + 1) * GRW]), axis=-1, keepdims=True) + EPS)
          for g in range(SSM_GROUPS)]
    return jnp.concatenate(yn, axis=1) * norm_g, jnp.concatenate(new_t, axis=1)


def ssd_fwd(xc, p, prm, name):
    bsz, s, _ = p.shape
    nc = s // CHUNK

    def body(xc_ref, dt_ref, *rest):
        z_refs, (db, al, dk, ng, o_ref, st_ref, state) = rest[:SSW // ZB], rest[SSW // ZB:]
        @pl.when(pl.program_id(0) == 0)
        def _():
            state[...] = jnp.zeros_like(state)

        for b in range(bsz):
            prev = state[b]
            st_ref[b, 0] = prev
            zb = jnp.concatenate([r[b] for r in z_refs], axis=1)
            out, new = _ssd_chunk(xc_ref[b], dt_ref[b], zb, prev, db[...], al[...], dk[...], ng[...])
            o_ref[b] = out.astype(o_ref.dtype)
            state[b] = new

    vec = pl.BlockSpec((1, LANES), lambda c: (0, 0))
    return pl.pallas_call(
        body, name=name, grid=(nc,),
        in_specs=[pl.BlockSpec((bsz, CHUNK, CCH), lambda c: (0, c, 0)),
                  pl.BlockSpec((bsz, CHUNK, LANES), lambda c: (0, c, OFF_DT // LANES)),
                  *[pl.BlockSpec((bsz, CHUNK, ZB), lambda c, i=i: (0, c, OFF_Z // ZB + i)) for i in range(SSW // ZB)],
                  vec, vec, vec, pl.BlockSpec((1, SSW), lambda c: (0, 0))],
        out_specs=[pl.BlockSpec((bsz, CHUNK, SSW), lambda c: (0, c, 0)),
                   pl.BlockSpec((bsz, 1, SSM_STATE, SSW), lambda c: (0, c, 0, 0))],
        out_shape=[jax.ShapeDtypeStruct((bsz, s, SSW), BF16), jax.ShapeDtypeStruct((bsz, nc, SSM_STATE, SSW), F32)],
        scratch_shapes=[pltpu.VMEM((bsz, SSM_STATE, SSW), F32)],
        compiler_params=_cp(("arbitrary",)),
    )(xc, p, *([p] * (SSW // ZB)), *prm)


def ssd_bwd(xc, p, states, dmix, prm, name):
    bsz, s, _ = p.shape
    nc = s // CHUNK

    def body(xc_ref, dt_ref, *rest):
        z_refs, (st_ref, do_ref, db, al, dk, ng, dxc_ref, ddt_ref, dz_ref) = rest[:SSW // ZB], rest[SSW // ZB:SSW // ZB + 9]
        rest = rest[SSW // ZB + 9:]
        dpar, dstate = rest[:4], rest[4]
        c = pl.program_id(0)

        @pl.when(c == 0)
        def _():
            dstate[...] = jnp.zeros_like(dstate)

        dpars = None
        for b in range(bsz):
            zb = jnp.concatenate([r[b] for r in z_refs], axis=1)
            _, vjp = jax.vjp(_ssd_chunk, xc_ref[b], dt_ref[b], zb, st_ref[b, 0], db[...], al[...], dk[...], ng[...])
            gr = vjp((do_ref[b], dstate[b]))
            dxc_ref[b] = gr[0]
            ddt_ref[b] = gr[1].astype(ddt_ref.dtype)
            dz_ref[b] = gr[2].astype(dz_ref.dtype)
            dstate[b] = gr[3]
            dpars = gr[4:] if dpars is None else [u + v for u, v in zip(dpars, gr[4:])]
        _accumulate(c == 0, dpar, dpars)

    rv = lambda c: nc - 1 - c
    vec = pl.BlockSpec((1, LANES), lambda c: (0, 0))
    ngs = pl.BlockSpec((1, SSW), lambda c: (0, 0))
    return pl.pallas_call(
        body, name=name, grid=(nc,),
        in_specs=[pl.BlockSpec((bsz, CHUNK, CCH), lambda c: (0, rv(c), 0)),
                  pl.BlockSpec((bsz, CHUNK, LANES), lambda c: (0, rv(c), OFF_DT // LANES)),
                  *[pl.BlockSpec((bsz, CHUNK, ZB), lambda c, i=i: (0, rv(c), OFF_Z // ZB + i)) for i in range(SSW // ZB)],
                  pl.BlockSpec((bsz, 1, SSM_STATE, SSW), lambda c: (0, rv(c), 0, 0)),
                  pl.BlockSpec((bsz, CHUNK, SSW), lambda c: (0, rv(c), (GMW + ATW) // SSW)),
                  vec, vec, vec, ngs],
        out_specs=[pl.BlockSpec((bsz, CHUNK, CCH), lambda c: (0, rv(c), 0)),
                   pl.BlockSpec((bsz, CHUNK, LANES), lambda c: (0, rv(c), 0)),
                   pl.BlockSpec((bsz, CHUNK, SSW), lambda c: (0, rv(c), 0)),
                   vec, vec, vec, ngs],
        out_shape=[jax.ShapeDtypeStruct((bsz, s, CCH), F32), jax.ShapeDtypeStruct((bsz, s, LANES), BF16),
                   jax.ShapeDtypeStruct((bsz, s, SSW), BF16)] + [jax.ShapeDtypeStruct((1, LANES), F32)] * 3
                  + [jax.ShapeDtypeStruct((1, SSW), F32)],
        scratch_shapes=[pltpu.VMEM((bsz, SSM_STATE, SSW), F32)],
        compiler_params=_cp(("arbitrary",)),
    )(xc, p, *([p] * (SSW // ZB)), states, dmix, *prm)


def _rows2d(a):
    return a.reshape(-1, a.shape[-1])


def _ew_tile(r, c):
    t = r
    while t * c > (1 << 20) and t % 16 == 0:
        t //= 2
    return t


def add_pair(g, theirs, chip, core, name):
    k, r, c = g.shape
    h = r // 2
    tr = _ew_tile(h, c)
    nb = h // tr

    def body(s_ref, a_ref, b_ref, own_ref, ob_ref):
        s = a_ref[...] + b_ref[...]
        ob_ref[...] = s.astype(ob_ref.dtype)

        @pl.when(pl.program_id(1) == s_ref[1])
        def _():
            own_ref[...] = s

    blk = pl.BlockSpec((None, tr, c), lambda i, kk, sr: (kk, i, 0))
    return pl.pallas_call(
        body, name=name,
        grid_spec=pltpu.PrefetchScalarGridSpec(
            num_scalar_prefetch=1, grid=(nb, k),
            in_specs=[pl.BlockSpec((None, tr, c), lambda i, kk, sr: (kk, sr[0] * nb + i, 0)), blk],
            out_specs=[pl.BlockSpec((tr, c), lambda i, kk, sr: (i, 0)), blk]),
        out_shape=[jax.ShapeDtypeStruct((h, c), F32), jax.ShapeDtypeStruct(theirs.shape, BF16)],
        compiler_params=_cp(("parallel", "arbitrary")),
    )(jnp.stack([core, chip]).astype(jnp.int32), g, theirs)


def sum_own_recv(own, recv, core, name):
    h, c = own.shape
    tr = _ew_tile(h, c)
    nb = h // tr

    def body(k_ref, o_ref, r_ref, out_ref):
        s = o_ref[...]
        for j in range(3):
            s = s + r_ref[j].astype(F32)
        out_ref[...] = s

    return pl.pallas_call(
        body, name=name,
        grid_spec=pltpu.PrefetchScalarGridSpec(
            num_scalar_prefetch=1, grid=(nb,),
            in_specs=[pl.BlockSpec((tr, c), lambda i, kr: (i, 0)),
                      pl.BlockSpec((3, tr, c), lambda i, kr: (0, i, 0))],
            out_specs=pl.BlockSpec((tr, c), lambda i, kr: (kr[0] * nb + i, 0))),
        out_shape=jax.ShapeDtypeStruct((2 * h, c), F32),
        compiler_params=_cp(("parallel",)),
    )(core.reshape(1).astype(jnp.int32), own, recv)


def _adam_math(w, m, v, g):
    mn = ADAM_B1 * m + (1.0 - ADAM_B1) * g
    vn = ADAM_B2 * v + (1.0 - ADAM_B2) * (g * g)
    mh = mn / (1.0 - ADAM_B1 ** ADAM_STEP)
    vh = vn / (1.0 - ADAM_B2 ** ADAM_STEP)
    return -ADAM_LR * (mh / (jnp.sqrt(vh) + ADAM_EPS) + ADAM_WD * w), mn, vn


def adamw_minor_rows(w, m, v, g, name):
    r, nl, c = w.shape
    tr = max(t for t in range(1, r + 1) if r % t == 0 and t * nl * c <= (1 << 19))

    def body(w_ref, m_ref, v_ref, g_ref, d_ref, mo_ref, vo_ref):
        d_ref[...], mo_ref[...], vo_ref[...] = _adam_math(w_ref[...], m_ref[...], v_ref[...], g_ref[...])

    blk = pl.BlockSpec((tr, nl, c), lambda i: (i, 0, 0))
    return pl.pallas_call(
        body, name=name, grid=(r // tr,), in_specs=[blk] * 4, out_specs=[blk] * 3,
        out_shape=[jax.ShapeDtypeStruct(w.shape, F32)] * 3, compiler_params=_cp(("parallel",)),
    )(w, m, v, g)


def adamw_layer(w, m, v, g, layer, prev, name):
    nl, r, c = w.shape
    tr = _ew_tile(r, c * 2)

    def body(w_ref, m_ref, v_ref, g_ref, *rest):
        go_ref, d_ref, mo_ref, vo_ref = rest[-4:]
        gv = g_ref[...]
        dl, mn, vn = _adam_math(w_ref[...], m_ref[...], v_ref[...], gv)
        go_ref[...] = gv
        d_ref[...] = dl
        mo_ref[...] = mn
        vo_ref[...] = vn

    lay = pl.BlockSpec((None, tr, c), lambda i: (layer, i, 0))
    n_prev = 0 if prev is None else 4
    return pl.pallas_call(
        body, name=name, grid=(r // tr,),
        in_specs=[lay, lay, lay, pl.BlockSpec((tr, c), lambda i: (i, 0))] + [_ANY] * n_prev,
        out_specs=[lay] * 4, out_shape=[jax.ShapeDtypeStruct(w.shape, F32)] * 4,
        input_output_aliases={4 + i: i for i in range(n_prev)},
        compiler_params=_cp(("parallel",)),
    )(w, m, v, g, *(prev or ()))


def sum_devices(parts, name):
    n, r, c = parts.shape
    tr = _ew_tile(r, c * n)

    def body(p_ref, o_ref):
        s = p_ref[0]
        for j in range(1, n):
            s = s + p_ref[j]
        o_ref[...] = s

    return pl.pallas_call(
        body, name=name, grid=(r // tr,),
        in_specs=[pl.BlockSpec((n, tr, c), lambda i: (0, i, 0))],
        out_specs=pl.BlockSpec((tr, c), lambda i: (i, 0)),
        out_shape=jax.ShapeDtypeStruct((r, c), F32),
        compiler_params=_cp(("parallel",)),
    )(parts)


def adamw(w, m, v, g, name):
    r, c = w.shape
    tr = _ew_tile(r, c * 2)

    def body(w_ref, m_ref, v_ref, g_ref, d_ref, mo_ref, vo_ref):
        gv = g_ref[...]
        mn = ADAM_B1 * m_ref[...] + (1.0 - ADAM_B1) * gv
        vn = ADAM_B2 * v_ref[...] + (1.0 - ADAM_B2) * (gv * gv)
        mh = mn / (1.0 - ADAM_B1 ** ADAM_STEP)
        vh = vn / (1.0 - ADAM_B2 ** ADAM_STEP)
        d_ref[...] = -ADAM_LR * (mh / (jnp.sqrt(vh) + ADAM_EPS) + ADAM_WD * w_ref[...])
        mo_ref[...] = mn
        vo_ref[...] = vn

    blk = pl.BlockSpec((tr, c), lambda i: (i, 0))
    return pl.pallas_call(
        body, name=name, grid=(r // tr,), in_specs=[blk] * 4, out_specs=[blk] * 3,
        out_shape=[jax.ShapeDtypeStruct((r, c), F32)] * 3,
        compiler_params=_cp(("parallel",)),
    )(w, m, v, g)


def _place():
    x, y, c = lax.axis_index("x"), lax.axis_index("y"), lax.axis_index("c")
    chips = [(1 - x, y), (x, 1 - y), (1 - x, 1 - y)]
    return x, y, c, chips


def all_gather_small(v, name):
    r, w = v.shape

    def body(x_ref, out_ref, send_sems, recv_sems, local_sem):
        x, y, c, chips = _place()
        me, sibling = (x, y, c), (x, y, 1 - c)

        def rows(px, py, pc):
            return out_ref.at[pl.ds((4 * px + 2 * py + pc) * r, r), :]

        def copy(k, block, to, src=None):
            return pltpu.make_async_remote_copy(
                src_ref=rows(*block) if src is None else src, dst_ref=rows(*block),
                send_sem=send_sems.at[k], recv_sem=recv_sems.at[k], device_id=to, device_id_type=MESH)

        mine = pltpu.make_async_copy(x_ref, rows(*me), local_sem)
        mine.start()
        first = [copy(0, me, sibling, src=x_ref)]
        first += [copy(1 + j, me, (*chip, c), src=x_ref) for j, chip in enumerate(chips)]
        for cp in first:
            cp.start()
        passed = [copy(4 + j, (*chip, c), sibling) for j, chip in enumerate(chips)]
        for j, chip in enumerate(chips):
            copy(1 + j, (*chip, c), me).wait_recv()
            passed[j].start()
        copy(0, sibling, me).wait_recv()
        for j, chip in enumerate(chips):
            copy(4 + j, (*chip, 1 - c), me).wait_recv()
        for cp in first + passed:
            cp.wait_send()
        mine.wait()

    out = pl.pallas_call(
        body, name=name, out_shape=jax.ShapeDtypeStruct((8 * r, w), v.dtype),
        in_specs=[pl.BlockSpec(memory_space=pltpu.VMEM)], out_specs=pl.BlockSpec(memory_space=pltpu.VMEM),
        scratch_shapes=[pltpu.SemaphoreType.DMA((7,)), pltpu.SemaphoreType.DMA((7,)), pltpu.SemaphoreType.DMA],
        compiler_params=pltpu.CompilerParams(vmem_limit_bytes=VMEM_LIMIT),
    )(v)
    return out.reshape(8, r, w)


_HBM = pl.BlockSpec(memory_space=pltpu.HBM)


_SEM = pl.BlockSpec(memory_space=pltpu.SEMAPHORE)
_ANY = pl.BlockSpec(memory_space=pl.ANY)
_EFFECT = pltpu.SideEffectType.DATAFLOW_SIDE_EFFECTING


def _hbm(a):
    return pltpu.with_memory_space_constraint(a, pltpu.HBM)


def split_copy_start(srcs, land_shapes, copies, after, name):
    n, nl = len(srcs), len(land_shapes)
    n_after = 0 if after is None else 1
    ncopy = [0]

    def body(*refs):
        ins, lands = refs[:n], refs[n:n + nl]
        send_sems, recv_sems = refs[n + nl + n_after], refs[n + nl + n_after + 1]
        token = refs[-1]
        x, y, c, chips = _place()
        for k, (src, dst, to) in enumerate(copies(x, y, c, chips, ins, lands)):
            pltpu.make_async_remote_copy(src_ref=src, dst_ref=dst, send_sem=send_sems.at[k], recv_sem=recv_sems.at[k],
                                         device_id=to, device_id_type=MESH).start()
        token[...] = jnp.zeros_like(token)

    ncopy[0] = len(copies(0, 0, 0, [(1, 0), (0, 1), (1, 1)], [None] * n, [None] * nl, count_only=True))
    k = ncopy[0]
    lands = [_hbm(lax.empty(s.shape, s.dtype)) for s in land_shapes]
    res = pl.pallas_call(
        body, name=name,
        out_shape=(pltpu.SemaphoreType.DMA((k,)), pltpu.SemaphoreType.DMA((k,)))
        + tuple(pltpu.HBM(s.shape, s.dtype) for s in srcs) + tuple(pltpu.HBM(s.shape, s.dtype) for s in land_shapes)
        + (jax.ShapeDtypeStruct((8, LANES), F32),),
        in_specs=[_HBM] * (n + nl) + [_ANY] * n_after,
        out_specs=(_SEM, _SEM) + (_HBM,) * (n + nl) + (pl.BlockSpec(memory_space=pltpu.VMEM),),
        input_output_aliases={i: 2 + i for i in range(n + nl)},
        compiler_params=pltpu.CompilerParams(has_side_effects=_EFFECT),
    )(*[_hbm(s) for s in srcs], *lands, *([after] if n_after else []))
    return res[0], res[1], list(res[2:2 + n]), list(res[2 + n:2 + n + nl]), res[-1]


def split_copy_wait(send_sems, recv_sems, srcs, lands, copies, after, name):
    n, nl = len(srcs), len(lands)

    def body(*refs):
        ins, lnd = refs[:n], refs[n:n + nl]
        ss, rs = refs[n + nl], refs[n + nl + 1]
        x, y, c, chips = _place()
        for k, (src, dst, to) in enumerate(copies(x, y, c, chips, ins, lnd, receive=True)):
            cp = pltpu.make_async_remote_copy(src_ref=src, dst_ref=dst, send_sem=ss.at[k], recv_sem=rs.at[k],
                                              device_id=to, device_id_type=MESH)
            cp.wait_send()
            cp.wait_recv()

    res = pl.pallas_call(
        body, name=name,
        out_shape=tuple(pltpu.HBM(s.shape, s.dtype) for s in srcs) + tuple(pltpu.HBM(s.shape, s.dtype) for s in lands),
        in_specs=[_HBM] * (n + nl) + [_SEM, _SEM, _ANY], out_specs=(_HBM,) * (n + nl),
        input_output_aliases={i: i for i in range(n + nl)},
        compiler_params=pltpu.CompilerParams(has_side_effects=_EFFECT),
    )(*srcs, *lands, send_sems, recv_sems, after)
    return list(res[:n]), list(res[n:])


def _gather_copies(x, y, c, chips, ins, lands, receive=False, count_only=False):
    out = []
    for i in range(len(ins)):
        for cx, cy in chips:
            if count_only:
                out.append(None)
                continue
            h = ins[i].shape[0] // 2
            rows = pl.ds(c * h, h)
            k_dst = (2 * cx + cy) if receive else (2 * x + y)
            out.append((ins[i].at[rows, :], lands[i].at[k_dst, rows, :], (cx, cy, c)))
    for i in range(len(ins)):
        out.append(None if count_only else (ins[i], lands[i].at[2 * x + y], (x, y, 1 - c)))
    return out


def _swap_copies(x, y, c, chips, ins, lands, receive=False, count_only=False):
    out = []
    for i in range(len(ins)):
        if count_only:
            out.append(None)
            continue
        h = ins[i].shape[1] // 2
        out.append((ins[i].at[:, pl.ds((1 - c) * h, h), :], lands[i], (x, y, 1 - c)))
    return out


def _scatter_copies(x, y, c, chips, ins, lands, receive=False, count_only=False):
    out = []
    for i in range(len(ins)):
        for j, (cx, cy) in enumerate(chips):
            if count_only:
                out.append(None)
                continue
            out.append((ins[i].at[2 * cx + cy], lands[i].at[j], (cx, cy, c)))
    return out


def forward_halves(lands, name):
    n = len(lands)

    def body(*refs):
        ins, outs = refs[:n], refs[n:2 * n]
        send_sems, recv_sems = refs[2 * n:]
        x, y, c, chips = _place()
        sibling = (x, y, 1 - c)
        sent = []
        for i in range(n):
            h = ins[i].shape[1] // 2
            for j, (cx, cy) in enumerate(chips):
                blk = ins[i].at[2 * cx + cy, pl.ds(c * h, h), :]
                sent.append(pltpu.make_async_remote_copy(
                    src_ref=blk, dst_ref=outs[i].at[2 * cx + cy, pl.ds(c * h, h), :], send_sem=send_sems.at[3 * i + j],
                    recv_sem=recv_sems.at[3 * i + j], device_id=sibling, device_id_type=MESH))
                sent[-1].start()
        for i in range(n):
            h = ins[i].shape[1] // 2
            for j, (cx, cy) in enumerate(chips):
                theirs = outs[i].at[2 * cx + cy, pl.ds((1 - c) * h, h), :]
                pltpu.make_async_remote_copy(
                    src_ref=theirs, dst_ref=theirs, send_sem=send_sems.at[3 * i + j], recv_sem=recv_sems.at[3 * i + j],
                    device_id=sibling, device_id_type=MESH).wait_recv()
        for cp in sent:
            cp.wait_send()

    return pl.pallas_call(
        body, name=name, out_shape=[jax.ShapeDtypeStruct(s.shape, s.dtype) for s in lands],
        in_specs=[_HBM] * n, out_specs=[_HBM] * n, input_output_aliases={i: i for i in range(n)},
        scratch_shapes=[pltpu.SemaphoreType.DMA((3 * n,)), pltpu.SemaphoreType.DMA((3 * n,))],
    )(*lands)


def join_halves(halves, name):
    n = len(halves)

    def body(*refs):
        ins, outs = refs[:n], refs[n:2 * n]
        send_sems, recv_sems = refs[2 * n:]
        x, y, c, _ = _place()
        sibling = (x, y, 1 - c)
        sent = []
        for i in range(n):
            h = ins[i].shape[0] // 2
            sent.append(pltpu.make_async_remote_copy(
                src_ref=ins[i].at[pl.ds(c * h, h), :], dst_ref=outs[i].at[pl.ds(c * h, h), :], send_sem=send_sems.at[i],
                recv_sem=recv_sems.at[i], device_id=sibling, device_id_type=MESH))
            sent[-1].start()
        for i in range(n):
            h = ins[i].shape[0] // 2
            theirs = outs[i].at[pl.ds((1 - c) * h, h), :]
            pltpu.make_async_remote_copy(
                src_ref=theirs, dst_ref=theirs, send_sem=send_sems.at[i],
                recv_sem=recv_sems.at[i], device_id=sibling, device_id_type=MESH).wait_recv()
        for cp in sent:
            cp.wait_send()

    return pl.pallas_call(
        body, name=name, out_shape=[jax.ShapeDtypeStruct(s.shape, F32) for s in halves],
        in_specs=[_HBM] * n, out_specs=[_HBM] * n, input_output_aliases={i: i for i in range(n)},
        scratch_shapes=[pltpu.SemaphoreType.DMA((n,)), pltpu.SemaphoreType.DMA((n,))],
    )(*halves)


_PACK_ROWS = 8 * LANES


def _pack(arrs):
    flat = jnp.concatenate([a.reshape(-1).astype(F32) for a in arrs])
    pad = (-flat.shape[0]) % _PACK_ROWS
    return jnp.pad(flat, (0, pad)).reshape(-1, LANES)


def _unpack(flat, shapes):
    flat = flat.reshape(-1)
    out, off = [], 0
    for s in shapes:
        n = int(np.prod(s))
        out.append(flat[off:off + n].reshape(s))
        off += n
    return out


def _win_from_blocks(g):
    d = g.shape[1]
    return jnp.pad(g.transpose(1, 0, 2).reshape(d, IN_W), ((0, 0), (0, PW - IN_W)))


def _win_to_blocks(w):
    return w[:, :IN_W].reshape(w.shape[0], 4, IN_W // 4).transpose(1, 0, 2)


def _relu2(a):
    r = jnp.maximum(a, 0)
    return r * r


def kernel(x, c, ada_w, ada_b, norm1_g, w_in, gm_ln_g, gm_ln_b, gm_ws, gm_bs, gm_norm_g, attn_sinks, attn_norm_g, conv_w, conv_b, dt_bias, a_log, d_skip, ssm_norm_g, w_out, norm2_g, w_mlp1, w_mlp2, final_norm_g, loss_target, m_ada_w, m_ada_b, m_norm1_g, m_w_in, m_gm_ln_g, m_gm_ln_b, m_gm_ws, m_gm_bs, m_gm_norm_g, m_attn_sinks, m_attn_norm_g, m_conv_w, m_conv_b, m_dt_bias, m_a_log, m_d_skip, m_ssm_norm_g, m_w_out, m_norm2_g, m_w_mlp1, m_w_mlp2, m_final_norm_g, v_ada_w, v_ada_b, v_norm1_g, v_w_in, v_gm_ln_g, v_gm_ln_b, v_gm_ws, v_gm_bs, v_gm_norm_g, v_attn_sinks, v_attn_norm_g, v_conv_w, v_conv_b, v_dt_bias, v_a_log, v_d_skip, v_ssm_norm_g, v_w_out, v_norm2_g, v_w_mlp1, v_w_mlp2, v_final_norm_g):
    nl = ada_w.shape[0]
    bl, s, d = x.shape
    t = bl * s
    dff4 = w_mlp1.shape[2]
    dff = 4 * dff4
    mod_w = ada_w.shape[2]
    cw_w = conv_w.shape[2]
    xi, yi, ci = lax.axis_index("x"), lax.axis_index("y"), lax.axis_index("c")
    chip = 2 * xi + yi
    dev = 2 * chip + ci
    nex = 8 * bl

    shards = [[w_in[l].astype(BF16), w_out[l].astype(BF16), w_mlp1[l].astype(BF16), w_mlp2[l].astype(BF16)]
              for l in range(nl)]
    groups = [[shards[0][i]] for i in range(4)] + [shards[l] for l in range(1, nl)]

    def start_gather(gi, behind):
        ss, rs, srcs, lands, token = split_copy_start(
            groups[gi], [jax.ShapeDtypeStruct((4,) + a.shape, a.dtype) for a in groups[gi]], _gather_copies, behind,
            f"gather_start_{gi}")
        return (ss, rs, srcs, lands), token

    first_gather, first_token = start_gather(0, None)
    g0 = all_gather_small(_pack([c, conv_w]) + first_token[0, 0], "ag_c")
    g0 = g0.reshape(8, -1)
    c_all = g0[:, :bl * d].reshape(nex, d)
    cw_parts = g0[0::2, bl * d:bl * d + conv_w.size].reshape(4, nl, CONV_K, cw_w)
    conv_w_full = cw_parts.transpose(1, 2, 0, 3).reshape(nl, CONV_K, CCH)

    def c_act(a):
        return _silu(a).astype(BF16)

    def to_bf16(a):
        return a.astype(BF16)

    mod_parts = []
    for l in range(nl):
        bias = lax.dynamic_slice(ada_b[l].reshape(1, -1), (0, chip * mod_w), (1, mod_w))
        mod_parts.append(_mm("nn", c_all, ada_w, dims=(nex, mod_w, d), tm=nex, tn=512, tk=d, out_dtypes=[F32],
                             name=f"mod_{l}", pro_a=c_act, pro_b=to_bf16,
                             b_spec=pl.BlockSpec((None, d, 512), lambda i, j, kk, l=l: (l, kk, j)),
                             extras=[(bias, pl.BlockSpec((1, 512), lambda i, j, kk: (0, j)))],
                             epi=lambda acc, bv: (acc + bv,))[0])
    g1 = all_gather_small(_pack(mod_parts), "ag_mod").reshape(8, -1)
    mod_all = g1[0::2, :nl * nex * mod_w].reshape(4, nl, nex, mod_w).transpose(1, 2, 0, 3).reshape(nl, nex, 4 * mod_w)
    mod = lax.dynamic_slice(mod_all, (0, dev * bl, 0), (nl, bl, 4 * mod_w))
    mods = [[mod[l, :, i * d:(i + 1) * d].reshape(bl, 1, d) for i in range(6)] for l in range(nl)]

    pending, after = [first_gather], g1
    for gi in range(1, len(groups)):
        state, after = start_gather(gi, after)
        pending.append(state)
    mods[0][0] = mods[0][0] + after[0, 0]

    def fetch(gi, behind):
        ss, rs, srcs, lands = pending[gi]
        srcs, lands = split_copy_wait(ss, rs, srcs, lands, _gather_copies, behind, f"gather_wait_{gi}")
        return forward_halves(lands, f"gather_pass_{gi}")

    as_win = _win_from_blocks

    wfull = [None] * nl
    row = lambda a: a.reshape(1, -1)
    pad16 = lambda a: jnp.pad(a.reshape(1, -1), ((0, 0), (0, LANES - SSM_HEADS)))
    tm_res = min(1024, s)

    def residual(acc, xt, gt):
        return acc, xt + gt * acc

    def res_extras(xin, gate, tm=tm_res, tn=512):
        return [(xin.reshape(t, d), pl.BlockSpec((tm, tn), lambda i, j, kk: (i, j))),
                (gate, pl.BlockSpec((None, 1, tn), lambda i, j, kk: (i * tm // s, 0, j)))]

    w1_blk = lambda tk, tn: pl.BlockSpec((None, tk, tn), lambda i, j, kk: (j // (dff4 // tn), kk, j % (dff4 // tn)))

    saved = []
    xcur = x
    for l in range(nl):
        sh1, sc1, gt1, sh2, sc2, gt2 = mods[l]
        if l == 0:
            win = as_win(fetch(0, mod)[0])
        else:
            g_in, g_out, w1, g_2 = fetch(3 + l, xcur)
            win, wout, w2 = as_win(g_in), g_out.reshape(-1, d), g_2.reshape(dff, d)
        prm_a = (row(gm_ln_g[l]), row(gm_ln_b[l]), gm_ws[l], gm_bs[l].T, row(gm_norm_g[l]))
        prm_b = (row(attn_sinks[l]), row(attn_norm_g[l]))
        prm_c = (pad16(dt_bias[l]), pad16(a_log[l]), pad16(d_skip[l]), row(ssm_norm_g[l]))
        h1 = ln_mod_fwd(xcur, row(norm1_g[l]), sc1, sh1, f"ln1_fwd_{l}")
        p = _mm("nn", h1.reshape(t, d), win, dims=(t, PW, d), tm=1024, tn=PW // 3, tk=d, out_dtypes=[F32],
                name=f"proj_in_{l}")[0].reshape(bl, s, PW)
        out_a = gmlp_fwd(p, prm_a, f"gmlp_fwd_{l}")
        out_b = attn_fwd(p, *prm_b, f"attn_fwd_{l}")
        xc = conv_fwd(p, conv_w_full[l], row(conv_b[l]), f"conv_fwd_{l}")
        out_c, states = ssd_fwd(xc, p, prm_c, f"ssd_fwd_{l}")
        mix = jnp.concatenate([out_a, out_b, out_c], axis=-1)
        if l == 0:
            wout = fetch(1, mix)[0].reshape(-1, d)
        mm1, x2 = _mm("nn", mix.reshape(t, d), wout, dims=(t, d, d), tm=tm_res, tn=1024, tk=d, out_dtypes=[F32, F32],
                      name=f"proj_out_{l}", extras=res_extras(xcur, gt1, tm_res, 1024), epi=residual)
        x2 = x2.reshape(bl, s, d)
        h2 = ln_mod_fwd(x2, row(norm2_g[l]), sc2, sh2, f"ln2_fwd_{l}")
        if l == 0:
            w1 = fetch(2, h2)[0]
        a1, act = _mm("nn", h2.reshape(t, d), w1, dims=(t, dff, d), tm=1024, tn=2048, tk=d, out_dtypes=[BF16, BF16],
                      name=f"mlp1_{l}", b_spec=w1_blk(d, 2048), epi=lambda acc: (acc, _relu2(acc)))
        if l == 0:
            w2 = fetch(3, a1)[0].reshape(dff, d)
        tm2 = min(512, s)
        mm2, x3 = _mm("nn", act, w2, dims=(t, d, dff), tm=tm2, tn=512, tk=dff, out_dtypes=[F32, F32],
                      name=f"mlp2_{l}", extras=res_extras(x2, gt2, tm2), epi=residual)
        x3 = x3.reshape(bl, s, d)
        wfull[l] = (win, wout, w1, w2)
        saved.append((xcur, h1, p, xc, states, mix, mm1.reshape(bl, s, d), x2, h2, (a1, act), mm2.reshape(bl, s, d),
                      prm_a, prm_b, prm_c))
        xcur = x3

    dx, d_final_g, loss_part = loss_head(xcur, row(final_norm_g), loss_target, "loss_head")
    loss = lax.psum(loss_part[0, 0], ("x", "y", "c"))

    def rs_swap(grads, tag):
        ss, rs, srcs, lands, token = split_copy_start(
            grads, [jax.ShapeDtypeStruct((4, g.shape[1] // 2, g.shape[2]), F32) for g in grads],
            _swap_copies, None, f"rs_swap_{tag}")
        return (ss, rs, srcs, lands), token

    def rs_begin(swap_state, tag, swapped_behind, start_behind=None):
        ss, rs, srcs, lands = swap_state
        grads, theirs = split_copy_wait(ss, rs, srcs, lands, _swap_copies, swapped_behind, f"rs_swapped_{tag}")
        sums = [add_pair(g, th, chip, ci, f"rs_add_{tag}_{i}") for i, (g, th) in enumerate(zip(grads, theirs))]
        ss, rs, srcs, lands, token = split_copy_start(
            [sm[1] for sm in sums], [jax.ShapeDtypeStruct((3,) + sm[1].shape[1:], BF16) for sm in sums],
            _scatter_copies, sums[0][0] if start_behind is None else start_behind, f"rs_start_{tag}")
        return (ss, rs, srcs, lands, [sm[0] for sm in sums]), token

    def rs_end(state, behind, tag):
        ss, rs, srcs, lands, sums_f32 = state
        _, got = split_copy_wait(ss, rs, srcs, lands, _scatter_copies, behind, f"rs_wait_{tag}")
        halves = [sum_own_recv(sf, g, ci, f"rs_sum_{tag}_{i}") for i, (sf, g) in enumerate(zip(sums_f32, got))]
        return join_halves(halves, f"rs_join_{tag}")

    small_parts = [None] * nl
    dmods = [None] * nl
    reduced = [[None] * 4 for _ in range(nl)]
    pending_rs, rs_token = [], None
    part_slots = {"a": (0, 1), "m": (2, 3)}

    def finish(behind):
        for ll, part, state in pending_rs:
            for slot, blk in zip(part_slots[part], rs_end(state, behind, f"{ll}{part}")):
                reduced[ll][slot] = blk
        pending_rs.clear()

    for l in reversed(range(nl)):
        sh1, sc1, gt1, sh2, sc2, gt2 = mods[l]
        win, wout, w1, w2 = wfull[l]
        xin, h1, p, xc, states, mix, mm1, x2, h2, (a1, act), mm2, prm_a, prm_b, prm_c = saved[l]
        if rs_token is not None:
            gt2 = gt2 + rs_token[0, 0]
        dm2, dgt2 = gate_bwd(dx, mm2, gt2, f"gate2_bwd_{l}")
        dm2 = dm2.reshape(t, d)
        da1 = _mm("nt", dm2, w2, dims=(t, dff, d), tm=1024, tn=2048, tk=d, out_dtypes=[BF16], name=f"mlp2_dx_{l}",
                  extras=[(a1, pl.BlockSpec((1024 if t >= 1024 else t, 2048), lambda i, j, kk: (i, j)))],
                  epi=lambda acc, av: (acc * (2.0 * jnp.maximum(av, 0).astype(F32)),))[0]
        dw2 = _mm("tn", act, dm2, dims=(dff, d, t), tm=512, tn=d, tk=2048, out_dtypes=[F32], name=f"mlp2_dw_{l}",
                  out_shapes=[(4, dff4, d)],
                  out_specs=[pl.BlockSpec((None, 512, d), lambda i, j, kk: (i // (dff4 // 512), i % (dff4 // 512), 0))])[0]
        dw1 = _mm("tn", h2.reshape(t, d), da1, dims=(d, dff, t), tm=512, tn=dff4, tk=2048, out_dtypes=[F32],
                  name=f"mlp1_dw_{l}", out_shapes=[(4, d, dff4)],
                  out_specs=[pl.BlockSpec((None, 512, dff4), lambda i, j, kk: (j, i, 0))])[0]
        swap_state, swap_token = rs_swap([dw1, dw2], f"{l}m")
        dh2 = _mm_nt_blocked(da1, w1, tm=512, tn=512, name=f"mlp1_dx_{l}", behind=swap_token)
        mlp_state, mlp_token = rs_begin(swap_state, f"{l}m", dh2)
        sc2 = sc2 + mlp_token[0, 0]
        dx2, dsc2, dsh2, dn2 = ln_mod_bwd(dh2.reshape(bl, s, d), x2, dx, row(norm2_g[l]), sc2, f"ln2_bwd_{l}")
        dm1, dgt1 = gate_bwd(dx2, mm1, gt1, f"gate1_bwd_{l}")
        dm1 = dm1.reshape(t, d)
        dmix = _mm("nt", dm1, wout, dims=(t, d, d), tm=1024, tn=2048, tk=d, out_dtypes=[F32],
                   name=f"proj_out_dx_{l}")[0].reshape(bl, s, d)
        dwout = _mm("tn", mix.reshape(t, d), dm1, dims=(d, d, t), tm=512, tn=d, tk=2048, out_dtypes=[F32],
                    name=f"proj_out_dw_{l}", out_shapes=[(4, d // 4, d)],
                    out_specs=[pl.BlockSpec((None, 512, d), lambda i, j, kk: (i // (d // 4 // 512), i % (d // 4 // 512), 0))])[0]
        du, dv, dlg, dlb, dws, dbst, dgng = gmlp_bwd(p, dmix, prm_a, f"gmlp_bwd_{l}")
        dq, dk, dvv, dsinks, dang = attn_bwd(p, dmix, *prm_b, f"attn_bwd_{l}")
        dxc, ddt, dz, ddtb, dalog, ddsk, dsng = ssd_bwd(xc, p, states, dmix, prm_c, f"ssd_bwd_{l}")
        dxbc, dcw, dcb = conv_bwd(p, dxc, conv_w_full[l], row(conv_b[l]), f"conv_bwd_{l}")
        dp = jnp.concatenate([du, dv, dq, dk, dvv, dz, dxbc, ddt, jnp.zeros((bl, s, PW - OFF_DT - LANES), BF16)],
                             axis=-1).reshape(t, PW)
        dwin = _mm("tn", h1.reshape(t, d), dp, dims=(d, PW, t), tm=512, tn=PW // 2, tk=2048, out_dtypes=[F32],
                   name=f"proj_in_dw_{l}")[0]
        dwin_blocks = _win_to_blocks(dwin)
        mixer_swap, swap_token = rs_swap([dwin_blocks, dwout], f"{l}a")
        dh1 = _mm("nt", dp, win, dims=(t, d, PW), tm=1024, tn=512, tk=PW, out_dtypes=[F32],
                  name=f"proj_in_dx_{l}", behind=swap_token)[0]
        dx, dsc1, dsh1, dn1 = ln_mod_bwd(dh1.reshape(bl, s, d), xin, dx2, row(norm1_g[l]), sc1, f"ln1_bwd_{l}")
        dmods[l] = jnp.concatenate([dsh1, dsc1, dgt1, dsh2, dsc2, dgt2], axis=-1).reshape(bl, 6 * d)
        small_parts[l] = [dn1, dlg, dlb, dws, dbst.T, dgng, dsinks, dang, dcw, dcb, ddtb[:, :SSM_HEADS],
                          dalog[:, :SSM_HEADS], ddsk[:, :SSM_HEADS], dsng, dn2]
        finish(dx)
        pending_rs.append((l, "m", mlp_state))
        if l > 0:
            state, rs_token = rs_begin(mixer_swap, f"{l}a", dx)
            pending_rs.append((l, "a", state))
    grad_x = dx

    big = [(w_in, m_w_in, v_w_in), (w_out, m_w_out, v_w_out), (w_mlp1, m_w_mlp1, v_w_mlp1), (w_mlp2, m_w_mlp2, v_w_mlp2)]
    big_out = [None] * 4
    for l in reversed(range(1, nl)):
        for i, (wt, mt, vt) in enumerate(big):
            if i > 0:
                big_out[i] = adamw_layer(wt, mt, vt, reduced[l][i], l, big_out[i], f"adamw_big_{i}_{l}")

    small_names = [norm1_g, gm_ln_g, gm_ln_b, gm_ws, gm_bs, gm_norm_g, attn_sinks, attn_norm_g, None, conv_b, dt_bias,
                   a_log, d_skip, ssm_norm_g, norm2_g]
    n_small = len(small_names)
    per_param = [jnp.stack([small_parts[l][i].reshape(-1) for l in range(nl)]) for i in range(n_small)]
    small_vec = _pack(per_param + [d_final_g])
    rs_small = small_vec.shape[0]
    dmod_local = jnp.stack(dmods, axis=1)
    g2 = all_gather_small(jnp.concatenate([small_vec, _pack([dmod_local])], axis=0), "ag_small")
    state, rs_token = rs_begin(mixer_swap, "0a", grad_x, start_behind=g2)
    pending_rs.append((0, "a", state))
    g2 = g2 + rs_token[0, 0]
    g_small = sum_devices(g2[:, :rs_small, :], "sum_small")
    dmod_all = g2[:, rs_small:, :].reshape(8, -1)[:, :bl * nl * 6 * d].reshape(nex, nl * 6 * d)
    g_ada_b = sum_devices(dmod_all.reshape(nex, -1, LANES), "sum_ada_b").reshape(nl, 6 * d)
    shapes = [(nl, int(np.prod(small_parts[0][i].shape))) for i in range(n_small)] + [(d,)]
    g_list = _unpack(g_small, shapes)
    g_conv_w = lax.dynamic_slice(g_list[8].reshape(nl, CONV_K, CCH), (0, 0, chip * cw_w), (nl, CONV_K, cw_w))

    dm_cols = lax.dynamic_slice(dmod_all.reshape(nex, nl, 6 * d), (0, 0, chip * mod_w), (nex, nl, mod_w))
    g_ada_w = _mm("tn", c_all, dm_cols.reshape(nex, nl * mod_w), dims=(d, nl * mod_w, nex), tm=512, tn=512, tk=nex,
                  out_dtypes=[F32], name="ada_w_grad", pro_a=c_act, pro_b=to_bf16, out_shapes=[(nl, d, mod_w)],
                  out_specs=[pl.BlockSpec((None, 512, 512), lambda i, j, kk: (j // (mod_w // 512), i, j % (mod_w // 512)))])[0]
    d_ada_w, m_ada_w_n, v_ada_w_n = [a.reshape(ada_w.shape) for a in
                                     adamw(_rows2d(ada_w), _rows2d(m_ada_w), _rows2d(v_ada_w), _rows2d(g_ada_w), "adamw_ada_w")]

    smalls = {
        "ada_b": (ada_b, m_ada_b, v_ada_b, g_ada_b), "norm1_g": (norm1_g, m_norm1_g, v_norm1_g, g_list[0]),
        "gm_ln_g": (gm_ln_g, m_gm_ln_g, v_gm_ln_g, g_list[1]), "gm_ln_b": (gm_ln_b, m_gm_ln_b, v_gm_ln_b, g_list[2]),
        "gm_ws": (gm_ws, m_gm_ws, v_gm_ws, g_list[3]), "gm_bs": (gm_bs, m_gm_bs, v_gm_bs, g_list[4]),
        "gm_norm_g": (gm_norm_g, m_gm_norm_g, v_gm_norm_g, g_list[5]),
        "attn_sinks": (attn_sinks, m_attn_sinks, v_attn_sinks, g_list[6]),
        "attn_norm_g": (attn_norm_g, m_attn_norm_g, v_attn_norm_g, g_list[7]),
        "conv_w": (conv_w, m_conv_w, v_conv_w, g_conv_w), "conv_b": (conv_b, m_conv_b, v_conv_b, g_list[9]),
        "dt_bias": (dt_bias, m_dt_bias, v_dt_bias, g_list[10]), "a_log": (a_log, m_a_log, v_a_log, g_list[11]),
        "d_skip": (d_skip, m_d_skip, v_d_skip, g_list[12]),
        "ssm_norm_g": (ssm_norm_g, m_ssm_norm_g, v_ssm_norm_g, g_list[13]),
        "norm2_g": (norm2_g, m_norm2_g, v_norm2_g, g_list[14]),
        "final_norm_g": (final_norm_g, m_final_norm_g, v_final_norm_g, g_list[15]),
    }
    keys = list(smalls)
    wv, mv, vv_, gv = [_pack([smalls[k][i].reshape(smalls[k][0].shape) for k in keys]) for i in range(4)]
    sd_, sm_, sv_ = adamw(wv, mv, vv_, gv, "adamw_small")
    shp = [smalls[k][0].shape for k in keys]
    small_out = {k: (smalls[k][3].reshape(smalls[k][0].shape), a, b, cc)
                 for k, a, b, cc in zip(keys, _unpack(sd_, shp), _unpack(sm_, shp), _unpack(sv_, shp))}

    late = jnp.zeros((8, LANES), F32) + (sv_[0, 0] + v_ada_w_n[0, 0, 0])
    for bo in big_out:
        if bo is not None:
            late = late + bo[3][nl - 1, 0, 0]
    finish(late)
    for i, (wt, mt, vt) in enumerate(big):
        if i > 0:
            big_out[i] = adamw_layer(wt, mt, vt, reduced[0][i], 0, big_out[i], f"adamw_big_{i}_0")
    minor_first = lambda a: jnp.transpose(a, (2, 0, 1))
    g_in = jnp.stack([reduced[l][0].T for l in range(nl)], axis=1)
    back = lambda a: jnp.transpose(a, (1, 2, 0))
    big_out[0] = [back(a) for a in [g_in, *adamw_minor_rows(minor_first(w_in), minor_first(m_w_in),
                                                            minor_first(v_w_in), g_in, "adamw_w_in")]]

    out = {"ada_w": (g_ada_w, d_ada_w, m_ada_w_n, v_ada_w_n), "w_in": big_out[0], "w_out": big_out[1],
           "w_mlp1": big_out[2], "w_mlp2": big_out[3], **small_out}
    order = ["ada_w", "ada_b", "norm1_g", "w_in", "gm_ln_g", "gm_ln_b", "gm_ws", "gm_bs", "gm_norm_g", "attn_sinks",
             "attn_norm_g", "conv_w", "conv_b", "dt_bias", "a_log", "d_skip", "ssm_norm_g", "w_out", "norm2_g",
             "w_mlp1", "w_mlp2", "final_norm_g"]
    return (loss, grad_x, *[out[k][0] for k in order], *[out[k][1] for k in order],
            *[out[k][2] for k in order], *[out[k][3] for k in order])
```
